```python
import jax, jax.numpy as jnp
from jax import lax
import numpy as np

D_MODEL = 2048
BATCH = 8
SEQ = 8192
DEPTH = 1

CHUNK = 128
A_HEADS = 8
A_WIDTH = D_MODEL
A_HEAD_DIM = A_WIDTH // A_HEADS
B_GROUPS = 16
B_WIDTH = D_MODEL
CONV_WIDTH = 3
N_BRANCH = 2
DN_ALPHA = (2.0 * DEPTH) ** 0.25
DN_BETA = (8.0 * DEPTH) ** -0.25
LN_EPS = 1e-5
IN_WIDTHS = (A_WIDTH, A_WIDTH, A_WIDTH, B_WIDTH, B_WIDTH, B_WIDTH, B_WIDTH, D_MODEL, D_MODEL)
IN_COLS = 3 * A_WIDTH + 4 * B_WIDTH + N_BRANCH * D_MODEL

kernel_name = "hybrid_gated_sgu_shortconv_deepnorm"


def _split_points():
    pts, acc = [], 0
    for w in IN_WIDTHS[:-1]:
        acc += w
        pts.append(acc)
    return pts


def layer_norm(x, g, b):
    xf = x.astype(jnp.float32)
    mu = jnp.mean(xf, axis=-1, keepdims=True)
    var = jnp.mean(jnp.square(xf - mu), axis=-1, keepdims=True)
    y = (xf - mu) * lax.rsqrt(var + LN_EPS)
    return (y * g.astype(jnp.float32) + b.astype(jnp.float32)).astype(x.dtype)


def chunked_sgu(u, v, ln_g, ln_b, w_s, b_s):
    bsz, s, _ = u.shape
    nc = s // CHUNK
    u = jax.nn.gelu(u).reshape(bsz, nc, CHUNK, A_HEADS, A_HEAD_DIM)
    v = jax.nn.gelu(v).reshape(bsz, nc, CHUNK, A_HEADS, A_HEAD_DIM)
    v = layer_norm(v, ln_g.reshape(A_HEADS, A_HEAD_DIM), ln_b.reshape(A_HEADS, A_HEAD_DIM))
    causal = jnp.tril(jnp.ones((CHUNK, CHUNK), dtype=bool))
    w = jnp.where(causal[None], w_s, jnp.zeros_like(w_s))
    mixed = jnp.einsum('hts,bcshd->bcthd', w, v)
    mixed = mixed + jnp.transpose(b_s)[None, None, :, :, None]
    return (u * mixed).reshape(bsz, s, A_WIDTH)


def short_gated_conv(xb, cb, bb, conv_w, conv_b):
    s = xb.shape[1]
    h = cb * xb
    hp = jnp.pad(h, ((0, 0), (CONV_WIDTH - 1, 0), (0, 0)))
    conv = conv_b + conv_w[0] * hp[:, 0:s, :]
    for k in range(1, CONV_WIDTH):
        conv = conv + conv_w[k] * hp[:, k:k + s, :]
    return bb * conv


def _fwd_setup_inputs(seed: int = 0) -> dict:
    key = jax.random.key(seed)
    ks = jax.random.split(key, 16)
    nrm = jax.random.normal
    x = nrm(ks[0], (BATCH, SEQ, D_MODEL), jnp.float32)
    w_in = nrm(ks[1], (DEPTH, D_MODEL, IN_COLS), jnp.float32) * D_MODEL ** -0.5
    b_gate = 0.02 * nrm(ks[2], (DEPTH, N_BRANCH * D_MODEL), jnp.float32)
    ln_v_g = 1.0 + 0.02 * nrm(ks[3], (DEPTH, A_WIDTH), jnp.float32)
    ln_v_b = 0.02 * nrm(ks[4], (DEPTH, A_WIDTH), jnp.float32)
    w_s = nrm(ks[5], (DEPTH, A_HEADS, CHUNK, CHUNK), jnp.float32) * (0.5 * CHUNK ** -0.5)
    b_s = 1.0 + 0.02 * nrm(ks[6], (DEPTH, A_HEADS, CHUNK), jnp.float32)
    conv_w = nrm(ks[7], (DEPTH, CONV_WIDTH, B_WIDTH), jnp.float32) * CONV_WIDTH ** -0.5
    conv_b = 0.02 * nrm(ks[8], (DEPTH, B_WIDTH), jnp.float32)
    w_oa = nrm(ks[9], (DEPTH, A_WIDTH, D_MODEL), jnp.float32) * (A_WIDTH ** -0.5 * DN_BETA)
    w_ob = nrm(ks[10], (DEPTH, B_WIDTH, D_MODEL), jnp.float32) * (B_WIDTH ** -0.5 * DN_BETA)
    w_out = nrm(ks[11], (DEPTH, D_MODEL, D_MODEL), jnp.float32) * (D_MODEL ** -0.5 * DN_BETA)
    ln_g = 1.0 + 0.02 * nrm(ks[12], (DEPTH, D_MODEL), jnp.float32)
    ln_b = 0.02 * nrm(ks[13], (DEPTH, D_MODEL), jnp.float32)
    return {"x": x, "w_in": w_in, "b_gate": b_gate, "ln_v_g": ln_v_g, "ln_v_b": ln_v_b,
            "w_s": w_s, "b_s": b_s, "conv_w": conv_w, "conv_b": conv_b,
            "w_oa": w_oa, "w_ob": w_ob, "w_out": w_out, "ln_g": ln_g, "ln_b": ln_b}


def _fwd_reference(x, w_in, b_gate, ln_v_g, ln_v_b, w_s, b_s, conv_w, conv_b,
              w_oa, w_ob, w_out, ln_g, ln_b):
    splits = _split_points()
    for l in range(DEPTH):
        p = jnp.einsum('bsd,dc->bsc', x, w_in[l])
        ua, va, za, xb, cb, bb, zb, ga, gb = jnp.split(p, splits, axis=-1)
        ya = chunked_sgu(ua, va, ln_v_g[l], ln_v_b[l], w_s[l], b_s[l]) * jax.nn.silu(za)
        yb = short_gated_conv(xb, cb, bb, conv_w[l], conv_b[l]) * jax.nn.silu(zb)
        gate_a = jax.nn.sigmoid(ga + b_gate[l, :D_MODEL])
        gate_b = jax.nn.sigmoid(gb + b_gate[l, D_MODEL:])
        merged = (gate_a * jnp.einsum('bse,ed->bsd', ya, w_oa[l])
                  + gate_b * jnp.einsum('bse,ed->bsd', yb, w_ob[l]))
        out = jnp.einsum('bsd,de->bse', merged, w_out[l])
        x = layer_norm(DN_ALPHA * x + out, ln_g[l], ln_b[l])
    return x


import jax as _jax
import jax.numpy as _jnp

TWIN_FORMAT = 'train_step'
FWD_PARAMS = ['x', 'w_in', 'b_gate', 'ln_v_g', 'ln_v_b', 'w_s', 'b_s', 'conv_w', 'conv_b', 'w_oa', 'w_ob', 'w_out', 'ln_g', 'ln_b']
TWIN_WEIGHTS = ['w_in', 'b_gate', 'ln_v_g', 'ln_v_b', 'w_s', 'b_s', 'conv_w', 'conv_b', 'w_oa', 'w_ob', 'w_out', 'ln_g', 'ln_b']
TWIN_DIFF_INPUT = 'x'
TWIN_INPUTS = ['x', 'w_in', 'b_gate', 'ln_v_g', 'ln_v_b', 'w_s', 'b_s', 'conv_w', 'conv_b', 'w_oa', 'w_ob', 'w_out', 'ln_g', 'ln_b', 'loss_target', 'm_w_in', 'm_b_gate', 'm_ln_v_g', 'm_ln_v_b', 'm_w_s', 'm_b_s', 'm_conv_w', 'm_conv_b', 'm_w_oa', 'm_w_ob', 'm_w_out', 'm_ln_g', 'm_ln_b', 'v_w_in', 'v_b_gate', 'v_ln_v_g', 'v_ln_v_b', 'v_w_s', 'v_b_s', 'v_conv_w', 'v_conv_b', 'v_w_oa', 'v_w_ob', 'v_w_out', 'v_ln_g', 'v_ln_b']
TWIN_OUTPUTS = ['loss', 'grad_x', 'grad_w_in', 'grad_b_gate', 'grad_ln_v_g', 'grad_ln_v_b', 'grad_w_s', 'grad_b_s', 'grad_conv_w', 'grad_conv_b', 'grad_w_oa', 'grad_w_ob', 'grad_w_out', 'grad_ln_g', 'grad_ln_b', 'delta_w_in', 'delta_b_gate', 'delta_ln_v_g', 'delta_ln_v_b', 'delta_w_s', 'delta_b_s', 'delta_conv_w', 'delta_conv_b', 'delta_w_oa', 'delta_w_ob', 'delta_w_out', 'delta_ln_g', 'delta_ln_b', 'new_m_w_in', 'new_m_b_gate', 'new_m_ln_v_g', 'new_m_ln_v_b', 'new_m_w_s', 'new_m_b_s', 'new_m_conv_w', 'new_m_conv_b', 'new_m_w_oa', 'new_m_w_ob', 'new_m_w_out', 'new_m_ln_g', 'new_m_ln_b', 'new_v_w_in', 'new_v_b_gate', 'new_v_ln_v_g', 'new_v_ln_v_b', 'new_v_w_s', 'new_v_b_s', 'new_v_conv_w', 'new_v_conv_b', 'new_v_w_oa', 'new_v_w_ob', 'new_v_w_out', 'new_v_ln_g', 'new_v_ln_b']
TWIN_LEAF_KINDS = {'loss': 'loss', 'grad_x': 'grad_x', 'grad_w_in': 'grad_w', 'grad_b_gate': 'grad_w', 'grad_ln_v_g': 'grad_w', 'grad_ln_v_b': 'grad_w', 'grad_w_s': 'grad_w', 'grad_b_s': 'grad_w', 'grad_conv_w': 'grad_w', 'grad_conv_b': 'grad_w', 'grad_w_oa': 'grad_w', 'grad_w_ob': 'grad_w', 'grad_w_out': 'grad_w', 'grad_ln_g': 'grad_w', 'grad_ln_b': 'grad_w', 'delta_w_in': 'delta_w', 'delta_b_gate': 'delta_w', 'delta_ln_v_g': 'delta_w', 'delta_ln_v_b': 'delta_w', 'delta_w_s': 'delta_w', 'delta_b_s': 'delta_w', 'delta_conv_w': 'delta_w', 'delta_conv_b': 'delta_w', 'delta_w_oa': 'delta_w', 'delta_w_ob': 'delta_w', 'delta_w_out': 'delta_w', 'delta_ln_g': 'delta_w', 'delta_ln_b': 'delta_w', 'new_m_w_in': 'new_m', 'new_m_b_gate': 'new_m', 'new_m_ln_v_g': 'new_m', 'new_m_ln_v_b': 'new_m', 'new_m_w_s': 'new_m', 'new_m_b_s': 'new_m', 'new_m_conv_w': 'new_m', 'new_m_conv_b': 'new_m', 'new_m_w_oa': 'new_m', 'new_m_w_ob': 'new_m', 'new_m_w_out': 'new_m', 'new_m_ln_g': 'new_m', 'new_m_ln_b': 'new_m', 'new_v_w_in': 'new_v', 'new_v_b_gate': 'new_v', 'new_v_ln_v_g': 'new_v', 'new_v_ln_v_b': 'new_v', 'new_v_w_s': 'new_v', 'new_v_b_s': 'new_v', 'new_v_conv_w': 'new_v', 'new_v_conv_b': 'new_v', 'new_v_w_oa': 'new_v', 'new_v_w_ob': 'new_v', 'new_v_w_out': 'new_v', 'new_v_ln_g': 'new_v', 'new_v_ln_b': 'new_v'}


def _forward(args):
    return _fwd_reference(*[args[k] for k in FWD_PARAMS])


def _output_shape():
    def fwd():
        inp = _fwd_setup_inputs(0)
        return _fwd_reference(*[inp[k] for k in FWD_PARAMS])
    out = _jax.eval_shape(fwd)
    return out.shape, out.dtype

N_MICROBATCH = 1
ADAM_LR = 0.001
ADAM_B1 = 0.9
ADAM_B2 = 0.999
ADAM_EPS = 1e-08
ADAM_WD = 0.01
ADAM_STEP = 10
PER_EXAMPLE_BATCH_AXIS = {'x': 0, 'loss_target': 0}
SHARED_INPUTS = []
_WEIGHT_DTYPES = {'w_in': _jnp.float32, 'b_gate': _jnp.float32, 'ln_v_g': _jnp.float32, 'ln_v_b': _jnp.float32, 'w_s': _jnp.float32, 'b_s': _jnp.float32, 'conv_w': _jnp.float32, 'conv_b': _jnp.float32, 'w_oa': _jnp.float32, 'w_ob': _jnp.float32, 'w_out': _jnp.float32, 'ln_g': _jnp.float32, 'ln_b': _jnp.float32}
MOMENT_SCALE = {'w_in': 9.191146e-03, 'b_gate': 4.056515e-03, 'ln_v_g': 2.831670e-03, 'ln_v_b': 2.651243e-03, 'w_s': 7.857487e-03, 'b_s': 1.113424e-02, 'conv_w': 1.193340e-02, 'conv_b': 1.206042e-02, 'w_oa': 1.480482e-02, 'w_ob': 1.997707e-02, 'w_out': 2.484942e-02, 'ln_g': 3.198203e+01, 'ln_b': 5.046319e-01}


def _to_microbatches(a, axis):
    t = _jnp.moveaxis(a, axis, 0)
    t = t.reshape((N_MICROBATCH, t.shape[0] // N_MICROBATCH) + t.shape[1:])
    return _jnp.moveaxis(t, 1, axis + 1)


def setup_inputs(seed: int = 0) -> dict:
    inp = _fwd_setup_inputs(seed)
    key = _jax.random.fold_in(_jax.random.key(seed), 7919)
    shape, _ = _output_shape()
    out = dict(inp)
    out["loss_target"] = _jax.random.normal(_jax.random.fold_in(key, 0), shape, _jnp.float32)
    for i, name in enumerate(TWIN_WEIGHTS):
        w = inp[name].astype(_jnp.float32)
        if MOMENT_SCALE is None:
            s = _jnp.sqrt(_jnp.mean(_jnp.square(w)) + 1e-30)
        else:
            s = MOMENT_SCALE[name]
        km, kv = _jax.random.split(_jax.random.fold_in(key, i + 1))
        out[name] = w
        out["m_" + name] = s * _jax.random.normal(km, w.shape, _jnp.float32)
        out["v_" + name] = (s * s) * _jax.random.uniform(kv, w.shape, _jnp.float32, 0.5, 1.5)
    if N_MICROBATCH > 1:
        for name, axis in PER_EXAMPLE_BATCH_AXIS.items():
            out[name] = _to_microbatches(out[name], axis)
    return {'x': out['x'], 'w_in': out['w_in'], 'b_gate': out['b_gate'], 'ln_v_g': out['ln_v_g'], 'ln_v_b': out['ln_v_b'], 'w_s': out['w_s'], 'b_s': out['b_s'], 'conv_w': out['conv_w'], 'conv_b': out['conv_b'], 'w_oa': out['w_oa'], 'w_ob': out['w_ob'], 'w_out': out['w_out'], 'ln_g': out['ln_g'], 'ln_b': out['ln_b'], 'loss_target': out['loss_target'], 'm_w_in': out['m_w_in'], 'm_b_gate': out['m_b_gate'], 'm_ln_v_g': out['m_ln_v_g'], 'm_ln_v_b': out['m_ln_v_b'], 'm_w_s': out['m_w_s'], 'm_b_s': out['m_b_s'], 'm_conv_w': out['m_conv_w'], 'm_conv_b': out['m_conv_b'], 'm_w_oa': out['m_w_oa'], 'm_w_ob': out['m_w_ob'], 'm_w_out': out['m_w_out'], 'm_ln_g': out['m_ln_g'], 'm_ln_b': out['m_ln_b'], 'v_w_in': out['v_w_in'], 'v_b_gate': out['v_b_gate'], 'v_ln_v_g': out['v_ln_v_g'], 'v_ln_v_b': out['v_ln_v_b'], 'v_w_s': out['v_w_s'], 'v_b_s': out['v_b_s'], 'v_conv_w': out['v_conv_w'], 'v_conv_b': out['v_conv_b'], 'v_w_oa': out['v_w_oa'], 'v_w_ob': out['v_w_ob'], 'v_w_out': out['v_w_out'], 'v_ln_g': out['v_ln_g'], 'v_ln_b': out['v_ln_b']}


def _loss(weights, diff, rest, loss_target):
    with _jax.named_scope("forward"):
        args = {**rest, TWIN_DIFF_INPUT: diff, **{k: w.astype(_WEIGHT_DTYPES[k]) for k, w in weights.items()}}
        y = _forward(args)
    with _jax.named_scope("loss_head"):
        err = _jnp.square(y.astype(_jnp.float32) - loss_target)
        return 0.5 * _jnp.sum(_jnp.mean(err, axis=-1)) if err.ndim else 0.5 * err


def _adamw(w, g, m, v):
    m = ADAM_B1 * m + (1.0 - ADAM_B1) * g
    v = ADAM_B2 * v + (1.0 - ADAM_B2) * _jnp.square(g)
    m_hat = m / (1.0 - ADAM_B1 ** ADAM_STEP)
    v_hat = v / (1.0 - ADAM_B2 ** ADAM_STEP)
    delta = -ADAM_LR * (m_hat / (_jnp.sqrt(v_hat) + ADAM_EPS) + ADAM_WD * w)
    return delta, m, v


def reference(x, w_in, b_gate, ln_v_g, ln_v_b, w_s, b_s, conv_w, conv_b, w_oa, w_ob, w_out, ln_g, ln_b, loss_target, m_w_in, m_b_gate, m_ln_v_g, m_ln_v_b, m_w_s, m_b_s, m_conv_w, m_conv_b, m_w_oa, m_w_ob, m_w_out, m_ln_g, m_ln_b, v_w_in, v_b_gate, v_ln_v_g, v_ln_v_b, v_w_s, v_b_s, v_conv_w, v_conv_b, v_w_oa, v_w_ob, v_w_out, v_ln_g, v_ln_b):
    given = dict(x=x, w_in=w_in, b_gate=b_gate, ln_v_g=ln_v_g, ln_v_b=ln_v_b, w_s=w_s, b_s=b_s, conv_w=conv_w, conv_b=conv_b, w_oa=w_oa, w_ob=w_ob, w_out=w_out, ln_g=ln_g, ln_b=ln_b, loss_target=loss_target, m_w_in=m_w_in, m_b_gate=m_b_gate, m_ln_v_g=m_ln_v_g, m_ln_v_b=m_ln_v_b, m_w_s=m_w_s, m_b_s=m_b_s, m_conv_w=m_conv_w, m_conv_b=m_conv_b, m_w_oa=m_w_oa, m_w_ob=m_w_ob, m_w_out=m_w_out, m_ln_g=m_ln_g, m_ln_b=m_ln_b, v_w_in=v_w_in, v_b_gate=v_b_gate, v_ln_v_g=v_ln_v_g, v_ln_v_b=v_ln_v_b, v_w_s=v_w_s, v_b_s=v_b_s, v_conv_w=v_conv_w, v_conv_b=v_conv_b, v_w_oa=v_w_oa, v_w_ob=v_w_ob, v_w_out=v_w_out, v_ln_g=v_ln_g, v_ln_b=v_ln_b)
    weights = {n: given[n] for n in TWIN_WEIGHTS}
    shared = {n: given[n] for n in SHARED_INPUTS}
    per_example = {n: given[n] for n in ['x']}
    grad_fn = _jax.value_and_grad(_loss, argnums=(0, 1))

    def one_microbatch(ex, loss_target):
        ex = dict(ex)
        diff = ex.pop(TWIN_DIFF_INPUT)
        return grad_fn(weights, diff, {**shared, **ex}, loss_target)

    if N_MICROBATCH == 1:
        loss, (grad_w, grad_x) = one_microbatch(per_example, given["loss_target"])
    else:
        def body(carry, xs):
            loss_sum, grad_sum = carry
            l_k, (gw_k, gx_k) = one_microbatch(xs[0], xs[1])
            with _jax.named_scope("update"):
                return (loss_sum + l_k, _jax.tree.map(_jnp.add, grad_sum, gw_k)), gx_k

        init = (_jnp.zeros((), _jnp.float32), _jax.tree.map(_jnp.zeros_like, weights))
        (loss, grad_w), grad_x = _jax.lax.scan(body, init, (per_example, given["loss_target"]))
    with _jax.named_scope("update"):
        delta_w, new_m, new_v = {}, {}, {}
        for n in TWIN_WEIGHTS:
            delta_w[n], new_m[n], new_v[n] = _adamw(weights[n], grad_w[n], given["m_" + n], given["v_" + n])
    return (loss, grad_x, *[grad_w[n] for n in TWIN_WEIGHTS], *[delta_w[n] for n in TWIN_WEIGHTS],
            *[new_m[n] for n in TWIN_WEIGHTS], *[new_v[n] for n in TWIN_WEIGHTS])
```

```python
import functools

import jax
import jax.numpy as jnp
from jax import lax
from jax.experimental import pallas as pl
from jax.experimental.pallas import tpu as pltpu

F32 = jnp.float32
BF16 = jnp.bfloat16
MESH = pl.DeviceIdType.MESH

N_DEV = 8
N_CHIP = 4
HEADS = 8
CHUNK = 128
N_IN = 9
N_MIX = 7
N_SAVE = 8
LN_EPS = 1e-5
DN_ALPHA = 2.0 ** 0.25
ADAM_LR = 0.001
ADAM_B1 = 0.9
ADAM_B2 = 0.999
ADAM_EPS = 1e-08
ADAM_WD = 0.01
ADAM_STEP = 10
GELU_C0 = 0.7978845608028654
GELU_C1 = 0.044715
SUBLANES = 8
VMEM_LIMIT = 56 << 20


def _cparams(sem):
    return pltpu.CompilerParams(dimension_semantics=sem, vmem_limit_bytes=VMEM_LIMIT)


def _tile(n, want):
    t = min(n, want)
    while n % t:
        t //= 2
    return t


def _gelu(u):
    t = jnp.tanh(GELU_C0 * (u + GELU_C1 * u * u * u))
    return 0.5 * u * (1.0 + t), t


def _gelu_grad(u, t):
    return 0.5 * (1.0 + t) + 0.5 * u * (1.0 - t * t) * (GELU_C0 * (1.0 + 3.0 * GELU_C1 * u * u))


def _silu(z):
    s = jax.nn.sigmoid(z)
    return z * s, s


def _silu_grad(z, s):
    return s * (1.0 + z * (1.0 - s))


def _fold8(a):
    return jnp.sum(a.reshape(a.shape[0] // SUBLANES, SUBLANES, a.shape[1]), axis=0)


def _dot(a, b):
    return jnp.dot(a, b, preferred_element_type=F32)


def _dot_nt(a, b):
    return lax.dot_general(a, b, (((1,), (1,)), ((), ())), preferred_element_type=F32)


def _dot_tn(a, b):
    return lax.dot_general(a, b, (((0,), (0,)), ((), ())), preferred_element_type=F32)


def _tril_mask():
    r = lax.broadcasted_iota(jnp.int32, (CHUNK, CHUNK), 0)
    c = lax.broadcasted_iota(jnp.int32, (CHUNK, CHUNK), 1)
    return c <= r


def _cast_bf16(a, name):
    rows, cols = a.shape
    rb = _tile(rows, 256)

    def body(a_ref, o_ref):
        o_ref[...] = a_ref[...].astype(BF16)

    return pl.pallas_call(
        body, name=name, grid=(rows // rb,),
        in_specs=[pl.BlockSpec((rb, cols), lambda i: (i, 0))],
        out_specs=pl.BlockSpec((rb, cols), lambda i: (i, 0)),
        out_shape=jax.ShapeDtypeStruct((rows, cols), BF16),
        compiler_params=_cparams(("parallel",)),
    )(a)


def _shard_ref(full, kind, s, n):
    if kind == "col":
        return full.at[:, pl.ds(pl.multiple_of(s * n, 128), n)]
    if kind == "row":
        return full.at[pl.ds(pl.multiple_of(s * n, SUBLANES), n), :]
    return full.at[s]


def _all_gather(shards, kinds, name, vmem):
    nt = len(shards)
    out_shapes = []
    for a, kind in zip(shards, kinds):
        if kind == "col":
            out_shapes.append(jax.ShapeDtypeStruct((a.shape[0], N_DEV * a.shape[1]), a.dtype))
        elif kind == "row":
            out_shapes.append(jax.ShapeDtypeStruct((N_DEV * a.shape[0], a.shape[1]), a.dtype))
        else:
            out_shapes.append(jax.ShapeDtypeStruct((N_DEV,) + a.shape, a.dtype))

    def body(*refs):
        srcs, fulls = refs[:nt], refs[nt:2 * nt]
        send_sems, recv_sems, local_sems = refs[2 * nt:]
        x, y, c = lax.axis_index("x"), lax.axis_index("y"), lax.axis_index("c")
        sibling = (x, y, 1 - c)
        chips = [(1 - x, y), (x, 1 - y), (1 - x, 1 - y)]

        def dev(px, py, pc):
            return 4 * px + 2 * py + pc

        def region(t, s):
            a, kind = shards[t], kinds[t]
            n = a.shape[1] if kind == "col" else a.shape[0]
            return _shard_ref(fulls[t], kind, s, n)

        def copy(t, k, block, to, own=False):
            return pltpu.make_async_remote_copy(
                src_ref=srcs[t] if own else region(t, block), dst_ref=region(t, block),
                send_sem=send_sems.at[7 * t + k], recv_sem=recv_sems.at[7 * t + k],
                device_id=to, device_id_type=MESH)

        me = dev(x, y, c)
        started = []
        for t in range(nt):
            mine = pltpu.make_async_copy(srcs[t], region(t, me), local_sems.at[t])
            mine.start()
            started.append(mine)
        first = []
        for t in range(nt):
            first.append(copy(t, 0, me, sibling, own=True))
            for j, chip in enumerate(chips):
                first.append(copy(t, 1 + j, me, (*chip, c), own=True))
        for cp in first:
            cp.start()
        passed = []
        for t in range(nt):
            for j, chip in enumerate(chips):
                blk = dev(*chip, c)
                copy(t, 1 + j, blk, sibling).wait_recv()
                fwd = copy(t, 4 + j, blk, sibling)
                fwd.start()
                passed.append(fwd)
        for t in range(nt):
            copy(t, 0, dev(x, y, 1 - c), sibling).wait_recv()
            for j, chip in enumerate(chips):
                copy(t, 4 + j, dev(*chip, 1 - c), sibling).wait_recv()
        for cp in first + passed:
            cp.wait_send()
        for mine in started:
            mine.wait()

    space = pltpu.VMEM if vmem else pl.ANY
    return pl.pallas_call(
        body, name=name,
        in_specs=[pl.BlockSpec(memory_space=space)] * nt,
        out_specs=[pl.BlockSpec(memory_space=space)] * nt,
        out_shape=out_shapes,
        scratch_shapes=[pltpu.SemaphoreType.DMA((7 * nt,)), pltpu.SemaphoreType.DMA((7 * nt,)),
                        pltpu.SemaphoreType.DMA((nt,))],
        compiler_params=pltpu.CompilerParams(vmem_limit_bytes=VMEM_LIMIT, has_side_effects=True),
    )(*shards)


def _pair_exchange(grads, kinds, name):
    nt = len(grads)
    shard_shapes = []
    for g, kind in zip(grads, kinds):
        shard_shapes.append((g.shape[0], g.shape[1] // N_DEV) if kind == "col" else (g.shape[0] // N_DEV, g.shape[1]))

    def body(*refs):
        srcs, lands = refs[:nt], refs[nt:2 * nt]
        send_sems, recv_sems = refs[2 * nt:]
        x, y, c = lax.axis_index("x"), lax.axis_index("y"), lax.axis_index("c")
        copies = []
        for t in range(nt):
            n = shard_shapes[t][1] if kinds[t] == "col" else shard_shapes[t][0]
            for k in range(N_CHIP):
                cp = pltpu.make_async_remote_copy(
                    src_ref=_shard_ref(srcs[t], kinds[t], 2 * k + 1 - c, n), dst_ref=lands[t].at[k],
                    send_sem=send_sems.at[N_CHIP * t + k], recv_sem=recv_sems.at[N_CHIP * t + k],
                    device_id=(x, y, 1 - c), device_id_type=MESH)
                cp.start()
                copies.append(cp)
        for cp in copies:
            cp.wait()

    return pl.pallas_call(
        body, name=name,
        in_specs=[pl.BlockSpec(memory_space=pl.ANY)] * nt,
        out_specs=[pl.BlockSpec(memory_space=pl.ANY)] * nt,
        out_shape=[jax.ShapeDtypeStruct((N_CHIP,) + s, g.dtype) for s, g in zip(shard_shapes, grads)],
        scratch_shapes=[pltpu.SemaphoreType.DMA((N_CHIP * nt,)), pltpu.SemaphoreType.DMA((N_CHIP * nt,))],
        compiler_params=pltpu.CompilerParams(has_side_effects=True),
    )(*grads)


def _chip_exchange(parts, name):
    nt = len(parts)

    def body(*refs):
        srcs, lands = refs[:nt], refs[nt:2 * nt]
        send_sems, recv_sems = refs[2 * nt:]
        x, y, c = lax.axis_index("x"), lax.axis_index("y"), lax.axis_index("c")
        my_chip = 2 * x + y
        chips = [(1 - x, y), (x, 1 - y), (1 - x, 1 - y)]
        sends = []
        for t in range(nt):
            for j, (px, py) in enumerate(chips):
                cp = pltpu.make_async_remote_copy(
                    src_ref=srcs[t].at[2 * px + py], dst_ref=lands[t].at[my_chip],
                    send_sem=send_sems.at[3 * t + j], recv_sem=recv_sems.at[3 * t + j],
                    device_id=(px, py, c), device_id_type=MESH)
                cp.start()
                sends.append(cp)
        for t in range(nt):
            for j, (px, py) in enumerate(chips):
                pltpu.make_async_remote_copy(
                    src_ref=srcs[t].at[my_chip], dst_ref=lands[t].at[2 * px + py],
                    send_sem=send_sems.at[3 * t + j], recv_sem=recv_sems.at[3 * t + j],
                    device_id=(px, py, c), device_id_type=MESH).wait_recv()
        for cp in sends:
            cp.wait_send()

    return pl.pallas_call(
        body, name=name,
        in_specs=[pl.BlockSpec(memory_space=pl.ANY)] * nt,
        out_specs=[pl.BlockSpec(memory_space=pl.ANY)] * nt,
        out_shape=[jax.ShapeDtypeStruct(p.shape, p.dtype) for p in parts],
        scratch_shapes=[pltpu.SemaphoreType.DMA((3 * nt,)), pltpu.SemaphoreType.DMA((3 * nt,))],
        compiler_params=pltpu.CompilerParams(has_side_effects=True),
    )(*parts)


def _pair_sum(grad, land, kind, c_arr, name):
    _, r, w = land.shape
    rb = _tile(r, 256)
    nrb = r // rb

    def body(c_ref, g_ref, l_ref, o_ref):
        o_ref[...] = (g_ref[...].astype(F32) + l_ref[...].astype(F32)).astype(o_ref.dtype)

    if kind == "col":
        g_spec = pl.BlockSpec((rb, w), lambda k, i, c: (i, 2 * k + c[0]))
    else:
        g_spec = pl.BlockSpec((rb, w), lambda k, i, c: ((2 * k + c[0]) * nrb + i, 0))
    return pl.pallas_call(
        body, name=name,
        grid_spec=pltpu.PrefetchScalarGridSpec(
            num_scalar_prefetch=1, grid=(N_CHIP, nrb),
            in_specs=[g_spec, pl.BlockSpec((None, rb, w), lambda k, i, c: (k, i, 0))],
            out_specs=pl.BlockSpec((None, rb, w), lambda k, i, c: (k, i, 0))),
        out_shape=jax.ShapeDtypeStruct(land.shape, BF16),
        compiler_params=_cparams(("parallel", "parallel")),
    )(c_arr, grad, land)


def _adam(w, g, m, v):
    m = ADAM_B1 * m + (1.0 - ADAM_B1) * g
    v = ADAM_B2 * v + (1.0 - ADAM_B2) * jnp.square(g)
    m_hat = m / (1.0 - ADAM_B1 ** ADAM_STEP)
    v_hat = v / (1.0 - ADAM_B2 ** ADAM_STEP)
    delta = -ADAM_LR * (m_hat / (jnp.sqrt(v_hat) + ADAM_EPS) + ADAM_WD * w)
    return delta, m, v


def _sum_adam(part, land, w, m, v, chip_arr, name):
    r, wd = w.shape
    rb = _tile(r, 128)

    def body(k_ref, own, r1, r2, r3, w_ref, m_ref, v_ref, g_out, d_out, m_out, v_out):
        g = own[...].astype(F32) + r1[...].astype(F32) + r2[...].astype(F32) + r3[...].astype(F32)
        d, mn, vn = _adam(w_ref[...], g, m_ref[...], v_ref[...])
        g_out[...] = g
        d_out[...] = d
        m_out[...] = mn
        v_out[...] = vn

    def slot(off):
        return pl.BlockSpec((None, rb, wd), lambda i, k: ((k[0] + off) % N_CHIP, i, 0))

    plain = pl.BlockSpec((rb, wd), lambda i, k: (i, 0))
    return pl.pallas_call(
        body, name=name,
        grid_spec=pltpu.PrefetchScalarGridSpec(
            num_scalar_prefetch=1, grid=(r // rb,),
            in_specs=[slot(0), slot(1), slot(2), slot(3), plain, plain, plain],
            out_specs=[plain] * 4),
        out_shape=[jax.ShapeDtypeStruct(w.shape, F32)] * 4,
        compiler_params=_cparams(("parallel",)),
    )(chip_arr, part, land, land, land, w, m, v)


def _small_sum(gathered, name):
    _, r, w = gathered.shape

    def body(g_ref, o_ref):
        acc = g_ref[0]
        for d in range(1, N_DEV):
            acc = acc + g_ref[d]
        o_ref[...] = acc

    return pl.pallas_call(body, name=name, out_shape=jax.ShapeDtypeStruct((r, w), F32))(gathered)


def _small_adam(ws, gs, ms, vs, name):
    n = len(ws)

    def body(*refs):
        ins, outs = refs[:4 * n], refs[4 * n:]
        for t in range(n):
            d, mn, vn = _adam(ins[t][...], ins[n + t][...], ins[2 * n + t][...], ins[3 * n + t][...])
            outs[t][...] = d
            outs[n + t][...] = mn
            outs[2 * n + t][...] = vn

    shapes = [jax.ShapeDtypeStruct(w.shape, F32) for w in ws]
    return pl.pallas_call(body, name=name, out_shape=shapes * 3)(*ws, *gs, *ms, *vs)


def _mixer_fwd(xb, w_in, ln_v_g, ln_v_b, w_s, b_s3, conv_w, conv_b):
    t_len, d = xb.shape
    hd = d // HEADS
    tm = _tile(t_len, 512)
    nt = t_len // tm

    def body(x_ref, wu, wv, wz, wxb, wcb, wbb, wzb, lng, lnb, ws_ref, bs_ref, cw_ref, cb_ref,
             save_ref, ya_ref, yb_ref, hbuf):
        i = pl.program_id(1)
        xt = x_ref[...]
        u = _dot(xt, wu[...])
        v = _dot(xt, wv[...])
        z = _dot(xt, wz[...])
        save_ref[0] = u.astype(BF16)
        save_ref[1] = v.astype(BF16)
        save_ref[2] = z.astype(BF16)
        gu, _ = _gelu(u)
        gv, _ = _gelu(v)
        mu = jnp.mean(gv, axis=-1, keepdims=True)
        dv = gv - mu
        var = jnp.mean(dv * dv, axis=-1, keepdims=True)
        vn = (dv * lax.rsqrt(var + LN_EPS) * lng[...] + lnb[...]).astype(BF16)
        sz, _ = _silu(z)
        gate = gu * sz
        wm = jnp.where(_tril_mask(), ws_ref[0], 0.0).astype(BF16)
        bs = bs_ref[0]
        for ck in range(tm // CHUNK):
            rows = slice(ck * CHUNK, (ck + 1) * CHUNK)
            mixed = _dot(wm, vn[rows]) + bs
            ya_ref[rows, :] = (gate[rows] * mixed).astype(BF16)

        xbv = _dot(xt, wxb[...])
        cbv = _dot(xt, wcb[...])
        bbv = _dot(xt, wbb[...])
        zbv = _dot(xt, wzb[...])
        save_ref[3] = xbv.astype(BF16)
        save_ref[4] = cbv.astype(BF16)
        save_ref[5] = bbv.astype(BF16)
        save_ref[6] = zbv.astype(BF16)
        h = cbv * xbv

        @pl.when(i == 0)
        def _():
            hbuf[0:SUBLANES, :] = jnp.zeros((SUBLANES, hd), F32)

        hbuf[SUBLANES:SUBLANES + tm, :] = h
        h1 = hbuf[SUBLANES - 1:SUBLANES - 1 + tm, :]
        h2 = hbuf[SUBLANES - 2:SUBLANES - 2 + tm, :]
        conv = cb_ref[...] + cw_ref[0:1, :] * h2 + cw_ref[1:2, :] * h1 + cw_ref[2:3, :] * h
        hbuf[0:SUBLANES, :] = h[tm - SUBLANES:tm, :]
        save_ref[7] = conv.astype(BF16)
        szb, _ = _silu(zbv)
        yb_ref[...] = (bbv * conv * szb).astype(BF16)

    def wspec(b):
        return pl.BlockSpec((d, hd), lambda g, i, b=b: (0, b * HEADS + g))

    vec = pl.BlockSpec((1, hd), lambda g, i: (0, g))
    return pl.pallas_call(
        body, name="mixer_fwd", grid=(HEADS, nt),
        in_specs=[pl.BlockSpec((tm, d), lambda g, i: (i, 0))] + [wspec(b) for b in range(N_MIX)] + [
            vec, vec,
            pl.BlockSpec((1, CHUNK, CHUNK), lambda g, i: (g, 0, 0)),
            pl.BlockSpec((1, CHUNK, 1), lambda g, i: (g, 0, 0)),
            pl.BlockSpec((3, hd), lambda g, i: (0, g)),
            vec],
        out_specs=[pl.BlockSpec((N_SAVE, tm, hd), lambda g, i: (0, i, g)),
                   pl.BlockSpec((tm, hd), lambda g, i: (i, g)),
                   pl.BlockSpec((tm, hd), lambda g, i: (i, g))],
        out_shape=[jax.ShapeDtypeStruct((N_SAVE, t_len, d), BF16),
                   jax.ShapeDtypeStruct((t_len, d), BF16),
                   jax.ShapeDtypeStruct((t_len, d), BF16)],
        scratch_shapes=[pltpu.VMEM((SUBLANES + tm, hd), F32)],
        compiler_params=_cparams(("parallel", "arbitrary")),
    )(xb, *([w_in] * N_MIX), ln_v_g, ln_v_b, w_s, b_s3, conv_w, conv_b)


def _merge_fwd(xb, ya, yb, w_in, w_oa, w_ob, b_gate):
    t_len, d = xb.shape
    tm = _tile(t_len, 512)
    tn = _tile(d, 512)
    nj = d // tn

    def body(x_ref, ya_ref, yb_ref, wga, wgb, woa, wob, bga, bgb, sa_ref, sb_ref, pa_ref, pb_ref, mg_ref):
        xt = x_ref[...]
        sa = jax.nn.sigmoid(_dot(xt, wga[...]) + bga[...])
        sb = jax.nn.sigmoid(_dot(xt, wgb[...]) + bgb[...])
        pa = _dot(ya_ref[...], woa[...])
        pb = _dot(yb_ref[...], wob[...])
        sa_ref[...] = sa.astype(BF16)
        sb_ref[...] = sb.astype(BF16)
        pa_ref[...] = pa.astype(BF16)
        pb_ref[...] = pb.astype(BF16)
        mg_ref[...] = (sa * pa + sb * pb).astype(BF16)

    row = pl.BlockSpec((tm, d), lambda j, i: (i, 0))
    out = pl.BlockSpec((tm, tn), lambda j, i: (i, j))
    return pl.pallas_call(
        body, name="merge_fwd", grid=(nj, t_len // tm),
        in_specs=[row, row, row,
                  pl.BlockSpec((d, tn), lambda j, i: (0, 7 * nj + j)),
                  pl.BlockSpec((d, tn), lambda j, i: (0, 8 * nj + j)),
                  pl.BlockSpec((d, tn), lambda j, i: (0, j)),
                  pl.BlockSpec((d, tn), lambda j, i: (0, j)),
                  pl.BlockSpec((1, tn), lambda j, i: (0, j)),
                  pl.BlockSpec((1, tn), lambda j, i: (0, nj + j))],
        out_specs=[out] * 5,
        out_shape=[jax.ShapeDtypeStruct((t_len, d), BF16)] * 5,
        compiler_params=_cparams(("parallel", "arbitrary")),
    )(xb, ya, yb, w_in, w_in, w_oa, w_ob, b_gate, b_gate)


def _out_ln_loss(merged, w_out, x, target, ln_g, ln_b):
    t_len, d = x.shape
    tm = _tile(t_len, 256)
    nt = t_len // tm

    def body(mg_ref, w_ref, x_ref, t_ref, g_ref, b_ref, dz_ref, gx_ref, glg_ref, glb_ref, ls_ref, a_g, a_b, a_l):
        i = pl.program_id(0)

        @pl.when(i == 0)
        def _():
            a_g[...] = jnp.zeros_like(a_g)
            a_b[...] = jnp.zeros_like(a_b)
            a_l[...] = jnp.zeros_like(a_l)

        zres = DN_ALPHA * x_ref[...] + _dot(mg_ref[...], w_ref[...])
        mu = jnp.mean(zres, axis=-1, keepdims=True)
        dc = zres - mu
        var = jnp.mean(dc * dc, axis=-1, keepdims=True)
        rstd = lax.rsqrt(var + LN_EPS)
        xhat = dc * rstd
        g = g_ref[...]
        err = xhat * g + b_ref[...] - t_ref[...]
        dy = err * (1.0 / d)
        a_l[...] += _fold8(err * err)
        a_g[...] += _fold8(dy * xhat)
        a_b[...] += _fold8(dy)
        dxh = dy * g
        m1 = jnp.mean(dxh, axis=-1, keepdims=True)
        m2 = jnp.mean(dxh * xhat, axis=-1, keepdims=True)
        dz = rstd * (dxh - m1 - xhat * m2)
        dz_ref[...] = dz.astype(BF16)
        gx_ref[...] = DN_ALPHA * dz

        @pl.when(i == nt - 1)
        def _():
            glg_ref[...] = jnp.sum(a_g[...], axis=0, keepdims=True)
            glb_ref[...] = jnp.sum(a_b[...], axis=0, keepdims=True)
            ls_ref[...] = jnp.sum(a_l[...], axis=0, keepdims=True)

    row = pl.BlockSpec((tm, d), lambda i: (i, 0))
    vec = pl.BlockSpec((1, d), lambda i: (0, 0))
    return pl.pallas_call(
        body, name="out_ln_loss", grid=(nt,),
        in_specs=[row, pl.BlockSpec((d, d), lambda i: (0, 0)), row, row, vec, vec],
        out_specs=[row, row, vec, vec, vec],
        out_shape=[jax.ShapeDtypeStruct((t_len, d), BF16), jax.ShapeDtypeStruct((t_len, d), F32)]
        + [jax.ShapeDtypeStruct((1, d), F32)] * 3,
        scratch_shapes=[pltpu.VMEM((SUBLANES, d), F32)] * 3,
        compiler_params=_cparams(("arbitrary",)),
    )(merged, w_out, x, target, ln_g, ln_b)


def _merge_bwd(dz, w_out, sa, sb, pa, pb):
    t_len, d = dz.shape
    tm = _tile(t_len, 512)
    tn = _tile(d, 512)
    nt = t_len // tm

    def body(dz_ref, w_ref, sa_ref, sb_ref, pa_ref, pb_ref, da_ref, db_ref, dg_ref, ga_ref, gb_ref, acc_a, acc_b):
        i = pl.program_id(1)

        @pl.when(i == 0)
        def _():
            acc_a[...] = jnp.zeros_like(acc_a)
            acc_b[...] = jnp.zeros_like(acc_b)

        dm = _dot_nt(dz_ref[...], w_ref[...])
        sa = sa_ref[...].astype(F32)
        sb = sb_ref[...].astype(F32)
        da = dm * sa
        db = dm * sb
        da_ref[...] = da.astype(BF16)
        db_ref[...] = db.astype(BF16)
        dga = da * pa_ref[...].astype(F32) * (1.0 - sa)
        dgb = db * pb_ref[...].astype(F32) * (1.0 - sb)
        dg_ref[0] = dga.astype(BF16)
        dg_ref[1] = dgb.astype(BF16)
        acc_a[...] += _fold8(dga)
        acc_b[...] += _fold8(dgb)

        @pl.when(i == nt - 1)
        def _():
            ga_ref[...] = jnp.sum(acc_a[...], axis=0, keepdims=True)
            gb_ref[...] = jnp.sum(acc_b[...], axis=0, keepdims=True)

    blk = pl.BlockSpec((tm, tn), lambda j, i: (i, j))
    vec = pl.BlockSpec((1, tn), lambda j, i: (0, j))
    return pl.pallas_call(
        body, name="merge_bwd", grid=(d // tn, nt),
        in_specs=[pl.BlockSpec((tm, d), lambda j, i: (i, 0)), pl.BlockSpec((tn, d), lambda j, i: (j, 0)),
                  blk, blk, blk, blk],
        out_specs=[blk, blk, pl.BlockSpec((2, tm, tn), lambda j, i: (0, i, j)), vec, vec],
        out_shape=[jax.ShapeDtypeStruct((t_len, d), BF16)] * 2 + [jax.ShapeDtypeStruct((2, t_len, d), BF16)]
        + [jax.ShapeDtypeStruct((1, d), F32)] * 2,
        scratch_shapes=[pltpu.VMEM((SUBLANES, tn), F32)] * 2,
        compiler_params=_cparams(("parallel", "arbitrary")),
    )(dz, w_out, sa, sb, pa, pb)


def _grad_w(a, b, name):
    t_len, m = a.shape
    n = b.shape[1]
    tm, tn, tk = _tile(m, 1024), _tile(n, 1024), _tile(t_len, 512)
    nk = t_len // tk

    def body(a_ref, b_ref, o_ref, acc):
        k = pl.program_id(2)

        @pl.when(k == 0)
        def _():
            acc[...] = jnp.zeros_like(acc)

        acc[...] += _dot_tn(a_ref[...], b_ref[...])

        @pl.when(k == nk - 1)
        def _():
            o_ref[...] = acc[...].astype(BF16)

    return pl.pallas_call(
        body, name=name, grid=(m // tm, n // tn, nk),
        in_specs=[pl.BlockSpec((tk, tm), lambda i, j, k: (k, i)), pl.BlockSpec((tk, tn), lambda i, j, k: (k, j))],
        out_specs=pl.BlockSpec((tm, tn), lambda i, j, k: (i, j)),
        out_shape=jax.ShapeDtypeStruct((m, n), BF16),
        scratch_shapes=[pltpu.VMEM((tm, tn), F32)],
        compiler_params=_cparams(("parallel", "parallel", "arbitrary")),
    )(a, b)


def _mixer_bwd(da, db, w_oa, w_ob, saved, dgate, ln_v_g, ln_v_b, w_s, b_s3, conv_w):
    t_len, d = da.shape
    hd = d // HEADS
    tm = _tile(t_len, 512)
    nt = t_len // tm

    def body(da_ref, db_ref, woa, wob, sv, dgt, lng, lnb, ws_ref, bs_ref, cw_ref,
             dp_ref, gws_ref, gbs_ref, glg_ref, glb_ref, gcw_ref, gcb_ref,
             dbuf, carry, a_ws, a_bs, a_lg, a_lb, a_c0, a_c1, a_c2, a_cb):
        i = pl.program_id(1)

        @pl.when(i == 0)
        def _():
            carry[...] = jnp.zeros_like(carry)
            for a in (a_ws, a_bs, a_lg, a_lb, a_c0, a_c1, a_c2, a_cb):
                a[...] = jnp.zeros_like(a)

        dya = _dot_nt(da_ref[...], woa[...])
        u = sv[0].astype(F32)
        v = sv[1].astype(F32)
        z = sv[2].astype(F32)
        gu, tu = _gelu(u)
        gv, tv = _gelu(v)
        mu = jnp.mean(gv, axis=-1, keepdims=True)
        dvc = gv - mu
        var = jnp.mean(dvc * dvc, axis=-1, keepdims=True)
        rstd = lax.rsqrt(var + LN_EPS)
        vhat = dvc * rstd
        g = lng[...]
        vn = (vhat * g + lnb[...]).astype(BF16)
        sz, s = _silu(z)
        t1 = dya * sz
        dmixed = t1 * gu
        dmixed_b = dmixed.astype(BF16)
        wm = jnp.where(_tril_mask(), ws_ref[0], 0.0).astype(BF16)
        bs = bs_ref[0]
        dvn_parts = []
        mixed_parts = []
        gws = a_ws[...]
        gbs = a_bs[...]
        for ck in range(tm // CHUNK):
            rows = slice(ck * CHUNK, (ck + 1) * CHUNK)
            mixed_parts.append(_dot(wm, vn[rows]) + bs)
            gws = gws + _dot_nt(dmixed_b[rows], vn[rows])
            gbs = gbs + dmixed[rows]
            dvn_parts.append(_dot_tn(wm, dmixed_b[rows]))
        a_ws[...] = gws
        a_bs[...] = gbs
        mixed = jnp.concatenate(mixed_parts, axis=0)
        dvn = jnp.concatenate(dvn_parts, axis=0)
        dp_ref[0] = (t1 * mixed * _gelu_grad(u, tu)).astype(BF16)
        dp_ref[2] = (dya * gu * mixed * _silu_grad(z, s)).astype(BF16)
        a_lg[...] += _fold8(dvn * vhat)
        a_lb[...] += _fold8(dvn)
        dvh = dvn * g
        m1 = jnp.mean(dvh, axis=-1, keepdims=True)
        m2 = jnp.mean(dvh * vhat, axis=-1, keepdims=True)
        dp_ref[1] = (rstd * (dvh - m1 - vhat * m2) * _gelu_grad(v, tv)).astype(BF16)

        dyb = _dot_nt(db_ref[...], wob[...])
        xbv = sv[3].astype(F32)
        cbv = sv[4].astype(F32)
        bbv = sv[5].astype(F32)
        zbv = sv[6].astype(F32)
        conv = sv[7].astype(F32)
        szb, sb = _silu(zbv)
        dp_ref[5] = (dyb * conv * szb).astype(BF16)
        dp_ref[6] = (dyb * bbv * conv * _silu_grad(zbv, sb)).astype(BF16)
        dconv = dyb * bbv * szb
        dbuf[0:tm, :] = dconv
        dbuf[tm:tm + SUBLANES, :] = carry[...]
        dc1 = dbuf[1:tm + 1, :]
        dc2 = dbuf[2:tm + 2, :]
        carry[...] = dconv[0:SUBLANES, :]
        h = cbv * xbv
        a_c2[...] += _fold8(dconv * h)
        a_c1[...] += _fold8(dc1 * h)
        a_c0[...] += _fold8(dc2 * h)
        a_cb[...] += _fold8(dconv)
        dh = cw_ref[2:3, :] * dconv + cw_ref[1:2, :] * dc1 + cw_ref[0:1, :] * dc2
        dp_ref[3] = (dh * cbv).astype(BF16)
        dp_ref[4] = (dh * xbv).astype(BF16)
        dp_ref[7] = dgt[0]
        dp_ref[8] = dgt[1]

        @pl.when(i == nt - 1)
        def _():
            gws_ref[0] = jnp.where(_tril_mask(), a_ws[...], 0.0)
            gbs_ref[0] = jnp.sum(a_bs[...], axis=1, keepdims=True)
            glg_ref[...] = jnp.sum(a_lg[...], axis=0, keepdims=True)
            glb_ref[...] = jnp.sum(a_lb[...], axis=0, keepdims=True)
            gcw_ref[0:1, :] = jnp.sum(a_c0[...], axis=0, keepdims=True)
            gcw_ref[1:2, :] = jnp.sum(a_c1[...], axis=0, keepdims=True)
            gcw_ref[2:3, :] = jnp.sum(a_c2[...], axis=0, keepdims=True)
            gcb_ref[...] = jnp.sum(a_cb[...], axis=0, keepdims=True)

    def rev(i):
        return nt - 1 - i

    row = pl.BlockSpec((tm, d), lambda g, i: (rev(i), 0))
    wrow = pl.BlockSpec((hd, d), lambda g, i: (g, 0))
    vec = pl.BlockSpec((1, hd), lambda g, i: (0, g))
    acc8 = pltpu.VMEM((SUBLANES, hd), F32)
    return pl.pallas_call(
        body, name="mixer_bwd", grid=(HEADS, nt),
        in_specs=[row, row, wrow, wrow,
                  pl.BlockSpec((N_SAVE, tm, hd), lambda g, i: (0, rev(i), g)),
                  pl.BlockSpec((2, tm, hd), lambda g, i: (0, rev(i), g)),
                  vec, vec,
                  pl.BlockSpec((1, CHUNK, CHUNK), lambda g, i: (g, 0, 0)),
                  pl.BlockSpec((1, CHUNK, 1), lambda g, i: (g, 0, 0)),
                  pl.BlockSpec((3, hd), lambda g, i: (0, g))],
        out_specs=[pl.BlockSpec((N_IN, tm, hd), lambda g, i: (0, rev(i), g)),
                   pl.BlockSpec((1, CHUNK, CHUNK), lambda g, i: (g, 0, 0)),
                   pl.BlockSpec((1, CHUNK, 1), lambda g, i: (g, 0, 0)),
                   vec, vec,
                   pl.BlockSpec((3, hd), lambda g, i: (0, g)),
                   vec],
        out_shape=[jax.ShapeDtypeStruct((N_IN, t_len, d), BF16),
                   jax.ShapeDtypeStruct((HEADS, CHUNK, CHUNK), F32),
                   jax.ShapeDtypeStruct((HEADS, CHUNK, 1), F32),
                   jax.ShapeDtypeStruct((1, d), F32), jax.ShapeDtypeStruct((1, d), F32),
                   jax.ShapeDtypeStruct((3, d), F32), jax.ShapeDtypeStruct((1, d), F32)],
        scratch_shapes=[pltpu.VMEM((tm + SUBLANES, hd), F32), acc8,
                        pltpu.VMEM((CHUNK, CHUNK), F32), pltpu.VMEM((CHUNK, hd), F32),
                        acc8, acc8, acc8, acc8, acc8, acc8],
        compiler_params=_cparams(("parallel", "arbitrary")),
    )(da, db, w_oa, w_ob, saved, dgate, ln_v_g, ln_v_b, w_s, b_s3, conv_w)


def _grad_w_in(xb, dp):
    t_len, d = xb.shape
    tm, tn, tk = _tile(d, 1024), _tile(d, 1024), _tile(t_len, 512)
    nj = d // tn
    nk = t_len // tk

    def body(a_ref, b_ref, o_ref, acc):
        k = pl.program_id(2)

        @pl.when(k == 0)
        def _():
            acc[...] = jnp.zeros_like(acc)

        acc[...] += _dot_tn(a_ref[...], b_ref[...])

        @pl.when(k == nk - 1)
        def _():
            o_ref[...] = acc[...].astype(BF16)

    return pl.pallas_call(
        body, name="grad_w_in", grid=(d // tm, N_IN * nj, nk),
        in_specs=[pl.BlockSpec((tk, tm), lambda i, j, k: (k, i)),
                  pl.BlockSpec((None, tk, tn), lambda i, j, k: (j // nj, k, j % nj))],
        out_specs=pl.BlockSpec((tm, tn), lambda i, j, k: (i, j)),
        out_shape=jax.ShapeDtypeStruct((d, N_IN * d), BF16),
        scratch_shapes=[pltpu.VMEM((tm, tn), F32)],
        compiler_params=_cparams(("parallel", "parallel", "arbitrary")),
    )(xb, dp)


def _grad_x(dp, w_in, gx_direct):
    _, t_len, d = dp.shape
    tm, tn, tk = _tile(t_len, 1024), _tile(d, 1024), _tile(d, 512)
    nkb = d // tk
    nk = N_IN * nkb

    def body(a_ref, b_ref, r_ref, o_ref, acc):
        k = pl.program_id(2)

        @pl.when(k == 0)
        def _():
            acc[...] = r_ref[...]

        acc[...] += _dot_nt(a_ref[...], b_ref[...])

        @pl.when(k == nk - 1)
        def _():
            o_ref[...] = acc[...]

    return pl.pallas_call(
        body, name="grad_x", grid=(t_len // tm, d // tn, nk),
        in_specs=[pl.BlockSpec((None, tm, tk), lambda i, j, k: (k // nkb, i, k % nkb)),
                  pl.BlockSpec((tn, tk), lambda i, j, k: (j, k)),
                  pl.BlockSpec((tm, tn), lambda i, j, k: (i, j))],
        out_specs=pl.BlockSpec((tm, tn), lambda i, j, k: (i, j)),
        out_shape=jax.ShapeDtypeStruct((t_len, d), F32),
        scratch_shapes=[pltpu.VMEM((tm, tn), F32)],
        compiler_params=_cparams(("parallel", "parallel", "arbitrary")),
    )(dp, w_in, gx_direct)


def kernel(x, w_in, b_gate, ln_v_g, ln_v_b, w_s, b_s, conv_w, conv_b, w_oa, w_ob, w_out, ln_g, ln_b, loss_target, m_w_in, m_b_gate, m_ln_v_g, m_ln_v_b, m_w_s, m_b_s, m_conv_w, m_conv_b, m_w_oa, m_w_ob, m_w_out, m_ln_g, m_ln_b, v_w_in, v_b_gate, v_ln_v_g, v_ln_v_b, v_w_s, v_b_s, v_conv_w, v_conv_b, v_w_oa, v_w_ob, v_w_out, v_ln_g, v_ln_b):
    _, t_len, d = x.shape
    assert d % (HEADS * 128) == 0 and t_len % CHUNK == 0 and w_in.shape[2] * N_DEV == N_IN * d
    x2 = x[0]
    tgt2 = loss_target[0]
    c_arr = lax.axis_index("c").astype(jnp.int32).reshape(1)
    chip_arr = (2 * lax.axis_index("x") + lax.axis_index("y")).astype(jnp.int32).reshape(1)
    dev = 4 * lax.axis_index("x") + 2 * lax.axis_index("y") + lax.axis_index("c")

    xb = _cast_bf16(x2, "cast_x")
    shards = [_cast_bf16(w_in[0], "cast_w_in"), _cast_bf16(w_oa[0], "cast_w_oa"),
              _cast_bf16(w_ob[0], "cast_w_ob"), _cast_bf16(w_out[0], "cast_w_out")]
    w_in_f, w_oa_f, w_ob_f, w_out_f = _all_gather(shards, ["col", "row", "row", "row"], "gather_weights", vmem=False)
    (conv_w_g,) = _all_gather([conv_w[0]], ["lead"], "gather_conv_w", vmem=True)
    conv_w_f = jnp.transpose(conv_w_g, (1, 0, 2)).reshape(3, d)
    w_s3 = w_s[0]
    b_s3 = b_s[0].reshape(HEADS, CHUNK, 1)

    saved, ya, yb = _mixer_fwd(xb, w_in_f, ln_v_g, ln_v_b, w_s3, b_s3, conv_w_f, conv_b)
    sa, sb, pa, pb, merged = _merge_fwd(xb, ya, yb, w_in_f, w_oa_f, w_ob_f, b_gate)
    dz, gx_direct, g_ln_g, g_ln_b, err2 = _out_ln_loss(merged, w_out_f, x2, tgt2, ln_g, ln_b)
    loss = lax.psum(0.5 * jnp.sum(err2) / d, ("x", "y", "c"))

    da, db, dgate, g_bga, g_bgb = _merge_bwd(dz, w_out_f, sa, sb, pa, pb)
    gw_out = _grad_w(merged, dz, "grad_w_out")
    gw_oa = _grad_w(ya, da, "grad_w_oa")
    gw_ob = _grad_w(yb, db, "grad_w_ob")
    dp, g_ws, g_bs, g_lvg, g_lvb, g_cw, g_cb = _mixer_bwd(
        da, db, w_oa_f, w_ob_f, saved, dgate, ln_v_g, ln_v_b, w_s3, b_s3, conv_w_f)
    gw_in = _grad_w_in(xb, dp)
    grad_x = _grad_x(dp, w_in_f, gx_direct)[None]

    grads = [gw_in, gw_oa, gw_ob, gw_out]
    kinds = ["col", "row", "row", "row"]
    lands = _pair_exchange(grads, kinds, "grad_pair_exchange")
    parts = [_pair_sum(g, l, k, c_arr, "grad_pair_sum_%d" % n) for n, (g, l, k) in enumerate(zip(grads, lands, kinds))]
    lands2 = _chip_exchange(parts, "grad_chip_exchange")
    big = []
    for n, (w, m, v) in enumerate([(w_in, m_w_in, v_w_in), (w_oa, m_w_oa, v_w_oa), (w_ob, m_w_ob, v_w_ob),
                                   (w_out, m_w_out, v_w_out)]):
        big.append([o[None] for o in _sum_adam(parts[n], lands2[n], w[0], m[0], v[0], chip_arr, "sum_adam_%d" % n)])
    (g_w_in, d_w_in, nm_w_in, nv_w_in), (g_w_oa, d_w_oa, nm_w_oa, nv_w_oa), \
        (g_w_ob, d_w_ob, nm_w_ob, nv_w_ob), (g_w_out, d_w_out, nm_w_out, nv_w_out) = big

    pieces = [jnp.concatenate([g_bga, g_bgb], axis=1), g_lvg, g_lvb, g_ws, g_bs, g_cw, g_cb, g_ln_g, g_ln_b]
    sizes = [p.size for p in pieces]
    packed = jnp.concatenate([p.reshape(-1, 128) for p in pieces], axis=0)
    (gathered,) = _all_gather([packed], ["lead"], "gather_small_grads", vmem=True)
    total = _small_sum(gathered, "sum_small_grads")
    offs = [0]
    for s in sizes:
        offs.append(offs[-1] + s // 128)
    unpacked = [total[offs[n]:offs[n + 1]] for n in range(len(pieces))]
    g_b_gate = unpacked[0].reshape(b_gate.shape)
    g_ln_v_g = unpacked[1].reshape(ln_v_g.shape)
    g_ln_v_b = unpacked[2].reshape(ln_v_b.shape)
    g_w_s = unpacked[3].reshape(w_s.shape)
    g_b_s = unpacked[4].reshape(b_s.shape)
    g_conv_w = lax.dynamic_slice_in_dim(unpacked[5].reshape(3, d), dev * (d // N_DEV), d // N_DEV, axis=1)[None]
    g_conv_b = unpacked[6].reshape(conv_b.shape)
    g_ln_g2 = unpacked[7].reshape(ln_g.shape)
    g_ln_b2 = unpacked[8].reshape(ln_b.shape)

    small_w = [b_gate, ln_v_g, ln_v_b, w_s, b_s, conv_w, conv_b, ln_g, ln_b]
    small_g = [g_b_gate, g_ln_v_g, g_ln_v_b, g_w_s, g_b_s, g_conv_w, g_conv_b, g_ln_g2, g_ln_b2]
    small_m = [m_b_gate, m_ln_v_g, m_ln_v_b, m_w_s, m_b_s, m_conv_w, m_conv_b, m_ln_g, m_ln_b]
    small_v = [v_b_gate, v_ln_v_g, v_ln_v_b, v_w_s, v_b_s, v_conv_w, v_conv_b, v_ln_g, v_ln_b]

    def flat(a):
        return a.reshape(-1, a.shape[-1])

    res = _small_adam([flat(a) for a in small_w], [flat(a) for a in small_g], [flat(a) for a in small_m],
                      [flat(a) for a in small_v], "adam_small")
    ns = len(small_w)
    d_s = [res[n].reshape(small_w[n].shape) for n in range(ns)]
    nm_s = [res[ns + n].reshape(small_w[n].shape) for n in range(ns)]
    nv_s = [res[2 * ns + n].reshape(small_w[n].shape) for n in range(ns)]

    def ordered(first, small, oa, ob, out):
        return [first] + small[:7] + [oa, ob, out] + small[7:]

    return (loss, grad_x,
            *ordered(g_w_in, small_g, g_w_oa, g_w_ob, g_w_out),
            *ordered(d_w_in, d_s, d_w_oa, d_w_ob, d_w_out),
            *ordered(nm_w_in, nm_s, nm_w_oa, nm_w_ob, nm_w_out),
            *ordered(nv_w_in, nv_s, nv_w_oa, nv_w_ob, nv_w_out))
```

```python
import functools

import jax
import jax.numpy as jnp
from jax import lax
from jax.experimental import pallas as pl
from jax.experimental.pallas import tpu as pltpu

F32 = jnp.float32
BF16 = jnp.bfloat16
MESH = pl.DeviceIdType.MESH

N_DEV = 8
N_CHIP = 4
HEADS = 8
CHUNK = 128
N_IN = 9
N_MIX = 7
N_SAVE = 8
LN_EPS = 1e-5
DN_ALPHA = 2.0 ** 0.25
ADAM_LR = 0.001
ADAM_B1 = 0.9
ADAM_B2 = 0.999
ADAM_EPS = 1e-08
ADAM_WD = 0.01
ADAM_STEP = 10
GELU_C0 = 0.7978845608028654
GELU_C1 = 0.044715
SUBLANES = 8
VMEM_LIMIT = 56 << 20


def _cparams(sem):
    return pltpu.CompilerParams(dimension_semantics=sem, vmem_limit_bytes=VMEM_LIMIT)


def _tile(n, want):
    t = min(n, want)
    while n % t:
        t //= 2
    return t


def _gelu(u):
    t = jnp.tanh(GELU_C0 * (u + GELU_C1 * u * u * u))
    return 0.5 * u * (1.0 + t), t


def _gelu_grad(u, t):
    return 0.5 * (1.0 + t) + 0.5 * u * (1.0 - t * t) * (GELU_C0 * (1.0 + 3.0 * GELU_C1 * u * u))


def _silu(z):
    s = jax.nn.sigmoid(z)
    return z * s, s


def _silu_grad(z, s):
    return s * (1.0 + z * (1.0 - s))


def _fold8(a):
    return jnp.sum(a.reshape(a.shape[0] // SUBLANES, SUBLANES, a.shape[1]), axis=0)


def _dot(a, b):
    return jnp.dot(a, b, preferred_element_type=F32)


def _dot_nt(a, b):
    return lax.dot_general(a, b, (((1,), (1,)), ((), ())), preferred_element_type=F32)


def _dot_tn(a, b):
    return lax.dot_general(a, b, (((0,), (0,)), ((), ())), preferred_element_type=F32)


def _tril_mask():
    r = lax.broadcasted_iota(jnp.int32, (CHUNK, CHUNK), 0)
    c = lax.broadcasted_iota(jnp.int32, (CHUNK, CHUNK), 1)
    return c <= r


def _cast_bf16(a, name):
    rows, cols = a.shape
    rb = _tile(rows, 256)

    def body(a_ref, o_ref):
        o_ref[...] = a_ref[...].astype(BF16)

    return pl.pallas_call(
        body, name=name, grid=(rows // rb,),
        in_specs=[pl.BlockSpec((rb, cols), lambda i: (i, 0))],
        out_specs=pl.BlockSpec((rb, cols), lambda i: (i, 0)),
        out_shape=jax.ShapeDtypeStruct((rows, cols), BF16),
        compiler_params=_cparams(("parallel",)),
    )(a)


def _shard_ref(full, kind, s, n):
    if kind == "col":
        return full.at[:, pl.ds(pl.multiple_of(s * n, 128), n)]
    if kind == "row":
        return full.at[pl.ds(pl.multiple_of(s * n, SUBLANES), n), :]
    return full.at[s]


def _all_gather(shards, kinds, name, vmem):
    nt = len(shards)
    out_shapes = []
    for a, kind in zip(shards, kinds):
        if kind == "col":
            out_shapes.append(jax.ShapeDtypeStruct((a.shape[0], N_DEV * a.shape[1]), a.dtype))
        elif kind == "row":
            out_shapes.append(jax.ShapeDtypeStruct((N_DEV * a.shape[0], a.shape[1]), a.dtype))
        else:
            out_shapes.append(jax.ShapeDtypeStruct((N_DEV,) + a.shape, a.dtype))

    def body(*refs):
        srcs, fulls = refs[:nt], refs[nt:2 * nt]
        send_sems, recv_sems, local_sems = refs[2 * nt:]
        x, y, c = lax.axis_index("x"), lax.axis_index("y"), lax.axis_index("c")
        sibling = (x, y, 1 - c)
        chips = [(1 - x, y), (x, 1 - y), (1 - x, 1 - y)]

        def dev(px, py, pc):
            return 4 * px + 2 * py + pc

        def region(t, s):
            a, kind = shards[t], kinds[t]
            n = a.shape[1] if kind == "col" else a.shape[0]
            return _shard_ref(fulls[t], kind, s, n)

        def copy(t, k, block, to, own=False):
            return pltpu.make_async_remote_copy(
                src_ref=srcs[t] if own else region(t, block), dst_ref=region(t, block),
                send_sem=send_sems.at[7 * t + k], recv_sem=recv_sems.at[7 * t + k],
                device_id=to, device_id_type=MESH)

        me = dev(x, y, c)
        started = []
        for t in range(nt):
            mine = pltpu.make_async_copy(srcs[t], region(t, me), local_sems.at[t])
            mine.start()
            started.append(mine)
        first = []
        for t in range(nt):
            first.append(copy(t, 0, me, sibling, own=True))
            for j, chip in enumerate(chips):
                first.append(copy(t, 1 + j, me, (*chip, c), own=True))
        for cp in first:
            cp.start()
        passed = []
        for t in range(nt):
            for j, chip in enumerate(chips):
                blk = dev(*chip, c)
                copy(t, 1 + j, blk, sibling).wait_recv()
                fwd = copy(t, 4 + j, blk, sibling)
                fwd.start()
                passed.append(fwd)
        for t in range(nt):
            copy(t, 0, dev(x, y, 1 - c), sibling).wait_recv()
            for j, chip in enumerate(chips):
                copy(t, 4 + j, dev(*chip, 1 - c), sibling).wait_recv()
        for cp in first + passed:
            cp.wait_send()
        for mine in started:
            mine.wait()

    space = pltpu.VMEM if vmem else pl.ANY
    return pl.pallas_call(
        body, name=name,
        in_specs=[pl.BlockSpec(memory_space=space)] * nt,
        out_specs=[pl.BlockSpec(memory_space=space)] * nt,
        out_shape=out_shapes,
        scratch_shapes=[pltpu.SemaphoreType.DMA((7 * nt,)), pltpu.SemaphoreType.DMA((7 * nt,)),
                        pltpu.SemaphoreType.DMA((nt,))],
        compiler_params=pltpu.CompilerParams(vmem_limit_bytes=VMEM_LIMIT, has_side_effects=True),
    )(*shards)


def _pair_exchange(grads, kinds, name):
    nt = len(grads)
    shard_shapes = []
    for g, kind in zip(grads, kinds):
        shard_shapes.append((g.shape[0], g.shape[1] // N_DEV) if kind == "col" else (g.shape[0] // N_DEV, g.shape[1]))

    def body(*refs):
        srcs, lands = refs[:nt], refs[nt:2 * nt]
        send_sems, recv_sems = refs[2 * nt:]
        x, y, c = lax.axis_index("x"), lax.axis_index("y"), lax.axis_index("c")
        copies = []
        for t in range(nt):
            n = shard_shapes[t][1] if kinds[t] == "col" else shard_shapes[t][0]
            for k in range(N_CHIP):
                cp = pltpu.make_async_remote_copy(
                    src_ref=_shard_ref(srcs[t], kinds[t], 2 * k + 1 - c, n), dst_ref=lands[t].at[k],
                    send_sem=send_sems.at[N_CHIP * t + k], recv_sem=recv_sems.at[N_CHIP * t + k],
                    device_id=(x, y, 1 - c), device_id_type=MESH)
                cp.start()
                copies.append(cp)
        for cp in copies:
            cp.wait()

    return pl.pallas_call(
        body, name=name,
        in_specs=[pl.BlockSpec(memory_space=pl.ANY)] * nt,
        out_specs=[pl.BlockSpec(memory_space=pl.ANY)] * nt,
        out_shape=[jax.ShapeDtypeStruct((N_CHIP,) + s, g.dtype) for s, g in zip(shard_shapes, grads)],
        scratch_shapes=[pltpu.SemaphoreType.DMA((N_CHIP * nt,)), pltpu.SemaphoreType.DMA((N_CHIP * nt,))],
        compiler_params=pltpu.CompilerParams(has_side_effects=True),
    )(*grads)


def _pair_sum(grad, land, kind, c_arr, name):
    _, r, w = land.shape
    rb = _tile(r, 256)
    nrb = r // rb

    def body(c_ref, g_ref, l_ref, o_ref):
        o_ref[...] = (g_ref[...].astype(F32) + l_ref[...].astype(F32)).astype(o_ref.dtype)

    if kind == "col":
        g_spec = pl.BlockSpec((rb, w), lambda k, i, c: (i, 2 * k + c[0]))
    else:
        g_spec = pl.BlockSpec((rb, w), lambda k, i, c: ((2 * k + c[0]) * nrb + i, 0))
    return pl.pallas_call(
        body, name=name,
        grid_spec=pltpu.PrefetchScalarGridSpec(
            num_scalar_prefetch=1, grid=(N_CHIP, nrb),
            in_specs=[g_spec, pl.BlockSpec((None, rb, w), lambda k, i, c: (k, i, 0))],
            out_specs=pl.BlockSpec((None, rb, w), lambda k, i, c: (k, i, 0))),
        out_shape=jax.ShapeDtypeStruct(land.shape, BF16),
        compiler_params=_cparams(("parallel", "parallel")),
    )(c_arr, grad, land)


def _adam(w, g, m, v):
    m = ADAM_B1 * m + (1.0 - ADAM_B1) * g
    v = ADAM_B2 * v + (1.0 - ADAM_B2) * jnp.square(g)
    m_hat = m / (1.0 - ADAM_B1 ** ADAM_STEP)
    v_hat = v / (1.0 - ADAM_B2 ** ADAM_STEP)
    delta = -ADAM_LR * (m_hat / (jnp.sqrt(v_hat) + ADAM_EPS) + ADAM_WD * w)
    return delta, m, v


def _sum_adam(part, land, w, m, v, chip_arr, name):
    r, wd = w.shape
    rb = _tile(r, 128)

    def body(k_ref, own, r1, r2, r3, w_ref, m_ref, v_ref, g_out, d_out, m_out, v_out):
        g = own[...].astype(F32) + r1[...].astype(F32) + r2[...].astype(F32) + r3[...].astype(F32)
        d, mn, vn = _adam(w_ref[...], g, m_ref[...], v_ref[...])
        g_out[...] = g
        d_out[...] = d
        m_out[...] = mn
        v_out[...] = vn

    def slot(off):
        return pl.BlockSpec((None, rb, wd), lambda i, k: ((k[0] + off) % N_CHIP, i, 0))

    plain = pl.BlockSpec((rb, wd), lambda i, k: (i, 0))
    return pl.pallas_call(
        body, name=name,
        grid_spec=pltpu.PrefetchScalarGridSpec(
            num_scalar_prefetch=1, grid=(r // rb,),
            in_specs=[slot(0), slot(1), slot(2), slot(3), plain, plain, plain],
            out_specs=[plain] * 4),
        out_shape=[jax.ShapeDtypeStruct(w.shape, F32)] * 4,
        compiler_params=_cparams(("parallel",)),
    )(chip_arr, part, land, land, land, w, m, v)


def _small_sum(gathered, name):
    _, r, w = gathered.shape

    def body(g_ref, o_ref):
        acc = g_ref[0]
        for d in range(1, N_DEV):
            acc = acc + g_ref[d]
        o_ref[...] = acc

    return pl.pallas_call(body, name=name, out_shape=jax.ShapeDtypeStruct((r, w), F32))(gathered)


def _small_adam(ws, gs, ms, vs, name):
    n = len(ws)

    def body(*refs):
        ins, outs = refs[:4 * n], refs[4 * n:]
        for t in range(n):
            d, mn, vn = _adam(ins[t][...], ins[n + t][...], ins[2 * n + t][...], ins[3 * n + t][...])
            outs[t][...] = d
            outs[n + t][...] = mn
            outs[2 * n + t][...] = vn

    shapes = [jax.ShapeDtypeStruct(w.shape, F32) for w in ws]
    return pl.pallas_call(body, name=name, out_shape=shapes * 3)(*ws, *gs, *ms, *vs)


def _mixer_fwd(xb, shards, ln_v_g, ln_v_b, w_s, b_s3, conv_w, conv_b):
    t_len, d = xb.shape
    hd = d // HEADS
    tm = _tile(t_len, 512)
    nt = t_len // tm
    assert nt >= 2
    w8 = shards[0].shape[1]
    r8 = shards[1].shape[0]
    bps = w8 // hd
    nq = N_IN * HEADS
    n_blocks = nq + 3 * N_DEV
    groups = [[("in", HEADS * b + k) for b in range(N_MIX)] for k in range(HEADS)]
    groups.append([("in", q) for q in range(N_MIX * HEADS, nq)] + [(t, s) for t in (1, 2, 3) for s in range(N_DEV)])

    def body(x_ref, sh_in, sh_oa, sh_ob, sh_out, lng, lnb, ws_ref, bs_ref, cw_ref, cb_ref,
             save_ref, ya_ref, yb_ref, f_in, f_oa, f_ob, f_out,
             hbuf, wbuf, recv_sems, own_sems, fwd_sems, local_sems, load_sems):
        g = pl.program_id(0)
        i = pl.program_id(1)
        x, y, c = lax.axis_index("x"), lax.axis_index("y"), lax.axis_index("c")
        sibling = (x, y, 1 - c)
        chips = [(1 - x, y), (x, 1 - y), (1 - x, 1 - y)]
        shard_refs = (sh_in, sh_oa, sh_ob, sh_out)
        fulls = (f_in, f_oa, f_ob, f_out)

        def tensor(blk):
            return 0 if blk[0] == "in" else blk[0]

        def owner(blk):
            s = blk[1] // bps if blk[0] == "in" else blk[1]
            return s // 4, (s // 2) % 2, s % 2

        def bid(blk):
            return blk[1] if blk[0] == "in" else nq + (blk[0] - 1) * N_DEV + blk[1]

        def region(blk):
            if blk[0] == "in":
                return f_in.at[:, pl.ds(blk[1] * hd, hd)]
            return fulls[blk[0]].at[pl.ds(blk[1] * r8, r8), :]

        def own_src(blk):
            if blk[0] == "in":
                return sh_in.at[:, pl.ds((blk[1] % bps) * hd, hd)]
            return shard_refs[blk[0]]

        def rcopy(blk, to, send_sem, own):
            return pltpu.make_async_remote_copy(
                src_ref=own_src(blk) if own else region(blk), dst_ref=region(blk),
                send_sem=send_sem, recv_sem=recv_sems.at[bid(blk)], device_id=to, device_id_type=MESH)

        def send_own(blk):
            ox, oy, oc = owner(blk)

            @pl.when((x == ox) & (y == oy) & (c == oc))
            def _():
                rcopy(blk, sibling, own_sems.at[tensor(blk)], True).start()
                for chip in chips:
                    rcopy(blk, (*chip, c), own_sems.at[tensor(blk)], True).start()

        def pass_on(blk):
            ox, oy, oc = owner(blk)

            @pl.when(((x != ox) | (y != oy)) & (c == oc))
            def _():
                rcopy(blk, sibling, fwd_sems.at[tensor(blk)], False).wait_recv()
                rcopy(blk, sibling, fwd_sems.at[tensor(blk)], False).start()

        def wait_from_sibling(blk):
            @pl.when(c != owner(blk)[2])
            def _():
                rcopy(blk, sibling, fwd_sems.at[tensor(blk)], False).wait_recv()

        def local_copy(t):
            n = w8 if t == 0 else r8
            me = 4 * x + 2 * y + c
            return pltpu.make_async_copy(shard_refs[t], _shard_ref(fulls[t], "col" if t == 0 else "row", me, n),
                                         local_sems.at[t])

        first = (g == 0) & (i == 0)

        @pl.when(first)
        def _():
            for t in range(4):
                local_copy(t).start()
            for grp in groups:
                for blk in grp:
                    send_own(blk)
            for blk in groups[0]:
                pass_on(blk)
            for t in range(4):
                local_copy(t).wait()

        @pl.when(i == nt - 2)
        def _():
            for k in range(HEADS - 1):
                @pl.when(g == k)
                def _(k=k):
                    for blk in groups[k + 1]:
                        pass_on(blk)
                    if k == HEADS - 2:
                        for blk in groups[HEADS]:
                            pass_on(blk)

        @pl.when(i == 0)
        def _():
            for k in range(HEADS):
                @pl.when(g == k)
                def _(k=k):
                    for blk in groups[k]:
                        wait_from_sibling(blk)
                    loads = [pltpu.make_async_copy(region(blk), wbuf.at[b], load_sems.at[b])
                             for b, blk in enumerate(groups[k])]
                    for cp in loads:
                        cp.start()
                    for cp in loads:
                        cp.wait()

        xt = x_ref[...]
        u = _dot(xt, wbuf[0])
        v = _dot(xt, wbuf[1])
        z = _dot(xt, wbuf[2])
        save_ref[0] = u.astype(BF16)
        save_ref[1] = v.astype(BF16)
        save_ref[2] = z.astype(BF16)
        gu, _ = _gelu(u)
        gv, _ = _gelu(v)
        mu = jnp.mean(gv, axis=-1, keepdims=True)
        dv = gv - mu
        var = jnp.mean(dv * dv, axis=-1, keepdims=True)
        vn = (dv * lax.rsqrt(var + LN_EPS) * lng[...] + lnb[...]).astype(BF16)
        sz, _ = _silu(z)
        gate = gu * sz
        wm = jnp.where(_tril_mask(), ws_ref[0], 0.0).astype(BF16)
        bs = bs_ref[0]
        for ck in range(tm // CHUNK):
            rows = slice(ck * CHUNK, (ck + 1) * CHUNK)
            mixed = _dot(wm, vn[rows]) + bs
            ya_ref[rows, :] = (gate[rows] * mixed).astype(BF16)

        xbv = _dot(xt, wbuf[3])
        cbv = _dot(xt, wbuf[4])
        bbv = _dot(xt, wbuf[5])
        zbv = _dot(xt, wbuf[6])
        save_ref[3] = xbv.astype(BF16)
        save_ref[4] = cbv.astype(BF16)
        save_ref[5] = bbv.astype(BF16)
        save_ref[6] = zbv.astype(BF16)
        h = cbv * xbv

        @pl.when(i == 0)
        def _():
            hbuf[0:SUBLANES, :] = jnp.zeros((SUBLANES, hd), F32)

        hbuf[SUBLANES:SUBLANES + tm, :] = h
        h1 = hbuf[SUBLANES - 1:SUBLANES - 1 + tm, :]
        h2 = hbuf[SUBLANES - 2:SUBLANES - 2 + tm, :]
        conv = cb_ref[...] + cw_ref[0:1, :] * h2 + cw_ref[1:2, :] * h1 + cw_ref[2:3, :] * h
        hbuf[0:SUBLANES, :] = h[tm - SUBLANES:tm, :]
        save_ref[7] = conv.astype(BF16)
        szb, _ = _silu(zbv)
        yb_ref[...] = (bbv * conv * szb).astype(BF16)

        @pl.when((g == HEADS - 1) & (i == nt - 1))
        def _():
            for blk in groups[HEADS]:
                wait_from_sibling(blk)
            for t in range(4):
                if t == 0:
                    own_all, fwd_all = f_in.at[:, pl.ds(0, 4 * w8)], f_in.at[:, pl.ds(0, 3 * w8)]
                else:
                    own_all, fwd_all = fulls[t].at[pl.ds(0, 4 * r8), :], fulls[t].at[pl.ds(0, 3 * r8), :]
                for ref, sem in ((own_all, own_sems.at[t]), (fwd_all, fwd_sems.at[t])):
                    pltpu.make_async_remote_copy(src_ref=ref, dst_ref=ref, send_sem=sem, recv_sem=sem,
                                                 device_id=sibling, device_id_type=MESH).wait_send()

    vec = pl.BlockSpec((1, hd), lambda g, i: (0, g))
    hbm = pl.BlockSpec(memory_space=pl.ANY)
    return pl.pallas_call(
        body, name="mixer_fwd", grid=(HEADS, nt),
        in_specs=[pl.BlockSpec((tm, d), lambda g, i: (i, 0)), hbm, hbm, hbm, hbm,
                  vec, vec,
                  pl.BlockSpec((1, CHUNK, CHUNK), lambda g, i: (g, 0, 0)),
                  pl.BlockSpec((1, CHUNK, 1), lambda g, i: (g, 0, 0)),
                  pl.BlockSpec((3, hd), lambda g, i: (0, g)),
                  vec],
        out_specs=[pl.BlockSpec((N_SAVE, tm, hd), lambda g, i: (0, i, g)),
                   pl.BlockSpec((tm, hd), lambda g, i: (i, g)),
                   pl.BlockSpec((tm, hd), lambda g, i: (i, g)),
                   hbm, hbm, hbm, hbm],
        out_shape=[jax.ShapeDtypeStruct((N_SAVE, t_len, d), BF16),
                   jax.ShapeDtypeStruct((t_len, d), BF16),
                   jax.ShapeDtypeStruct((t_len, d), BF16),
                   jax.ShapeDtypeStruct((d, N_DEV * w8), BF16),
                   jax.ShapeDtypeStruct((d, d), BF16), jax.ShapeDtypeStruct((d, d), BF16),
                   jax.ShapeDtypeStruct((d, d), BF16)],
        scratch_shapes=[pltpu.VMEM((SUBLANES + tm, hd), F32), pltpu.VMEM((N_MIX, d, hd), BF16),
                        pltpu.SemaphoreType.DMA((n_blocks,)), pltpu.SemaphoreType.DMA((4,)),
                        pltpu.SemaphoreType.DMA((4,)), pltpu.SemaphoreType.DMA((4,)),
                        pltpu.SemaphoreType.DMA((N_MIX,))],
        compiler_params=_cparams(("arbitrary", "arbitrary")),
    )(xb, *shards, ln_v_g, ln_v_b, w_s, b_s3, conv_w, conv_b)


def _merge_fwd(xb, ya, yb, w_in, w_oa, w_ob, b_gate):
    t_len, d = xb.shape
    tm = _tile(t_len, 512)
    tn = _tile(d, 512)
    nj = d // tn

    def body(x_ref, ya_ref, yb_ref, wga, wgb, woa, wob, bga, bgb, sa_ref, sb_ref, pa_ref, pb_ref, mg_ref):
        xt = x_ref[...]
        sa = jax.nn.sigmoid(_dot(xt, wga[...]) + bga[...])
        sb = jax.nn.sigmoid(_dot(xt, wgb[...]) + bgb[...])
        pa = _dot(ya_ref[...], woa[...])
        pb = _dot(yb_ref[...], wob[...])
        sa_ref[...] = sa.astype(BF16)
        sb_ref[...] = sb.astype(BF16)
        pa_ref[...] = pa.astype(BF16)
        pb_ref[...] = pb.astype(BF16)
        mg_ref[...] = (sa * pa + sb * pb).astype(BF16)

    row = pl.BlockSpec((tm, d), lambda j, i: (i, 0))
    out = pl.BlockSpec((tm, tn), lambda j, i: (i, j))
    return pl.pallas_call(
        body, name="merge_fwd", grid=(nj, t_len // tm),
        in_specs=[row, row, row,
                  pl.BlockSpec((d, tn), lambda j, i: (0, 7 * nj + j)),
                  pl.BlockSpec((d, tn), lambda j, i: (0, 8 * nj + j)),
                  pl.BlockSpec((d, tn), lambda j, i: (0, j)),
                  pl.BlockSpec((d, tn), lambda j, i: (0, j)),
                  pl.BlockSpec((1, tn), lambda j, i: (0, j)),
                  pl.BlockSpec((1, tn), lambda j, i: (0, nj + j))],
        out_specs=[out] * 5,
        out_shape=[jax.ShapeDtypeStruct((t_len, d), BF16)] * 5,
        compiler_params=_cparams(("parallel", "arbitrary")),
    )(xb, ya, yb, w_in, w_in, w_oa, w_ob, b_gate, b_gate)


def _out_ln_loss(merged, w_out, x, target, ln_g, ln_b):
    t_len, d = x.shape
    tm = _tile(t_len, 256)
    nt = t_len // tm

    def body(mg_ref, w_ref, x_ref, t_ref, g_ref, b_ref, dz_ref, gx_ref, glg_ref, glb_ref, ls_ref, a_g, a_b, a_l):
        i = pl.program_id(0)

        @pl.when(i == 0)
        def _():
            a_g[...] = jnp.zeros_like(a_g)
            a_b[...] = jnp.zeros_like(a_b)
            a_l[...] = jnp.zeros_like(a_l)

        zres = DN_ALPHA * x_ref[...] + _dot(mg_ref[...], w_ref[...])
        mu = jnp.mean(zres, axis=-1, keepdims=True)
        dc = zres - mu
        var = jnp.mean(dc * dc, axis=-1, keepdims=True)
        rstd = lax.rsqrt(var + LN_EPS)
        xhat = dc * rstd
        g = g_ref[...]
        err = xhat * g + b_ref[...] - t_ref[...]
        dy = err * (1.0 / d)
        a_l[...] += _fold8(err * err)
        a_g[...] += _fold8(dy * xhat)
        a_b[...] += _fold8(dy)
        dxh = dy * g
        m1 = jnp.mean(dxh, axis=-1, keepdims=True)
        m2 = jnp.mean(dxh * xhat, axis=-1, keepdims=True)
        dz = rstd * (dxh - m1 - xhat * m2)
        dz_ref[...] = dz.astype(BF16)
        gx_ref[...] = DN_ALPHA * dz

        @pl.when(i == nt - 1)
        def _():
            glg_ref[...] = jnp.sum(a_g[...], axis=0, keepdims=True)
            glb_ref[...] = jnp.sum(a_b[...], axis=0, keepdims=True)
            ls_ref[...] = jnp.sum(a_l[...], axis=0, keepdims=True)

    row = pl.BlockSpec((tm, d), lambda i: (i, 0))
    vec = pl.BlockSpec((1, d), lambda i: (0, 0))
    return pl.pallas_call(
        body, name="out_ln_loss", grid=(nt,),
        in_specs=[row, pl.BlockSpec((d, d), lambda i: (0, 0)), row, row, vec, vec],
        out_specs=[row, row, vec, vec, vec],
        out_shape=[jax.ShapeDtypeStruct((t_len, d), BF16), jax.ShapeDtypeStruct((t_len, d), F32)]
        + [jax.ShapeDtypeStruct((1, d), F32)] * 3,
        scratch_shapes=[pltpu.VMEM((SUBLANES, d), F32)] * 3,
        compiler_params=_cparams(("arbitrary",)),
    )(merged, w_out, x, target, ln_g, ln_b)


def _merge_bwd(dz, w_out, sa, sb, pa, pb):
    t_len, d = dz.shape
    tm = _tile(t_len, 512)
    tn = _tile(d, 512)
    nt = t_len // tm

    def body(dz_ref, w_ref, sa_ref, sb_ref, pa_ref, pb_ref, da_ref, db_ref, dg_ref, ga_ref, gb_ref, acc_a, acc_b):
        i = pl.program_id(1)

        @pl.when(i == 0)
        def _():
            acc_a[...] = jnp.zeros_like(acc_a)
            acc_b[...] = jnp.zeros_like(acc_b)

        dm = _dot_nt(dz_ref[...], w_ref[...])
        sa = sa_ref[...].astype(F32)
        sb = sb_ref[...].astype(F32)
        da = dm * sa
        db = dm * sb
        da_ref[...] = da.astype(BF16)
        db_ref[...] = db.astype(BF16)
        dga = da * pa_ref[...].astype(F32) * (1.0 - sa)
        dgb = db * pb_ref[...].astype(F32) * (1.0 - sb)
        dg_ref[0] = dga.astype(BF16)
        dg_ref[1] = dgb.astype(BF16)
        acc_a[...] += _fold8(dga)
        acc_b[...] += _fold8(dgb)

        @pl.when(i == nt - 1)
        def _():
            ga_ref[...] = jnp.sum(acc_a[...], axis=0, keepdims=True)
            gb_ref[...] = jnp.sum(acc_b[...], axis=0, keepdims=True)

    blk = pl.BlockSpec((tm, tn), lambda j, i: (i, j))
    vec = pl.BlockSpec((1, tn), lambda j, i: (0, j))
    return pl.pallas_call(
        body, name="merge_bwd", grid=(d // tn, nt),
        in_specs=[pl.BlockSpec((tm, d), lambda j, i: (i, 0)), pl.BlockSpec((tn, d), lambda j, i: (j, 0)),
                  blk, blk, blk, blk],
        out_specs=[blk, blk, pl.BlockSpec((2, tm, tn), lambda j, i: (0, i, j)), vec, vec],
        out_shape=[jax.ShapeDtypeStruct((t_len, d), BF16)] * 2 + [jax.ShapeDtypeStruct((2, t_len, d), BF16)]
        + [jax.ShapeDtypeStruct((1, d), F32)] * 2,
        scratch_shapes=[pltpu.VMEM((SUBLANES, tn), F32)] * 2,
        compiler_params=_cparams(("parallel", "arbitrary")),
    )(dz, w_out, sa, sb, pa, pb)


def _grad_w(a, b, name):
    t_len, m = a.shape
    n = b.shape[1]
    tm, tn, tk = _tile(m, 1024), _tile(n, 1024), _tile(t_len, 512)
    nk = t_len // tk

    def body(a_ref, b_ref, o_ref, acc):
        k = pl.program_id(2)

        @pl.when(k == 0)
        def _():
            acc[...] = jnp.zeros_like(acc)

        acc[...] += _dot_tn(a_ref[...], b_ref[...])

        @pl.when(k == nk - 1)
        def _():
            o_ref[...] = acc[...].astype(BF16)

    return pl.pallas_call(
        body, name=name, grid=(m // tm, n // tn, nk),
        in_specs=[pl.BlockSpec((tk, tm), lambda i, j, k: (k, i)), pl.BlockSpec((tk, tn), lambda i, j, k: (k, j))],
        out_specs=pl.BlockSpec((tm, tn), lambda i, j, k: (i, j)),
        out_shape=jax.ShapeDtypeStruct((m, n), BF16),
        scratch_shapes=[pltpu.VMEM((tm, tn), F32)],
        compiler_params=_cparams(("parallel", "parallel", "arbitrary")),
    )(a, b)


def _mixer_bwd(da, db, w_oa, w_ob, saved, dgate, ln_v_g, ln_v_b, w_s, b_s3, conv_w):
    t_len, d = da.shape
    hd = d // HEADS
    tm = _tile(t_len, 512)
    nt = t_len // tm

    def body(da_ref, db_ref, woa, wob, sv, dgt, lng, lnb, ws_ref, bs_ref, cw_ref,
             dp_ref, gws_ref, gbs_ref, glg_ref, glb_ref, gcw_ref, gcb_ref,
             dbuf, carry, a_ws, a_bs, a_lg, a_lb, a_c0, a_c1, a_c2, a_cb):
        i = pl.program_id(1)

        @pl.when(i == 0)
        def _():
            carry[...] = jnp.zeros_like(carry)
            for a in (a_ws, a_bs, a_lg, a_lb, a_c0, a_c1, a_c2, a_cb):
                a[...] = jnp.zeros_like(a)

        dya = _dot_nt(da_ref[...], woa[...])
        u = sv[0].astype(F32)
        v = sv[1].astype(F32)
        z = sv[2].astype(F32)
        gu, tu = _gelu(u)
        gv, tv = _gelu(v)
        mu = jnp.mean(gv, axis=-1, keepdims=True)
        dvc = gv - mu
        var = jnp.mean(dvc * dvc, axis=-1, keepdims=True)
        rstd = lax.rsqrt(var + LN_EPS)
        vhat = dvc * rstd
        g = lng[...]
        vn = (vhat * g + lnb[...]).astype(BF16)
        sz, s = _silu(z)
        t1 = dya * sz
        dmixed = t1 * gu
        dmixed_b = dmixed.astype(BF16)
        wm = jnp.where(_tril_mask(), ws_ref[0], 0.0).astype(BF16)
        bs = bs_ref[0]
        dvn_parts = []
        mixed_parts = []
        gws = a_ws[...]
        gbs = a_bs[...]
        for ck in range(tm // CHUNK):
            rows = slice(ck * CHUNK, (ck + 1) * CHUNK)
            mixed_parts.append(_dot(wm, vn[rows]) + bs)
            gws = gws + _dot_nt(dmixed_b[rows], vn[rows])
            gbs = gbs + dmixed[rows]
            dvn_parts.append(_dot_tn(wm, dmixed_b[rows]))
        a_ws[...] = gws
        a_bs[...] = gbs
        mixed = jnp.concatenate(mixed_parts, axis=0)
        dvn = jnp.concatenate(dvn_parts, axis=0)
        dp_ref[0] = (t1 * mixed * _gelu_grad(u, tu)).astype(BF16)
        dp_ref[2] = (dya * gu * mixed * _silu_grad(z, s)).astype(BF16)
        a_lg[...] += _fold8(dvn * vhat)
        a_lb[...] += _fold8(dvn)
        dvh = dvn * g
        m1 = jnp.mean(dvh, axis=-1, keepdims=True)
        m2 = jnp.mean(dvh * vhat, axis=-1, keepdims=True)
        dp_ref[1] = (rstd * (dvh - m1 - vhat * m2) * _gelu_grad(v, tv)).astype(BF16)

        dyb = _dot_nt(db_ref[...], wob[...])
        xbv = sv[3].astype(F32)
        cbv = sv[4].astype(F32)
        bbv = sv[5].astype(F32)
        zbv = sv[6].astype(F32)
        conv = sv[7].astype(F32)
        szb, sb = _silu(zbv)
        dp_ref[5] = (dyb * conv * szb).astype(BF16)
        dp_ref[6] = (dyb * bbv * conv * _silu_grad(zbv, sb)).astype(BF16)
        dconv = dyb * bbv * szb
        dbuf[0:tm, :] = dconv
        dbuf[tm:tm + SUBLANES, :] = carry[...]
        dc1 = dbuf[1:tm + 1, :]
        dc2 = dbuf[2:tm + 2, :]
        carry[...] = dconv[0:SUBLANES, :]
        h = cbv * xbv
        a_c2[...] += _fold8(dconv * h)
        a_c1[...] += _fold8(dc1 * h)
        a_c0[...] += _fold8(dc2 * h)
        a_cb[...] += _fold8(dconv)
        dh = cw_ref[2:3, :] * dconv + cw_ref[1:2, :] * dc1 + cw_ref[0:1, :] * dc2
        dp_ref[3] = (dh * cbv).astype(BF16)
        dp_ref[4] = (dh * xbv).astype(BF16)
        dp_ref[7] = dgt[0]
        dp_ref[8] = dgt[1]

        @pl.when(i == nt - 1)
        def _():
            gws_ref[0] = jnp.where(_tril_mask(), a_ws[...], 0.0)
            gbs_ref[0] = jnp.sum(a_bs[...], axis=1, keepdims=True)
            glg_ref[...] = jnp.sum(a_lg[...], axis=0, keepdims=True)
            glb_ref[...] = jnp.sum(a_lb[...], axis=0, keepdims=True)
            gcw_ref[0:1, :] = jnp.sum(a_c0[...], axis=0, keepdims=True)
            gcw_ref[1:2, :] = jnp.sum(a_c1[...], axis=0, keepdims=True)
            gcw_ref[2:3, :] = jnp.sum(a_c2[...], axis=0, keepdims=True)
            gcb_ref[...] = jnp.sum(a_cb[...], axis=0, keepdims=True)

    def rev(i):
        return nt - 1 - i

    row = pl.BlockSpec((tm, d), lambda g, i: (rev(i), 0))
    wrow = pl.BlockSpec((hd, d), lambda g, i: (g, 0))
    vec = pl.BlockSpec((1, hd), lambda g, i: (0, g))
    acc8 = pltpu.VMEM((SUBLANES, hd), F32)
    return pl.pallas_call(
        body, name="mixer_bwd", grid=(HEADS, nt),
        in_specs=[row, row, wrow, wrow,
                  pl.BlockSpec((N_SAVE, tm, hd), lambda g, i: (0, rev(i), g)),
                  pl.BlockSpec((2, tm, hd), lambda g, i: (0, rev(i), g)),
                  vec, vec,
                  pl.BlockSpec((1, CHUNK, CHUNK), lambda g, i: (g, 0, 0)),
                  pl.BlockSpec((1, CHUNK, 1), lambda g, i: (g, 0, 0)),
                  pl.BlockSpec((3, hd), lambda g, i: (0, g))],
        out_specs=[pl.BlockSpec((N_IN, tm, hd), lambda g, i: (0, rev(i), g)),
                   pl.BlockSpec((1, CHUNK, CHUNK), lambda g, i: (g, 0, 0)),
                   pl.BlockSpec((1, CHUNK, 1), lambda g, i: (g, 0, 0)),
                   vec, vec,
                   pl.BlockSpec((3, hd), lambda g, i: (0, g)),
                   vec],
        out_shape=[jax.ShapeDtypeStruct((N_IN, t_len, d), BF16),
                   jax.ShapeDtypeStruct((HEADS, CHUNK, CHUNK), F32),
                   jax.ShapeDtypeStruct((HEADS, CHUNK, 1), F32),
                   jax.ShapeDtypeStruct((1, d), F32), jax.ShapeDtypeStruct((1, d), F32),
                   jax.ShapeDtypeStruct((3, d), F32), jax.ShapeDtypeStruct((1, d), F32)],
        scratch_shapes=[pltpu.VMEM((tm + SUBLANES, hd), F32), acc8,
                        pltpu.VMEM((CHUNK, CHUNK), F32), pltpu.VMEM((CHUNK, hd), F32),
                        acc8, acc8, acc8, acc8, acc8, acc8],
        compiler_params=_cparams(("parallel", "arbitrary")),
    )(da, db, w_oa, w_ob, saved, dgate, ln_v_g, ln_v_b, w_s, b_s3, conv_w)


def _grad_w_in(xb, dp):
    t_len, d = xb.shape
    tm, tn, tk = _tile(d, 1024), _tile(d, 1024), _tile(t_len, 512)
    nj = d // tn
    nk = t_len // tk

    def body(a_ref, b_ref, o_ref, acc):
        k = pl.program_id(2)

        @pl.when(k == 0)
        def _():
            acc[...] = jnp.zeros_like(acc)

        acc[...] += _dot_tn(a_ref[...], b_ref[...])

        @pl.when(k == nk - 1)
        def _():
            o_ref[...] = acc[...].astype(BF16)

    return pl.pallas_call(
        body, name="grad_w_in", grid=(d // tm, N_IN * nj, nk),
        in_specs=[pl.BlockSpec((tk, tm), lambda i, j, k: (k, i)),
                  pl.BlockSpec((None, tk, tn), lambda i, j, k: (j // nj, k, j % nj))],
        out_specs=pl.BlockSpec((tm, tn), lambda i, j, k: (i, j)),
        out_shape=jax.ShapeDtypeStruct((d, N_IN * d), BF16),
        scratch_shapes=[pltpu.VMEM((tm, tn), F32)],
        compiler_params=_cparams(("parallel", "parallel", "arbitrary")),
    )(xb, dp)


def _grad_x(dp, w_in, gx_direct, parts, packed):
    _, t_len, d = dp.shape
    tm, tn, tk = _tile(t_len, 1024), _tile(d, 1024), _tile(d, 512)
    nkb = d // tk
    nk = N_IN * nkb
    ni, nj = t_len // tm, d // tn
    n_parts = len(parts)

    def body(a_ref, b_ref, r_ref, *rest):
        srcs = rest[:n_parts]
        pk_ref = rest[n_parts]
        o_ref = rest[n_parts + 1]
        lands = rest[n_parts + 2:2 * n_parts + 2]
        gath = rest[2 * n_parts + 2]
        acc, send_sems, recv_sems, pk_send, pk_recv, pk_local = rest[2 * n_parts + 3:]
        i, j, k = pl.program_id(0), pl.program_id(1), pl.program_id(2)
        x, y, c = lax.axis_index("x"), lax.axis_index("y"), lax.axis_index("c")
        my_chip = 2 * x + y
        me = 4 * x + 2 * y + c
        chips = [(1 - x, y), (x, 1 - y), (1 - x, 1 - y)]

        def part_copy(t, n):
            px, py = chips[n]
            return pltpu.make_async_remote_copy(
                src_ref=srcs[t].at[2 * px + py], dst_ref=lands[t].at[my_chip],
                send_sem=send_sems.at[3 * t + n], recv_sem=recv_sems.at[3 * t + n],
                device_id=(px, py, c), device_id_type=MESH)

        def part_landing(t, n):
            px, py = chips[n]
            return pltpu.make_async_remote_copy(
                src_ref=srcs[t].at[my_chip], dst_ref=lands[t].at[2 * px + py],
                send_sem=send_sems.at[3 * t + n], recv_sem=recv_sems.at[3 * t + n],
                device_id=(px, py, c), device_id_type=MESH)

        def pk_copy(s):
            return pltpu.make_async_remote_copy(
                src_ref=pk_ref, dst_ref=gath.at[me], send_sem=pk_send, recv_sem=pk_recv.at[me],
                device_id=(s // 4, (s // 2) % 2, s % 2), device_id_type=MESH)

        @pl.when((i == 0) & (j == 0) & (k == 0))
        def _():
            for t in range(n_parts):
                for n in range(3):
                    part_copy(t, n).start()
            pltpu.make_async_copy(pk_ref, gath.at[me], pk_local).start()
            for s in range(N_DEV):
                @pl.when(s != me)
                def _(s=s):
                    pk_copy(s).start()

        @pl.when(k == 0)
        def _():
            acc[...] = r_ref[...]

        acc[...] += _dot_nt(a_ref[...], b_ref[...])

        @pl.when(k == nk - 1)
        def _():
            o_ref[...] = acc[...]

        @pl.when((i == ni - 1) & (j == nj - 1) & (k == nk - 1))
        def _():
            for t in range(n_parts):
                for n in range(3):
                    part_landing(t, n).wait_recv()
            for t in range(n_parts):
                for n in range(3):
                    part_copy(t, n).wait_send()
            for s in range(N_DEV):
                @pl.when(s != me)
                def _(s=s):
                    pltpu.make_async_remote_copy(
                        src_ref=pk_ref, dst_ref=gath.at[s], send_sem=pk_send, recv_sem=pk_recv.at[s],
                        device_id=(s // 4, (s // 2) % 2, s % 2), device_id_type=MESH).wait_recv()
            seven = gath.at[pl.ds(0, N_DEV - 1)]
            pltpu.make_async_remote_copy(src_ref=seven, dst_ref=seven, send_sem=pk_send, recv_sem=pk_send,
                                         device_id=(x, y, 1 - c), device_id_type=MESH).wait_send()
            pltpu.make_async_copy(pk_ref, gath.at[me], pk_local).wait()

    hbm = pl.BlockSpec(memory_space=pl.ANY)
    outs = pl.pallas_call(
        body, name="grad_x", grid=(ni, nj, nk),
        in_specs=[pl.BlockSpec((None, tm, tk), lambda i, j, k: (k // nkb, i, k % nkb)),
                  pl.BlockSpec((tn, tk), lambda i, j, k: (j, k)),
                  pl.BlockSpec((tm, tn), lambda i, j, k: (i, j))] + [hbm] * (n_parts + 1),
        out_specs=[pl.BlockSpec((tm, tn), lambda i, j, k: (i, j))] + [hbm] * (n_parts + 1),
        out_shape=[jax.ShapeDtypeStruct((t_len, d), F32)] + [jax.ShapeDtypeStruct(p.shape, p.dtype) for p in parts]
        + [jax.ShapeDtypeStruct((N_DEV,) + packed.shape, packed.dtype)],
        scratch_shapes=[pltpu.VMEM((tm, tn), F32),
                        pltpu.SemaphoreType.DMA((3 * n_parts,)), pltpu.SemaphoreType.DMA((3 * n_parts,)),
                        pltpu.SemaphoreType.DMA(()), pltpu.SemaphoreType.DMA((N_DEV,)), pltpu.SemaphoreType.DMA(())],
        compiler_params=_cparams(("arbitrary", "arbitrary", "arbitrary")),
    )(dp, w_in, gx_direct, *parts, packed)
    return outs[0], list(outs[1:1 + n_parts]), outs[1 + n_parts]


def kernel(x, w_in, b_gate, ln_v_g, ln_v_b, w_s, b_s, conv_w, conv_b, w_oa, w_ob, w_out, ln_g, ln_b, loss_target, m_w_in, m_b_gate, m_ln_v_g, m_ln_v_b, m_w_s, m_b_s, m_conv_w, m_conv_b, m_w_oa, m_w_ob, m_w_out, m_ln_g, m_ln_b, v_w_in, v_b_gate, v_ln_v_g, v_ln_v_b, v_w_s, v_b_s, v_conv_w, v_conv_b, v_w_oa, v_w_ob, v_w_out, v_ln_g, v_ln_b):
    _, t_len, d = x.shape
    assert d % (HEADS * 128) == 0 and t_len % CHUNK == 0 and w_in.shape[2] * N_DEV == N_IN * d
    x2 = x[0]
    tgt2 = loss_target[0]
    c_arr = lax.axis_index("c").astype(jnp.int32).reshape(1)
    chip_arr = (2 * lax.axis_index("x") + lax.axis_index("y")).astype(jnp.int32).reshape(1)
    dev = 4 * lax.axis_index("x") + 2 * lax.axis_index("y") + lax.axis_index("c")

    xb = _cast_bf16(x2, "cast_x")
    shards = [_cast_bf16(w_in[0], "cast_w_in"), _cast_bf16(w_oa[0], "cast_w_oa"),
              _cast_bf16(w_ob[0], "cast_w_ob"), _cast_bf16(w_out[0], "cast_w_out")]
    (conv_w_g,) = _all_gather([conv_w[0]], ["lead"], "gather_conv_w", vmem=True)
    conv_w_f = jnp.transpose(conv_w_g, (1, 0, 2)).reshape(3, d)
    w_s3 = w_s[0]
    b_s3 = b_s[0].reshape(HEADS, CHUNK, 1)

    saved, ya, yb, w_in_f, w_oa_f, w_ob_f, w_out_f = _mixer_fwd(
        xb, shards, ln_v_g, ln_v_b, w_s3, b_s3, conv_w_f, conv_b)
    sa, sb, pa, pb, merged = _merge_fwd(xb, ya, yb, w_in_f, w_oa_f, w_ob_f, b_gate)
    dz, gx_direct, g_ln_g, g_ln_b, err2 = _out_ln_loss(merged, w_out_f, x2, tgt2, ln_g, ln_b)
    loss = lax.psum(0.5 * jnp.sum(err2) / d, ("x", "y", "c"))

    da, db, dgate, g_bga, g_bgb = _merge_bwd(dz, w_out_f, sa, sb, pa, pb)
    gw_out = _grad_w(merged, dz, "grad_w_out")
    gw_oa = _grad_w(ya, da, "grad_w_oa")
    gw_ob = _grad_w(yb, db, "grad_w_ob")
    dp, g_ws, g_bs, g_lvg, g_lvb, g_cw, g_cb = _mixer_bwd(
        da, db, w_oa_f, w_ob_f, saved, dgate, ln_v_g, ln_v_b, w_s3, b_s3, conv_w_f)
    gw_in = _grad_w_in(xb, dp)

    grads = [gw_in, gw_oa, gw_ob, gw_out]
    kinds = ["col", "row", "row", "row"]
    lands = _pair_exchange(grads, kinds, "grad_pair_exchange")
    parts = [_pair_sum(g, l, k, c_arr, "grad_pair_sum_%d" % n) for n, (g, l, k) in enumerate(zip(grads, lands, kinds))]
    pieces = [jnp.concatenate([g_bga, g_bgb], axis=1), g_lvg, g_lvb, g_ws, g_bs, g_cw, g_cb, g_ln_g, g_ln_b]
    sizes = [p.size for p in pieces]
    packed = jnp.concatenate([p.reshape(-1, 128) for p in pieces], axis=0)
    grad_x, lands2, gathered = _grad_x(dp, w_in_f, gx_direct, parts, packed)
    grad_x = grad_x[None]

    big = []
    for n, (w, m, v) in enumerate([(w_in, m_w_in, v_w_in), (w_oa, m_w_oa, v_w_oa), (w_ob, m_w_ob, v_w_ob),
                                   (w_out, m_w_out, v_w_out)]):
        big.append([o[None] for o in _sum_adam(parts[n], lands2[n], w[0], m[0], v[0], chip_arr, "sum_adam_%d" % n)])
    (g_w_in, d_w_in, nm_w_in, nv_w_in), (g_w_oa, d_w_oa, nm_w_oa, nv_w_oa), \
        (g_w_ob, d_w_ob, nm_w_ob, nv_w_ob), (g_w_out, d_w_out, nm_w_out, nv_w_out) = big

    total = _small_sum(gathered, "sum_small_grads")
    offs = [0]
    for s in sizes:
        offs.append(offs[-1] + s // 128)
    unpacked = [total[offs[n]:offs[n + 1]] for n in range(len(pieces))]
    g_b_gate = unpacked[0].reshape(b_gate.shape)
    g_ln_v_g = unpacked[1].reshape(ln_v_g.shape)
    g_ln_v_b = unpacked[2].reshape(ln_v_b.shape)
    g_w_s = unpacked[3].reshape(w_s.shape)
    g_b_s = unpacked[4].reshape(b_s.shape)
    g_conv_w = lax.dynamic_slice_in_dim(unpacked[5].reshape(3, d), dev * (d // N_DEV), d // N_DEV, axis=1)[None]
    g_conv_b = unpacked[6].reshape(conv_b.shape)
    g_ln_g2 = unpacked[7].reshape(ln_g.shape)
    g_ln_b2 = unpacked[8].reshape(ln_b.shape)

    small_w = [b_gate, ln_v_g, ln_v_b, w_s, b_s, conv_w, conv_b, ln_g, ln_b]
    small_g = [g_b_gate, g_ln_v_g, g_ln_v_b, g_w_s, g_b_s, g_conv_w, g_conv_b, g_ln_g2, g_ln_b2]
    small_m = [m_b_gate, m_ln_v_g, m_ln_v_b, m_w_s, m_b_s, m_conv_w, m_conv_b, m_ln_g, m_ln_b]
    small_v = [v_b_gate, v_ln_v_g, v_ln_v_b, v_w_s, v_b_s, v_conv_w, v_conv_b, v_ln_g, v_ln_b]

    def flat(a):
        return a.reshape(-1, a.shape[-1])

    res = _small_adam([flat(a) for a in small_w], [flat(a) for a in small_g], [flat(a) for a in small_m],
                      [flat(a) for a in small_v], "adam_small")
    ns = len(small_w)
    d_s = [res[n].reshape(small_w[n].shape) for n in range(ns)]
    nm_s = [res[ns + n].reshape(small_w[n].shape) for n in range(ns)]
    nv_s = [res[2 * ns + n].reshape(small_w[n].shape) for n in range(ns)]

    def ordered(first, small, oa, ob, out):
        return [first] + small[:7] + [oa, ob, out] + small[7:]

    return (loss, grad_x,
            *ordered(g_w_in, small_g, g_w_oa, g_w_ob, g_w_out),
            *ordered(d_w_in, d_s, d_w_oa, d_w_ob, d_w_out),
            *ordered(nm_w_in, nm_s, nm_w_oa, nm_w_ob, nm_w_out),
            *ordered(nv_w_in, nv_s, nv_w_oa, nv_w_ob, nv_w_out))
```

```python
import functools

import jax
import jax.numpy as jnp
from jax import lax
from jax.experimental import pallas as pl
from jax.experimental.pallas import tpu as pltpu

F32 = jnp.float32
BF16 = jnp.bfloat16
MESH = pl.DeviceIdType.MESH

N_DEV = 8
N_CHIP = 4
HEADS = 8
CHUNK = 128
N_IN = 9
N_MIX = 7
N_SAVE = 8
LN_EPS = 1e-5
DN_ALPHA = 2.0 ** 0.25
ADAM_LR = 0.001
ADAM_B1 = 0.9
ADAM_B2 = 0.999
ADAM_EPS = 1e-08
ADAM_WD = 0.01
ADAM_STEP = 10
GELU_C0 = 0.7978845608028654
GELU_C1 = 0.044715
SUBLANES = 8
FIRST_SENDS = 4
SENDS_PER_GROUP = 2
MM_TILE = 1024
MM_DEPTH = 2048
VMEM_LIMIT = 56 << 20


def _cparams(sem):
    return pltpu.CompilerParams(dimension_semantics=sem, vmem_limit_bytes=VMEM_LIMIT)


def _tile(n, want):
    t = min(n, want)
    while n % t:
        t //= 2
    return t


def _gelu(u):
    t = jnp.tanh(GELU_C0 * (u + GELU_C1 * u * u * u))
    return 0.5 * u * (1.0 + t), t


def _gelu_grad(u, t):
    return 0.5 * (1.0 + t) + 0.5 * u * (1.0 - t * t) * (GELU_C0 * (1.0 + 3.0 * GELU_C1 * u * u))


def _silu(z):
    s = jax.nn.sigmoid(z)
    return z * s, s


def _silu_grad(z, s):
    return s * (1.0 + z * (1.0 - s))


def _fold8(a):
    return jnp.sum(a.reshape(a.shape[0] // SUBLANES, SUBLANES, a.shape[1]), axis=0)


def _dot(a, b):
    return jnp.dot(a, b, preferred_element_type=F32)


def _dot_nt(a, b):
    return lax.dot_general(a, b, (((1,), (1,)), ((), ())), preferred_element_type=F32)


def _dot_tn(a, b):
    return lax.dot_general(a, b, (((0,), (0,)), ((), ())), preferred_element_type=F32)


def _tril_mask():
    r = lax.broadcasted_iota(jnp.int32, (CHUNK, CHUNK), 0)
    c = lax.broadcasted_iota(jnp.int32, (CHUNK, CHUNK), 1)
    return c <= r


def _cast_bf16(a, name):
    rows, cols = a.shape
    rb = _tile(rows, 256)

    def body(a_ref, o_ref):
        o_ref[...] = a_ref[...].astype(BF16)

    return pl.pallas_call(
        body, name=name, grid=(rows // rb,),
        in_specs=[pl.BlockSpec((rb, cols), lambda i: (i, 0))],
        out_specs=pl.BlockSpec((rb, cols), lambda i: (i, 0)),
        out_shape=jax.ShapeDtypeStruct((rows, cols), BF16),
        compiler_params=_cparams(("parallel",)),
    )(a)


def _shard_ref(full, kind, s, n):
    if kind == "col":
        return full.at[:, pl.ds(pl.multiple_of(s * n, 128), n)]
    if kind == "row":
        return full.at[pl.ds(pl.multiple_of(s * n, SUBLANES), n), :]
    return full.at[s]


def _all_gather(shards, kinds, name, vmem):
    nt = len(shards)
    out_shapes = []
    for a, kind in zip(shards, kinds):
        if kind == "col":
            out_shapes.append(jax.ShapeDtypeStruct((a.shape[0], N_DEV * a.shape[1]), a.dtype))
        elif kind == "row":
            out_shapes.append(jax.ShapeDtypeStruct((N_DEV * a.shape[0], a.shape[1]), a.dtype))
        else:
            out_shapes.append(jax.ShapeDtypeStruct((N_DEV,) + a.shape, a.dtype))

    def body(*refs):
        srcs, fulls = refs[:nt], refs[nt:2 * nt]
        send_sems, recv_sems, local_sems = refs[2 * nt:]
        x, y, c = lax.axis_index("x"), lax.axis_index("y"), lax.axis_index("c")
        sibling = (x, y, 1 - c)
        chips = [(1 - x, y), (x, 1 - y), (1 - x, 1 - y)]

        def dev(px, py, pc):
            return 4 * px + 2 * py + pc

        def region(t, s):
            a, kind = shards[t], kinds[t]
            n = a.shape[1] if kind == "col" else a.shape[0]
            return _shard_ref(fulls[t], kind, s, n)

        def copy(t, k, block, to, own=False):
            return pltpu.make_async_remote_copy(
                src_ref=srcs[t] if own else region(t, block), dst_ref=region(t, block),
                send_sem=send_sems.at[7 * t + k], recv_sem=recv_sems.at[7 * t + k],
                device_id=to, device_id_type=MESH)

        me = dev(x, y, c)
        started = []
        for t in range(nt):
            mine = pltpu.make_async_copy(srcs[t], region(t, me), local_sems.at[t])
            mine.start()
            started.append(mine)
        first = []
        for t in range(nt):
            first.append(copy(t, 0, me, sibling, own=True))
            for j, chip in enumerate(chips):
                first.append(copy(t, 1 + j, me, (*chip, c), own=True))
        for cp in first:
            cp.start()
        passed = []
        for t in range(nt):
            for j, chip in enumerate(chips):
                blk = dev(*chip, c)
                copy(t, 1 + j, blk, sibling).wait_recv()
                fwd = copy(t, 4 + j, blk, sibling)
                fwd.start()
                passed.append(fwd)
        for t in range(nt):
            copy(t, 0, dev(x, y, 1 - c), sibling).wait_recv()
            for j, chip in enumerate(chips):
                copy(t, 4 + j, dev(*chip, 1 - c), sibling).wait_recv()
        for cp in first + passed:
            cp.wait_send()
        for mine in started:
            mine.wait()

    space = pltpu.VMEM if vmem else pl.ANY
    return pl.pallas_call(
        body, name=name,
        in_specs=[pl.BlockSpec(memory_space=space)] * nt,
        out_specs=[pl.BlockSpec(memory_space=space)] * nt,
        out_shape=out_shapes,
        scratch_shapes=[pltpu.SemaphoreType.DMA((7 * nt,)), pltpu.SemaphoreType.DMA((7 * nt,)),
                        pltpu.SemaphoreType.DMA((nt,))],
        compiler_params=pltpu.CompilerParams(vmem_limit_bytes=VMEM_LIMIT, has_side_effects=True),
    )(*shards)


def _pair_exchange(grads, kinds, name):
    nt = len(grads)
    shard_shapes = []
    for g, kind in zip(grads, kinds):
        shard_shapes.append((g.shape[0], g.shape[1] // N_DEV) if kind == "col" else (g.shape[0] // N_DEV, g.shape[1]))

    def body(*refs):
        srcs, lands = refs[:nt], refs[nt:2 * nt]
        send_sems, recv_sems = refs[2 * nt:]
        x, y, c = lax.axis_index("x"), lax.axis_index("y"), lax.axis_index("c")
        copies = []
        for t in range(nt):
            n = shard_shapes[t][1] if kinds[t] == "col" else shard_shapes[t][0]
            for k in range(N_CHIP):
                cp = pltpu.make_async_remote_copy(
                    src_ref=_shard_ref(srcs[t], kinds[t], 2 * k + 1 - c, n), dst_ref=lands[t].at[k],
                    send_sem=send_sems.at[N_CHIP * t + k], recv_sem=recv_sems.at[N_CHIP * t + k],
                    device_id=(x, y, 1 - c), device_id_type=MESH)
                cp.start()
                copies.append(cp)
        for cp in copies:
            cp.wait()

    return pl.pallas_call(
        body, name=name,
        in_specs=[pl.BlockSpec(memory_space=pl.ANY)] * nt,
        out_specs=[pl.BlockSpec(memory_space=pl.ANY)] * nt,
        out_shape=[jax.ShapeDtypeStruct((N_CHIP,) + s, g.dtype) for s, g in zip(shard_shapes, grads)],
        scratch_shapes=[pltpu.SemaphoreType.DMA((N_CHIP * nt,)), pltpu.SemaphoreType.DMA((N_CHIP * nt,))],
        compiler_params=pltpu.CompilerParams(has_side_effects=True),
    )(*grads)


def _pair_sum(grad, land, kind, c_arr, name):
    _, r, w = land.shape
    rb = _tile(r, 256)
    nrb = r // rb

    def body(c_ref, g_ref, l_ref, o_ref):
        o_ref[...] = (g_ref[...].astype(F32) + l_ref[...].astype(F32)).astype(o_ref.dtype)

    if kind == "col":
        g_spec = pl.BlockSpec((rb, w), lambda k, i, c: (i, 2 * k + c[0]))
    else:
        g_spec = pl.BlockSpec((rb, w), lambda k, i, c: ((2 * k + c[0]) * nrb + i, 0))
    return pl.pallas_call(
        body, name=name,
        grid_spec=pltpu.PrefetchScalarGridSpec(
            num_scalar_prefetch=1, grid=(N_CHIP, nrb),
            in_specs=[g_spec, pl.BlockSpec((None, rb, w), lambda k, i, c: (k, i, 0))],
            out_specs=pl.BlockSpec((None, rb, w), lambda k, i, c: (k, i, 0))),
        out_shape=jax.ShapeDtypeStruct(land.shape, BF16),
        compiler_params=_cparams(("parallel", "parallel")),
    )(c_arr, grad, land)


def _adam(w, g, m, v):
    m = ADAM_B1 * m + (1.0 - ADAM_B1) * g
    v = ADAM_B2 * v + (1.0 - ADAM_B2) * jnp.square(g)
    m_hat = m / (1.0 - ADAM_B1 ** ADAM_STEP)
    v_hat = v / (1.0 - ADAM_B2 ** ADAM_STEP)
    delta = -ADAM_LR * (m_hat / (jnp.sqrt(v_hat) + ADAM_EPS) + ADAM_WD * w)
    return delta, m, v


def _sum_adam(part, land, w, m, v, chip_arr, name):
    r, wd = w.shape
    rb = _tile(r, 128)

    def body(k_ref, own, r1, r2, r3, w_ref, m_ref, v_ref, g_out, d_out, m_out, v_out):
        g = own[...].astype(F32) + r1[...].astype(F32) + r2[...].astype(F32) + r3[...].astype(F32)
        d, mn, vn = _adam(w_ref[...], g, m_ref[...], v_ref[...])
        g_out[...] = g
        d_out[...] = d
        m_out[...] = mn
        v_out[...] = vn

    def slot(off):
        return pl.BlockSpec((None, rb, wd), lambda i, k: ((k[0] + off) % N_CHIP, i, 0))

    plain = pl.BlockSpec((rb, wd), lambda i, k: (i, 0))
    return pl.pallas_call(
        body, name=name,
        grid_spec=pltpu.PrefetchScalarGridSpec(
            num_scalar_prefetch=1, grid=(r // rb,),
            in_specs=[slot(0), slot(1), slot(2), slot(3), plain, plain, plain],
            out_specs=[plain] * 4),
        out_shape=[jax.ShapeDtypeStruct(w.shape, F32)] * 4,
        compiler_params=_cparams(("parallel",)),
    )(chip_arr, part, land, land, land, w, m, v)


def _small_sum(gathered, name):
    _, r, w = gathered.shape

    def body(g_ref, o_ref):
        acc = g_ref[0]
        for d in range(1, N_DEV):
            acc = acc + g_ref[d]
        o_ref[...] = acc

    return pl.pallas_call(body, name=name, out_shape=jax.ShapeDtypeStruct((r, w), F32))(gathered)


def _small_adam(ws, gs, ms, vs, name):
    n = len(ws)

    def body(*refs):
        ins, outs = refs[:4 * n], refs[4 * n:]
        for t in range(n):
            d, mn, vn = _adam(ins[t][...], ins[n + t][...], ins[2 * n + t][...], ins[3 * n + t][...])
            outs[t][...] = d
            outs[n + t][...] = mn
            outs[2 * n + t][...] = vn

    shapes = [jax.ShapeDtypeStruct(w.shape, F32) for w in ws]
    return pl.pallas_call(body, name=name, out_shape=shapes * 3)(*ws, *gs, *ms, *vs)


def _mixer_fwd(xb, shards, ln_v_g, ln_v_b, w_s, b_s3, conv_w, conv_b):
    t_len, d = xb.shape
    hd = d // HEADS
    tm = _tile(t_len, 512)
    nt = t_len // tm
    assert nt >= 2
    w8 = shards[0].shape[1]
    r8 = shards[1].shape[0]
    bps = w8 // hd
    nq = N_IN * HEADS
    n_blocks = nq + 3 * N_DEV
    groups = [[("in", HEADS * b + k) for b in range(N_MIX)] for k in range(HEADS)]
    groups.append([("in", q) for q in range(N_MIX * HEADS, nq)] + [(t, s) for t in (1, 2, 3) for s in range(N_DEV)])

    def owner_of(blk):
        return blk[1] // bps if blk[0] == "in" else blk[1]

    send_step = {}
    for s in range(N_DEV):
        mine = [blk for grp in groups for blk in grp if owner_of(blk) == s]
        for pos, blk in enumerate(mine):
            send_step[blk] = 0 if pos < FIRST_SENDS else 1 + (pos - FIRST_SENDS) // SENDS_PER_GROUP
    assert max(send_step.values()) < HEADS
    tail_pass_step = nt - 6 if nt >= 8 else nt - 2

    def body(x_ref, sh_in, sh_oa, sh_ob, sh_out, lng, lnb, ws_ref, bs_ref, cw_ref, cb_ref,
             save_ref, ya_ref, yb_ref, f_in, f_oa, f_ob, f_out,
             hbuf, wbuf, recv_sems, own_sems, fwd_sems, local_sems, load_sems):
        g = pl.program_id(0)
        i = pl.program_id(1)
        x, y, c = lax.axis_index("x"), lax.axis_index("y"), lax.axis_index("c")
        sibling = (x, y, 1 - c)
        chips = [(1 - x, y), (x, 1 - y), (1 - x, 1 - y)]
        shard_refs = (sh_in, sh_oa, sh_ob, sh_out)
        fulls = (f_in, f_oa, f_ob, f_out)

        def tensor(blk):
            return 0 if blk[0] == "in" else blk[0]

        def owner(blk):
            s = owner_of(blk)
            return s // 4, (s // 2) % 2, s % 2

        def bid(blk):
            return blk[1] if blk[0] == "in" else nq + (blk[0] - 1) * N_DEV + blk[1]

        def region(blk):
            if blk[0] == "in":
                return f_in.at[:, pl.ds(blk[1] * hd, hd)]
            return fulls[blk[0]].at[pl.ds(blk[1] * r8, r8), :]

        def own_src(blk):
            if blk[0] == "in":
                return sh_in.at[:, pl.ds((blk[1] % bps) * hd, hd)]
            return shard_refs[blk[0]]

        def rcopy(blk, to, send_sem, own):
            return pltpu.make_async_remote_copy(
                src_ref=own_src(blk) if own else region(blk), dst_ref=region(blk),
                send_sem=send_sem, recv_sem=recv_sems.at[bid(blk)], device_id=to, device_id_type=MESH)

        def send_own(blk):
            ox, oy, oc = owner(blk)

            @pl.when((x == ox) & (y == oy) & (c == oc))
            def _():
                rcopy(blk, sibling, own_sems.at[tensor(blk)], True).start()
                for chip in chips:
                    rcopy(blk, (*chip, c), own_sems.at[tensor(blk)], True).start()

        def pass_on(blk):
            ox, oy, oc = owner(blk)

            @pl.when(((x != ox) | (y != oy)) & (c == oc))
            def _():
                rcopy(blk, sibling, fwd_sems.at[tensor(blk)], False).wait_recv()
                rcopy(blk, sibling, fwd_sems.at[tensor(blk)], False).start()

        def wait_from_sibling(blk):
            @pl.when(c != owner(blk)[2])
            def _():
                rcopy(blk, sibling, fwd_sems.at[tensor(blk)], False).wait_recv()

        def local_copy(t):
            n = w8 if t == 0 else r8
            me = 4 * x + 2 * y + c
            return pltpu.make_async_copy(shard_refs[t], _shard_ref(fulls[t], "col" if t == 0 else "row", me, n),
                                         local_sems.at[t])

        first = (g == 0) & (i == 0)

        @pl.when(first)
        def _():
            for t in range(4):
                local_copy(t).start()
            for grp in groups:
                for blk in grp:
                    if send_step[blk] == 0:
                        send_own(blk)
            for blk in groups[0]:
                pass_on(blk)
            for t in range(4):
                local_copy(t).wait()

        @pl.when(i == nt - 2)
        def _():
            for k in range(HEADS - 1):
                @pl.when(g == k)
                def _(k=k):
                    for blk in groups[k + 1]:
                        pass_on(blk)

        @pl.when((g == HEADS - 1) & (i == tail_pass_step))
        def _():
            for blk in groups[HEADS]:
                pass_on(blk)

        @pl.when(i == 0)
        def _():
            for k in range(HEADS):
                @pl.when(g == k)
                def _(k=k):
                    if k > 0:
                        for grp in groups:
                            for blk in grp:
                                if send_step[blk] == k:
                                    send_own(blk)
                    for blk in groups[k]:
                        wait_from_sibling(blk)
                    loads = [pltpu.make_async_copy(region(blk), wbuf.at[b], load_sems.at[b])
                             for b, blk in enumerate(groups[k])]
                    for cp in loads:
                        cp.start()
                    for cp in loads:
                        cp.wait()

        xt = x_ref[...]
        u = _dot(xt, wbuf[0])
        v = _dot(xt, wbuf[1])
        z = _dot(xt, wbuf[2])
        save_ref[0] = u.astype(BF16)
        save_ref[1] = v.astype(BF16)
        save_ref[2] = z.astype(BF16)
        gu, _ = _gelu(u)
        gv, _ = _gelu(v)
        mu = jnp.mean(gv, axis=-1, keepdims=True)
        dv = gv - mu
        var = jnp.mean(dv * dv, axis=-1, keepdims=True)
        vn = (dv * lax.rsqrt(var + LN_EPS) * lng[...] + lnb[...]).astype(BF16)
        sz, _ = _silu(z)
        gate = gu * sz
        wm = jnp.where(_tril_mask(), ws_ref[0], 0.0).astype(BF16)
        bs = bs_ref[0]
        for ck in range(tm // CHUNK):
            rows = slice(ck * CHUNK, (ck + 1) * CHUNK)
            mixed = _dot(wm, vn[rows]) + bs
            ya_ref[rows, :] = (gate[rows] * mixed).astype(BF16)

        xbv = _dot(xt, wbuf[3])
        cbv = _dot(xt, wbuf[4])
        bbv = _dot(xt, wbuf[5])
        zbv = _dot(xt, wbuf[6])
        save_ref[3] = xbv.astype(BF16)
        save_ref[4] = cbv.astype(BF16)
        save_ref[5] = bbv.astype(BF16)
        save_ref[6] = zbv.astype(BF16)
        h = cbv * xbv

        @pl.when(i == 0)
        def _():
            hbuf[0:SUBLANES, :] = jnp.zeros((SUBLANES, hd), F32)

        hbuf[SUBLANES:SUBLANES + tm, :] = h
        h1 = hbuf[SUBLANES - 1:SUBLANES - 1 + tm, :]
        h2 = hbuf[SUBLANES - 2:SUBLANES - 2 + tm, :]
        conv = cb_ref[...] + cw_ref[0:1, :] * h2 + cw_ref[1:2, :] * h1 + cw_ref[2:3, :] * h
        hbuf[0:SUBLANES, :] = h[tm - SUBLANES:tm, :]
        save_ref[7] = conv.astype(BF16)
        szb, _ = _silu(zbv)
        yb_ref[...] = (bbv * conv * szb).astype(BF16)

        @pl.when((g == HEADS - 1) & (i == nt - 1))
        def _():
            for blk in groups[HEADS]:
                wait_from_sibling(blk)
            for t in range(4):
                if t == 0:
                    own_all, fwd_all = f_in.at[:, pl.ds(0, 4 * w8)], f_in.at[:, pl.ds(0, 3 * w8)]
                else:
                    own_all, fwd_all = fulls[t].at[pl.ds(0, 4 * r8), :], fulls[t].at[pl.ds(0, 3 * r8), :]
                for ref, sem in ((own_all, own_sems.at[t]), (fwd_all, fwd_sems.at[t])):
                    pltpu.make_async_remote_copy(src_ref=ref, dst_ref=ref, send_sem=sem, recv_sem=sem,
                                                 device_id=sibling, device_id_type=MESH).wait_send()

    vec = pl.BlockSpec((1, hd), lambda g, i: (0, g))
    hbm = pl.BlockSpec(memory_space=pl.ANY)
    return pl.pallas_call(
        body, name="mixer_fwd", grid=(HEADS, nt),
        in_specs=[pl.BlockSpec((tm, d), lambda g, i: (i, 0)), hbm, hbm, hbm, hbm,
                  vec, vec,
                  pl.BlockSpec((1, CHUNK, CHUNK), lambda g, i: (g, 0, 0)),
                  pl.BlockSpec((1, CHUNK, 1), lambda g, i: (g, 0, 0)),
                  pl.BlockSpec((3, hd), lambda g, i: (0, g)),
                  vec],
        out_specs=[pl.BlockSpec((N_SAVE, tm, hd), lambda g, i: (0, i, g)),
                   pl.BlockSpec((tm, hd), lambda g, i: (i, g)),
                   pl.BlockSpec((tm, hd), lambda g, i: (i, g)),
                   hbm, hbm, hbm, hbm],
        out_shape=[jax.ShapeDtypeStruct((N_SAVE, t_len, d), BF16),
                   jax.ShapeDtypeStruct((t_len, d), BF16),
                   jax.ShapeDtypeStruct((t_len, d), BF16),
                   jax.ShapeDtypeStruct((d, N_DEV * w8), BF16),
                   jax.ShapeDtypeStruct((d, d), BF16), jax.ShapeDtypeStruct((d, d), BF16),
                   jax.ShapeDtypeStruct((d, d), BF16)],
        scratch_shapes=[pltpu.VMEM((SUBLANES + tm, hd), F32), pltpu.VMEM((N_MIX, d, hd), BF16),
                        pltpu.SemaphoreType.DMA((n_blocks,)), pltpu.SemaphoreType.DMA((4,)),
                        pltpu.SemaphoreType.DMA((4,)), pltpu.SemaphoreType.DMA((4,)),
                        pltpu.SemaphoreType.DMA((N_MIX,))],
        compiler_params=_cparams(("arbitrary", "arbitrary")),
    )(xb, *shards, ln_v_g, ln_v_b, w_s, b_s3, conv_w, conv_b)


def _merge_fwd(xb, ya, yb, w_in, w_oa, w_ob, b_gate):
    t_len, d = xb.shape
    tm = _tile(t_len, 512)
    tn = _tile(d, 512)
    nj = d // tn

    def body(x_ref, ya_ref, yb_ref, wga, wgb, woa, wob, bga, bgb, sa_ref, sb_ref, pa_ref, pb_ref, mg_ref):
        xt = x_ref[...]
        sa = jax.nn.sigmoid(_dot(xt, wga[...]) + bga[...])
        sb = jax.nn.sigmoid(_dot(xt, wgb[...]) + bgb[...])
        pa = _dot(ya_ref[...], woa[...])
        pb = _dot(yb_ref[...], wob[...])
        sa_ref[...] = sa.astype(BF16)
        sb_ref[...] = sb.astype(BF16)
        pa_ref[...] = pa.astype(BF16)
        pb_ref[...] = pb.astype(BF16)
        mg_ref[...] = (sa * pa + sb * pb).astype(BF16)

    row = pl.BlockSpec((tm, d), lambda j, i: (i, 0))
    out = pl.BlockSpec((tm, tn), lambda j, i: (i, j))
    return pl.pallas_call(
        body, name="merge_fwd", grid=(nj, t_len // tm),
        in_specs=[row, row, row,
                  pl.BlockSpec((d, tn), lambda j, i: (0, 7 * nj + j)),
                  pl.BlockSpec((d, tn), lambda j, i: (0, 8 * nj + j)),
                  pl.BlockSpec((d, tn), lambda j, i: (0, j)),
                  pl.BlockSpec((d, tn), lambda j, i: (0, j)),
                  pl.BlockSpec((1, tn), lambda j, i: (0, j)),
                  pl.BlockSpec((1, tn), lambda j, i: (0, nj + j))],
        out_specs=[out] * 5,
        out_shape=[jax.ShapeDtypeStruct((t_len, d), BF16)] * 5,
        compiler_params=_cparams(("parallel", "arbitrary")),
    )(xb, ya, yb, w_in, w_in, w_oa, w_ob, b_gate, b_gate)


def _out_ln_loss(merged, w_out, x, target, ln_g, ln_b):
    t_len, d = x.shape
    tm = _tile(t_len, 256)
    nt = t_len // tm

    def body(mg_ref, w_ref, x_ref, t_ref, g_ref, b_ref, dz_ref, gx_ref, glg_ref, glb_ref, ls_ref, a_g, a_b, a_l):
        i = pl.program_id(0)

        @pl.when(i == 0)
        def _():
            a_g[...] = jnp.zeros_like(a_g)
            a_b[...] = jnp.zeros_like(a_b)
            a_l[...] = jnp.zeros_like(a_l)

        zres = DN_ALPHA * x_ref[...] + _dot(mg_ref[...], w_ref[...])
        mu = jnp.mean(zres, axis=-1, keepdims=True)
        dc = zres - mu
        var = jnp.mean(dc * dc, axis=-1, keepdims=True)
        rstd = lax.rsqrt(var + LN_EPS)
        xhat = dc * rstd
        g = g_ref[...]
        err = xhat * g + b_ref[...] - t_ref[...]
        dy = err * (1.0 / d)
        a_l[...] += _fold8(err * err)
        a_g[...] += _fold8(dy * xhat)
        a_b[...] += _fold8(dy)
        dxh = dy * g
        m1 = jnp.mean(dxh, axis=-1, keepdims=True)
        m2 = jnp.mean(dxh * xhat, axis=-1, keepdims=True)
        dz = rstd * (dxh - m1 - xhat * m2)
        dz_ref[...] = dz.astype(BF16)
        gx_ref[...] = DN_ALPHA * dz

        @pl.when(i == nt - 1)
        def _():
            glg_ref[...] = jnp.sum(a_g[...], axis=0, keepdims=True)
            glb_ref[...] = jnp.sum(a_b[...], axis=0, keepdims=True)
            ls_ref[...] = jnp.sum(a_l[...], axis=0, keepdims=True)

    row = pl.BlockSpec((tm, d), lambda i: (i, 0))
    vec = pl.BlockSpec((1, d), lambda i: (0, 0))
    return pl.pallas_call(
        body, name="out_ln_loss", grid=(nt,),
        in_specs=[row, pl.BlockSpec((d, d), lambda i: (0, 0)), row, row, vec, vec],
        out_specs=[row, row, vec, vec, vec],
        out_shape=[jax.ShapeDtypeStruct((t_len, d), BF16), jax.ShapeDtypeStruct((t_len, d), F32)]
        + [jax.ShapeDtypeStruct((1, d), F32)] * 3,
        scratch_shapes=[pltpu.VMEM((SUBLANES, d), F32)] * 3,
        compiler_params=_cparams(("arbitrary",)),
    )(merged, w_out, x, target, ln_g, ln_b)


def _merge_bwd(dz, w_out, sa, sb, pa, pb):
    t_len, d = dz.shape
    tm = _tile(t_len, 512)
    tn = _tile(d, 512)
    nt = t_len // tm

    def body(dz_ref, w_ref, sa_ref, sb_ref, pa_ref, pb_ref, da_ref, db_ref, dg_ref, ga_ref, gb_ref, acc_a, acc_b):
        i = pl.program_id(1)

        @pl.when(i == 0)
        def _():
            acc_a[...] = jnp.zeros_like(acc_a)
            acc_b[...] = jnp.zeros_like(acc_b)

        dm = _dot_nt(dz_ref[...], w_ref[...])
        sa = sa_ref[...].astype(F32)
        sb = sb_ref[...].astype(F32)
        da = dm * sa
        db = dm * sb
        da_ref[...] = da.astype(BF16)
        db_ref[...] = db.astype(BF16)
        dga = da * pa_ref[...].astype(F32) * (1.0 - sa)
        dgb = db * pb_ref[...].astype(F32) * (1.0 - sb)
        dg_ref[0] = dga.astype(BF16)
        dg_ref[1] = dgb.astype(BF16)
        acc_a[...] += _fold8(dga)
        acc_b[...] += _fold8(dgb)

        @pl.when(i == nt - 1)
        def _():
            ga_ref[...] = jnp.sum(acc_a[...], axis=0, keepdims=True)
            gb_ref[...] = jnp.sum(acc_b[...], axis=0, keepdims=True)

    blk = pl.BlockSpec((tm, tn), lambda j, i: (i, j))
    vec = pl.BlockSpec((1, tn), lambda j, i: (0, j))
    return pl.pallas_call(
        body, name="merge_bwd", grid=(d // tn, nt),
        in_specs=[pl.BlockSpec((tm, d), lambda j, i: (i, 0)), pl.BlockSpec((tn, d), lambda j, i: (j, 0)),
                  blk, blk, blk, blk],
        out_specs=[blk, blk, pl.BlockSpec((2, tm, tn), lambda j, i: (0, i, j)), vec, vec],
        out_shape=[jax.ShapeDtypeStruct((t_len, d), BF16)] * 2 + [jax.ShapeDtypeStruct((2, t_len, d), BF16)]
        + [jax.ShapeDtypeStruct((1, d), F32)] * 2,
        scratch_shapes=[pltpu.VMEM((SUBLANES, tn), F32)] * 2,
        compiler_params=_cparams(("parallel", "arbitrary")),
    )(dz, w_out, sa, sb, pa, pb)


def _grad_w(a, b, name):
    t_len, m = a.shape
    n = b.shape[1]
    tm, tn, tk = _tile(m, MM_TILE), _tile(n, MM_TILE), _tile(t_len, MM_DEPTH)
    nk = t_len // tk

    def body(a_ref, b_ref, o_ref, acc):
        k = pl.program_id(2)

        @pl.when(k == 0)
        def _():
            acc[...] = jnp.zeros_like(acc)

        acc[...] += _dot_tn(a_ref[...], b_ref[...])

        @pl.when(k == nk - 1)
        def _():
            o_ref[...] = acc[...].astype(BF16)

    return pl.pallas_call(
        body, name=name, grid=(m // tm, n // tn, nk),
        in_specs=[pl.BlockSpec((tk, tm), lambda i, j, k: (k, i)), pl.BlockSpec((tk, tn), lambda i, j, k: (k, j))],
        out_specs=pl.BlockSpec((tm, tn), lambda i, j, k: (i, j)),
        out_shape=jax.ShapeDtypeStruct((m, n), BF16),
        scratch_shapes=[pltpu.VMEM((tm, tn), F32)],
        compiler_params=_cparams(("parallel", "parallel", "arbitrary")),
    )(a, b)


def _mixer_bwd(da, db, w_oa, w_ob, saved, dgate, ln_v_g, ln_v_b, w_s, b_s3, conv_w):
    t_len, d = da.shape
    hd = d // HEADS
    tm = _tile(t_len, 512)
    nt = t_len // tm

    def body(da_ref, db_ref, woa, wob, sv, dgt, lng, lnb, ws_ref, bs_ref, cw_ref,
             dp_ref, gws_ref, gbs_ref, glg_ref, glb_ref, gcw_ref, gcb_ref,
             dbuf, carry, a_ws, a_bs, a_lg, a_lb, a_c0, a_c1, a_c2, a_cb):
        i = pl.program_id(1)

        @pl.when(i == 0)
        def _():
            carry[...] = jnp.zeros_like(carry)
            for a in (a_ws, a_bs, a_lg, a_lb, a_c0, a_c1, a_c2, a_cb):
                a[...] = jnp.zeros_like(a)

        dya = _dot_nt(da_ref[...], woa[...])
        u = sv[0].astype(F32)
        v = sv[1].astype(F32)
        z = sv[2].astype(F32)
        gu, tu = _gelu(u)
        gv, tv = _gelu(v)
        mu = jnp.mean(gv, axis=-1, keepdims=True)
        dvc = gv - mu
        var = jnp.mean(dvc * dvc, axis=-1, keepdims=True)
        rstd = lax.rsqrt(var + LN_EPS)
        vhat = dvc * rstd
        g = lng[...]
        vn = (vhat * g + lnb[...]).astype(BF16)
        sz, s = _silu(z)
        t1 = dya * sz
        dmixed = t1 * gu
        dmixed_b = dmixed.astype(BF16)
        wm = jnp.where(_tril_mask(), ws_ref[0], 0.0).astype(BF16)
        bs = bs_ref[0]
        dvn_parts = []
        mixed_parts = []
        gws = a_ws[...]
        gbs = a_bs[...]
        for ck in range(tm // CHUNK):
            rows = slice(ck * CHUNK, (ck + 1) * CHUNK)
            mixed_parts.append(_dot(wm, vn[rows]) + bs)
            gws = gws + _dot_nt(dmixed_b[rows], vn[rows])
            gbs = gbs + dmixed[rows]
            dvn_parts.append(_dot_tn(wm, dmixed_b[rows]))
        a_ws[...] = gws
        a_bs[...] = gbs
        mixed = jnp.concatenate(mixed_parts, axis=0)
        dvn = jnp.concatenate(dvn_parts, axis=0)
        dp_ref[0] = (t1 * mixed * _gelu_grad(u, tu)).astype(BF16)
        dp_ref[2] = (dya * gu * mixed * _silu_grad(z, s)).astype(BF16)
        a_lg[...] += _fold8(dvn * vhat)
        a_lb[...] += _fold8(dvn)
        dvh = dvn * g
        m1 = jnp.mean(dvh, axis=-1, keepdims=True)
        m2 = jnp.mean(dvh * vhat, axis=-1, keepdims=True)
        dp_ref[1] = (rstd * (dvh - m1 - vhat * m2) * _gelu_grad(v, tv)).astype(BF16)

        dyb = _dot_nt(db_ref[...], wob[...])
        xbv = sv[3].astype(F32)
        cbv = sv[4].astype(F32)
        bbv = sv[5].astype(F32)
        zbv = sv[6].astype(F32)
        conv = sv[7].astype(F32)
        szb, sb = _silu(zbv)
        dp_ref[5] = (dyb * conv * szb).astype(BF16)
        dp_ref[6] = (dyb * bbv * conv * _silu_grad(zbv, sb)).astype(BF16)
        dconv = dyb * bbv * szb
        dbuf[0:tm, :] = dconv
        dbuf[tm:tm + SUBLANES, :] = carry[...]
        dc1 = dbuf[1:tm + 1, :]
        dc2 = dbuf[2:tm + 2, :]
        carry[...] = dconv[0:SUBLANES, :]
        h = cbv * xbv
        a_c2[...] += _fold8(dconv * h)
        a_c1[...] += _fold8(dc1 * h)
        a_c0[...] += _fold8(dc2 * h)
        a_cb[...] += _fold8(dconv)
        dh = cw_ref[2:3, :] * dconv + cw_ref[1:2, :] * dc1 + cw_ref[0:1, :] * dc2
        dp_ref[3] = (dh * cbv).astype(BF16)
        dp_ref[4] = (dh * xbv).astype(BF16)
        dp_ref[7] = dgt[0]
        dp_ref[8] = dgt[1]

        @pl.when(i == nt - 1)
        def _():
            gws_ref[0] = jnp.where(_tril_mask(), a_ws[...], 0.0)
            gbs_ref[0] = jnp.sum(a_bs[...], axis=1, keepdims=True)
            glg_ref[...] = jnp.sum(a_lg[...], axis=0, keepdims=True)
            glb_ref[...] = jnp.sum(a_lb[...], axis=0, keepdims=True)
            gcw_ref[0:1, :] = jnp.sum(a_c0[...], axis=0, keepdims=True)
            gcw_ref[1:2, :] = jnp.sum(a_c1[...], axis=0, keepdims=True)
            gcw_ref[2:3, :] = jnp.sum(a_c2[...], axis=0, keepdims=True)
            gcb_ref[...] = jnp.sum(a_cb[...], axis=0, keepdims=True)

    def rev(i):
        return nt - 1 - i

    row = pl.BlockSpec((tm, d), lambda g, i: (rev(i), 0))
    wrow = pl.BlockSpec((hd, d), lambda g, i: (g, 0))
    vec = pl.BlockSpec((1, hd), lambda g, i: (0, g))
    acc8 = pltpu.VMEM((SUBLANES, hd), F32)
    return pl.pallas_call(
        body, name="mixer_bwd", grid=(HEADS, nt),
        in_specs=[row, row, wrow, wrow,
                  pl.BlockSpec((N_SAVE, tm, hd), lambda g, i: (0, rev(i), g)),
                  pl.BlockSpec((2, tm, hd), lambda g, i: (0, rev(i), g)),
                  vec, vec,
                  pl.BlockSpec((1, CHUNK, CHUNK), lambda g, i: (g, 0, 0)),
                  pl.BlockSpec((1, CHUNK, 1), lambda g, i: (g, 0, 0)),
                  pl.BlockSpec((3, hd), lambda g, i: (0, g))],
        out_specs=[pl.BlockSpec((N_IN, tm, hd), lambda g, i: (0, rev(i), g)),
                   pl.BlockSpec((1, CHUNK, CHUNK), lambda g, i: (g, 0, 0)),
                   pl.BlockSpec((1, CHUNK, 1), lambda g, i: (g, 0, 0)),
                   vec, vec,
                   pl.BlockSpec((3, hd), lambda g, i: (0, g)),
                   vec],
        out_shape=[jax.ShapeDtypeStruct((N_IN, t_len, d), BF16),
                   jax.ShapeDtypeStruct((HEADS, CHUNK, CHUNK), F32),
                   jax.ShapeDtypeStruct((HEADS, CHUNK, 1), F32),
                   jax.ShapeDtypeStruct((1, d), F32), jax.ShapeDtypeStruct((1, d), F32),
                   jax.ShapeDtypeStruct((3, d), F32), jax.ShapeDtypeStruct((1, d), F32)],
        scratch_shapes=[pltpu.VMEM((tm + SUBLANES, hd), F32), acc8,
                        pltpu.VMEM((CHUNK, CHUNK), F32), pltpu.VMEM((CHUNK, hd), F32),
                        acc8, acc8, acc8, acc8, acc8, acc8],
        compiler_params=_cparams(("parallel", "arbitrary")),
    )(da, db, w_oa, w_ob, saved, dgate, ln_v_g, ln_v_b, w_s, b_s3, conv_w)


def _grad_w_in(xb, dp):
    t_len, d = xb.shape
    tm, tn, tk = _tile(d, MM_TILE), _tile(d, MM_TILE), _tile(t_len, MM_DEPTH)
    nj = d // tn
    nk = t_len // tk

    def body(a_ref, b_ref, o_ref, acc):
        k = pl.program_id(2)

        @pl.when(k == 0)
        def _():
            acc[...] = jnp.zeros_like(acc)

        acc[...] += _dot_tn(a_ref[...], b_ref[...])

        @pl.when(k == nk - 1)
        def _():
            o_ref[...] = acc[...].astype(BF16)

    return pl.pallas_call(
        body, name="grad_w_in", grid=(d // tm, N_IN * nj, nk),
        in_specs=[pl.BlockSpec((tk, tm), lambda i, j, k: (k, i)),
                  pl.BlockSpec((None, tk, tn), lambda i, j, k: (j // nj, k, j % nj))],
        out_specs=pl.BlockSpec((tm, tn), lambda i, j, k: (i, j)),
        out_shape=jax.ShapeDtypeStruct((d, N_IN * d), BF16),
        scratch_shapes=[pltpu.VMEM((tm, tn), F32)],
        compiler_params=_cparams(("parallel", "parallel", "arbitrary")),
    )(xb, dp)


def _grad_x(dp, w_in, gx_direct, parts, packed):
    _, t_len, d = dp.shape
    tm, tn, tk = _tile(t_len, MM_TILE), _tile(d, MM_TILE), _tile(d, MM_DEPTH)
    nkb = d // tk
    nk = N_IN * nkb
    ni, nj = t_len // tm, d // tn
    n_parts = len(parts)

    def body(a_ref, b_ref, r_ref, *rest):
        srcs = rest[:n_parts]
        pk_ref = rest[n_parts]
        o_ref = rest[n_parts + 1]
        lands = rest[n_parts + 2:2 * n_parts + 2]
        gath = rest[2 * n_parts + 2]
        acc, send_sems, recv_sems, pk_send, pk_recv, pk_local = rest[2 * n_parts + 3:]
        i, j, k = pl.program_id(0), pl.program_id(1), pl.program_id(2)
        x, y, c = lax.axis_index("x"), lax.axis_index("y"), lax.axis_index("c")
        my_chip = 2 * x + y
        me = 4 * x + 2 * y + c
        chips = [(1 - x, y), (x, 1 - y), (1 - x, 1 - y)]

        def part_copy(t, n):
            px, py = chips[n]
            return pltpu.make_async_remote_copy(
                src_ref=srcs[t].at[2 * px + py], dst_ref=lands[t].at[my_chip],
                send_sem=send_sems.at[3 * t + n], recv_sem=recv_sems.at[3 * t + n],
                device_id=(px, py, c), device_id_type=MESH)

        def part_landing(t, n):
            px, py = chips[n]
            return pltpu.make_async_remote_copy(
                src_ref=srcs[t].at[my_chip], dst_ref=lands[t].at[2 * px + py],
                send_sem=send_sems.at[3 * t + n], recv_sem=recv_sems.at[3 * t + n],
                device_id=(px, py, c), device_id_type=MESH)

        def pk_copy(s):
            return pltpu.make_async_remote_copy(
                src_ref=pk_ref, dst_ref=gath.at[me], send_sem=pk_send, recv_sem=pk_recv.at[me],
                device_id=(s // 4, (s // 2) % 2, s % 2), device_id_type=MESH)

        @pl.when((i == 0) & (j == 0) & (k == 0))
        def _():
            for t in range(n_parts):
                for n in range(3):
                    part_copy(t, n).start()
            pltpu.make_async_copy(pk_ref, gath.at[me], pk_local).start()
            for s in range(N_DEV):
                @pl.when(s != me)
                def _(s=s):
                    pk_copy(s).start()

        @pl.when(k == 0)
        def _():
            acc[...] = r_ref[...]

        acc[...] += _dot_nt(a_ref[...], b_ref[...])

        @pl.when(k == nk - 1)
        def _():
            o_ref[...] = acc[...]

        @pl.when((i == ni - 1) & (j == nj - 1) & (k == nk - 1))
        def _():
            for t in range(n_parts):
                for n in range(3):
                    part_landing(t, n).wait_recv()
            for t in range(n_parts):
                for n in range(3):
                    part_copy(t, n).wait_send()
            for s in range(N_DEV):
                @pl.when(s != me)
                def _(s=s):
                    pltpu.make_async_remote_copy(
                        src_ref=pk_ref, dst_ref=gath.at[s], send_sem=pk_send, recv_sem=pk_recv.at[s],
                        device_id=(s // 4, (s // 2) % 2, s % 2), device_id_type=MESH).wait_recv()
            seven = gath.at[pl.ds(0, N_DEV - 1)]
            pltpu.make_async_remote_copy(src_ref=seven, dst_ref=seven, send_sem=pk_send, recv_sem=pk_send,
                                         device_id=(x, y, 1 - c), device_id_type=MESH).wait_send()
            pltpu.make_async_copy(pk_ref, gath.at[me], pk_local).wait()

    hbm = pl.BlockSpec(memory_space=pl.ANY)
    outs = pl.pallas_call(
        body, name="grad_x", grid=(ni, nj, nk),
        in_specs=[pl.BlockSpec((None, tm, tk), lambda i, j, k: (k // nkb, i, k % nkb)),
                  pl.BlockSpec((tn, tk), lambda i, j, k: (j, k)),
                  pl.BlockSpec((tm, tn), lambda i, j, k: (i, j))] + [hbm] * (n_parts + 1),
        out_specs=[pl.BlockSpec((tm, tn), lambda i, j, k: (i, j))] + [hbm] * (n_parts + 1),
        out_shape=[jax.ShapeDtypeStruct((t_len, d), F32)] + [jax.ShapeDtypeStruct(p.shape, p.dtype) for p in parts]
        + [jax.ShapeDtypeStruct((N_DEV,) + packed.shape, packed.dtype)],
        scratch_shapes=[pltpu.VMEM((tm, tn), F32),
                        pltpu.SemaphoreType.DMA((3 * n_parts,)), pltpu.SemaphoreType.DMA((3 * n_parts,)),
                        pltpu.SemaphoreType.DMA(()), pltpu.SemaphoreType.DMA((N_DEV,)), pltpu.SemaphoreType.DMA(())],
        compiler_params=_cparams(("arbitrary", "arbitrary", "arbitrary")),
    )(dp, w_in, gx_direct, *parts, packed)
    return outs[0], list(outs[1:1 + n_parts]), outs[1 + n_parts]


def kernel(x, w_in, b_gate, ln_v_g, ln_v_b, w_s, b_s, conv_w, conv_b, w_oa, w_ob, w_out, ln_g, ln_b, loss_target, m_w_in, m_b_gate, m_ln_v_g, m_ln_v_b, m_w_s, m_b_s, m_conv_w, m_conv_b, m_w_oa, m_w_ob, m_w_out, m_ln_g, m_ln_b, v_w_in, v_b_gate, v_ln_v_g, v_ln_v_b, v_w_s, v_b_s, v_conv_w, v_conv_b, v_w_oa, v_w_ob, v_w_out, v_ln_g, v_ln_b):
    _, t_len, d = x.shape
    assert d % (HEADS * 128) == 0 and t_len % CHUNK == 0 and w_in.shape[2] * N_DEV == N_IN * d
    x2 = x[0]
    tgt2 = loss_target[0]
    c_arr = lax.axis_index("c").astype(jnp.int32).reshape(1)
    chip_arr = (2 * lax.axis_index("x") + lax.axis_index("y")).astype(jnp.int32).reshape(1)
    dev = 4 * lax.axis_index("x") + 2 * lax.axis_index("y") + lax.axis_index("c")

    xb = _cast_bf16(x2, "cast_x")
    shards = [_cast_bf16(w_in[0], "cast_w_in"), _cast_bf16(w_oa[0], "cast_w_oa"),
              _cast_bf16(w_ob[0], "cast_w_ob"), _cast_bf16(w_out[0], "cast_w_out")]
    (conv_w_g,) = _all_gather([conv_w[0]], ["lead"], "gather_conv_w", vmem=True)
    conv_w_f = jnp.transpose(conv_w_g, (1, 0, 2)).reshape(3, d)
    w_s3 = w_s[0]
    b_s3 = b_s[0].reshape(HEADS, CHUNK, 1)

    saved, ya, yb, w_in_f, w_oa_f, w_ob_f, w_out_f = _mixer_fwd(
        xb, shards, ln_v_g, ln_v_b, w_s3, b_s3, conv_w_f, conv_b)
    sa, sb, pa, pb, merged = _merge_fwd(xb, ya, yb, w_in_f, w_oa_f, w_ob_f, b_gate)
    dz, gx_direct, g_ln_g, g_ln_b, err2 = _out_ln_loss(merged, w_out_f, x2, tgt2, ln_g, ln_b)
    loss = lax.psum(0.5 * jnp.sum(err2) / d, ("x", "y", "c"))

    da, db, dgate, g_bga, g_bgb = _merge_bwd(dz, w_out_f, sa, sb, pa, pb)
    gw_out = _grad_w(merged, dz, "grad_w_out")
    gw_oa = _grad_w(ya, da, "grad_w_oa")
    gw_ob = _grad_w(yb, db, "grad_w_ob")
    dp, g_ws, g_bs, g_lvg, g_lvb, g_cw, g_cb = _mixer_bwd(
        da, db, w_oa_f, w_ob_f, saved, dgate, ln_v_g, ln_v_b, w_s3, b_s3, conv_w_f)
    gw_in = _grad_w_in(xb, dp)

    grads = [gw_in, gw_oa, gw_ob, gw_out]
    kinds = ["col", "row", "row", "row"]
    lands = _pair_exchange(grads, kinds, "grad_pair_exchange")
    parts = [_pair_sum(g, l, k, c_arr, "grad_pair_sum_%d" % n) for n, (g, l, k) in enumerate(zip(grads, lands, kinds))]
    pieces = [jnp.concatenate([g_bga, g_bgb], axis=1), g_lvg, g_lvb, g_ws, g_bs, g_cw, g_cb, g_ln_g, g_ln_b]
    sizes = [p.size for p in pieces]
    packed = jnp.concatenate([p.reshape(-1, 128) for p in pieces], axis=0)
    grad_x, lands2, gathered = _grad_x(dp, w_in_f, gx_direct, parts, packed)
    grad_x = grad_x[None]

    big = []
    for n, (w, m, v) in enumerate([(w_in, m_w_in, v_w_in), (w_oa, m_w_oa, v_w_oa), (w_ob, m_w_ob, v_w_ob),
                                   (w_out, m_w_out, v_w_out)]):
        big.append([o[None] for o in _sum_adam(parts[n], lands2[n], w[0], m[0], v[0], chip_arr, "sum_adam_%d" % n)])
    (g_w_in, d_w_in, nm_w_in, nv_w_in), (g_w_oa, d_w_oa, nm_w_oa, nv_w_oa), \
        (g_w_ob, d_w_ob, nm_w_ob, nv_w_ob), (g_w_out, d_w_out, nm_w_out, nv_w_out) = big

    total = _small_sum(gathered, "sum_small_grads")
    offs = [0]
    for s in sizes:
        offs.append(offs[-1] + s // 128)
    unpacked = [total[offs[n]:offs[n + 1]] for n in range(len(pieces))]
    g_b_gate = unpacked[0].reshape(b_gate.shape)
    g_ln_v_g = unpacked[1].reshape(ln_v_g.shape)
    g_ln_v_b = unpacked[2].reshape(ln_v_b.shape)
    g_w_s = unpacked[3].reshape(w_s.shape)
    g_b_s = unpacked[4].reshape(b_s.shape)
    g_conv_w = lax.dynamic_slice_in_dim(unpacked[5].reshape(3, d), dev * (d // N_DEV), d // N_DEV, axis=1)[None]
    g_conv_b = unpacked[6].reshape(conv_b.shape)
    g_ln_g2 = unpacked[7].reshape(ln_g.shape)
    g_ln_b2 = unpacked[8].reshape(ln_b.shape)

    small_w = [b_gate, ln_v_g, ln_v_b, w_s, b_s, conv_w, conv_b, ln_g, ln_b]
    small_g = [g_b_gate, g_ln_v_g, g_ln_v_b, g_w_s, g_b_s, g_conv_w, g_conv_b, g_ln_g2, g_ln_b2]
    small_m = [m_b_gate, m_ln_v_g, m_ln_v_b, m_w_s, m_b_s, m_conv_w, m_conv_b, m_ln_g, m_ln_b]
    small_v = [v_b_gate, v_ln_v_g, v_ln_v_b, v_w_s, v_b_s, v_conv_w, v_conv_b, v_ln_g, v_ln_b]

    def flat(a):
        return a.reshape(-1, a.shape[-1])

    res = _small_adam([flat(a) for a in small_w], [flat(a) for a in small_g], [flat(a) for a in small_m],
                      [flat(a) for a in small_v], "adam_small")
    ns = len(small_w)
    d_s = [res[n].reshape(small_w[n].shape) for n in range(ns)]
    nm_s = [res[ns + n].reshape(small_w[n].shape) for n in range(ns)]
    nv_s = [res[2 * ns + n].reshape(small_w[n].shape) for n in range(ns)]

    def ordered(first, small, oa, ob, out):
        return [first] + small[:7] + [oa, ob, out] + small[7:]

    return (loss, grad_x,
            *ordered(g_w_in, small_g, g_w_oa, g_w_ob, g_w_out),
            *ordered(d_w_in, d_s, d_w_oa, d_w_ob, d_w_out),
            *ordered(nm_w_in, nm_s, nm_w_oa, nm_w_ob, nm_w_out),
            *ordered(nv_w_in, nv_s, nv_w_oa, nv_w_ob, nv_w_out))
```

```python
import functools

import jax
import jax.numpy as jnp
from jax import lax
from jax.experimental import pallas as pl
from jax.experimental.pallas import tpu as pltpu

F32 = jnp.float32
BF16 = jnp.bfloat16
MESH = pl.DeviceIdType.MESH

N_DEV = 8
N_CHIP = 4
HEADS = 8
CHUNK = 128
N_IN = 9
N_MIX = 7
N_SAVE = 8
LN_EPS = 1e-5
DN_ALPHA = 2.0 ** 0.25
ADAM_LR = 0.001
ADAM_B1 = 0.9
ADAM_B2 = 0.999
ADAM_EPS = 1e-08
ADAM_WD = 0.01
ADAM_STEP = 10
GELU_C0 = 0.7978845608028654
GELU_C1 = 0.044715
SUBLANES = 8
FIRST_SENDS = 4
SENDS_PER_GROUP = 2
MM_TILE = 1024
MM_DEPTH = 2048
VMEM_LIMIT = 56 << 20


def _cparams(sem):
    return pltpu.CompilerParams(dimension_semantics=sem, vmem_limit_bytes=VMEM_LIMIT)


def _tile(n, want):
    t = min(n, want)
    while n % t:
        t //= 2
    return t


def _gelu(u):
    t = jnp.tanh(GELU_C0 * (u + GELU_C1 * u * u * u))
    return 0.5 * u * (1.0 + t), t


def _gelu_grad(u, t):
    return 0.5 * (1.0 + t) + 0.5 * u * (1.0 - t * t) * (GELU_C0 * (1.0 + 3.0 * GELU_C1 * u * u))


def _silu(z):
    s = jax.nn.sigmoid(z)
    return z * s, s


def _silu_grad(z, s):
    return s * (1.0 + z * (1.0 - s))


def _fold8(a):
    return jnp.sum(a.reshape(a.shape[0] // SUBLANES, SUBLANES, a.shape[1]), axis=0)


def _dot(a, b):
    return jnp.dot(a, b, preferred_element_type=F32)


def _dot_nt(a, b):
    return lax.dot_general(a, b, (((1,), (1,)), ((), ())), preferred_element_type=F32)


def _dot_tn(a, b):
    return lax.dot_general(a, b, (((0,), (0,)), ((), ())), preferred_element_type=F32)


def _tril_mask():
    r = lax.broadcasted_iota(jnp.int32, (CHUNK, CHUNK), 0)
    c = lax.broadcasted_iota(jnp.int32, (CHUNK, CHUNK), 1)
    return c <= r


def _cast_bf16(a, name):
    rows, cols = a.shape
    rb = _tile(rows, 256)

    def body(a_ref, o_ref):
        o_ref[...] = a_ref[...].astype(BF16)

    return pl.pallas_call(
        body, name=name, grid=(rows // rb,),
        in_specs=[pl.BlockSpec((rb, cols), lambda i: (i, 0))],
        out_specs=pl.BlockSpec((rb, cols), lambda i: (i, 0)),
        out_shape=jax.ShapeDtypeStruct((rows, cols), BF16),
        compiler_params=_cparams(("parallel",)),
    )(a)


def _cast_blocks(a, width, name):
    rows, cols = a.shape
    rb = _tile(rows, 256)

    def body(a_ref, o_ref):
        o_ref[...] = a_ref[...].astype(BF16)

    return pl.pallas_call(
        body, name=name, grid=(cols // width, rows // rb),
        in_specs=[pl.BlockSpec((rb, width), lambda j, i: (i, j))],
        out_specs=pl.BlockSpec((None, rb, width), lambda j, i: (j, i, 0)),
        out_shape=jax.ShapeDtypeStruct((cols // width, rows, width), BF16),
        compiler_params=_cparams(("parallel", "parallel")),
    )(a)


def _shard_ref(full, kind, s, n):
    if kind == "col":
        return full.at[:, pl.ds(pl.multiple_of(s * n, 128), n)]
    if kind == "row":
        return full.at[pl.ds(pl.multiple_of(s * n, SUBLANES), n), :]
    return full.at[s]


def _all_gather(shards, kinds, name, vmem):
    nt = len(shards)
    out_shapes = []
    for a, kind in zip(shards, kinds):
        if kind == "col":
            out_shapes.append(jax.ShapeDtypeStruct((a.shape[0], N_DEV * a.shape[1]), a.dtype))
        elif kind == "row":
            out_shapes.append(jax.ShapeDtypeStruct((N_DEV * a.shape[0], a.shape[1]), a.dtype))
        else:
            out_shapes.append(jax.ShapeDtypeStruct((N_DEV,) + a.shape, a.dtype))

    def body(*refs):
        srcs, fulls = refs[:nt], refs[nt:2 * nt]
        send_sems, recv_sems, local_sems = refs[2 * nt:]
        x, y, c = lax.axis_index("x"), lax.axis_index("y"), lax.axis_index("c")
        sibling = (x, y, 1 - c)
        chips = [(1 - x, y), (x, 1 - y), (1 - x, 1 - y)]

        def dev(px, py, pc):
            return 4 * px + 2 * py + pc

        def region(t, s):
            a, kind = shards[t], kinds[t]
            n = a.shape[1] if kind == "col" else a.shape[0]
            return _shard_ref(fulls[t], kind, s, n)

        def copy(t, k, block, to, own=False):
            return pltpu.make_async_remote_copy(
                src_ref=srcs[t] if own else region(t, block), dst_ref=region(t, block),
                send_sem=send_sems.at[7 * t + k], recv_sem=recv_sems.at[7 * t + k],
                device_id=to, device_id_type=MESH)

        me = dev(x, y, c)
        started = []
        for t in range(nt):
            mine = pltpu.make_async_copy(srcs[t], region(t, me), local_sems.at[t])
            mine.start()
            started.append(mine)
        first = []
        for t in range(nt):
            first.append(copy(t, 0, me, sibling, own=True))
            for j, chip in enumerate(chips):
                first.append(copy(t, 1 + j, me, (*chip, c), own=True))
        for cp in first:
            cp.start()
        passed = []
        for t in range(nt):
            for j, chip in enumerate(chips):
                blk = dev(*chip, c)
                copy(t, 1 + j, blk, sibling).wait_recv()
                fwd = copy(t, 4 + j, blk, sibling)
                fwd.start()
                passed.append(fwd)
        for t in range(nt):
            copy(t, 0, dev(x, y, 1 - c), sibling).wait_recv()
            for j, chip in enumerate(chips):
                copy(t, 4 + j, dev(*chip, 1 - c), sibling).wait_recv()
        for cp in first + passed:
            cp.wait_send()
        for mine in started:
            mine.wait()

    space = pltpu.VMEM if vmem else pl.ANY
    return pl.pallas_call(
        body, name=name,
        in_specs=[pl.BlockSpec(memory_space=space)] * nt,
        out_specs=[pl.BlockSpec(memory_space=space)] * nt,
        out_shape=out_shapes,
        scratch_shapes=[pltpu.SemaphoreType.DMA((7 * nt,)), pltpu.SemaphoreType.DMA((7 * nt,)),
                        pltpu.SemaphoreType.DMA((nt,))],
        compiler_params=pltpu.CompilerParams(vmem_limit_bytes=VMEM_LIMIT, has_side_effects=True),
    )(*shards)


def _pair_exchange(grads, kinds, name):
    nt = len(grads)
    shard_shapes = []
    for g, kind in zip(grads, kinds):
        shard_shapes.append((g.shape[0], g.shape[1] // N_DEV) if kind == "col" else (g.shape[0] // N_DEV, g.shape[1]))

    def body(*refs):
        srcs, lands = refs[:nt], refs[nt:2 * nt]
        send_sems, recv_sems = refs[2 * nt:]
        x, y, c = lax.axis_index("x"), lax.axis_index("y"), lax.axis_index("c")
        copies = []
        for t in range(nt):
            n = shard_shapes[t][1] if kinds[t] == "col" else shard_shapes[t][0]
            for k in range(N_CHIP):
                cp = pltpu.make_async_remote_copy(
                    src_ref=_shard_ref(srcs[t], kinds[t], 2 * k + 1 - c, n), dst_ref=lands[t].at[k],
                    send_sem=send_sems.at[N_CHIP * t + k], recv_sem=recv_sems.at[N_CHIP * t + k],
                    device_id=(x, y, 1 - c), device_id_type=MESH)
                cp.start()
                copies.append(cp)
        for cp in copies:
            cp.wait()

    return pl.pallas_call(
        body, name=name,
        in_specs=[pl.BlockSpec(memory_space=pl.ANY)] * nt,
        out_specs=[pl.BlockSpec(memory_space=pl.ANY)] * nt,
        out_shape=[jax.ShapeDtypeStruct((N_CHIP,) + s, g.dtype) for s, g in zip(shard_shapes, grads)],
        scratch_shapes=[pltpu.SemaphoreType.DMA((N_CHIP * nt,)), pltpu.SemaphoreType.DMA((N_CHIP * nt,))],
        compiler_params=pltpu.CompilerParams(has_side_effects=True),
    )(*grads)


def _pair_sum(grad, land, kind, c_arr, name):
    _, r, w = land.shape
    rb = _tile(r, 256)
    nrb = r // rb

    def body(c_ref, g_ref, l_ref, o_ref):
        o_ref[...] = (g_ref[...].astype(F32) + l_ref[...].astype(F32)).astype(o_ref.dtype)

    if kind == "col":
        g_spec = pl.BlockSpec((rb, w), lambda k, i, c: (i, 2 * k + c[0]))
    else:
        g_spec = pl.BlockSpec((rb, w), lambda k, i, c: ((2 * k + c[0]) * nrb + i, 0))
    return pl.pallas_call(
        body, name=name,
        grid_spec=pltpu.PrefetchScalarGridSpec(
            num_scalar_prefetch=1, grid=(N_CHIP, nrb),
            in_specs=[g_spec, pl.BlockSpec((None, rb, w), lambda k, i, c: (k, i, 0))],
            out_specs=pl.BlockSpec((None, rb, w), lambda k, i, c: (k, i, 0))),
        out_shape=jax.ShapeDtypeStruct(land.shape, BF16),
        compiler_params=_cparams(("parallel", "parallel")),
    )(c_arr, grad, land)


def _adam(w, g, m, v):
    m = ADAM_B1 * m + (1.0 - ADAM_B1) * g
    v = ADAM_B2 * v + (1.0 - ADAM_B2) * jnp.square(g)
    m_hat = m / (1.0 - ADAM_B1 ** ADAM_STEP)
    v_hat = v / (1.0 - ADAM_B2 ** ADAM_STEP)
    delta = -ADAM_LR * (m_hat / (jnp.sqrt(v_hat) + ADAM_EPS) + ADAM_WD * w)
    return delta, m, v


def _sum_adam(part, land, w, m, v, chip_arr, name):
    r, wd = w.shape
    rb = _tile(r, 128)

    def body(k_ref, own, r1, r2, r3, w_ref, m_ref, v_ref, g_out, d_out, m_out, v_out):
        g = own[...].astype(F32) + r1[...].astype(F32) + r2[...].astype(F32) + r3[...].astype(F32)
        d, mn, vn = _adam(w_ref[...], g, m_ref[...], v_ref[...])
        g_out[...] = g
        d_out[...] = d
        m_out[...] = mn
        v_out[...] = vn

    def slot(off):
        return pl.BlockSpec((None, rb, wd), lambda i, k: ((k[0] + off) % N_CHIP, i, 0))

    plain = pl.BlockSpec((rb, wd), lambda i, k: (i, 0))
    return pl.pallas_call(
        body, name=name,
        grid_spec=pltpu.PrefetchScalarGridSpec(
            num_scalar_prefetch=1, grid=(r // rb,),
            in_specs=[slot(0), slot(1), slot(2), slot(3), plain, plain, plain],
            out_specs=[plain] * 4),
        out_shape=[jax.ShapeDtypeStruct(w.shape, F32)] * 4,
        compiler_params=_cparams(("parallel",)),
    )(chip_arr, part, land, land, land, w, m, v)


def _small_sum(gathered, name):
    _, r, w = gathered.shape

    def body(g_ref, o_ref):
        acc = g_ref[0]
        for d in range(1, N_DEV):
            acc = acc + g_ref[d]
        o_ref[...] = acc

    return pl.pallas_call(body, name=name, out_shape=jax.ShapeDtypeStruct((r, w), F32))(gathered)


def _small_adam(ws, gs, ms, vs, name):
    n = len(ws)

    def body(*refs):
        ins, outs = refs[:4 * n], refs[4 * n:]
        for t in range(n):
            d, mn, vn = _adam(ins[t][...], ins[n + t][...], ins[2 * n + t][...], ins[3 * n + t][...])
            outs[t][...] = d
            outs[n + t][...] = mn
            outs[2 * n + t][...] = vn

    shapes = [jax.ShapeDtypeStruct(w.shape, F32) for w in ws]
    return pl.pallas_call(body, name=name, out_shape=shapes * 3)(*ws, *gs, *ms, *vs)


def _mixer_fwd(xb, shards, ln_v_g, ln_v_b, w_s, b_s3, conv_w, conv_b):
    t_len, d = xb.shape
    hd = d // HEADS
    tm = _tile(t_len, 512)
    nt = t_len // tm
    assert nt >= 2
    bps = shards[0].shape[0]
    w8 = bps * hd
    r8 = shards[1].shape[0]
    nq = N_IN * HEADS
    n_blocks = nq + 3 * N_DEV
    groups = [[("in", HEADS * b + k) for b in range(N_MIX)] for k in range(HEADS)]
    groups.append([("in", q) for q in range(N_MIX * HEADS, nq)] + [(t, s) for t in (1, 2, 3) for s in range(N_DEV)])

    def owner_of(blk):
        return blk[1] // bps if blk[0] == "in" else blk[1]

    send_step = {}
    for s in range(N_DEV):
        mine = [blk for grp in groups for blk in grp if owner_of(blk) == s]
        for pos, blk in enumerate(mine):
            send_step[blk] = 0 if pos < FIRST_SENDS else 1 + (pos - FIRST_SENDS) // SENDS_PER_GROUP
    assert max(send_step.values()) < HEADS
    tail_pass_step = nt - 6 if nt >= 8 else nt - 2

    def body(x_ref, sh_in, sh_oa, sh_ob, sh_out, lng, lnb, ws_ref, bs_ref, cw_ref, cb_ref,
             save_ref, ya_ref, yb_ref, f_in, f_oa, f_ob, f_out, blocks_in,
             hbuf, wbuf, recv_sems, own_sems, fwd_sems, local_sems, load_sems, flat_sem):
        g = pl.program_id(0)
        i = pl.program_id(1)
        x, y, c = lax.axis_index("x"), lax.axis_index("y"), lax.axis_index("c")
        sibling = (x, y, 1 - c)
        chips = [(1 - x, y), (x, 1 - y), (1 - x, 1 - y)]
        shard_refs = (sh_in, sh_oa, sh_ob, sh_out)
        fulls = (f_in, f_oa, f_ob, f_out)

        def tensor(blk):
            return 0 if blk[0] == "in" else blk[0]

        def owner(blk):
            s = owner_of(blk)
            return s // 4, (s // 2) % 2, s % 2

        def bid(blk):
            return blk[1] if blk[0] == "in" else nq + (blk[0] - 1) * N_DEV + blk[1]

        def region(blk):
            if blk[0] == "in":
                return blocks_in.at[blk[1]]
            return fulls[blk[0]].at[pl.ds(blk[1] * r8, r8), :]

        def own_src(blk):
            if blk[0] == "in":
                return sh_in.at[blk[1] % bps]
            return shard_refs[blk[0]]

        def to_flat(blk):
            if blk[0] == "in":
                pltpu.make_async_copy(region(blk), f_in.at[:, pl.ds(blk[1] * hd, hd)], flat_sem).start()

        def rcopy(blk, to, send_sem, own):
            return pltpu.make_async_remote_copy(
                src_ref=own_src(blk) if own else region(blk), dst_ref=region(blk),
                send_sem=send_sem, recv_sem=recv_sems.at[bid(blk)], device_id=to, device_id_type=MESH)

        def send_own(blk):
            ox, oy, oc = owner(blk)

            @pl.when((x == ox) & (y == oy) & (c == oc))
            def _():
                rcopy(blk, sibling, own_sems.at[tensor(blk)], True).start()
                for chip in chips:
                    rcopy(blk, (*chip, c), own_sems.at[tensor(blk)], True).start()

        def pass_on(blk):
            ox, oy, oc = owner(blk)

            @pl.when(((x != ox) | (y != oy)) & (c == oc))
            def _():
                rcopy(blk, sibling, fwd_sems.at[tensor(blk)], False).wait_recv()
                rcopy(blk, sibling, fwd_sems.at[tensor(blk)], False).start()
                to_flat(blk)

        def wait_from_sibling(blk):
            @pl.when(c != owner(blk)[2])
            def _():
                rcopy(blk, sibling, fwd_sems.at[tensor(blk)], False).wait_recv()
                to_flat(blk)

        def local_copy(t):
            me = 4 * x + 2 * y + c
            if t == 0:
                return pltpu.make_async_copy(sh_in, blocks_in.at[pl.ds(bps * me, bps)], local_sems.at[t])
            return pltpu.make_async_copy(shard_refs[t], _shard_ref(fulls[t], "row", me, r8), local_sems.at[t])

        first = (g == 0) & (i == 0)

        @pl.when(first)
        def _():
            for t in range(4):
                local_copy(t).start()
            for grp in groups:
                for blk in grp:
                    if send_step[blk] == 0:
                        send_own(blk)
            for blk in groups[0]:
                pass_on(blk)
            for t in range(4):
                local_copy(t).wait()
            for grp in groups:
                for blk in grp:
                    if blk[0] == "in":
                        ox, oy, oc = owner(blk)

                        @pl.when((x == ox) & (y == oy) & (c == oc))
                        def _(blk=blk):
                            to_flat(blk)

        @pl.when(i == nt - 2)
        def _():
            for k in range(HEADS - 1):
                @pl.when(g == k)
                def _(k=k):
                    for blk in groups[k + 1]:
                        pass_on(blk)

        @pl.when((g == HEADS - 1) & (i == tail_pass_step))
        def _():
            for blk in groups[HEADS]:
                pass_on(blk)

        @pl.when(i == 0)
        def _():
            for k in range(HEADS):
                @pl.when(g == k)
                def _(k=k):
                    if k > 0:
                        for grp in groups:
                            for blk in grp:
                                if send_step[blk] == k:
                                    send_own(blk)
                    for blk in groups[k]:
                        wait_from_sibling(blk)
                    loads = [pltpu.make_async_copy(region(blk), wbuf.at[b], load_sems.at[b])
                             for b, blk in enumerate(groups[k])]
                    for cp in loads:
                        cp.start()
                    for cp in loads:
                        cp.wait()

        xt = x_ref[...]
        u = _dot(xt, wbuf[0])
        v = _dot(xt, wbuf[1])
        z = _dot(xt, wbuf[2])
        save_ref[0] = u.astype(BF16)
        save_ref[1] = v.astype(BF16)
        save_ref[2] = z.astype(BF16)
        gu, _ = _gelu(u)
        gv, _ = _gelu(v)
        mu = jnp.mean(gv, axis=-1, keepdims=True)
        dv = gv - mu
        var = jnp.mean(dv * dv, axis=-1, keepdims=True)
        vn = (dv * lax.rsqrt(var + LN_EPS) * lng[...] + lnb[...]).astype(BF16)
        sz, _ = _silu(z)
        gate = gu * sz
        wm = jnp.where(_tril_mask(), ws_ref[0], 0.0).astype(BF16)
        bs = bs_ref[0]
        for ck in range(tm // CHUNK):
            rows = slice(ck * CHUNK, (ck + 1) * CHUNK)
            mixed = _dot(wm, vn[rows]) + bs
            ya_ref[rows, :] = (gate[rows] * mixed).astype(BF16)

        xbv = _dot(xt, wbuf[3])
        cbv = _dot(xt, wbuf[4])
        bbv = _dot(xt, wbuf[5])
        zbv = _dot(xt, wbuf[6])
        save_ref[3] = xbv.astype(BF16)
        save_ref[4] = cbv.astype(BF16)
        save_ref[5] = bbv.astype(BF16)
        save_ref[6] = zbv.astype(BF16)
        h = cbv * xbv

        @pl.when(i == 0)
        def _():
            hbuf[0:SUBLANES, :] = jnp.zeros((SUBLANES, hd), F32)

        hbuf[SUBLANES:SUBLANES + tm, :] = h
        h1 = hbuf[SUBLANES - 1:SUBLANES - 1 + tm, :]
        h2 = hbuf[SUBLANES - 2:SUBLANES - 2 + tm, :]
        conv = cb_ref[...] + cw_ref[0:1, :] * h2 + cw_ref[1:2, :] * h1 + cw_ref[2:3, :] * h
        hbuf[0:SUBLANES, :] = h[tm - SUBLANES:tm, :]
        save_ref[7] = conv.astype(BF16)
        szb, _ = _silu(zbv)
        yb_ref[...] = (bbv * conv * szb).astype(BF16)

        @pl.when((g == HEADS - 1) & (i == nt - 1))
        def _():
            for blk in groups[HEADS]:
                wait_from_sibling(blk)
            for t in range(4):
                if t == 0:
                    own_all, fwd_all = blocks_in.at[pl.ds(0, 4 * bps)], blocks_in.at[pl.ds(0, 3 * bps)]
                else:
                    own_all, fwd_all = fulls[t].at[pl.ds(0, 4 * r8), :], fulls[t].at[pl.ds(0, 3 * r8), :]
                for ref, sem in ((own_all, own_sems.at[t]), (fwd_all, fwd_sems.at[t])):
                    pltpu.make_async_remote_copy(src_ref=ref, dst_ref=ref, send_sem=sem, recv_sem=sem,
                                                 device_id=sibling, device_id_type=MESH).wait_send()
            pltpu.make_async_copy(blocks_in, blocks_in, flat_sem).wait()

    vec = pl.BlockSpec((1, hd), lambda g, i: (0, g))
    hbm = pl.BlockSpec(memory_space=pl.ANY)
    return pl.pallas_call(
        body, name="mixer_fwd", grid=(HEADS, nt),
        in_specs=[pl.BlockSpec((tm, d), lambda g, i: (i, 0)), hbm, hbm, hbm, hbm,
                  vec, vec,
                  pl.BlockSpec((1, CHUNK, CHUNK), lambda g, i: (g, 0, 0)),
                  pl.BlockSpec((1, CHUNK, 1), lambda g, i: (g, 0, 0)),
                  pl.BlockSpec((3, hd), lambda g, i: (0, g)),
                  vec],
        out_specs=[pl.BlockSpec((N_SAVE, tm, hd), lambda g, i: (0, i, g)),
                   pl.BlockSpec((tm, hd), lambda g, i: (i, g)),
                   pl.BlockSpec((tm, hd), lambda g, i: (i, g)),
                   hbm, hbm, hbm, hbm, hbm],
        out_shape=[jax.ShapeDtypeStruct((N_SAVE, t_len, d), BF16),
                   jax.ShapeDtypeStruct((t_len, d), BF16),
                   jax.ShapeDtypeStruct((t_len, d), BF16),
                   jax.ShapeDtypeStruct((d, N_DEV * w8), BF16),
                   jax.ShapeDtypeStruct((d, d), BF16), jax.ShapeDtypeStruct((d, d), BF16),
                   jax.ShapeDtypeStruct((d, d), BF16),
                   jax.ShapeDtypeStruct((nq, d, hd), BF16)],
        scratch_shapes=[pltpu.VMEM((SUBLANES + tm, hd), F32), pltpu.VMEM((N_MIX, d, hd), BF16),
                        pltpu.SemaphoreType.DMA((n_blocks,)), pltpu.SemaphoreType.DMA((4,)),
                        pltpu.SemaphoreType.DMA((4,)), pltpu.SemaphoreType.DMA((4,)),
                        pltpu.SemaphoreType.DMA((N_MIX,)), pltpu.SemaphoreType.DMA(())],
        compiler_params=_cparams(("arbitrary", "arbitrary")),
    )(xb, *shards, ln_v_g, ln_v_b, w_s, b_s3, conv_w, conv_b)[:7]


def _merge_fwd(xb, ya, yb, w_in, w_oa, w_ob, b_gate):
    t_len, d = xb.shape
    tm = _tile(t_len, 512)
    tn = _tile(d, 512)
    nj = d // tn

    def body(x_ref, ya_ref, yb_ref, wga, wgb, woa, wob, bga, bgb, sa_ref, sb_ref, pa_ref, pb_ref, mg_ref):
        xt = x_ref[...]
        sa = jax.nn.sigmoid(_dot(xt, wga[...]) + bga[...])
        sb = jax.nn.sigmoid(_dot(xt, wgb[...]) + bgb[...])
        pa = _dot(ya_ref[...], woa[...])
        pb = _dot(yb_ref[...], wob[...])
        sa_ref[...] = sa.astype(BF16)
        sb_ref[...] = sb.astype(BF16)
        pa_ref[...] = pa.astype(BF16)
        pb_ref[...] = pb.astype(BF16)
        mg_ref[...] = (sa * pa + sb * pb).astype(BF16)

    row = pl.BlockSpec((tm, d), lambda j, i: (i, 0))
    out = pl.BlockSpec((tm, tn), lambda j, i: (i, j))
    return pl.pallas_call(
        body, name="merge_fwd", grid=(nj, t_len // tm),
        in_specs=[row, row, row,
                  pl.BlockSpec((d, tn), lambda j, i: (0, 7 * nj + j)),
                  pl.BlockSpec((d, tn), lambda j, i: (0, 8 * nj + j)),
                  pl.BlockSpec((d, tn), lambda j, i: (0, j)),
                  pl.BlockSpec((d, tn), lambda j, i: (0, j)),
                  pl.BlockSpec((1, tn), lambda j, i: (0, j)),
                  pl.BlockSpec((1, tn), lambda j, i: (0, nj + j))],
        out_specs=[out] * 5,
        out_shape=[jax.ShapeDtypeStruct((t_len, d), BF16)] * 5,
        compiler_params=_cparams(("parallel", "arbitrary")),
    )(xb, ya, yb, w_in, w_in, w_oa, w_ob, b_gate, b_gate)


def _out_ln_loss(merged, w_out, x, target, ln_g, ln_b):
    t_len, d = x.shape
    tm = _tile(t_len, 256)
    nt = t_len // tm

    def body(mg_ref, w_ref, x_ref, t_ref, g_ref, b_ref, dz_ref, gx_ref, glg_ref, glb_ref, ls_ref, a_g, a_b, a_l):
        i = pl.program_id(0)

        @pl.when(i == 0)
        def _():
            a_g[...] = jnp.zeros_like(a_g)
            a_b[...] = jnp.zeros_like(a_b)
            a_l[...] = jnp.zeros_like(a_l)

        zres = DN_ALPHA * x_ref[...] + _dot(mg_ref[...], w_ref[...])
        mu = jnp.mean(zres, axis=-1, keepdims=True)
        dc = zres - mu
        var = jnp.mean(dc * dc, axis=-1, keepdims=True)
        rstd = lax.rsqrt(var + LN_EPS)
        xhat = dc * rstd
        g = g_ref[...]
        err = xhat * g + b_ref[...] - t_ref[...]
        dy = err * (1.0 / d)
        a_l[...] += _fold8(err * err)
        a_g[...] += _fold8(dy * xhat)
        a_b[...] += _fold8(dy)
        dxh = dy * g
        m1 = jnp.mean(dxh, axis=-1, keepdims=True)
        m2 = jnp.mean(dxh * xhat, axis=-1, keepdims=True)
        dz = rstd * (dxh - m1 - xhat * m2)
        dz_ref[...] = dz.astype(BF16)
        gx_ref[...] = DN_ALPHA * dz

        @pl.when(i == nt - 1)
        def _():
            glg_ref[...] = jnp.sum(a_g[...], axis=0, keepdims=True)
            glb_ref[...] = jnp.sum(a_b[...], axis=0, keepdims=True)
            ls_ref[...] = jnp.sum(a_l[...], axis=0, keepdims=True)

    row = pl.BlockSpec((tm, d), lambda i: (i, 0))
    vec = pl.BlockSpec((1, d), lambda i: (0, 0))
    return pl.pallas_call(
        body, name="out_ln_loss", grid=(nt,),
        in_specs=[row, pl.BlockSpec((d, d), lambda i: (0, 0)), row, row, vec, vec],
        out_specs=[row, row, vec, vec, vec],
        out_shape=[jax.ShapeDtypeStruct((t_len, d), BF16), jax.ShapeDtypeStruct((t_len, d), F32)]
        + [jax.ShapeDtypeStruct((1, d), F32)] * 3,
        scratch_shapes=[pltpu.VMEM((SUBLANES, d), F32)] * 3,
        compiler_params=_cparams(("arbitrary",)),
    )(merged, w_out, x, target, ln_g, ln_b)


def _merge_bwd(dz, w_out, sa, sb, pa, pb):
    t_len, d = dz.shape
    tm = _tile(t_len, 512)
    tn = _tile(d, 512)
    nt = t_len // tm

    def body(dz_ref, w_ref, sa_ref, sb_ref, pa_ref, pb_ref, da_ref, db_ref, dg_ref, ga_ref, gb_ref, acc_a, acc_b):
        i = pl.program_id(1)

        @pl.when(i == 0)
        def _():
            acc_a[...] = jnp.zeros_like(acc_a)
            acc_b[...] = jnp.zeros_like(acc_b)

        dm = _dot_nt(dz_ref[...], w_ref[...])
        sa = sa_ref[...].astype(F32)
        sb = sb_ref[...].astype(F32)
        da = dm * sa
        db = dm * sb
        da_ref[...] = da.astype(BF16)
        db_ref[...] = db.astype(BF16)
        dga = da * pa_ref[...].astype(F32) * (1.0 - sa)
        dgb = db * pb_ref[...].astype(F32) * (1.0 - sb)
        dg_ref[0] = dga.astype(BF16)
        dg_ref[1] = dgb.astype(BF16)
        acc_a[...] += _fold8(dga)
        acc_b[...] += _fold8(dgb)

        @pl.when(i == nt - 1)
        def _():
            ga_ref[...] = jnp.sum(acc_a[...], axis=0, keepdims=True)
            gb_ref[...] = jnp.sum(acc_b[...], axis=0, keepdims=True)

    blk = pl.BlockSpec((tm, tn), lambda j, i: (i, j))
    vec = pl.BlockSpec((1, tn), lambda j, i: (0, j))
    return pl.pallas_call(
        body, name="merge_bwd", grid=(d // tn, nt),
        in_specs=[pl.BlockSpec((tm, d), lambda j, i: (i, 0)), pl.BlockSpec((tn, d), lambda j, i: (j, 0)),
                  blk, blk, blk, blk],
        out_specs=[blk, blk, pl.BlockSpec((2, tm, tn), lambda j, i: (0, i, j)), vec, vec],
        out_shape=[jax.ShapeDtypeStruct((t_len, d), BF16)] * 2 + [jax.ShapeDtypeStruct((2, t_len, d), BF16)]
        + [jax.ShapeDtypeStruct((1, d), F32)] * 2,
        scratch_shapes=[pltpu.VMEM((SUBLANES, tn), F32)] * 2,
        compiler_params=_cparams(("parallel", "arbitrary")),
    )(dz, w_out, sa, sb, pa, pb)


def _grad_w(a, b, name):
    t_len, m = a.shape
    n = b.shape[1]
    tm, tn, tk = _tile(m, MM_TILE), _tile(n, MM_TILE), _tile(t_len, MM_DEPTH)
    nk = t_len // tk

    def body(a_ref, b_ref, o_ref, acc):
        k = pl.program_id(2)

        @pl.when(k == 0)
        def _():
            acc[...] = jnp.zeros_like(acc)

        acc[...] += _dot_tn(a_ref[...], b_ref[...])

        @pl.when(k == nk - 1)
        def _():
            o_ref[...] = acc[...].astype(BF16)

    return pl.pallas_call(
        body, name=name, grid=(m // tm, n // tn, nk),
        in_specs=[pl.BlockSpec((tk, tm), lambda i, j, k: (k, i)), pl.BlockSpec((tk, tn), lambda i, j, k: (k, j))],
        out_specs=pl.BlockSpec((tm, tn), lambda i, j, k: (i, j)),
        out_shape=jax.ShapeDtypeStruct((m, n), BF16),
        scratch_shapes=[pltpu.VMEM((tm, tn), F32)],
        compiler_params=_cparams(("parallel", "parallel", "arbitrary")),
    )(a, b)


def _mixer_bwd(da, db, w_oa, w_ob, saved, dgate, ln_v_g, ln_v_b, w_s, b_s3, conv_w):
    t_len, d = da.shape
    hd = d // HEADS
    tm = _tile(t_len, 512)
    nt = t_len // tm

    def body(da_ref, db_ref, woa, wob, sv, dgt, lng, lnb, ws_ref, bs_ref, cw_ref,
             dp_ref, gws_ref, gbs_ref, glg_ref, glb_ref, gcw_ref, gcb_ref,
             dbuf, carry, a_ws, a_bs, a_lg, a_lb, a_c0, a_c1, a_c2, a_cb):
        i = pl.program_id(1)

        @pl.when(i == 0)
        def _():
            carry[...] = jnp.zeros_like(carry)
            for a in (a_ws, a_bs, a_lg, a_lb, a_c0, a_c1, a_c2, a_cb):
                a[...] = jnp.zeros_like(a)

        dya = _dot_nt(da_ref[...], woa[...])
        u = sv[0].astype(F32)
        v = sv[1].astype(F32)
        z = sv[2].astype(F32)
        gu, tu = _gelu(u)
        gv, tv = _gelu(v)
        mu = jnp.mean(gv, axis=-1, keepdims=True)
        dvc = gv - mu
        var = jnp.mean(dvc * dvc, axis=-1, keepdims=True)
        rstd = lax.rsqrt(var + LN_EPS)
        vhat = dvc * rstd
        g = lng[...]
        vn = (vhat * g + lnb[...]).astype(BF16)
        sz, s = _silu(z)
        t1 = dya * sz
        dmixed = t1 * gu
        dmixed_b = dmixed.astype(BF16)
        wm = jnp.where(_tril_mask(), ws_ref[0], 0.0).astype(BF16)
        bs = bs_ref[0]
        dvn_parts = []
        mixed_parts = []
        gws = a_ws[...]
        gbs = a_bs[...]
        for ck in range(tm // CHUNK):
            rows = slice(ck * CHUNK, (ck + 1) * CHUNK)
            mixed_parts.append(_dot(wm, vn[rows]) + bs)
            gws = gws + _dot_nt(dmixed_b[rows], vn[rows])
            gbs = gbs + dmixed[rows]
            dvn_parts.append(_dot_tn(wm, dmixed_b[rows]))
        a_ws[...] = gws
        a_bs[...] = gbs
        mixed = jnp.concatenate(mixed_parts, axis=0)
        dvn = jnp.concatenate(dvn_parts, axis=0)
        dp_ref[0] = (t1 * mixed * _gelu_grad(u, tu)).astype(BF16)
        dp_ref[2] = (dya * gu * mixed * _silu_grad(z, s)).astype(BF16)
        a_lg[...] += _fold8(dvn * vhat)
        a_lb[...] += _fold8(dvn)
        dvh = dvn * g
        m1 = jnp.mean(dvh, axis=-1, keepdims=True)
        m2 = jnp.mean(dvh * vhat, axis=-1, keepdims=True)
        dp_ref[1] = (rstd * (dvh - m1 - vhat * m2) * _gelu_grad(v, tv)).astype(BF16)

        dyb = _dot_nt(db_ref[...], wob[...])
        xbv = sv[3].astype(F32)
        cbv = sv[4].astype(F32)
        bbv = sv[5].astype(F32)
        zbv = sv[6].astype(F32)
        conv = sv[7].astype(F32)
        szb, sb = _silu(zbv)
        dp_ref[5] = (dyb * conv * szb).astype(BF16)
        dp_ref[6] = (dyb * bbv * conv * _silu_grad(zbv, sb)).astype(BF16)
        dconv = dyb * bbv * szb
        dbuf[0:tm, :] = dconv
        dbuf[tm:tm + SUBLANES, :] = carry[...]
        dc1 = dbuf[1:tm + 1, :]
        dc2 = dbuf[2:tm + 2, :]
        carry[...] = dconv[0:SUBLANES, :]
        h = cbv * xbv
        a_c2[...] += _fold8(dconv * h)
        a_c1[...] += _fold8(dc1 * h)
        a_c0[...] += _fold8(dc2 * h)
        a_cb[...] += _fold8(dconv)
        dh = cw_ref[2:3, :] * dconv + cw_ref[1:2, :] * dc1 + cw_ref[0:1, :] * dc2
        dp_ref[3] = (dh * cbv).astype(BF16)
        dp_ref[4] = (dh * xbv).astype(BF16)
        dp_ref[7] = dgt[0]
        dp_ref[8] = dgt[1]

        @pl.when(i == nt - 1)
        def _():
            gws_ref[0] = jnp.where(_tril_mask(), a_ws[...], 0.0)
            gbs_ref[0] = jnp.sum(a_bs[...], axis=1, keepdims=True)
            glg_ref[...] = jnp.sum(a_lg[...], axis=0, keepdims=True)
            glb_ref[...] = jnp.sum(a_lb[...], axis=0, keepdims=True)
            gcw_ref[0:1, :] = jnp.sum(a_c0[...], axis=0, keepdims=True)
            gcw_ref[1:2, :] = jnp.sum(a_c1[...], axis=0, keepdims=True)
            gcw_ref[2:3, :] = jnp.sum(a_c2[...], axis=0, keepdims=True)
            gcb_ref[...] = jnp.sum(a_cb[...], axis=0, keepdims=True)

    def rev(i):
        return nt - 1 - i

    row = pl.BlockSpec((tm, d), lambda g, i: (rev(i), 0))
    wrow = pl.BlockSpec((hd, d), lambda g, i: (g, 0))
    vec = pl.BlockSpec((1, hd), lambda g, i: (0, g))
    acc8 = pltpu.VMEM((SUBLANES, hd), F32)
    return pl.pallas_call(
        body, name="mixer_bwd", grid=(HEADS, nt),
        in_specs=[row, row, wrow, wrow,
                  pl.BlockSpec((N_SAVE, tm, hd), lambda g, i: (0, rev(i), g)),
                  pl.BlockSpec((2, tm, hd), lambda g, i: (0, rev(i), g)),
                  vec, vec,
                  pl.BlockSpec((1, CHUNK, CHUNK), lambda g, i: (g, 0, 0)),
                  pl.BlockSpec((1, CHUNK, 1), lambda g, i: (g, 0, 0)),
                  pl.BlockSpec((3, hd), lambda g, i: (0, g))],
        out_specs=[pl.BlockSpec((N_IN, tm, hd), lambda g, i: (0, rev(i), g)),
                   pl.BlockSpec((1, CHUNK, CHUNK), lambda g, i: (g, 0, 0)),
                   pl.BlockSpec((1, CHUNK, 1), lambda g, i: (g, 0, 0)),
                   vec, vec,
                   pl.BlockSpec((3, hd), lambda g, i: (0, g)),
                   vec],
        out_shape=[jax.ShapeDtypeStruct((N_IN, t_len, d), BF16),
                   jax.ShapeDtypeStruct((HEADS, CHUNK, CHUNK), F32),
                   jax.ShapeDtypeStruct((HEADS, CHUNK, 1), F32),
                   jax.ShapeDtypeStruct((1, d), F32), jax.ShapeDtypeStruct((1, d), F32),
                   jax.ShapeDtypeStruct((3, d), F32), jax.ShapeDtypeStruct((1, d), F32)],
        scratch_shapes=[pltpu.VMEM((tm + SUBLANES, hd), F32), acc8,
                        pltpu.VMEM((CHUNK, CHUNK), F32), pltpu.VMEM((CHUNK, hd), F32),
                        acc8, acc8, acc8, acc8, acc8, acc8],
        compiler_params=_cparams(("parallel", "arbitrary")),
    )(da, db, w_oa, w_ob, saved, dgate, ln_v_g, ln_v_b, w_s, b_s3, conv_w)


def _grad_w_in(xb, dp):
    t_len, d = xb.shape
    tm, tn, tk = _tile(d, MM_TILE), _tile(d, MM_TILE), _tile(t_len, MM_DEPTH)
    nj = d // tn
    nk = t_len // tk

    def body(a_ref, b_ref, o_ref, acc):
        k = pl.program_id(2)

        @pl.when(k == 0)
        def _():
            acc[...] = jnp.zeros_like(acc)

        acc[...] += _dot_tn(a_ref[...], b_ref[...])

        @pl.when(k == nk - 1)
        def _():
            o_ref[...] = acc[...].astype(BF16)

    return pl.pallas_call(
        body, name="grad_w_in", grid=(d // tm, N_IN * nj, nk),
        in_specs=[pl.BlockSpec((tk, tm), lambda i, j, k: (k, i)),
                  pl.BlockSpec((None, tk, tn), lambda i, j, k: (j // nj, k, j % nj))],
        out_specs=pl.BlockSpec((tm, tn), lambda i, j, k: (i, j)),
        out_shape=jax.ShapeDtypeStruct((d, N_IN * d), BF16),
        scratch_shapes=[pltpu.VMEM((tm, tn), F32)],
        compiler_params=_cparams(("parallel", "parallel", "arbitrary")),
    )(xb, dp)


def _grad_x(dp, w_in, gx_direct, parts, packed):
    _, t_len, d = dp.shape
    tm, tn, tk = _tile(t_len, MM_TILE), _tile(d, MM_TILE), _tile(d, MM_DEPTH)
    nkb = d // tk
    nk = N_IN * nkb
    ni, nj = t_len // tm, d // tn
    n_parts = len(parts)

    def body(a_ref, b_ref, r_ref, *rest):
        srcs = rest[:n_parts]
        pk_ref = rest[n_parts]
        o_ref = rest[n_parts + 1]
        lands = rest[n_parts + 2:2 * n_parts + 2]
        gath = rest[2 * n_parts + 2]
        acc, send_sems, recv_sems, pk_send, pk_recv, pk_local = rest[2 * n_parts + 3:]
        i, j, k = pl.program_id(0), pl.program_id(1), pl.program_id(2)
        x, y, c = lax.axis_index("x"), lax.axis_index("y"), lax.axis_index("c")
        my_chip = 2 * x + y
        me = 4 * x + 2 * y + c
        chips = [(1 - x, y), (x, 1 - y), (1 - x, 1 - y)]

        def part_copy(t, n):
            px, py = chips[n]
            return pltpu.make_async_remote_copy(
                src_ref=srcs[t].at[2 * px + py], dst_ref=lands[t].at[my_chip],
                send_sem=send_sems.at[3 * t + n], recv_sem=recv_sems.at[3 * t + n],
                device_id=(px, py, c), device_id_type=MESH)

        def part_landing(t, n):
            px, py = chips[n]
            return pltpu.make_async_remote_copy(
                src_ref=srcs[t].at[my_chip], dst_ref=lands[t].at[2 * px + py],
                send_sem=send_sems.at[3 * t + n], recv_sem=recv_sems.at[3 * t + n],
                device_id=(px, py, c), device_id_type=MESH)

        def pk_copy(s):
            return pltpu.make_async_remote_copy(
                src_ref=pk_ref, dst_ref=gath.at[me], send_sem=pk_send, recv_sem=pk_recv.at[me],
                device_id=(s // 4, (s // 2) % 2, s % 2), device_id_type=MESH)

        @pl.when((i == 0) & (j == 0) & (k == 0))
        def _():
            for t in range(n_parts):
                for n in range(3):
                    part_copy(t, n).start()
            pltpu.make_async_copy(pk_ref, gath.at[me], pk_local).start()
            for s in range(N_DEV):
                @pl.when(s != me)
                def _(s=s):
                    pk_copy(s).start()

        @pl.when(k == 0)
        def _():
            acc[...] = r_ref[...]

        acc[...] += _dot_nt(a_ref[...], b_ref[...])

        @pl.when(k == nk - 1)
        def _():
            o_ref[...] = acc[...]

        @pl.when((i == ni - 1) & (j == nj - 1) & (k == nk - 1))
        def _():
            for t in range(n_parts):
                for n in range(3):
                    part_landing(t, n).wait_recv()
            for t in range(n_parts):
                for n in range(3):
                    part_copy(t, n).wait_send()
            for s in range(N_DEV):
                @pl.when(s != me)
                def _(s=s):
                    pltpu.make_async_remote_copy(
                        src_ref=pk_ref, dst_ref=gath.at[s], send_sem=pk_send, recv_sem=pk_recv.at[s],
                        device_id=(s // 4, (s // 2) % 2, s % 2), device_id_type=MESH).wait_recv()
            seven = gath.at[pl.ds(0, N_DEV - 1)]
            pltpu.make_async_remote_copy(src_ref=seven, dst_ref=seven, send_sem=pk_send, recv_sem=pk_send,
                                         device_id=(x, y, 1 - c), device_id_type=MESH).wait_send()
            pltpu.make_async_copy(pk_ref, gath.at[me], pk_local).wait()

    hbm = pl.BlockSpec(memory_space=pl.ANY)
    outs = pl.pallas_call(
        body, name="grad_x", grid=(ni, nj, nk),
        in_specs=[pl.BlockSpec((None, tm, tk), lambda i, j, k: (k // nkb, i, k % nkb)),
                  pl.BlockSpec((tn, tk), lambda i, j, k: (j, k)),
                  pl.BlockSpec((tm, tn), lambda i, j, k: (i, j))] + [hbm] * (n_parts + 1),
        out_specs=[pl.BlockSpec((tm, tn), lambda i, j, k: (i, j))] + [hbm] * (n_parts + 1),
        out_shape=[jax.ShapeDtypeStruct((t_len, d), F32)] + [jax.ShapeDtypeStruct(p.shape, p.dtype) for p in parts]
        + [jax.ShapeDtypeStruct((N_DEV,) + packed.shape, packed.dtype)],
        scratch_shapes=[pltpu.VMEM((tm, tn), F32),
                        pltpu.SemaphoreType.DMA((3 * n_parts,)), pltpu.SemaphoreType.DMA((3 * n_parts,)),
                        pltpu.SemaphoreType.DMA(()), pltpu.SemaphoreType.DMA((N_DEV,)), pltpu.SemaphoreType.DMA(())],
        compiler_params=_cparams(("arbitrary", "arbitrary", "arbitrary")),
    )(dp, w_in, gx_direct, *parts, packed)
    return outs[0], list(outs[1:1 + n_parts]), outs[1 + n_parts]


def kernel(x, w_in, b_gate, ln_v_g, ln_v_b, w_s, b_s, conv_w, conv_b, w_oa, w_ob, w_out, ln_g, ln_b, loss_target, m_w_in, m_b_gate, m_ln_v_g, m_ln_v_b, m_w_s, m_b_s, m_conv_w, m_conv_b, m_w_oa, m_w_ob, m_w_out, m_ln_g, m_ln_b, v_w_in, v_b_gate, v_ln_v_g, v_ln_v_b, v_w_s, v_b_s, v_conv_w, v_conv_b, v_w_oa, v_w_ob, v_w_out, v_ln_g, v_ln_b):
    _, t_len, d = x.shape
    assert d % (HEADS * 128) == 0 and t_len % CHUNK == 0 and w_in.shape[2] * N_DEV == N_IN * d
    x2 = x[0]
    tgt2 = loss_target[0]
    c_arr = lax.axis_index("c").astype(jnp.int32).reshape(1)
    chip_arr = (2 * lax.axis_index("x") + lax.axis_index("y")).astype(jnp.int32).reshape(1)
    dev = 4 * lax.axis_index("x") + 2 * lax.axis_index("y") + lax.axis_index("c")

    xb = _cast_bf16(x2, "cast_x")
    shards = [_cast_blocks(w_in[0], d // HEADS, "cast_w_in"), _cast_bf16(w_oa[0], "cast_w_oa"),
              _cast_bf16(w_ob[0], "cast_w_ob"), _cast_bf16(w_out[0], "cast_w_out")]
    (conv_w_g,) = _all_gather([conv_w[0]], ["lead"], "gather_conv_w", vmem=True)
    conv_w_f = jnp.transpose(conv_w_g, (1, 0, 2)).reshape(3, d)
    w_s3 = w_s[0]
    b_s3 = b_s[0].reshape(HEADS, CHUNK, 1)

    saved, ya, yb, w_in_f, w_oa_f, w_ob_f, w_out_f = _mixer_fwd(
        xb, shards, ln_v_g, ln_v_b, w_s3, b_s3, conv_w_f, conv_b)
    sa, sb, pa, pb, merged = _merge_fwd(xb, ya, yb, w_in_f, w_oa_f, w_ob_f, b_gate)
    dz, gx_direct, g_ln_g, g_ln_b, err2 = _out_ln_loss(merged, w_out_f, x2, tgt2, ln_g, ln_b)
    loss = lax.psum(0.5 * jnp.sum(err2) / d, ("x", "y", "c"))

    da, db, dgate, g_bga, g_bgb = _merge_bwd(dz, w_out_f, sa, sb, pa, pb)
    gw_out = _grad_w(merged, dz, "grad_w_out")
    gw_oa = _grad_w(ya, da, "grad_w_oa")
    gw_ob = _grad_w(yb, db, "grad_w_ob")
    dp, g_ws, g_bs, g_lvg, g_lvb, g_cw, g_cb = _mixer_bwd(
        da, db, w_oa_f, w_ob_f, saved, dgate, ln_v_g, ln_v_b, w_s3, b_s3, conv_w_f)
    gw_in = _grad_w_in(xb, dp)

    grads = [gw_in, gw_oa, gw_ob, gw_out]
    kinds = ["col", "row", "row", "row"]
    lands = _pair_exchange(grads, kinds, "grad_pair_exchange")
    parts = [_pair_sum(g, l, k, c_arr, "grad_pair_sum_%d" % n) for n, (g, l, k) in enumerate(zip(grads, lands, kinds))]
    pieces = [jnp.concatenate([g_bga, g_bgb], axis=1), g_lvg, g_lvb, g_ws, g_bs, g_cw, g_cb, g_ln_g, g_ln_b]
    sizes = [p.size for p in pieces]
    packed = jnp.concatenate([p.reshape(-1, 128) for p in pieces], axis=0)
    grad_x, lands2, gathered = _grad_x(dp, w_in_f, gx_direct, parts, packed)
    grad_x = grad_x[None]

    big = []
    for n, (w, m, v) in enumerate([(w_in, m_w_in, v_w_in), (w_oa, m_w_oa, v_w_oa), (w_ob, m_w_ob, v_w_ob),
                                   (w_out, m_w_out, v_w_out)]):
        big.append([o[None] for o in _sum_adam(parts[n], lands2[n], w[0], m[0], v[0], chip_arr, "sum_adam_%d" % n)])
    (g_w_in, d_w_in, nm_w_in, nv_w_in), (g_w_oa, d_w_oa, nm_w_oa, nv_w_oa), \
        (g_w_ob, d_w_ob, nm_w_ob, nv_w_ob), (g_w_out, d_w_out, nm_w_out, nv_w_out) = big

    total = _small_sum(gathered, "sum_small_grads")
    offs = [0]
    for s in sizes:
        offs.append(offs[-1] + s // 128)
    unpacked = [total[offs[n]:offs[n + 1]] for n in range(len(pieces))]
    g_b_gate = unpacked[0].reshape(b_gate.shape)
    g_ln_v_g = unpacked[1].reshape(ln_v_g.shape)
    g_ln_v_b = unpacked[2].reshape(ln_v_b.shape)
    g_w_s = unpacked[3].reshape(w_s.shape)
    g_b_s = unpacked[4].reshape(b_s.shape)
    g_conv_w = lax.dynamic_slice_in_dim(unpacked[5].reshape(3, d), dev * (d // N_DEV), d // N_DEV, axis=1)[None]
    g_conv_b = unpacked[6].reshape(conv_b.shape)
    g_ln_g2 = unpacked[7].reshape(ln_g.shape)
    g_ln_b2 = unpacked[8].reshape(ln_b.shape)

    small_w = [b_gate, ln_v_g, ln_v_b, w_s, b_s, conv_w, conv_b, ln_g, ln_b]
    small_g = [g_b_gate, g_ln_v_g, g_ln_v_b, g_w_s, g_b_s, g_conv_w, g_conv_b, g_ln_g2, g_ln_b2]
    small_m = [m_b_gate, m_ln_v_g, m_ln_v_b, m_w_s, m_b_s, m_conv_w, m_conv_b, m_ln_g, m_ln_b]
    small_v = [v_b_gate, v_ln_v_g, v_ln_v_b, v_w_s, v_b_s, v_conv_w, v_conv_b, v_ln_g, v_ln_b]

    def flat(a):
        return a.reshape(-1, a.shape[-1])

    res = _small_adam([flat(a) for a in small_w], [flat(a) for a in small_g], [flat(a) for a in small_m],
                      [flat(a) for a in small_v], "adam_small")
    ns = len(small_w)
    d_s = [res[n].reshape(small_w[n].shape) for n in range(ns)]
    nm_s = [res[ns + n].reshape(small_w[n].shape) for n in range(ns)]
    nv_s = [res[2 * ns + n].reshape(small_w[n].shape) for n in range(ns)]

    def ordered(first, small, oa, ob, out):
        return [first] + small[:7] + [oa, ob, out] + small[7:]

    return (loss, grad_x,
            *ordered(g_w_in, small_g, g_w_oa, g_w_ob, g_w_out),
            *ordered(d_w_in, d_s, d_w_oa, d_w_ob, d_w_out),
            *ordered(nm_w_in, nm_s, nm_w_oa, nm_w_ob, nm_w_out),
            *ordered(nv_w_in, nv_s, nv_w_oa, nv_w_ob, nv_w_out))
```

```python
import functools

import jax
import jax.numpy as jnp
from jax import lax
from jax.experimental import pallas as pl
from jax.experimental.pallas import tpu as pltpu

F32 = jnp.float32
BF16 = jnp.bfloat16
MESH = pl.DeviceIdType.MESH

N_DEV = 8
N_CHIP = 4
HEADS = 8
CHUNK = 128
N_IN = 9
N_MIX = 7
N_SAVE = 8
LN_EPS = 1e-5
DN_ALPHA = 2.0 ** 0.25
ADAM_LR = 0.001
ADAM_B1 = 0.9
ADAM_B2 = 0.999
ADAM_EPS = 1e-08
ADAM_WD = 0.01
ADAM_STEP = 10
GELU_C0 = 0.7978845608028654
GELU_C1 = 0.044715
SUBLANES = 8
FIRST_SENDS = 4
SENDS_PER_GROUP = 2
MM_TILE = 1024
MM_DEPTH = 2048
VMEM_LIMIT = 56 << 20


def _cparams(sem):
    return pltpu.CompilerParams(dimension_semantics=sem, vmem_limit_bytes=VMEM_LIMIT)


def _tile(n, want):
    t = min(n, want)
    while n % t:
        t //= 2
    return t


def _gelu(u):
    t = jnp.tanh(GELU_C0 * (u + GELU_C1 * u * u * u))
    return 0.5 * u * (1.0 + t), t


def _gelu_grad(u, t):
    return 0.5 * (1.0 + t) + 0.5 * u * (1.0 - t * t) * (GELU_C0 * (1.0 + 3.0 * GELU_C1 * u * u))


def _silu(z):
    s = jax.nn.sigmoid(z)
    return z * s, s


def _silu_grad(z, s):
    return s * (1.0 + z * (1.0 - s))


def _fold8(a):
    return jnp.sum(a.reshape(a.shape[0] // SUBLANES, SUBLANES, a.shape[1]), axis=0)


def _dot(a, b):
    return jnp.dot(a, b, preferred_element_type=F32)


def _dot_nt(a, b):
    return lax.dot_general(a, b, (((1,), (1,)), ((), ())), preferred_element_type=F32)


def _dot_tn(a, b):
    return lax.dot_general(a, b, (((0,), (0,)), ((), ())), preferred_element_type=F32)


def _tril_mask():
    r = lax.broadcasted_iota(jnp.int32, (CHUNK, CHUNK), 0)
    c = lax.broadcasted_iota(jnp.int32, (CHUNK, CHUNK), 1)
    return c <= r


def _cast_bf16(a, name):
    rows, cols = a.shape
    rb = _tile(rows, 256)

    def body(a_ref, o_ref):
        o_ref[...] = a_ref[...].astype(BF16)

    return pl.pallas_call(
        body, name=name, grid=(rows // rb,),
        in_specs=[pl.BlockSpec((rb, cols), lambda i: (i, 0))],
        out_specs=pl.BlockSpec((rb, cols), lambda i: (i, 0)),
        out_shape=jax.ShapeDtypeStruct((rows, cols), BF16),
        compiler_params=_cparams(("parallel",)),
    )(a)


def _cast_transposed(a, name):
    rows, cols = a.shape
    tb = _tile(rows, 256)
    tc = _tile(cols, 256)

    def body(a_ref, o_ref):
        o_ref[...] = a_ref[...].T.astype(BF16)

    return pl.pallas_call(
        body, name=name, grid=(cols // tc, rows // tb),
        in_specs=[pl.BlockSpec((tb, tc), lambda j, i: (i, j))],
        out_specs=pl.BlockSpec((tc, tb), lambda j, i: (j, i)),
        out_shape=jax.ShapeDtypeStruct((cols, rows), BF16),
        compiler_params=_cparams(("parallel", "parallel")),
    )(a)


def _shard_ref(full, kind, s, n):
    if kind == "col":
        return full.at[:, pl.ds(pl.multiple_of(s * n, 128), n)]
    if kind == "row":
        return full.at[pl.ds(pl.multiple_of(s * n, SUBLANES), n), :]
    return full.at[s]


def _all_gather(shards, kinds, name, vmem):
    nt = len(shards)
    out_shapes = []
    for a, kind in zip(shards, kinds):
        if kind == "col":
            out_shapes.append(jax.ShapeDtypeStruct((a.shape[0], N_DEV * a.shape[1]), a.dtype))
        elif kind == "row":
            out_shapes.append(jax.ShapeDtypeStruct((N_DEV * a.shape[0], a.shape[1]), a.dtype))
        else:
            out_shapes.append(jax.ShapeDtypeStruct((N_DEV,) + a.shape, a.dtype))

    def body(*refs):
        srcs, fulls = refs[:nt], refs[nt:2 * nt]
        send_sems, recv_sems, local_sems = refs[2 * nt:]
        x, y, c = lax.axis_index("x"), lax.axis_index("y"), lax.axis_index("c")
        sibling = (x, y, 1 - c)
        chips = [(1 - x, y), (x, 1 - y), (1 - x, 1 - y)]

        def dev(px, py, pc):
            return 4 * px + 2 * py + pc

        def region(t, s):
            a, kind = shards[t], kinds[t]
            n = a.shape[1] if kind == "col" else a.shape[0]
            return _shard_ref(fulls[t], kind, s, n)

        def copy(t, k, block, to, own=False):
            return pltpu.make_async_remote_copy(
                src_ref=srcs[t] if own else region(t, block), dst_ref=region(t, block),
                send_sem=send_sems.at[7 * t + k], recv_sem=recv_sems.at[7 * t + k],
                device_id=to, device_id_type=MESH)

        me = dev(x, y, c)
        started = []
        for t in range(nt):
            mine = pltpu.make_async_copy(srcs[t], region(t, me), local_sems.at[t])
            mine.start()
            started.append(mine)
        first = []
        for t in range(nt):
            first.append(copy(t, 0, me, sibling, own=True))
            for j, chip in enumerate(chips):
                first.append(copy(t, 1 + j, me, (*chip, c), own=True))
        for cp in first:
            cp.start()
        passed = []
        for t in range(nt):
            for j, chip in enumerate(chips):
                blk = dev(*chip, c)
                copy(t, 1 + j, blk, sibling).wait_recv()
                fwd = copy(t, 4 + j, blk, sibling)
                fwd.start()
                passed.append(fwd)
        for t in range(nt):
            copy(t, 0, dev(x, y, 1 - c), sibling).wait_recv()
            for j, chip in enumerate(chips):
                copy(t, 4 + j, dev(*chip, 1 - c), sibling).wait_recv()
        for cp in first + passed:
            cp.wait_send()
        for mine in started:
            mine.wait()

    space = pltpu.VMEM if vmem else pl.ANY
    return pl.pallas_call(
        body, name=name,
        in_specs=[pl.BlockSpec(memory_space=space)] * nt,
        out_specs=[pl.BlockSpec(memory_space=space)] * nt,
        out_shape=out_shapes,
        scratch_shapes=[pltpu.SemaphoreType.DMA((7 * nt,)), pltpu.SemaphoreType.DMA((7 * nt,)),
                        pltpu.SemaphoreType.DMA((nt,))],
        compiler_params=pltpu.CompilerParams(vmem_limit_bytes=VMEM_LIMIT, has_side_effects=True),
    )(*shards)


def _pair_exchange(grads, kinds, name):
    nt = len(grads)
    shard_shapes = []
    for g, kind in zip(grads, kinds):
        shard_shapes.append((g.shape[0], g.shape[1] // N_DEV) if kind == "col" else (g.shape[0] // N_DEV, g.shape[1]))

    def body(*refs):
        srcs, lands = refs[:nt], refs[nt:2 * nt]
        send_sems, recv_sems = refs[2 * nt:]
        x, y, c = lax.axis_index("x"), lax.axis_index("y"), lax.axis_index("c")
        copies = []
        for t in range(nt):
            n = shard_shapes[t][1] if kinds[t] == "col" else shard_shapes[t][0]
            for k in range(N_CHIP):
                cp = pltpu.make_async_remote_copy(
                    src_ref=_shard_ref(srcs[t], kinds[t], 2 * k + 1 - c, n), dst_ref=lands[t].at[k],
                    send_sem=send_sems.at[N_CHIP * t + k], recv_sem=recv_sems.at[N_CHIP * t + k],
                    device_id=(x, y, 1 - c), device_id_type=MESH)
                cp.start()
                copies.append(cp)
        for cp in copies:
            cp.wait()

    return pl.pallas_call(
        body, name=name,
        in_specs=[pl.BlockSpec(memory_space=pl.ANY)] * nt,
        out_specs=[pl.BlockSpec(memory_space=pl.ANY)] * nt,
        out_shape=[jax.ShapeDtypeStruct((N_CHIP,) + s, g.dtype) for s, g in zip(shard_shapes, grads)],
        scratch_shapes=[pltpu.SemaphoreType.DMA((N_CHIP * nt,)), pltpu.SemaphoreType.DMA((N_CHIP * nt,))],
        compiler_params=pltpu.CompilerParams(has_side_effects=True),
    )(*grads)


def _pair_sum(grad, land, kind, c_arr, name):
    _, r, w = land.shape
    rb = _tile(r, 256)
    nrb = r // rb

    def body(c_ref, g_ref, l_ref, o_ref):
        o_ref[...] = (g_ref[...].astype(F32) + l_ref[...].astype(F32)).astype(o_ref.dtype)

    if kind == "col":
        g_spec = pl.BlockSpec((rb, w), lambda k, i, c: (i, 2 * k + c[0]))
    else:
        g_spec = pl.BlockSpec((rb, w), lambda k, i, c: ((2 * k + c[0]) * nrb + i, 0))
    return pl.pallas_call(
        body, name=name,
        grid_spec=pltpu.PrefetchScalarGridSpec(
            num_scalar_prefetch=1, grid=(N_CHIP, nrb),
            in_specs=[g_spec, pl.BlockSpec((None, rb, w), lambda k, i, c: (k, i, 0))],
            out_specs=pl.BlockSpec((None, rb, w), lambda k, i, c: (k, i, 0))),
        out_shape=jax.ShapeDtypeStruct(land.shape, BF16),
        compiler_params=_cparams(("parallel", "parallel")),
    )(c_arr, grad, land)


def _adam(w, g, m, v):
    m = ADAM_B1 * m + (1.0 - ADAM_B1) * g
    v = ADAM_B2 * v + (1.0 - ADAM_B2) * jnp.square(g)
    m_hat = m / (1.0 - ADAM_B1 ** ADAM_STEP)
    v_hat = v / (1.0 - ADAM_B2 ** ADAM_STEP)
    delta = -ADAM_LR * (m_hat / (jnp.sqrt(v_hat) + ADAM_EPS) + ADAM_WD * w)
    return delta, m, v


def _sum_adam(part, land, w, m, v, chip_arr, name):
    r, wd = w.shape
    rb = _tile(r, 128)

    def body(k_ref, own, r1, r2, r3, w_ref, m_ref, v_ref, g_out, d_out, m_out, v_out):
        g = own[...].astype(F32) + r1[...].astype(F32) + r2[...].astype(F32) + r3[...].astype(F32)
        d, mn, vn = _adam(w_ref[...], g, m_ref[...], v_ref[...])
        g_out[...] = g
        d_out[...] = d
        m_out[...] = mn
        v_out[...] = vn

    def slot(off):
        return pl.BlockSpec((None, rb, wd), lambda i, k: ((k[0] + off) % N_CHIP, i, 0))

    plain = pl.BlockSpec((rb, wd), lambda i, k: (i, 0))
    return pl.pallas_call(
        body, name=name,
        grid_spec=pltpu.PrefetchScalarGridSpec(
            num_scalar_prefetch=1, grid=(r // rb,),
            in_specs=[slot(0), slot(1), slot(2), slot(3), plain, plain, plain],
            out_specs=[plain] * 4),
        out_shape=[jax.ShapeDtypeStruct(w.shape, F32)] * 4,
        compiler_params=_cparams(("parallel",)),
    )(chip_arr, part, land, land, land, w, m, v)


def _small_sum(gathered, name):
    _, r, w = gathered.shape

    def body(g_ref, o_ref):
        acc = g_ref[0]
        for d in range(1, N_DEV):
            acc = acc + g_ref[d]
        o_ref[...] = acc

    return pl.pallas_call(body, name=name, out_shape=jax.ShapeDtypeStruct((r, w), F32))(gathered)


def _small_adam(ws, gs, ms, vs, name):
    n = len(ws)

    def body(*refs):
        ins, outs = refs[:4 * n], refs[4 * n:]
        for t in range(n):
            d, mn, vn = _adam(ins[t][...], ins[n + t][...], ins[2 * n + t][...], ins[3 * n + t][...])
            outs[t][...] = d
            outs[n + t][...] = mn
            outs[2 * n + t][...] = vn

    shapes = [jax.ShapeDtypeStruct(w.shape, F32) for w in ws]
    return pl.pallas_call(body, name=name, out_shape=shapes * 3)(*ws, *gs, *ms, *vs)


def _mixer_fwd(xb, shards, ln_v_g, ln_v_b, w_s, b_s3, conv_w, conv_b):
    t_len, d = xb.shape
    hd = d // HEADS
    tm = _tile(t_len, 512)
    nt = t_len // tm
    assert nt >= 2
    w8 = shards[0].shape[0]
    bps = w8 // hd
    r8 = shards[1].shape[0]
    nq = N_IN * HEADS
    n_blocks = nq + 3 * N_DEV
    groups = [[("in", HEADS * b + k) for b in range(N_MIX)] for k in range(HEADS)]
    groups.append([("in", q) for q in range(N_MIX * HEADS, nq)] + [(t, s) for t in (1, 2, 3) for s in range(N_DEV)])

    def owner_of(blk):
        return blk[1] // bps if blk[0] == "in" else blk[1]

    send_step = {}
    for s in range(N_DEV):
        mine = [blk for grp in groups for blk in grp if owner_of(blk) == s]
        for pos, blk in enumerate(mine):
            send_step[blk] = 0 if pos < FIRST_SENDS else 1 + (pos - FIRST_SENDS) // SENDS_PER_GROUP
    assert max(send_step.values()) < HEADS
    tail_pass_step = nt - 6 if nt >= 8 else nt - 2

    def body(x_ref, sh_in, sh_oa, sh_ob, sh_out, lng, lnb, ws_ref, bs_ref, cw_ref, cb_ref,
             save_ref, ya_ref, yb_ref, f_in, f_oa, f_ob, f_out,
             hbuf, wbuf, recv_sems, own_sems, fwd_sems, local_sems, load_sems):
        g = pl.program_id(0)
        i = pl.program_id(1)
        x, y, c = lax.axis_index("x"), lax.axis_index("y"), lax.axis_index("c")
        sibling = (x, y, 1 - c)
        chips = [(1 - x, y), (x, 1 - y), (1 - x, 1 - y)]
        shard_refs = (sh_in, sh_oa, sh_ob, sh_out)
        fulls = (f_in, f_oa, f_ob, f_out)

        def tensor(blk):
            return 0 if blk[0] == "in" else blk[0]

        def owner(blk):
            s = owner_of(blk)
            return s // 4, (s // 2) % 2, s % 2

        def bid(blk):
            return blk[1] if blk[0] == "in" else nq + (blk[0] - 1) * N_DEV + blk[1]

        def region(blk):
            if blk[0] == "in":
                return f_in.at[pl.ds(blk[1] * hd, hd), :]
            return fulls[blk[0]].at[pl.ds(blk[1] * r8, r8), :]

        def own_src(blk):
            if blk[0] == "in":
                return sh_in.at[pl.ds((blk[1] % bps) * hd, hd), :]
            return shard_refs[blk[0]]

        def rcopy(blk, to, send_sem, own):
            return pltpu.make_async_remote_copy(
                src_ref=own_src(blk) if own else region(blk), dst_ref=region(blk),
                send_sem=send_sem, recv_sem=recv_sems.at[bid(blk)], device_id=to, device_id_type=MESH)

        def send_own(blk):
            ox, oy, oc = owner(blk)

            @pl.when((x == ox) & (y == oy) & (c == oc))
            def _():
                rcopy(blk, sibling, own_sems.at[tensor(blk)], True).start()
                for chip in chips:
                    rcopy(blk, (*chip, c), own_sems.at[tensor(blk)], True).start()

        def pass_on(blk):
            ox, oy, oc = owner(blk)

            @pl.when(((x != ox) | (y != oy)) & (c == oc))
            def _():
                rcopy(blk, sibling, fwd_sems.at[tensor(blk)], False).wait_recv()
                rcopy(blk, sibling, fwd_sems.at[tensor(blk)], False).start()

        def wait_from_sibling(blk):
            @pl.when(c != owner(blk)[2])
            def _():
                rcopy(blk, sibling, fwd_sems.at[tensor(blk)], False).wait_recv()

        def local_copy(t):
            me = 4 * x + 2 * y + c
            return pltpu.make_async_copy(shard_refs[t], _shard_ref(fulls[t], "row", me, w8 if t == 0 else r8),
                                         local_sems.at[t])

        first = (g == 0) & (i == 0)

        @pl.when(first)
        def _():
            for t in range(4):
                local_copy(t).start()
            for grp in groups:
                for blk in grp:
                    if send_step[blk] == 0:
                        send_own(blk)
            for blk in groups[0]:
                pass_on(blk)
            for t in range(4):
                local_copy(t).wait()

        @pl.when(i == nt - 2)
        def _():
            for k in range(HEADS - 1):
                @pl.when(g == k)
                def _(k=k):
                    for blk in groups[k + 1]:
                        pass_on(blk)

        @pl.when((g == HEADS - 1) & (i == tail_pass_step))
        def _():
            for blk in groups[HEADS]:
                pass_on(blk)

        @pl.when(i == 0)
        def _():
            for k in range(HEADS):
                @pl.when(g == k)
                def _(k=k):
                    if k > 0:
                        for grp in groups:
                            for blk in grp:
                                if send_step[blk] == k:
                                    send_own(blk)
                    for blk in groups[k]:
                        wait_from_sibling(blk)
                    loads = [pltpu.make_async_copy(region(blk), wbuf.at[b], load_sems.at[b])
                             for b, blk in enumerate(groups[k])]
                    for cp in loads:
                        cp.start()
                    for cp in loads:
                        cp.wait()

        xt = x_ref[...]
        u = _dot_nt(xt, wbuf[0])
        v = _dot_nt(xt, wbuf[1])
        z = _dot_nt(xt, wbuf[2])
        save_ref[0] = u.astype(BF16)
        save_ref[1] = v.astype(BF16)
        save_ref[2] = z.astype(BF16)
        gu, _ = _gelu(u)
        gv, _ = _gelu(v)
        mu = jnp.mean(gv, axis=-1, keepdims=True)
        dv = gv - mu
        var = jnp.mean(dv * dv, axis=-1, keepdims=True)
        vn = (dv * lax.rsqrt(var + LN_EPS) * lng[...] + lnb[...]).astype(BF16)
        sz, _ = _silu(z)
        gate = gu * sz
        wm = jnp.where(_tril_mask(), ws_ref[0], 0.0).astype(BF16)
        bs = bs_ref[0]
        for ck in range(tm // CHUNK):
            rows = slice(ck * CHUNK, (ck + 1) * CHUNK)
            mixed = _dot(wm, vn[rows]) + bs
            ya_ref[rows, :] = (gate[rows] * mixed).astype(BF16)

        xbv = _dot_nt(xt, wbuf[3])
        cbv = _dot_nt(xt, wbuf[4])
        bbv = _dot_nt(xt, wbuf[5])
        zbv = _dot_nt(xt, wbuf[6])
        save_ref[3] = xbv.astype(BF16)
        save_ref[4] = cbv.astype(BF16)
        save_ref[5] = bbv.astype(BF16)
        save_ref[6] = zbv.astype(BF16)
        h = cbv * xbv

        @pl.when(i == 0)
        def _():
            hbuf[0:SUBLANES, :] = jnp.zeros((SUBLANES, hd), F32)

        hbuf[SUBLANES:SUBLANES + tm, :] = h
        h1 = hbuf[SUBLANES - 1:SUBLANES - 1 + tm, :]
        h2 = hbuf[SUBLANES - 2:SUBLANES - 2 + tm, :]
        conv = cb_ref[...] + cw_ref[0:1, :] * h2 + cw_ref[1:2, :] * h1 + cw_ref[2:3, :] * h
        hbuf[0:SUBLANES, :] = h[tm - SUBLANES:tm, :]
        save_ref[7] = conv.astype(BF16)
        szb, _ = _silu(zbv)
        yb_ref[...] = (bbv * conv * szb).astype(BF16)

        @pl.when((g == HEADS - 1) & (i == nt - 1))
        def _():
            for blk in groups[HEADS]:
                wait_from_sibling(blk)
            for t in range(4):
                n = w8 if t == 0 else r8
                own_all, fwd_all = fulls[t].at[pl.ds(0, 4 * n), :], fulls[t].at[pl.ds(0, 3 * n), :]
                for ref, sem in ((own_all, own_sems.at[t]), (fwd_all, fwd_sems.at[t])):
                    pltpu.make_async_remote_copy(src_ref=ref, dst_ref=ref, send_sem=sem, recv_sem=sem,
                                                 device_id=sibling, device_id_type=MESH).wait_send()

    vec = pl.BlockSpec((1, hd), lambda g, i: (0, g))
    hbm = pl.BlockSpec(memory_space=pl.ANY)
    return pl.pallas_call(
        body, name="mixer_fwd", grid=(HEADS, nt),
        in_specs=[pl.BlockSpec((tm, d), lambda g, i: (i, 0)), hbm, hbm, hbm, hbm,
                  vec, vec,
                  pl.BlockSpec((1, CHUNK, CHUNK), lambda g, i: (g, 0, 0)),
                  pl.BlockSpec((1, CHUNK, 1), lambda g, i: (g, 0, 0)),
                  pl.BlockSpec((3, hd), lambda g, i: (0, g)),
                  vec],
        out_specs=[pl.BlockSpec((N_SAVE, tm, hd), lambda g, i: (0, i, g)),
                   pl.BlockSpec((tm, hd), lambda g, i: (i, g)),
                   pl.BlockSpec((tm, hd), lambda g, i: (i, g)),
                   hbm, hbm, hbm, hbm],
        out_shape=[jax.ShapeDtypeStruct((N_SAVE, t_len, d), BF16),
                   jax.ShapeDtypeStruct((t_len, d), BF16),
                   jax.ShapeDtypeStruct((t_len, d), BF16),
                   jax.ShapeDtypeStruct((N_DEV * w8, d), BF16),
                   jax.ShapeDtypeStruct((d, d), BF16), jax.ShapeDtypeStruct((d, d), BF16),
                   jax.ShapeDtypeStruct((d, d), BF16)],
        scratch_shapes=[pltpu.VMEM((SUBLANES + tm, hd), F32), pltpu.VMEM((N_MIX, hd, d), BF16),
                        pltpu.SemaphoreType.DMA((n_blocks,)), pltpu.SemaphoreType.DMA((4,)),
                        pltpu.SemaphoreType.DMA((4,)), pltpu.SemaphoreType.DMA((4,)),
                        pltpu.SemaphoreType.DMA((N_MIX,))],
        compiler_params=_cparams(("arbitrary", "arbitrary")),
    )(xb, *shards, ln_v_g, ln_v_b, w_s, b_s3, conv_w, conv_b)


def _merge_fwd(xb, ya, yb, w_in, w_oa, w_ob, b_gate):
    t_len, d = xb.shape
    tm = _tile(t_len, 512)
    tn = _tile(d, 512)
    nj = d // tn

    def body(x_ref, ya_ref, yb_ref, wga, wgb, woa, wob, bga, bgb, sa_ref, sb_ref, pa_ref, pb_ref, mg_ref):
        xt = x_ref[...]
        sa = jax.nn.sigmoid(_dot_nt(xt, wga[...]) + bga[...])
        sb = jax.nn.sigmoid(_dot_nt(xt, wgb[...]) + bgb[...])
        pa = _dot(ya_ref[...], woa[...])
        pb = _dot(yb_ref[...], wob[...])
        sa_ref[...] = sa.astype(BF16)
        sb_ref[...] = sb.astype(BF16)
        pa_ref[...] = pa.astype(BF16)
        pb_ref[...] = pb.astype(BF16)
        mg_ref[...] = (sa * pa + sb * pb).astype(BF16)

    row = pl.BlockSpec((tm, d), lambda j, i: (i, 0))
    out = pl.BlockSpec((tm, tn), lambda j, i: (i, j))
    return pl.pallas_call(
        body, name="merge_fwd", grid=(nj, t_len // tm),
        in_specs=[row, row, row,
                  pl.BlockSpec((tn, d), lambda j, i: (7 * nj + j, 0)),
                  pl.BlockSpec((tn, d), lambda j, i: (8 * nj + j, 0)),
                  pl.BlockSpec((d, tn), lambda j, i: (0, j)),
                  pl.BlockSpec((d, tn), lambda j, i: (0, j)),
                  pl.BlockSpec((1, tn), lambda j, i: (0, j)),
                  pl.BlockSpec((1, tn), lambda j, i: (0, nj + j))],
        out_specs=[out] * 5,
        out_shape=[jax.ShapeDtypeStruct((t_len, d), BF16)] * 5,
        compiler_params=_cparams(("parallel", "arbitrary")),
    )(xb, ya, yb, w_in, w_in, w_oa, w_ob, b_gate, b_gate)


def _out_ln_loss(merged, w_out, x, target, ln_g, ln_b):
    t_len, d = x.shape
    tm = _tile(t_len, 256)
    nt = t_len // tm

    def body(mg_ref, w_ref, x_ref, t_ref, g_ref, b_ref, dz_ref, gx_ref, glg_ref, glb_ref, ls_ref, a_g, a_b, a_l):
        i = pl.program_id(0)

        @pl.when(i == 0)
        def _():
            a_g[...] = jnp.zeros_like(a_g)
            a_b[...] = jnp.zeros_like(a_b)
            a_l[...] = jnp.zeros_like(a_l)

        zres = DN_ALPHA * x_ref[...] + _dot(mg_ref[...], w_ref[...])
        mu = jnp.mean(zres, axis=-1, keepdims=True)
        dc = zres - mu
        var = jnp.mean(dc * dc, axis=-1, keepdims=True)
        rstd = lax.rsqrt(var + LN_EPS)
        xhat = dc * rstd
        g = g_ref[...]
        err = xhat * g + b_ref[...] - t_ref[...]
        dy = err * (1.0 / d)
        a_l[...] += _fold8(err * err)
        a_g[...] += _fold8(dy * xhat)
        a_b[...] += _fold8(dy)
        dxh = dy * g
        m1 = jnp.mean(dxh, axis=-1, keepdims=True)
        m2 = jnp.mean(dxh * xhat, axis=-1, keepdims=True)
        dz = rstd * (dxh - m1 - xhat * m2)
        dz_ref[...] = dz.astype(BF16)
        gx_ref[...] = DN_ALPHA * dz

        @pl.when(i == nt - 1)
        def _():
            glg_ref[...] = jnp.sum(a_g[...], axis=0, keepdims=True)
            glb_ref[...] = jnp.sum(a_b[...], axis=0, keepdims=True)
            ls_ref[...] = jnp.sum(a_l[...], axis=0, keepdims=True)

    row = pl.BlockSpec((tm, d), lambda i: (i, 0))
    vec = pl.BlockSpec((1, d), lambda i: (0, 0))
    return pl.pallas_call(
        body, name="out_ln_loss", grid=(nt,),
        in_specs=[row, pl.BlockSpec((d, d), lambda i: (0, 0)), row, row, vec, vec],
        out_specs=[row, row, vec, vec, vec],
        out_shape=[jax.ShapeDtypeStruct((t_len, d), BF16), jax.ShapeDtypeStruct((t_len, d), F32)]
        + [jax.ShapeDtypeStruct((1, d), F32)] * 3,
        scratch_shapes=[pltpu.VMEM((SUBLANES, d), F32)] * 3,
        compiler_params=_cparams(("arbitrary",)),
    )(merged, w_out, x, target, ln_g, ln_b)


def _merge_bwd(dz, w_out, sa, sb, pa, pb):
    t_len, d = dz.shape
    tm = _tile(t_len, 512)
    tn = _tile(d, 512)
    nt = t_len // tm

    def body(dz_ref, w_ref, sa_ref, sb_ref, pa_ref, pb_ref, da_ref, db_ref, dg_ref, ga_ref, gb_ref, acc_a, acc_b):
        i = pl.program_id(1)

        @pl.when(i == 0)
        def _():
            acc_a[...] = jnp.zeros_like(acc_a)
            acc_b[...] = jnp.zeros_like(acc_b)

        dm = _dot_nt(dz_ref[...], w_ref[...])
        sa = sa_ref[...].astype(F32)
        sb = sb_ref[...].astype(F32)
        da = dm * sa
        db = dm * sb
        da_ref[...] = da.astype(BF16)
        db_ref[...] = db.astype(BF16)
        dga = da * pa_ref[...].astype(F32) * (1.0 - sa)
        dgb = db * pb_ref[...].astype(F32) * (1.0 - sb)
        dg_ref[0] = dga.astype(BF16)
        dg_ref[1] = dgb.astype(BF16)
        acc_a[...] += _fold8(dga)
        acc_b[...] += _fold8(dgb)

        @pl.when(i == nt - 1)
        def _():
            ga_ref[...] = jnp.sum(acc_a[...], axis=0, keepdims=True)
            gb_ref[...] = jnp.sum(acc_b[...], axis=0, keepdims=True)

    blk = pl.BlockSpec((tm, tn), lambda j, i: (i, j))
    vec = pl.BlockSpec((1, tn), lambda j, i: (0, j))
    return pl.pallas_call(
        body, name="merge_bwd", grid=(d // tn, nt),
        in_specs=[pl.BlockSpec((tm, d), lambda j, i: (i, 0)), pl.BlockSpec((tn, d), lambda j, i: (j, 0)),
                  blk, blk, blk, blk],
        out_specs=[blk, blk, pl.BlockSpec((2, tm, tn), lambda j, i: (0, i, j)), vec, vec],
        out_shape=[jax.ShapeDtypeStruct((t_len, d), BF16)] * 2 + [jax.ShapeDtypeStruct((2, t_len, d), BF16)]
        + [jax.ShapeDtypeStruct((1, d), F32)] * 2,
        scratch_shapes=[pltpu.VMEM((SUBLANES, tn), F32)] * 2,
        compiler_params=_cparams(("parallel", "arbitrary")),
    )(dz, w_out, sa, sb, pa, pb)


def _grad_w(a, b, name):
    t_len, m = a.shape
    n = b.shape[1]
    tm, tn, tk = _tile(m, MM_TILE), _tile(n, MM_TILE), _tile(t_len, MM_DEPTH)
    nk = t_len // tk

    def body(a_ref, b_ref, o_ref, acc):
        k = pl.program_id(2)

        @pl.when(k == 0)
        def _():
            acc[...] = jnp.zeros_like(acc)

        acc[...] += _dot_tn(a_ref[...], b_ref[...])

        @pl.when(k == nk - 1)
        def _():
            o_ref[...] = acc[...].astype(BF16)

    return pl.pallas_call(
        body, name=name, grid=(m // tm, n // tn, nk),
        in_specs=[pl.BlockSpec((tk, tm), lambda i, j, k: (k, i)), pl.BlockSpec((tk, tn), lambda i, j, k: (k, j))],
        out_specs=pl.BlockSpec((tm, tn), lambda i, j, k: (i, j)),
        out_shape=jax.ShapeDtypeStruct((m, n), BF16),
        scratch_shapes=[pltpu.VMEM((tm, tn), F32)],
        compiler_params=_cparams(("parallel", "parallel", "arbitrary")),
    )(a, b)


def _mixer_bwd(da, db, w_oa, w_ob, saved, dgate, ln_v_g, ln_v_b, w_s, b_s3, conv_w):
    t_len, d = da.shape
    hd = d // HEADS
    tm = _tile(t_len, 512)
    nt = t_len // tm

    def body(da_ref, db_ref, woa, wob, sv, dgt, lng, lnb, ws_ref, bs_ref, cw_ref,
             dp_ref, gws_ref, gbs_ref, glg_ref, glb_ref, gcw_ref, gcb_ref,
             dbuf, carry, a_ws, a_bs, a_lg, a_lb, a_c0, a_c1, a_c2, a_cb):
        i = pl.program_id(1)

        @pl.when(i == 0)
        def _():
            carry[...] = jnp.zeros_like(carry)
            for a in (a_ws, a_bs, a_lg, a_lb, a_c0, a_c1, a_c2, a_cb):
                a[...] = jnp.zeros_like(a)

        dya = _dot_nt(da_ref[...], woa[...])
        u = sv[0].astype(F32)
        v = sv[1].astype(F32)
        z = sv[2].astype(F32)
        gu, tu = _gelu(u)
        gv, tv = _gelu(v)
        mu = jnp.mean(gv, axis=-1, keepdims=True)
        dvc = gv - mu
        var = jnp.mean(dvc * dvc, axis=-1, keepdims=True)
        rstd = lax.rsqrt(var + LN_EPS)
        vhat = dvc * rstd
        g = lng[...]
        vn = (vhat * g + lnb[...]).astype(BF16)
        sz, s = _silu(z)
        t1 = dya * sz
        dmixed = t1 * gu
        dmixed_b = dmixed.astype(BF16)
        wm = jnp.where(_tril_mask(), ws_ref[0], 0.0).astype(BF16)
        bs = bs_ref[0]
        dvn_parts = []
        mixed_parts = []
        gws = a_ws[...]
        gbs = a_bs[...]
        for ck in range(tm // CHUNK):
            rows = slice(ck * CHUNK, (ck + 1) * CHUNK)
            mixed_parts.append(_dot(wm, vn[rows]) + bs)
            gws = gws + _dot_nt(dmixed_b[rows], vn[rows])
            gbs = gbs + dmixed[rows]
            dvn_parts.append(_dot_tn(wm, dmixed_b[rows]))
        a_ws[...] = gws
        a_bs[...] = gbs
        mixed = jnp.concatenate(mixed_parts, axis=0)
        dvn = jnp.concatenate(dvn_parts, axis=0)
        dp_ref[0] = (t1 * mixed * _gelu_grad(u, tu)).astype(BF16)
        dp_ref[2] = (dya * gu * mixed * _silu_grad(z, s)).astype(BF16)
        a_lg[...] += _fold8(dvn * vhat)
        a_lb[...] += _fold8(dvn)
        dvh = dvn * g
        m1 = jnp.mean(dvh, axis=-1, keepdims=True)
        m2 = jnp.mean(dvh * vhat, axis=-1, keepdims=True)
        dp_ref[1] = (rstd * (dvh - m1 - vhat * m2) * _gelu_grad(v, tv)).astype(BF16)

        dyb = _dot_nt(db_ref[...], wob[...])
        xbv = sv[3].astype(F32)
        cbv = sv[4].astype(F32)
        bbv = sv[5].astype(F32)
        zbv = sv[6].astype(F32)
        conv = sv[7].astype(F32)
        szb, sb = _silu(zbv)
        dp_ref[5] = (dyb * conv * szb).astype(BF16)
        dp_ref[6] = (dyb * bbv * conv * _silu_grad(zbv, sb)).astype(BF16)
        dconv = dyb * bbv * szb
        dbuf[0:tm, :] = dconv
        dbuf[tm:tm + SUBLANES, :] = carry[...]
        dc1 = dbuf[1:tm + 1, :]
        dc2 = dbuf[2:tm + 2, :]
        carry[...] = dconv[0:SUBLANES, :]
        h = cbv * xbv
        a_c2[...] += _fold8(dconv * h)
        a_c1[...] += _fold8(dc1 * h)
        a_c0[...] += _fold8(dc2 * h)
        a_cb[...] += _fold8(dconv)
        dh = cw_ref[2:3, :] * dconv + cw_ref[1:2, :] * dc1 + cw_ref[0:1, :] * dc2
        dp_ref[3] = (dh * cbv).astype(BF16)
        dp_ref[4] = (dh * xbv).astype(BF16)
        dp_ref[7] = dgt[0]
        dp_ref[8] = dgt[1]

        @pl.when(i == nt - 1)
        def _():
            gws_ref[0] = jnp.where(_tril_mask(), a_ws[...], 0.0)
            gbs_ref[0] = jnp.sum(a_bs[...], axis=1, keepdims=True)
            glg_ref[...] = jnp.sum(a_lg[...], axis=0, keepdims=True)
            glb_ref[...] = jnp.sum(a_lb[...], axis=0, keepdims=True)
            gcw_ref[0:1, :] = jnp.sum(a_c0[...], axis=0, keepdims=True)
            gcw_ref[1:2, :] = jnp.sum(a_c1[...], axis=0, keepdims=True)
            gcw_ref[2:3, :] = jnp.sum(a_c2[...], axis=0, keepdims=True)
            gcb_ref[...] = jnp.sum(a_cb[...], axis=0, keepdims=True)

    def rev(i):
        return nt - 1 - i

    row = pl.BlockSpec((tm, d), lambda g, i: (rev(i), 0))
    wrow = pl.BlockSpec((hd, d), lambda g, i: (g, 0))
    vec = pl.BlockSpec((1, hd), lambda g, i: (0, g))
    acc8 = pltpu.VMEM((SUBLANES, hd), F32)
    return pl.pallas_call(
        body, name="mixer_bwd", grid=(HEADS, nt),
        in_specs=[row, row, wrow, wrow,
                  pl.BlockSpec((N_SAVE, tm, hd), lambda g, i: (0, rev(i), g)),
                  pl.BlockSpec((2, tm, hd), lambda g, i: (0, rev(i), g)),
                  vec, vec,
                  pl.BlockSpec((1, CHUNK, CHUNK), lambda g, i: (g, 0, 0)),
                  pl.BlockSpec((1, CHUNK, 1), lambda g, i: (g, 0, 0)),
                  pl.BlockSpec((3, hd), lambda g, i: (0, g))],
        out_specs=[pl.BlockSpec((N_IN, tm, hd), lambda g, i: (0, rev(i), g)),
                   pl.BlockSpec((1, CHUNK, CHUNK), lambda g, i: (g, 0, 0)),
                   pl.BlockSpec((1, CHUNK, 1), lambda g, i: (g, 0, 0)),
                   vec, vec,
                   pl.BlockSpec((3, hd), lambda g, i: (0, g)),
                   vec],
        out_shape=[jax.ShapeDtypeStruct((N_IN, t_len, d), BF16),
                   jax.ShapeDtypeStruct((HEADS, CHUNK, CHUNK), F32),
                   jax.ShapeDtypeStruct((HEADS, CHUNK, 1), F32),
                   jax.ShapeDtypeStruct((1, d), F32), jax.ShapeDtypeStruct((1, d), F32),
                   jax.ShapeDtypeStruct((3, d), F32), jax.ShapeDtypeStruct((1, d), F32)],
        scratch_shapes=[pltpu.VMEM((tm + SUBLANES, hd), F32), acc8,
                        pltpu.VMEM((CHUNK, CHUNK), F32), pltpu.VMEM((CHUNK, hd), F32),
                        acc8, acc8, acc8, acc8, acc8, acc8],
        compiler_params=_cparams(("parallel", "arbitrary")),
    )(da, db, w_oa, w_ob, saved, dgate, ln_v_g, ln_v_b, w_s, b_s3, conv_w)


def _grad_w_in(xb, dp):
    t_len, d = xb.shape
    tm, tn, tk = _tile(d, MM_TILE), _tile(d, MM_TILE), _tile(t_len, MM_DEPTH)
    nj = d // tn
    nk = t_len // tk

    def body(a_ref, b_ref, o_ref, acc):
        k = pl.program_id(2)

        @pl.when(k == 0)
        def _():
            acc[...] = jnp.zeros_like(acc)

        acc[...] += _dot_tn(a_ref[...], b_ref[...])

        @pl.when(k == nk - 1)
        def _():
            o_ref[...] = acc[...].astype(BF16)

    return pl.pallas_call(
        body, name="grad_w_in", grid=(d // tm, N_IN * nj, nk),
        in_specs=[pl.BlockSpec((tk, tm), lambda i, j, k: (k, i)),
                  pl.BlockSpec((None, tk, tn), lambda i, j, k: (j // nj, k, j % nj))],
        out_specs=pl.BlockSpec((tm, tn), lambda i, j, k: (i, j)),
        out_shape=jax.ShapeDtypeStruct((d, N_IN * d), BF16),
        scratch_shapes=[pltpu.VMEM((tm, tn), F32)],
        compiler_params=_cparams(("parallel", "parallel", "arbitrary")),
    )(xb, dp)


def _grad_x(dp, w_in, gx_direct, parts, packed):
    _, t_len, d = dp.shape
    tm, tn, tk = _tile(t_len, MM_TILE), _tile(d, MM_TILE), _tile(d, MM_DEPTH)
    nkb = d // tk
    nk = N_IN * nkb
    ni, nj = t_len // tm, d // tn
    n_parts = len(parts)

    def body(a_ref, b_ref, r_ref, *rest):
        srcs = rest[:n_parts]
        pk_ref = rest[n_parts]
        o_ref = rest[n_parts + 1]
        lands = rest[n_parts + 2:2 * n_parts + 2]
        gath = rest[2 * n_parts + 2]
        acc, send_sems, recv_sems, pk_send, pk_recv, pk_local = rest[2 * n_parts + 3:]
        i, j, k = pl.program_id(0), pl.program_id(1), pl.program_id(2)
        x, y, c = lax.axis_index("x"), lax.axis_index("y"), lax.axis_index("c")
        my_chip = 2 * x + y
        me = 4 * x + 2 * y + c
        chips = [(1 - x, y), (x, 1 - y), (1 - x, 1 - y)]

        def part_copy(t, n):
            px, py = chips[n]
            return pltpu.make_async_remote_copy(
                src_ref=srcs[t].at[2 * px + py], dst_ref=lands[t].at[my_chip],
                send_sem=send_sems.at[3 * t + n], recv_sem=recv_sems.at[3 * t + n],
                device_id=(px, py, c), device_id_type=MESH)

        def part_landing(t, n):
            px, py = chips[n]
            return pltpu.make_async_remote_copy(
                src_ref=srcs[t].at[my_chip], dst_ref=lands[t].at[2 * px + py],
                send_sem=send_sems.at[3 * t + n], recv_sem=recv_sems.at[3 * t + n],
                device_id=(px, py, c), device_id_type=MESH)

        def pk_copy(s):
            return pltpu.make_async_remote_copy(
                src_ref=pk_ref, dst_ref=gath.at[me], send_sem=pk_send, recv_sem=pk_recv.at[me],
                device_id=(s // 4, (s // 2) % 2, s % 2), device_id_type=MESH)

        @pl.when((i == 0) & (j == 0) & (k == 0))
        def _():
            for t in range(n_parts):
                for n in range(3):
                    part_copy(t, n).start()
            pltpu.make_async_copy(pk_ref, gath.at[me], pk_local).start()
            for s in range(N_DEV):
                @pl.when(s != me)
                def _(s=s):
                    pk_copy(s).start()

        @pl.when(k == 0)
        def _():
            acc[...] = r_ref[...]

        acc[...] += _dot(a_ref[...], b_ref[...])

        @pl.when(k == nk - 1)
        def _():
            o_ref[...] = acc[...]

        @pl.when((i == ni - 1) & (j == nj - 1) & (k == nk - 1))
        def _():
            for t in range(n_parts):
                for n in range(3):
                    part_landing(t, n).wait_recv()
            for t in range(n_parts):
                for n in range(3):
                    part_copy(t, n).wait_send()
            for s in range(N_DEV):
                @pl.when(s != me)
                def _(s=s):
                    pltpu.make_async_remote_copy(
                        src_ref=pk_ref, dst_ref=gath.at[s], send_sem=pk_send, recv_sem=pk_recv.at[s],
                        device_id=(s // 4, (s // 2) % 2, s % 2), device_id_type=MESH).wait_recv()
            seven = gath.at[pl.ds(0, N_DEV - 1)]
            pltpu.make_async_remote_copy(src_ref=seven, dst_ref=seven, send_sem=pk_send, recv_sem=pk_send,
                                         device_id=(x, y, 1 - c), device_id_type=MESH).wait_send()
            pltpu.make_async_copy(pk_ref, gath.at[me], pk_local).wait()

    hbm = pl.BlockSpec(memory_space=pl.ANY)
    outs = pl.pallas_call(
        body, name="grad_x", grid=(ni, nj, nk),
        in_specs=[pl.BlockSpec((None, tm, tk), lambda i, j, k: (k // nkb, i, k % nkb)),
                  pl.BlockSpec((tk, tn), lambda i, j, k: (k, j)),
                  pl.BlockSpec((tm, tn), lambda i, j, k: (i, j))] + [hbm] * (n_parts + 1),
        out_specs=[pl.BlockSpec((tm, tn), lambda i, j, k: (i, j))] + [hbm] * (n_parts + 1),
        out_shape=[jax.ShapeDtypeStruct((t_len, d), F32)] + [jax.ShapeDtypeStruct(p.shape, p.dtype) for p in parts]
        + [jax.ShapeDtypeStruct((N_DEV,) + packed.shape, packed.dtype)],
        scratch_shapes=[pltpu.VMEM((tm, tn), F32),
                        pltpu.SemaphoreType.DMA((3 * n_parts,)), pltpu.SemaphoreType.DMA((3 * n_parts,)),
                        pltpu.SemaphoreType.DMA(()), pltpu.SemaphoreType.DMA((N_DEV,)), pltpu.SemaphoreType.DMA(())],
        compiler_params=_cparams(("arbitrary", "arbitrary", "arbitrary")),
    )(dp, w_in, gx_direct, *parts, packed)
    return outs[0], list(outs[1:1 + n_parts]), outs[1 + n_parts]


def kernel(x, w_in, b_gate, ln_v_g, ln_v_b, w_s, b_s, conv_w, conv_b, w_oa, w_ob, w_out, ln_g, ln_b, loss_target, m_w_in, m_b_gate, m_ln_v_g, m_ln_v_b, m_w_s, m_b_s, m_conv_w, m_conv_b, m_w_oa, m_w_ob, m_w_out, m_ln_g, m_ln_b, v_w_in, v_b_gate, v_ln_v_g, v_ln_v_b, v_w_s, v_b_s, v_conv_w, v_conv_b, v_w_oa, v_w_ob, v_w_out, v_ln_g, v_ln_b):
    _, t_len, d = x.shape
    assert d % (HEADS * 128) == 0 and t_len % CHUNK == 0 and w_in.shape[2] * N_DEV == N_IN * d
    x2 = x[0]
    tgt2 = loss_target[0]
    c_arr = lax.axis_index("c").astype(jnp.int32).reshape(1)
    chip_arr = (2 * lax.axis_index("x") + lax.axis_index("y")).astype(jnp.int32).reshape(1)
    dev = 4 * lax.axis_index("x") + 2 * lax.axis_index("y") + lax.axis_index("c")

    xb = _cast_bf16(x2, "cast_x")
    shards = [_cast_transposed(w_in[0], "cast_w_in"), _cast_bf16(w_oa[0], "cast_w_oa"),
              _cast_bf16(w_ob[0], "cast_w_ob"), _cast_bf16(w_out[0], "cast_w_out")]
    (conv_w_g,) = _all_gather([conv_w[0]], ["lead"], "gather_conv_w", vmem=True)
    conv_w_f = jnp.transpose(conv_w_g, (1, 0, 2)).reshape(3, d)
    w_s3 = w_s[0]
    b_s3 = b_s[0].reshape(HEADS, CHUNK, 1)

    saved, ya, yb, w_in_f, w_oa_f, w_ob_f, w_out_f = _mixer_fwd(
        xb, shards, ln_v_g, ln_v_b, w_s3, b_s3, conv_w_f, conv_b)
    sa, sb, pa, pb, merged = _merge_fwd(xb, ya, yb, w_in_f, w_oa_f, w_ob_f, b_gate)
    dz, gx_direct, g_ln_g, g_ln_b, err2 = _out_ln_loss(merged, w_out_f, x2, tgt2, ln_g, ln_b)
    loss = lax.psum(0.5 * jnp.sum(err2) / d, ("x", "y", "c"))

    da, db, dgate, g_bga, g_bgb = _merge_bwd(dz, w_out_f, sa, sb, pa, pb)
    gw_out = _grad_w(merged, dz, "grad_w_out")
    gw_oa = _grad_w(ya, da, "grad_w_oa")
    gw_ob = _grad_w(yb, db, "grad_w_ob")
    dp, g_ws, g_bs, g_lvg, g_lvb, g_cw, g_cb = _mixer_bwd(
        da, db, w_oa_f, w_ob_f, saved, dgate, ln_v_g, ln_v_b, w_s3, b_s3, conv_w_f)
    gw_in = _grad_w_in(xb, dp)

    grads = [gw_in, gw_oa, gw_ob, gw_out]
    kinds = ["col", "row", "row", "row"]
    lands = _pair_exchange(grads, kinds, "grad_pair_exchange")
    parts = [_pair_sum(g, l, k, c_arr, "grad_pair_sum_%d" % n) for n, (g, l, k) in enumerate(zip(grads, lands, kinds))]
    pieces = [jnp.concatenate([g_bga, g_bgb], axis=1), g_lvg, g_lvb, g_ws, g_bs, g_cw, g_cb, g_ln_g, g_ln_b]
    sizes = [p.size for p in pieces]
    packed = jnp.concatenate([p.reshape(-1, 128) for p in pieces], axis=0)
    grad_x, lands2, gathered = _grad_x(dp, w_in_f, gx_direct, parts, packed)
    grad_x = grad_x[None]

    big = []
    for n, (w, m, v) in enumerate([(w_in, m_w_in, v_w_in), (w_oa, m_w_oa, v_w_oa), (w_ob, m_w_ob, v_w_ob),
                                   (w_out, m_w_out, v_w_out)]):
        big.append([o[None] for o in _sum_adam(parts[n], lands2[n], w[0], m[0], v[0], chip_arr, "sum_adam_%d" % n)])
    (g_w_in, d_w_in, nm_w_in, nv_w_in), (g_w_oa, d_w_oa, nm_w_oa, nv_w_oa), \
        (g_w_ob, d_w_ob, nm_w_ob, nv_w_ob), (g_w_out, d_w_out, nm_w_out, nv_w_out) = big

    total = _small_sum(gathered, "sum_small_grads")
    offs = [0]
    for s in sizes:
        offs.append(offs[-1] + s // 128)
    unpacked = [total[offs[n]:offs[n + 1]] for n in range(len(pieces))]
    g_b_gate = unpacked[0].reshape(b_gate.shape)
    g_ln_v_g = unpacked[1].reshape(ln_v_g.shape)
    g_ln_v_b = unpacked[2].reshape(ln_v_b.shape)
    g_w_s = unpacked[3].reshape(w_s.shape)
    g_b_s = unpacked[4].reshape(b_s.shape)
    g_conv_w = lax.dynamic_slice_in_dim(unpacked[5].reshape(3, d), dev * (d // N_DEV), d // N_DEV, axis=1)[None]
    g_conv_b = unpacked[6].reshape(conv_b.shape)
    g_ln_g2 = unpacked[7].reshape(ln_g.shape)
    g_ln_b2 = unpacked[8].reshape(ln_b.shape)

    small_w = [b_gate, ln_v_g, ln_v_b, w_s, b_s, conv_w, conv_b, ln_g, ln_b]
    small_g = [g_b_gate, g_ln_v_g, g_ln_v_b, g_w_s, g_b_s, g_conv_w, g_conv_b, g_ln_g2, g_ln_b2]
    small_m = [m_b_gate, m_ln_v_g, m_ln_v_b, m_w_s, m_b_s, m_conv_w, m_conv_b, m_ln_g, m_ln_b]
    small_v = [v_b_gate, v_ln_v_g, v_ln_v_b, v_w_s, v_b_s, v_conv_w, v_conv_b, v_ln_g, v_ln_b]

    def flat(a):
        return a.reshape(-1, a.shape[-1])

    res = _small_adam([flat(a) for a in small_w], [flat(a) for a in small_g], [flat(a) for a in small_m],
                      [flat(a) for a in small_v], "adam_small")
    ns = len(small_w)
    d_s = [res[n].reshape(small_w[n].shape) for n in range(ns)]
    nm_s = [res[ns + n].reshape(small_w[n].shape) for n in range(ns)]
    nv_s = [res[2 * ns + n].reshape(small_w[n].shape) for n in range(ns)]

    def ordered(first, small, oa, ob, out):
        return [first] + small[:7] + [oa, ob, out] + small[7:]

    return (loss, grad_x,
            *ordered(g_w_in, small_g, g_w_oa, g_w_ob, g_w_out),
            *ordered(d_w_in, d_s, d_w_oa, d_w_ob, d_w_out),
            *ordered(nm_w_in, nm_s, nm_w_oa, nm_w_ob, nm_w_out),
            *ordered(nv_w_in, nv_s, nv_w_oa, nv_w_ob, nv_w_out))
```

```python
import functools

import jax
import jax.numpy as jnp
from jax import lax
from jax.experimental import pallas as pl
from jax.experimental.pallas import tpu as pltpu

F32 = jnp.float32
BF16 = jnp.bfloat16
MESH = pl.DeviceIdType.MESH

N_DEV = 8
N_CHIP = 4
HEADS = 8
CHUNK = 128
N_IN = 9
N_MIX = 7
N_SAVE = 8
LN_EPS = 1e-5
DN_ALPHA = 2.0 ** 0.25
ADAM_LR = 0.001
ADAM_B1 = 0.9
ADAM_B2 = 0.999
ADAM_EPS = 1e-08
ADAM_WD = 0.01
ADAM_STEP = 10
GELU_C0 = 0.7978845608028654
GELU_C1 = 0.044715
SUBLANES = 8
FIRST_SENDS = 4
SENDS_PER_GROUP = 2
MM_TILE = 1024
MM_DEPTH = 2048
VMEM_LIMIT = 56 << 20


def _cparams(sem):
    return pltpu.CompilerParams(dimension_semantics=sem, vmem_limit_bytes=VMEM_LIMIT)


def _tile(n, want):
    t = min(n, want)
    while n % t:
        t //= 2
    return t


def _gelu(u):
    t = jnp.tanh(GELU_C0 * (u + GELU_C1 * u * u * u))
    return 0.5 * u * (1.0 + t), t


def _gelu_grad(u, t):
    return 0.5 * (1.0 + t) + 0.5 * u * (1.0 - t * t) * (GELU_C0 * (1.0 + 3.0 * GELU_C1 * u * u))


def _silu(z):
    s = jax.nn.sigmoid(z)
    return z * s, s


def _silu_grad(z, s):
    return s * (1.0 + z * (1.0 - s))


def _fold8(a):
    return jnp.sum(a.reshape(a.shape[0] // SUBLANES, SUBLANES, a.shape[1]), axis=0)


def _dot(a, b):
    return jnp.dot(a, b, preferred_element_type=F32)


def _dot_nt(a, b):
    return lax.dot_general(a, b, (((1,), (1,)), ((), ())), preferred_element_type=F32)


def _dot_tn(a, b):
    return lax.dot_general(a, b, (((0,), (0,)), ((), ())), preferred_element_type=F32)


def _tril_mask():
    r = lax.broadcasted_iota(jnp.int32, (CHUNK, CHUNK), 0)
    c = lax.broadcasted_iota(jnp.int32, (CHUNK, CHUNK), 1)
    return c <= r


def _cast_bf16(a, name):
    rows, cols = a.shape
    rb = _tile(rows, 256)

    def body(a_ref, o_ref):
        o_ref[...] = a_ref[...].astype(BF16)

    return pl.pallas_call(
        body, name=name, grid=(rows // rb,),
        in_specs=[pl.BlockSpec((rb, cols), lambda i: (i, 0))],
        out_specs=pl.BlockSpec((rb, cols), lambda i: (i, 0)),
        out_shape=jax.ShapeDtypeStruct((rows, cols), BF16),
        compiler_params=_cparams(("parallel",)),
    )(a)


def _cast_transposed(a, name):
    rows, cols = a.shape
    tb = _tile(rows, 2048)
    tc = _tile(cols, 256)

    def body(a_ref, o_ref):
        o_ref[...] = a_ref[...].T.astype(BF16)

    return pl.pallas_call(
        body, name=name, grid=(cols // tc, rows // tb),
        in_specs=[pl.BlockSpec((tb, tc), lambda j, i: (i, j))],
        out_specs=pl.BlockSpec((tc, tb), lambda j, i: (j, i)),
        out_shape=jax.ShapeDtypeStruct((cols, rows), BF16),
        compiler_params=_cparams(("parallel", "parallel")),
    )(a)


def _shard_ref(full, kind, s, n):
    if kind == "col":
        return full.at[:, pl.ds(pl.multiple_of(s * n, 128), n)]
    if kind == "row":
        return full.at[pl.ds(pl.multiple_of(s * n, SUBLANES), n), :]
    return full.at[s]


def _all_gather(shards, kinds, name, vmem):
    nt = len(shards)
    out_shapes = []
    for a, kind in zip(shards, kinds):
        if kind == "col":
            out_shapes.append(jax.ShapeDtypeStruct((a.shape[0], N_DEV * a.shape[1]), a.dtype))
        elif kind == "row":
            out_shapes.append(jax.ShapeDtypeStruct((N_DEV * a.shape[0], a.shape[1]), a.dtype))
        else:
            out_shapes.append(jax.ShapeDtypeStruct((N_DEV,) + a.shape, a.dtype))

    def body(*refs):
        srcs, fulls = refs[:nt], refs[nt:2 * nt]
        send_sems, recv_sems, local_sems = refs[2 * nt:]
        x, y, c = lax.axis_index("x"), lax.axis_index("y"), lax.axis_index("c")
        sibling = (x, y, 1 - c)
        chips = [(1 - x, y), (x, 1 - y), (1 - x, 1 - y)]

        def dev(px, py, pc):
            return 4 * px + 2 * py + pc

        def region(t, s):
            a, kind = shards[t], kinds[t]
            n = a.shape[1] if kind == "col" else a.shape[0]
            return _shard_ref(fulls[t], kind, s, n)

        def copy(t, k, block, to, own=False):
            return pltpu.make_async_remote_copy(
                src_ref=srcs[t] if own else region(t, block), dst_ref=region(t, block),
                send_sem=send_sems.at[7 * t + k], recv_sem=recv_sems.at[7 * t + k],
                device_id=to, device_id_type=MESH)

        me = dev(x, y, c)
        started = []
        for t in range(nt):
            mine = pltpu.make_async_copy(srcs[t], region(t, me), local_sems.at[t])
            mine.start()
            started.append(mine)
        first = []
        for t in range(nt):
            first.append(copy(t, 0, me, sibling, own=True))
            for j, chip in enumerate(chips):
                first.append(copy(t, 1 + j, me, (*chip, c), own=True))
        for cp in first:
            cp.start()
        passed = []
        for t in range(nt):
            for j, chip in enumerate(chips):
                blk = dev(*chip, c)
                copy(t, 1 + j, blk, sibling).wait_recv()
                fwd = copy(t, 4 + j, blk, sibling)
                fwd.start()
                passed.append(fwd)
        for t in range(nt):
            copy(t, 0, dev(x, y, 1 - c), sibling).wait_recv()
            for j, chip in enumerate(chips):
                copy(t, 4 + j, dev(*chip, 1 - c), sibling).wait_recv()
        for cp in first + passed:
            cp.wait_send()
        for mine in started:
            mine.wait()

    space = pltpu.VMEM if vmem else pl.ANY
    return pl.pallas_call(
        body, name=name,
        in_specs=[pl.BlockSpec(memory_space=space)] * nt,
        out_specs=[pl.BlockSpec(memory_space=space)] * nt,
        out_shape=out_shapes,
        scratch_shapes=[pltpu.SemaphoreType.DMA((7 * nt,)), pltpu.SemaphoreType.DMA((7 * nt,)),
                        pltpu.SemaphoreType.DMA((nt,))],
        compiler_params=pltpu.CompilerParams(vmem_limit_bytes=VMEM_LIMIT, has_side_effects=True),
    )(*shards)


def _pair_exchange(grads, kinds, name):
    nt = len(grads)
    shard_shapes = []
    for g, kind in zip(grads, kinds):
        shard_shapes.append((g.shape[0], g.shape[1] // N_DEV) if kind == "col" else (g.shape[0] // N_DEV, g.shape[1]))

    def body(*refs):
        srcs, lands = refs[:nt], refs[nt:2 * nt]
        send_sems, recv_sems = refs[2 * nt:]
        x, y, c = lax.axis_index("x"), lax.axis_index("y"), lax.axis_index("c")
        copies = []
        for t in range(nt):
            n = shard_shapes[t][1] if kinds[t] == "col" else shard_shapes[t][0]
            for k in range(N_CHIP):
                cp = pltpu.make_async_remote_copy(
                    src_ref=_shard_ref(srcs[t], kinds[t], 2 * k + 1 - c, n), dst_ref=lands[t].at[k],
                    send_sem=send_sems.at[N_CHIP * t + k], recv_sem=recv_sems.at[N_CHIP * t + k],
                    device_id=(x, y, 1 - c), device_id_type=MESH)
                cp.start()
                copies.append(cp)
        for cp in copies:
            cp.wait()

    return pl.pallas_call(
        body, name=name,
        in_specs=[pl.BlockSpec(memory_space=pl.ANY)] * nt,
        out_specs=[pl.BlockSpec(memory_space=pl.ANY)] * nt,
        out_shape=[jax.ShapeDtypeStruct((N_CHIP,) + s, g.dtype) for s, g in zip(shard_shapes, grads)],
        scratch_shapes=[pltpu.SemaphoreType.DMA((N_CHIP * nt,)), pltpu.SemaphoreType.DMA((N_CHIP * nt,))],
        compiler_params=pltpu.CompilerParams(has_side_effects=True),
    )(*grads)


def _pair_sum(grad, land, kind, c_arr, name):
    _, r, w = land.shape
    rb = _tile(r, 256)
    nrb = r // rb

    def body(c_ref, g_ref, l_ref, o_ref):
        o_ref[...] = (g_ref[...].astype(F32) + l_ref[...].astype(F32)).astype(o_ref.dtype)

    if kind == "col":
        g_spec = pl.BlockSpec((rb, w), lambda k, i, c: (i, 2 * k + c[0]))
    else:
        g_spec = pl.BlockSpec((rb, w), lambda k, i, c: ((2 * k + c[0]) * nrb + i, 0))
    return pl.pallas_call(
        body, name=name,
        grid_spec=pltpu.PrefetchScalarGridSpec(
            num_scalar_prefetch=1, grid=(N_CHIP, nrb),
            in_specs=[g_spec, pl.BlockSpec((None, rb, w), lambda k, i, c: (k, i, 0))],
            out_specs=pl.BlockSpec((None, rb, w), lambda k, i, c: (k, i, 0))),
        out_shape=jax.ShapeDtypeStruct(land.shape, BF16),
        compiler_params=_cparams(("parallel", "parallel")),
    )(c_arr, grad, land)


def _adam(w, g, m, v):
    m = ADAM_B1 * m + (1.0 - ADAM_B1) * g
    v = ADAM_B2 * v + (1.0 - ADAM_B2) * jnp.square(g)
    m_hat = m / (1.0 - ADAM_B1 ** ADAM_STEP)
    v_hat = v / (1.0 - ADAM_B2 ** ADAM_STEP)
    delta = -ADAM_LR * (m_hat / (jnp.sqrt(v_hat) + ADAM_EPS) + ADAM_WD * w)
    return delta, m, v


def _sum_adam(part, land, w, m, v, chip_arr, name):
    r, wd = w.shape
    rb = _tile(r, 128)

    def body(k_ref, own, r1, r2, r3, w_ref, m_ref, v_ref, g_out, d_out, m_out, v_out):
        g = own[...].astype(F32) + r1[...].astype(F32) + r2[...].astype(F32) + r3[...].astype(F32)
        d, mn, vn = _adam(w_ref[...], g, m_ref[...], v_ref[...])
        g_out[...] = g
        d_out[...] = d
        m_out[...] = mn
        v_out[...] = vn

    def slot(off):
        return pl.BlockSpec((None, rb, wd), lambda i, k: ((k[0] + off) % N_CHIP, i, 0))

    plain = pl.BlockSpec((rb, wd), lambda i, k: (i, 0))
    return pl.pallas_call(
        body, name=name,
        grid_spec=pltpu.PrefetchScalarGridSpec(
            num_scalar_prefetch=1, grid=(r // rb,),
            in_specs=[slot(0), slot(1), slot(2), slot(3), plain, plain, plain],
            out_specs=[plain] * 4),
        out_shape=[jax.ShapeDtypeStruct(w.shape, F32)] * 4,
        compiler_params=_cparams(("parallel",)),
    )(chip_arr, part, land, land, land, w, m, v)


def _small_sum(gathered, name):
    _, r, w = gathered.shape

    def body(g_ref, o_ref):
        acc = g_ref[0]
        for d in range(1, N_DEV):
            acc = acc + g_ref[d]
        o_ref[...] = acc

    return pl.pallas_call(body, name=name, out_shape=jax.ShapeDtypeStruct((r, w), F32))(gathered)


def _small_adam(ws, gs, ms, vs, name):
    n = len(ws)

    def body(*refs):
        ins, outs = refs[:4 * n], refs[4 * n:]
        for t in range(n):
            d, mn, vn = _adam(ins[t][...], ins[n + t][...], ins[2 * n + t][...], ins[3 * n + t][...])
            outs[t][...] = d
            outs[n + t][...] = mn
            outs[2 * n + t][...] = vn

    shapes = [jax.ShapeDtypeStruct(w.shape, F32) for w in ws]
    return pl.pallas_call(body, name=name, out_shape=shapes * 3)(*ws, *gs, *ms, *vs)


def _mixer_fwd(xb, shards, ln_v_g, ln_v_b, w_s, b_s3, conv_w, conv_b):
    t_len, d = xb.shape
    hd = d // HEADS
    tm = _tile(t_len, 512)
    nt = t_len // tm
    assert nt >= 2
    w8 = shards[0].shape[0]
    bps = w8 // hd
    r8 = shards[1].shape[0]
    nq = N_IN * HEADS
    n_blocks = nq + 3 * N_DEV
    groups = [[("in", HEADS * b + k) for b in range(N_MIX)] for k in range(HEADS)]
    groups.append([("in", q) for q in range(N_MIX * HEADS, nq)] + [(t, s) for t in (1, 2, 3) for s in range(N_DEV)])

    def owner_of(blk):
        return blk[1] // bps if blk[0] == "in" else blk[1]

    send_step = {}
    for s in range(N_DEV):
        mine = [blk for grp in groups for blk in grp if owner_of(blk) == s]
        for pos, blk in enumerate(mine):
            send_step[blk] = 0 if pos < FIRST_SENDS else 1 + (pos - FIRST_SENDS) // SENDS_PER_GROUP
    assert max(send_step.values()) < HEADS
    tail_pass_step = nt - 6 if nt >= 8 else nt - 2

    def body(x_ref, sh_in, sh_oa, sh_ob, sh_out, lng, lnb, ws_ref, bs_ref, cw_ref, cb_ref,
             save_ref, ya_ref, yb_ref, f_in, f_oa, f_ob, f_out,
             hbuf, wbuf, recv_sems, own_sems, fwd_sems, local_sems, load_sems):
        g = pl.program_id(0)
        i = pl.program_id(1)
        x, y, c = lax.axis_index("x"), lax.axis_index("y"), lax.axis_index("c")
        sibling = (x, y, 1 - c)
        chips = [(1 - x, y), (x, 1 - y), (1 - x, 1 - y)]
        shard_refs = (sh_in, sh_oa, sh_ob, sh_out)
        fulls = (f_in, f_oa, f_ob, f_out)

        def tensor(blk):
            return 0 if blk[0] == "in" else blk[0]

        def owner(blk):
            s = owner_of(blk)
            return s // 4, (s // 2) % 2, s % 2

        def bid(blk):
            return blk[1] if blk[0] == "in" else nq + (blk[0] - 1) * N_DEV + blk[1]

        def region(blk):
            if blk[0] == "in":
                return f_in.at[pl.ds(blk[1] * hd, hd), :]
            return fulls[blk[0]].at[pl.ds(blk[1] * r8, r8), :]

        def own_src(blk):
            if blk[0] == "in":
                return sh_in.at[pl.ds((blk[1] % bps) * hd, hd), :]
            return shard_refs[blk[0]]

        def rcopy(blk, to, send_sem, own):
            return pltpu.make_async_remote_copy(
                src_ref=own_src(blk) if own else region(blk), dst_ref=region(blk),
                send_sem=send_sem, recv_sem=recv_sems.at[bid(blk)], device_id=to, device_id_type=MESH)

        def send_own(blk):
            ox, oy, oc = owner(blk)

            @pl.when((x == ox) & (y == oy) & (c == oc))
            def _():
                rcopy(blk, sibling, own_sems.at[tensor(blk)], True).start()
                for chip in chips:
                    rcopy(blk, (*chip, c), own_sems.at[tensor(blk)], True).start()

        def pass_on(blk):
            ox, oy, oc = owner(blk)

            @pl.when(((x != ox) | (y != oy)) & (c == oc))
            def _():
                rcopy(blk, sibling, fwd_sems.at[tensor(blk)], False).wait_recv()
                rcopy(blk, sibling, fwd_sems.at[tensor(blk)], False).start()

        def wait_from_sibling(blk):
            @pl.when(c != owner(blk)[2])
            def _():
                rcopy(blk, sibling, fwd_sems.at[tensor(blk)], False).wait_recv()

        def local_copy(t):
            me = 4 * x + 2 * y + c
            return pltpu.make_async_copy(shard_refs[t], _shard_ref(fulls[t], "row", me, w8 if t == 0 else r8),
                                         local_sems.at[t])

        first = (g == 0) & (i == 0)

        @pl.when(first)
        def _():
            for t in range(4):
                local_copy(t).start()
            for grp in groups:
                for blk in grp:
                    if send_step[blk] == 0:
                        send_own(blk)
            for blk in groups[0]:
                pass_on(blk)

        @pl.when(i == nt - 2)
        def _():
            for k in range(HEADS - 1):
                @pl.when(g == k)
                def _(k=k):
                    for blk in groups[k + 1]:
                        pass_on(blk)

        @pl.when((g == HEADS - 1) & (i == tail_pass_step))
        def _():
            for blk in groups[HEADS]:
                pass_on(blk)

        @pl.when(i == 0)
        def _():
            for k in range(HEADS):
                @pl.when(g == k)
                def _(k=k):
                    if k > 0:
                        for grp in groups:
                            for blk in grp:
                                if send_step[blk] == k:
                                    send_own(blk)
                    for blk in groups[k]:
                        wait_from_sibling(blk)
                    for b, blk in enumerate(groups[k]):
                        ox, oy, oc = owner(blk)
                        mine = (x == ox) & (y == oy) & (c == oc)

                        @pl.when(mine)
                        def _(b=b, blk=blk):
                            pltpu.make_async_copy(own_src(blk), wbuf.at[b], load_sems.at[b]).start()

                        @pl.when(jnp.logical_not(mine))
                        def _(b=b, blk=blk):
                            pltpu.make_async_copy(region(blk), wbuf.at[b], load_sems.at[b]).start()

                    for b, blk in enumerate(groups[k]):
                        pltpu.make_async_copy(region(blk), wbuf.at[b], load_sems.at[b]).wait()

        xt = x_ref[...]
        u = _dot_nt(xt, wbuf[0])
        v = _dot_nt(xt, wbuf[1])
        z = _dot_nt(xt, wbuf[2])
        save_ref[0] = u.astype(BF16)
        save_ref[1] = v.astype(BF16)
        save_ref[2] = z.astype(BF16)
        gu, _ = _gelu(u)
        gv, _ = _gelu(v)
        mu = jnp.mean(gv, axis=-1, keepdims=True)
        dv = gv - mu
        var = jnp.mean(dv * dv, axis=-1, keepdims=True)
        vn = (dv * lax.rsqrt(var + LN_EPS) * lng[...] + lnb[...]).astype(BF16)
        sz, _ = _silu(z)
        gate = gu * sz
        wm = jnp.where(_tril_mask(), ws_ref[0], 0.0).astype(BF16)
        bs = bs_ref[0]
        for ck in range(tm // CHUNK):
            rows = slice(ck * CHUNK, (ck + 1) * CHUNK)
            mixed = _dot(wm, vn[rows]) + bs
            ya_ref[rows, :] = (gate[rows] * mixed).astype(BF16)

        xbv = _dot_nt(xt, wbuf[3])
        cbv = _dot_nt(xt, wbuf[4])
        bbv = _dot_nt(xt, wbuf[5])
        zbv = _dot_nt(xt, wbuf[6])
        save_ref[3] = xbv.astype(BF16)
        save_ref[4] = cbv.astype(BF16)
        save_ref[5] = bbv.astype(BF16)
        save_ref[6] = zbv.astype(BF16)
        h = cbv * xbv

        @pl.when(i == 0)
        def _():
            hbuf[0:SUBLANES, :] = jnp.zeros((SUBLANES, hd), F32)

        hbuf[SUBLANES:SUBLANES + tm, :] = h
        h1 = hbuf[SUBLANES - 1:SUBLANES - 1 + tm, :]
        h2 = hbuf[SUBLANES - 2:SUBLANES - 2 + tm, :]
        conv = cb_ref[...] + cw_ref[0:1, :] * h2 + cw_ref[1:2, :] * h1 + cw_ref[2:3, :] * h
        hbuf[0:SUBLANES, :] = h[tm - SUBLANES:tm, :]
        save_ref[7] = conv.astype(BF16)
        szb, _ = _silu(zbv)
        yb_ref[...] = (bbv * conv * szb).astype(BF16)

        @pl.when((g == HEADS - 1) & (i == nt - 1))
        def _():
            for blk in groups[HEADS]:
                wait_from_sibling(blk)
            for t in range(4):
                local_copy(t).wait()
            for t in range(4):
                n = w8 if t == 0 else r8
                own_all, fwd_all = fulls[t].at[pl.ds(0, 4 * n), :], fulls[t].at[pl.ds(0, 3 * n), :]
                for ref, sem in ((own_all, own_sems.at[t]), (fwd_all, fwd_sems.at[t])):
                    pltpu.make_async_remote_copy(src_ref=ref, dst_ref=ref, send_sem=sem, recv_sem=sem,
                                                 device_id=sibling, device_id_type=MESH).wait_send()

    vec = pl.BlockSpec((1, hd), lambda g, i: (0, g))
    hbm = pl.BlockSpec(memory_space=pl.ANY)
    return pl.pallas_call(
        body, name="mixer_fwd", grid=(HEADS, nt),
        in_specs=[pl.BlockSpec((tm, d), lambda g, i: (i, 0)), hbm, hbm, hbm, hbm,
                  vec, vec,
                  pl.BlockSpec((1, CHUNK, CHUNK), lambda g, i: (g, 0, 0)),
                  pl.BlockSpec((1, CHUNK, 1), lambda g, i: (g, 0, 0)),
                  pl.BlockSpec((3, hd), lambda g, i: (0, g)),
                  vec],
        out_specs=[pl.BlockSpec((N_SAVE, tm, hd), lambda g, i: (0, i, g)),
                   pl.BlockSpec((tm, hd), lambda g, i: (i, g)),
                   pl.BlockSpec((tm, hd), lambda g, i: (i, g)),
                   hbm, hbm, hbm, hbm],
        out_shape=[jax.ShapeDtypeStruct((N_SAVE, t_len, d), BF16),
                   jax.ShapeDtypeStruct((t_len, d), BF16),
                   jax.ShapeDtypeStruct((t_len, d), BF16),
                   jax.ShapeDtypeStruct((N_DEV * w8, d), BF16),
                   jax.ShapeDtypeStruct((d, d), BF16), jax.ShapeDtypeStruct((d, d), BF16),
                   jax.ShapeDtypeStruct((d, d), BF16)],
        scratch_shapes=[pltpu.VMEM((SUBLANES + tm, hd), F32), pltpu.VMEM((N_MIX, hd, d), BF16),
                        pltpu.SemaphoreType.DMA((n_blocks,)), pltpu.SemaphoreType.DMA((4,)),
                        pltpu.SemaphoreType.DMA((4,)), pltpu.SemaphoreType.DMA((4,)),
                        pltpu.SemaphoreType.DMA((N_MIX,))],
        compiler_params=_cparams(("arbitrary", "arbitrary")),
    )(xb, *shards, ln_v_g, ln_v_b, w_s, b_s3, conv_w, conv_b)


def _merge_fwd(xb, ya, yb, w_in, w_oa, w_ob, b_gate):
    t_len, d = xb.shape
    tm = _tile(t_len, 512)
    tn = _tile(d, 512)
    nj = d // tn

    def body(x_ref, ya_ref, yb_ref, wga, wgb, woa, wob, bga, bgb, sa_ref, sb_ref, pa_ref, pb_ref, mg_ref):
        xt = x_ref[...]
        sa = jax.nn.sigmoid(_dot_nt(xt, wga[...]) + bga[...])
        sb = jax.nn.sigmoid(_dot_nt(xt, wgb[...]) + bgb[...])
        pa = _dot(ya_ref[...], woa[...])
        pb = _dot(yb_ref[...], wob[...])
        sa_ref[...] = sa.astype(BF16)
        sb_ref[...] = sb.astype(BF16)
        pa_ref[...] = pa.astype(BF16)
        pb_ref[...] = pb.astype(BF16)
        mg_ref[...] = (sa * pa + sb * pb).astype(BF16)

    row = pl.BlockSpec((tm, d), lambda j, i: (i, 0))
    out = pl.BlockSpec((tm, tn), lambda j, i: (i, j))
    return pl.pallas_call(
        body, name="merge_fwd", grid=(nj, t_len // tm),
        in_specs=[row, row, row,
                  pl.BlockSpec((tn, d), lambda j, i: (7 * nj + j, 0)),
                  pl.BlockSpec((tn, d), lambda j, i: (8 * nj + j, 0)),
                  pl.BlockSpec((d, tn), lambda j, i: (0, j)),
                  pl.BlockSpec((d, tn), lambda j, i: (0, j)),
                  pl.BlockSpec((1, tn), lambda j, i: (0, j)),
                  pl.BlockSpec((1, tn), lambda j, i: (0, nj + j))],
        out_specs=[out] * 5,
        out_shape=[jax.ShapeDtypeStruct((t_len, d), BF16)] * 5,
        compiler_params=_cparams(("parallel", "arbitrary")),
    )(xb, ya, yb, w_in, w_in, w_oa, w_ob, b_gate, b_gate)


def _out_ln_loss(merged, w_out, x, target, ln_g, ln_b):
    t_len, d = x.shape
    tm = _tile(t_len, 256)
    nt = t_len // tm

    def body(mg_ref, w_ref, x_ref, t_ref, g_ref, b_ref, dz_ref, gx_ref, glg_ref, glb_ref, ls_ref, a_g, a_b, a_l):
        i = pl.program_id(0)

        @pl.when(i == 0)
        def _():
            a_g[...] = jnp.zeros_like(a_g)
            a_b[...] = jnp.zeros_like(a_b)
            a_l[...] = jnp.zeros_like(a_l)

        zres = DN_ALPHA * x_ref[...] + _dot(mg_ref[...], w_ref[...])
        mu = jnp.mean(zres, axis=-1, keepdims=True)
        dc = zres - mu
        var = jnp.mean(dc * dc, axis=-1, keepdims=True)
        rstd = lax.rsqrt(var + LN_EPS)
        xhat = dc * rstd
        g = g_ref[...]
        err = xhat * g + b_ref[...] - t_ref[...]
        dy = err * (1.0 / d)
        a_l[...] += _fold8(err * err)
        a_g[...] += _fold8(dy * xhat)
        a_b[...] += _fold8(dy)
        dxh = dy * g
        m1 = jnp.mean(dxh, axis=-1, keepdims=True)
        m2 = jnp.mean(dxh * xhat, axis=-1, keepdims=True)
        dz = rstd * (dxh - m1 - xhat * m2)
        dz_ref[...] = dz.astype(BF16)
        gx_ref[...] = DN_ALPHA * dz

        @pl.when(i == nt - 1)
        def _():
            glg_ref[...] = jnp.sum(a_g[...], axis=0, keepdims=True)
            glb_ref[...] = jnp.sum(a_b[...], axis=0, keepdims=True)
            ls_ref[...] = jnp.sum(a_l[...], axis=0, keepdims=True)

    row = pl.BlockSpec((tm, d), lambda i: (i, 0))
    vec = pl.BlockSpec((1, d), lambda i: (0, 0))
    return pl.pallas_call(
        body, name="out_ln_loss", grid=(nt,),
        in_specs=[row, pl.BlockSpec((d, d), lambda i: (0, 0)), row, row, vec, vec],
        out_specs=[row, row, vec, vec, vec],
        out_shape=[jax.ShapeDtypeStruct((t_len, d), BF16), jax.ShapeDtypeStruct((t_len, d), F32)]
        + [jax.ShapeDtypeStruct((1, d), F32)] * 3,
        scratch_shapes=[pltpu.VMEM((SUBLANES, d), F32)] * 3,
        compiler_params=_cparams(("arbitrary",)),
    )(merged, w_out, x, target, ln_g, ln_b)


def _merge_bwd(dz, w_out, sa, sb, pa, pb):
    t_len, d = dz.shape
    tm = _tile(t_len, 512)
    tn = _tile(d, 512)
    nt = t_len // tm

    def body(dz_ref, w_ref, sa_ref, sb_ref, pa_ref, pb_ref, da_ref, db_ref, dg_ref, ga_ref, gb_ref, acc_a, acc_b):
        i = pl.program_id(1)

        @pl.when(i == 0)
        def _():
            acc_a[...] = jnp.zeros_like(acc_a)
            acc_b[...] = jnp.zeros_like(acc_b)

        dm = _dot_nt(dz_ref[...], w_ref[...])
        sa = sa_ref[...].astype(F32)
        sb = sb_ref[...].astype(F32)
        da = dm * sa
        db = dm * sb
        da_ref[...] = da.astype(BF16)
        db_ref[...] = db.astype(BF16)
        dga = da * pa_ref[...].astype(F32) * (1.0 - sa)
        dgb = db * pb_ref[...].astype(F32) * (1.0 - sb)
        dg_ref[0] = dga.astype(BF16)
        dg_ref[1] = dgb.astype(BF16)
        acc_a[...] += _fold8(dga)
        acc_b[...] += _fold8(dgb)

        @pl.when(i == nt - 1)
        def _():
            ga_ref[...] = jnp.sum(acc_a[...], axis=0, keepdims=True)
            gb_ref[...] = jnp.sum(acc_b[...], axis=0, keepdims=True)

    blk = pl.BlockSpec((tm, tn), lambda j, i: (i, j))
    vec = pl.BlockSpec((1, tn), lambda j, i: (0, j))
    return pl.pallas_call(
        body, name="merge_bwd", grid=(d // tn, nt),
        in_specs=[pl.BlockSpec((tm, d), lambda j, i: (i, 0)), pl.BlockSpec((tn, d), lambda j, i: (j, 0)),
                  blk, blk, blk, blk],
        out_specs=[blk, blk, pl.BlockSpec((2, tm, tn), lambda j, i: (0, i, j)), vec, vec],
        out_shape=[jax.ShapeDtypeStruct((t_len, d), BF16)] * 2 + [jax.ShapeDtypeStruct((2, t_len, d), BF16)]
        + [jax.ShapeDtypeStruct((1, d), F32)] * 2,
        scratch_shapes=[pltpu.VMEM((SUBLANES, tn), F32)] * 2,
        compiler_params=_cparams(("parallel", "arbitrary")),
    )(dz, w_out, sa, sb, pa, pb)


def _grad_w(a, b, name):
    t_len, m = a.shape
    n = b.shape[1]
    tm, tn, tk = _tile(m, MM_TILE), _tile(n, MM_TILE), _tile(t_len, MM_DEPTH)
    nk = t_len // tk

    def body(a_ref, b_ref, o_ref, acc):
        k = pl.program_id(2)

        @pl.when(k == 0)
        def _():
            acc[...] = jnp.zeros_like(acc)

        acc[...] += _dot_tn(a_ref[...], b_ref[...])

        @pl.when(k == nk - 1)
        def _():
            o_ref[...] = acc[...].astype(BF16)

    return pl.pallas_call(
        body, name=name, grid=(m // tm, n // tn, nk),
        in_specs=[pl.BlockSpec((tk, tm), lambda i, j, k: (k, i)), pl.BlockSpec((tk, tn), lambda i, j, k: (k, j))],
        out_specs=pl.BlockSpec((tm, tn), lambda i, j, k: (i, j)),
        out_shape=jax.ShapeDtypeStruct((m, n), BF16),
        scratch_shapes=[pltpu.VMEM((tm, tn), F32)],
        compiler_params=_cparams(("parallel", "parallel", "arbitrary")),
    )(a, b)


def _mixer_bwd(da, db, w_oa, w_ob, saved, dgate, ln_v_g, ln_v_b, w_s, b_s3, conv_w):
    t_len, d = da.shape
    hd = d // HEADS
    tm = _tile(t_len, 512)
    nt = t_len // tm

    def body(da_ref, db_ref, woa, wob, sv, dgt, lng, lnb, ws_ref, bs_ref, cw_ref,
             dp_ref, gws_ref, gbs_ref, glg_ref, glb_ref, gcw_ref, gcb_ref,
             dbuf, carry, a_ws, a_bs, a_lg, a_lb, a_c0, a_c1, a_c2, a_cb):
        i = pl.program_id(1)

        @pl.when(i == 0)
        def _():
            carry[...] = jnp.zeros_like(carry)
            for a in (a_ws, a_bs, a_lg, a_lb, a_c0, a_c1, a_c2, a_cb):
                a[...] = jnp.zeros_like(a)

        dya = _dot_nt(da_ref[...], woa[...])
        u = sv[0].astype(F32)
        v = sv[1].astype(F32)
        z = sv[2].astype(F32)
        gu, tu = _gelu(u)
        gv, tv = _gelu(v)
        mu = jnp.mean(gv, axis=-1, keepdims=True)
        dvc = gv - mu
        var = jnp.mean(dvc * dvc, axis=-1, keepdims=True)
        rstd = lax.rsqrt(var + LN_EPS)
        vhat = dvc * rstd
        g = lng[...]
        vn = (vhat * g + lnb[...]).astype(BF16)
        sz, s = _silu(z)
        t1 = dya * sz
        dmixed = t1 * gu
        dmixed_b = dmixed.astype(BF16)
        wm = jnp.where(_tril_mask(), ws_ref[0], 0.0).astype(BF16)
        bs = bs_ref[0]
        dvn_parts = []
        mixed_parts = []
        gws = a_ws[...]
        gbs = a_bs[...]
        for ck in range(tm // CHUNK):
            rows = slice(ck * CHUNK, (ck + 1) * CHUNK)
            mixed_parts.append(_dot(wm, vn[rows]) + bs)
            gws = gws + _dot_nt(dmixed_b[rows], vn[rows])
            gbs = gbs + dmixed[rows]
            dvn_parts.append(_dot_tn(wm, dmixed_b[rows]))
        a_ws[...] = gws
        a_bs[...] = gbs
        mixed = jnp.concatenate(mixed_parts, axis=0)
        dvn = jnp.concatenate(dvn_parts, axis=0)
        dp_ref[0] = (t1 * mixed * _gelu_grad(u, tu)).astype(BF16)
        dp_ref[2] = (dya * gu * mixed * _silu_grad(z, s)).astype(BF16)
        a_lg[...] += _fold8(dvn * vhat)
        a_lb[...] += _fold8(dvn)
        dvh = dvn * g
        m1 = jnp.mean(dvh, axis=-1, keepdims=True)
        m2 = jnp.mean(dvh * vhat, axis=-1, keepdims=True)
        dp_ref[1] = (rstd * (dvh - m1 - vhat * m2) * _gelu_grad(v, tv)).astype(BF16)

        dyb = _dot_nt(db_ref[...], wob[...])
        xbv = sv[3].astype(F32)
        cbv = sv[4].astype(F32)
        bbv = sv[5].astype(F32)
        zbv = sv[6].astype(F32)
        conv = sv[7].astype(F32)
        szb, sb = _silu(zbv)
        dp_ref[5] = (dyb * conv * szb).astype(BF16)
        dp_ref[6] = (dyb * bbv * conv * _silu_grad(zbv, sb)).astype(BF16)
        dconv = dyb * bbv * szb
        dbuf[0:tm, :] = dconv
        dbuf[tm:tm + SUBLANES, :] = carry[...]
        dc1 = dbuf[1:tm + 1, :]
        dc2 = dbuf[2:tm + 2, :]
        carry[...] = dconv[0:SUBLANES, :]
        h = cbv * xbv
        a_c2[...] += _fold8(dconv * h)
        a_c1[...] += _fold8(dc1 * h)
        a_c0[...] += _fold8(dc2 * h)
        a_cb[...] += _fold8(dconv)
        dh = cw_ref[2:3, :] * dconv + cw_ref[1:2, :] * dc1 + cw_ref[0:1, :] * dc2
        dp_ref[3] = (dh * cbv).astype(BF16)
        dp_ref[4] = (dh * xbv).astype(BF16)
        dp_ref[7] = dgt[0]
        dp_ref[8] = dgt[1]

        @pl.when(i == nt - 1)
        def _():
            gws_ref[0] = jnp.where(_tril_mask(), a_ws[...], 0.0)
            gbs_ref[0] = jnp.sum(a_bs[...], axis=1, keepdims=True)
            glg_ref[...] = jnp.sum(a_lg[...], axis=0, keepdims=True)
            glb_ref[...] = jnp.sum(a_lb[...], axis=0, keepdims=True)
            gcw_ref[0:1, :] = jnp.sum(a_c0[...], axis=0, keepdims=True)
            gcw_ref[1:2, :] = jnp.sum(a_c1[...], axis=0, keepdims=True)
            gcw_ref[2:3, :] = jnp.sum(a_c2[...], axis=0, keepdims=True)
            gcb_ref[...] = jnp.sum(a_cb[...], axis=0, keepdims=True)

    def rev(i):
        return nt - 1 - i

    row = pl.BlockSpec((tm, d), lambda g, i: (rev(i), 0))
    wrow = pl.BlockSpec((hd, d), lambda g, i: (g, 0))
    vec = pl.BlockSpec((1, hd), lambda g, i: (0, g))
    acc8 = pltpu.VMEM((SUBLANES, hd), F32)
    return pl.pallas_call(
        body, name="mixer_bwd", grid=(HEADS, nt),
        in_specs=[row, row, wrow, wrow,
                  pl.BlockSpec((N_SAVE, tm, hd), lambda g, i: (0, rev(i), g)),
                  pl.BlockSpec((2, tm, hd), lambda g, i: (0, rev(i), g)),
                  vec, vec,
                  pl.BlockSpec((1, CHUNK, CHUNK), lambda g, i: (g, 0, 0)),
                  pl.BlockSpec((1, CHUNK, 1), lambda g, i: (g, 0, 0)),
                  pl.BlockSpec((3, hd), lambda g, i: (0, g))],
        out_specs=[pl.BlockSpec((N_IN, tm, hd), lambda g, i: (0, rev(i), g)),
                   pl.BlockSpec((1, CHUNK, CHUNK), lambda g, i: (g, 0, 0)),
                   pl.BlockSpec((1, CHUNK, 1), lambda g, i: (g, 0, 0)),
                   vec, vec,
                   pl.BlockSpec((3, hd), lambda g, i: (0, g)),
                   vec],
        out_shape=[jax.ShapeDtypeStruct((N_IN, t_len, d), BF16),
                   jax.ShapeDtypeStruct((HEADS, CHUNK, CHUNK), F32),
                   jax.ShapeDtypeStruct((HEADS, CHUNK, 1), F32),
                   jax.ShapeDtypeStruct((1, d), F32), jax.ShapeDtypeStruct((1, d), F32),
                   jax.ShapeDtypeStruct((3, d), F32), jax.ShapeDtypeStruct((1, d), F32)],
        scratch_shapes=[pltpu.VMEM((tm + SUBLANES, hd), F32), acc8,
                        pltpu.VMEM((CHUNK, CHUNK), F32), pltpu.VMEM((CHUNK, hd), F32),
                        acc8, acc8, acc8, acc8, acc8, acc8],
        compiler_params=_cparams(("parallel", "arbitrary")),
    )(da, db, w_oa, w_ob, saved, dgate, ln_v_g, ln_v_b, w_s, b_s3, conv_w)


def _grad_w_in(xb, dp):
    t_len, d = xb.shape
    tm, tn, tk = _tile(d, MM_TILE), _tile(d, MM_TILE), _tile(t_len, MM_DEPTH)
    nj = d // tn
    nk = t_len // tk

    def body(a_ref, b_ref, o_ref, acc):
        k = pl.program_id(2)

        @pl.when(k == 0)
        def _():
            acc[...] = jnp.zeros_like(acc)

        acc[...] += _dot_tn(a_ref[...], b_ref[...])

        @pl.when(k == nk - 1)
        def _():
            o_ref[...] = acc[...].astype(BF16)

    return pl.pallas_call(
        body, name="grad_w_in", grid=(d // tm, N_IN * nj, nk),
        in_specs=[pl.BlockSpec((tk, tm), lambda i, j, k: (k, i)),
                  pl.BlockSpec((None, tk, tn), lambda i, j, k: (j // nj, k, j % nj))],
        out_specs=pl.BlockSpec((tm, tn), lambda i, j, k: (i, j)),
        out_shape=jax.ShapeDtypeStruct((d, N_IN * d), BF16),
        scratch_shapes=[pltpu.VMEM((tm, tn), F32)],
        compiler_params=_cparams(("parallel", "parallel", "arbitrary")),
    )(xb, dp)


def _grad_x(dp, w_in, gx_direct, parts, packed):
    _, t_len, d = dp.shape
    tm, tn, tk = _tile(t_len, MM_TILE), _tile(d, MM_TILE), _tile(d, MM_DEPTH)
    nkb = d // tk
    nk = N_IN * nkb
    ni, nj = t_len // tm, d // tn
    n_parts = len(parts)

    def body(a_ref, b_ref, r_ref, *rest):
        srcs = rest[:n_parts]
        pk_ref = rest[n_parts]
        o_ref = rest[n_parts + 1]
        lands = rest[n_parts + 2:2 * n_parts + 2]
        gath = rest[2 * n_parts + 2]
        acc, send_sems, recv_sems, pk_send, pk_recv, pk_local = rest[2 * n_parts + 3:]
        i, j, k = pl.program_id(0), pl.program_id(1), pl.program_id(2)
        x, y, c = lax.axis_index("x"), lax.axis_index("y"), lax.axis_index("c")
        my_chip = 2 * x + y
        me = 4 * x + 2 * y + c
        chips = [(1 - x, y), (x, 1 - y), (1 - x, 1 - y)]

        def part_copy(t, n):
            px, py = chips[n]
            return pltpu.make_async_remote_copy(
                src_ref=srcs[t].at[2 * px + py], dst_ref=lands[t].at[my_chip],
                send_sem=send_sems.at[3 * t + n], recv_sem=recv_sems.at[3 * t + n],
                device_id=(px, py, c), device_id_type=MESH)

        def part_landing(t, n):
            px, py = chips[n]
            return pltpu.make_async_remote_copy(
                src_ref=srcs[t].at[my_chip], dst_ref=lands[t].at[2 * px + py],
                send_sem=send_sems.at[3 * t + n], recv_sem=recv_sems.at[3 * t + n],
                device_id=(px, py, c), device_id_type=MESH)

        def pk_copy(s):
            return pltpu.make_async_remote_copy(
                src_ref=pk_ref, dst_ref=gath.at[me], send_sem=pk_send, recv_sem=pk_recv.at[me],
                device_id=(s // 4, (s // 2) % 2, s % 2), device_id_type=MESH)

        @pl.when((i == 0) & (j == 0) & (k == 0))
        def _():
            for t in range(n_parts):
                for n in range(3):
                    part_copy(t, n).start()
            pltpu.make_async_copy(pk_ref, gath.at[me], pk_local).start()
            for s in range(N_DEV):
                @pl.when(s != me)
                def _(s=s):
                    pk_copy(s).start()

        @pl.when(k == 0)
        def _():
            acc[...] = r_ref[...]

        acc[...] += _dot(a_ref[...], b_ref[...])

        @pl.when(k == nk - 1)
        def _():
            o_ref[...] = acc[...]

        @pl.when((i == ni - 1) & (j == nj - 1) & (k == nk - 1))
        def _():
            for t in range(n_parts):
                for n in range(3):
                    part_landing(t, n).wait_recv()
            for t in range(n_parts):
                for n in range(3):
                    part_copy(t, n).wait_send()
            for s in range(N_DEV):
                @pl.when(s != me)
                def _(s=s):
                    pltpu.make_async_remote_copy(
                        src_ref=pk_ref, dst_ref=gath.at[s], send_sem=pk_send, recv_sem=pk_recv.at[s],
                        device_id=(s // 4, (s // 2) % 2, s % 2), device_id_type=MESH).wait_recv()
            seven = gath.at[pl.ds(0, N_DEV - 1)]
            pltpu.make_async_remote_copy(src_ref=seven, dst_ref=seven, send_sem=pk_send, recv_sem=pk_send,
                                         device_id=(x, y, 1 - c), device_id_type=MESH).wait_send()
            pltpu.make_async_copy(pk_ref, gath.at[me], pk_local).wait()

    hbm = pl.BlockSpec(memory_space=pl.ANY)
    outs = pl.pallas_call(
        body, name="grad_x", grid=(ni, nj, nk),
        in_specs=[pl.BlockSpec((None, tm, tk), lambda i, j, k: (k // nkb, i, k % nkb)),
                  pl.BlockSpec((tk, tn), lambda i, j, k: (k, j)),
                  pl.BlockSpec((tm, tn), lambda i, j, k: (i, j))] + [hbm] * (n_parts + 1),
        out_specs=[pl.BlockSpec((tm, tn), lambda i, j, k: (i, j))] + [hbm] * (n_parts + 1),
        out_shape=[jax.ShapeDtypeStruct((t_len, d), F32)] + [jax.ShapeDtypeStruct(p.shape, p.dtype) for p in parts]
        + [jax.ShapeDtypeStruct((N_DEV,) + packed.shape, packed.dtype)],
        scratch_shapes=[pltpu.VMEM((tm, tn), F32),
                        pltpu.SemaphoreType.DMA((3 * n_parts,)), pltpu.SemaphoreType.DMA((3 * n_parts,)),
                        pltpu.SemaphoreType.DMA(()), pltpu.SemaphoreType.DMA((N_DEV,)), pltpu.SemaphoreType.DMA(())],
        compiler_params=_cparams(("arbitrary", "arbitrary", "arbitrary")),
    )(dp, w_in, gx_direct, *parts, packed)
    return outs[0], list(outs[1:1 + n_parts]), outs[1 + n_parts]


def kernel(x, w_in, b_gate, ln_v_g, ln_v_b, w_s, b_s, conv_w, conv_b, w_oa, w_ob, w_out, ln_g, ln_b, loss_target, m_w_in, m_b_gate, m_ln_v_g, m_ln_v_b, m_w_s, m_b_s, m_conv_w, m_conv_b, m_w_oa, m_w_ob, m_w_out, m_ln_g, m_ln_b, v_w_in, v_b_gate, v_ln_v_g, v_ln_v_b, v_w_s, v_b_s, v_conv_w, v_conv_b, v_w_oa, v_w_ob, v_w_out, v_ln_g, v_ln_b):
    _, t_len, d = x.shape
    assert d % (HEADS * 128) == 0 and t_len % CHUNK == 0 and w_in.shape[2] * N_DEV == N_IN * d
    x2 = x[0]
    tgt2 = loss_target[0]
    c_arr = lax.axis_index("c").astype(jnp.int32).reshape(1)
    chip_arr = (2 * lax.axis_index("x") + lax.axis_index("y")).astype(jnp.int32).reshape(1)
    dev = 4 * lax.axis_index("x") + 2 * lax.axis_index("y") + lax.axis_index("c")

    xb = _cast_bf16(x2, "cast_x")
    shards = [_cast_transposed(w_in[0], "cast_w_in"), _cast_bf16(w_oa[0], "cast_w_oa"),
              _cast_bf16(w_ob[0], "cast_w_ob"), _cast_bf16(w_out[0], "cast_w_out")]
    (conv_w_g,) = _all_gather([conv_w[0]], ["lead"], "gather_conv_w", vmem=True)
    conv_w_f = jnp.transpose(conv_w_g, (1, 0, 2)).reshape(3, d)
    w_s3 = w_s[0]
    b_s3 = b_s[0].reshape(HEADS, CHUNK, 1)

    saved, ya, yb, w_in_f, w_oa_f, w_ob_f, w_out_f = _mixer_fwd(
        xb, shards, ln_v_g, ln_v_b, w_s3, b_s3, conv_w_f, conv_b)
    sa, sb, pa, pb, merged = _merge_fwd(xb, ya, yb, w_in_f, w_oa_f, w_ob_f, b_gate)
    dz, gx_direct, g_ln_g, g_ln_b, err2 = _out_ln_loss(merged, w_out_f, x2, tgt2, ln_g, ln_b)
    loss = lax.psum(0.5 * jnp.sum(err2) / d, ("x", "y", "c"))

    da, db, dgate, g_bga, g_bgb = _merge_bwd(dz, w_out_f, sa, sb, pa, pb)
    gw_out = _grad_w(merged, dz, "grad_w_out")
    gw_oa = _grad_w(ya, da, "grad_w_oa")
    gw_ob = _grad_w(yb, db, "grad_w_ob")
    dp, g_ws, g_bs, g_lvg, g_lvb, g_cw, g_cb = _mixer_bwd(
        da, db, w_oa_f, w_ob_f, saved, dgate, ln_v_g, ln_v_b, w_s3, b_s3, conv_w_f)
    gw_in = _grad_w_in(xb, dp)

    grads = [gw_in, gw_oa, gw_ob, gw_out]
    kinds = ["col", "row", "row", "row"]
    lands = _pair_exchange(grads, kinds, "grad_pair_exchange")
    parts = [_pair_sum(g, l, k, c_arr, "grad_pair_sum_%d" % n) for n, (g, l, k) in enumerate(zip(grads, lands, kinds))]
    pieces = [jnp.concatenate([g_bga, g_bgb], axis=1), g_lvg, g_lvb, g_ws, g_bs, g_cw, g_cb, g_ln_g, g_ln_b]
    sizes = [p.size for p in pieces]
    packed = jnp.concatenate([p.reshape(-1, 128) for p in pieces], axis=0)
    grad_x, lands2, gathered = _grad_x(dp, w_in_f, gx_direct, parts, packed)
    grad_x = grad_x[None]

    big = []
    for n, (w, m, v) in enumerate([(w_in, m_w_in, v_w_in), (w_oa, m_w_oa, v_w_oa), (w_ob, m_w_ob, v_w_ob),
                                   (w_out, m_w_out, v_w_out)]):
        big.append([o[None] for o in _sum_adam(parts[n], lands2[n], w[0], m[0], v[0], chip_arr, "sum_adam_%d" % n)])
    (g_w_in, d_w_in, nm_w_in, nv_w_in), (g_w_oa, d_w_oa, nm_w_oa, nv_w_oa), \
        (g_w_ob, d_w_ob, nm_w_ob, nv_w_ob), (g_w_out, d_w_out, nm_w_out, nv_w_out) = big

    total = _small_sum(gathered, "sum_small_grads")
    offs = [0]
    for s in sizes:
        offs.append(offs[-1] + s // 128)
    unpacked = [total[offs[n]:offs[n + 1]] for n in range(len(pieces))]
    g_b_gate = unpacked[0].reshape(b_gate.shape)
    g_ln_v_g = unpacked[1].reshape(ln_v_g.shape)
    g_ln_v_b = unpacked[2].reshape(ln_v_b.shape)
    g_w_s = unpacked[3].reshape(w_s.shape)
    g_b_s = unpacked[4].reshape(b_s.shape)
    g_conv_w = lax.dynamic_slice_in_dim(unpacked[5].reshape(3, d), dev * (d // N_DEV), d // N_DEV, axis=1)[None]
    g_conv_b = unpacked[6].reshape(conv_b.shape)
    g_ln_g2 = unpacked[7].reshape(ln_g.shape)
    g_ln_b2 = unpacked[8].reshape(ln_b.shape)

    small_w = [b_gate, ln_v_g, ln_v_b, w_s, b_s, conv_w, conv_b, ln_g, ln_b]
    small_g = [g_b_gate, g_ln_v_g, g_ln_v_b, g_w_s, g_b_s, g_conv_w, g_conv_b, g_ln_g2, g_ln_b2]
    small_m = [m_b_gate, m_ln_v_g, m_ln_v_b, m_w_s, m_b_s, m_conv_w, m_conv_b, m_ln_g, m_ln_b]
    small_v = [v_b_gate, v_ln_v_g, v_ln_v_b, v_w_s, v_b_s, v_conv_w, v_conv_b, v_ln_g, v_ln_b]

    def flat(a):
        return a.reshape(-1, a.shape[-1])

    res = _small_adam([flat(a) for a in small_w], [flat(a) for a in small_g], [flat(a) for a in small_m],
                      [flat(a) for a in small_v], "adam_small")
    ns = len(small_w)
    d_s = [res[n].reshape(small_w[n].shape) for n in range(ns)]
    nm_s = [res[ns + n].reshape(small_w[n].shape) for n in range(ns)]
    nv_s = [res[2 * ns + n].reshape(small_w[n].shape) for n in range(ns)]

    def ordered(first, small, oa, ob, out):
        return [first] + small[:7] + [oa, ob, out] + small[7:]

    return (loss, grad_x,
            *ordered(g_w_in, small_g, g_w_oa, g_w_ob, g_w_out),
            *ordered(d_w_in, d_s, d_w_oa, d_w_ob, d_w_out),
            *ordered(nm_w_in, nm_s, nm_w_oa, nm_w_ob, nm_w_out),
            *ordered(nv_w_in, nv_s, nv_w_oa, nv_w_ob, nv_w_out))
```

```python
import functools

import jax
import jax.numpy as jnp
from jax import lax
from jax.experimental import pallas as pl
from jax.experimental.pallas import tpu as pltpu

F32 = jnp.float32
BF16 = jnp.bfloat16
MESH = pl.DeviceIdType.MESH

N_DEV = 8
N_CHIP = 4
HEADS = 8
CHUNK = 128
N_IN = 9
N_MIX = 7
N_SAVE = 8
LN_EPS = 1e-5
DN_ALPHA = 2.0 ** 0.25
ADAM_LR = 0.001
ADAM_B1 = 0.9
ADAM_B2 = 0.999
ADAM_EPS = 1e-08
ADAM_WD = 0.01
ADAM_STEP = 10
GELU_C0 = 0.7978845608028654
GELU_C1 = 0.044715
SUBLANES = 8
FIRST_SENDS = 4
SENDS_PER_GROUP = 2
MM_TILE = 1024
MM_DEPTH = 2048
VMEM_LIMIT = 56 << 20


def _cparams(sem):
    return pltpu.CompilerParams(dimension_semantics=sem, vmem_limit_bytes=VMEM_LIMIT)


def _tile(n, want):
    t = min(n, want)
    while n % t:
        t //= 2
    return t


def _gelu(u):
    u2 = u * u
    t = jnp.tanh(u * (GELU_C0 + (GELU_C0 * GELU_C1) * u2))
    hp = 0.5 * t + 0.5
    grad = hp + (0.5 * u) * (1.0 - t * t) * (GELU_C0 + (3.0 * GELU_C0 * GELU_C1) * u2)
    return u * hp, grad


def _silu(z):
    s = jax.nn.sigmoid(z)
    sil = z * s
    return sil, s + sil * (1.0 - s)


def _fold8(a):
    return jnp.sum(a.reshape(a.shape[0] // SUBLANES, SUBLANES, a.shape[1]), axis=0)


def _dot(a, b):
    return jnp.dot(a, b, preferred_element_type=F32)


def _dot_nt(a, b):
    return lax.dot_general(a, b, (((1,), (1,)), ((), ())), preferred_element_type=F32)


def _dot_tn(a, b):
    return lax.dot_general(a, b, (((0,), (0,)), ((), ())), preferred_element_type=F32)


def _tril_mask():
    r = lax.broadcasted_iota(jnp.int32, (CHUNK, CHUNK), 0)
    c = lax.broadcasted_iota(jnp.int32, (CHUNK, CHUNK), 1)
    return c <= r


def _cast_bf16(a, name):
    rows, cols = a.shape
    rb = _tile(rows, 256)

    def body(a_ref, o_ref):
        o_ref[...] = a_ref[...].astype(BF16)

    return pl.pallas_call(
        body, name=name, grid=(rows // rb,),
        in_specs=[pl.BlockSpec((rb, cols), lambda i: (i, 0))],
        out_specs=pl.BlockSpec((rb, cols), lambda i: (i, 0)),
        out_shape=jax.ShapeDtypeStruct((rows, cols), BF16),
        compiler_params=_cparams(("parallel",)),
    )(a)


def _cast_transposed(a, name):
    rows, cols = a.shape
    tb = _tile(rows, 2048)
    tc = _tile(cols, 256)

    def body(a_ref, o_ref):
        o_ref[...] = a_ref[...].T.astype(BF16)

    return pl.pallas_call(
        body, name=name, grid=(cols // tc, rows // tb),
        in_specs=[pl.BlockSpec((tb, tc), lambda j, i: (i, j))],
        out_specs=pl.BlockSpec((tc, tb), lambda j, i: (j, i)),
        out_shape=jax.ShapeDtypeStruct((cols, rows), BF16),
        compiler_params=_cparams(("parallel", "parallel")),
    )(a)


def _shard_ref(full, kind, s, n):
    if kind == "col":
        return full.at[:, pl.ds(pl.multiple_of(s * n, 128), n)]
    if kind == "row":
        return full.at[pl.ds(pl.multiple_of(s * n, SUBLANES), n), :]
    return full.at[s]


def _all_gather(shards, kinds, name, vmem):
    nt = len(shards)
    out_shapes = []
    for a, kind in zip(shards, kinds):
        if kind == "col":
            out_shapes.append(jax.ShapeDtypeStruct((a.shape[0], N_DEV * a.shape[1]), a.dtype))
        elif kind == "row":
            out_shapes.append(jax.ShapeDtypeStruct((N_DEV * a.shape[0], a.shape[1]), a.dtype))
        else:
            out_shapes.append(jax.ShapeDtypeStruct((N_DEV,) + a.shape, a.dtype))

    def body(*refs):
        srcs, fulls = refs[:nt], refs[nt:2 * nt]
        send_sems, recv_sems, local_sems = refs[2 * nt:]
        x, y, c = lax.axis_index("x"), lax.axis_index("y"), lax.axis_index("c")
        sibling = (x, y, 1 - c)
        chips = [(1 - x, y), (x, 1 - y), (1 - x, 1 - y)]

        def dev(px, py, pc):
            return 4 * px + 2 * py + pc

        def region(t, s):
            a, kind = shards[t], kinds[t]
            n = a.shape[1] if kind == "col" else a.shape[0]
            return _shard_ref(fulls[t], kind, s, n)

        def copy(t, k, block, to, own=False):
            return pltpu.make_async_remote_copy(
                src_ref=srcs[t] if own else region(t, block), dst_ref=region(t, block),
                send_sem=send_sems.at[7 * t + k], recv_sem=recv_sems.at[7 * t + k],
                device_id=to, device_id_type=MESH)

        me = dev(x, y, c)
        started = []
        for t in range(nt):
            mine = pltpu.make_async_copy(srcs[t], region(t, me), local_sems.at[t])
            mine.start()
            started.append(mine)
        first = []
        for t in range(nt):
            first.append(copy(t, 0, me, sibling, own=True))
            for j, chip in enumerate(chips):
                first.append(copy(t, 1 + j, me, (*chip, c), own=True))
        for cp in first:
            cp.start()
        passed = []
        for t in range(nt):
            for j, chip in enumerate(chips):
                blk = dev(*chip, c)
                copy(t, 1 + j, blk, sibling).wait_recv()
                fwd = copy(t, 4 + j, blk, sibling)
                fwd.start()
                passed.append(fwd)
        for t in range(nt):
            copy(t, 0, dev(x, y, 1 - c), sibling).wait_recv()
            for j, chip in enumerate(chips):
                copy(t, 4 + j, dev(*chip, 1 - c), sibling).wait_recv()
        for cp in first + passed:
            cp.wait_send()
        for mine in started:
            mine.wait()

    space = pltpu.VMEM if vmem else pl.ANY
    return pl.pallas_call(
        body, name=name,
        in_specs=[pl.BlockSpec(memory_space=space)] * nt,
        out_specs=[pl.BlockSpec(memory_space=space)] * nt,
        out_shape=out_shapes,
        scratch_shapes=[pltpu.SemaphoreType.DMA((7 * nt,)), pltpu.SemaphoreType.DMA((7 * nt,)),
                        pltpu.SemaphoreType.DMA((nt,))],
        compiler_params=pltpu.CompilerParams(vmem_limit_bytes=VMEM_LIMIT, has_side_effects=True),
    )(*shards)


def _pair_exchange(grads, kinds, name):
    nt = len(grads)
    shard_shapes = []
    for g, kind in zip(grads, kinds):
        shard_shapes.append((g.shape[0], g.shape[1] // N_DEV) if kind == "col" else (g.shape[0] // N_DEV, g.shape[1]))

    def body(*refs):
        srcs, lands = refs[:nt], refs[nt:2 * nt]
        send_sems, recv_sems = refs[2 * nt:]
        x, y, c = lax.axis_index("x"), lax.axis_index("y"), lax.axis_index("c")
        copies = []
        for t in range(nt):
            n = shard_shapes[t][1] if kinds[t] == "col" else shard_shapes[t][0]
            for k in range(N_CHIP):
                cp = pltpu.make_async_remote_copy(
                    src_ref=_shard_ref(srcs[t], kinds[t], 2 * k + 1 - c, n), dst_ref=lands[t].at[k],
                    send_sem=send_sems.at[N_CHIP * t + k], recv_sem=recv_sems.at[N_CHIP * t + k],
                    device_id=(x, y, 1 - c), device_id_type=MESH)
                cp.start()
                copies.append(cp)
        for cp in copies:
            cp.wait()

    return pl.pallas_call(
        body, name=name,
        in_specs=[pl.BlockSpec(memory_space=pl.ANY)] * nt,
        out_specs=[pl.BlockSpec(memory_space=pl.ANY)] * nt,
        out_shape=[jax.ShapeDtypeStruct((N_CHIP,) + s, g.dtype) for s, g in zip(shard_shapes, grads)],
        scratch_shapes=[pltpu.SemaphoreType.DMA((N_CHIP * nt,)), pltpu.SemaphoreType.DMA((N_CHIP * nt,))],
        compiler_params=pltpu.CompilerParams(has_side_effects=True),
    )(*grads)


def _pair_sum(grad, land, kind, c_arr, name):
    _, r, w = land.shape
    rb = _tile(r, 256)
    nrb = r // rb

    def body(c_ref, g_ref, l_ref, o_ref):
        o_ref[...] = (g_ref[...].astype(F32) + l_ref[...].astype(F32)).astype(o_ref.dtype)

    if kind == "col":
        g_spec = pl.BlockSpec((rb, w), lambda k, i, c: (i, 2 * k + c[0]))
    else:
        g_spec = pl.BlockSpec((rb, w), lambda k, i, c: ((2 * k + c[0]) * nrb + i, 0))
    return pl.pallas_call(
        body, name=name,
        grid_spec=pltpu.PrefetchScalarGridSpec(
            num_scalar_prefetch=1, grid=(N_CHIP, nrb),
            in_specs=[g_spec, pl.BlockSpec((None, rb, w), lambda k, i, c: (k, i, 0))],
            out_specs=pl.BlockSpec((None, rb, w), lambda k, i, c: (k, i, 0))),
        out_shape=jax.ShapeDtypeStruct(land.shape, BF16),
        compiler_params=_cparams(("parallel", "parallel")),
    )(c_arr, grad, land)


def _adam(w, g, m, v):
    m = ADAM_B1 * m + (1.0 - ADAM_B1) * g
    v = ADAM_B2 * v + (1.0 - ADAM_B2) * jnp.square(g)
    m_hat = m / (1.0 - ADAM_B1 ** ADAM_STEP)
    v_hat = v / (1.0 - ADAM_B2 ** ADAM_STEP)
    delta = -ADAM_LR * (m_hat / (jnp.sqrt(v_hat) + ADAM_EPS) + ADAM_WD * w)
    return delta, m, v


def _sum_adam(part, land, w, m, v, chip_arr, name):
    r, wd = w.shape
    rb = _tile(r, 128)

    def body(k_ref, own, r1, r2, r3, w_ref, m_ref, v_ref, g_out, d_out, m_out, v_out):
        g = own[...].astype(F32) + r1[...].astype(F32) + r2[...].astype(F32) + r3[...].astype(F32)
        d, mn, vn = _adam(w_ref[...], g, m_ref[...], v_ref[...])
        g_out[...] = g
        d_out[...] = d
        m_out[...] = mn
        v_out[...] = vn

    def slot(off):
        return pl.BlockSpec((None, rb, wd), lambda i, k: ((k[0] + off) % N_CHIP, i, 0))

    plain = pl.BlockSpec((rb, wd), lambda i, k: (i, 0))
    return pl.pallas_call(
        body, name=name,
        grid_spec=pltpu.PrefetchScalarGridSpec(
            num_scalar_prefetch=1, grid=(r // rb,),
            in_specs=[slot(0), slot(1), slot(2), slot(3), plain, plain, plain],
            out_specs=[plain] * 4),
        out_shape=[jax.ShapeDtypeStruct(w.shape, F32)] * 4,
        compiler_params=_cparams(("parallel",)),
    )(chip_arr, part, land, land, land, w, m, v)


def _small_sum(gathered, name):
    _, r, w = gathered.shape

    def body(g_ref, o_ref):
        acc = g_ref[0]
        for d in range(1, N_DEV):
            acc = acc + g_ref[d]
        o_ref[...] = acc

    return pl.pallas_call(body, name=name, out_shape=jax.ShapeDtypeStruct((r, w), F32))(gathered)


def _small_adam(ws, gs, ms, vs, name):
    n = len(ws)

    def body(*refs):
        ins, outs = refs[:4 * n], refs[4 * n:]
        for t in range(n):
            d, mn, vn = _adam(ins[t][...], ins[n + t][...], ins[2 * n + t][...], ins[3 * n + t][...])
            outs[t][...] = d
            outs[n + t][...] = mn
            outs[2 * n + t][...] = vn

    shapes = [jax.ShapeDtypeStruct(w.shape, F32) for w in ws]
    return pl.pallas_call(body, name=name, out_shape=shapes * 3)(*ws, *gs, *ms, *vs)


def _mixer_fwd(xb, shards, ln_v_g, ln_v_b, w_s, b_s3, conv_w, conv_b):
    t_len, d = xb.shape
    hd = d // HEADS
    tm = _tile(t_len, 512)
    nt = t_len // tm
    assert nt >= 2
    w8 = shards[0].shape[0]
    bps = w8 // hd
    r8 = shards[1].shape[0]
    nq = N_IN * HEADS
    n_blocks = nq + 3 * N_DEV
    groups = [[("in", HEADS * b + k) for b in range(N_MIX)] for k in range(HEADS)]
    groups.append([("in", q) for q in range(N_MIX * HEADS, nq)] + [(t, s) for t in (1, 2, 3) for s in range(N_DEV)])

    def owner_of(blk):
        return blk[1] // bps if blk[0] == "in" else blk[1]

    send_step = {}
    for s in range(N_DEV):
        mine = [blk for grp in groups for blk in grp if owner_of(blk) == s]
        for pos, blk in enumerate(mine):
            send_step[blk] = 0 if pos < FIRST_SENDS else 1 + (pos - FIRST_SENDS) // SENDS_PER_GROUP
    assert max(send_step.values()) < HEADS
    tail_pass_step = nt - 6 if nt >= 8 else nt - 2

    def body(x_ref, sh_in, sh_oa, sh_ob, sh_out, lng, lnb, ws_ref, bs_ref, cw_ref, cb_ref,
             save_ref, ya_ref, yb_ref, f_in, f_oa, f_ob, f_out,
             hbuf, wbuf, recv_sems, own_sems, fwd_sems, local_sems, load_sems):
        g = pl.program_id(0)
        i = pl.program_id(1)
        x, y, c = lax.axis_index("x"), lax.axis_index("y"), lax.axis_index("c")
        sibling = (x, y, 1 - c)
        chips = [(1 - x, y), (x, 1 - y), (1 - x, 1 - y)]
        shard_refs = (sh_in, sh_oa, sh_ob, sh_out)
        fulls = (f_in, f_oa, f_ob, f_out)

        def tensor(blk):
            return 0 if blk[0] == "in" else blk[0]

        def owner(blk):
            s = owner_of(blk)
            return s // 4, (s // 2) % 2, s % 2

        def bid(blk):
            return blk[1] if blk[0] == "in" else nq + (blk[0] - 1) * N_DEV + blk[1]

        def region(blk):
            if blk[0] == "in":
                return f_in.at[pl.ds(blk[1] * hd, hd), :]
            return fulls[blk[0]].at[pl.ds(blk[1] * r8, r8), :]

        def own_src(blk):
            if blk[0] == "in":
                return sh_in.at[pl.ds((blk[1] % bps) * hd, hd), :]
            return shard_refs[blk[0]]

        def rcopy(blk, to, send_sem, own):
            return pltpu.make_async_remote_copy(
                src_ref=own_src(blk) if own else region(blk), dst_ref=region(blk),
                send_sem=send_sem, recv_sem=recv_sems.at[bid(blk)], device_id=to, device_id_type=MESH)

        def send_own(blk):
            ox, oy, oc = owner(blk)

            @pl.when((x == ox) & (y == oy) & (c == oc))
            def _():
                rcopy(blk, sibling, own_sems.at[tensor(blk)], True).start()
                for chip in chips:
                    rcopy(blk, (*chip, c), own_sems.at[tensor(blk)], True).start()

        def pass_on(blk):
            ox, oy, oc = owner(blk)

            @pl.when(((x != ox) | (y != oy)) & (c == oc))
            def _():
                rcopy(blk, sibling, fwd_sems.at[tensor(blk)], False).wait_recv()
                rcopy(blk, sibling, fwd_sems.at[tensor(blk)], False).start()

        def wait_from_sibling(blk):
            @pl.when(c != owner(blk)[2])
            def _():
                rcopy(blk, sibling, fwd_sems.at[tensor(blk)], False).wait_recv()

        def local_copy(t):
            me = 4 * x + 2 * y + c
            return pltpu.make_async_copy(shard_refs[t], _shard_ref(fulls[t], "row", me, w8 if t == 0 else r8),
                                         local_sems.at[t])

        first = (g == 0) & (i == 0)

        @pl.when(first)
        def _():
            for t in range(4):
                local_copy(t).start()
            for grp in groups:
                for blk in grp:
                    if send_step[blk] == 0:
                        send_own(blk)
            for blk in groups[0]:
                pass_on(blk)

        @pl.when(i == nt - 2)
        def _():
            for k in range(HEADS - 1):
                @pl.when(g == k)
                def _(k=k):
                    for blk in groups[k + 1]:
                        pass_on(blk)

        @pl.when((g == HEADS - 1) & (i == tail_pass_step))
        def _():
            for blk in groups[HEADS]:
                pass_on(blk)

        @pl.when(i == 0)
        def _():
            for k in range(HEADS):
                @pl.when(g == k)
                def _(k=k):
                    if k > 0:
                        for grp in groups:
                            for blk in grp:
                                if send_step[blk] == k:
                                    send_own(blk)
                    for blk in groups[k]:
                        wait_from_sibling(blk)
                    for b, blk in enumerate(groups[k]):
                        ox, oy, oc = owner(blk)
                        mine = (x == ox) & (y == oy) & (c == oc)

                        @pl.when(mine)
                        def _(b=b, blk=blk):
                            pltpu.make_async_copy(own_src(blk), wbuf.at[b], load_sems.at[b]).start()

                        @pl.when(jnp.logical_not(mine))
                        def _(b=b, blk=blk):
                            pltpu.make_async_copy(region(blk), wbuf.at[b], load_sems.at[b]).start()

                    for b, blk in enumerate(groups[k]):
                        pltpu.make_async_copy(region(blk), wbuf.at[b], load_sems.at[b]).wait()

        xt = x_ref[...]
        u = _dot_nt(xt, wbuf[0])
        v = _dot_nt(xt, wbuf[1])
        z = _dot_nt(xt, wbuf[2])
        save_ref[0] = u.astype(BF16)
        save_ref[1] = v.astype(BF16)
        save_ref[2] = z.astype(BF16)
        gu, _ = _gelu(u)
        gv, _ = _gelu(v)
        mu = jnp.mean(gv, axis=-1, keepdims=True)
        dv = gv - mu
        var = jnp.mean(dv * dv, axis=-1, keepdims=True)
        vn = (dv * lax.rsqrt(var + LN_EPS) * lng[...] + lnb[...]).astype(BF16)
        sz, _ = _silu(z)
        gate = gu * sz
        wm = jnp.where(_tril_mask(), ws_ref[0], 0.0).astype(BF16)
        bs = bs_ref[0]
        for ck in range(tm // CHUNK):
            rows = slice(ck * CHUNK, (ck + 1) * CHUNK)
            mixed = _dot(wm, vn[rows]) + bs
            ya_ref[rows, :] = (gate[rows] * mixed).astype(BF16)

        xbv = _dot_nt(xt, wbuf[3])
        cbv = _dot_nt(xt, wbuf[4])
        bbv = _dot_nt(xt, wbuf[5])
        zbv = _dot_nt(xt, wbuf[6])
        save_ref[3] = xbv.astype(BF16)
        save_ref[4] = cbv.astype(BF16)
        save_ref[5] = bbv.astype(BF16)
        save_ref[6] = zbv.astype(BF16)
        h = cbv * xbv

        @pl.when(i == 0)
        def _():
            hbuf[0:SUBLANES, :] = jnp.zeros((SUBLANES, hd), F32)

        hbuf[SUBLANES:SUBLANES + tm, :] = h
        h1 = hbuf[SUBLANES - 1:SUBLANES - 1 + tm, :]
        h2 = hbuf[SUBLANES - 2:SUBLANES - 2 + tm, :]
        conv = cb_ref[...] + cw_ref[0:1, :] * h2 + cw_ref[1:2, :] * h1 + cw_ref[2:3, :] * h
        hbuf[0:SUBLANES, :] = h[tm - SUBLANES:tm, :]
        save_ref[7] = conv.astype(BF16)
        szb, _ = _silu(zbv)
        yb_ref[...] = (bbv * conv * szb).astype(BF16)

        @pl.when((g == HEADS - 1) & (i == nt - 1))
        def _():
            for blk in groups[HEADS]:
                wait_from_sibling(blk)
            for t in range(4):
                local_copy(t).wait()
            for t in range(4):
                n = w8 if t == 0 else r8
                own_all, fwd_all = fulls[t].at[pl.ds(0, 4 * n), :], fulls[t].at[pl.ds(0, 3 * n), :]
                for ref, sem in ((own_all, own_sems.at[t]), (fwd_all, fwd_sems.at[t])):
                    pltpu.make_async_remote_copy(src_ref=ref, dst_ref=ref, send_sem=sem, recv_sem=sem,
                                                 device_id=sibling, device_id_type=MESH).wait_send()

    vec = pl.BlockSpec((1, hd), lambda g, i: (0, g))
    hbm = pl.BlockSpec(memory_space=pl.ANY)
    return pl.pallas_call(
        body, name="mixer_fwd", grid=(HEADS, nt),
        in_specs=[pl.BlockSpec((tm, d), lambda g, i: (i, 0)), hbm, hbm, hbm, hbm,
                  vec, vec,
                  pl.BlockSpec((1, CHUNK, CHUNK), lambda g, i: (g, 0, 0)),
                  pl.BlockSpec((1, CHUNK, 1), lambda g, i: (g, 0, 0)),
                  pl.BlockSpec((3, hd), lambda g, i: (0, g)),
                  vec],
        out_specs=[pl.BlockSpec((N_SAVE, tm, hd), lambda g, i: (0, i, g)),
                   pl.BlockSpec((tm, hd), lambda g, i: (i, g)),
                   pl.BlockSpec((tm, hd), lambda g, i: (i, g)),
                   hbm, hbm, hbm, hbm],
        out_shape=[jax.ShapeDtypeStruct((N_SAVE, t_len, d), BF16),
                   jax.ShapeDtypeStruct((t_len, d), BF16),
                   jax.ShapeDtypeStruct((t_len, d), BF16),
                   jax.ShapeDtypeStruct((N_DEV * w8, d), BF16),
                   jax.ShapeDtypeStruct((d, d), BF16), jax.ShapeDtypeStruct((d, d), BF16),
                   jax.ShapeDtypeStruct((d, d), BF16)],
        scratch_shapes=[pltpu.VMEM((SUBLANES + tm, hd), F32), pltpu.VMEM((N_MIX, hd, d), BF16),
                        pltpu.SemaphoreType.DMA((n_blocks,)), pltpu.SemaphoreType.DMA((4,)),
                        pltpu.SemaphoreType.DMA((4,)), pltpu.SemaphoreType.DMA((4,)),
                        pltpu.SemaphoreType.DMA((N_MIX,))],
        compiler_params=_cparams(("arbitrary", "arbitrary")),
    )(xb, *shards, ln_v_g, ln_v_b, w_s, b_s3, conv_w, conv_b)


def _merge_fwd(xb, ya, yb, w_in, w_oa, w_ob, b_gate):
    t_len, d = xb.shape
    tm = _tile(t_len, 512)
    tn = _tile(d, 512)
    nj = d // tn

    def body(x_ref, ya_ref, yb_ref, wga, wgb, woa, wob, bga, bgb, sa_ref, sb_ref, pa_ref, pb_ref, mg_ref):
        xt = x_ref[...]
        sa = jax.nn.sigmoid(_dot_nt(xt, wga[...]) + bga[...])
        sb = jax.nn.sigmoid(_dot_nt(xt, wgb[...]) + bgb[...])
        pa = _dot(ya_ref[...], woa[...])
        pb = _dot(yb_ref[...], wob[...])
        sa_ref[...] = sa.astype(BF16)
        sb_ref[...] = sb.astype(BF16)
        pa_ref[...] = pa.astype(BF16)
        pb_ref[...] = pb.astype(BF16)
        mg_ref[...] = (sa * pa + sb * pb).astype(BF16)

    row = pl.BlockSpec((tm, d), lambda j, i: (i, 0))
    out = pl.BlockSpec((tm, tn), lambda j, i: (i, j))
    return pl.pallas_call(
        body, name="merge_fwd", grid=(nj, t_len // tm),
        in_specs=[row, row, row,
                  pl.BlockSpec((tn, d), lambda j, i: (7 * nj + j, 0)),
                  pl.BlockSpec((tn, d), lambda j, i: (8 * nj + j, 0)),
                  pl.BlockSpec((d, tn), lambda j, i: (0, j)),
                  pl.BlockSpec((d, tn), lambda j, i: (0, j)),
                  pl.BlockSpec((1, tn), lambda j, i: (0, j)),
                  pl.BlockSpec((1, tn), lambda j, i: (0, nj + j))],
        out_specs=[out] * 5,
        out_shape=[jax.ShapeDtypeStruct((t_len, d), BF16)] * 5,
        compiler_params=_cparams(("parallel", "arbitrary")),
    )(xb, ya, yb, w_in, w_in, w_oa, w_ob, b_gate, b_gate)


def _out_ln_loss(merged, w_out, x, target, ln_g, ln_b):
    t_len, d = x.shape
    tm = _tile(t_len, 256)
    nt = t_len // tm

    def body(mg_ref, w_ref, x_ref, t_ref, g_ref, b_ref, dz_ref, gx_ref, glg_ref, glb_ref, ls_ref, a_g, a_b, a_l):
        i = pl.program_id(0)

        @pl.when(i == 0)
        def _():
            a_g[...] = jnp.zeros_like(a_g)
            a_b[...] = jnp.zeros_like(a_b)
            a_l[...] = jnp.zeros_like(a_l)

        zres = DN_ALPHA * x_ref[...] + _dot(mg_ref[...], w_ref[...])
        mu = jnp.mean(zres, axis=-1, keepdims=True)
        dc = zres - mu
        var = jnp.mean(dc * dc, axis=-1, keepdims=True)
        rstd = lax.rsqrt(var + LN_EPS)
        xhat = dc * rstd
        g = g_ref[...]
        err = xhat * g + b_ref[...] - t_ref[...]
        dy = err * (1.0 / d)
        a_l[...] += _fold8(err * err)
        a_g[...] += _fold8(dy * xhat)
        a_b[...] += _fold8(dy)
        dxh = dy * g
        m1 = jnp.mean(dxh, axis=-1, keepdims=True)
        m2 = jnp.mean(dxh * xhat, axis=-1, keepdims=True)
        dz = rstd * (dxh - m1 - xhat * m2)
        dz_ref[...] = dz.astype(BF16)
        gx_ref[...] = DN_ALPHA * dz

        @pl.when(i == nt - 1)
        def _():
            glg_ref[...] = jnp.sum(a_g[...], axis=0, keepdims=True)
            glb_ref[...] = jnp.sum(a_b[...], axis=0, keepdims=True)
            ls_ref[...] = jnp.sum(a_l[...], axis=0, keepdims=True)

    row = pl.BlockSpec((tm, d), lambda i: (i, 0))
    vec = pl.BlockSpec((1, d), lambda i: (0, 0))
    return pl.pallas_call(
        body, name="out_ln_loss", grid=(nt,),
        in_specs=[row, pl.BlockSpec((d, d), lambda i: (0, 0)), row, row, vec, vec],
        out_specs=[row, row, vec, vec, vec],
        out_shape=[jax.ShapeDtypeStruct((t_len, d), BF16), jax.ShapeDtypeStruct((t_len, d), F32)]
        + [jax.ShapeDtypeStruct((1, d), F32)] * 3,
        scratch_shapes=[pltpu.VMEM((SUBLANES, d), F32)] * 3,
        compiler_params=_cparams(("arbitrary",)),
    )(merged, w_out, x, target, ln_g, ln_b)


def _merge_bwd(dz, w_out, sa, sb, pa, pb):
    t_len, d = dz.shape
    tm = _tile(t_len, 256)
    nt = t_len // tm

    def body(dz_ref, w_ref, sa_ref, sb_ref, pa_ref, pb_ref, da_ref, db_ref, dg_ref, ga_ref, gb_ref, acc_a, acc_b):
        i = pl.program_id(0)

        @pl.when(i == 0)
        def _():
            acc_a[...] = jnp.zeros_like(acc_a)
            acc_b[...] = jnp.zeros_like(acc_b)

        dm = _dot_nt(dz_ref[...], w_ref[...])
        sa = sa_ref[...].astype(F32)
        sb = sb_ref[...].astype(F32)
        da = dm * sa
        db = dm * sb
        da_ref[...] = da.astype(BF16)
        db_ref[...] = db.astype(BF16)
        dga = da * pa_ref[...].astype(F32) * (1.0 - sa)
        dgb = db * pb_ref[...].astype(F32) * (1.0 - sb)
        dg_ref[0] = dga.astype(BF16)
        dg_ref[1] = dgb.astype(BF16)
        acc_a[...] += _fold8(dga)
        acc_b[...] += _fold8(dgb)

        @pl.when(i == nt - 1)
        def _():
            ga_ref[...] = jnp.sum(acc_a[...], axis=0, keepdims=True)
            gb_ref[...] = jnp.sum(acc_b[...], axis=0, keepdims=True)

    row = pl.BlockSpec((tm, d), lambda i: (i, 0))
    vec = pl.BlockSpec((1, d), lambda i: (0, 0))
    return pl.pallas_call(
        body, name="merge_bwd", grid=(nt,),
        in_specs=[row, pl.BlockSpec((d, d), lambda i: (0, 0)), row, row, row, row],
        out_specs=[row, row, pl.BlockSpec((2, tm, d), lambda i: (0, i, 0)), vec, vec],
        out_shape=[jax.ShapeDtypeStruct((t_len, d), BF16)] * 2 + [jax.ShapeDtypeStruct((2, t_len, d), BF16)]
        + [jax.ShapeDtypeStruct((1, d), F32)] * 2,
        scratch_shapes=[pltpu.VMEM((SUBLANES, d), F32)] * 2,
        compiler_params=_cparams(("arbitrary",)),
    )(dz, w_out, sa, sb, pa, pb)


def _grad_w(a, b, name):
    t_len, m = a.shape
    n = b.shape[1]
    tm, tn, tk = _tile(m, MM_TILE), _tile(n, MM_TILE), _tile(t_len, MM_DEPTH)
    nk = t_len // tk

    def body(a_ref, b_ref, o_ref, acc):
        k = pl.program_id(2)

        @pl.when(k == 0)
        def _():
            acc[...] = jnp.zeros_like(acc)

        acc[...] += _dot_tn(a_ref[...], b_ref[...])

        @pl.when(k == nk - 1)
        def _():
            o_ref[...] = acc[...].astype(BF16)

    return pl.pallas_call(
        body, name=name, grid=(m // tm, n // tn, nk),
        in_specs=[pl.BlockSpec((tk, tm), lambda i, j, k: (k, i)), pl.BlockSpec((tk, tn), lambda i, j, k: (k, j))],
        out_specs=pl.BlockSpec((tm, tn), lambda i, j, k: (i, j)),
        out_shape=jax.ShapeDtypeStruct((m, n), BF16),
        scratch_shapes=[pltpu.VMEM((tm, tn), F32)],
        compiler_params=_cparams(("parallel", "parallel", "arbitrary")),
    )(a, b)


def _mixer_bwd(da, db, w_oa, w_ob, saved, dgate, ln_v_g, ln_v_b, w_s, b_s3, conv_w):
    t_len, d = da.shape
    hd = d // HEADS
    tm = _tile(t_len, 512)
    nt = t_len // tm

    def body(da_ref, db_ref, woa, wob, sv, dgt, lng, lnb, ws_ref, bs_ref, cw_ref,
             dp_ref, gws_ref, gbs_ref, glg_ref, glb_ref, gcw_ref, gcb_ref,
             dbuf, carry, a_ws, a_bs, a_lg, a_lb, a_c0, a_c1, a_c2, a_cb):
        i = pl.program_id(1)

        @pl.when(i == 0)
        def _():
            carry[...] = jnp.zeros_like(carry)
            for a in (a_ws, a_bs, a_lg, a_lb, a_c0, a_c1, a_c2, a_cb):
                a[...] = jnp.zeros_like(a)

        dya_t = _dot_nt(da_ref[...], woa[...])
        dyb_t = _dot_nt(db_ref[...], wob[...])
        g = lng[...]
        beta = lnb[...]
        wm = jnp.where(_tril_mask(), ws_ref[0], 0.0).astype(BF16)
        bs = bs_ref[0]
        w0, w1, w2 = cw_ref[0:1, :], cw_ref[1:2, :], cw_ref[2:3, :]
        dbuf[tm:tm + SUBLANES, :] = carry[...]
        gws = a_ws[...]
        gbs = a_bs[...]
        lg, lb = a_lg[...], a_lb[...]
        c0, c1, c2, cb = a_c0[...], a_c1[...], a_c2[...], a_cb[...]
        for ck in reversed(range(tm // CHUNK)):
            r0 = ck * CHUNK
            rows = slice(r0, r0 + CHUNK)
            dya = dya_t[rows]
            u = sv[0, rows, :].astype(F32)
            v = sv[1, rows, :].astype(F32)
            z = sv[2, rows, :].astype(F32)
            gu, gu_grad = _gelu(u)
            gv, gv_grad = _gelu(v)
            mu = jnp.mean(gv, axis=-1, keepdims=True)
            dvc = gv - mu
            var = jnp.mean(dvc * dvc, axis=-1, keepdims=True)
            rstd = lax.rsqrt(var + LN_EPS)
            vhat = dvc * rstd
            vn = (vhat * g + beta).astype(BF16)
            sz, sz_grad = _silu(z)
            t1 = dya * sz
            dmixed = t1 * gu
            dmixed_b = dmixed.astype(BF16)
            mixed = _dot(wm, vn) + bs
            gws = gws + _dot_nt(dmixed_b, vn)
            gbs = gbs + dmixed
            dvn = _dot_tn(wm, dmixed_b)
            dp_ref[0, rows, :] = (t1 * mixed * gu_grad).astype(BF16)
            dp_ref[2, rows, :] = (dya * gu * mixed * sz_grad).astype(BF16)
            lg = lg + _fold8(dvn * vhat)
            lb = lb + _fold8(dvn)
            dvh = dvn * g
            m1 = jnp.mean(dvh, axis=-1, keepdims=True)
            m2 = jnp.mean(dvh * vhat, axis=-1, keepdims=True)
            dp_ref[1, rows, :] = (rstd * (dvh - m1 - vhat * m2) * gv_grad).astype(BF16)
            dyb = dyb_t[rows]
            xbv = sv[3, rows, :].astype(F32)
            cbv = sv[4, rows, :].astype(F32)
            bbv = sv[5, rows, :].astype(F32)
            zbv = sv[6, rows, :].astype(F32)
            conv = sv[7, rows, :].astype(F32)
            szb, szb_grad = _silu(zbv)
            dp_ref[5, rows, :] = (dyb * conv * szb).astype(BF16)
            dp_ref[6, rows, :] = (dyb * bbv * conv * szb_grad).astype(BF16)
            dconv = dyb * bbv * szb
            dbuf[r0:r0 + CHUNK, :] = dconv
            dc1 = dbuf[r0 + 1:r0 + 1 + CHUNK, :]
            dc2 = dbuf[r0 + 2:r0 + 2 + CHUNK, :]
            if ck == 0:
                carry[...] = dconv[0:SUBLANES, :]
            h = cbv * xbv
            c2 = c2 + _fold8(dconv * h)
            c1 = c1 + _fold8(dc1 * h)
            c0 = c0 + _fold8(dc2 * h)
            cb = cb + _fold8(dconv)
            dh = w2 * dconv + w1 * dc1 + w0 * dc2
            dp_ref[3, rows, :] = (dh * cbv).astype(BF16)
            dp_ref[4, rows, :] = (dh * xbv).astype(BF16)
            dp_ref[7, rows, :] = dgt[0, rows, :]
            dp_ref[8, rows, :] = dgt[1, rows, :]
        a_ws[...] = gws
        a_bs[...] = gbs
        a_lg[...], a_lb[...] = lg, lb
        a_c0[...], a_c1[...], a_c2[...], a_cb[...] = c0, c1, c2, cb

        @pl.when(i == nt - 1)
        def _():
            gws_ref[0] = jnp.where(_tril_mask(), a_ws[...], 0.0)
            gbs_ref[0] = jnp.sum(a_bs[...], axis=1, keepdims=True)
            glg_ref[...] = jnp.sum(a_lg[...], axis=0, keepdims=True)
            glb_ref[...] = jnp.sum(a_lb[...], axis=0, keepdims=True)
            gcw_ref[0:1, :] = jnp.sum(a_c0[...], axis=0, keepdims=True)
            gcw_ref[1:2, :] = jnp.sum(a_c1[...], axis=0, keepdims=True)
            gcw_ref[2:3, :] = jnp.sum(a_c2[...], axis=0, keepdims=True)
            gcb_ref[...] = jnp.sum(a_cb[...], axis=0, keepdims=True)

    def rev(i):
        return nt - 1 - i

    row = pl.BlockSpec((tm, d), lambda g, i: (rev(i), 0))
    wrow = pl.BlockSpec((hd, d), lambda g, i: (g, 0))
    vec = pl.BlockSpec((1, hd), lambda g, i: (0, g))
    acc8 = pltpu.VMEM((SUBLANES, hd), F32)
    return pl.pallas_call(
        body, name="mixer_bwd", grid=(HEADS, nt),
        in_specs=[row, row, wrow, wrow,
                  pl.BlockSpec((N_SAVE, tm, hd), lambda g, i: (0, rev(i), g)),
                  pl.BlockSpec((2, tm, hd), lambda g, i: (0, rev(i), g)),
                  vec, vec,
                  pl.BlockSpec((1, CHUNK, CHUNK), lambda g, i: (g, 0, 0)),
                  pl.BlockSpec((1, CHUNK, 1), lambda g, i: (g, 0, 0)),
                  pl.BlockSpec((3, hd), lambda g, i: (0, g))],
        out_specs=[pl.BlockSpec((N_IN, tm, hd), lambda g, i: (0, rev(i), g)),
                   pl.BlockSpec((1, CHUNK, CHUNK), lambda g, i: (g, 0, 0)),
                   pl.BlockSpec((1, CHUNK, 1), lambda g, i: (g, 0, 0)),
                   vec, vec,
                   pl.BlockSpec((3, hd), lambda g, i: (0, g)),
                   vec],
        out_shape=[jax.ShapeDtypeStruct((N_IN, t_len, d), BF16),
                   jax.ShapeDtypeStruct((HEADS, CHUNK, CHUNK), F32),
                   jax.ShapeDtypeStruct((HEADS, CHUNK, 1), F32),
                   jax.ShapeDtypeStruct((1, d), F32), jax.ShapeDtypeStruct((1, d), F32),
                   jax.ShapeDtypeStruct((3, d), F32), jax.ShapeDtypeStruct((1, d), F32)],
        scratch_shapes=[pltpu.VMEM((tm + SUBLANES, hd), F32), acc8,
                        pltpu.VMEM((CHUNK, CHUNK), F32), pltpu.VMEM((CHUNK, hd), F32),
                        acc8, acc8, acc8, acc8, acc8, acc8],
        compiler_params=_cparams(("parallel", "arbitrary")),
    )(da, db, w_oa, w_ob, saved, dgate, ln_v_g, ln_v_b, w_s, b_s3, conv_w)


def _grad_w_in(xb, dp):
    t_len, d = xb.shape
    tm, tn, tk = _tile(d, MM_TILE), _tile(d, MM_TILE), _tile(t_len, MM_DEPTH)
    nj = d // tn
    nk = t_len // tk

    def body(a_ref, b_ref, o_ref, acc):
        k = pl.program_id(2)

        @pl.when(k == 0)
        def _():
            acc[...] = jnp.zeros_like(acc)

        acc[...] += _dot_tn(a_ref[...], b_ref[...])

        @pl.when(k == nk - 1)
        def _():
            o_ref[...] = acc[...].astype(BF16)

    return pl.pallas_call(
        body, name="grad_w_in", grid=(d // tm, N_IN * nj, nk),
        in_specs=[pl.BlockSpec((tk, tm), lambda i, j, k: (k, i)),
                  pl.BlockSpec((None, tk, tn), lambda i, j, k: (j // nj, k, j % nj))],
        out_specs=pl.BlockSpec((tm, tn), lambda i, j, k: (i, j)),
        out_shape=jax.ShapeDtypeStruct((d, N_IN * d), BF16),
        scratch_shapes=[pltpu.VMEM((tm, tn), F32)],
        compiler_params=_cparams(("parallel", "parallel", "arbitrary")),
    )(xb, dp)


def _grad_x(dp, w_in, gx_direct, parts, packed):
    _, t_len, d = dp.shape
    tm, tn, tk = _tile(t_len, MM_TILE), _tile(d, MM_TILE), _tile(d, MM_DEPTH)
    nkb = d // tk
    nk = N_IN * nkb
    ni, nj = t_len // tm, d // tn
    n_parts = len(parts)

    def body(a_ref, b_ref, r_ref, *rest):
        srcs = rest[:n_parts]
        pk_ref = rest[n_parts]
        o_ref = rest[n_parts + 1]
        lands = rest[n_parts + 2:2 * n_parts + 2]
        gath = rest[2 * n_parts + 2]
        acc, send_sems, recv_sems, pk_send, pk_recv, pk_local = rest[2 * n_parts + 3:]
        i, j, k = pl.program_id(0), pl.program_id(1), pl.program_id(2)
        x, y, c = lax.axis_index("x"), lax.axis_index("y"), lax.axis_index("c")
        my_chip = 2 * x + y
        me = 4 * x + 2 * y + c
        chips = [(1 - x, y), (x, 1 - y), (1 - x, 1 - y)]

        def part_copy(t, n):
            px, py = chips[n]
            return pltpu.make_async_remote_copy(
                src_ref=srcs[t].at[2 * px + py], dst_ref=lands[t].at[my_chip],
                send_sem=send_sems.at[3 * t + n], recv_sem=recv_sems.at[3 * t + n],
                device_id=(px, py, c), device_id_type=MESH)

        def part_landing(t, n):
            px, py = chips[n]
            return pltpu.make_async_remote_copy(
                src_ref=srcs[t].at[my_chip], dst_ref=lands[t].at[2 * px + py],
                send_sem=send_sems.at[3 * t + n], recv_sem=recv_sems.at[3 * t + n],
                device_id=(px, py, c), device_id_type=MESH)

        def pk_copy(s):
            return pltpu.make_async_remote_copy(
                src_ref=pk_ref, dst_ref=gath.at[me], send_sem=pk_send, recv_sem=pk_recv.at[me],
                device_id=(s // 4, (s // 2) % 2, s % 2), device_id_type=MESH)

        @pl.when((i == 0) & (j == 0) & (k == 0))
        def _():
            for t in range(n_parts):
                for n in range(3):
                    part_copy(t, n).start()
            pltpu.make_async_copy(pk_ref, gath.at[me], pk_local).start()
            for s in range(N_DEV):
                @pl.when(s != me)
                def _(s=s):
                    pk_copy(s).start()

        @pl.when(k == 0)
        def _():
            acc[...] = r_ref[...]

        acc[...] += _dot(a_ref[...], b_ref[...])

        @pl.when(k == nk - 1)
        def _():
            o_ref[...] = acc[...]

        @pl.when((i == ni - 1) & (j == nj - 1) & (k == nk - 1))
        def _():
            for t in range(n_parts):
                for n in range(3):
                    part_landing(t, n).wait_recv()
            for t in range(n_parts):
                for n in range(3):
                    part_copy(t, n).wait_send()
            for s in range(N_DEV):
                @pl.when(s != me)
                def _(s=s):
                    pltpu.make_async_remote_copy(
                        src_ref=pk_ref, dst_ref=gath.at[s], send_sem=pk_send, recv_sem=pk_recv.at[s],
                        device_id=(s // 4, (s // 2) % 2, s % 2), device_id_type=MESH).wait_recv()
            seven = gath.at[pl.ds(0, N_DEV - 1)]
            pltpu.make_async_remote_copy(src_ref=seven, dst_ref=seven, send_sem=pk_send, recv_sem=pk_send,
                                         device_id=(x, y, 1 - c), device_id_type=MESH).wait_send()
            pltpu.make_async_copy(pk_ref, gath.at[me], pk_local).wait()

    hbm = pl.BlockSpec(memory_space=pl.ANY)
    outs = pl.pallas_call(
        body, name="grad_x", grid=(ni, nj, nk),
        in_specs=[pl.BlockSpec((None, tm, tk), lambda i, j, k: (k // nkb, i, k % nkb)),
                  pl.BlockSpec((tk, tn), lambda i, j, k: (k, j)),
                  pl.BlockSpec((tm, tn), lambda i, j, k: (i, j))] + [hbm] * (n_parts + 1),
        out_specs=[pl.BlockSpec((tm, tn), lambda i, j, k: (i, j))] + [hbm] * (n_parts + 1),
        out_shape=[jax.ShapeDtypeStruct((t_len, d), F32)] + [jax.ShapeDtypeStruct(p.shape, p.dtype) for p in parts]
        + [jax.ShapeDtypeStruct((N_DEV,) + packed.shape, packed.dtype)],
        scratch_shapes=[pltpu.VMEM((tm, tn), F32),
                        pltpu.SemaphoreType.DMA((3 * n_parts,)), pltpu.SemaphoreType.DMA((3 * n_parts,)),
                        pltpu.SemaphoreType.DMA(()), pltpu.SemaphoreType.DMA((N_DEV,)), pltpu.SemaphoreType.DMA(())],
        compiler_params=_cparams(("arbitrary", "arbitrary", "arbitrary")),
    )(dp, w_in, gx_direct, *parts, packed)
    return outs[0], list(outs[1:1 + n_parts]), outs[1 + n_parts]


def kernel(x, w_in, b_gate, ln_v_g, ln_v_b, w_s, b_s, conv_w, conv_b, w_oa, w_ob, w_out, ln_g, ln_b, loss_target, m_w_in, m_b_gate, m_ln_v_g, m_ln_v_b, m_w_s, m_b_s, m_conv_w, m_conv_b, m_w_oa, m_w_ob, m_w_out, m_ln_g, m_ln_b, v_w_in, v_b_gate, v_ln_v_g, v_ln_v_b, v_w_s, v_b_s, v_conv_w, v_conv_b, v_w_oa, v_w_ob, v_w_out, v_ln_g, v_ln_b):
    _, t_len, d = x.shape
    assert d % (HEADS * 128) == 0 and t_len % CHUNK == 0 and w_in.shape[2] * N_DEV == N_IN * d
    x2 = x[0]
    tgt2 = loss_target[0]
    c_arr = lax.axis_index("c").astype(jnp.int32).reshape(1)
    chip_arr = (2 * lax.axis_index("x") + lax.axis_index("y")).astype(jnp.int32).reshape(1)
    dev = 4 * lax.axis_index("x") + 2 * lax.axis_index("y") + lax.axis_index("c")

    xb = _cast_bf16(x2, "cast_x")
    shards = [_cast_transposed(w_in[0], "cast_w_in"), _cast_bf16(w_oa[0], "cast_w_oa"),
              _cast_bf16(w_ob[0], "cast_w_ob"), _cast_bf16(w_out[0], "cast_w_out")]
    (conv_w_g,) = _all_gather([conv_w[0]], ["lead"], "gather_conv_w", vmem=True)
    conv_w_f = jnp.transpose(conv_w_g, (1, 0, 2)).reshape(3, d)
    w_s3 = w_s[0]
    b_s3 = b_s[0].reshape(HEADS, CHUNK, 1)

    saved, ya, yb, w_in_f, w_oa_f, w_ob_f, w_out_f = _mixer_fwd(
        xb, shards, ln_v_g, ln_v_b, w_s3, b_s3, conv_w_f, conv_b)
    sa, sb, pa, pb, merged = _merge_fwd(xb, ya, yb, w_in_f, w_oa_f, w_ob_f, b_gate)
    dz, gx_direct, g_ln_g, g_ln_b, err2 = _out_ln_loss(merged, w_out_f, x2, tgt2, ln_g, ln_b)
    loss = lax.psum(0.5 * jnp.sum(err2) / d, ("x", "y", "c"))

    da, db, dgate, g_bga, g_bgb = _merge_bwd(dz, w_out_f, sa, sb, pa, pb)
    gw_out = _grad_w(merged, dz, "grad_w_out")
    gw_oa = _grad_w(ya, da, "grad_w_oa")
    gw_ob = _grad_w(yb, db, "grad_w_ob")
    dp, g_ws, g_bs, g_lvg, g_lvb, g_cw, g_cb = _mixer_bwd(
        da, db, w_oa_f, w_ob_f, saved, dgate, ln_v_g, ln_v_b, w_s3, b_s3, conv_w_f)
    gw_in = _grad_w_in(xb, dp)

    grads = [gw_in, gw_oa, gw_ob, gw_out]
    kinds = ["col", "row", "row", "row"]
    lands = _pair_exchange(grads, kinds, "grad_pair_exchange")
    parts = [_pair_sum(g, l, k, c_arr, "grad_pair_sum_%d" % n) for n, (g, l, k) in enumerate(zip(grads, lands, kinds))]
    pieces = [jnp.concatenate([g_bga, g_bgb], axis=1), g_lvg, g_lvb, g_ws, g_bs, g_cw, g_cb, g_ln_g, g_ln_b]
    sizes = [p.size for p in pieces]
    packed = jnp.concatenate([p.reshape(-1, 128) for p in pieces], axis=0)
    grad_x, lands2, gathered = _grad_x(dp, w_in_f, gx_direct, parts, packed)
    grad_x = grad_x[None]

    big = []
    for n, (w, m, v) in enumerate([(w_in, m_w_in, v_w_in), (w_oa, m_w_oa, v_w_oa), (w_ob, m_w_ob, v_w_ob),
                                   (w_out, m_w_out, v_w_out)]):
        big.append([o[None] for o in _sum_adam(parts[n], lands2[n], w[0], m[0], v[0], chip_arr, "sum_adam_%d" % n)])
    (g_w_in, d_w_in, nm_w_in, nv_w_in), (g_w_oa, d_w_oa, nm_w_oa, nv_w_oa), \
        (g_w_ob, d_w_ob, nm_w_ob, nv_w_ob), (g_w_out, d_w_out, nm_w_out, nv_w_out) = big

    total = _small_sum(gathered, "sum_small_grads")
    offs = [0]
    for s in sizes:
        offs.append(offs[-1] + s // 128)
    unpacked = [total[offs[n]:offs[n + 1]] for n in range(len(pieces))]
    g_b_gate = unpacked[0].reshape(b_gate.shape)
    g_ln_v_g = unpacked[1].reshape(ln_v_g.shape)
    g_ln_v_b = unpacked[2].reshape(ln_v_b.shape)
    g_w_s = unpacked[3].reshape(w_s.shape)
    g_b_s = unpacked[4].reshape(b_s.shape)
    g_conv_w = lax.dynamic_slice_in_dim(unpacked[5].reshape(3, d), dev * (d // N_DEV), d // N_DEV, axis=1)[None]
    g_conv_b = unpacked[6].reshape(conv_b.shape)
    g_ln_g2 = unpacked[7].reshape(ln_g.shape)
    g_ln_b2 = unpacked[8].reshape(ln_b.shape)

    small_w = [b_gate, ln_v_g, ln_v_b, w_s, b_s, conv_w, conv_b, ln_g, ln_b]
    small_g = [g_b_gate, g_ln_v_g, g_ln_v_b, g_w_s, g_b_s, g_conv_w, g_conv_b, g_ln_g2, g_ln_b2]
    small_m = [m_b_gate, m_ln_v_g, m_ln_v_b, m_w_s, m_b_s, m_conv_w, m_conv_b, m_ln_g, m_ln_b]
    small_v = [v_b_gate, v_ln_v_g, v_ln_v_b, v_w_s, v_b_s, v_conv_w, v_conv_b, v_ln_g, v_ln_b]

    def flat(a):
        return a.reshape(-1, a.shape[-1])

    res = _small_adam([flat(a) for a in small_w], [flat(a) for a in small_g], [flat(a) for a in small_m],
                      [flat(a) for a in small_v], "adam_small")
    ns = len(small_w)
    d_s = [res[n].reshape(small_w[n].shape) for n in range(ns)]
    nm_s = [res[ns + n].reshape(small_w[n].shape) for n in range(ns)]
    nv_s = [res[2 * ns + n].reshape(small_w[n].shape) for n in range(ns)]

    def ordered(first, small, oa, ob, out):
        return [first] + small[:7] + [oa, ob, out] + small[7:]

    return (loss, grad_x,
            *ordered(g_w_in, small_g, g_w_oa, g_w_ob, g_w_out),
            *ordered(d_w_in, d_s, d_w_oa, d_w_ob, d_w_out),
            *ordered(nm_w_in, nm_s, nm_w_oa, nm_w_ob, nm_w_out),
            *ordered(nv_w_in, nv_s, nv_w_oa, nv_w_ob, nv_w_out))
```

```python
import functools

import jax
import jax.numpy as jnp
from jax import lax
from jax.experimental import pallas as pl
from jax.experimental.pallas import tpu as pltpu

F32 = jnp.float32
BF16 = jnp.bfloat16
MESH = pl.DeviceIdType.MESH

N_DEV = 8
N_CHIP = 4
HEADS = 8
CHUNK = 128
N_IN = 9
N_MIX = 7
N_SAVE = 8
LN_EPS = 1e-5
DN_ALPHA = 2.0 ** 0.25
ADAM_LR = 0.001
ADAM_B1 = 0.9
ADAM_B2 = 0.999
ADAM_EPS = 1e-08
ADAM_WD = 0.01
ADAM_STEP = 10
GELU_C0 = 0.7978845608028654
GELU_C1 = 0.044715
SUBLANES = 8
FIRST_SENDS = 4
SENDS_PER_GROUP = 2
MM_TILE = 1024
MM_DEPTH = 2048
VMEM_LIMIT = 56 << 20


def _cparams(sem):
    return pltpu.CompilerParams(dimension_semantics=sem, vmem_limit_bytes=VMEM_LIMIT)


def _tile(n, want):
    t = min(n, want)
    while n % t:
        t //= 2
    return t


def _gelu(u):
    u2 = u * u
    t = jnp.tanh(u * (GELU_C0 + (GELU_C0 * GELU_C1) * u2))
    hp = 0.5 * t + 0.5
    grad = hp + (0.5 * u) * (1.0 - t * t) * (GELU_C0 + (3.0 * GELU_C0 * GELU_C1) * u2)
    return u * hp, grad


def _silu(z):
    s = jax.nn.sigmoid(z)
    sil = z * s
    return sil, s + sil * (1.0 - s)


def _fold8(a):
    return jnp.sum(a.reshape(a.shape[0] // SUBLANES, SUBLANES, a.shape[1]), axis=0)


def _dot(a, b):
    return jnp.dot(a, b, preferred_element_type=F32)


def _dot_nt(a, b):
    return lax.dot_general(a, b, (((1,), (1,)), ((), ())), preferred_element_type=F32)


def _dot_tn(a, b):
    return lax.dot_general(a, b, (((0,), (0,)), ((), ())), preferred_element_type=F32)


def _tril_mask():
    r = lax.broadcasted_iota(jnp.int32, (CHUNK, CHUNK), 0)
    c = lax.broadcasted_iota(jnp.int32, (CHUNK, CHUNK), 1)
    return c <= r


def _cast_bf16(a, name):
    rows, cols = a.shape
    rb = _tile(rows, 256)

    def body(a_ref, o_ref):
        o_ref[...] = a_ref[...].astype(BF16)

    return pl.pallas_call(
        body, name=name, grid=(rows // rb,),
        in_specs=[pl.BlockSpec((rb, cols), lambda i: (i, 0))],
        out_specs=pl.BlockSpec((rb, cols), lambda i: (i, 0)),
        out_shape=jax.ShapeDtypeStruct((rows, cols), BF16),
        compiler_params=_cparams(("parallel",)),
    )(a)


def _cast_transposed(a, name):
    rows, cols = a.shape
    tb = _tile(rows, 2048)
    tc = _tile(cols, 256)

    def body(a_ref, o_ref):
        o_ref[...] = a_ref[...].T.astype(BF16)

    return pl.pallas_call(
        body, name=name, grid=(cols // tc, rows // tb),
        in_specs=[pl.BlockSpec((tb, tc), lambda j, i: (i, j))],
        out_specs=pl.BlockSpec((tc, tb), lambda j, i: (j, i)),
        out_shape=jax.ShapeDtypeStruct((cols, rows), BF16),
        compiler_params=_cparams(("parallel", "parallel")),
    )(a)


def _shard_ref(full, kind, s, n):
    if kind == "col":
        return full.at[:, pl.ds(pl.multiple_of(s * n, 128), n)]
    if kind == "row":
        return full.at[pl.ds(pl.multiple_of(s * n, SUBLANES), n), :]
    return full.at[s]


def _all_gather(shards, kinds, name, vmem):
    nt = len(shards)
    out_shapes = []
    for a, kind in zip(shards, kinds):
        if kind == "col":
            out_shapes.append(jax.ShapeDtypeStruct((a.shape[0], N_DEV * a.shape[1]), a.dtype))
        elif kind == "row":
            out_shapes.append(jax.ShapeDtypeStruct((N_DEV * a.shape[0], a.shape[1]), a.dtype))
        else:
            out_shapes.append(jax.ShapeDtypeStruct((N_DEV,) + a.shape, a.dtype))

    def body(*refs):
        srcs, fulls = refs[:nt], refs[nt:2 * nt]
        send_sems, recv_sems, local_sems = refs[2 * nt:]
        x, y, c = lax.axis_index("x"), lax.axis_index("y"), lax.axis_index("c")
        sibling = (x, y, 1 - c)
        chips = [(1 - x, y), (x, 1 - y), (1 - x, 1 - y)]

        def dev(px, py, pc):
            return 4 * px + 2 * py + pc

        def region(t, s):
            a, kind = shards[t], kinds[t]
            n = a.shape[1] if kind == "col" else a.shape[0]
            return _shard_ref(fulls[t], kind, s, n)

        def copy(t, k, block, to, own=False):
            return pltpu.make_async_remote_copy(
                src_ref=srcs[t] if own else region(t, block), dst_ref=region(t, block),
                send_sem=send_sems.at[7 * t + k], recv_sem=recv_sems.at[7 * t + k],
                device_id=to, device_id_type=MESH)

        me = dev(x, y, c)
        started = []
        for t in range(nt):
            mine = pltpu.make_async_copy(srcs[t], region(t, me), local_sems.at[t])
            mine.start()
            started.append(mine)
        first = []
        for t in range(nt):
            first.append(copy(t, 0, me, sibling, own=True))
            for j, chip in enumerate(chips):
                first.append(copy(t, 1 + j, me, (*chip, c), own=True))
        for cp in first:
            cp.start()
        passed = []
        for t in range(nt):
            for j, chip in enumerate(chips):
                blk = dev(*chip, c)
                copy(t, 1 + j, blk, sibling).wait_recv()
                fwd = copy(t, 4 + j, blk, sibling)
                fwd.start()
                passed.append(fwd)
        for t in range(nt):
            copy(t, 0, dev(x, y, 1 - c), sibling).wait_recv()
            for j, chip in enumerate(chips):
                copy(t, 4 + j, dev(*chip, 1 - c), sibling).wait_recv()
        for cp in first + passed:
            cp.wait_send()
        for mine in started:
            mine.wait()

    space = pltpu.VMEM if vmem else pl.ANY
    return pl.pallas_call(
        body, name=name,
        in_specs=[pl.BlockSpec(memory_space=space)] * nt,
        out_specs=[pl.BlockSpec(memory_space=space)] * nt,
        out_shape=out_shapes,
        scratch_shapes=[pltpu.SemaphoreType.DMA((7 * nt,)), pltpu.SemaphoreType.DMA((7 * nt,)),
                        pltpu.SemaphoreType.DMA((nt,))],
        compiler_params=pltpu.CompilerParams(vmem_limit_bytes=VMEM_LIMIT, has_side_effects=True),
    )(*shards)


def _pair_exchange(grads, kinds, name):
    nt = len(grads)
    shard_shapes = []
    for g, kind in zip(grads, kinds):
        shard_shapes.append((g.shape[0], g.shape[1] // N_DEV) if kind == "col" else (g.shape[0] // N_DEV, g.shape[1]))

    def body(*refs):
        srcs, lands = refs[:nt], refs[nt:2 * nt]
        send_sems, recv_sems = refs[2 * nt:]
        x, y, c = lax.axis_index("x"), lax.axis_index("y"), lax.axis_index("c")
        copies = []
        for t in range(nt):
            n = shard_shapes[t][1] if kinds[t] == "col" else shard_shapes[t][0]
            for k in range(N_CHIP):
                cp = pltpu.make_async_remote_copy(
                    src_ref=_shard_ref(srcs[t], kinds[t], 2 * k + 1 - c, n), dst_ref=lands[t].at[k],
                    send_sem=send_sems.at[N_CHIP * t + k], recv_sem=recv_sems.at[N_CHIP * t + k],
                    device_id=(x, y, 1 - c), device_id_type=MESH)
                cp.start()
                copies.append(cp)
        for cp in copies:
            cp.wait()

    return pl.pallas_call(
        body, name=name,
        in_specs=[pl.BlockSpec(memory_space=pl.ANY)] * nt,
        out_specs=[pl.BlockSpec(memory_space=pl.ANY)] * nt,
        out_shape=[jax.ShapeDtypeStruct((N_CHIP,) + s, g.dtype) for s, g in zip(shard_shapes, grads)],
        scratch_shapes=[pltpu.SemaphoreType.DMA((N_CHIP * nt,)), pltpu.SemaphoreType.DMA((N_CHIP * nt,))],
        compiler_params=pltpu.CompilerParams(has_side_effects=True),
    )(*grads)


def _pair_sum(grad, land, kind, c_arr, name):
    _, r, w = land.shape
    rb = _tile(r, 256)
    nrb = r // rb

    def body(c_ref, g_ref, l_ref, o_ref):
        o_ref[...] = (g_ref[...].astype(F32) + l_ref[...].astype(F32)).astype(o_ref.dtype)

    if kind == "col":
        g_spec = pl.BlockSpec((rb, w), lambda k, i, c: (i, 2 * k + c[0]))
    else:
        g_spec = pl.BlockSpec((rb, w), lambda k, i, c: ((2 * k + c[0]) * nrb + i, 0))
    return pl.pallas_call(
        body, name=name,
        grid_spec=pltpu.PrefetchScalarGridSpec(
            num_scalar_prefetch=1, grid=(N_CHIP, nrb),
            in_specs=[g_spec, pl.BlockSpec((None, rb, w), lambda k, i, c: (k, i, 0))],
            out_specs=pl.BlockSpec((None, rb, w), lambda k, i, c: (k, i, 0))),
        out_shape=jax.ShapeDtypeStruct(land.shape, BF16),
        compiler_params=_cparams(("parallel", "parallel")),
    )(c_arr, grad, land)


def _adam(w, g, m, v):
    m = ADAM_B1 * m + (1.0 - ADAM_B1) * g
    v = ADAM_B2 * v + (1.0 - ADAM_B2) * jnp.square(g)
    m_hat = m / (1.0 - ADAM_B1 ** ADAM_STEP)
    v_hat = v / (1.0 - ADAM_B2 ** ADAM_STEP)
    delta = -ADAM_LR * (m_hat / (jnp.sqrt(v_hat) + ADAM_EPS) + ADAM_WD * w)
    return delta, m, v


def _sum_adam(part, land, w, m, v, chip_arr, name):
    r, wd = w.shape
    rb = _tile(r, 128)

    def body(k_ref, own, r1, r2, r3, w_ref, m_ref, v_ref, g_out, d_out, m_out, v_out):
        g = own[...].astype(F32) + r1[...].astype(F32) + r2[...].astype(F32) + r3[...].astype(F32)
        d, mn, vn = _adam(w_ref[...], g, m_ref[...], v_ref[...])
        g_out[...] = g
        d_out[...] = d
        m_out[...] = mn
        v_out[...] = vn

    def slot(off):
        return pl.BlockSpec((None, rb, wd), lambda i, k: ((k[0] + off) % N_CHIP, i, 0))

    plain = pl.BlockSpec((rb, wd), lambda i, k: (i, 0))
    return pl.pallas_call(
        body, name=name,
        grid_spec=pltpu.PrefetchScalarGridSpec(
            num_scalar_prefetch=1, grid=(r // rb,),
            in_specs=[slot(0), slot(1), slot(2), slot(3), plain, plain, plain],
            out_specs=[plain] * 4),
        out_shape=[jax.ShapeDtypeStruct(w.shape, F32)] * 4,
        compiler_params=_cparams(("parallel",)),
    )(chip_arr, part, land, land, land, w, m, v)


def _small_sum(gathered, name):
    _, r, w = gathered.shape

    def body(g_ref, o_ref):
        acc = g_ref[0]
        for d in range(1, N_DEV):
            acc = acc + g_ref[d]
        o_ref[...] = acc

    return pl.pallas_call(body, name=name, out_shape=jax.ShapeDtypeStruct((r, w), F32))(gathered)


def _small_adam(ws, gs, ms, vs, name):
    n = len(ws)

    def body(*refs):
        ins, outs = refs[:4 * n], refs[4 * n:]
        for t in range(n):
            d, mn, vn = _adam(ins[t][...], ins[n + t][...], ins[2 * n + t][...], ins[3 * n + t][...])
            outs[t][...] = d
            outs[n + t][...] = mn
            outs[2 * n + t][...] = vn

    shapes = [jax.ShapeDtypeStruct(w.shape, F32) for w in ws]
    return pl.pallas_call(body, name=name, out_shape=shapes * 3)(*ws, *gs, *ms, *vs)


def _mixer_fwd(xb, shards, ln_v_g, ln_v_b, w_s, b_s3, conv_w, conv_b):
    t_len, d = xb.shape
    hd = d // HEADS
    tm = _tile(t_len, 512)
    nt = t_len // tm
    assert nt >= 2
    w8 = shards[0].shape[0]
    bps = w8 // hd
    r8 = shards[1].shape[0]
    nq = N_IN * HEADS
    n_blocks = nq + 3 * N_DEV
    groups = [[("in", HEADS * b + k) for b in range(N_MIX)] for k in range(HEADS)]
    groups.append([("in", q) for q in range(N_MIX * HEADS, nq)] + [(t, s) for t in (1, 2, 3) for s in range(N_DEV)])

    def owner_of(blk):
        return blk[1] // bps if blk[0] == "in" else blk[1]

    send_step = {}
    for s in range(N_DEV):
        mine = [blk for grp in groups for blk in grp if owner_of(blk) == s]
        for pos, blk in enumerate(mine):
            send_step[blk] = 0 if pos < FIRST_SENDS else 1 + (pos - FIRST_SENDS) // SENDS_PER_GROUP
    assert max(send_step.values()) < HEADS
    tail_pass_step = nt - 6 if nt >= 8 else nt - 2

    def body(x_ref, sh_in, sh_oa, sh_ob, sh_out, lng, lnb, ws_ref, bs_ref, cw_ref, cb_ref,
             save_ref, ya_ref, yb_ref, f_in, f_oa, f_ob, f_out,
             hbuf, wbuf, recv_sems, own_sems, fwd_sems, local_sems, load_sems):
        g = pl.program_id(0)
        i = pl.program_id(1)
        x, y, c = lax.axis_index("x"), lax.axis_index("y"), lax.axis_index("c")
        sibling = (x, y, 1 - c)
        chips = [(1 - x, y), (x, 1 - y), (1 - x, 1 - y)]
        shard_refs = (sh_in, sh_oa, sh_ob, sh_out)
        fulls = (f_in, f_oa, f_ob, f_out)

        def tensor(blk):
            return 0 if blk[0] == "in" else blk[0]

        def owner(blk):
            s = owner_of(blk)
            return s // 4, (s // 2) % 2, s % 2

        def bid(blk):
            return blk[1] if blk[0] == "in" else nq + (blk[0] - 1) * N_DEV + blk[1]

        def region(blk):
            if blk[0] == "in":
                return f_in.at[pl.ds(blk[1] * hd, hd), :]
            return fulls[blk[0]].at[pl.ds(blk[1] * r8, r8), :]

        def own_src(blk):
            if blk[0] == "in":
                return sh_in.at[pl.ds((blk[1] % bps) * hd, hd), :]
            return shard_refs[blk[0]]

        def rcopy(blk, to, send_sem, own):
            return pltpu.make_async_remote_copy(
                src_ref=own_src(blk) if own else region(blk), dst_ref=region(blk),
                send_sem=send_sem, recv_sem=recv_sems.at[bid(blk)], device_id=to, device_id_type=MESH)

        def send_own(blk):
            ox, oy, oc = owner(blk)

            @pl.when((x == ox) & (y == oy) & (c == oc))
            def _():
                rcopy(blk, sibling, own_sems.at[tensor(blk)], True).start()
                for chip in chips:
                    rcopy(blk, (*chip, c), own_sems.at[tensor(blk)], True).start()

        def pass_on(blk):
            ox, oy, oc = owner(blk)

            @pl.when(((x != ox) | (y != oy)) & (c == oc))
            def _():
                rcopy(blk, sibling, fwd_sems.at[tensor(blk)], False).wait_recv()
                rcopy(blk, sibling, fwd_sems.at[tensor(blk)], False).start()

        def wait_from_sibling(blk):
            @pl.when(c != owner(blk)[2])
            def _():
                rcopy(blk, sibling, fwd_sems.at[tensor(blk)], False).wait_recv()

        def local_copy(t):
            me = 4 * x + 2 * y + c
            return pltpu.make_async_copy(shard_refs[t], _shard_ref(fulls[t], "row", me, w8 if t == 0 else r8),
                                         local_sems.at[t])

        first = (g == 0) & (i == 0)

        @pl.when(first)
        def _():
            for t in range(4):
                local_copy(t).start()
            for grp in groups:
                for blk in grp:
                    if send_step[blk] == 0:
                        send_own(blk)
            for blk in groups[0]:
                pass_on(blk)

        @pl.when(i == nt - 2)
        def _():
            for k in range(HEADS - 1):
                @pl.when(g == k)
                def _(k=k):
                    for blk in groups[k + 1]:
                        pass_on(blk)

        @pl.when((g == HEADS - 1) & (i == tail_pass_step))
        def _():
            for blk in groups[HEADS]:
                pass_on(blk)

        @pl.when(i == 0)
        def _():
            for k in range(HEADS):
                @pl.when(g == k)
                def _(k=k):
                    if k > 0:
                        for grp in groups:
                            for blk in grp:
                                if send_step[blk] == k:
                                    send_own(blk)
                    for blk in groups[k]:
                        wait_from_sibling(blk)
                    for b, blk in enumerate(groups[k]):
                        ox, oy, oc = owner(blk)
                        mine = (x == ox) & (y == oy) & (c == oc)

                        @pl.when(mine)
                        def _(b=b, blk=blk):
                            pltpu.make_async_copy(own_src(blk), wbuf.at[b], load_sems.at[b]).start()

                        @pl.when(jnp.logical_not(mine))
                        def _(b=b, blk=blk):
                            pltpu.make_async_copy(region(blk), wbuf.at[b], load_sems.at[b]).start()

                    for b, blk in enumerate(groups[k]):
                        pltpu.make_async_copy(region(blk), wbuf.at[b], load_sems.at[b]).wait()

        xt = x_ref[...]
        u = _dot_nt(xt, wbuf[0])
        v = _dot_nt(xt, wbuf[1])
        z = _dot_nt(xt, wbuf[2])
        save_ref[0] = u.astype(BF16)
        save_ref[1] = v.astype(BF16)
        save_ref[2] = z.astype(BF16)
        gu, _ = _gelu(u)
        gv, _ = _gelu(v)
        mu = jnp.mean(gv, axis=-1, keepdims=True)
        dv = gv - mu
        var = jnp.mean(dv * dv, axis=-1, keepdims=True)
        vn = (dv * lax.rsqrt(var + LN_EPS) * lng[...] + lnb[...]).astype(BF16)
        sz, _ = _silu(z)
        gate = gu * sz
        wm = jnp.where(_tril_mask(), ws_ref[0], 0.0).astype(BF16)
        bs = bs_ref[0]
        for ck in range(tm // CHUNK):
            rows = slice(ck * CHUNK, (ck + 1) * CHUNK)
            mixed = _dot(wm, vn[rows]) + bs
            ya_ref[rows, :] = (gate[rows] * mixed).astype(BF16)

        xbv = _dot_nt(xt, wbuf[3])
        cbv = _dot_nt(xt, wbuf[4])
        bbv = _dot_nt(xt, wbuf[5])
        zbv = _dot_nt(xt, wbuf[6])
        save_ref[3] = xbv.astype(BF16)
        save_ref[4] = cbv.astype(BF16)
        save_ref[5] = bbv.astype(BF16)
        save_ref[6] = zbv.astype(BF16)
        h = cbv * xbv

        @pl.when(i == 0)
        def _():
            hbuf[0:SUBLANES, :] = jnp.zeros((SUBLANES, hd), F32)

        hbuf[SUBLANES:SUBLANES + tm, :] = h
        h1 = hbuf[SUBLANES - 1:SUBLANES - 1 + tm, :]
        h2 = hbuf[SUBLANES - 2:SUBLANES - 2 + tm, :]
        conv = cb_ref[...] + cw_ref[0:1, :] * h2 + cw_ref[1:2, :] * h1 + cw_ref[2:3, :] * h
        hbuf[0:SUBLANES, :] = h[tm - SUBLANES:tm, :]
        save_ref[7] = conv.astype(BF16)
        szb, _ = _silu(zbv)
        yb_ref[...] = (bbv * conv * szb).astype(BF16)

        @pl.when((g == HEADS - 1) & (i == nt - 1))
        def _():
            for blk in groups[HEADS]:
                wait_from_sibling(blk)
            for t in range(4):
                local_copy(t).wait()
            for t in range(4):
                n = w8 if t == 0 else r8
                own_all, fwd_all = fulls[t].at[pl.ds(0, 4 * n), :], fulls[t].at[pl.ds(0, 3 * n), :]
                for ref, sem in ((own_all, own_sems.at[t]), (fwd_all, fwd_sems.at[t])):
                    pltpu.make_async_remote_copy(src_ref=ref, dst_ref=ref, send_sem=sem, recv_sem=sem,
                                                 device_id=sibling, device_id_type=MESH).wait_send()

    vec = pl.BlockSpec((1, hd), lambda g, i: (0, g))
    hbm = pl.BlockSpec(memory_space=pl.ANY)
    return pl.pallas_call(
        body, name="mixer_fwd", grid=(HEADS, nt),
        in_specs=[pl.BlockSpec((tm, d), lambda g, i: (i, 0)), hbm, hbm, hbm, hbm,
                  vec, vec,
                  pl.BlockSpec((1, CHUNK, CHUNK), lambda g, i: (g, 0, 0)),
                  pl.BlockSpec((1, CHUNK, 1), lambda g, i: (g, 0, 0)),
                  pl.BlockSpec((3, hd), lambda g, i: (0, g)),
                  vec],
        out_specs=[pl.BlockSpec((N_SAVE, tm, hd), lambda g, i: (0, i, g)),
                   pl.BlockSpec((tm, hd), lambda g, i: (i, g)),
                   pl.BlockSpec((tm, hd), lambda g, i: (i, g)),
                   hbm, hbm, hbm, hbm],
        out_shape=[jax.ShapeDtypeStruct((N_SAVE, t_len, d), BF16),
                   jax.ShapeDtypeStruct((t_len, d), BF16),
                   jax.ShapeDtypeStruct((t_len, d), BF16),
                   jax.ShapeDtypeStruct((N_DEV * w8, d), BF16),
                   jax.ShapeDtypeStruct((d, d), BF16), jax.ShapeDtypeStruct((d, d), BF16),
                   jax.ShapeDtypeStruct((d, d), BF16)],
        scratch_shapes=[pltpu.VMEM((SUBLANES + tm, hd), F32), pltpu.VMEM((N_MIX, hd, d), BF16),
                        pltpu.SemaphoreType.DMA((n_blocks,)), pltpu.SemaphoreType.DMA((4,)),
                        pltpu.SemaphoreType.DMA((4,)), pltpu.SemaphoreType.DMA((4,)),
                        pltpu.SemaphoreType.DMA((N_MIX,))],
        compiler_params=_cparams(("arbitrary", "arbitrary")),
    )(xb, *shards, ln_v_g, ln_v_b, w_s, b_s3, conv_w, conv_b)


def _merge_fwd(xb, ya, yb, w_in, w_oa, w_ob, b_gate):
    t_len, d = xb.shape
    tm = _tile(t_len, 512)
    tn = _tile(d, 512)
    nj = d // tn

    def body(x_ref, ya_ref, yb_ref, wga, wgb, woa, wob, bga, bgb, sa_ref, sb_ref, pa_ref, pb_ref, mg_ref):
        xt = x_ref[...]
        sa = jax.nn.sigmoid(_dot_nt(xt, wga[...]) + bga[...])
        sb = jax.nn.sigmoid(_dot_nt(xt, wgb[...]) + bgb[...])
        pa = _dot(ya_ref[...], woa[...])
        pb = _dot(yb_ref[...], wob[...])
        sa_ref[...] = sa.astype(BF16)
        sb_ref[...] = sb.astype(BF16)
        pa_ref[...] = pa.astype(BF16)
        pb_ref[...] = pb.astype(BF16)
        mg_ref[...] = (sa * pa + sb * pb).astype(BF16)

    row = pl.BlockSpec((tm, d), lambda j, i: (i, 0))
    out = pl.BlockSpec((tm, tn), lambda j, i: (i, j))
    return pl.pallas_call(
        body, name="merge_fwd", grid=(nj, t_len // tm),
        in_specs=[row, row, row,
                  pl.BlockSpec((tn, d), lambda j, i: (7 * nj + j, 0)),
                  pl.BlockSpec((tn, d), lambda j, i: (8 * nj + j, 0)),
                  pl.BlockSpec((d, tn), lambda j, i: (0, j)),
                  pl.BlockSpec((d, tn), lambda j, i: (0, j)),
                  pl.BlockSpec((1, tn), lambda j, i: (0, j)),
                  pl.BlockSpec((1, tn), lambda j, i: (0, nj + j))],
        out_specs=[out] * 5,
        out_shape=[jax.ShapeDtypeStruct((t_len, d), BF16)] * 5,
        compiler_params=_cparams(("parallel", "arbitrary")),
    )(xb, ya, yb, w_in, w_in, w_oa, w_ob, b_gate, b_gate)


def _out_ln_loss(merged, w_out, x, target, ln_g, ln_b):
    t_len, d = x.shape
    tm = _tile(t_len, 256)
    nt = t_len // tm

    def body(mg_ref, w_ref, x_ref, t_ref, g_ref, b_ref, dz_ref, gx_ref, glg_ref, glb_ref, ls_ref, a_g, a_b, a_l):
        i = pl.program_id(0)

        @pl.when(i == 0)
        def _():
            a_g[...] = jnp.zeros_like(a_g)
            a_b[...] = jnp.zeros_like(a_b)
            a_l[...] = jnp.zeros_like(a_l)

        zres = DN_ALPHA * x_ref[...] + _dot(mg_ref[...], w_ref[...])
        mu = jnp.mean(zres, axis=-1, keepdims=True)
        dc = zres - mu
        var = jnp.mean(dc * dc, axis=-1, keepdims=True)
        rstd = lax.rsqrt(var + LN_EPS)
        xhat = dc * rstd
        g = g_ref[...]
        err = xhat * g + b_ref[...] - t_ref[...]
        dy = err * (1.0 / d)
        a_l[...] += _fold8(err * err)
        a_g[...] += _fold8(dy * xhat)
        a_b[...] += _fold8(dy)
        dxh = dy * g
        m1 = jnp.mean(dxh, axis=-1, keepdims=True)
        m2 = jnp.mean(dxh * xhat, axis=-1, keepdims=True)
        dz = rstd * (dxh - m1 - xhat * m2)
        dz_ref[...] = dz.astype(BF16)
        gx_ref[...] = DN_ALPHA * dz

        @pl.when(i == nt - 1)
        def _():
            glg_ref[...] = jnp.sum(a_g[...], axis=0, keepdims=True)
            glb_ref[...] = jnp.sum(a_b[...], axis=0, keepdims=True)
            ls_ref[...] = jnp.sum(a_l[...], axis=0, keepdims=True)

    row = pl.BlockSpec((tm, d), lambda i: (i, 0))
    vec = pl.BlockSpec((1, d), lambda i: (0, 0))
    return pl.pallas_call(
        body, name="out_ln_loss", grid=(nt,),
        in_specs=[row, pl.BlockSpec((d, d), lambda i: (0, 0)), row, row, vec, vec],
        out_specs=[row, row, vec, vec, vec],
        out_shape=[jax.ShapeDtypeStruct((t_len, d), BF16), jax.ShapeDtypeStruct((t_len, d), F32)]
        + [jax.ShapeDtypeStruct((1, d), F32)] * 3,
        scratch_shapes=[pltpu.VMEM((SUBLANES, d), F32)] * 3,
        compiler_params=_cparams(("arbitrary",)),
    )(merged, w_out, x, target, ln_g, ln_b)


def _merge_bwd(dz, w_out, sa, sb, pa, pb):
    t_len, d = dz.shape
    tm = _tile(t_len, 256)
    nt = t_len // tm

    def body(dz_ref, w_ref, sa_ref, sb_ref, pa_ref, pb_ref, da_ref, db_ref, dg_ref, ga_ref, gb_ref, acc_a, acc_b):
        i = pl.program_id(0)

        @pl.when(i == 0)
        def _():
            acc_a[...] = jnp.zeros_like(acc_a)
            acc_b[...] = jnp.zeros_like(acc_b)

        dm = _dot_nt(dz_ref[...], w_ref[...])
        sa = sa_ref[...].astype(F32)
        sb = sb_ref[...].astype(F32)
        da = dm * sa
        db = dm * sb
        da_ref[...] = da.astype(BF16)
        db_ref[...] = db.astype(BF16)
        dga = da * pa_ref[...].astype(F32) * (1.0 - sa)
        dgb = db * pb_ref[...].astype(F32) * (1.0 - sb)
        dg_ref[0] = dga.astype(BF16)
        dg_ref[1] = dgb.astype(BF16)
        acc_a[...] += _fold8(dga)
        acc_b[...] += _fold8(dgb)

        @pl.when(i == nt - 1)
        def _():
            ga_ref[...] = jnp.sum(acc_a[...], axis=0, keepdims=True)
            gb_ref[...] = jnp.sum(acc_b[...], axis=0, keepdims=True)

    row = pl.BlockSpec((tm, d), lambda i: (i, 0))
    vec = pl.BlockSpec((1, d), lambda i: (0, 0))
    return pl.pallas_call(
        body, name="merge_bwd", grid=(nt,),
        in_specs=[row, pl.BlockSpec((d, d), lambda i: (0, 0)), row, row, row, row],
        out_specs=[row, row, pl.BlockSpec((2, tm, d), lambda i: (0, i, 0)), vec, vec],
        out_shape=[jax.ShapeDtypeStruct((t_len, d), BF16)] * 2 + [jax.ShapeDtypeStruct((2, t_len, d), BF16)]
        + [jax.ShapeDtypeStruct((1, d), F32)] * 2,
        scratch_shapes=[pltpu.VMEM((SUBLANES, d), F32)] * 2,
        compiler_params=_cparams(("arbitrary",)),
    )(dz, w_out, sa, sb, pa, pb)


def _grad_w(a, b, name, exchange=None):
    t_len, m = a.shape
    n = b.shape[1]
    tm, tn, tk = _tile(m, MM_TILE), _tile(n, MM_TILE), _tile(t_len, MM_DEPTH)
    ni, nj, nk = m // tm, n // tn, t_len // tk

    def body(a_ref, b_ref, *rest):
        if exchange is None:
            o_ref, acc = rest
        else:
            src, o_ref, land, acc, send_sems, recv_sems = rest
        i, j, k = pl.program_id(0), pl.program_id(1), pl.program_id(2)

        def copies():
            x, y, c = lax.axis_index("x"), lax.axis_index("y"), lax.axis_index("c")
            w = exchange.shape[1] // N_DEV
            return [pltpu.make_async_remote_copy(
                src_ref=_shard_ref(src, "col", 2 * s + 1 - c, w), dst_ref=land.at[s],
                send_sem=send_sems.at[s], recv_sem=recv_sems.at[s],
                device_id=(x, y, 1 - c), device_id_type=MESH) for s in range(N_CHIP)]

        if exchange is not None:
            @pl.when((i == 0) & (j == 0) & (k == 0))
            def _():
                for cp in copies():
                    cp.start()

        @pl.when(k == 0)
        def _():
            acc[...] = jnp.zeros_like(acc)

        acc[...] += _dot_tn(a_ref[...], b_ref[...])

        @pl.when(k == nk - 1)
        def _():
            o_ref[...] = acc[...].astype(BF16)

        if exchange is not None:
            @pl.when((i == ni - 1) & (j == nj - 1) & (k == nk - 1))
            def _():
                for cp in copies():
                    cp.wait()

    in_specs = [pl.BlockSpec((tk, tm), lambda i, j, k: (k, i)), pl.BlockSpec((tk, tn), lambda i, j, k: (k, j))]
    out_specs = [pl.BlockSpec((tm, tn), lambda i, j, k: (i, j))]
    out_shape = [jax.ShapeDtypeStruct((m, n), BF16)]
    scratch = [pltpu.VMEM((tm, tn), F32)]
    args = [a, b]
    if exchange is not None:
        hbm = pl.BlockSpec(memory_space=pl.ANY)
        in_specs.append(hbm)
        out_specs.append(hbm)
        out_shape.append(jax.ShapeDtypeStruct((N_CHIP, exchange.shape[0], exchange.shape[1] // N_DEV), exchange.dtype))
        scratch += [pltpu.SemaphoreType.DMA((N_CHIP,)), pltpu.SemaphoreType.DMA((N_CHIP,))]
        args.append(exchange)
    outs = pl.pallas_call(
        body, name=name, grid=(ni, nj, nk),
        in_specs=in_specs, out_specs=out_specs, out_shape=out_shape, scratch_shapes=scratch,
        compiler_params=_cparams(("arbitrary", "arbitrary", "arbitrary")),
    )(*args)
    return outs[0] if exchange is None else (outs[0], outs[1])


def _mixer_bwd(da, db, w_oa, w_ob, saved, dgate, ln_v_g, ln_v_b, w_s, b_s3, conv_w):
    t_len, d = da.shape
    hd = d // HEADS
    tm = _tile(t_len, 512)
    nt = t_len // tm

    def body(da_ref, db_ref, woa, wob, sv, dgt, lng, lnb, ws_ref, bs_ref, cw_ref,
             dp_ref, gws_ref, gbs_ref, glg_ref, glb_ref, gcw_ref, gcb_ref,
             dbuf, carry, a_ws, a_bs, a_lg, a_lb, a_c0, a_c1, a_c2, a_cb):
        i = pl.program_id(1)

        @pl.when(i == 0)
        def _():
            carry[...] = jnp.zeros_like(carry)
            for a in (a_ws, a_bs, a_lg, a_lb, a_c0, a_c1, a_c2, a_cb):
                a[...] = jnp.zeros_like(a)

        dya_t = _dot_nt(da_ref[...], woa[...])
        dyb_t = _dot_nt(db_ref[...], wob[...])
        g = lng[...]
        beta = lnb[...]
        wm = jnp.where(_tril_mask(), ws_ref[0], 0.0).astype(BF16)
        bs = bs_ref[0]
        w0, w1, w2 = cw_ref[0:1, :], cw_ref[1:2, :], cw_ref[2:3, :]
        dbuf[tm:tm + SUBLANES, :] = carry[...]
        gws = a_ws[...]
        gbs = a_bs[...]
        lg, lb = a_lg[...], a_lb[...]
        c0, c1, c2, cb = a_c0[...], a_c1[...], a_c2[...], a_cb[...]
        for ck in reversed(range(tm // CHUNK)):
            r0 = ck * CHUNK
            rows = slice(r0, r0 + CHUNK)
            dya = dya_t[rows]
            u = sv[0, rows, :].astype(F32)
            v = sv[1, rows, :].astype(F32)
            z = sv[2, rows, :].astype(F32)
            gu, gu_grad = _gelu(u)
            gv, gv_grad = _gelu(v)
            mu = jnp.mean(gv, axis=-1, keepdims=True)
            dvc = gv - mu
            var = jnp.mean(dvc * dvc, axis=-1, keepdims=True)
            rstd = lax.rsqrt(var + LN_EPS)
            vhat = dvc * rstd
            vn = (vhat * g + beta).astype(BF16)
            sz, sz_grad = _silu(z)
            t1 = dya * sz
            dmixed = t1 * gu
            dmixed_b = dmixed.astype(BF16)
            mixed = _dot(wm, vn) + bs
            gws = gws + _dot_nt(dmixed_b, vn)
            gbs = gbs + dmixed
            dvn = _dot_tn(wm, dmixed_b)
            dp_ref[0, rows, :] = (t1 * mixed * gu_grad).astype(BF16)
            dp_ref[2, rows, :] = (dya * gu * mixed * sz_grad).astype(BF16)
            lg = lg + _fold8(dvn * vhat)
            lb = lb + _fold8(dvn)
            dvh = dvn * g
            m1 = jnp.mean(dvh, axis=-1, keepdims=True)
            m2 = jnp.mean(dvh * vhat, axis=-1, keepdims=True)
            dp_ref[1, rows, :] = (rstd * (dvh - m1 - vhat * m2) * gv_grad).astype(BF16)
            dyb = dyb_t[rows]
            xbv = sv[3, rows, :].astype(F32)
            cbv = sv[4, rows, :].astype(F32)
            bbv = sv[5, rows, :].astype(F32)
            zbv = sv[6, rows, :].astype(F32)
            conv = sv[7, rows, :].astype(F32)
            szb, szb_grad = _silu(zbv)
            dp_ref[5, rows, :] = (dyb * conv * szb).astype(BF16)
            dp_ref[6, rows, :] = (dyb * bbv * conv * szb_grad).astype(BF16)
            dconv = dyb * bbv * szb
            dbuf[r0:r0 + CHUNK, :] = dconv
            dc1 = dbuf[r0 + 1:r0 + 1 + CHUNK, :]
            dc2 = dbuf[r0 + 2:r0 + 2 + CHUNK, :]
            if ck == 0:
                carry[...] = dconv[0:SUBLANES, :]
            h = cbv * xbv
            c2 = c2 + _fold8(dconv * h)
            c1 = c1 + _fold8(dc1 * h)
            c0 = c0 + _fold8(dc2 * h)
            cb = cb + _fold8(dconv)
            dh = w2 * dconv + w1 * dc1 + w0 * dc2
            dp_ref[3, rows, :] = (dh * cbv).astype(BF16)
            dp_ref[4, rows, :] = (dh * xbv).astype(BF16)
            dp_ref[7, rows, :] = dgt[0, rows, :]
            dp_ref[8, rows, :] = dgt[1, rows, :]
        a_ws[...] = gws
        a_bs[...] = gbs
        a_lg[...], a_lb[...] = lg, lb
        a_c0[...], a_c1[...], a_c2[...], a_cb[...] = c0, c1, c2, cb

        @pl.when(i == nt - 1)
        def _():
            gws_ref[0] = jnp.where(_tril_mask(), a_ws[...], 0.0)
            gbs_ref[0] = jnp.sum(a_bs[...], axis=1, keepdims=True)
            glg_ref[...] = jnp.sum(a_lg[...], axis=0, keepdims=True)
            glb_ref[...] = jnp.sum(a_lb[...], axis=0, keepdims=True)
            gcw_ref[0:1, :] = jnp.sum(a_c0[...], axis=0, keepdims=True)
            gcw_ref[1:2, :] = jnp.sum(a_c1[...], axis=0, keepdims=True)
            gcw_ref[2:3, :] = jnp.sum(a_c2[...], axis=0, keepdims=True)
            gcb_ref[...] = jnp.sum(a_cb[...], axis=0, keepdims=True)

    def rev(i):
        return nt - 1 - i

    row = pl.BlockSpec((tm, d), lambda g, i: (rev(i), 0))
    wrow = pl.BlockSpec((hd, d), lambda g, i: (g, 0))
    vec = pl.BlockSpec((1, hd), lambda g, i: (0, g))
    acc8 = pltpu.VMEM((SUBLANES, hd), F32)
    return pl.pallas_call(
        body, name="mixer_bwd", grid=(HEADS, nt),
        in_specs=[row, row, wrow, wrow,
                  pl.BlockSpec((N_SAVE, tm, hd), lambda g, i: (0, rev(i), g)),
                  pl.BlockSpec((2, tm, hd), lambda g, i: (0, rev(i), g)),
                  vec, vec,
                  pl.BlockSpec((1, CHUNK, CHUNK), lambda g, i: (g, 0, 0)),
                  pl.BlockSpec((1, CHUNK, 1), lambda g, i: (g, 0, 0)),
                  pl.BlockSpec((3, hd), lambda g, i: (0, g))],
        out_specs=[pl.BlockSpec((N_IN, tm, hd), lambda g, i: (0, rev(i), g)),
                   pl.BlockSpec((1, CHUNK, CHUNK), lambda g, i: (g, 0, 0)),
                   pl.BlockSpec((1, CHUNK, 1), lambda g, i: (g, 0, 0)),
                   vec, vec,
                   pl.BlockSpec((3, hd), lambda g, i: (0, g)),
                   vec],
        out_shape=[jax.ShapeDtypeStruct((N_IN, t_len, d), BF16),
                   jax.ShapeDtypeStruct((HEADS, CHUNK, CHUNK), F32),
                   jax.ShapeDtypeStruct((HEADS, CHUNK, 1), F32),
                   jax.ShapeDtypeStruct((1, d), F32), jax.ShapeDtypeStruct((1, d), F32),
                   jax.ShapeDtypeStruct((3, d), F32), jax.ShapeDtypeStruct((1, d), F32)],
        scratch_shapes=[pltpu.VMEM((tm + SUBLANES, hd), F32), acc8,
                        pltpu.VMEM((CHUNK, CHUNK), F32), pltpu.VMEM((CHUNK, hd), F32),
                        acc8, acc8, acc8, acc8, acc8, acc8],
        compiler_params=_cparams(("parallel", "arbitrary")),
    )(da, db, w_oa, w_ob, saved, dgate, ln_v_g, ln_v_b, w_s, b_s3, conv_w)


def _grad_w_in(xb, dp):
    t_len, d = xb.shape
    tm, tn, tk = _tile(d, MM_TILE), _tile(d, MM_TILE), _tile(t_len, MM_DEPTH)
    nj = d // tn
    nk = t_len // tk

    def body(a_ref, b_ref, o_ref, acc):
        k = pl.program_id(2)

        @pl.when(k == 0)
        def _():
            acc[...] = jnp.zeros_like(acc)

        acc[...] += _dot_tn(a_ref[...], b_ref[...])

        @pl.when(k == nk - 1)
        def _():
            o_ref[...] = acc[...].astype(BF16)

    return pl.pallas_call(
        body, name="grad_w_in", grid=(d // tm, N_IN * nj, nk),
        in_specs=[pl.BlockSpec((tk, tm), lambda i, j, k: (k, i)),
                  pl.BlockSpec((None, tk, tn), lambda i, j, k: (j // nj, k, j % nj))],
        out_specs=pl.BlockSpec((tm, tn), lambda i, j, k: (i, j)),
        out_shape=jax.ShapeDtypeStruct((d, N_IN * d), BF16),
        scratch_shapes=[pltpu.VMEM((tm, tn), F32)],
        compiler_params=_cparams(("parallel", "parallel", "arbitrary")),
    )(xb, dp)


def _grad_x(dp, w_in, gx_direct, parts, packed):
    _, t_len, d = dp.shape
    tm, tn, tk = _tile(t_len, MM_TILE), _tile(d, MM_TILE), _tile(d, MM_DEPTH)
    nkb = d // tk
    nk = N_IN * nkb
    ni, nj = t_len // tm, d // tn
    n_parts = len(parts)

    def body(a_ref, b_ref, r_ref, *rest):
        srcs = rest[:n_parts]
        pk_ref = rest[n_parts]
        o_ref = rest[n_parts + 1]
        lands = rest[n_parts + 2:2 * n_parts + 2]
        gath = rest[2 * n_parts + 2]
        acc, send_sems, recv_sems, pk_send, pk_recv, pk_local = rest[2 * n_parts + 3:]
        i, j, k = pl.program_id(0), pl.program_id(1), pl.program_id(2)
        x, y, c = lax.axis_index("x"), lax.axis_index("y"), lax.axis_index("c")
        my_chip = 2 * x + y
        me = 4 * x + 2 * y + c
        chips = [(1 - x, y), (x, 1 - y), (1 - x, 1 - y)]

        def part_copy(t, n):
            px, py = chips[n]
            return pltpu.make_async_remote_copy(
                src_ref=srcs[t].at[2 * px + py], dst_ref=lands[t].at[my_chip],
                send_sem=send_sems.at[3 * t + n], recv_sem=recv_sems.at[3 * t + n],
                device_id=(px, py, c), device_id_type=MESH)

        def part_landing(t, n):
            px, py = chips[n]
            return pltpu.make_async_remote_copy(
                src_ref=srcs[t].at[my_chip], dst_ref=lands[t].at[2 * px + py],
                send_sem=send_sems.at[3 * t + n], recv_sem=recv_sems.at[3 * t + n],
                device_id=(px, py, c), device_id_type=MESH)

        def pk_copy(s):
            return pltpu.make_async_remote_copy(
                src_ref=pk_ref, dst_ref=gath.at[me], send_sem=pk_send, recv_sem=pk_recv.at[me],
                device_id=(s // 4, (s // 2) % 2, s % 2), device_id_type=MESH)

        @pl.when((i == 0) & (j == 0) & (k == 0))
        def _():
            for t in range(n_parts):
                for n in range(3):
                    part_copy(t, n).start()
            pltpu.make_async_copy(pk_ref, gath.at[me], pk_local).start()
            for s in range(N_DEV):
                @pl.when(s != me)
                def _(s=s):
                    pk_copy(s).start()

        @pl.when(k == 0)
        def _():
            acc[...] = r_ref[...]

        acc[...] += _dot(a_ref[...], b_ref[...])

        @pl.when(k == nk - 1)
        def _():
            o_ref[...] = acc[...]

        @pl.when((i == ni - 1) & (j == nj - 1) & (k == nk - 1))
        def _():
            for t in range(n_parts):
                for n in range(3):
                    part_landing(t, n).wait_recv()
            for t in range(n_parts):
                for n in range(3):
                    part_copy(t, n).wait_send()
            for s in range(N_DEV):
                @pl.when(s != me)
                def _(s=s):
                    pltpu.make_async_remote_copy(
                        src_ref=pk_ref, dst_ref=gath.at[s], send_sem=pk_send, recv_sem=pk_recv.at[s],
                        device_id=(s // 4, (s // 2) % 2, s % 2), device_id_type=MESH).wait_recv()
            seven = gath.at[pl.ds(0, N_DEV - 1)]
            pltpu.make_async_remote_copy(src_ref=seven, dst_ref=seven, send_sem=pk_send, recv_sem=pk_send,
                                         device_id=(x, y, 1 - c), device_id_type=MESH).wait_send()
            pltpu.make_async_copy(pk_ref, gath.at[me], pk_local).wait()

    hbm = pl.BlockSpec(memory_space=pl.ANY)
    outs = pl.pallas_call(
        body, name="grad_x", grid=(ni, nj, nk),
        in_specs=[pl.BlockSpec((None, tm, tk), lambda i, j, k: (k // nkb, i, k % nkb)),
                  pl.BlockSpec((tk, tn), lambda i, j, k: (k, j)),
                  pl.BlockSpec((tm, tn), lambda i, j, k: (i, j))] + [hbm] * (n_parts + 1),
        out_specs=[pl.BlockSpec((tm, tn), lambda i, j, k: (i, j))] + [hbm] * (n_parts + 1),
        out_shape=[jax.ShapeDtypeStruct((t_len, d), F32)] + [jax.ShapeDtypeStruct(p.shape, p.dtype) for p in parts]
        + [jax.ShapeDtypeStruct((N_DEV,) + packed.shape, packed.dtype)],
        scratch_shapes=[pltpu.VMEM((tm, tn), F32),
                        pltpu.SemaphoreType.DMA((3 * n_parts,)), pltpu.SemaphoreType.DMA((3 * n_parts,)),
                        pltpu.SemaphoreType.DMA(()), pltpu.SemaphoreType.DMA((N_DEV,)), pltpu.SemaphoreType.DMA(())],
        compiler_params=_cparams(("arbitrary", "arbitrary", "arbitrary")),
    )(dp, w_in, gx_direct, *parts, packed)
    return outs[0], list(outs[1:1 + n_parts]), outs[1 + n_parts]


def kernel(x, w_in, b_gate, ln_v_g, ln_v_b, w_s, b_s, conv_w, conv_b, w_oa, w_ob, w_out, ln_g, ln_b, loss_target, m_w_in, m_b_gate, m_ln_v_g, m_ln_v_b, m_w_s, m_b_s, m_conv_w, m_conv_b, m_w_oa, m_w_ob, m_w_out, m_ln_g, m_ln_b, v_w_in, v_b_gate, v_ln_v_g, v_ln_v_b, v_w_s, v_b_s, v_conv_w, v_conv_b, v_w_oa, v_w_ob, v_w_out, v_ln_g, v_ln_b):
    _, t_len, d = x.shape
    assert d % (HEADS * 128) == 0 and t_len % CHUNK == 0 and w_in.shape[2] * N_DEV == N_IN * d
    x2 = x[0]
    tgt2 = loss_target[0]
    c_arr = lax.axis_index("c").astype(jnp.int32).reshape(1)
    chip_arr = (2 * lax.axis_index("x") + lax.axis_index("y")).astype(jnp.int32).reshape(1)
    dev = 4 * lax.axis_index("x") + 2 * lax.axis_index("y") + lax.axis_index("c")

    xb = _cast_bf16(x2, "cast_x")
    shards = [_cast_transposed(w_in[0], "cast_w_in"), _cast_bf16(w_oa[0], "cast_w_oa"),
              _cast_bf16(w_ob[0], "cast_w_ob"), _cast_bf16(w_out[0], "cast_w_out")]
    (conv_w_g,) = _all_gather([conv_w[0]], ["lead"], "gather_conv_w", vmem=True)
    conv_w_f = jnp.transpose(conv_w_g, (1, 0, 2)).reshape(3, d)
    w_s3 = w_s[0]
    b_s3 = b_s[0].reshape(HEADS, CHUNK, 1)

    saved, ya, yb, w_in_f, w_oa_f, w_ob_f, w_out_f = _mixer_fwd(
        xb, shards, ln_v_g, ln_v_b, w_s3, b_s3, conv_w_f, conv_b)
    sa, sb, pa, pb, merged = _merge_fwd(xb, ya, yb, w_in_f, w_oa_f, w_ob_f, b_gate)
    dz, gx_direct, g_ln_g, g_ln_b, err2 = _out_ln_loss(merged, w_out_f, x2, tgt2, ln_g, ln_b)
    loss = lax.psum(0.5 * jnp.sum(err2) / d, ("x", "y", "c"))

    da, db, dgate, g_bga, g_bgb = _merge_bwd(dz, w_out_f, sa, sb, pa, pb)
    dp, g_ws, g_bs, g_lvg, g_lvb, g_cw, g_cb = _mixer_bwd(
        da, db, w_oa_f, w_ob_f, saved, dgate, ln_v_g, ln_v_b, w_s3, b_s3, conv_w_f)
    gw_in = _grad_w_in(xb, dp)

    gw_out, land_in = _grad_w(merged, dz, "grad_w_out", exchange=gw_in)
    part_in = _pair_sum(gw_in, land_in, "col", c_arr, "grad_pair_sum_0")
    gw_oa = _grad_w(ya, da, "grad_w_oa")
    gw_ob = _grad_w(yb, db, "grad_w_ob")
    rows = [gw_oa, gw_ob, gw_out]
    lands = _pair_exchange(rows, ["row"] * 3, "grad_pair_exchange")
    parts = [part_in] + [_pair_sum(g, l, "row", c_arr, "grad_pair_sum_%d" % (n + 1))
                         for n, (g, l) in enumerate(zip(rows, lands))]
    pieces = [jnp.concatenate([g_bga, g_bgb], axis=1), g_lvg, g_lvb, g_ws, g_bs, g_cw, g_cb, g_ln_g, g_ln_b]
    sizes = [p.size for p in pieces]
    packed = jnp.concatenate([p.reshape(-1, 128) for p in pieces], axis=0)
    grad_x, lands2, gathered = _grad_x(dp, w_in_f, gx_direct, parts, packed)
    grad_x = grad_x[None]

    big = []
    for n, (w, m, v) in enumerate([(w_in, m_w_in, v_w_in), (w_oa, m_w_oa, v_w_oa), (w_ob, m_w_ob, v_w_ob),
                                   (w_out, m_w_out, v_w_out)]):
        big.append([o[None] for o in _sum_adam(parts[n], lands2[n], w[0], m[0], v[0], chip_arr, "sum_adam_%d" % n)])
    (g_w_in, d_w_in, nm_w_in, nv_w_in), (g_w_oa, d_w_oa, nm_w_oa, nv_w_oa), \
        (g_w_ob, d_w_ob, nm_w_ob, nv_w_ob), (g_w_out, d_w_out, nm_w_out, nv_w_out) = big

    total = _small_sum(gathered, "sum_small_grads")
    offs = [0]
    for s in sizes:
        offs.append(offs[-1] + s // 128)
    unpacked = [total[offs[n]:offs[n + 1]] for n in range(len(pieces))]
    g_b_gate = unpacked[0].reshape(b_gate.shape)
    g_ln_v_g = unpacked[1].reshape(ln_v_g.shape)
    g_ln_v_b = unpacked[2].reshape(ln_v_b.shape)
    g_w_s = unpacked[3].reshape(w_s.shape)
    g_b_s = unpacked[4].reshape(b_s.shape)
    g_conv_w = lax.dynamic_slice_in_dim(unpacked[5].reshape(3, d), dev * (d // N_DEV), d // N_DEV, axis=1)[None]
    g_conv_b = unpacked[6].reshape(conv_b.shape)
    g_ln_g2 = unpacked[7].reshape(ln_g.shape)
    g_ln_b2 = unpacked[8].reshape(ln_b.shape)

    small_w = [b_gate, ln_v_g, ln_v_b, w_s, b_s, conv_w, conv_b, ln_g, ln_b]
    small_g = [g_b_gate, g_ln_v_g, g_ln_v_b, g_w_s, g_b_s, g_conv_w, g_conv_b, g_ln_g2, g_ln_b2]
    small_m = [m_b_gate, m_ln_v_g, m_ln_v_b, m_w_s, m_b_s, m_conv_w, m_conv_b, m_ln_g, m_ln_b]
    small_v = [v_b_gate, v_ln_v_g, v_ln_v_b, v_w_s, v_b_s, v_conv_w, v_conv_b, v_ln_g, v_ln_b]

    def flat(a):
        return a.reshape(-1, a.shape[-1])

    res = _small_adam([flat(a) for a in small_w], [flat(a) for a in small_g], [flat(a) for a in small_m],
                      [flat(a) for a in small_v], "adam_small")
    ns = len(small_w)
    d_s = [res[n].reshape(small_w[n].shape) for n in range(ns)]
    nm_s = [res[ns + n].reshape(small_w[n].shape) for n in range(ns)]
    nv_s = [res[2 * ns + n].reshape(small_w[n].shape) for n in range(ns)]

    def ordered(first, small, oa, ob, out):
        return [first] + small[:7] + [oa, ob, out] + small[7:]

    return (loss, grad_x,
            *ordered(g_w_in, small_g, g_w_oa, g_w_ob, g_w_out),
            *ordered(d_w_in, d_s, d_w_oa, d_w_ob, d_w_out),
            *ordered(nm_w_in, nm_s, nm_w_oa, nm_w_ob, nm_w_out),
            *ordered(nv_w_in, nv_s, nv_w_oa, nv_w_ob, nv_w_out))
```

```python
import functools

import jax
import jax.numpy as jnp
from jax import lax
from jax.experimental import pallas as pl
from jax.experimental.pallas import tpu as pltpu

F32 = jnp.float32
BF16 = jnp.bfloat16
MESH = pl.DeviceIdType.MESH

N_DEV = 8
N_CHIP = 4
HEADS = 8
CHUNK = 128
N_IN = 9
N_MIX = 7
N_SAVE = 8
LN_EPS = 1e-5
DN_ALPHA = 2.0 ** 0.25
ADAM_LR = 0.001
ADAM_B1 = 0.9
ADAM_B2 = 0.999
ADAM_EPS = 1e-08
ADAM_WD = 0.01
ADAM_STEP = 10
GELU_C0 = 0.7978845608028654
GELU_C1 = 0.044715
SUBLANES = 8
FIRST_SENDS = 4
SENDS_PER_GROUP = 2
MM_TILE = 1024
MM_DEPTH = 2048
MM_DEPTH_TOKENS = 4096
GRAD_X_COMM_BLOCKS = 7
VMEM_LIMIT = 56 << 20


def _cparams(sem):
    return pltpu.CompilerParams(dimension_semantics=sem, vmem_limit_bytes=VMEM_LIMIT)


def _tile(n, want):
    t = min(n, want)
    while n % t:
        t //= 2
    return t


def _gelu(u):
    u2 = u * u
    t = jnp.tanh(u * (GELU_C0 + (GELU_C0 * GELU_C1) * u2))
    hp = 0.5 * t + 0.5
    grad = hp + (0.5 * u) * (1.0 - t * t) * (GELU_C0 + (3.0 * GELU_C0 * GELU_C1) * u2)
    return u * hp, grad


def _silu(z):
    s = jax.nn.sigmoid(z)
    sil = z * s
    return sil, s + sil * (1.0 - s)


def _fold8(a):
    return jnp.sum(a.reshape(a.shape[0] // SUBLANES, SUBLANES, a.shape[1]), axis=0)


def _dot(a, b):
    return jnp.dot(a, b, preferred_element_type=F32)


def _dot_nt(a, b):
    return lax.dot_general(a, b, (((1,), (1,)), ((), ())), preferred_element_type=F32)


def _dot_tn(a, b):
    return lax.dot_general(a, b, (((0,), (0,)), ((), ())), preferred_element_type=F32)


def _tril_mask():
    r = lax.broadcasted_iota(jnp.int32, (CHUNK, CHUNK), 0)
    c = lax.broadcasted_iota(jnp.int32, (CHUNK, CHUNK), 1)
    return c <= r


def _cast_bf16(a, name):
    rows, cols = a.shape
    rb = _tile(rows, 256)

    def body(a_ref, o_ref):
        o_ref[...] = a_ref[...].astype(BF16)

    return pl.pallas_call(
        body, name=name, grid=(rows // rb,),
        in_specs=[pl.BlockSpec((rb, cols), lambda i: (i, 0))],
        out_specs=pl.BlockSpec((rb, cols), lambda i: (i, 0)),
        out_shape=jax.ShapeDtypeStruct((rows, cols), BF16),
        compiler_params=_cparams(("parallel",)),
    )(a)


def _cast_transposed(a, name):
    rows, cols = a.shape
    tb = _tile(rows, 2048)
    tc = _tile(cols, 256)

    def body(a_ref, o_ref):
        o_ref[...] = a_ref[...].T.astype(BF16)

    return pl.pallas_call(
        body, name=name, grid=(cols // tc, rows // tb),
        in_specs=[pl.BlockSpec((tb, tc), lambda j, i: (i, j))],
        out_specs=pl.BlockSpec((tc, tb), lambda j, i: (j, i)),
        out_shape=jax.ShapeDtypeStruct((cols, rows), BF16),
        compiler_params=_cparams(("parallel", "parallel")),
    )(a)


def _shard_ref(full, kind, s, n):
    if kind == "col":
        return full.at[:, pl.ds(pl.multiple_of(s * n, 128), n)]
    if kind == "row":
        return full.at[pl.ds(pl.multiple_of(s * n, SUBLANES), n), :]
    return full.at[s]


def _all_gather(shards, kinds, name, vmem):
    nt = len(shards)
    out_shapes = []
    for a, kind in zip(shards, kinds):
        if kind == "col":
            out_shapes.append(jax.ShapeDtypeStruct((a.shape[0], N_DEV * a.shape[1]), a.dtype))
        elif kind == "row":
            out_shapes.append(jax.ShapeDtypeStruct((N_DEV * a.shape[0], a.shape[1]), a.dtype))
        else:
            out_shapes.append(jax.ShapeDtypeStruct((N_DEV,) + a.shape, a.dtype))

    def body(*refs):
        srcs, fulls = refs[:nt], refs[nt:2 * nt]
        send_sems, recv_sems, local_sems = refs[2 * nt:]
        x, y, c = lax.axis_index("x"), lax.axis_index("y"), lax.axis_index("c")
        sibling = (x, y, 1 - c)
        chips = [(1 - x, y), (x, 1 - y), (1 - x, 1 - y)]

        def dev(px, py, pc):
            return 4 * px + 2 * py + pc

        def region(t, s):
            a, kind = shards[t], kinds[t]
            n = a.shape[1] if kind == "col" else a.shape[0]
            return _shard_ref(fulls[t], kind, s, n)

        def copy(t, k, block, to, own=False):
            return pltpu.make_async_remote_copy(
                src_ref=srcs[t] if own else region(t, block), dst_ref=region(t, block),
                send_sem=send_sems.at[7 * t + k], recv_sem=recv_sems.at[7 * t + k],
                device_id=to, device_id_type=MESH)

        me = dev(x, y, c)
        started = []
        for t in range(nt):
            mine = pltpu.make_async_copy(srcs[t], region(t, me), local_sems.at[t])
            mine.start()
            started.append(mine)
        first = []
        for t in range(nt):
            first.append(copy(t, 0, me, sibling, own=True))
            for j, chip in enumerate(chips):
                first.append(copy(t, 1 + j, me, (*chip, c), own=True))
        for cp in first:
            cp.start()
        passed = []
        for t in range(nt):
            for j, chip in enumerate(chips):
                blk = dev(*chip, c)
                copy(t, 1 + j, blk, sibling).wait_recv()
                fwd = copy(t, 4 + j, blk, sibling)
                fwd.start()
                passed.append(fwd)
        for t in range(nt):
            copy(t, 0, dev(x, y, 1 - c), sibling).wait_recv()
            for j, chip in enumerate(chips):
                copy(t, 4 + j, dev(*chip, 1 - c), sibling).wait_recv()
        for cp in first + passed:
            cp.wait_send()
        for mine in started:
            mine.wait()

    space = pltpu.VMEM if vmem else pl.ANY
    return pl.pallas_call(
        body, name=name,
        in_specs=[pl.BlockSpec(memory_space=space)] * nt,
        out_specs=[pl.BlockSpec(memory_space=space)] * nt,
        out_shape=out_shapes,
        scratch_shapes=[pltpu.SemaphoreType.DMA((7 * nt,)), pltpu.SemaphoreType.DMA((7 * nt,)),
                        pltpu.SemaphoreType.DMA((nt,))],
        compiler_params=pltpu.CompilerParams(vmem_limit_bytes=VMEM_LIMIT, has_side_effects=True),
    )(*shards)


def _pair_exchange(grads, kinds, name):
    nt = len(grads)
    shard_shapes = []
    for g, kind in zip(grads, kinds):
        shard_shapes.append((g.shape[0], g.shape[1] // N_DEV) if kind == "col" else (g.shape[0] // N_DEV, g.shape[1]))

    def body(*refs):
        srcs, lands = refs[:nt], refs[nt:2 * nt]
        send_sems, recv_sems = refs[2 * nt:]
        x, y, c = lax.axis_index("x"), lax.axis_index("y"), lax.axis_index("c")
        copies = []
        for t in range(nt):
            n = shard_shapes[t][1] if kinds[t] == "col" else shard_shapes[t][0]
            for k in range(N_CHIP):
                cp = pltpu.make_async_remote_copy(
                    src_ref=_shard_ref(srcs[t], kinds[t], 2 * k + 1 - c, n), dst_ref=lands[t].at[k],
                    send_sem=send_sems.at[N_CHIP * t + k], recv_sem=recv_sems.at[N_CHIP * t + k],
                    device_id=(x, y, 1 - c), device_id_type=MESH)
                cp.start()
                copies.append(cp)
        for cp in copies:
            cp.wait()

    return pl.pallas_call(
        body, name=name,
        in_specs=[pl.BlockSpec(memory_space=pl.ANY)] * nt,
        out_specs=[pl.BlockSpec(memory_space=pl.ANY)] * nt,
        out_shape=[jax.ShapeDtypeStruct((N_CHIP,) + s, g.dtype) for s, g in zip(shard_shapes, grads)],
        scratch_shapes=[pltpu.SemaphoreType.DMA((N_CHIP * nt,)), pltpu.SemaphoreType.DMA((N_CHIP * nt,))],
        compiler_params=pltpu.CompilerParams(has_side_effects=True),
    )(*grads)


def _pair_sum(grad, land, kind, c_arr, name):
    _, r, w = land.shape
    rb = _tile(r, 256)
    nrb = r // rb

    def body(c_ref, g_ref, l_ref, o_ref):
        o_ref[...] = (g_ref[...].astype(F32) + l_ref[...].astype(F32)).astype(o_ref.dtype)

    if kind == "col":
        g_spec = pl.BlockSpec((rb, w), lambda k, i, c: (i, 2 * k + c[0]))
    else:
        g_spec = pl.BlockSpec((rb, w), lambda k, i, c: ((2 * k + c[0]) * nrb + i, 0))
    return pl.pallas_call(
        body, name=name,
        grid_spec=pltpu.PrefetchScalarGridSpec(
            num_scalar_prefetch=1, grid=(N_CHIP, nrb),
            in_specs=[g_spec, pl.BlockSpec((None, rb, w), lambda k, i, c: (k, i, 0))],
            out_specs=pl.BlockSpec((None, rb, w), lambda k, i, c: (k, i, 0))),
        out_shape=jax.ShapeDtypeStruct(land.shape, BF16),
        compiler_params=_cparams(("parallel", "parallel")),
    )(c_arr, grad, land)


def _adam(w, g, m, v):
    m = ADAM_B1 * m + (1.0 - ADAM_B1) * g
    v = ADAM_B2 * v + (1.0 - ADAM_B2) * jnp.square(g)
    m_hat = m / (1.0 - ADAM_B1 ** ADAM_STEP)
    v_hat = v / (1.0 - ADAM_B2 ** ADAM_STEP)
    delta = -ADAM_LR * (m_hat / (jnp.sqrt(v_hat) + ADAM_EPS) + ADAM_WD * w)
    return delta, m, v


def _sum_adam(part, land, w, m, v, chip_arr, name):
    r, wd = w.shape
    rb = _tile(r, 128)

    def body(k_ref, own, r1, r2, r3, w_ref, m_ref, v_ref, g_out, d_out, m_out, v_out):
        g = own[...].astype(F32) + r1[...].astype(F32) + r2[...].astype(F32) + r3[...].astype(F32)
        d, mn, vn = _adam(w_ref[...], g, m_ref[...], v_ref[...])
        g_out[...] = g
        d_out[...] = d
        m_out[...] = mn
        v_out[...] = vn

    def slot(off):
        return pl.BlockSpec((None, rb, wd), lambda i, k: ((k[0] + off) % N_CHIP, i, 0))

    plain = pl.BlockSpec((rb, wd), lambda i, k: (i, 0))
    return pl.pallas_call(
        body, name=name,
        grid_spec=pltpu.PrefetchScalarGridSpec(
            num_scalar_prefetch=1, grid=(r // rb,),
            in_specs=[slot(0), slot(1), slot(2), slot(3), plain, plain, plain],
            out_specs=[plain] * 4),
        out_shape=[jax.ShapeDtypeStruct(w.shape, F32)] * 4,
        compiler_params=_cparams(("parallel",)),
    )(chip_arr, part, land, land, land, w, m, v)


def _small_sum(gathered, name):
    _, r, w = gathered.shape

    def body(g_ref, o_ref):
        acc = g_ref[0]
        for d in range(1, N_DEV):
            acc = acc + g_ref[d]
        o_ref[...] = acc

    return pl.pallas_call(body, name=name, out_shape=jax.ShapeDtypeStruct((r, w), F32))(gathered)


def _small_adam(ws, gs, ms, vs, name):
    n = len(ws)

    def body(*refs):
        ins, outs = refs[:4 * n], refs[4 * n:]
        for t in range(n):
            d, mn, vn = _adam(ins[t][...], ins[n + t][...], ins[2 * n + t][...], ins[3 * n + t][...])
            outs[t][...] = d
            outs[n + t][...] = mn
            outs[2 * n + t][...] = vn

    shapes = [jax.ShapeDtypeStruct(w.shape, F32) for w in ws]
    return pl.pallas_call(body, name=name, out_shape=shapes * 3)(*ws, *gs, *ms, *vs)


def _mixer_fwd(x, shards, ln_v_g, ln_v_b, w_s, b_s3, conv_w, conv_b):
    t_len, d = x.shape
    hd = d // HEADS
    tm = _tile(t_len, 512)
    nt = t_len // tm
    assert nt >= 2
    w8 = shards[0].shape[0]
    bps = w8 // hd
    r8 = shards[1].shape[0]
    nq = N_IN * HEADS
    n_blocks = nq + 3 * N_DEV
    groups = [[("in", HEADS * b + k) for b in range(N_MIX)] for k in range(HEADS)]
    groups.append([("in", q) for q in range(N_MIX * HEADS, nq)] + [(t, s) for t in (1, 2, 3) for s in range(N_DEV)])

    def owner_of(blk):
        return blk[1] // bps if blk[0] == "in" else blk[1]

    send_step = {}
    for s in range(N_DEV):
        mine = [blk for grp in groups for blk in grp if owner_of(blk) == s]
        for pos, blk in enumerate(mine):
            send_step[blk] = 0 if pos < FIRST_SENDS else 1 + (pos - FIRST_SENDS) // SENDS_PER_GROUP
    assert max(send_step.values()) < HEADS
    tail_pass_step = nt - 6 if nt >= 8 else nt - 2

    def body(x_ref, sh_in, sh_oa, sh_ob, sh_out, lng, lnb, ws_ref, bs_ref, cw_ref, cb_ref,
             save_ref, ya_ref, yb_ref, f_in, f_oa, f_ob, f_out,
             hbuf, wbuf, recv_sems, own_sems, fwd_sems, local_sems, load_sems):
        g = pl.program_id(0)
        i = pl.program_id(1)
        x, y, c = lax.axis_index("x"), lax.axis_index("y"), lax.axis_index("c")
        sibling = (x, y, 1 - c)
        chips = [(1 - x, y), (x, 1 - y), (1 - x, 1 - y)]
        shard_refs = (sh_in, sh_oa, sh_ob, sh_out)
        fulls = (f_in, f_oa, f_ob, f_out)

        def tensor(blk):
            return 0 if blk[0] == "in" else blk[0]

        def owner(blk):
            s = owner_of(blk)
            return s // 4, (s // 2) % 2, s % 2

        def bid(blk):
            return blk[1] if blk[0] == "in" else nq + (blk[0] - 1) * N_DEV + blk[1]

        def region(blk):
            if blk[0] == "in":
                return f_in.at[pl.ds(blk[1] * hd, hd), :]
            return fulls[blk[0]].at[pl.ds(blk[1] * r8, r8), :]

        def own_src(blk):
            if blk[0] == "in":
                return sh_in.at[pl.ds((blk[1] % bps) * hd, hd), :]
            return shard_refs[blk[0]]

        def rcopy(blk, to, send_sem, own):
            return pltpu.make_async_remote_copy(
                src_ref=own_src(blk) if own else region(blk), dst_ref=region(blk),
                send_sem=send_sem, recv_sem=recv_sems.at[bid(blk)], device_id=to, device_id_type=MESH)

        def send_own(blk):
            ox, oy, oc = owner(blk)

            @pl.when((x == ox) & (y == oy) & (c == oc))
            def _():
                rcopy(blk, sibling, own_sems.at[tensor(blk)], True).start()
                for chip in chips:
                    rcopy(blk, (*chip, c), own_sems.at[tensor(blk)], True).start()

        def pass_on(blk):
            ox, oy, oc = owner(blk)

            @pl.when(((x != ox) | (y != oy)) & (c == oc))
            def _():
                rcopy(blk, sibling, fwd_sems.at[tensor(blk)], False).wait_recv()
                rcopy(blk, sibling, fwd_sems.at[tensor(blk)], False).start()

        def wait_from_sibling(blk):
            @pl.when(c != owner(blk)[2])
            def _():
                rcopy(blk, sibling, fwd_sems.at[tensor(blk)], False).wait_recv()

        def local_copy(t):
            me = 4 * x + 2 * y + c
            return pltpu.make_async_copy(shard_refs[t], _shard_ref(fulls[t], "row", me, w8 if t == 0 else r8),
                                         local_sems.at[t])

        first = (g == 0) & (i == 0)

        @pl.when(first)
        def _():
            for t in range(4):
                local_copy(t).start()
            for grp in groups:
                for blk in grp:
                    if send_step[blk] == 0:
                        send_own(blk)
            for blk in groups[0]:
                pass_on(blk)

        @pl.when(i == nt - 2)
        def _():
            for k in range(HEADS - 1):
                @pl.when(g == k)
                def _(k=k):
                    for blk in groups[k + 1]:
                        pass_on(blk)

        @pl.when((g == HEADS - 1) & (i == tail_pass_step))
        def _():
            for blk in groups[HEADS]:
                pass_on(blk)

        @pl.when(i == 0)
        def _():
            for k in range(HEADS):
                @pl.when(g == k)
                def _(k=k):
                    if k > 0:
                        for grp in groups:
                            for blk in grp:
                                if send_step[blk] == k:
                                    send_own(blk)
                    for blk in groups[k]:
                        wait_from_sibling(blk)
                    for b, blk in enumerate(groups[k]):
                        ox, oy, oc = owner(blk)
                        mine = (x == ox) & (y == oy) & (c == oc)

                        @pl.when(mine)
                        def _(b=b, blk=blk):
                            pltpu.make_async_copy(own_src(blk), wbuf.at[b], load_sems.at[b]).start()

                        @pl.when(jnp.logical_not(mine))
                        def _(b=b, blk=blk):
                            pltpu.make_async_copy(region(blk), wbuf.at[b], load_sems.at[b]).start()

                    for b, blk in enumerate(groups[k]):
                        pltpu.make_async_copy(region(blk), wbuf.at[b], load_sems.at[b]).wait()

        xt = x_ref[...].astype(BF16)
        u = _dot_nt(xt, wbuf[0])
        v = _dot_nt(xt, wbuf[1])
        z = _dot_nt(xt, wbuf[2])
        save_ref[0] = u.astype(BF16)
        save_ref[1] = v.astype(BF16)
        save_ref[2] = z.astype(BF16)
        gu, _ = _gelu(u)
        gv, _ = _gelu(v)
        mu = jnp.mean(gv, axis=-1, keepdims=True)
        dv = gv - mu
        var = jnp.mean(dv * dv, axis=-1, keepdims=True)
        vn = (dv * lax.rsqrt(var + LN_EPS) * lng[...] + lnb[...]).astype(BF16)
        sz, _ = _silu(z)
        gate = gu * sz
        wm = jnp.where(_tril_mask(), ws_ref[0], 0.0).astype(BF16)
        bs = bs_ref[0]
        for ck in range(tm // CHUNK):
            rows = slice(ck * CHUNK, (ck + 1) * CHUNK)
            mixed = _dot(wm, vn[rows]) + bs
            ya_ref[rows, :] = (gate[rows] * mixed).astype(BF16)

        xbv = _dot_nt(xt, wbuf[3])
        cbv = _dot_nt(xt, wbuf[4])
        bbv = _dot_nt(xt, wbuf[5])
        zbv = _dot_nt(xt, wbuf[6])
        save_ref[3] = xbv.astype(BF16)
        save_ref[4] = cbv.astype(BF16)
        save_ref[5] = bbv.astype(BF16)
        save_ref[6] = zbv.astype(BF16)
        h = cbv * xbv

        @pl.when(i == 0)
        def _():
            hbuf[0:SUBLANES, :] = jnp.zeros((SUBLANES, hd), F32)

        hbuf[SUBLANES:SUBLANES + tm, :] = h
        h1 = hbuf[SUBLANES - 1:SUBLANES - 1 + tm, :]
        h2 = hbuf[SUBLANES - 2:SUBLANES - 2 + tm, :]
        conv = cb_ref[...] + cw_ref[0:1, :] * h2 + cw_ref[1:2, :] * h1 + cw_ref[2:3, :] * h
        hbuf[0:SUBLANES, :] = h[tm - SUBLANES:tm, :]
        save_ref[7] = conv.astype(BF16)
        szb, _ = _silu(zbv)
        yb_ref[...] = (bbv * conv * szb).astype(BF16)

        @pl.when((g == HEADS - 1) & (i == nt - 1))
        def _():
            for blk in groups[HEADS]:
                wait_from_sibling(blk)
            for t in range(4):
                local_copy(t).wait()
            for t in range(4):
                n = w8 if t == 0 else r8
                own_all, fwd_all = fulls[t].at[pl.ds(0, 4 * n), :], fulls[t].at[pl.ds(0, 3 * n), :]
                for ref, sem in ((own_all, own_sems.at[t]), (fwd_all, fwd_sems.at[t])):
                    pltpu.make_async_remote_copy(src_ref=ref, dst_ref=ref, send_sem=sem, recv_sem=sem,
                                                 device_id=sibling, device_id_type=MESH).wait_send()

    vec = pl.BlockSpec((1, hd), lambda g, i: (0, g))
    hbm = pl.BlockSpec(memory_space=pl.ANY)
    return pl.pallas_call(
        body, name="mixer_fwd", grid=(HEADS, nt),
        in_specs=[pl.BlockSpec((tm, d), lambda g, i: (i, 0)), hbm, hbm, hbm, hbm,
                  vec, vec,
                  pl.BlockSpec((1, CHUNK, CHUNK), lambda g, i: (g, 0, 0)),
                  pl.BlockSpec((1, CHUNK, 1), lambda g, i: (g, 0, 0)),
                  pl.BlockSpec((3, hd), lambda g, i: (0, g)),
                  vec],
        out_specs=[pl.BlockSpec((N_SAVE, tm, hd), lambda g, i: (0, i, g)),
                   pl.BlockSpec((tm, hd), lambda g, i: (i, g)),
                   pl.BlockSpec((tm, hd), lambda g, i: (i, g)),
                   hbm, hbm, hbm, hbm],
        out_shape=[jax.ShapeDtypeStruct((N_SAVE, t_len, d), BF16),
                   jax.ShapeDtypeStruct((t_len, d), BF16),
                   jax.ShapeDtypeStruct((t_len, d), BF16),
                   jax.ShapeDtypeStruct((N_DEV * w8, d), BF16),
                   jax.ShapeDtypeStruct((d, d), BF16), jax.ShapeDtypeStruct((d, d), BF16),
                   jax.ShapeDtypeStruct((d, d), BF16)],
        scratch_shapes=[pltpu.VMEM((SUBLANES + tm, hd), F32), pltpu.VMEM((N_MIX, hd, d), BF16),
                        pltpu.SemaphoreType.DMA((n_blocks,)), pltpu.SemaphoreType.DMA((4,)),
                        pltpu.SemaphoreType.DMA((4,)), pltpu.SemaphoreType.DMA((4,)),
                        pltpu.SemaphoreType.DMA((N_MIX,))],
        compiler_params=_cparams(("arbitrary", "arbitrary")),
    )(x, *shards, ln_v_g, ln_v_b, w_s, b_s3, conv_w, conv_b)


def _merge_fwd(x, ya, yb, w_in, w_oa, w_ob, b_gate):
    t_len, d = x.shape
    tm = _tile(t_len, 512)
    tn = _tile(d, 512)
    nj = d // tn

    def body(x_ref, ya_ref, yb_ref, wga, wgb, woa, wob, bga, bgb, sa_ref, sb_ref, pa_ref, pb_ref, mg_ref):
        xt = x_ref[...].astype(BF16)
        sa = jax.nn.sigmoid(_dot_nt(xt, wga[...]) + bga[...])
        sb = jax.nn.sigmoid(_dot_nt(xt, wgb[...]) + bgb[...])
        pa = _dot(ya_ref[...], woa[...])
        pb = _dot(yb_ref[...], wob[...])
        sa_ref[...] = sa.astype(BF16)
        sb_ref[...] = sb.astype(BF16)
        pa_ref[...] = pa.astype(BF16)
        pb_ref[...] = pb.astype(BF16)
        mg_ref[...] = (sa * pa + sb * pb).astype(BF16)

    row = pl.BlockSpec((tm, d), lambda j, i: (i, 0))
    out = pl.BlockSpec((tm, tn), lambda j, i: (i, j))
    return pl.pallas_call(
        body, name="merge_fwd", grid=(nj, t_len // tm),
        in_specs=[row, row, row,
                  pl.BlockSpec((tn, d), lambda j, i: (7 * nj + j, 0)),
                  pl.BlockSpec((tn, d), lambda j, i: (8 * nj + j, 0)),
                  pl.BlockSpec((d, tn), lambda j, i: (0, j)),
                  pl.BlockSpec((d, tn), lambda j, i: (0, j)),
                  pl.BlockSpec((1, tn), lambda j, i: (0, j)),
                  pl.BlockSpec((1, tn), lambda j, i: (0, nj + j))],
        out_specs=[out] * 5,
        out_shape=[jax.ShapeDtypeStruct((t_len, d), BF16)] * 5,
        compiler_params=_cparams(("parallel", "arbitrary")),
    )(x, ya, yb, w_in, w_in, w_oa, w_ob, b_gate, b_gate)


def _out_ln_loss(merged, w_out, x, target, ln_g, ln_b):
    t_len, d = x.shape
    tm = _tile(t_len, 256)
    nt = t_len // tm

    def body(mg_ref, w_ref, x_ref, t_ref, g_ref, b_ref, dz_ref, gx_ref, xb_ref, glg_ref, glb_ref, ls_ref, a_g, a_b, a_l):
        i = pl.program_id(0)
        xb_ref[...] = x_ref[...].astype(BF16)

        @pl.when(i == 0)
        def _():
            a_g[...] = jnp.zeros_like(a_g)
            a_b[...] = jnp.zeros_like(a_b)
            a_l[...] = jnp.zeros_like(a_l)

        zres = DN_ALPHA * x_ref[...] + _dot(mg_ref[...], w_ref[...])
        mu = jnp.mean(zres, axis=-1, keepdims=True)
        dc = zres - mu
        var = jnp.mean(dc * dc, axis=-1, keepdims=True)
        rstd = lax.rsqrt(var + LN_EPS)
        xhat = dc * rstd
        g = g_ref[...]
        err = xhat * g + b_ref[...] - t_ref[...]
        dy = err * (1.0 / d)
        a_l[...] += _fold8(err * err)
        a_g[...] += _fold8(dy * xhat)
        a_b[...] += _fold8(dy)
        dxh = dy * g
        m1 = jnp.mean(dxh, axis=-1, keepdims=True)
        m2 = jnp.mean(dxh * xhat, axis=-1, keepdims=True)
        dz = rstd * (dxh - m1 - xhat * m2)
        dz_ref[...] = dz.astype(BF16)
        gx_ref[...] = DN_ALPHA * dz

        @pl.when(i == nt - 1)
        def _():
            glg_ref[...] = jnp.sum(a_g[...], axis=0, keepdims=True)
            glb_ref[...] = jnp.sum(a_b[...], axis=0, keepdims=True)
            ls_ref[...] = jnp.sum(a_l[...], axis=0, keepdims=True)

    row = pl.BlockSpec((tm, d), lambda i: (i, 0))
    vec = pl.BlockSpec((1, d), lambda i: (0, 0))
    return pl.pallas_call(
        body, name="out_ln_loss", grid=(nt,),
        in_specs=[row, pl.BlockSpec((d, d), lambda i: (0, 0)), row, row, vec, vec],
        out_specs=[row, row, row, vec, vec, vec],
        out_shape=[jax.ShapeDtypeStruct((t_len, d), BF16), jax.ShapeDtypeStruct((t_len, d), F32),
                   jax.ShapeDtypeStruct((t_len, d), BF16)] + [jax.ShapeDtypeStruct((1, d), F32)] * 3,
        scratch_shapes=[pltpu.VMEM((SUBLANES, d), F32)] * 3,
        compiler_params=_cparams(("arbitrary",)),
    )(merged, w_out, x, target, ln_g, ln_b)


def _merge_bwd(dz, w_out, sa, sb, pa, pb):
    t_len, d = dz.shape
    tm = _tile(t_len, 256)
    nt = t_len // tm

    def body(dz_ref, w_ref, sa_ref, sb_ref, pa_ref, pb_ref, da_ref, db_ref, dg_ref, ga_ref, gb_ref, acc_a, acc_b):
        i = pl.program_id(0)

        @pl.when(i == 0)
        def _():
            acc_a[...] = jnp.zeros_like(acc_a)
            acc_b[...] = jnp.zeros_like(acc_b)

        dm = _dot_nt(dz_ref[...], w_ref[...])
        sa = sa_ref[...].astype(F32)
        sb = sb_ref[...].astype(F32)
        da = dm * sa
        db = dm * sb
        da_ref[...] = da.astype(BF16)
        db_ref[...] = db.astype(BF16)
        dga = da * pa_ref[...].astype(F32) * (1.0 - sa)
        dgb = db * pb_ref[...].astype(F32) * (1.0 - sb)
        dg_ref[0] = dga.astype(BF16)
        dg_ref[1] = dgb.astype(BF16)
        acc_a[...] += _fold8(dga)
        acc_b[...] += _fold8(dgb)

        @pl.when(i == nt - 1)
        def _():
            ga_ref[...] = jnp.sum(acc_a[...], axis=0, keepdims=True)
            gb_ref[...] = jnp.sum(acc_b[...], axis=0, keepdims=True)

    row = pl.BlockSpec((tm, d), lambda i: (i, 0))
    vec = pl.BlockSpec((1, d), lambda i: (0, 0))
    return pl.pallas_call(
        body, name="merge_bwd", grid=(nt,),
        in_specs=[row, pl.BlockSpec((d, d), lambda i: (0, 0)), row, row, row, row],
        out_specs=[row, row, pl.BlockSpec((2, tm, d), lambda i: (0, i, 0)), vec, vec],
        out_shape=[jax.ShapeDtypeStruct((t_len, d), BF16)] * 2 + [jax.ShapeDtypeStruct((2, t_len, d), BF16)]
        + [jax.ShapeDtypeStruct((1, d), F32)] * 2,
        scratch_shapes=[pltpu.VMEM((SUBLANES, d), F32)] * 2,
        compiler_params=_cparams(("arbitrary",)),
    )(dz, w_out, sa, sb, pa, pb)


def _grad_w(a, b, name, exchange=None):
    t_len, m = a.shape
    n = b.shape[1]
    tm, tn, tk = _tile(m, MM_TILE), _tile(n, MM_TILE), _tile(t_len, MM_DEPTH_TOKENS)
    ni, nj, nk = m // tm, n // tn, t_len // tk

    def body(a_ref, b_ref, *rest):
        if exchange is None:
            o_ref, acc = rest
        else:
            src, o_ref, land, acc, send_sems, recv_sems = rest
        i, j, k = pl.program_id(0), pl.program_id(1), pl.program_id(2)

        def copies():
            x, y, c = lax.axis_index("x"), lax.axis_index("y"), lax.axis_index("c")
            w = exchange.shape[1] // N_DEV
            return [pltpu.make_async_remote_copy(
                src_ref=_shard_ref(src, "col", 2 * s + 1 - c, w), dst_ref=land.at[s],
                send_sem=send_sems.at[s], recv_sem=recv_sems.at[s],
                device_id=(x, y, 1 - c), device_id_type=MESH) for s in range(N_CHIP)]

        if exchange is not None:
            @pl.when((i == 0) & (j == 0) & (k == 0))
            def _():
                for cp in copies():
                    cp.start()

        @pl.when(k == 0)
        def _():
            acc[...] = jnp.zeros_like(acc)

        acc[...] += _dot_tn(a_ref[...], b_ref[...])

        @pl.when(k == nk - 1)
        def _():
            o_ref[...] = acc[...].astype(BF16)

        if exchange is not None:
            @pl.when((i == ni - 1) & (j == nj - 1) & (k == nk - 1))
            def _():
                for cp in copies():
                    cp.wait()

    in_specs = [pl.BlockSpec((tk, tm), lambda i, j, k: (k, i)), pl.BlockSpec((tk, tn), lambda i, j, k: (k, j))]
    out_specs = [pl.BlockSpec((tm, tn), lambda i, j, k: (i, j))]
    out_shape = [jax.ShapeDtypeStruct((m, n), BF16)]
    scratch = [pltpu.VMEM((tm, tn), F32)]
    args = [a, b]
    if exchange is not None:
        hbm = pl.BlockSpec(memory_space=pl.ANY)
        in_specs.append(hbm)
        out_specs.append(hbm)
        out_shape.append(jax.ShapeDtypeStruct((N_CHIP, exchange.shape[0], exchange.shape[1] // N_DEV), exchange.dtype))
        scratch += [pltpu.SemaphoreType.DMA((N_CHIP,)), pltpu.SemaphoreType.DMA((N_CHIP,))]
        args.append(exchange)
    outs = pl.pallas_call(
        body, name=name, grid=(ni, nj, nk),
        in_specs=in_specs, out_specs=out_specs, out_shape=out_shape, scratch_shapes=scratch,
        compiler_params=_cparams(("arbitrary", "arbitrary", "arbitrary")),
    )(*args)
    return outs[0] if exchange is None else (outs[0], outs[1])


def _mixer_bwd(da, db, w_oa, w_ob, saved, dgate, ln_v_g, ln_v_b, w_s, b_s3, conv_w):
    t_len, d = da.shape
    hd = d // HEADS
    tm = _tile(t_len, 512)
    nt = t_len // tm

    def body(da_ref, db_ref, woa, wob, sv, dgt, lng, lnb, ws_ref, bs_ref, cw_ref,
             dp_ref, gws_ref, gbs_ref, glg_ref, glb_ref, gcw_ref, gcb_ref,
             dbuf, carry, a_ws, a_bs, a_lg, a_lb, a_c0, a_c1, a_c2, a_cb):
        i = pl.program_id(1)

        @pl.when(i == 0)
        def _():
            carry[...] = jnp.zeros_like(carry)
            for a in (a_ws, a_bs, a_lg, a_lb, a_c0, a_c1, a_c2, a_cb):
                a[...] = jnp.zeros_like(a)

        dya_t = _dot_nt(da_ref[...], woa[...])
        dyb_t = _dot_nt(db_ref[...], wob[...])
        g = lng[...]
        beta = lnb[...]
        wm = jnp.where(_tril_mask(), ws_ref[0], 0.0).astype(BF16)
        bs = bs_ref[0]
        w0, w1, w2 = cw_ref[0:1, :], cw_ref[1:2, :], cw_ref[2:3, :]
        dbuf[tm:tm + SUBLANES, :] = carry[...]
        gws = a_ws[...]
        gbs = a_bs[...]
        lg, lb = a_lg[...], a_lb[...]
        c0, c1, c2, cb = a_c0[...], a_c1[...], a_c2[...], a_cb[...]
        for ck in reversed(range(tm // CHUNK)):
            r0 = ck * CHUNK
            rows = slice(r0, r0 + CHUNK)
            dya = dya_t[rows]
            u = sv[0, rows, :].astype(F32)
            v = sv[1, rows, :].astype(F32)
            z = sv[2, rows, :].astype(F32)
            gu, gu_grad = _gelu(u)
            gv, gv_grad = _gelu(v)
            mu = jnp.mean(gv, axis=-1, keepdims=True)
            dvc = gv - mu
            var = jnp.mean(dvc * dvc, axis=-1, keepdims=True)
            rstd = lax.rsqrt(var + LN_EPS)
            vhat = dvc * rstd
            vn = (vhat * g + beta).astype(BF16)
            sz, sz_grad = _silu(z)
            t1 = dya * sz
            dmixed = t1 * gu
            dmixed_b = dmixed.astype(BF16)
            mixed = _dot(wm, vn) + bs
            gws = gws + _dot_nt(dmixed_b, vn)
            gbs = gbs + dmixed
            dvn = _dot_tn(wm, dmixed_b)
            dp_ref[0, rows, :] = (t1 * mixed * gu_grad).astype(BF16)
            dp_ref[2, rows, :] = (dya * gu * mixed * sz_grad).astype(BF16)
            lg = lg + _fold8(dvn * vhat)
            lb = lb + _fold8(dvn)
            dvh = dvn * g
            m1 = jnp.mean(dvh, axis=-1, keepdims=True)
            m2 = jnp.mean(dvh * vhat, axis=-1, keepdims=True)
            dp_ref[1, rows, :] = (rstd * (dvh - m1 - vhat * m2) * gv_grad).astype(BF16)
            dyb = dyb_t[rows]
            xbv = sv[3, rows, :].astype(F32)
            cbv = sv[4, rows, :].astype(F32)
            bbv = sv[5, rows, :].astype(F32)
            zbv = sv[6, rows, :].astype(F32)
            conv = sv[7, rows, :].astype(F32)
            szb, szb_grad = _silu(zbv)
            dp_ref[5, rows, :] = (dyb * conv * szb).astype(BF16)
            dp_ref[6, rows, :] = (dyb * bbv * conv * szb_grad).astype(BF16)
            dconv = dyb * bbv * szb
            dbuf[r0:r0 + CHUNK, :] = dconv
            dc1 = dbuf[r0 + 1:r0 + 1 + CHUNK, :]
            dc2 = dbuf[r0 + 2:r0 + 2 + CHUNK, :]
            if ck == 0:
                carry[...] = dconv[0:SUBLANES, :]
            h = cbv * xbv
            c2 = c2 + _fold8(dconv * h)
            c1 = c1 + _fold8(dc1 * h)
            c0 = c0 + _fold8(dc2 * h)
            cb = cb + _fold8(dconv)
            dh = w2 * dconv + w1 * dc1 + w0 * dc2
            dp_ref[3, rows, :] = (dh * cbv).astype(BF16)
            dp_ref[4, rows, :] = (dh * xbv).astype(BF16)
            dp_ref[7, rows, :] = dgt[0, rows, :]
            dp_ref[8, rows, :] = dgt[1, rows, :]
        a_ws[...] = gws
        a_bs[...] = gbs
        a_lg[...], a_lb[...] = lg, lb
        a_c0[...], a_c1[...], a_c2[...], a_cb[...] = c0, c1, c2, cb

        @pl.when(i == nt - 1)
        def _():
            gws_ref[0] = jnp.where(_tril_mask(), a_ws[...], 0.0)
            gbs_ref[0] = jnp.sum(a_bs[...], axis=1, keepdims=True)
            glg_ref[...] = jnp.sum(a_lg[...], axis=0, keepdims=True)
            glb_ref[...] = jnp.sum(a_lb[...], axis=0, keepdims=True)
            gcw_ref[0:1, :] = jnp.sum(a_c0[...], axis=0, keepdims=True)
            gcw_ref[1:2, :] = jnp.sum(a_c1[...], axis=0, keepdims=True)
            gcw_ref[2:3, :] = jnp.sum(a_c2[...], axis=0, keepdims=True)
            gcb_ref[...] = jnp.sum(a_cb[...], axis=0, keepdims=True)

    def rev(i):
        return nt - 1 - i

    row = pl.BlockSpec((tm, d), lambda g, i: (rev(i), 0))
    wrow = pl.BlockSpec((hd, d), lambda g, i: (g, 0))
    vec = pl.BlockSpec((1, hd), lambda g, i: (0, g))
    acc8 = pltpu.VMEM((SUBLANES, hd), F32)
    return pl.pallas_call(
        body, name="mixer_bwd", grid=(HEADS, nt),
        in_specs=[row, row, wrow, wrow,
                  pl.BlockSpec((N_SAVE, tm, hd), lambda g, i: (0, rev(i), g)),
                  pl.BlockSpec((2, tm, hd), lambda g, i: (0, rev(i), g)),
                  vec, vec,
                  pl.BlockSpec((1, CHUNK, CHUNK), lambda g, i: (g, 0, 0)),
                  pl.BlockSpec((1, CHUNK, 1), lambda g, i: (g, 0, 0)),
                  pl.BlockSpec((3, hd), lambda g, i: (0, g))],
        out_specs=[pl.BlockSpec((N_IN, tm, hd), lambda g, i: (0, rev(i), g)),
                   pl.BlockSpec((1, CHUNK, CHUNK), lambda g, i: (g, 0, 0)),
                   pl.BlockSpec((1, CHUNK, 1), lambda g, i: (g, 0, 0)),
                   vec, vec,
                   pl.BlockSpec((3, hd), lambda g, i: (0, g)),
                   vec],
        out_shape=[jax.ShapeDtypeStruct((N_IN, t_len, d), BF16),
                   jax.ShapeDtypeStruct((HEADS, CHUNK, CHUNK), F32),
                   jax.ShapeDtypeStruct((HEADS, CHUNK, 1), F32),
                   jax.ShapeDtypeStruct((1, d), F32), jax.ShapeDtypeStruct((1, d), F32),
                   jax.ShapeDtypeStruct((3, d), F32), jax.ShapeDtypeStruct((1, d), F32)],
        scratch_shapes=[pltpu.VMEM((tm + SUBLANES, hd), F32), acc8,
                        pltpu.VMEM((CHUNK, CHUNK), F32), pltpu.VMEM((CHUNK, hd), F32),
                        acc8, acc8, acc8, acc8, acc8, acc8],
        compiler_params=_cparams(("parallel", "arbitrary")),
    )(da, db, w_oa, w_ob, saved, dgate, ln_v_g, ln_v_b, w_s, b_s3, conv_w)


def _grad_w_in(xb, dp):
    t_len, d = xb.shape
    tm, tn, tk = _tile(d, MM_TILE), _tile(d, MM_TILE), _tile(t_len, MM_DEPTH_TOKENS)
    nj = d // tn
    nk = t_len // tk

    def body(a_ref, b_ref, o_ref, acc):
        k = pl.program_id(2)

        @pl.when(k == 0)
        def _():
            acc[...] = jnp.zeros_like(acc)

        acc[...] += _dot_tn(a_ref[...], b_ref[...])

        @pl.when(k == nk - 1)
        def _():
            o_ref[...] = acc[...].astype(BF16)

    return pl.pallas_call(
        body, name="grad_w_in", grid=(d // tm, N_IN * nj, nk),
        in_specs=[pl.BlockSpec((tk, tm), lambda i, j, k: (k, i)),
                  pl.BlockSpec((None, tk, tn), lambda i, j, k: (j // nj, k, j % nj))],
        out_specs=pl.BlockSpec((tm, tn), lambda i, j, k: (i, j)),
        out_shape=jax.ShapeDtypeStruct((d, N_IN * d), BF16),
        scratch_shapes=[pltpu.VMEM((tm, tn), F32)],
        compiler_params=_cparams(("parallel", "parallel", "arbitrary")),
    )(xb, dp)


def _grad_x_tail(dp, w_in, partial, slots):
    _, t_len, d = dp.shape
    lo, hi = slots
    tm, tn, tk = _tile(t_len, MM_TILE), _tile(d, MM_TILE), _tile(d, MM_DEPTH)
    nkb = d // tk
    nk = (hi - lo) * nkb

    def body(a_ref, b_ref, r_ref, o_ref, acc):
        k = pl.program_id(2)

        @pl.when(k == 0)
        def _():
            acc[...] = r_ref[...]

        acc[...] += _dot(a_ref[...], b_ref[...])

        @pl.when(k == nk - 1)
        def _():
            o_ref[...] = acc[...]

    return pl.pallas_call(
        body, name="grad_x_tail", grid=(t_len // tm, d // tn, nk),
        in_specs=[pl.BlockSpec((None, tm, tk), lambda i, j, k: (lo + k // nkb, i, k % nkb)),
                  pl.BlockSpec((tk, tn), lambda i, j, k: (lo * nkb + k, j)),
                  pl.BlockSpec((tm, tn), lambda i, j, k: (i, j))],
        out_specs=pl.BlockSpec((tm, tn), lambda i, j, k: (i, j)),
        out_shape=jax.ShapeDtypeStruct((t_len, d), F32),
        scratch_shapes=[pltpu.VMEM((tm, tn), F32)],
        compiler_params=_cparams(("parallel", "parallel", "arbitrary")),
    )(dp, w_in, partial)


def _grad_x(dp, w_in, gx_direct, parts, packed, slots):
    _, t_len, d = dp.shape
    tm, tn, tk = _tile(t_len, MM_TILE), _tile(d, MM_TILE), _tile(d, MM_DEPTH)
    nkb = d // tk
    lo, hi = slots
    nk = (hi - lo) * nkb
    ni, nj = t_len // tm, d // tn
    n_parts = len(parts)

    def body(a_ref, b_ref, r_ref, *rest):
        srcs = rest[:n_parts]
        pk_ref = rest[n_parts]
        o_ref = rest[n_parts + 1]
        lands = rest[n_parts + 2:2 * n_parts + 2]
        gath = rest[2 * n_parts + 2]
        acc, send_sems, recv_sems, pk_send, pk_recv, pk_local = rest[2 * n_parts + 3:]
        i, j, k = pl.program_id(0), pl.program_id(1), pl.program_id(2)
        x, y, c = lax.axis_index("x"), lax.axis_index("y"), lax.axis_index("c")
        my_chip = 2 * x + y
        me = 4 * x + 2 * y + c
        chips = [(1 - x, y), (x, 1 - y), (1 - x, 1 - y)]

        def part_copy(t, n):
            px, py = chips[n]
            return pltpu.make_async_remote_copy(
                src_ref=srcs[t].at[2 * px + py], dst_ref=lands[t].at[my_chip],
                send_sem=send_sems.at[3 * t + n], recv_sem=recv_sems.at[3 * t + n],
                device_id=(px, py, c), device_id_type=MESH)

        def part_landing(t, n):
            px, py = chips[n]
            return pltpu.make_async_remote_copy(
                src_ref=srcs[t].at[my_chip], dst_ref=lands[t].at[2 * px + py],
                send_sem=send_sems.at[3 * t + n], recv_sem=recv_sems.at[3 * t + n],
                device_id=(px, py, c), device_id_type=MESH)

        def pk_copy(s):
            return pltpu.make_async_remote_copy(
                src_ref=pk_ref, dst_ref=gath.at[me], send_sem=pk_send, recv_sem=pk_recv.at[me],
                device_id=(s // 4, (s // 2) % 2, s % 2), device_id_type=MESH)

        @pl.when((i == 0) & (j == 0) & (k == 0))
        def _():
            for t in range(n_parts):
                for n in range(3):
                    part_copy(t, n).start()
            pltpu.make_async_copy(pk_ref, gath.at[me], pk_local).start()
            for s in range(N_DEV):
                @pl.when(s != me)
                def _(s=s):
                    pk_copy(s).start()

        @pl.when(k == 0)
        def _():
            acc[...] = r_ref[...]

        acc[...] += _dot(a_ref[...], b_ref[...])

        @pl.when(k == nk - 1)
        def _():
            o_ref[...] = acc[...]

        @pl.when((i == ni - 1) & (j == nj - 1) & (k == nk - 1))
        def _():
            for t in range(n_parts):
                for n in range(3):
                    part_landing(t, n).wait_recv()
            for t in range(n_parts):
                for n in range(3):
                    part_copy(t, n).wait_send()
            for s in range(N_DEV):
                @pl.when(s != me)
                def _(s=s):
                    pltpu.make_async_remote_copy(
                        src_ref=pk_ref, dst_ref=gath.at[s], send_sem=pk_send, recv_sem=pk_recv.at[s],
                        device_id=(s // 4, (s // 2) % 2, s % 2), device_id_type=MESH).wait_recv()
            seven = gath.at[pl.ds(0, N_DEV - 1)]
            pltpu.make_async_remote_copy(src_ref=seven, dst_ref=seven, send_sem=pk_send, recv_sem=pk_send,
                                         device_id=(x, y, 1 - c), device_id_type=MESH).wait_send()
            pltpu.make_async_copy(pk_ref, gath.at[me], pk_local).wait()

    hbm = pl.BlockSpec(memory_space=pl.ANY)
    outs = pl.pallas_call(
        body, name="grad_x", grid=(ni, nj, nk),
        in_specs=[pl.BlockSpec((None, tm, tk), lambda i, j, k: (lo + k // nkb, i, k % nkb)),
                  pl.BlockSpec((tk, tn), lambda i, j, k: (lo * nkb + k, j)),
                  pl.BlockSpec((tm, tn), lambda i, j, k: (i, j))] + [hbm] * (n_parts + 1),
        out_specs=[pl.BlockSpec((tm, tn), lambda i, j, k: (i, j))] + [hbm] * (n_parts + 1),
        out_shape=[jax.ShapeDtypeStruct((t_len, d), F32)] + [jax.ShapeDtypeStruct(p.shape, p.dtype) for p in parts]
        + [jax.ShapeDtypeStruct((N_DEV,) + packed.shape, packed.dtype)],
        scratch_shapes=[pltpu.VMEM((tm, tn), F32),
                        pltpu.SemaphoreType.DMA((3 * n_parts,)), pltpu.SemaphoreType.DMA((3 * n_parts,)),
                        pltpu.SemaphoreType.DMA(()), pltpu.SemaphoreType.DMA((N_DEV,)), pltpu.SemaphoreType.DMA(())],
        compiler_params=_cparams(("arbitrary", "arbitrary", "arbitrary")),
    )(dp, w_in, gx_direct, *parts, packed)
    return outs[0], list(outs[1:1 + n_parts]), outs[1 + n_parts]


def kernel(x, w_in, b_gate, ln_v_g, ln_v_b, w_s, b_s, conv_w, conv_b, w_oa, w_ob, w_out, ln_g, ln_b, loss_target, m_w_in, m_b_gate, m_ln_v_g, m_ln_v_b, m_w_s, m_b_s, m_conv_w, m_conv_b, m_w_oa, m_w_ob, m_w_out, m_ln_g, m_ln_b, v_w_in, v_b_gate, v_ln_v_g, v_ln_v_b, v_w_s, v_b_s, v_conv_w, v_conv_b, v_w_oa, v_w_ob, v_w_out, v_ln_g, v_ln_b):
    _, t_len, d = x.shape
    assert d % (HEADS * 128) == 0 and t_len % CHUNK == 0 and w_in.shape[2] * N_DEV == N_IN * d
    x2 = x[0]
    tgt2 = loss_target[0]
    c_arr = lax.axis_index("c").astype(jnp.int32).reshape(1)
    chip_arr = (2 * lax.axis_index("x") + lax.axis_index("y")).astype(jnp.int32).reshape(1)
    dev = 4 * lax.axis_index("x") + 2 * lax.axis_index("y") + lax.axis_index("c")

    shards = [_cast_transposed(w_in[0], "cast_w_in"), _cast_bf16(w_oa[0], "cast_w_oa"),
              _cast_bf16(w_ob[0], "cast_w_ob"), _cast_bf16(w_out[0], "cast_w_out")]
    (conv_w_g,) = _all_gather([conv_w[0]], ["lead"], "gather_conv_w", vmem=True)
    conv_w_f = jnp.transpose(conv_w_g, (1, 0, 2)).reshape(3, d)
    w_s3 = w_s[0]
    b_s3 = b_s[0].reshape(HEADS, CHUNK, 1)

    saved, ya, yb, w_in_f, w_oa_f, w_ob_f, w_out_f = _mixer_fwd(
        x2, shards, ln_v_g, ln_v_b, w_s3, b_s3, conv_w_f, conv_b)
    sa, sb, pa, pb, merged = _merge_fwd(x2, ya, yb, w_in_f, w_oa_f, w_ob_f, b_gate)
    dz, gx_direct, xb, g_ln_g, g_ln_b, err2 = _out_ln_loss(merged, w_out_f, x2, tgt2, ln_g, ln_b)
    loss = lax.psum(0.5 * jnp.sum(err2) / d, ("x", "y", "c"))

    da, db, dgate, g_bga, g_bgb = _merge_bwd(dz, w_out_f, sa, sb, pa, pb)
    dp, g_ws, g_bs, g_lvg, g_lvb, g_cw, g_cb = _mixer_bwd(
        da, db, w_oa_f, w_ob_f, saved, dgate, ln_v_g, ln_v_b, w_s3, b_s3, conv_w_f)
    gw_in = _grad_w_in(xb, dp)

    gw_out, land_in = _grad_w(merged, dz, "grad_w_out", exchange=gw_in)
    part_in = _pair_sum(gw_in, land_in, "col", c_arr, "grad_pair_sum_0")
    gw_oa = _grad_w(ya, da, "grad_w_oa")
    gw_ob = _grad_w(yb, db, "grad_w_ob")
    rows = [gw_oa, gw_ob, gw_out]
    lands = _pair_exchange(rows, ["row"] * 3, "grad_pair_exchange")
    parts = [part_in] + [_pair_sum(g, l, "row", c_arr, "grad_pair_sum_%d" % (n + 1))
                         for n, (g, l) in enumerate(zip(rows, lands))]
    pieces = [jnp.concatenate([g_bga, g_bgb], axis=1), g_lvg, g_lvb, g_ws, g_bs, g_cw, g_cb, g_ln_g, g_ln_b]
    sizes = [p.size for p in pieces]
    packed = jnp.concatenate([p.reshape(-1, 128) for p in pieces], axis=0)
    gx_part, lands2, gathered = _grad_x(dp, w_in_f, gx_direct, parts, packed, (0, GRAD_X_COMM_BLOCKS))
    grad_x = _grad_x_tail(dp, w_in_f, gx_part, (GRAD_X_COMM_BLOCKS, N_IN))[None]

    big = []
    for n, (w, m, v) in enumerate([(w_in, m_w_in, v_w_in), (w_oa, m_w_oa, v_w_oa), (w_ob, m_w_ob, v_w_ob),
                                   (w_out, m_w_out, v_w_out)]):
        big.append([o[None] for o in _sum_adam(parts[n], lands2[n], w[0], m[0], v[0], chip_arr, "sum_adam_%d" % n)])
    (g_w_in, d_w_in, nm_w_in, nv_w_in), (g_w_oa, d_w_oa, nm_w_oa, nv_w_oa), \
        (g_w_ob, d_w_ob, nm_w_ob, nv_w_ob), (g_w_out, d_w_out, nm_w_out, nv_w_out) = big

    total = _small_sum(gathered, "sum_small_grads")
    offs = [0]
    for s in sizes:
        offs.append(offs[-1] + s // 128)
    unpacked = [total[offs[n]:offs[n + 1]] for n in range(len(pieces))]
    g_b_gate = unpacked[0].reshape(b_gate.shape)
    g_ln_v_g = unpacked[1].reshape(ln_v_g.shape)
    g_ln_v_b = unpacked[2].reshape(ln_v_b.shape)
    g_w_s = unpacked[3].reshape(w_s.shape)
    g_b_s = unpacked[4].reshape(b_s.shape)
    g_conv_w = lax.dynamic_slice_in_dim(unpacked[5].reshape(3, d), dev * (d // N_DEV), d // N_DEV, axis=1)[None]
    g_conv_b = unpacked[6].reshape(conv_b.shape)
    g_ln_g2 = unpacked[7].reshape(ln_g.shape)
    g_ln_b2 = unpacked[8].reshape(ln_b.shape)

    small_w = [b_gate, ln_v_g, ln_v_b, w_s, b_s, conv_w, conv_b, ln_g, ln_b]
    small_g = [g_b_gate, g_ln_v_g, g_ln_v_b, g_w_s, g_b_s, g_conv_w, g_conv_b, g_ln_g2, g_ln_b2]
    small_m = [m_b_gate, m_ln_v_g, m_ln_v_b, m_w_s, m_b_s, m_conv_w, m_conv_b, m_ln_g, m_ln_b]
    small_v = [v_b_gate, v_ln_v_g, v_ln_v_b, v_w_s, v_b_s, v_conv_w, v_conv_b, v_ln_g, v_ln_b]

    def flat(a):
        return a.reshape(-1, a.shape[-1])

    res = _small_adam([flat(a) for a in small_w], [flat(a) for a in small_g], [flat(a) for a in small_m],
                      [flat(a) for a in small_v], "adam_small")
    ns = len(small_w)
    d_s = [res[n].reshape(small_w[n].shape) for n in range(ns)]
    nm_s = [res[ns + n].reshape(small_w[n].shape) for n in range(ns)]
    nv_s = [res[2 * ns + n].reshape(small_w[n].shape) for n in range(ns)]

    def ordered(first, small, oa, ob, out):
        return [first] + small[:7] + [oa, ob, out] + small[7:]

    return (loss, grad_x,
            *ordered(g_w_in, small_g, g_w_oa, g_w_ob, g_w_out),
            *ordered(d_w_in, d_s, d_w_oa, d_w_ob, d_w_out),
            *ordered(nm_w_in, nm_s, nm_w_oa, nm_w_ob, nm_w_out),
            *ordered(nv_w_in, nv_s, nv_w_oa, nv_w_ob, nv_w_out))
```

```python
import functools

import jax
import jax.numpy as jnp
from jax import lax
from jax.experimental import pallas as pl
from jax.experimental.pallas import tpu as pltpu

F32 = jnp.float32
BF16 = jnp.bfloat16
MESH = pl.DeviceIdType.MESH

N_DEV = 8
N_CHIP = 4
HEADS = 8
CHUNK = 128
N_IN = 9
N_MIX = 7
N_SAVE = 8
LN_EPS = 1e-5
DN_ALPHA = 2.0 ** 0.25
ADAM_LR = 0.001
ADAM_B1 = 0.9
ADAM_B2 = 0.999
ADAM_EPS = 1e-08
ADAM_WD = 0.01
ADAM_STEP = 10
GELU_C0 = 0.7978845608028654
GELU_C1 = 0.044715
SUBLANES = 8
FIRST_SENDS = 4
SENDS_PER_GROUP = 2
MM_TILE = 1024
MM_DEPTH = 2048
MM_DEPTH_TOKENS = 4096
GRAD_X_COMM_BLOCKS = 8
VMEM_LIMIT = 56 << 20


def _cparams(sem):
    return pltpu.CompilerParams(dimension_semantics=sem, vmem_limit_bytes=VMEM_LIMIT)


def _tile(n, want):
    t = min(n, want)
    while n % t:
        t //= 2
    return t


def _gelu(u):
    u2 = u * u
    t = jnp.tanh(u * (GELU_C0 + (GELU_C0 * GELU_C1) * u2))
    hp = 0.5 * t + 0.5
    grad = hp + (0.5 * u) * (1.0 - t * t) * (GELU_C0 + (3.0 * GELU_C0 * GELU_C1) * u2)
    return u * hp, grad


def _silu(z):
    s = jax.nn.sigmoid(z)
    sil = z * s
    return sil, s + sil * (1.0 - s)


def _fold8(a):
    return jnp.sum(a.reshape(a.shape[0] // SUBLANES, SUBLANES, a.shape[1]), axis=0)


def _dot(a, b):
    return jnp.dot(a, b, preferred_element_type=F32)


def _dot_nt(a, b):
    return lax.dot_general(a, b, (((1,), (1,)), ((), ())), preferred_element_type=F32)


def _dot_tn(a, b):
    return lax.dot_general(a, b, (((0,), (0,)), ((), ())), preferred_element_type=F32)


def _tril_mask():
    r = lax.broadcasted_iota(jnp.int32, (CHUNK, CHUNK), 0)
    c = lax.broadcasted_iota(jnp.int32, (CHUNK, CHUNK), 1)
    return c <= r


def _cast_bf16(a, name):
    rows, cols = a.shape
    rb = _tile(rows, 256)

    def body(a_ref, o_ref):
        o_ref[...] = a_ref[...].astype(BF16)

    return pl.pallas_call(
        body, name=name, grid=(rows // rb,),
        in_specs=[pl.BlockSpec((rb, cols), lambda i: (i, 0))],
        out_specs=pl.BlockSpec((rb, cols), lambda i: (i, 0)),
        out_shape=jax.ShapeDtypeStruct((rows, cols), BF16),
        compiler_params=_cparams(("parallel",)),
    )(a)


def _cast_transposed(a, name):
    rows, cols = a.shape
    tb = _tile(rows, 2048)
    tc = _tile(cols, 256)

    def body(a_ref, o_ref):
        o_ref[...] = a_ref[...].T.astype(BF16)

    return pl.pallas_call(
        body, name=name, grid=(cols // tc, rows // tb),
        in_specs=[pl.BlockSpec((tb, tc), lambda j, i: (i, j))],
        out_specs=pl.BlockSpec((tc, tb), lambda j, i: (j, i)),
        out_shape=jax.ShapeDtypeStruct((cols, rows), BF16),
        compiler_params=_cparams(("parallel", "parallel")),
    )(a)


def _shard_ref(full, kind, s, n):
    if kind == "col":
        return full.at[:, pl.ds(pl.multiple_of(s * n, 128), n)]
    if kind == "row":
        return full.at[pl.ds(pl.multiple_of(s * n, SUBLANES), n), :]
    return full.at[s]


def _all_gather(shards, kinds, name, vmem):
    nt = len(shards)
    out_shapes = []
    for a, kind in zip(shards, kinds):
        if kind == "col":
            out_shapes.append(jax.ShapeDtypeStruct((a.shape[0], N_DEV * a.shape[1]), a.dtype))
        elif kind == "row":
            out_shapes.append(jax.ShapeDtypeStruct((N_DEV * a.shape[0], a.shape[1]), a.dtype))
        else:
            out_shapes.append(jax.ShapeDtypeStruct((N_DEV,) + a.shape, a.dtype))

    def body(*refs):
        srcs, fulls = refs[:nt], refs[nt:2 * nt]
        send_sems, recv_sems, local_sems = refs[2 * nt:]
        x, y, c = lax.axis_index("x"), lax.axis_index("y"), lax.axis_index("c")
        sibling = (x, y, 1 - c)
        chips = [(1 - x, y), (x, 1 - y), (1 - x, 1 - y)]

        def dev(px, py, pc):
            return 4 * px + 2 * py + pc

        def region(t, s):
            a, kind = shards[t], kinds[t]
            n = a.shape[1] if kind == "col" else a.shape[0]
            return _shard_ref(fulls[t], kind, s, n)

        def copy(t, k, block, to, own=False):
            return pltpu.make_async_remote_copy(
                src_ref=srcs[t] if own else region(t, block), dst_ref=region(t, block),
                send_sem=send_sems.at[7 * t + k], recv_sem=recv_sems.at[7 * t + k],
                device_id=to, device_id_type=MESH)

        me = dev(x, y, c)
        started = []
        for t in range(nt):
            mine = pltpu.make_async_copy(srcs[t], region(t, me), local_sems.at[t])
            mine.start()
            started.append(mine)
        first = []
        for t in range(nt):
            first.append(copy(t, 0, me, sibling, own=True))
            for j, chip in enumerate(chips):
                first.append(copy(t, 1 + j, me, (*chip, c), own=True))
        for cp in first:
            cp.start()
        passed = []
        for t in range(nt):
            for j, chip in enumerate(chips):
                blk = dev(*chip, c)
                copy(t, 1 + j, blk, sibling).wait_recv()
                fwd = copy(t, 4 + j, blk, sibling)
                fwd.start()
                passed.append(fwd)
        for t in range(nt):
            copy(t, 0, dev(x, y, 1 - c), sibling).wait_recv()
            for j, chip in enumerate(chips):
                copy(t, 4 + j, dev(*chip, 1 - c), sibling).wait_recv()
        for cp in first + passed:
            cp.wait_send()
        for mine in started:
            mine.wait()

    space = pltpu.VMEM if vmem else pl.ANY
    return pl.pallas_call(
        body, name=name,
        in_specs=[pl.BlockSpec(memory_space=space)] * nt,
        out_specs=[pl.BlockSpec(memory_space=space)] * nt,
        out_shape=out_shapes,
        scratch_shapes=[pltpu.SemaphoreType.DMA((7 * nt,)), pltpu.SemaphoreType.DMA((7 * nt,)),
                        pltpu.SemaphoreType.DMA((nt,))],
        compiler_params=pltpu.CompilerParams(vmem_limit_bytes=VMEM_LIMIT, has_side_effects=True),
    )(*shards)


def _pair_exchange(grads, kinds, name):
    nt = len(grads)
    shard_shapes = []
    for g, kind in zip(grads, kinds):
        shard_shapes.append((g.shape[0], g.shape[1] // N_DEV) if kind == "col" else (g.shape[0] // N_DEV, g.shape[1]))

    def body(*refs):
        srcs, lands = refs[:nt], refs[nt:2 * nt]
        send_sems, recv_sems = refs[2 * nt:]
        x, y, c = lax.axis_index("x"), lax.axis_index("y"), lax.axis_index("c")
        copies = []
        for t in range(nt):
            n = shard_shapes[t][1] if kinds[t] == "col" else shard_shapes[t][0]
            for k in range(N_CHIP):
                cp = pltpu.make_async_remote_copy(
                    src_ref=_shard_ref(srcs[t], kinds[t], 2 * k + 1 - c, n), dst_ref=lands[t].at[k],
                    send_sem=send_sems.at[N_CHIP * t + k], recv_sem=recv_sems.at[N_CHIP * t + k],
                    device_id=(x, y, 1 - c), device_id_type=MESH)
                cp.start()
                copies.append(cp)
        for cp in copies:
            cp.wait()

    return pl.pallas_call(
        body, name=name,
        in_specs=[pl.BlockSpec(memory_space=pl.ANY)] * nt,
        out_specs=[pl.BlockSpec(memory_space=pl.ANY)] * nt,
        out_shape=[jax.ShapeDtypeStruct((N_CHIP,) + s, g.dtype) for s, g in zip(shard_shapes, grads)],
        scratch_shapes=[pltpu.SemaphoreType.DMA((N_CHIP * nt,)), pltpu.SemaphoreType.DMA((N_CHIP * nt,))],
        compiler_params=pltpu.CompilerParams(has_side_effects=True),
    )(*grads)


def _pair_sum(grad, land, kind, c_arr, name):
    _, r, w = land.shape
    rb = _tile(r, 256)
    nrb = r // rb

    def body(c_ref, g_ref, l_ref, o_ref):
        o_ref[...] = (g_ref[...].astype(F32) + l_ref[...].astype(F32)).astype(o_ref.dtype)

    if kind == "col":
        g_spec = pl.BlockSpec((rb, w), lambda k, i, c: (i, 2 * k + c[0]))
    else:
        g_spec = pl.BlockSpec((rb, w), lambda k, i, c: ((2 * k + c[0]) * nrb + i, 0))
    return pl.pallas_call(
        body, name=name,
        grid_spec=pltpu.PrefetchScalarGridSpec(
            num_scalar_prefetch=1, grid=(N_CHIP, nrb),
            in_specs=[g_spec, pl.BlockSpec((None, rb, w), lambda k, i, c: (k, i, 0))],
            out_specs=pl.BlockSpec((None, rb, w), lambda k, i, c: (k, i, 0))),
        out_shape=jax.ShapeDtypeStruct(land.shape, BF16),
        compiler_params=_cparams(("parallel", "parallel")),
    )(c_arr, grad, land)


def _adam(w, g, m, v):
    m = ADAM_B1 * m + (1.0 - ADAM_B1) * g
    v = ADAM_B2 * v + (1.0 - ADAM_B2) * jnp.square(g)
    m_hat = m / (1.0 - ADAM_B1 ** ADAM_STEP)
    v_hat = v / (1.0 - ADAM_B2 ** ADAM_STEP)
    delta = -ADAM_LR * (m_hat / (jnp.sqrt(v_hat) + ADAM_EPS) + ADAM_WD * w)
    return delta, m, v


def _sum_adam(part, land, w, m, v, chip_arr, name):
    r, wd = w.shape
    rb = _tile(r, 128)

    def body(k_ref, own, r1, r2, r3, w_ref, m_ref, v_ref, g_out, d_out, m_out, v_out):
        g = own[...].astype(F32) + r1[...].astype(F32) + r2[...].astype(F32) + r3[...].astype(F32)
        d, mn, vn = _adam(w_ref[...], g, m_ref[...], v_ref[...])
        g_out[...] = g
        d_out[...] = d
        m_out[...] = mn
        v_out[...] = vn

    def slot(off):
        return pl.BlockSpec((None, rb, wd), lambda i, k: ((k[0] + off) % N_CHIP, i, 0))

    plain = pl.BlockSpec((rb, wd), lambda i, k: (i, 0))
    return pl.pallas_call(
        body, name=name,
        grid_spec=pltpu.PrefetchScalarGridSpec(
            num_scalar_prefetch=1, grid=(r // rb,),
            in_specs=[slot(0), slot(1), slot(2), slot(3), plain, plain, plain],
            out_specs=[plain] * 4),
        out_shape=[jax.ShapeDtypeStruct(w.shape, F32)] * 4,
        compiler_params=_cparams(("parallel",)),
    )(chip_arr, part, land, land, land, w, m, v)


def _small_sum(gathered, name):
    _, r, w = gathered.shape

    def body(g_ref, o_ref):
        acc = g_ref[0]
        for d in range(1, N_DEV):
            acc = acc + g_ref[d]
        o_ref[...] = acc

    return pl.pallas_call(body, name=name, out_shape=jax.ShapeDtypeStruct((r, w), F32))(gathered)


def _small_adam(ws, gs, ms, vs, name):
    n = len(ws)

    def body(*refs):
        ins, outs = refs[:4 * n], refs[4 * n:]
        for t in range(n):
            d, mn, vn = _adam(ins[t][...], ins[n + t][...], ins[2 * n + t][...], ins[3 * n + t][...])
            outs[t][...] = d
            outs[n + t][...] = mn
            outs[2 * n + t][...] = vn

    shapes = [jax.ShapeDtypeStruct(w.shape, F32) for w in ws]
    return pl.pallas_call(body, name=name, out_shape=shapes * 3)(*ws, *gs, *ms, *vs)


def _mixer_fwd(x, shards, ln_v_g, ln_v_b, w_s, b_s3, conv_w, conv_b):
    t_len, d = x.shape
    hd = d // HEADS
    tm = _tile(t_len, 512)
    nt = t_len // tm
    assert nt >= 2
    w8 = shards[0].shape[0]
    bps = w8 // hd
    r8 = shards[1].shape[0]
    nq = N_IN * HEADS
    n_blocks = nq + 3 * N_DEV
    groups = [[("in", HEADS * b + k) for b in range(N_MIX)] for k in range(HEADS)]
    groups.append([("in", q) for q in range(N_MIX * HEADS, nq)] + [(t, s) for t in (1, 2, 3) for s in range(N_DEV)])

    def owner_of(blk):
        return blk[1] // bps if blk[0] == "in" else blk[1]

    send_step = {}
    for s in range(N_DEV):
        mine = [blk for grp in groups for blk in grp if owner_of(blk) == s]
        for pos, blk in enumerate(mine):
            send_step[blk] = 0 if pos < FIRST_SENDS else 1 + (pos - FIRST_SENDS) // SENDS_PER_GROUP
    assert max(send_step.values()) < HEADS
    tail_pass_step = nt - 6 if nt >= 8 else nt - 2

    def body(x_ref, sh_in, sh_oa, sh_ob, sh_out, lng, lnb, ws_ref, bs_ref, cw_ref, cb_ref,
             save_ref, ya_ref, yb_ref, f_in, f_oa, f_ob, f_out,
             hbuf, wbuf, recv_sems, own_sems, fwd_sems, local_sems, load_sems):
        g = pl.program_id(0)
        i = pl.program_id(1)
        x, y, c = lax.axis_index("x"), lax.axis_index("y"), lax.axis_index("c")
        sibling = (x, y, 1 - c)
        chips = [(1 - x, y), (x, 1 - y), (1 - x, 1 - y)]
        shard_refs = (sh_in, sh_oa, sh_ob, sh_out)
        fulls = (f_in, f_oa, f_ob, f_out)

        def tensor(blk):
            return 0 if blk[0] == "in" else blk[0]

        def owner(blk):
            s = owner_of(blk)
            return s // 4, (s // 2) % 2, s % 2

        def bid(blk):
            return blk[1] if blk[0] == "in" else nq + (blk[0] - 1) * N_DEV + blk[1]

        def region(blk):
            if blk[0] == "in":
                return f_in.at[pl.ds(blk[1] * hd, hd), :]
            return fulls[blk[0]].at[pl.ds(blk[1] * r8, r8), :]

        def own_src(blk):
            if blk[0] == "in":
                return sh_in.at[pl.ds((blk[1] % bps) * hd, hd), :]
            return shard_refs[blk[0]]

        def rcopy(blk, to, send_sem, own):
            return pltpu.make_async_remote_copy(
                src_ref=own_src(blk) if own else region(blk), dst_ref=region(blk),
                send_sem=send_sem, recv_sem=recv_sems.at[bid(blk)], device_id=to, device_id_type=MESH)

        def send_own(blk):
            ox, oy, oc = owner(blk)

            @pl.when((x == ox) & (y == oy) & (c == oc))
            def _():
                rcopy(blk, sibling, own_sems.at[tensor(blk)], True).start()
                for chip in chips:
                    rcopy(blk, (*chip, c), own_sems.at[tensor(blk)], True).start()

        def pass_on(blk):
            ox, oy, oc = owner(blk)

            @pl.when(((x != ox) | (y != oy)) & (c == oc))
            def _():
                rcopy(blk, sibling, fwd_sems.at[tensor(blk)], False).wait_recv()
                rcopy(blk, sibling, fwd_sems.at[tensor(blk)], False).start()

        def wait_from_sibling(blk):
            @pl.when(c != owner(blk)[2])
            def _():
                rcopy(blk, sibling, fwd_sems.at[tensor(blk)], False).wait_recv()

        def local_copy(t):
            me = 4 * x + 2 * y + c
            return pltpu.make_async_copy(shard_refs[t], _shard_ref(fulls[t], "row", me, w8 if t == 0 else r8),
                                         local_sems.at[t])

        first = (g == 0) & (i == 0)

        @pl.when(first)
        def _():
            for t in range(4):
                local_copy(t).start()
            for grp in groups:
                for blk in grp:
                    if send_step[blk] == 0:
                        send_own(blk)
            for blk in groups[0]:
                pass_on(blk)

        @pl.when(i == nt - 2)
        def _():
            for k in range(HEADS - 1):
                @pl.when(g == k)
                def _(k=k):
                    for blk in groups[k + 1]:
                        pass_on(blk)

        @pl.when((g == HEADS - 1) & (i == tail_pass_step))
        def _():
            for blk in groups[HEADS]:
                pass_on(blk)

        @pl.when(i == 0)
        def _():
            for k in range(HEADS):
                @pl.when(g == k)
                def _(k=k):
                    if k > 0:
                        for grp in groups:
                            for blk in grp:
                                if send_step[blk] == k:
                                    send_own(blk)
                    for blk in groups[k]:
                        wait_from_sibling(blk)
                    for b, blk in enumerate(groups[k]):
                        ox, oy, oc = owner(blk)
                        mine = (x == ox) & (y == oy) & (c == oc)

                        @pl.when(mine)
                        def _(b=b, blk=blk):
                            pltpu.make_async_copy(own_src(blk), wbuf.at[b], load_sems.at[b]).start()

                        @pl.when(jnp.logical_not(mine))
                        def _(b=b, blk=blk):
                            pltpu.make_async_copy(region(blk), wbuf.at[b], load_sems.at[b]).start()

                    for b, blk in enumerate(groups[k]):
                        pltpu.make_async_copy(region(blk), wbuf.at[b], load_sems.at[b]).wait()

        xt = x_ref[...].astype(BF16)
        u = _dot_nt(xt, wbuf[0])
        v = _dot_nt(xt, wbuf[1])
        z = _dot_nt(xt, wbuf[2])
        save_ref[0] = u.astype(BF16)
        save_ref[1] = v.astype(BF16)
        save_ref[2] = z.astype(BF16)
        gu, _ = _gelu(u)
        gv, _ = _gelu(v)
        mu = jnp.mean(gv, axis=-1, keepdims=True)
        dv = gv - mu
        var = jnp.mean(dv * dv, axis=-1, keepdims=True)
        vn = (dv * lax.rsqrt(var + LN_EPS) * lng[...] + lnb[...]).astype(BF16)
        sz, _ = _silu(z)
        gate = gu * sz
        wm = jnp.where(_tril_mask(), ws_ref[0], 0.0).astype(BF16)
        bs = bs_ref[0]
        for ck in range(tm // CHUNK):
            rows = slice(ck * CHUNK, (ck + 1) * CHUNK)
            mixed = _dot(wm, vn[rows]) + bs
            ya_ref[rows, :] = (gate[rows] * mixed).astype(BF16)

        xbv = _dot_nt(xt, wbuf[3])
        cbv = _dot_nt(xt, wbuf[4])
        bbv = _dot_nt(xt, wbuf[5])
        zbv = _dot_nt(xt, wbuf[6])
        save_ref[3] = xbv.astype(BF16)
        save_ref[4] = cbv.astype(BF16)
        save_ref[5] = bbv.astype(BF16)
        save_ref[6] = zbv.astype(BF16)
        h = cbv * xbv

        @pl.when(i == 0)
        def _():
            hbuf[0:SUBLANES, :] = jnp.zeros((SUBLANES, hd), F32)

        hbuf[SUBLANES:SUBLANES + tm, :] = h
        h1 = hbuf[SUBLANES - 1:SUBLANES - 1 + tm, :]
        h2 = hbuf[SUBLANES - 2:SUBLANES - 2 + tm, :]
        conv = cb_ref[...] + cw_ref[0:1, :] * h2 + cw_ref[1:2, :] * h1 + cw_ref[2:3, :] * h
        hbuf[0:SUBLANES, :] = h[tm - SUBLANES:tm, :]
        save_ref[7] = conv.astype(BF16)
        szb, _ = _silu(zbv)
        yb_ref[...] = (bbv * conv * szb).astype(BF16)

        @pl.when((g == HEADS - 1) & (i == nt - 1))
        def _():
            for blk in groups[HEADS]:
                wait_from_sibling(blk)
            for t in range(4):
                local_copy(t).wait()
            for t in range(4):
                n = w8 if t == 0 else r8
                own_all, fwd_all = fulls[t].at[pl.ds(0, 4 * n), :], fulls[t].at[pl.ds(0, 3 * n), :]
                for ref, sem in ((own_all, own_sems.at[t]), (fwd_all, fwd_sems.at[t])):
                    pltpu.make_async_remote_copy(src_ref=ref, dst_ref=ref, send_sem=sem, recv_sem=sem,
                                                 device_id=sibling, device_id_type=MESH).wait_send()

    vec = pl.BlockSpec((1, hd), lambda g, i: (0, g))
    hbm = pl.BlockSpec(memory_space=pl.ANY)
    return pl.pallas_call(
        body, name="mixer_fwd", grid=(HEADS, nt),
        in_specs=[pl.BlockSpec((tm, d), lambda g, i: (i, 0)), hbm, hbm, hbm, hbm,
                  vec, vec,
                  pl.BlockSpec((1, CHUNK, CHUNK), lambda g, i: (g, 0, 0)),
                  pl.BlockSpec((1, CHUNK, 1), lambda g, i: (g, 0, 0)),
                  pl.BlockSpec((3, hd), lambda g, i: (0, g)),
                  vec],
        out_specs=[pl.BlockSpec((N_SAVE, tm, hd), lambda g, i: (0, i, g)),
                   pl.BlockSpec((tm, hd), lambda g, i: (i, g)),
                   pl.BlockSpec((tm, hd), lambda g, i: (i, g)),
                   hbm, hbm, hbm, hbm],
        out_shape=[jax.ShapeDtypeStruct((N_SAVE, t_len, d), BF16),
                   jax.ShapeDtypeStruct((t_len, d), BF16),
                   jax.ShapeDtypeStruct((t_len, d), BF16),
                   jax.ShapeDtypeStruct((N_DEV * w8, d), BF16),
                   jax.ShapeDtypeStruct((d, d), BF16), jax.ShapeDtypeStruct((d, d), BF16),
                   jax.ShapeDtypeStruct((d, d), BF16)],
        scratch_shapes=[pltpu.VMEM((SUBLANES + tm, hd), F32), pltpu.VMEM((N_MIX, hd, d), BF16),
                        pltpu.SemaphoreType.DMA((n_blocks,)), pltpu.SemaphoreType.DMA((4,)),
                        pltpu.SemaphoreType.DMA((4,)), pltpu.SemaphoreType.DMA((4,)),
                        pltpu.SemaphoreType.DMA((N_MIX,))],
        compiler_params=_cparams(("arbitrary", "arbitrary")),
    )(x, *shards, ln_v_g, ln_v_b, w_s, b_s3, conv_w, conv_b)


def _merge_fwd(x, ya, yb, w_in, w_oa, w_ob, b_gate):
    t_len, d = x.shape
    tm = _tile(t_len, 512)
    tn = _tile(d, 512)
    nj = d // tn

    def body(x_ref, ya_ref, yb_ref, wga, wgb, woa, wob, bga, bgb, sa_ref, sb_ref, pa_ref, pb_ref, mg_ref):
        xt = x_ref[...].astype(BF16)
        sa = jax.nn.sigmoid(_dot_nt(xt, wga[...]) + bga[...])
        sb = jax.nn.sigmoid(_dot_nt(xt, wgb[...]) + bgb[...])
        pa = _dot(ya_ref[...], woa[...])
        pb = _dot(yb_ref[...], wob[...])
        sa_ref[...] = sa.astype(BF16)
        sb_ref[...] = sb.astype(BF16)
        pa_ref[...] = pa.astype(BF16)
        pb_ref[...] = pb.astype(BF16)
        mg_ref[...] = (sa * pa + sb * pb).astype(BF16)

    row = pl.BlockSpec((tm, d), lambda j, i: (i, 0))
    out = pl.BlockSpec((tm, tn), lambda j, i: (i, j))
    return pl.pallas_call(
        body, name="merge_fwd", grid=(nj, t_len // tm),
        in_specs=[row, row, row,
                  pl.BlockSpec((tn, d), lambda j, i: (7 * nj + j, 0)),
                  pl.BlockSpec((tn, d), lambda j, i: (8 * nj + j, 0)),
                  pl.BlockSpec((d, tn), lambda j, i: (0, j)),
                  pl.BlockSpec((d, tn), lambda j, i: (0, j)),
                  pl.BlockSpec((1, tn), lambda j, i: (0, j)),
                  pl.BlockSpec((1, tn), lambda j, i: (0, nj + j))],
        out_specs=[out] * 5,
        out_shape=[jax.ShapeDtypeStruct((t_len, d), BF16)] * 5,
        compiler_params=_cparams(("parallel", "arbitrary")),
    )(x, ya, yb, w_in, w_in, w_oa, w_ob, b_gate, b_gate)


def _out_ln_loss(merged, w_out, x, target, ln_g, ln_b):
    t_len, d = x.shape
    tm = _tile(t_len, 256)
    nt = t_len // tm

    def body(mg_ref, w_ref, x_ref, t_ref, g_ref, b_ref, dz_ref, gx_ref, xb_ref, glg_ref, glb_ref, ls_ref, a_g, a_b, a_l):
        i = pl.program_id(0)
        xb_ref[...] = x_ref[...].astype(BF16)

        @pl.when(i == 0)
        def _():
            a_g[...] = jnp.zeros_like(a_g)
            a_b[...] = jnp.zeros_like(a_b)
            a_l[...] = jnp.zeros_like(a_l)

        zres = DN_ALPHA * x_ref[...] + _dot(mg_ref[...], w_ref[...])
        mu = jnp.mean(zres, axis=-1, keepdims=True)
        dc = zres - mu
        var = jnp.mean(dc * dc, axis=-1, keepdims=True)
        rstd = lax.rsqrt(var + LN_EPS)
        xhat = dc * rstd
        g = g_ref[...]
        err = xhat * g + b_ref[...] - t_ref[...]
        dy = err * (1.0 / d)
        a_l[...] += _fold8(err * err)
        a_g[...] += _fold8(dy * xhat)
        a_b[...] += _fold8(dy)
        dxh = dy * g
        m1 = jnp.mean(dxh, axis=-1, keepdims=True)
        m2 = jnp.mean(dxh * xhat, axis=-1, keepdims=True)
        dz = rstd * (dxh - m1 - xhat * m2)
        dz_ref[...] = dz.astype(BF16)
        gx_ref[...] = DN_ALPHA * dz

        @pl.when(i == nt - 1)
        def _():
            glg_ref[...] = jnp.sum(a_g[...], axis=0, keepdims=True)
            glb_ref[...] = jnp.sum(a_b[...], axis=0, keepdims=True)
            ls_ref[...] = jnp.sum(a_l[...], axis=0, keepdims=True)

    row = pl.BlockSpec((tm, d), lambda i: (i, 0))
    vec = pl.BlockSpec((1, d), lambda i: (0, 0))
    return pl.pallas_call(
        body, name="out_ln_loss", grid=(nt,),
        in_specs=[row, pl.BlockSpec((d, d), lambda i: (0, 0)), row, row, vec, vec],
        out_specs=[row, row, row, vec, vec, vec],
        out_shape=[jax.ShapeDtypeStruct((t_len, d), BF16), jax.ShapeDtypeStruct((t_len, d), F32),
                   jax.ShapeDtypeStruct((t_len, d), BF16)] + [jax.ShapeDtypeStruct((1, d), F32)] * 3,
        scratch_shapes=[pltpu.VMEM((SUBLANES, d), F32)] * 3,
        compiler_params=_cparams(("arbitrary",)),
    )(merged, w_out, x, target, ln_g, ln_b)


def _merge_bwd(dz, w_out, sa, sb, pa, pb):
    t_len, d = dz.shape
    tm = _tile(t_len, 256)
    nt = t_len // tm

    def body(dz_ref, w_ref, sa_ref, sb_ref, pa_ref, pb_ref, da_ref, db_ref, dg_ref, ga_ref, gb_ref, acc_a, acc_b):
        i = pl.program_id(0)

        @pl.when(i == 0)
        def _():
            acc_a[...] = jnp.zeros_like(acc_a)
            acc_b[...] = jnp.zeros_like(acc_b)

        dm = _dot_nt(dz_ref[...], w_ref[...])
        sa = sa_ref[...].astype(F32)
        sb = sb_ref[...].astype(F32)
        da = dm * sa
        db = dm * sb
        da_ref[...] = da.astype(BF16)
        db_ref[...] = db.astype(BF16)
        dga = da * pa_ref[...].astype(F32) * (1.0 - sa)
        dgb = db * pb_ref[...].astype(F32) * (1.0 - sb)
        dg_ref[0] = dga.astype(BF16)
        dg_ref[1] = dgb.astype(BF16)
        acc_a[...] += _fold8(dga)
        acc_b[...] += _fold8(dgb)

        @pl.when(i == nt - 1)
        def _():
            ga_ref[...] = jnp.sum(acc_a[...], axis=0, keepdims=True)
            gb_ref[...] = jnp.sum(acc_b[...], axis=0, keepdims=True)

    row = pl.BlockSpec((tm, d), lambda i: (i, 0))
    vec = pl.BlockSpec((1, d), lambda i: (0, 0))
    return pl.pallas_call(
        body, name="merge_bwd", grid=(nt,),
        in_specs=[row, pl.BlockSpec((d, d), lambda i: (0, 0)), row, row, row, row],
        out_specs=[row, row, pl.BlockSpec((2, tm, d), lambda i: (0, i, 0)), vec, vec],
        out_shape=[jax.ShapeDtypeStruct((t_len, d), BF16)] * 2 + [jax.ShapeDtypeStruct((2, t_len, d), BF16)]
        + [jax.ShapeDtypeStruct((1, d), F32)] * 2,
        scratch_shapes=[pltpu.VMEM((SUBLANES, d), F32)] * 2,
        compiler_params=_cparams(("arbitrary",)),
    )(dz, w_out, sa, sb, pa, pb)


def _grad_w(a, b, name, exchange=None):
    t_len, m = a.shape
    n = b.shape[1]
    tm, tn, tk = _tile(m, MM_TILE), _tile(n, MM_TILE), _tile(t_len, MM_DEPTH_TOKENS)
    ni, nj, nk = m // tm, n // tn, t_len // tk

    def body(a_ref, b_ref, *rest):
        if exchange is None:
            o_ref, acc = rest
        else:
            src, o_ref, land, acc, send_sems, recv_sems = rest
        i, j, k = pl.program_id(0), pl.program_id(1), pl.program_id(2)

        def copies():
            x, y, c = lax.axis_index("x"), lax.axis_index("y"), lax.axis_index("c")
            w = exchange.shape[1] // N_DEV
            return [pltpu.make_async_remote_copy(
                src_ref=_shard_ref(src, "col", 2 * s + 1 - c, w), dst_ref=land.at[s],
                send_sem=send_sems.at[s], recv_sem=recv_sems.at[s],
                device_id=(x, y, 1 - c), device_id_type=MESH) for s in range(N_CHIP)]

        if exchange is not None:
            @pl.when((i == 0) & (j == 0) & (k == 0))
            def _():
                for cp in copies():
                    cp.start()

        @pl.when(k == 0)
        def _():
            acc[...] = jnp.zeros_like(acc)

        acc[...] += _dot_tn(a_ref[...], b_ref[...])

        @pl.when(k == nk - 1)
        def _():
            o_ref[...] = acc[...].astype(BF16)

        if exchange is not None:
            @pl.when((i == ni - 1) & (j == nj - 1) & (k == nk - 1))
            def _():
                for cp in copies():
                    cp.wait()

    in_specs = [pl.BlockSpec((tk, tm), lambda i, j, k: (k, i)), pl.BlockSpec((tk, tn), lambda i, j, k: (k, j))]
    out_specs = [pl.BlockSpec((tm, tn), lambda i, j, k: (i, j))]
    out_shape = [jax.ShapeDtypeStruct((m, n), BF16)]
    scratch = [pltpu.VMEM((tm, tn), F32)]
    args = [a, b]
    if exchange is not None:
        hbm = pl.BlockSpec(memory_space=pl.ANY)
        in_specs.append(hbm)
        out_specs.append(hbm)
        out_shape.append(jax.ShapeDtypeStruct((N_CHIP, exchange.shape[0], exchange.shape[1] // N_DEV), exchange.dtype))
        scratch += [pltpu.SemaphoreType.DMA((N_CHIP,)), pltpu.SemaphoreType.DMA((N_CHIP,))]
        args.append(exchange)
    outs = pl.pallas_call(
        body, name=name, grid=(ni, nj, nk),
        in_specs=in_specs, out_specs=out_specs, out_shape=out_shape, scratch_shapes=scratch,
        compiler_params=_cparams(("arbitrary", "arbitrary", "arbitrary")),
    )(*args)
    return outs[0] if exchange is None else (outs[0], outs[1])


def _mixer_bwd(da, db, w_oa, w_ob, saved, dgate, ln_v_g, ln_v_b, w_s, b_s3, conv_w):
    t_len, d = da.shape
    hd = d // HEADS
    tm = _tile(t_len, 512)
    nt = t_len // tm

    def body(da_ref, db_ref, woa, wob, sv, dgt, lng, lnb, ws_ref, bs_ref, cw_ref,
             dp_ref, gws_ref, gbs_ref, glg_ref, glb_ref, gcw_ref, gcb_ref,
             dbuf, carry, a_ws, a_bs, a_lg, a_lb, a_c0, a_c1, a_c2, a_cb):
        i = pl.program_id(1)

        @pl.when(i == 0)
        def _():
            carry[...] = jnp.zeros_like(carry)
            for a in (a_ws, a_bs, a_lg, a_lb, a_c0, a_c1, a_c2, a_cb):
                a[...] = jnp.zeros_like(a)

        dya_t = _dot_nt(da_ref[...], woa[...])
        dyb_t = _dot_nt(db_ref[...], wob[...])
        g = lng[...]
        beta = lnb[...]
        wm = jnp.where(_tril_mask(), ws_ref[0], 0.0).astype(BF16)
        bs = bs_ref[0]
        w0, w1, w2 = cw_ref[0:1, :], cw_ref[1:2, :], cw_ref[2:3, :]
        dbuf[tm:tm + SUBLANES, :] = carry[...]
        gws = a_ws[...]
        gbs = a_bs[...]
        lg, lb = a_lg[...], a_lb[...]
        c0, c1, c2, cb = a_c0[...], a_c1[...], a_c2[...], a_cb[...]
        for ck in reversed(range(tm // CHUNK)):
            r0 = ck * CHUNK
            rows = slice(r0, r0 + CHUNK)
            dya = dya_t[rows]
            u = sv[0, rows, :].astype(F32)
            v = sv[1, rows, :].astype(F32)
            z = sv[2, rows, :].astype(F32)
            gu, gu_grad = _gelu(u)
            gv, gv_grad = _gelu(v)
            mu = jnp.mean(gv, axis=-1, keepdims=True)
            dvc = gv - mu
            var = jnp.mean(dvc * dvc, axis=-1, keepdims=True)
            rstd = lax.rsqrt(var + LN_EPS)
            vhat = dvc * rstd
            vn = (vhat * g + beta).astype(BF16)
            sz, sz_grad = _silu(z)
            t1 = dya * sz
            dmixed = t1 * gu
            dmixed_b = dmixed.astype(BF16)
            mixed = _dot(wm, vn) + bs
            gws = gws + _dot_nt(dmixed_b, vn)
            gbs = gbs + dmixed
            dvn = _dot_tn(wm, dmixed_b)
            dp_ref[0, rows, :] = (t1 * mixed * gu_grad).astype(BF16)
            dp_ref[2, rows, :] = (dya * gu * mixed * sz_grad).astype(BF16)
            lg = lg + _fold8(dvn * vhat)
            lb = lb + _fold8(dvn)
            dvh = dvn * g
            m1 = jnp.mean(dvh, axis=-1, keepdims=True)
            m2 = jnp.mean(dvh * vhat, axis=-1, keepdims=True)
            dp_ref[1, rows, :] = (rstd * (dvh - m1 - vhat * m2) * gv_grad).astype(BF16)
            dyb = dyb_t[rows]
            xbv = sv[3, rows, :].astype(F32)
            cbv = sv[4, rows, :].astype(F32)
            bbv = sv[5, rows, :].astype(F32)
            zbv = sv[6, rows, :].astype(F32)
            conv = sv[7, rows, :].astype(F32)
            szb, szb_grad = _silu(zbv)
            dp_ref[5, rows, :] = (dyb * conv * szb).astype(BF16)
            dp_ref[6, rows, :] = (dyb * bbv * conv * szb_grad).astype(BF16)
            dconv = dyb * bbv * szb
            dbuf[r0:r0 + CHUNK, :] = dconv
            dc1 = dbuf[r0 + 1:r0 + 1 + CHUNK, :]
            dc2 = dbuf[r0 + 2:r0 + 2 + CHUNK, :]
            if ck == 0:
                carry[...] = dconv[0:SUBLANES, :]
            h = cbv * xbv
            c2 = c2 + _fold8(dconv * h)
            c1 = c1 + _fold8(dc1 * h)
            c0 = c0 + _fold8(dc2 * h)
            cb = cb + _fold8(dconv)
            dh = w2 * dconv + w1 * dc1 + w0 * dc2
            dp_ref[3, rows, :] = (dh * cbv).astype(BF16)
            dp_ref[4, rows, :] = (dh * xbv).astype(BF16)
            dp_ref[7, rows, :] = dgt[0, rows, :]
            dp_ref[8, rows, :] = dgt[1, rows, :]
        a_ws[...] = gws
        a_bs[...] = gbs
        a_lg[...], a_lb[...] = lg, lb
        a_c0[...], a_c1[...], a_c2[...], a_cb[...] = c0, c1, c2, cb

        @pl.when(i == nt - 1)
        def _():
            gws_ref[0] = jnp.where(_tril_mask(), a_ws[...], 0.0)
            gbs_ref[0] = jnp.sum(a_bs[...], axis=1, keepdims=True)
            glg_ref[...] = jnp.sum(a_lg[...], axis=0, keepdims=True)
            glb_ref[...] = jnp.sum(a_lb[...], axis=0, keepdims=True)
            gcw_ref[0:1, :] = jnp.sum(a_c0[...], axis=0, keepdims=True)
            gcw_ref[1:2, :] = jnp.sum(a_c1[...], axis=0, keepdims=True)
            gcw_ref[2:3, :] = jnp.sum(a_c2[...], axis=0, keepdims=True)
            gcb_ref[...] = jnp.sum(a_cb[...], axis=0, keepdims=True)

    def rev(i):
        return nt - 1 - i

    row = pl.BlockSpec((tm, d), lambda g, i: (rev(i), 0))
    wrow = pl.BlockSpec((hd, d), lambda g, i: (g, 0))
    vec = pl.BlockSpec((1, hd), lambda g, i: (0, g))
    acc8 = pltpu.VMEM((SUBLANES, hd), F32)
    return pl.pallas_call(
        body, name="mixer_bwd", grid=(HEADS, nt),
        in_specs=[row, row, wrow, wrow,
                  pl.BlockSpec((N_SAVE, tm, hd), lambda g, i: (0, rev(i), g)),
                  pl.BlockSpec((2, tm, hd), lambda g, i: (0, rev(i), g)),
                  vec, vec,
                  pl.BlockSpec((1, CHUNK, CHUNK), lambda g, i: (g, 0, 0)),
                  pl.BlockSpec((1, CHUNK, 1), lambda g, i: (g, 0, 0)),
                  pl.BlockSpec((3, hd), lambda g, i: (0, g))],
        out_specs=[pl.BlockSpec((N_IN, tm, hd), lambda g, i: (0, rev(i), g)),
                   pl.BlockSpec((1, CHUNK, CHUNK), lambda g, i: (g, 0, 0)),
                   pl.BlockSpec((1, CHUNK, 1), lambda g, i: (g, 0, 0)),
                   vec, vec,
                   pl.BlockSpec((3, hd), lambda g, i: (0, g)),
                   vec],
        out_shape=[jax.ShapeDtypeStruct((N_IN, t_len, d), BF16),
                   jax.ShapeDtypeStruct((HEADS, CHUNK, CHUNK), F32),
                   jax.ShapeDtypeStruct((HEADS, CHUNK, 1), F32),
                   jax.ShapeDtypeStruct((1, d), F32), jax.ShapeDtypeStruct((1, d), F32),
                   jax.ShapeDtypeStruct((3, d), F32), jax.ShapeDtypeStruct((1, d), F32)],
        scratch_shapes=[pltpu.VMEM((tm + SUBLANES, hd), F32), acc8,
                        pltpu.VMEM((CHUNK, CHUNK), F32), pltpu.VMEM((CHUNK, hd), F32),
                        acc8, acc8, acc8, acc8, acc8, acc8],
        compiler_params=_cparams(("parallel", "arbitrary")),
    )(da, db, w_oa, w_ob, saved, dgate, ln_v_g, ln_v_b, w_s, b_s3, conv_w)


def _grad_w_in(xb, dp):
    t_len, d = xb.shape
    tm, tn, tk = _tile(d, MM_TILE), _tile(d, MM_TILE), _tile(t_len, MM_DEPTH_TOKENS)
    nj = d // tn
    nk = t_len // tk

    def body(a_ref, b_ref, o_ref, acc):
        k = pl.program_id(2)

        @pl.when(k == 0)
        def _():
            acc[...] = jnp.zeros_like(acc)

        acc[...] += _dot_tn(a_ref[...], b_ref[...])

        @pl.when(k == nk - 1)
        def _():
            o_ref[...] = acc[...].astype(BF16)

    return pl.pallas_call(
        body, name="grad_w_in", grid=(d // tm, N_IN * nj, nk),
        in_specs=[pl.BlockSpec((tk, tm), lambda i, j, k: (k, i)),
                  pl.BlockSpec((None, tk, tn), lambda i, j, k: (j // nj, k, j % nj))],
        out_specs=pl.BlockSpec((tm, tn), lambda i, j, k: (i, j)),
        out_shape=jax.ShapeDtypeStruct((d, N_IN * d), BF16),
        scratch_shapes=[pltpu.VMEM((tm, tn), F32)],
        compiler_params=_cparams(("parallel", "parallel", "arbitrary")),
    )(xb, dp)


def _grad_x_tail(dp, w_in, partial, slots):
    _, t_len, d = dp.shape
    lo, hi = slots
    tm, tn, tk = _tile(t_len, MM_TILE), _tile(d, MM_TILE), _tile(d, MM_DEPTH)
    nkb = d // tk
    nk = (hi - lo) * nkb

    def body(a_ref, b_ref, r_ref, o_ref, acc):
        k = pl.program_id(2)

        @pl.when(k == 0)
        def _():
            acc[...] = r_ref[...]

        acc[...] += _dot(a_ref[...], b_ref[...])

        @pl.when(k == nk - 1)
        def _():
            o_ref[...] = acc[...]

    return pl.pallas_call(
        body, name="grad_x_tail", grid=(t_len // tm, d // tn, nk),
        in_specs=[pl.BlockSpec((None, tm, tk), lambda i, j, k: (lo + k // nkb, i, k % nkb)),
                  pl.BlockSpec((tk, tn), lambda i, j, k: (lo * nkb + k, j)),
                  pl.BlockSpec((tm, tn), lambda i, j, k: (i, j))],
        out_specs=pl.BlockSpec((tm, tn), lambda i, j, k: (i, j)),
        out_shape=jax.ShapeDtypeStruct((t_len, d), F32),
        scratch_shapes=[pltpu.VMEM((tm, tn), F32)],
        compiler_params=_cparams(("parallel", "parallel", "arbitrary")),
    )(dp, w_in, partial)


def _grad_x(dp, w_in, gx_direct, parts, packed, slots):
    _, t_len, d = dp.shape
    tm, tn, tk = _tile(t_len, MM_TILE), _tile(d, MM_TILE), _tile(d, MM_DEPTH)
    nkb = d // tk
    lo, hi = slots
    nk = (hi - lo) * nkb
    ni, nj = t_len // tm, d // tn
    n_parts = len(parts)

    def body(a_ref, b_ref, r_ref, *rest):
        srcs = rest[:n_parts]
        pk_ref = rest[n_parts]
        o_ref = rest[n_parts + 1]
        lands = rest[n_parts + 2:2 * n_parts + 2]
        gath = rest[2 * n_parts + 2]
        acc, send_sems, recv_sems, pk_send, pk_recv, pk_local = rest[2 * n_parts + 3:]
        i, j, k = pl.program_id(0), pl.program_id(1), pl.program_id(2)
        x, y, c = lax.axis_index("x"), lax.axis_index("y"), lax.axis_index("c")
        my_chip = 2 * x + y
        me = 4 * x + 2 * y + c
        chips = [(1 - x, y), (x, 1 - y), (1 - x, 1 - y)]

        def part_copy(t, n):
            px, py = chips[n]
            return pltpu.make_async_remote_copy(
                src_ref=srcs[t].at[2 * px + py], dst_ref=lands[t].at[my_chip],
                send_sem=send_sems.at[3 * t + n], recv_sem=recv_sems.at[3 * t + n],
                device_id=(px, py, c), device_id_type=MESH)

        def part_landing(t, n):
            px, py = chips[n]
            return pltpu.make_async_remote_copy(
                src_ref=srcs[t].at[my_chip], dst_ref=lands[t].at[2 * px + py],
                send_sem=send_sems.at[3 * t + n], recv_sem=recv_sems.at[3 * t + n],
                device_id=(px, py, c), device_id_type=MESH)

        def pk_copy(s):
            return pltpu.make_async_remote_copy(
                src_ref=pk_ref, dst_ref=gath.at[me], send_sem=pk_send, recv_sem=pk_recv.at[me],
                device_id=(s // 4, (s // 2) % 2, s % 2), device_id_type=MESH)

        @pl.when((i == 0) & (j == 0) & (k == 0))
        def _():
            for t in range(n_parts):
                for n in range(3):
                    part_copy(t, n).start()
            pltpu.make_async_copy(pk_ref, gath.at[me], pk_local).start()
            for s in range(N_DEV):
                @pl.when(s != me)
                def _(s=s):
                    pk_copy(s).start()

        @pl.when(k == 0)
        def _():
            acc[...] = r_ref[...]

        acc[...] += _dot(a_ref[...], b_ref[...])

        @pl.when(k == nk - 1)
        def _():
            o_ref[...] = acc[...]

        @pl.when((i == ni - 1) & (j == nj - 1) & (k == nk - 1))
        def _():
            for t in range(n_parts):
                for n in range(3):
                    part_landing(t, n).wait_recv()
            for t in range(n_parts):
                for n in range(3):
                    part_copy(t, n).wait_send()
            for s in range(N_DEV):
                @pl.when(s != me)
                def _(s=s):
                    pltpu.make_async_remote_copy(
                        src_ref=pk_ref, dst_ref=gath.at[s], send_sem=pk_send, recv_sem=pk_recv.at[s],
                        device_id=(s // 4, (s // 2) % 2, s % 2), device_id_type=MESH).wait_recv()
            seven = gath.at[pl.ds(0, N_DEV - 1)]
            pltpu.make_async_remote_copy(src_ref=seven, dst_ref=seven, send_sem=pk_send, recv_sem=pk_send,
                                         device_id=(x, y, 1 - c), device_id_type=MESH).wait_send()
            pltpu.make_async_copy(pk_ref, gath.at[me], pk_local).wait()

    hbm = pl.BlockSpec(memory_space=pl.ANY)
    outs = pl.pallas_call(
        body, name="grad_x", grid=(ni, nj, nk),
        in_specs=[pl.BlockSpec((None, tm, tk), lambda i, j, k: (lo + k // nkb, i, k % nkb)),
                  pl.BlockSpec((tk, tn), lambda i, j, k: (lo * nkb + k, j)),
                  pl.BlockSpec((tm, tn), lambda i, j, k: (i, j))] + [hbm] * (n_parts + 1),
        out_specs=[pl.BlockSpec((tm, tn), lambda i, j, k: (i, j))] + [hbm] * (n_parts + 1),
        out_shape=[jax.ShapeDtypeStruct((t_len, d), F32)] + [jax.ShapeDtypeStruct(p.shape, p.dtype) for p in parts]
        + [jax.ShapeDtypeStruct((N_DEV,) + packed.shape, packed.dtype)],
        scratch_shapes=[pltpu.VMEM((tm, tn), F32),
                        pltpu.SemaphoreType.DMA((3 * n_parts,)), pltpu.SemaphoreType.DMA((3 * n_parts,)),
                        pltpu.SemaphoreType.DMA(()), pltpu.SemaphoreType.DMA((N_DEV,)), pltpu.SemaphoreType.DMA(())],
        compiler_params=_cparams(("arbitrary", "arbitrary", "arbitrary")),
    )(dp, w_in, gx_direct, *parts, packed)
    return outs[0], list(outs[1:1 + n_parts]), outs[1 + n_parts]


def kernel(x, w_in, b_gate, ln_v_g, ln_v_b, w_s, b_s, conv_w, conv_b, w_oa, w_ob, w_out, ln_g, ln_b, loss_target, m_w_in, m_b_gate, m_ln_v_g, m_ln_v_b, m_w_s, m_b_s, m_conv_w, m_conv_b, m_w_oa, m_w_ob, m_w_out, m_ln_g, m_ln_b, v_w_in, v_b_gate, v_ln_v_g, v_ln_v_b, v_w_s, v_b_s, v_conv_w, v_conv_b, v_w_oa, v_w_ob, v_w_out, v_ln_g, v_ln_b):
    _, t_len, d = x.shape
    assert d % (HEADS * 128) == 0 and t_len % CHUNK == 0 and w_in.shape[2] * N_DEV == N_IN * d
    x2 = x[0]
    tgt2 = loss_target[0]
    c_arr = lax.axis_index("c").astype(jnp.int32).reshape(1)
    chip_arr = (2 * lax.axis_index("x") + lax.axis_index("y")).astype(jnp.int32).reshape(1)
    dev = 4 * lax.axis_index("x") + 2 * lax.axis_index("y") + lax.axis_index("c")

    shards = [_cast_transposed(w_in[0], "cast_w_in"), _cast_bf16(w_oa[0], "cast_w_oa"),
              _cast_bf16(w_ob[0], "cast_w_ob"), _cast_bf16(w_out[0], "cast_w_out")]
    (conv_w_g,) = _all_gather([conv_w[0]], ["lead"], "gather_conv_w", vmem=True)
    conv_w_f = jnp.transpose(conv_w_g, (1, 0, 2)).reshape(3, d)
    w_s3 = w_s[0]
    b_s3 = b_s[0].reshape(HEADS, CHUNK, 1)

    saved, ya, yb, w_in_f, w_oa_f, w_ob_f, w_out_f = _mixer_fwd(
        x2, shards, ln_v_g, ln_v_b, w_s3, b_s3, conv_w_f, conv_b)
    sa, sb, pa, pb, merged = _merge_fwd(x2, ya, yb, w_in_f, w_oa_f, w_ob_f, b_gate)
    dz, gx_direct, xb, g_ln_g, g_ln_b, err2 = _out_ln_loss(merged, w_out_f, x2, tgt2, ln_g, ln_b)
    loss = lax.psum(0.5 * jnp.sum(err2) / d, ("x", "y", "c"))

    da, db, dgate, g_bga, g_bgb = _merge_bwd(dz, w_out_f, sa, sb, pa, pb)
    dp, g_ws, g_bs, g_lvg, g_lvb, g_cw, g_cb = _mixer_bwd(
        da, db, w_oa_f, w_ob_f, saved, dgate, ln_v_g, ln_v_b, w_s3, b_s3, conv_w_f)
    gw_in = _grad_w_in(xb, dp)

    gw_out, land_in = _grad_w(merged, dz, "grad_w_out", exchange=gw_in)
    part_in = _pair_sum(gw_in, land_in, "col", c_arr, "grad_pair_sum_0")
    gw_oa = _grad_w(ya, da, "grad_w_oa")
    gw_ob = _grad_w(yb, db, "grad_w_ob")
    rows = [gw_oa, gw_ob, gw_out]
    lands = _pair_exchange(rows, ["row"] * 3, "grad_pair_exchange")
    parts = [part_in] + [_pair_sum(g, l, "row", c_arr, "grad_pair_sum_%d" % (n + 1))
                         for n, (g, l) in enumerate(zip(rows, lands))]
    pieces = [jnp.concatenate([g_bga, g_bgb], axis=1), g_lvg, g_lvb, g_ws, g_bs, g_cw, g_cb, g_ln_g, g_ln_b]
    sizes = [p.size for p in pieces]
    packed = jnp.concatenate([p.reshape(-1, 128) for p in pieces], axis=0)
    gx_part, lands2, gathered = _grad_x(dp, w_in_f, gx_direct, parts, packed, (0, GRAD_X_COMM_BLOCKS))
    grad_x = _grad_x_tail(dp, w_in_f, gx_part, (GRAD_X_COMM_BLOCKS, N_IN))[None]

    big = []
    for n, (w, m, v) in enumerate([(w_in, m_w_in, v_w_in), (w_oa, m_w_oa, v_w_oa), (w_ob, m_w_ob, v_w_ob),
                                   (w_out, m_w_out, v_w_out)]):
        big.append([o[None] for o in _sum_adam(parts[n], lands2[n], w[0], m[0], v[0], chip_arr, "sum_adam_%d" % n)])
    (g_w_in, d_w_in, nm_w_in, nv_w_in), (g_w_oa, d_w_oa, nm_w_oa, nv_w_oa), \
        (g_w_ob, d_w_ob, nm_w_ob, nv_w_ob), (g_w_out, d_w_out, nm_w_out, nv_w_out) = big

    total = _small_sum(gathered, "sum_small_grads")
    offs = [0]
    for s in sizes:
        offs.append(offs[-1] + s // 128)
    unpacked = [total[offs[n]:offs[n + 1]] for n in range(len(pieces))]
    g_b_gate = unpacked[0].reshape(b_gate.shape)
    g_ln_v_g = unpacked[1].reshape(ln_v_g.shape)
    g_ln_v_b = unpacked[2].reshape(ln_v_b.shape)
    g_w_s = unpacked[3].reshape(w_s.shape)
    g_b_s = unpacked[4].reshape(b_s.shape)
    g_conv_w = lax.dynamic_slice_in_dim(unpacked[5].reshape(3, d), dev * (d // N_DEV), d // N_DEV, axis=1)[None]
    g_conv_b = unpacked[6].reshape(conv_b.shape)
    g_ln_g2 = unpacked[7].reshape(ln_g.shape)
    g_ln_b2 = unpacked[8].reshape(ln_b.shape)

    small_w = [b_gate, ln_v_g, ln_v_b, w_s, b_s, conv_w, conv_b, ln_g, ln_b]
    small_g = [g_b_gate, g_ln_v_g, g_ln_v_b, g_w_s, g_b_s, g_conv_w, g_conv_b, g_ln_g2, g_ln_b2]
    small_m = [m_b_gate, m_ln_v_g, m_ln_v_b, m_w_s, m_b_s, m_conv_w, m_conv_b, m_ln_g, m_ln_b]
    small_v = [v_b_gate, v_ln_v_g, v_ln_v_b, v_w_s, v_b_s, v_conv_w, v_conv_b, v_ln_g, v_ln_b]

    def flat(a):
        return a.reshape(-1, a.shape[-1])

    res = _small_adam([flat(a) for a in small_w], [flat(a) for a in small_g], [flat(a) for a in small_m],
                      [flat(a) for a in small_v], "adam_small")
    ns = len(small_w)
    d_s = [res[n].reshape(small_w[n].shape) for n in range(ns)]
    nm_s = [res[ns + n].reshape(small_w[n].shape) for n in range(ns)]
    nv_s = [res[2 * ns + n].reshape(small_w[n].shape) for n in range(ns)]

    def ordered(first, small, oa, ob, out):
        return [first] + small[:7] + [oa, ob, out] + small[7:]

    return (loss, grad_x,
            *ordered(g_w_in, small_g, g_w_oa, g_w_ob, g_w_out),
            *ordered(d_w_in, d_s, d_w_oa, d_w_ob, d_w_out),
            *ordered(nm_w_in, nm_s, nm_w_oa, nm_w_ob, nm_w_out),
            *ordered(nv_w_in, nv_s, nv_w_oa, nv_w_ob, nv_w_out))
```

```python
import functools

import jax
import jax.numpy as jnp
from jax import lax
from jax.experimental import pallas as pl
from jax.experimental.pallas import tpu as pltpu

F32 = jnp.float32
BF16 = jnp.bfloat16
MESH = pl.DeviceIdType.MESH

N_DEV = 8
N_CHIP = 4
HEADS = 8
CHUNK = 128
N_IN = 9
N_MIX = 7
N_SAVE = 8
LN_EPS = 1e-5
DN_ALPHA = 2.0 ** 0.25
ADAM_LR = 0.001
ADAM_B1 = 0.9
ADAM_B2 = 0.999
ADAM_EPS = 1e-08
ADAM_WD = 0.01
ADAM_STEP = 10
GELU_C0 = 0.7978845608028654
GELU_C1 = 0.044715
SUBLANES = 8
FIRST_SENDS = 3
SENDS_PER_GROUP = 2
MM_TILE = 1024
MM_DEPTH = 2048
MM_DEPTH_TOKENS = 4096
GRAD_X_COMM_BLOCKS = 8
VMEM_LIMIT = 56 << 20


def _cparams(sem):
    return pltpu.CompilerParams(dimension_semantics=sem, vmem_limit_bytes=VMEM_LIMIT)


def _tile(n, want):
    t = min(n, want)
    while n % t:
        t //= 2
    return t


def _gelu(u):
    u2 = u * u
    t = jnp.tanh(u * (GELU_C0 + (GELU_C0 * GELU_C1) * u2))
    hp = 0.5 * t + 0.5
    grad = hp + (0.5 * u) * (1.0 - t * t) * (GELU_C0 + (3.0 * GELU_C0 * GELU_C1) * u2)
    return u * hp, grad


def _silu(z):
    s = jax.nn.sigmoid(z)
    sil = z * s
    return sil, s + sil * (1.0 - s)


def _fold8(a):
    return jnp.sum(a.reshape(a.shape[0] // SUBLANES, SUBLANES, a.shape[1]), axis=0)


def _dot(a, b):
    return jnp.dot(a, b, preferred_element_type=F32)


def _dot_nt(a, b):
    return lax.dot_general(a, b, (((1,), (1,)), ((), ())), preferred_element_type=F32)


def _dot_tn(a, b):
    return lax.dot_general(a, b, (((0,), (0,)), ((), ())), preferred_element_type=F32)


def _tril_mask():
    r = lax.broadcasted_iota(jnp.int32, (CHUNK, CHUNK), 0)
    c = lax.broadcasted_iota(jnp.int32, (CHUNK, CHUNK), 1)
    return c <= r


def _cast_bf16(a, name):
    rows, cols = a.shape
    rb = _tile(rows, 256)

    def body(a_ref, o_ref):
        o_ref[...] = a_ref[...].astype(BF16)

    return pl.pallas_call(
        body, name=name, grid=(rows // rb,),
        in_specs=[pl.BlockSpec((rb, cols), lambda i: (i, 0))],
        out_specs=pl.BlockSpec((rb, cols), lambda i: (i, 0)),
        out_shape=jax.ShapeDtypeStruct((rows, cols), BF16),
        compiler_params=_cparams(("parallel",)),
    )(a)


def _cast_transposed(a, name):
    rows, cols = a.shape
    tb = _tile(rows, 2048)
    tc = _tile(cols, 256)

    def body(a_ref, o_ref):
        o_ref[...] = a_ref[...].T.astype(BF16)

    return pl.pallas_call(
        body, name=name, grid=(cols // tc, rows // tb),
        in_specs=[pl.BlockSpec((tb, tc), lambda j, i: (i, j))],
        out_specs=pl.BlockSpec((tc, tb), lambda j, i: (j, i)),
        out_shape=jax.ShapeDtypeStruct((cols, rows), BF16),
        compiler_params=_cparams(("parallel", "parallel")),
    )(a)


def _shard_ref(full, kind, s, n):
    if kind == "col":
        return full.at[:, pl.ds(pl.multiple_of(s * n, 128), n)]
    if kind == "row":
        return full.at[pl.ds(pl.multiple_of(s * n, SUBLANES), n), :]
    return full.at[s]


def _all_gather(shards, kinds, name, vmem):
    nt = len(shards)
    out_shapes = []
    for a, kind in zip(shards, kinds):
        if kind == "col":
            out_shapes.append(jax.ShapeDtypeStruct((a.shape[0], N_DEV * a.shape[1]), a.dtype))
        elif kind == "row":
            out_shapes.append(jax.ShapeDtypeStruct((N_DEV * a.shape[0], a.shape[1]), a.dtype))
        else:
            out_shapes.append(jax.ShapeDtypeStruct((N_DEV,) + a.shape, a.dtype))

    def body(*refs):
        srcs, fulls = refs[:nt], refs[nt:2 * nt]
        send_sems, recv_sems, local_sems = refs[2 * nt:]
        x, y, c = lax.axis_index("x"), lax.axis_index("y"), lax.axis_index("c")
        sibling = (x, y, 1 - c)
        chips = [(1 - x, y), (x, 1 - y), (1 - x, 1 - y)]

        def dev(px, py, pc):
            return 4 * px + 2 * py + pc

        def region(t, s):
            a, kind = shards[t], kinds[t]
            n = a.shape[1] if kind == "col" else a.shape[0]
            return _shard_ref(fulls[t], kind, s, n)

        def copy(t, k, block, to, own=False):
            return pltpu.make_async_remote_copy(
                src_ref=srcs[t] if own else region(t, block), dst_ref=region(t, block),
                send_sem=send_sems.at[7 * t + k], recv_sem=recv_sems.at[7 * t + k],
                device_id=to, device_id_type=MESH)

        me = dev(x, y, c)
        started = []
        for t in range(nt):
            mine = pltpu.make_async_copy(srcs[t], region(t, me), local_sems.at[t])
            mine.start()
            started.append(mine)
        first = []
        for t in range(nt):
            first.append(copy(t, 0, me, sibling, own=True))
            for j, chip in enumerate(chips):
                first.append(copy(t, 1 + j, me, (*chip, c), own=True))
        for cp in first:
            cp.start()
        passed = []
        for t in range(nt):
            for j, chip in enumerate(chips):
                blk = dev(*chip, c)
                copy(t, 1 + j, blk, sibling).wait_recv()
                fwd = copy(t, 4 + j, blk, sibling)
                fwd.start()
                passed.append(fwd)
        for t in range(nt):
            copy(t, 0, dev(x, y, 1 - c), sibling).wait_recv()
            for j, chip in enumerate(chips):
                copy(t, 4 + j, dev(*chip, 1 - c), sibling).wait_recv()
        for cp in first + passed:
            cp.wait_send()
        for mine in started:
            mine.wait()

    space = pltpu.VMEM if vmem else pl.ANY
    return pl.pallas_call(
        body, name=name,
        in_specs=[pl.BlockSpec(memory_space=space)] * nt,
        out_specs=[pl.BlockSpec(memory_space=space)] * nt,
        out_shape=out_shapes,
        scratch_shapes=[pltpu.SemaphoreType.DMA((7 * nt,)), pltpu.SemaphoreType.DMA((7 * nt,)),
                        pltpu.SemaphoreType.DMA((nt,))],
        compiler_params=pltpu.CompilerParams(vmem_limit_bytes=VMEM_LIMIT, has_side_effects=True),
    )(*shards)


def _pair_exchange(grads, kinds, name):
    nt = len(grads)
    shard_shapes = []
    for g, kind in zip(grads, kinds):
        shard_shapes.append((g.shape[0], g.shape[1] // N_DEV) if kind == "col" else (g.shape[0] // N_DEV, g.shape[1]))

    def body(*refs):
        srcs, lands = refs[:nt], refs[nt:2 * nt]
        send_sems, recv_sems = refs[2 * nt:]
        x, y, c = lax.axis_index("x"), lax.axis_index("y"), lax.axis_index("c")
        copies = []
        for t in range(nt):
            n = shard_shapes[t][1] if kinds[t] == "col" else shard_shapes[t][0]
            for k in range(N_CHIP):
                cp = pltpu.make_async_remote_copy(
                    src_ref=_shard_ref(srcs[t], kinds[t], 2 * k + 1 - c, n), dst_ref=lands[t].at[k],
                    send_sem=send_sems.at[N_CHIP * t + k], recv_sem=recv_sems.at[N_CHIP * t + k],
                    device_id=(x, y, 1 - c), device_id_type=MESH)
                cp.start()
                copies.append(cp)
        for cp in copies:
            cp.wait()

    return pl.pallas_call(
        body, name=name,
        in_specs=[pl.BlockSpec(memory_space=pl.ANY)] * nt,
        out_specs=[pl.BlockSpec(memory_space=pl.ANY)] * nt,
        out_shape=[jax.ShapeDtypeStruct((N_CHIP,) + s, g.dtype) for s, g in zip(shard_shapes, grads)],
        scratch_shapes=[pltpu.SemaphoreType.DMA((N_CHIP * nt,)), pltpu.SemaphoreType.DMA((N_CHIP * nt,))],
        compiler_params=pltpu.CompilerParams(has_side_effects=True),
    )(*grads)


def _pair_sum(grad, land, kind, c_arr, name):
    _, r, w = land.shape
    rb = _tile(r, 256)
    nrb = r // rb

    def body(c_ref, g_ref, l_ref, o_ref):
        o_ref[...] = (g_ref[...].astype(F32) + l_ref[...].astype(F32)).astype(o_ref.dtype)

    if kind == "col":
        g_spec = pl.BlockSpec((rb, w), lambda k, i, c: (i, 2 * k + c[0]))
    else:
        g_spec = pl.BlockSpec((rb, w), lambda k, i, c: ((2 * k + c[0]) * nrb + i, 0))
    return pl.pallas_call(
        body, name=name,
        grid_spec=pltpu.PrefetchScalarGridSpec(
            num_scalar_prefetch=1, grid=(N_CHIP, nrb),
            in_specs=[g_spec, pl.BlockSpec((None, rb, w), lambda k, i, c: (k, i, 0))],
            out_specs=pl.BlockSpec((None, rb, w), lambda k, i, c: (k, i, 0))),
        out_shape=jax.ShapeDtypeStruct(land.shape, BF16),
        compiler_params=_cparams(("parallel", "parallel")),
    )(c_arr, grad, land)


def _adam(w, g, m, v):
    m = ADAM_B1 * m + (1.0 - ADAM_B1) * g
    v = ADAM_B2 * v + (1.0 - ADAM_B2) * jnp.square(g)
    m_hat = m / (1.0 - ADAM_B1 ** ADAM_STEP)
    v_hat = v / (1.0 - ADAM_B2 ** ADAM_STEP)
    delta = -ADAM_LR * (m_hat / (jnp.sqrt(v_hat) + ADAM_EPS) + ADAM_WD * w)
    return delta, m, v


def _sum_adam(part, land, w, m, v, chip_arr, name):
    r, wd = w.shape
    rb = _tile(r, 128)

    def body(k_ref, own, r1, r2, r3, w_ref, m_ref, v_ref, g_out, d_out, m_out, v_out):
        g = own[...].astype(F32) + r1[...].astype(F32) + r2[...].astype(F32) + r3[...].astype(F32)
        d, mn, vn = _adam(w_ref[...], g, m_ref[...], v_ref[...])
        g_out[...] = g
        d_out[...] = d
        m_out[...] = mn
        v_out[...] = vn

    def slot(off):
        return pl.BlockSpec((None, rb, wd), lambda i, k: ((k[0] + off) % N_CHIP, i, 0))

    plain = pl.BlockSpec((rb, wd), lambda i, k: (i, 0))
    return pl.pallas_call(
        body, name=name,
        grid_spec=pltpu.PrefetchScalarGridSpec(
            num_scalar_prefetch=1, grid=(r // rb,),
            in_specs=[slot(0), slot(1), slot(2), slot(3), plain, plain, plain],
            out_specs=[plain] * 4),
        out_shape=[jax.ShapeDtypeStruct(w.shape, F32)] * 4,
        compiler_params=_cparams(("parallel",)),
    )(chip_arr, part, land, land, land, w, m, v)


def _small_sum(gathered, name):
    _, r, w = gathered.shape

    def body(g_ref, o_ref):
        acc = g_ref[0]
        for d in range(1, N_DEV):
            acc = acc + g_ref[d]
        o_ref[...] = acc

    return pl.pallas_call(body, name=name, out_shape=jax.ShapeDtypeStruct((r, w), F32))(gathered)


def _small_adam(ws, gs, ms, vs, name):
    n = len(ws)

    def body(*refs):
        ins, outs = refs[:4 * n], refs[4 * n:]
        for t in range(n):
            d, mn, vn = _adam(ins[t][...], ins[n + t][...], ins[2 * n + t][...], ins[3 * n + t][...])
            outs[t][...] = d
            outs[n + t][...] = mn
            outs[2 * n + t][...] = vn

    shapes = [jax.ShapeDtypeStruct(w.shape, F32) for w in ws]
    return pl.pallas_call(body, name=name, out_shape=shapes * 3)(*ws, *gs, *ms, *vs)


def _mixer_fwd(x, shards, ln_v_g, ln_v_b, w_s, b_s3, conv_w, conv_b):
    t_len, d = x.shape
    hd = d // HEADS
    tm = _tile(t_len, 512)
    nt = t_len // tm
    assert nt >= 2
    w8 = shards[0].shape[0]
    bps = w8 // hd
    r8 = shards[1].shape[0]
    nq = N_IN * HEADS
    n_blocks = nq + 3 * N_DEV
    groups = [[("in", HEADS * b + k) for b in range(N_MIX)] for k in range(HEADS)]
    groups.append([("in", q) for q in range(N_MIX * HEADS, nq)] + [(t, s) for t in (1, 2, 3) for s in range(N_DEV)])

    def owner_of(blk):
        return blk[1] // bps if blk[0] == "in" else blk[1]

    send_step = {}
    for s in range(N_DEV):
        mine = [blk for grp in groups for blk in grp if owner_of(blk) == s]
        for pos, blk in enumerate(mine):
            send_step[blk] = 0 if pos < FIRST_SENDS else 1 + (pos - FIRST_SENDS) // SENDS_PER_GROUP
    assert max(send_step.values()) < HEADS
    assert all(send_step[blk] <= max(k - 1, 0) for k in range(HEADS) for blk in groups[k])
    tail_pass_step = nt - 6 if nt >= 8 else nt - 2

    def body(x_ref, sh_in, sh_oa, sh_ob, sh_out, lng, lnb, ws_ref, bs_ref, cw_ref, cb_ref,
             save_ref, ya_ref, yb_ref, f_in, f_oa, f_ob, f_out,
             hbuf, wbuf, recv_sems, own_sems, fwd_sems, local_sems, load_sems):
        g = pl.program_id(0)
        i = pl.program_id(1)
        x, y, c = lax.axis_index("x"), lax.axis_index("y"), lax.axis_index("c")
        sibling = (x, y, 1 - c)
        chips = [(1 - x, y), (x, 1 - y), (1 - x, 1 - y)]
        shard_refs = (sh_in, sh_oa, sh_ob, sh_out)
        fulls = (f_in, f_oa, f_ob, f_out)

        def tensor(blk):
            return 0 if blk[0] == "in" else blk[0]

        def owner(blk):
            s = owner_of(blk)
            return s // 4, (s // 2) % 2, s % 2

        def bid(blk):
            return blk[1] if blk[0] == "in" else nq + (blk[0] - 1) * N_DEV + blk[1]

        def region(blk):
            if blk[0] == "in":
                return f_in.at[pl.ds(blk[1] * hd, hd), :]
            return fulls[blk[0]].at[pl.ds(blk[1] * r8, r8), :]

        def own_src(blk):
            if blk[0] == "in":
                return sh_in.at[pl.ds((blk[1] % bps) * hd, hd), :]
            return shard_refs[blk[0]]

        def rcopy(blk, to, send_sem, own):
            return pltpu.make_async_remote_copy(
                src_ref=own_src(blk) if own else region(blk), dst_ref=region(blk),
                send_sem=send_sem, recv_sem=recv_sems.at[bid(blk)], device_id=to, device_id_type=MESH)

        def send_own(blk):
            ox, oy, oc = owner(blk)

            @pl.when((x == ox) & (y == oy) & (c == oc))
            def _():
                rcopy(blk, sibling, own_sems.at[tensor(blk)], True).start()
                for chip in chips:
                    rcopy(blk, (*chip, c), own_sems.at[tensor(blk)], True).start()

        def pass_on(blk):
            ox, oy, oc = owner(blk)

            @pl.when(((x != ox) | (y != oy)) & (c == oc))
            def _():
                rcopy(blk, sibling, fwd_sems.at[tensor(blk)], False).wait_recv()
                rcopy(blk, sibling, fwd_sems.at[tensor(blk)], False).start()

        def wait_from_sibling(blk):
            @pl.when(c != owner(blk)[2])
            def _():
                rcopy(blk, sibling, fwd_sems.at[tensor(blk)], False).wait_recv()

        def local_copy(t):
            me = 4 * x + 2 * y + c
            return pltpu.make_async_copy(shard_refs[t], _shard_ref(fulls[t], "row", me, w8 if t == 0 else r8),
                                         local_sems.at[t])

        first = (g == 0) & (i == 0)

        @pl.when(first)
        def _():
            for t in range(4):
                local_copy(t).start()
            for grp in groups:
                for blk in grp:
                    if send_step[blk] == 0:
                        send_own(blk)
            for blk in groups[0]:
                pass_on(blk)

        @pl.when(i == 0)
        def _():
            for k in range(1, HEADS):
                @pl.when(g == k)
                def _(k=k):
                    for grp in groups:
                        for blk in grp:
                            if send_step[blk] == k:
                                send_own(blk)

        @pl.when(i == nt - 2)
        def _():
            for k in range(HEADS - 1):
                @pl.when(g == k)
                def _(k=k):
                    for blk in groups[k + 1]:
                        pass_on(blk)

        @pl.when((g == HEADS - 1) & (i == tail_pass_step))
        def _():
            for blk in groups[HEADS]:
                pass_on(blk)

        def fetch_weights(k):
            for blk in groups[k]:
                wait_from_sibling(blk)
            for b, blk in enumerate(groups[k]):
                ox, oy, oc = owner(blk)
                mine = (x == ox) & (y == oy) & (c == oc)

                @pl.when(mine)
                def _(b=b, blk=blk):
                    pltpu.make_async_copy(own_src(blk), wbuf.at[k % 2, b], load_sems.at[b]).start()

                @pl.when(jnp.logical_not(mine))
                def _(b=b, blk=blk):
                    pltpu.make_async_copy(region(blk), wbuf.at[k % 2, b], load_sems.at[b]).start()

        @pl.when(i == 0)
        def _():
            for k in range(HEADS):
                @pl.when(g == k)
                def _(k=k):
                    if k == 0:
                        fetch_weights(0)
                    for b, blk in enumerate(groups[k]):
                        pltpu.make_async_copy(region(blk), wbuf.at[k % 2, b], load_sems.at[b]).wait()

        @pl.when(i == nt - 1)
        def _():
            for k in range(1, HEADS):
                @pl.when(g == k - 1)
                def _(k=k):
                    fetch_weights(k)

        xt = x_ref[...].astype(BF16)
        wslot = g % 2
        u = _dot_nt(xt, wbuf[wslot, 0])
        v = _dot_nt(xt, wbuf[wslot, 1])
        z = _dot_nt(xt, wbuf[wslot, 2])
        save_ref[0] = u.astype(BF16)
        save_ref[1] = v.astype(BF16)
        save_ref[2] = z.astype(BF16)
        gu, _ = _gelu(u)
        gv, _ = _gelu(v)
        mu = jnp.mean(gv, axis=-1, keepdims=True)
        dv = gv - mu
        var = jnp.mean(dv * dv, axis=-1, keepdims=True)
        vn = (dv * lax.rsqrt(var + LN_EPS) * lng[...] + lnb[...]).astype(BF16)
        sz, _ = _silu(z)
        gate = gu * sz
        wm = jnp.where(_tril_mask(), ws_ref[0], 0.0).astype(BF16)
        bs = bs_ref[0]
        for ck in range(tm // CHUNK):
            rows = slice(ck * CHUNK, (ck + 1) * CHUNK)
            mixed = _dot(wm, vn[rows]) + bs
            ya_ref[rows, :] = (gate[rows] * mixed).astype(BF16)

        xbv = _dot_nt(xt, wbuf[wslot, 3])
        cbv = _dot_nt(xt, wbuf[wslot, 4])
        bbv = _dot_nt(xt, wbuf[wslot, 5])
        zbv = _dot_nt(xt, wbuf[wslot, 6])
        save_ref[3] = xbv.astype(BF16)
        save_ref[4] = cbv.astype(BF16)
        save_ref[5] = bbv.astype(BF16)
        save_ref[6] = zbv.astype(BF16)
        h = cbv * xbv

        @pl.when(i == 0)
        def _():
            hbuf[0:SUBLANES, :] = jnp.zeros((SUBLANES, hd), F32)

        hbuf[SUBLANES:SUBLANES + tm, :] = h
        h1 = hbuf[SUBLANES - 1:SUBLANES - 1 + tm, :]
        h2 = hbuf[SUBLANES - 2:SUBLANES - 2 + tm, :]
        conv = cb_ref[...] + cw_ref[0:1, :] * h2 + cw_ref[1:2, :] * h1 + cw_ref[2:3, :] * h
        hbuf[0:SUBLANES, :] = h[tm - SUBLANES:tm, :]
        save_ref[7] = conv.astype(BF16)
        szb, _ = _silu(zbv)
        yb_ref[...] = (bbv * conv * szb).astype(BF16)

        @pl.when((g == HEADS - 1) & (i == nt - 1))
        def _():
            for blk in groups[HEADS]:
                wait_from_sibling(blk)
            for t in range(4):
                local_copy(t).wait()
            for t in range(4):
                n = w8 if t == 0 else r8
                own_all, fwd_all = fulls[t].at[pl.ds(0, 4 * n), :], fulls[t].at[pl.ds(0, 3 * n), :]
                for ref, sem in ((own_all, own_sems.at[t]), (fwd_all, fwd_sems.at[t])):
                    pltpu.make_async_remote_copy(src_ref=ref, dst_ref=ref, send_sem=sem, recv_sem=sem,
                                                 device_id=sibling, device_id_type=MESH).wait_send()

    vec = pl.BlockSpec((1, hd), lambda g, i: (0, g))
    hbm = pl.BlockSpec(memory_space=pl.ANY)
    return pl.pallas_call(
        body, name="mixer_fwd", grid=(HEADS, nt),
        in_specs=[pl.BlockSpec((tm, d), lambda g, i: (i, 0)), hbm, hbm, hbm, hbm,
                  vec, vec,
                  pl.BlockSpec((1, CHUNK, CHUNK), lambda g, i: (g, 0, 0)),
                  pl.BlockSpec((1, CHUNK, 1), lambda g, i: (g, 0, 0)),
                  pl.BlockSpec((3, hd), lambda g, i: (0, g)),
                  vec],
        out_specs=[pl.BlockSpec((N_SAVE, tm, hd), lambda g, i: (0, i, g)),
                   pl.BlockSpec((tm, hd), lambda g, i: (i, g)),
                   pl.BlockSpec((tm, hd), lambda g, i: (i, g)),
                   hbm, hbm, hbm, hbm],
        out_shape=[jax.ShapeDtypeStruct((N_SAVE, t_len, d), BF16),
                   jax.ShapeDtypeStruct((t_len, d), BF16),
                   jax.ShapeDtypeStruct((t_len, d), BF16),
                   jax.ShapeDtypeStruct((N_DEV * w8, d), BF16),
                   jax.ShapeDtypeStruct((d, d), BF16), jax.ShapeDtypeStruct((d, d), BF16),
                   jax.ShapeDtypeStruct((d, d), BF16)],
        scratch_shapes=[pltpu.VMEM((SUBLANES + tm, hd), F32), pltpu.VMEM((2, N_MIX, hd, d), BF16),
                        pltpu.SemaphoreType.DMA((n_blocks,)), pltpu.SemaphoreType.DMA((4,)),
                        pltpu.SemaphoreType.DMA((4,)), pltpu.SemaphoreType.DMA((4,)),
                        pltpu.SemaphoreType.DMA((N_MIX,))],
        compiler_params=_cparams(("arbitrary", "arbitrary")),
    )(x, *shards, ln_v_g, ln_v_b, w_s, b_s3, conv_w, conv_b)


def _merge_fwd(x, ya, yb, w_in, w_oa, w_ob, b_gate):
    t_len, d = x.shape
    tm = _tile(t_len, 512)
    tn = _tile(d, 512)
    nj = d // tn

    def body(x_ref, ya_ref, yb_ref, wga, wgb, woa, wob, bga, bgb, sa_ref, sb_ref, pa_ref, pb_ref, mg_ref):
        xt = x_ref[...].astype(BF16)
        sa = jax.nn.sigmoid(_dot_nt(xt, wga[...]) + bga[...])
        sb = jax.nn.sigmoid(_dot_nt(xt, wgb[...]) + bgb[...])
        pa = _dot(ya_ref[...], woa[...])
        pb = _dot(yb_ref[...], wob[...])
        sa_ref[...] = sa.astype(BF16)
        sb_ref[...] = sb.astype(BF16)
        pa_ref[...] = pa.astype(BF16)
        pb_ref[...] = pb.astype(BF16)
        mg_ref[...] = (sa * pa + sb * pb).astype(BF16)

    row = pl.BlockSpec((tm, d), lambda j, i: (i, 0))
    out = pl.BlockSpec((tm, tn), lambda j, i: (i, j))
    return pl.pallas_call(
        body, name="merge_fwd", grid=(nj, t_len // tm),
        in_specs=[row, row, row,
                  pl.BlockSpec((tn, d), lambda j, i: (7 * nj + j, 0)),
                  pl.BlockSpec((tn, d), lambda j, i: (8 * nj + j, 0)),
                  pl.BlockSpec((d, tn), lambda j, i: (0, j)),
                  pl.BlockSpec((d, tn), lambda j, i: (0, j)),
                  pl.BlockSpec((1, tn), lambda j, i: (0, j)),
                  pl.BlockSpec((1, tn), lambda j, i: (0, nj + j))],
        out_specs=[out] * 5,
        out_shape=[jax.ShapeDtypeStruct((t_len, d), BF16)] * 5,
        compiler_params=_cparams(("parallel", "arbitrary")),
    )(x, ya, yb, w_in, w_in, w_oa, w_ob, b_gate, b_gate)


def _out_ln_loss(merged, w_out, x, target, ln_g, ln_b):
    t_len, d = x.shape
    tm = _tile(t_len, 256)
    nt = t_len // tm

    def body(mg_ref, w_ref, x_ref, t_ref, g_ref, b_ref, dz_ref, gx_ref, xb_ref, glg_ref, glb_ref, ls_ref, a_g, a_b, a_l):
        i = pl.program_id(0)
        xb_ref[...] = x_ref[...].astype(BF16)

        @pl.when(i == 0)
        def _():
            a_g[...] = jnp.zeros_like(a_g)
            a_b[...] = jnp.zeros_like(a_b)
            a_l[...] = jnp.zeros_like(a_l)

        zres = DN_ALPHA * x_ref[...] + _dot(mg_ref[...], w_ref[...])
        mu = jnp.mean(zres, axis=-1, keepdims=True)
        dc = zres - mu
        var = jnp.mean(dc * dc, axis=-1, keepdims=True)
        rstd = lax.rsqrt(var + LN_EPS)
        xhat = dc * rstd
        g = g_ref[...]
        err = xhat * g + b_ref[...] - t_ref[...]
        dy = err * (1.0 / d)
        a_l[...] += _fold8(err * err)
        a_g[...] += _fold8(dy * xhat)
        a_b[...] += _fold8(dy)
        dxh = dy * g
        m1 = jnp.mean(dxh, axis=-1, keepdims=True)
        m2 = jnp.mean(dxh * xhat, axis=-1, keepdims=True)
        dz = rstd * (dxh - m1 - xhat * m2)
        dz_ref[...] = dz.astype(BF16)
        gx_ref[...] = DN_ALPHA * dz

        @pl.when(i == nt - 1)
        def _():
            glg_ref[...] = jnp.sum(a_g[...], axis=0, keepdims=True)
            glb_ref[...] = jnp.sum(a_b[...], axis=0, keepdims=True)
            ls_ref[...] = jnp.sum(a_l[...], axis=0, keepdims=True)

    row = pl.BlockSpec((tm, d), lambda i: (i, 0))
    vec = pl.BlockSpec((1, d), lambda i: (0, 0))
    return pl.pallas_call(
        body, name="out_ln_loss", grid=(nt,),
        in_specs=[row, pl.BlockSpec((d, d), lambda i: (0, 0)), row, row, vec, vec],
        out_specs=[row, row, row, vec, vec, vec],
        out_shape=[jax.ShapeDtypeStruct((t_len, d), BF16), jax.ShapeDtypeStruct((t_len, d), F32),
                   jax.ShapeDtypeStruct((t_len, d), BF16)] + [jax.ShapeDtypeStruct((1, d), F32)] * 3,
        scratch_shapes=[pltpu.VMEM((SUBLANES, d), F32)] * 3,
        compiler_params=_cparams(("arbitrary",)),
    )(merged, w_out, x, target, ln_g, ln_b)


def _merge_bwd(dz, w_out, sa, sb, pa, pb):
    t_len, d = dz.shape
    tm = _tile(t_len, 256)
    nt = t_len // tm

    def body(dz_ref, w_ref, sa_ref, sb_ref, pa_ref, pb_ref, da_ref, db_ref, dg_ref, ga_ref, gb_ref, acc_a, acc_b):
        i = pl.program_id(0)

        @pl.when(i == 0)
        def _():
            acc_a[...] = jnp.zeros_like(acc_a)
            acc_b[...] = jnp.zeros_like(acc_b)

        dm = _dot_nt(dz_ref[...], w_ref[...])
        sa = sa_ref[...].astype(F32)
        sb = sb_ref[...].astype(F32)
        da = dm * sa
        db = dm * sb
        da_ref[...] = da.astype(BF16)
        db_ref[...] = db.astype(BF16)
        dga = da * pa_ref[...].astype(F32) * (1.0 - sa)
        dgb = db * pb_ref[...].astype(F32) * (1.0 - sb)
        dg_ref[0] = dga.astype(BF16)
        dg_ref[1] = dgb.astype(BF16)
        acc_a[...] += _fold8(dga)
        acc_b[...] += _fold8(dgb)

        @pl.when(i == nt - 1)
        def _():
            ga_ref[...] = jnp.sum(acc_a[...], axis=0, keepdims=True)
            gb_ref[...] = jnp.sum(acc_b[...], axis=0, keepdims=True)

    row = pl.BlockSpec((tm, d), lambda i: (i, 0))
    vec = pl.BlockSpec((1, d), lambda i: (0, 0))
    return pl.pallas_call(
        body, name="merge_bwd", grid=(nt,),
        in_specs=[row, pl.BlockSpec((d, d), lambda i: (0, 0)), row, row, row, row],
        out_specs=[row, row, pl.BlockSpec((2, tm, d), lambda i: (0, i, 0)), vec, vec],
        out_shape=[jax.ShapeDtypeStruct((t_len, d), BF16)] * 2 + [jax.ShapeDtypeStruct((2, t_len, d), BF16)]
        + [jax.ShapeDtypeStruct((1, d), F32)] * 2,
        scratch_shapes=[pltpu.VMEM((SUBLANES, d), F32)] * 2,
        compiler_params=_cparams(("arbitrary",)),
    )(dz, w_out, sa, sb, pa, pb)


def _grad_w(a, b, name, exchange=None):
    t_len, m = a.shape
    n = b.shape[1]
    tm, tn, tk = _tile(m, MM_TILE), _tile(n, MM_TILE), _tile(t_len, MM_DEPTH_TOKENS)
    ni, nj, nk = m // tm, n // tn, t_len // tk

    def body(a_ref, b_ref, *rest):
        if exchange is None:
            o_ref, acc = rest
        else:
            src, o_ref, land, acc, send_sems, recv_sems = rest
        i, j, k = pl.program_id(0), pl.program_id(1), pl.program_id(2)

        def copies():
            x, y, c = lax.axis_index("x"), lax.axis_index("y"), lax.axis_index("c")
            w = exchange.shape[1] // N_DEV
            return [pltpu.make_async_remote_copy(
                src_ref=_shard_ref(src, "col", 2 * s + 1 - c, w), dst_ref=land.at[s],
                send_sem=send_sems.at[s], recv_sem=recv_sems.at[s],
                device_id=(x, y, 1 - c), device_id_type=MESH) for s in range(N_CHIP)]

        if exchange is not None:
            @pl.when((i == 0) & (j == 0) & (k == 0))
            def _():
                for cp in copies():
                    cp.start()

        @pl.when(k == 0)
        def _():
            acc[...] = jnp.zeros_like(acc)

        acc[...] += _dot_tn(a_ref[...], b_ref[...])

        @pl.when(k == nk - 1)
        def _():
            o_ref[...] = acc[...].astype(BF16)

        if exchange is not None:
            @pl.when((i == ni - 1) & (j == nj - 1) & (k == nk - 1))
            def _():
                for cp in copies():
                    cp.wait()

    in_specs = [pl.BlockSpec((tk, tm), lambda i, j, k: (k, i)), pl.BlockSpec((tk, tn), lambda i, j, k: (k, j))]
    out_specs = [pl.BlockSpec((tm, tn), lambda i, j, k: (i, j))]
    out_shape = [jax.ShapeDtypeStruct((m, n), BF16)]
    scratch = [pltpu.VMEM((tm, tn), F32)]
    args = [a, b]
    if exchange is not None:
        hbm = pl.BlockSpec(memory_space=pl.ANY)
        in_specs.append(hbm)
        out_specs.append(hbm)
        out_shape.append(jax.ShapeDtypeStruct((N_CHIP, exchange.shape[0], exchange.shape[1] // N_DEV), exchange.dtype))
        scratch += [pltpu.SemaphoreType.DMA((N_CHIP,)), pltpu.SemaphoreType.DMA((N_CHIP,))]
        args.append(exchange)
    outs = pl.pallas_call(
        body, name=name, grid=(ni, nj, nk),
        in_specs=in_specs, out_specs=out_specs, out_shape=out_shape, scratch_shapes=scratch,
        compiler_params=_cparams(("arbitrary", "arbitrary", "arbitrary")),
    )(*args)
    return outs[0] if exchange is None else (outs[0], outs[1])


def _mixer_bwd(da, db, w_oa, w_ob, saved, dgate, ln_v_g, ln_v_b, w_s, b_s3, conv_w):
    t_len, d = da.shape
    hd = d // HEADS
    tm = _tile(t_len, 512)
    nt = t_len // tm

    def body(da_ref, db_ref, woa, wob, sv, dgt, lng, lnb, ws_ref, bs_ref, cw_ref,
             dp_ref, gws_ref, gbs_ref, glg_ref, glb_ref, gcw_ref, gcb_ref,
             dbuf, carry, a_ws, a_bs, a_lg, a_lb, a_c0, a_c1, a_c2, a_cb):
        i = pl.program_id(1)

        @pl.when(i == 0)
        def _():
            carry[...] = jnp.zeros_like(carry)
            for a in (a_ws, a_bs, a_lg, a_lb, a_c0, a_c1, a_c2, a_cb):
                a[...] = jnp.zeros_like(a)

        dya_t = _dot_nt(da_ref[...], woa[...])
        dyb_t = _dot_nt(db_ref[...], wob[...])
        g = lng[...]
        beta = lnb[...]
        wm = jnp.where(_tril_mask(), ws_ref[0], 0.0).astype(BF16)
        bs = bs_ref[0]
        w0, w1, w2 = cw_ref[0:1, :], cw_ref[1:2, :], cw_ref[2:3, :]
        dbuf[tm:tm + SUBLANES, :] = carry[...]
        gws = a_ws[...]
        gbs = a_bs[...]
        lg, lb = a_lg[...], a_lb[...]
        c0, c1, c2, cb = a_c0[...], a_c1[...], a_c2[...], a_cb[...]
        for ck in reversed(range(tm // CHUNK)):
            r0 = ck * CHUNK
            rows = slice(r0, r0 + CHUNK)
            dya = dya_t[rows]
            u = sv[0, rows, :].astype(F32)
            v = sv[1, rows, :].astype(F32)
            z = sv[2, rows, :].astype(F32)
            gu, gu_grad = _gelu(u)
            gv, gv_grad = _gelu(v)
            mu = jnp.mean(gv, axis=-1, keepdims=True)
            dvc = gv - mu
            var = jnp.mean(dvc * dvc, axis=-1, keepdims=True)
            rstd = lax.rsqrt(var + LN_EPS)
            vhat = dvc * rstd
            vn = (vhat * g + beta).astype(BF16)
            sz, sz_grad = _silu(z)
            t1 = dya * sz
            dmixed = t1 * gu
            dmixed_b = dmixed.astype(BF16)
            mixed = _dot(wm, vn) + bs
            gws = gws + _dot_nt(dmixed_b, vn)
            gbs = gbs + dmixed
            dvn = _dot_tn(wm, dmixed_b)
            dp_ref[0, rows, :] = (t1 * mixed * gu_grad).astype(BF16)
            dp_ref[2, rows, :] = (dya * gu * mixed * sz_grad).astype(BF16)
            lg = lg + _fold8(dvn * vhat)
            lb = lb + _fold8(dvn)
            dvh = dvn * g
            m1 = jnp.mean(dvh, axis=-1, keepdims=True)
            m2 = jnp.mean(dvh * vhat, axis=-1, keepdims=True)
            dp_ref[1, rows, :] = (rstd * (dvh - m1 - vhat * m2) * gv_grad).astype(BF16)
            dyb = dyb_t[rows]
            xbv = sv[3, rows, :].astype(F32)
            cbv = sv[4, rows, :].astype(F32)
            bbv = sv[5, rows, :].astype(F32)
            zbv = sv[6, rows, :].astype(F32)
            conv = sv[7, rows, :].astype(F32)
            szb, szb_grad = _silu(zbv)
            dp_ref[5, rows, :] = (dyb * conv * szb).astype(BF16)
            dp_ref[6, rows, :] = (dyb * bbv * conv * szb_grad).astype(BF16)
            dconv = dyb * bbv * szb
            dbuf[r0:r0 + CHUNK, :] = dconv
            dc1 = dbuf[r0 + 1:r0 + 1 + CHUNK, :]
            dc2 = dbuf[r0 + 2:r0 + 2 + CHUNK, :]
            if ck == 0:
                carry[...] = dconv[0:SUBLANES, :]
            h = cbv * xbv
            c2 = c2 + _fold8(dconv * h)
            c1 = c1 + _fold8(dc1 * h)
            c0 = c0 + _fold8(dc2 * h)
            cb = cb + _fold8(dconv)
            dh = w2 * dconv + w1 * dc1 + w0 * dc2
            dp_ref[3, rows, :] = (dh * cbv).astype(BF16)
            dp_ref[4, rows, :] = (dh * xbv).astype(BF16)
            dp_ref[7, rows, :] = dgt[0, rows, :]
            dp_ref[8, rows, :] = dgt[1, rows, :]
        a_ws[...] = gws
        a_bs[...] = gbs
        a_lg[...], a_lb[...] = lg, lb
        a_c0[...], a_c1[...], a_c2[...], a_cb[...] = c0, c1, c2, cb

        @pl.when(i == nt - 1)
        def _():
            gws_ref[0] = jnp.where(_tril_mask(), a_ws[...], 0.0)
            gbs_ref[0] = jnp.sum(a_bs[...], axis=1, keepdims=True)
            glg_ref[...] = jnp.sum(a_lg[...], axis=0, keepdims=True)
            glb_ref[...] = jnp.sum(a_lb[...], axis=0, keepdims=True)
            gcw_ref[0:1, :] = jnp.sum(a_c0[...], axis=0, keepdims=True)
            gcw_ref[1:2, :] = jnp.sum(a_c1[...], axis=0, keepdims=True)
            gcw_ref[2:3, :] = jnp.sum(a_c2[...], axis=0, keepdims=True)
            gcb_ref[...] = jnp.sum(a_cb[...], axis=0, keepdims=True)

    def rev(i):
        return nt - 1 - i

    row = pl.BlockSpec((tm, d), lambda g, i: (rev(i), 0))
    wrow = pl.BlockSpec((hd, d), lambda g, i: (g, 0))
    vec = pl.BlockSpec((1, hd), lambda g, i: (0, g))
    acc8 = pltpu.VMEM((SUBLANES, hd), F32)
    return pl.pallas_call(
        body, name="mixer_bwd", grid=(HEADS, nt),
        in_specs=[row, row, wrow, wrow,
                  pl.BlockSpec((N_SAVE, tm, hd), lambda g, i: (0, rev(i), g)),
                  pl.BlockSpec((2, tm, hd), lambda g, i: (0, rev(i), g)),
                  vec, vec,
                  pl.BlockSpec((1, CHUNK, CHUNK), lambda g, i: (g, 0, 0)),
                  pl.BlockSpec((1, CHUNK, 1), lambda g, i: (g, 0, 0)),
                  pl.BlockSpec((3, hd), lambda g, i: (0, g))],
        out_specs=[pl.BlockSpec((N_IN, tm, hd), lambda g, i: (0, rev(i), g)),
                   pl.BlockSpec((1, CHUNK, CHUNK), lambda g, i: (g, 0, 0)),
                   pl.BlockSpec((1, CHUNK, 1), lambda g, i: (g, 0, 0)),
                   vec, vec,
                   pl.BlockSpec((3, hd), lambda g, i: (0, g)),
                   vec],
        out_shape=[jax.ShapeDtypeStruct((N_IN, t_len, d), BF16),
                   jax.ShapeDtypeStruct((HEADS, CHUNK, CHUNK), F32),
                   jax.ShapeDtypeStruct((HEADS, CHUNK, 1), F32),
                   jax.ShapeDtypeStruct((1, d), F32), jax.ShapeDtypeStruct((1, d), F32),
                   jax.ShapeDtypeStruct((3, d), F32), jax.ShapeDtypeStruct((1, d), F32)],
        scratch_shapes=[pltpu.VMEM((tm + SUBLANES, hd), F32), acc8,
                        pltpu.VMEM((CHUNK, CHUNK), F32), pltpu.VMEM((CHUNK, hd), F32),
                        acc8, acc8, acc8, acc8, acc8, acc8],
        compiler_params=_cparams(("parallel", "arbitrary")),
    )(da, db, w_oa, w_ob, saved, dgate, ln_v_g, ln_v_b, w_s, b_s3, conv_w)


def _grad_w_in(xb, dp):
    t_len, d = xb.shape
    tm, tn, tk = _tile(d, MM_TILE), _tile(d, MM_TILE), _tile(t_len, MM_DEPTH_TOKENS)
    nj = d // tn
    nk = t_len // tk

    def body(a_ref, b_ref, o_ref, acc):
        k = pl.program_id(2)

        @pl.when(k == 0)
        def _():
            acc[...] = jnp.zeros_like(acc)

        acc[...] += _dot_tn(a_ref[...], b_ref[...])

        @pl.when(k == nk - 1)
        def _():
            o_ref[...] = acc[...].astype(BF16)

    return pl.pallas_call(
        body, name="grad_w_in", grid=(d // tm, N_IN * nj, nk),
        in_specs=[pl.BlockSpec((tk, tm), lambda i, j, k: (k, i)),
                  pl.BlockSpec((None, tk, tn), lambda i, j, k: (j // nj, k, j % nj))],
        out_specs=pl.BlockSpec((tm, tn), lambda i, j, k: (i, j)),
        out_shape=jax.ShapeDtypeStruct((d, N_IN * d), BF16),
        scratch_shapes=[pltpu.VMEM((tm, tn), F32)],
        compiler_params=_cparams(("parallel", "parallel", "arbitrary")),
    )(xb, dp)


def _grad_x_tail(dp, w_in, partial, slots):
    _, t_len, d = dp.shape
    lo, hi = slots
    tm, tn, tk = _tile(t_len, MM_TILE), _tile(d, MM_TILE), _tile(d, MM_DEPTH)
    nkb = d // tk
    nk = (hi - lo) * nkb

    def body(a_ref, b_ref, r_ref, o_ref, acc):
        k = pl.program_id(2)

        @pl.when(k == 0)
        def _():
            acc[...] = r_ref[...]

        acc[...] += _dot(a_ref[...], b_ref[...])

        @pl.when(k == nk - 1)
        def _():
            o_ref[...] = acc[...]

    return pl.pallas_call(
        body, name="grad_x_tail", grid=(t_len // tm, d // tn, nk),
        in_specs=[pl.BlockSpec((None, tm, tk), lambda i, j, k: (lo + k // nkb, i, k % nkb)),
                  pl.BlockSpec((tk, tn), lambda i, j, k: (lo * nkb + k, j)),
                  pl.BlockSpec((tm, tn), lambda i, j, k: (i, j))],
        out_specs=pl.BlockSpec((tm, tn), lambda i, j, k: (i, j)),
        out_shape=jax.ShapeDtypeStruct((t_len, d), F32),
        scratch_shapes=[pltpu.VMEM((tm, tn), F32)],
        compiler_params=_cparams(("parallel", "parallel", "arbitrary")),
    )(dp, w_in, partial)


def _grad_x(dp, w_in, gx_direct, parts, packed, slots):
    _, t_len, d = dp.shape
    tm, tn, tk = _tile(t_len, MM_TILE), _tile(d, MM_TILE), _tile(d, MM_DEPTH)
    nkb = d // tk
    lo, hi = slots
    nk = (hi - lo) * nkb
    ni, nj = t_len // tm, d // tn
    n_parts = len(parts)

    def body(a_ref, b_ref, r_ref, *rest):
        srcs = rest[:n_parts]
        pk_ref = rest[n_parts]
        o_ref = rest[n_parts + 1]
        lands = rest[n_parts + 2:2 * n_parts + 2]
        gath = rest[2 * n_parts + 2]
        acc, send_sems, recv_sems, pk_send, pk_recv, pk_local = rest[2 * n_parts + 3:]
        i, j, k = pl.program_id(0), pl.program_id(1), pl.program_id(2)
        x, y, c = lax.axis_index("x"), lax.axis_index("y"), lax.axis_index("c")
        my_chip = 2 * x + y
        me = 4 * x + 2 * y + c
        chips = [(1 - x, y), (x, 1 - y), (1 - x, 1 - y)]

        def part_copy(t, n):
            px, py = chips[n]
            return pltpu.make_async_remote_copy(
                src_ref=srcs[t].at[2 * px + py], dst_ref=lands[t].at[my_chip],
                send_sem=send_sems.at[3 * t + n], recv_sem=recv_sems.at[3 * t + n],
                device_id=(px, py, c), device_id_type=MESH)

        def part_landing(t, n):
            px, py = chips[n]
            return pltpu.make_async_remote_copy(
                src_ref=srcs[t].at[my_chip], dst_ref=lands[t].at[2 * px + py],
                send_sem=send_sems.at[3 * t + n], recv_sem=recv_sems.at[3 * t + n],
                device_id=(px, py, c), device_id_type=MESH)

        def pk_copy(s):
            return pltpu.make_async_remote_copy(
                src_ref=pk_ref, dst_ref=gath.at[me], send_sem=pk_send, recv_sem=pk_recv.at[me],
                device_id=(s // 4, (s // 2) % 2, s % 2), device_id_type=MESH)

        @pl.when((i == 0) & (j == 0) & (k == 0))
        def _():
            for t in range(n_parts):
                for n in range(3):
                    part_copy(t, n).start()
            pltpu.make_async_copy(pk_ref, gath.at[me], pk_local).start()
            for s in range(N_DEV):
                @pl.when(s != me)
                def _(s=s):
                    pk_copy(s).start()

        @pl.when(k == 0)
        def _():
            acc[...] = r_ref[...]

        acc[...] += _dot(a_ref[...], b_ref[...])

        @pl.when(k == nk - 1)
        def _():
            o_ref[...] = acc[...]

        @pl.when((i == ni - 1) & (j == nj - 1) & (k == nk - 1))
        def _():
            for t in range(n_parts):
                for n in range(3):
                    part_landing(t, n).wait_recv()
            for t in range(n_parts):
                for n in range(3):
                    part_copy(t, n).wait_send()
            for s in range(N_DEV):
                @pl.when(s != me)
                def _(s=s):
                    pltpu.make_async_remote_copy(
                        src_ref=pk_ref, dst_ref=gath.at[s], send_sem=pk_send, recv_sem=pk_recv.at[s],
                        device_id=(s // 4, (s // 2) % 2, s % 2), device_id_type=MESH).wait_recv()
            seven = gath.at[pl.ds(0, N_DEV - 1)]
            pltpu.make_async_remote_copy(src_ref=seven, dst_ref=seven, send_sem=pk_send, recv_sem=pk_send,
                                         device_id=(x, y, 1 - c), device_id_type=MESH).wait_send()
            pltpu.make_async_copy(pk_ref, gath.at[me], pk_local).wait()

    hbm = pl.BlockSpec(memory_space=pl.ANY)
    outs = pl.pallas_call(
        body, name="grad_x", grid=(ni, nj, nk),
        in_specs=[pl.BlockSpec((None, tm, tk), lambda i, j, k: (lo + k // nkb, i, k % nkb)),
                  pl.BlockSpec((tk, tn), lambda i, j, k: (lo * nkb + k, j)),
                  pl.BlockSpec((tm, tn), lambda i, j, k: (i, j))] + [hbm] * (n_parts + 1),
        out_specs=[pl.BlockSpec((tm, tn), lambda i, j, k: (i, j))] + [hbm] * (n_parts + 1),
        out_shape=[jax.ShapeDtypeStruct((t_len, d), F32)] + [jax.ShapeDtypeStruct(p.shape, p.dtype) for p in parts]
        + [jax.ShapeDtypeStruct((N_DEV,) + packed.shape, packed.dtype)],
        scratch_shapes=[pltpu.VMEM((tm, tn), F32),
                        pltpu.SemaphoreType.DMA((3 * n_parts,)), pltpu.SemaphoreType.DMA((3 * n_parts,)),
                        pltpu.SemaphoreType.DMA(()), pltpu.SemaphoreType.DMA((N_DEV,)), pltpu.SemaphoreType.DMA(())],
        compiler_params=_cparams(("arbitrary", "arbitrary", "arbitrary")),
    )(dp, w_in, gx_direct, *parts, packed)
    return outs[0], list(outs[1:1 + n_parts]), outs[1 + n_parts]


def kernel(x, w_in, b_gate, ln_v_g, ln_v_b, w_s, b_s, conv_w, conv_b, w_oa, w_ob, w_out, ln_g, ln_b, loss_target, m_w_in, m_b_gate, m_ln_v_g, m_ln_v_b, m_w_s, m_b_s, m_conv_w, m_conv_b, m_w_oa, m_w_ob, m_w_out, m_ln_g, m_ln_b, v_w_in, v_b_gate, v_ln_v_g, v_ln_v_b, v_w_s, v_b_s, v_conv_w, v_conv_b, v_w_oa, v_w_ob, v_w_out, v_ln_g, v_ln_b):
    _, t_len, d = x.shape
    assert d % (HEADS * 128) == 0 and t_len % CHUNK == 0 and w_in.shape[2] * N_DEV == N_IN * d
    x2 = x[0]
    tgt2 = loss_target[0]
    c_arr = lax.axis_index("c").astype(jnp.int32).reshape(1)
    chip_arr = (2 * lax.axis_index("x") + lax.axis_index("y")).astype(jnp.int32).reshape(1)
    dev = 4 * lax.axis_index("x") + 2 * lax.axis_index("y") + lax.axis_index("c")

    shards = [_cast_transposed(w_in[0], "cast_w_in"), _cast_bf16(w_oa[0], "cast_w_oa"),
              _cast_bf16(w_ob[0], "cast_w_ob"), _cast_bf16(w_out[0], "cast_w_out")]
    (conv_w_g,) = _all_gather([conv_w[0]], ["lead"], "gather_conv_w", vmem=True)
    conv_w_f = jnp.transpose(conv_w_g, (1, 0, 2)).reshape(3, d)
    w_s3 = w_s[0]
    b_s3 = b_s[0].reshape(HEADS, CHUNK, 1)

    saved, ya, yb, w_in_f, w_oa_f, w_ob_f, w_out_f = _mixer_fwd(
        x2, shards, ln_v_g, ln_v_b, w_s3, b_s3, conv_w_f, conv_b)
    sa, sb, pa, pb, merged = _merge_fwd(x2, ya, yb, w_in_f, w_oa_f, w_ob_f, b_gate)
    dz, gx_direct, xb, g_ln_g, g_ln_b, err2 = _out_ln_loss(merged, w_out_f, x2, tgt2, ln_g, ln_b)
    loss = lax.psum(0.5 * jnp.sum(err2) / d, ("x", "y", "c"))

    da, db, dgate, g_bga, g_bgb = _merge_bwd(dz, w_out_f, sa, sb, pa, pb)
    dp, g_ws, g_bs, g_lvg, g_lvb, g_cw, g_cb = _mixer_bwd(
        da, db, w_oa_f, w_ob_f, saved, dgate, ln_v_g, ln_v_b, w_s3, b_s3, conv_w_f)
    gw_in = _grad_w_in(xb, dp)

    gw_out, land_in = _grad_w(merged, dz, "grad_w_out", exchange=gw_in)
    part_in = _pair_sum(gw_in, land_in, "col", c_arr, "grad_pair_sum_0")
    gw_oa = _grad_w(ya, da, "grad_w_oa")
    gw_ob = _grad_w(yb, db, "grad_w_ob")
    rows = [gw_oa, gw_ob, gw_out]
    lands = _pair_exchange(rows, ["row"] * 3, "grad_pair_exchange")
    parts = [part_in] + [_pair_sum(g, l, "row", c_arr, "grad_pair_sum_%d" % (n + 1))
                         for n, (g, l) in enumerate(zip(rows, lands))]
    pieces = [jnp.concatenate([g_bga, g_bgb], axis=1), g_lvg, g_lvb, g_ws, g_bs, g_cw, g_cb, g_ln_g, g_ln_b]
    sizes = [p.size for p in pieces]
    packed = jnp.concatenate([p.reshape(-1, 128) for p in pieces], axis=0)
    gx_part, lands2, gathered = _grad_x(dp, w_in_f, gx_direct, parts, packed, (0, GRAD_X_COMM_BLOCKS))
    grad_x = _grad_x_tail(dp, w_in_f, gx_part, (GRAD_X_COMM_BLOCKS, N_IN))[None]

    big = []
    for n, (w, m, v) in enumerate([(w_in, m_w_in, v_w_in), (w_oa, m_w_oa, v_w_oa), (w_ob, m_w_ob, v_w_ob),
                                   (w_out, m_w_out, v_w_out)]):
        big.append([o[None] for o in _sum_adam(parts[n], lands2[n], w[0], m[0], v[0], chip_arr, "sum_adam_%d" % n)])
    (g_w_in, d_w_in, nm_w_in, nv_w_in), (g_w_oa, d_w_oa, nm_w_oa, nv_w_oa), \
        (g_w_ob, d_w_ob, nm_w_ob, nv_w_ob), (g_w_out, d_w_out, nm_w_out, nv_w_out) = big

    total = _small_sum(gathered, "sum_small_grads")
    offs = [0]
    for s in sizes:
        offs.append(offs[-1] + s // 128)
    unpacked = [total[offs[n]:offs[n + 1]] for n in range(len(pieces))]
    g_b_gate = unpacked[0].reshape(b_gate.shape)
    g_ln_v_g = unpacked[1].reshape(ln_v_g.shape)
    g_ln_v_b = unpacked[2].reshape(ln_v_b.shape)
    g_w_s = unpacked[3].reshape(w_s.shape)
    g_b_s = unpacked[4].reshape(b_s.shape)
    g_conv_w = lax.dynamic_slice_in_dim(unpacked[5].reshape(3, d), dev * (d // N_DEV), d // N_DEV, axis=1)[None]
    g_conv_b = unpacked[6].reshape(conv_b.shape)
    g_ln_g2 = unpacked[7].reshape(ln_g.shape)
    g_ln_b2 = unpacked[8].reshape(ln_b.shape)

    small_w = [b_gate, ln_v_g, ln_v_b, w_s, b_s, conv_w, conv_b, ln_g, ln_b]
    small_g = [g_b_gate, g_ln_v_g, g_ln_v_b, g_w_s, g_b_s, g_conv_w, g_conv_b, g_ln_g2, g_ln_b2]
    small_m = [m_b_gate, m_ln_v_g, m_ln_v_b, m_w_s, m_b_s, m_conv_w, m_conv_b, m_ln_g, m_ln_b]
    small_v = [v_b_gate, v_ln_v_g, v_ln_v_b, v_w_s, v_b_s, v_conv_w, v_conv_b, v_ln_g, v_ln_b]

    def flat(a):
        return a.reshape(-1, a.shape[-1])

    res = _small_adam([flat(a) for a in small_w], [flat(a) for a in small_g], [flat(a) for a in small_m],
                      [flat(a) for a in small_v], "adam_small")
    ns = len(small_w)
    d_s = [res[n].reshape(small_w[n].shape) for n in range(ns)]
    nm_s = [res[ns + n].reshape(small_w[n].shape) for n in range(ns)]
    nv_s = [res[2 * ns + n].reshape(small_w[n].shape) for n in range(ns)]

    def ordered(first, small, oa, ob, out):
        return [first] + small[:7] + [oa, ob, out] + small[7:]

    return (loss, grad_x,
            *ordered(g_w_in, small_g, g_w_oa, g_w_ob, g_w_out),
            *ordered(d_w_in, d_s, d_w_oa, d_w_ob, d_w_out),
            *ordered(nm_w_in, nm_s, nm_w_oa, nm_w_ob, nm_w_out),
            *ordered(nv_w_in, nv_s, nv_w_oa, nv_w_ob, nv_w_out))
```

```python
import functools

import jax
import jax.numpy as jnp
from jax import lax
from jax.experimental import pallas as pl
from jax.experimental.pallas import tpu as pltpu

F32 = jnp.float32
BF16 = jnp.bfloat16
MESH = pl.DeviceIdType.MESH

N_DEV = 8
N_CHIP = 4
HEADS = 8
CHUNK = 128
N_IN = 9
N_MIX = 7
N_SAVE = 8
LN_EPS = 1e-5
DN_ALPHA = 2.0 ** 0.25
ADAM_LR = 0.001
ADAM_B1 = 0.9
ADAM_B2 = 0.999
ADAM_EPS = 1e-08
ADAM_WD = 0.01
ADAM_STEP = 10
GELU_C0 = 0.7978845608028654
GELU_C1 = 0.044715
SUBLANES = 8
FIRST_SENDS = 3
SENDS_PER_GROUP = 2
MM_TILE = 1024
MM_DEPTH = 2048
MM_DEPTH_TOKENS = 4096
GRAD_X_COMM_BLOCKS = 8
VMEM_LIMIT = 56 << 20


def _cparams(sem):
    return pltpu.CompilerParams(dimension_semantics=sem, vmem_limit_bytes=VMEM_LIMIT)


def _tile(n, want):
    t = min(n, want)
    while n % t:
        t //= 2
    return t


def _gelu(u):
    u2 = u * u
    t = jnp.tanh(u * (GELU_C0 + (GELU_C0 * GELU_C1) * u2))
    hp = 0.5 * t + 0.5
    grad = hp + (0.5 * u) * (1.0 - t * t) * (GELU_C0 + (3.0 * GELU_C0 * GELU_C1) * u2)
    return u * hp, grad


def _silu(z):
    s = jax.nn.sigmoid(z)
    sil = z * s
    return sil, s + sil * (1.0 - s)


def _fold8(a):
    return jnp.sum(a.reshape(a.shape[0] // SUBLANES, SUBLANES, a.shape[1]), axis=0)


def _dot(a, b):
    return jnp.dot(a, b, preferred_element_type=F32)


def _dot_nt(a, b):
    return lax.dot_general(a, b, (((1,), (1,)), ((), ())), preferred_element_type=F32)


def _dot_tn(a, b):
    return lax.dot_general(a, b, (((0,), (0,)), ((), ())), preferred_element_type=F32)


def _tril_mask():
    r = lax.broadcasted_iota(jnp.int32, (CHUNK, CHUNK), 0)
    c = lax.broadcasted_iota(jnp.int32, (CHUNK, CHUNK), 1)
    return c <= r


def _cast_bf16(a, name):
    rows, cols = a.shape
    rb = _tile(rows, 256)

    def body(a_ref, o_ref):
        o_ref[...] = a_ref[...].astype(BF16)

    return pl.pallas_call(
        body, name=name, grid=(rows // rb,),
        in_specs=[pl.BlockSpec((rb, cols), lambda i: (i, 0))],
        out_specs=pl.BlockSpec((rb, cols), lambda i: (i, 0)),
        out_shape=jax.ShapeDtypeStruct((rows, cols), BF16),
        compiler_params=_cparams(("parallel",)),
    )(a)


def _cast_transposed(a, name):
    rows, cols = a.shape
    tb = _tile(rows, 2048)
    tc = _tile(cols, 256)

    def body(a_ref, o_ref):
        o_ref[...] = a_ref[...].T.astype(BF16)

    return pl.pallas_call(
        body, name=name, grid=(cols // tc, rows // tb),
        in_specs=[pl.BlockSpec((tb, tc), lambda j, i: (i, j))],
        out_specs=pl.BlockSpec((tc, tb), lambda j, i: (j, i)),
        out_shape=jax.ShapeDtypeStruct((cols, rows), BF16),
        compiler_params=_cparams(("parallel", "parallel")),
    )(a)


def _shard_ref(full, kind, s, n):
    if kind == "col":
        return full.at[:, pl.ds(pl.multiple_of(s * n, 128), n)]
    if kind == "row":
        return full.at[pl.ds(pl.multiple_of(s * n, SUBLANES), n), :]
    return full.at[s]


def _all_gather(shards, kinds, name, vmem):
    nt = len(shards)
    out_shapes = []
    for a, kind in zip(shards, kinds):
        if kind == "col":
            out_shapes.append(jax.ShapeDtypeStruct((a.shape[0], N_DEV * a.shape[1]), a.dtype))
        elif kind == "row":
            out_shapes.append(jax.ShapeDtypeStruct((N_DEV * a.shape[0], a.shape[1]), a.dtype))
        else:
            out_shapes.append(jax.ShapeDtypeStruct((N_DEV,) + a.shape, a.dtype))

    def body(*refs):
        srcs, fulls = refs[:nt], refs[nt:2 * nt]
        send_sems, recv_sems, local_sems = refs[2 * nt:]
        x, y, c = lax.axis_index("x"), lax.axis_index("y"), lax.axis_index("c")
        sibling = (x, y, 1 - c)
        chips = [(1 - x, y), (x, 1 - y), (1 - x, 1 - y)]

        def dev(px, py, pc):
            return 4 * px + 2 * py + pc

        def region(t, s):
            a, kind = shards[t], kinds[t]
            n = a.shape[1] if kind == "col" else a.shape[0]
            return _shard_ref(fulls[t], kind, s, n)

        def copy(t, k, block, to, own=False):
            return pltpu.make_async_remote_copy(
                src_ref=srcs[t] if own else region(t, block), dst_ref=region(t, block),
                send_sem=send_sems.at[7 * t + k], recv_sem=recv_sems.at[7 * t + k],
                device_id=to, device_id_type=MESH)

        me = dev(x, y, c)
        started = []
        for t in range(nt):
            mine = pltpu.make_async_copy(srcs[t], region(t, me), local_sems.at[t])
            mine.start()
            started.append(mine)
        first = []
        for t in range(nt):
            first.append(copy(t, 0, me, sibling, own=True))
            for j, chip in enumerate(chips):
                first.append(copy(t, 1 + j, me, (*chip, c), own=True))
        for cp in first:
            cp.start()
        passed = []
        for t in range(nt):
            for j, chip in enumerate(chips):
                blk = dev(*chip, c)
                copy(t, 1 + j, blk, sibling).wait_recv()
                fwd = copy(t, 4 + j, blk, sibling)
                fwd.start()
                passed.append(fwd)
        for t in range(nt):
            copy(t, 0, dev(x, y, 1 - c), sibling).wait_recv()
            for j, chip in enumerate(chips):
                copy(t, 4 + j, dev(*chip, 1 - c), sibling).wait_recv()
        for cp in first + passed:
            cp.wait_send()
        for mine in started:
            mine.wait()

    space = pltpu.VMEM if vmem else pl.ANY
    return pl.pallas_call(
        body, name=name,
        in_specs=[pl.BlockSpec(memory_space=space)] * nt,
        out_specs=[pl.BlockSpec(memory_space=space)] * nt,
        out_shape=out_shapes,
        scratch_shapes=[pltpu.SemaphoreType.DMA((7 * nt,)), pltpu.SemaphoreType.DMA((7 * nt,)),
                        pltpu.SemaphoreType.DMA((nt,))],
        compiler_params=pltpu.CompilerParams(vmem_limit_bytes=VMEM_LIMIT, has_side_effects=True),
    )(*shards)


def _pair_exchange(grads, kinds, name):
    nt = len(grads)
    shard_shapes = []
    for g, kind in zip(grads, kinds):
        shard_shapes.append((g.shape[0], g.shape[1] // N_DEV) if kind == "col" else (g.shape[0] // N_DEV, g.shape[1]))

    def body(*refs):
        srcs, lands = refs[:nt], refs[nt:2 * nt]
        send_sems, recv_sems = refs[2 * nt:]
        x, y, c = lax.axis_index("x"), lax.axis_index("y"), lax.axis_index("c")
        copies = []
        for t in range(nt):
            n = shard_shapes[t][1] if kinds[t] == "col" else shard_shapes[t][0]
            for k in range(N_CHIP):
                cp = pltpu.make_async_remote_copy(
                    src_ref=_shard_ref(srcs[t], kinds[t], 2 * k + 1 - c, n), dst_ref=lands[t].at[k],
                    send_sem=send_sems.at[N_CHIP * t + k], recv_sem=recv_sems.at[N_CHIP * t + k],
                    device_id=(x, y, 1 - c), device_id_type=MESH)
                cp.start()
                copies.append(cp)
        for cp in copies:
            cp.wait()

    return pl.pallas_call(
        body, name=name,
        in_specs=[pl.BlockSpec(memory_space=pl.ANY)] * nt,
        out_specs=[pl.BlockSpec(memory_space=pl.ANY)] * nt,
        out_shape=[jax.ShapeDtypeStruct((N_CHIP,) + s, g.dtype) for s, g in zip(shard_shapes, grads)],
        scratch_shapes=[pltpu.SemaphoreType.DMA((N_CHIP * nt,)), pltpu.SemaphoreType.DMA((N_CHIP * nt,))],
        compiler_params=pltpu.CompilerParams(has_side_effects=True),
    )(*grads)


def _pair_sum(grad, land, kind, c_arr, name):
    _, r, w = land.shape
    rb = _tile(r, 256)
    nrb = r // rb

    def body(c_ref, g_ref, l_ref, o_ref):
        o_ref[...] = (g_ref[...].astype(F32) + l_ref[...].astype(F32)).astype(o_ref.dtype)

    if kind == "col":
        g_spec = pl.BlockSpec((rb, w), lambda k, i, c: (i, 2 * k + c[0]))
    else:
        g_spec = pl.BlockSpec((rb, w), lambda k, i, c: ((2 * k + c[0]) * nrb + i, 0))
    return pl.pallas_call(
        body, name=name,
        grid_spec=pltpu.PrefetchScalarGridSpec(
            num_scalar_prefetch=1, grid=(N_CHIP, nrb),
            in_specs=[g_spec, pl.BlockSpec((None, rb, w), lambda k, i, c: (k, i, 0))],
            out_specs=pl.BlockSpec((None, rb, w), lambda k, i, c: (k, i, 0))),
        out_shape=jax.ShapeDtypeStruct(land.shape, BF16),
        compiler_params=_cparams(("parallel", "parallel")),
    )(c_arr, grad, land)


def _adam(w, g, m, v):
    m = ADAM_B1 * m + (1.0 - ADAM_B1) * g
    v = ADAM_B2 * v + (1.0 - ADAM_B2) * jnp.square(g)
    m_hat = m / (1.0 - ADAM_B1 ** ADAM_STEP)
    v_hat = v / (1.0 - ADAM_B2 ** ADAM_STEP)
    delta = -ADAM_LR * (m_hat / (jnp.sqrt(v_hat) + ADAM_EPS) + ADAM_WD * w)
    return delta, m, v


def _sum_adam(part, land, w, m, v, chip_arr, name):
    r, wd = w.shape
    rb = _tile(r, 128)

    def body(k_ref, own, r1, r2, r3, w_ref, m_ref, v_ref, g_out, d_out, m_out, v_out):
        g = own[...].astype(F32) + r1[...].astype(F32) + r2[...].astype(F32) + r3[...].astype(F32)
        d, mn, vn = _adam(w_ref[...], g, m_ref[...], v_ref[...])
        g_out[...] = g
        d_out[...] = d
        m_out[...] = mn
        v_out[...] = vn

    def slot(off):
        return pl.BlockSpec((None, rb, wd), lambda i, k: ((k[0] + off) % N_CHIP, i, 0))

    plain = pl.BlockSpec((rb, wd), lambda i, k: (i, 0))
    return pl.pallas_call(
        body, name=name,
        grid_spec=pltpu.PrefetchScalarGridSpec(
            num_scalar_prefetch=1, grid=(r // rb,),
            in_specs=[slot(0), slot(1), slot(2), slot(3), plain, plain, plain],
            out_specs=[plain] * 4),
        out_shape=[jax.ShapeDtypeStruct(w.shape, F32)] * 4,
        compiler_params=_cparams(("parallel",)),
    )(chip_arr, part, land, land, land, w, m, v)


def _small_sum(gathered, name):
    _, r, w = gathered.shape

    def body(g_ref, o_ref):
        acc = g_ref[0]
        for d in range(1, N_DEV):
            acc = acc + g_ref[d]
        o_ref[...] = acc

    return pl.pallas_call(body, name=name, out_shape=jax.ShapeDtypeStruct((r, w), F32))(gathered)


def _small_adam(ws, gs, ms, vs, name):
    n = len(ws)

    def body(*refs):
        ins, outs = refs[:4 * n], refs[4 * n:]
        for t in range(n):
            d, mn, vn = _adam(ins[t][...], ins[n + t][...], ins[2 * n + t][...], ins[3 * n + t][...])
            outs[t][...] = d
            outs[n + t][...] = mn
            outs[2 * n + t][...] = vn

    shapes = [jax.ShapeDtypeStruct(w.shape, F32) for w in ws]
    return pl.pallas_call(body, name=name, out_shape=shapes * 3)(*ws, *gs, *ms, *vs)


def _mixer_fwd(x, shards, ln_v_g, ln_v_b, w_s, b_s3, conv_w, conv_b):
    t_len, d = x.shape
    hd = d // HEADS
    tm = _tile(t_len, 512)
    nt = t_len // tm
    assert nt >= 2
    w8 = shards[0].shape[0]
    bps = w8 // hd
    r8 = shards[1].shape[0]
    nq = N_IN * HEADS
    n_blocks = nq + 3 * N_DEV
    groups = [[("in", HEADS * b + k) for b in range(N_MIX)] for k in range(HEADS)]
    groups.append([("in", q) for q in range(N_MIX * HEADS, nq)] + [(t, s) for t in (1, 2, 3) for s in range(N_DEV)])

    def owner_of(blk):
        return blk[1] // bps if blk[0] == "in" else blk[1]

    send_step = {}
    for s in range(N_DEV):
        mine = [blk for grp in groups for blk in grp if owner_of(blk) == s]
        for pos, blk in enumerate(mine):
            send_step[blk] = 0 if pos < FIRST_SENDS else 1 + (pos - FIRST_SENDS) // SENDS_PER_GROUP
    assert max(send_step.values()) < HEADS
    assert all(send_step[blk] <= max(k - 1, 0) for k in range(HEADS) for blk in groups[k])
    tail_pass_step = nt - 6 if nt >= 8 else nt - 2

    def body(x_ref, sh_in, sh_oa, sh_ob, sh_out, lng, lnb, ws_ref, bs_ref, cw_ref, cb_ref,
             save_ref, ya_ref, yb_ref, f_in, f_oa, f_ob, f_out,
             hbuf, wbuf, recv_sems, own_sems, fwd_sems, local_sems, load_sems):
        g = pl.program_id(0)
        i = pl.program_id(1)
        x, y, c = lax.axis_index("x"), lax.axis_index("y"), lax.axis_index("c")
        sibling = (x, y, 1 - c)
        chips = [(1 - x, y), (x, 1 - y), (1 - x, 1 - y)]
        shard_refs = (sh_in, sh_oa, sh_ob, sh_out)
        fulls = (f_in, f_oa, f_ob, f_out)

        def tensor(blk):
            return 0 if blk[0] == "in" else blk[0]

        def owner(blk):
            s = owner_of(blk)
            return s // 4, (s // 2) % 2, s % 2

        def bid(blk):
            return blk[1] if blk[0] == "in" else nq + (blk[0] - 1) * N_DEV + blk[1]

        def region(blk):
            if blk[0] == "in":
                return f_in.at[pl.ds(blk[1] * hd, hd), :]
            return fulls[blk[0]].at[pl.ds(blk[1] * r8, r8), :]

        def own_src(blk):
            if blk[0] == "in":
                return sh_in.at[pl.ds((blk[1] % bps) * hd, hd), :]
            return shard_refs[blk[0]]

        def rcopy(blk, to, send_sem, own):
            return pltpu.make_async_remote_copy(
                src_ref=own_src(blk) if own else region(blk), dst_ref=region(blk),
                send_sem=send_sem, recv_sem=recv_sems.at[bid(blk)], device_id=to, device_id_type=MESH)

        def send_own(blk):
            ox, oy, oc = owner(blk)

            @pl.when((x == ox) & (y == oy) & (c == oc))
            def _():
                rcopy(blk, sibling, own_sems.at[tensor(blk)], True).start()
                for chip in chips:
                    rcopy(blk, (*chip, c), own_sems.at[tensor(blk)], True).start()

        def pass_on(blk):
            ox, oy, oc = owner(blk)

            @pl.when(((x != ox) | (y != oy)) & (c == oc))
            def _():
                rcopy(blk, sibling, fwd_sems.at[tensor(blk)], False).wait_recv()
                rcopy(blk, sibling, fwd_sems.at[tensor(blk)], False).start()

        def wait_from_sibling(blk):
            @pl.when(c != owner(blk)[2])
            def _():
                rcopy(blk, sibling, fwd_sems.at[tensor(blk)], False).wait_recv()

        def local_copy(t):
            me = 4 * x + 2 * y + c
            return pltpu.make_async_copy(shard_refs[t], _shard_ref(fulls[t], "row", me, w8 if t == 0 else r8),
                                         local_sems.at[t])

        first = (g == 0) & (i == 0)

        @pl.when(first)
        def _():
            for t in range(4):
                local_copy(t).start()
            for grp in groups:
                for blk in grp:
                    if send_step[blk] == 0:
                        send_own(blk)
            for blk in groups[0]:
                pass_on(blk)

        @pl.when(i == 0)
        def _():
            for k in range(1, HEADS):
                @pl.when(g == k)
                def _(k=k):
                    for grp in groups:
                        for blk in grp:
                            if send_step[blk] == k:
                                send_own(blk)

        @pl.when(i == nt - 2)
        def _():
            for k in range(HEADS - 1):
                @pl.when(g == k)
                def _(k=k):
                    for blk in groups[k + 1]:
                        pass_on(blk)

        @pl.when((g == HEADS - 1) & (i == tail_pass_step))
        def _():
            for blk in groups[HEADS]:
                pass_on(blk)

        def fetch_weights(k):
            for blk in groups[k]:
                wait_from_sibling(blk)
            for b, blk in enumerate(groups[k]):
                ox, oy, oc = owner(blk)
                mine = (x == ox) & (y == oy) & (c == oc)

                @pl.when(mine)
                def _(b=b, blk=blk):
                    pltpu.make_async_copy(own_src(blk), wbuf.at[k % 2, b], load_sems.at[b]).start()

                @pl.when(jnp.logical_not(mine))
                def _(b=b, blk=blk):
                    pltpu.make_async_copy(region(blk), wbuf.at[k % 2, b], load_sems.at[b]).start()

        @pl.when(i == 0)
        def _():
            for k in range(HEADS):
                @pl.when(g == k)
                def _(k=k):
                    if k == 0:
                        fetch_weights(0)
                    for b, blk in enumerate(groups[k]):
                        pltpu.make_async_copy(region(blk), wbuf.at[k % 2, b], load_sems.at[b]).wait()

        @pl.when(i == nt - 1)
        def _():
            for k in range(1, HEADS):
                @pl.when(g == k - 1)
                def _(k=k):
                    fetch_weights(k)

        xt = x_ref[...].astype(BF16)
        wslot = g % 2
        u = _dot_nt(xt, wbuf[wslot, 0])
        v = _dot_nt(xt, wbuf[wslot, 1])
        z = _dot_nt(xt, wbuf[wslot, 2])
        save_ref[0] = u.astype(BF16)
        save_ref[1] = v.astype(BF16)
        save_ref[2] = z.astype(BF16)
        gu, _ = _gelu(u)
        gv, _ = _gelu(v)
        mu = jnp.mean(gv, axis=-1, keepdims=True)
        dv = gv - mu
        var = jnp.mean(dv * dv, axis=-1, keepdims=True)
        vn = (dv * lax.rsqrt(var + LN_EPS) * lng[...] + lnb[...]).astype(BF16)
        sz, _ = _silu(z)
        gate = gu * sz
        wm = jnp.where(_tril_mask(), ws_ref[0], 0.0).astype(BF16)
        bs = bs_ref[0]
        for ck in range(tm // CHUNK):
            rows = slice(ck * CHUNK, (ck + 1) * CHUNK)
            mixed = _dot(wm, vn[rows]) + bs
            ya_ref[rows, :] = (gate[rows] * mixed).astype(BF16)

        xbv = _dot_nt(xt, wbuf[wslot, 3])
        cbv = _dot_nt(xt, wbuf[wslot, 4])
        bbv = _dot_nt(xt, wbuf[wslot, 5])
        zbv = _dot_nt(xt, wbuf[wslot, 6])
        save_ref[3] = xbv.astype(BF16)
        save_ref[4] = cbv.astype(BF16)
        save_ref[5] = bbv.astype(BF16)
        save_ref[6] = zbv.astype(BF16)
        h = cbv * xbv

        @pl.when(i == 0)
        def _():
            hbuf[0:SUBLANES, :] = jnp.zeros((SUBLANES, hd), F32)

        hbuf[SUBLANES:SUBLANES + tm, :] = h
        h1 = hbuf[SUBLANES - 1:SUBLANES - 1 + tm, :]
        h2 = hbuf[SUBLANES - 2:SUBLANES - 2 + tm, :]
        conv = cb_ref[...] + cw_ref[0:1, :] * h2 + cw_ref[1:2, :] * h1 + cw_ref[2:3, :] * h
        hbuf[0:SUBLANES, :] = h[tm - SUBLANES:tm, :]
        save_ref[7] = conv.astype(BF16)
        szb, _ = _silu(zbv)
        yb_ref[...] = (bbv * conv * szb).astype(BF16)

        @pl.when((g == HEADS - 1) & (i == nt - 1))
        def _():
            for blk in groups[HEADS]:
                wait_from_sibling(blk)
            for t in range(4):
                local_copy(t).wait()
            for t in range(4):
                n = w8 if t == 0 else r8
                own_all, fwd_all = fulls[t].at[pl.ds(0, 4 * n), :], fulls[t].at[pl.ds(0, 3 * n), :]
                for ref, sem in ((own_all, own_sems.at[t]), (fwd_all, fwd_sems.at[t])):
                    pltpu.make_async_remote_copy(src_ref=ref, dst_ref=ref, send_sem=sem, recv_sem=sem,
                                                 device_id=sibling, device_id_type=MESH).wait_send()

    vec = pl.BlockSpec((1, hd), lambda g, i: (0, g))
    hbm = pl.BlockSpec(memory_space=pl.ANY)
    return pl.pallas_call(
        body, name="mixer_fwd", grid=(HEADS, nt),
        in_specs=[pl.BlockSpec((tm, d), lambda g, i: (i, 0)), hbm, hbm, hbm, hbm,
                  vec, vec,
                  pl.BlockSpec((1, CHUNK, CHUNK), lambda g, i: (g, 0, 0)),
                  pl.BlockSpec((1, CHUNK, 1), lambda g, i: (g, 0, 0)),
                  pl.BlockSpec((3, hd), lambda g, i: (0, g)),
                  vec],
        out_specs=[pl.BlockSpec((N_SAVE, tm, hd), lambda g, i: (0, i, g)),
                   pl.BlockSpec((tm, hd), lambda g, i: (i, g)),
                   pl.BlockSpec((tm, hd), lambda g, i: (i, g)),
                   hbm, hbm, hbm, hbm],
        out_shape=[jax.ShapeDtypeStruct((N_SAVE, t_len, d), BF16),
                   jax.ShapeDtypeStruct((t_len, d), BF16),
                   jax.ShapeDtypeStruct((t_len, d), BF16),
                   jax.ShapeDtypeStruct((N_DEV * w8, d), BF16),
                   jax.ShapeDtypeStruct((d, d), BF16), jax.ShapeDtypeStruct((d, d), BF16),
                   jax.ShapeDtypeStruct((d, d), BF16)],
        scratch_shapes=[pltpu.VMEM((SUBLANES + tm, hd), F32), pltpu.VMEM((2, N_MIX, hd, d), BF16),
                        pltpu.SemaphoreType.DMA((n_blocks,)), pltpu.SemaphoreType.DMA((4,)),
                        pltpu.SemaphoreType.DMA((4,)), pltpu.SemaphoreType.DMA((4,)),
                        pltpu.SemaphoreType.DMA((N_MIX,))],
        compiler_params=_cparams(("arbitrary", "arbitrary")),
    )(x, *shards, ln_v_g, ln_v_b, w_s, b_s3, conv_w, conv_b)


def _merge_fwd(x, ya, yb, w_in, w_oa, w_ob, b_gate):
    t_len, d = x.shape
    tm = _tile(t_len, 256)
    tn = _tile(d, 1024)
    nj = d // tn

    def body(x_ref, ya_ref, yb_ref, wga, wgb, woa, wob, bga, bgb, sa_ref, sb_ref, pa_ref, pb_ref, mg_ref):
        xt = x_ref[...].astype(BF16)
        sa = jax.nn.sigmoid(_dot_nt(xt, wga[...]) + bga[...])
        sb = jax.nn.sigmoid(_dot_nt(xt, wgb[...]) + bgb[...])
        pa = _dot(ya_ref[...], woa[...])
        pb = _dot(yb_ref[...], wob[...])
        sa_ref[...] = sa.astype(BF16)
        sb_ref[...] = sb.astype(BF16)
        pa_ref[...] = pa.astype(BF16)
        pb_ref[...] = pb.astype(BF16)
        mg_ref[...] = (sa * pa + sb * pb).astype(BF16)

    row = pl.BlockSpec((tm, d), lambda j, i: (i, 0))
    out = pl.BlockSpec((tm, tn), lambda j, i: (i, j))
    return pl.pallas_call(
        body, name="merge_fwd", grid=(nj, t_len // tm),
        in_specs=[row, row, row,
                  pl.BlockSpec((tn, d), lambda j, i: (7 * nj + j, 0)),
                  pl.BlockSpec((tn, d), lambda j, i: (8 * nj + j, 0)),
                  pl.BlockSpec((d, tn), lambda j, i: (0, j)),
                  pl.BlockSpec((d, tn), lambda j, i: (0, j)),
                  pl.BlockSpec((1, tn), lambda j, i: (0, j)),
                  pl.BlockSpec((1, tn), lambda j, i: (0, nj + j))],
        out_specs=[out] * 5,
        out_shape=[jax.ShapeDtypeStruct((t_len, d), BF16)] * 5,
        compiler_params=_cparams(("parallel", "arbitrary")),
    )(x, ya, yb, w_in, w_in, w_oa, w_ob, b_gate, b_gate)


def _out_ln_loss(merged, w_out, x, target, ln_g, ln_b, sa, sb, pa, pb):
    t_len, d = x.shape
    tm = _tile(t_len, 256)
    nt = t_len // tm

    def body(mg_ref, w_ref, x_ref, t_ref, g_ref, b_ref, sa_ref, sb_ref, pa_ref, pb_ref,
             dz_ref, gx_ref, xb_ref, da_ref, db_ref, dg_ref, glg_ref, glb_ref, ls_ref, ga_ref, gb_ref,
             a_g, a_b, a_l, acc_a, acc_b):
        i = pl.program_id(0)
        xb_ref[...] = x_ref[...].astype(BF16)

        @pl.when(i == 0)
        def _():
            for a in (a_g, a_b, a_l, acc_a, acc_b):
                a[...] = jnp.zeros_like(a)

        zres = DN_ALPHA * x_ref[...] + _dot(mg_ref[...], w_ref[...])
        mu = jnp.mean(zres, axis=-1, keepdims=True)
        dc = zres - mu
        var = jnp.mean(dc * dc, axis=-1, keepdims=True)
        rstd = lax.rsqrt(var + LN_EPS)
        xhat = dc * rstd
        g = g_ref[...]
        err = xhat * g + b_ref[...] - t_ref[...]
        dy = err * (1.0 / d)
        a_l[...] += _fold8(err * err)
        a_g[...] += _fold8(dy * xhat)
        a_b[...] += _fold8(dy)
        dxh = dy * g
        m1 = jnp.mean(dxh, axis=-1, keepdims=True)
        m2 = jnp.mean(dxh * xhat, axis=-1, keepdims=True)
        dz = rstd * (dxh - m1 - xhat * m2)
        dz_bf = dz.astype(BF16)
        dz_ref[...] = dz_bf
        gx_ref[...] = DN_ALPHA * dz

        dm = _dot_nt(dz_bf, w_ref[...])
        sa = sa_ref[...].astype(F32)
        sb = sb_ref[...].astype(F32)
        da = dm * sa
        db = dm * sb
        da_ref[...] = da.astype(BF16)
        db_ref[...] = db.astype(BF16)
        dga = da * pa_ref[...].astype(F32) * (1.0 - sa)
        dgb = db * pb_ref[...].astype(F32) * (1.0 - sb)
        dg_ref[0] = dga.astype(BF16)
        dg_ref[1] = dgb.astype(BF16)
        acc_a[...] += _fold8(dga)
        acc_b[...] += _fold8(dgb)

        @pl.when(i == nt - 1)
        def _():
            glg_ref[...] = jnp.sum(a_g[...], axis=0, keepdims=True)
            glb_ref[...] = jnp.sum(a_b[...], axis=0, keepdims=True)
            ls_ref[...] = jnp.sum(a_l[...], axis=0, keepdims=True)
            ga_ref[...] = jnp.sum(acc_a[...], axis=0, keepdims=True)
            gb_ref[...] = jnp.sum(acc_b[...], axis=0, keepdims=True)

    row = pl.BlockSpec((tm, d), lambda i: (i, 0))
    vec = pl.BlockSpec((1, d), lambda i: (0, 0))
    act = jax.ShapeDtypeStruct((t_len, d), BF16)
    part = jax.ShapeDtypeStruct((1, d), F32)
    return pl.pallas_call(
        body, name="out_ln_loss", grid=(nt,),
        in_specs=[row, pl.BlockSpec((d, d), lambda i: (0, 0)), row, row, vec, vec, row, row, row, row],
        out_specs=[row, row, row, row, row, pl.BlockSpec((2, tm, d), lambda i: (0, i, 0)), vec, vec, vec, vec, vec],
        out_shape=[act, jax.ShapeDtypeStruct((t_len, d), F32), act, act, act,
                   jax.ShapeDtypeStruct((2, t_len, d), BF16), part, part, part, part, part],
        scratch_shapes=[pltpu.VMEM((SUBLANES, d), F32)] * 5,
        compiler_params=_cparams(("arbitrary",)),
    )(merged, w_out, x, target, ln_g, ln_b, sa, sb, pa, pb)


def _grad_w(a, b, name, exchange=None):
    t_len, m = a.shape
    n = b.shape[1]
    tm, tn, tk = _tile(m, MM_TILE), _tile(n, MM_TILE), _tile(t_len, MM_DEPTH_TOKENS)
    ni, nj, nk = m // tm, n // tn, t_len // tk

    def body(a_ref, b_ref, *rest):
        if exchange is None:
            o_ref, acc = rest
        else:
            src, o_ref, land, acc, send_sems, recv_sems = rest
        i, j, k = pl.program_id(0), pl.program_id(1), pl.program_id(2)

        def copies():
            x, y, c = lax.axis_index("x"), lax.axis_index("y"), lax.axis_index("c")
            w = exchange.shape[1] // N_DEV
            return [pltpu.make_async_remote_copy(
                src_ref=_shard_ref(src, "col", 2 * s + 1 - c, w), dst_ref=land.at[s],
                send_sem=send_sems.at[s], recv_sem=recv_sems.at[s],
                device_id=(x, y, 1 - c), device_id_type=MESH) for s in range(N_CHIP)]

        if exchange is not None:
            @pl.when((i == 0) & (j == 0) & (k == 0))
            def _():
                for cp in copies():
                    cp.start()

        @pl.when(k == 0)
        def _():
            acc[...] = jnp.zeros_like(acc)

        acc[...] += _dot_tn(a_ref[...], b_ref[...])

        @pl.when(k == nk - 1)
        def _():
            o_ref[...] = acc[...].astype(BF16)

        if exchange is not None:
            @pl.when((i == ni - 1) & (j == nj - 1) & (k == nk - 1))
            def _():
                for cp in copies():
                    cp.wait()

    in_specs = [pl.BlockSpec((tk, tm), lambda i, j, k: (k, i)), pl.BlockSpec((tk, tn), lambda i, j, k: (k, j))]
    out_specs = [pl.BlockSpec((tm, tn), lambda i, j, k: (i, j))]
    out_shape = [jax.ShapeDtypeStruct((m, n), BF16)]
    scratch = [pltpu.VMEM((tm, tn), F32)]
    args = [a, b]
    if exchange is not None:
        hbm = pl.BlockSpec(memory_space=pl.ANY)
        in_specs.append(hbm)
        out_specs.append(hbm)
        out_shape.append(jax.ShapeDtypeStruct((N_CHIP, exchange.shape[0], exchange.shape[1] // N_DEV), exchange.dtype))
        scratch += [pltpu.SemaphoreType.DMA((N_CHIP,)), pltpu.SemaphoreType.DMA((N_CHIP,))]
        args.append(exchange)
    outs = pl.pallas_call(
        body, name=name, grid=(ni, nj, nk),
        in_specs=in_specs, out_specs=out_specs, out_shape=out_shape, scratch_shapes=scratch,
        compiler_params=_cparams(("arbitrary", "arbitrary", "arbitrary")),
    )(*args)
    return outs[0] if exchange is None else (outs[0], outs[1])


def _mixer_bwd(da, db, w_oa, w_ob, saved, dgate, ln_v_g, ln_v_b, w_s, b_s3, conv_w):
    t_len, d = da.shape
    hd = d // HEADS
    tm = _tile(t_len, 512)
    nt = t_len // tm

    def body(da_ref, db_ref, woa, wob, sv, dgt, lng, lnb, ws_ref, bs_ref, cw_ref,
             dp_ref, gws_ref, gbs_ref, glg_ref, glb_ref, gcw_ref, gcb_ref,
             dbuf, carry, a_ws, a_bs, a_lg, a_lb, a_c0, a_c1, a_c2, a_cb):
        i = pl.program_id(1)

        @pl.when(i == 0)
        def _():
            carry[...] = jnp.zeros_like(carry)
            for a in (a_ws, a_bs, a_lg, a_lb, a_c0, a_c1, a_c2, a_cb):
                a[...] = jnp.zeros_like(a)

        dya_t = _dot_nt(da_ref[...], woa[...])
        dyb_t = _dot_nt(db_ref[...], wob[...])
        g = lng[...]
        beta = lnb[...]
        wm = jnp.where(_tril_mask(), ws_ref[0], 0.0).astype(BF16)
        bs = bs_ref[0]
        w0, w1, w2 = cw_ref[0:1, :], cw_ref[1:2, :], cw_ref[2:3, :]
        dbuf[tm:tm + SUBLANES, :] = carry[...]
        gws = a_ws[...]
        gbs = a_bs[...]
        lg, lb = a_lg[...], a_lb[...]
        c0, c1, c2, cb = a_c0[...], a_c1[...], a_c2[...], a_cb[...]
        for ck in reversed(range(tm // CHUNK)):
            r0 = ck * CHUNK
            rows = slice(r0, r0 + CHUNK)
            dya = dya_t[rows]
            u = sv[0, rows, :].astype(F32)
            v = sv[1, rows, :].astype(F32)
            z = sv[2, rows, :].astype(F32)
            gu, gu_grad = _gelu(u)
            gv, gv_grad = _gelu(v)
            mu = jnp.mean(gv, axis=-1, keepdims=True)
            dvc = gv - mu
            var = jnp.mean(dvc * dvc, axis=-1, keepdims=True)
            rstd = lax.rsqrt(var + LN_EPS)
            vhat = dvc * rstd
            vn = (vhat * g + beta).astype(BF16)
            sz, sz_grad = _silu(z)
            t1 = dya * sz
            dmixed = t1 * gu
            dmixed_b = dmixed.astype(BF16)
            mixed = _dot(wm, vn) + bs
            gws = gws + _dot_nt(dmixed_b, vn)
            gbs = gbs + dmixed
            dvn = _dot_tn(wm, dmixed_b)
            dp_ref[0, rows, :] = (t1 * mixed * gu_grad).astype(BF16)
            dp_ref[2, rows, :] = (dya * gu * mixed * sz_grad).astype(BF16)
            lg = lg + _fold8(dvn * vhat)
            lb = lb + _fold8(dvn)
            dvh = dvn * g
            m1 = jnp.mean(dvh, axis=-1, keepdims=True)
            m2 = jnp.mean(dvh * vhat, axis=-1, keepdims=True)
            dp_ref[1, rows, :] = (rstd * (dvh - m1 - vhat * m2) * gv_grad).astype(BF16)
            dyb = dyb_t[rows]
            xbv = sv[3, rows, :].astype(F32)
            cbv = sv[4, rows, :].astype(F32)
            bbv = sv[5, rows, :].astype(F32)
            zbv = sv[6, rows, :].astype(F32)
            conv = sv[7, rows, :].astype(F32)
            szb, szb_grad = _silu(zbv)
            dp_ref[5, rows, :] = (dyb * conv * szb).astype(BF16)
            dp_ref[6, rows, :] = (dyb * bbv * conv * szb_grad).astype(BF16)
            dconv = dyb * bbv * szb
            dbuf[r0:r0 + CHUNK, :] = dconv
            dc1 = dbuf[r0 + 1:r0 + 1 + CHUNK, :]
            dc2 = dbuf[r0 + 2:r0 + 2 + CHUNK, :]
            if ck == 0:
                carry[...] = dconv[0:SUBLANES, :]
            h = cbv * xbv
            c2 = c2 + _fold8(dconv * h)
            c1 = c1 + _fold8(dc1 * h)
            c0 = c0 + _fold8(dc2 * h)
            cb = cb + _fold8(dconv)
            dh = w2 * dconv + w1 * dc1 + w0 * dc2
            dp_ref[3, rows, :] = (dh * cbv).astype(BF16)
            dp_ref[4, rows, :] = (dh * xbv).astype(BF16)
            dp_ref[7, rows, :] = dgt[0, rows, :]
            dp_ref[8, rows, :] = dgt[1, rows, :]
        a_ws[...] = gws
        a_bs[...] = gbs
        a_lg[...], a_lb[...] = lg, lb
        a_c0[...], a_c1[...], a_c2[...], a_cb[...] = c0, c1, c2, cb

        @pl.when(i == nt - 1)
        def _():
            gws_ref[0] = jnp.where(_tril_mask(), a_ws[...], 0.0)
            gbs_ref[0] = jnp.sum(a_bs[...], axis=1, keepdims=True)
            glg_ref[...] = jnp.sum(a_lg[...], axis=0, keepdims=True)
            glb_ref[...] = jnp.sum(a_lb[...], axis=0, keepdims=True)
            gcw_ref[0:1, :] = jnp.sum(a_c0[...], axis=0, keepdims=True)
            gcw_ref[1:2, :] = jnp.sum(a_c1[...], axis=0, keepdims=True)
            gcw_ref[2:3, :] = jnp.sum(a_c2[...], axis=0, keepdims=True)
            gcb_ref[...] = jnp.sum(a_cb[...], axis=0, keepdims=True)

    def rev(i):
        return nt - 1 - i

    row = pl.BlockSpec((tm, d), lambda g, i: (rev(i), 0))
    wrow = pl.BlockSpec((hd, d), lambda g, i: (g, 0))
    vec = pl.BlockSpec((1, hd), lambda g, i: (0, g))
    acc8 = pltpu.VMEM((SUBLANES, hd), F32)
    return pl.pallas_call(
        body, name="mixer_bwd", grid=(HEADS, nt),
        in_specs=[row, row, wrow, wrow,
                  pl.BlockSpec((N_SAVE, tm, hd), lambda g, i: (0, rev(i), g)),
                  pl.BlockSpec((2, tm, hd), lambda g, i: (0, rev(i), g)),
                  vec, vec,
                  pl.BlockSpec((1, CHUNK, CHUNK), lambda g, i: (g, 0, 0)),
                  pl.BlockSpec((1, CHUNK, 1), lambda g, i: (g, 0, 0)),
                  pl.BlockSpec((3, hd), lambda g, i: (0, g))],
        out_specs=[pl.BlockSpec((N_IN, tm, hd), lambda g, i: (0, rev(i), g)),
                   pl.BlockSpec((1, CHUNK, CHUNK), lambda g, i: (g, 0, 0)),
                   pl.BlockSpec((1, CHUNK, 1), lambda g, i: (g, 0, 0)),
                   vec, vec,
                   pl.BlockSpec((3, hd), lambda g, i: (0, g)),
                   vec],
        out_shape=[jax.ShapeDtypeStruct((N_IN, t_len, d), BF16),
                   jax.ShapeDtypeStruct((HEADS, CHUNK, CHUNK), F32),
                   jax.ShapeDtypeStruct((HEADS, CHUNK, 1), F32),
                   jax.ShapeDtypeStruct((1, d), F32), jax.ShapeDtypeStruct((1, d), F32),
                   jax.ShapeDtypeStruct((3, d), F32), jax.ShapeDtypeStruct((1, d), F32)],
        scratch_shapes=[pltpu.VMEM((tm + SUBLANES, hd), F32), acc8,
                        pltpu.VMEM((CHUNK, CHUNK), F32), pltpu.VMEM((CHUNK, hd), F32),
                        acc8, acc8, acc8, acc8, acc8, acc8],
        compiler_params=_cparams(("parallel", "arbitrary")),
    )(da, db, w_oa, w_ob, saved, dgate, ln_v_g, ln_v_b, w_s, b_s3, conv_w)


def _grad_w_in(xb, dp):
    t_len, d = xb.shape
    tm, tn, tk = _tile(d, MM_TILE), _tile(d, MM_TILE), _tile(t_len, MM_DEPTH_TOKENS)
    nj = d // tn
    nk = t_len // tk

    def body(a_ref, b_ref, o_ref, acc):
        k = pl.program_id(2)

        @pl.when(k == 0)
        def _():
            acc[...] = jnp.zeros_like(acc)

        acc[...] += _dot_tn(a_ref[...], b_ref[...])

        @pl.when(k == nk - 1)
        def _():
            o_ref[...] = acc[...].astype(BF16)

    return pl.pallas_call(
        body, name="grad_w_in", grid=(d // tm, N_IN * nj, nk),
        in_specs=[pl.BlockSpec((tk, tm), lambda i, j, k: (k, i)),
                  pl.BlockSpec((None, tk, tn), lambda i, j, k: (j // nj, k, j % nj))],
        out_specs=pl.BlockSpec((tm, tn), lambda i, j, k: (i, j)),
        out_shape=jax.ShapeDtypeStruct((d, N_IN * d), BF16),
        scratch_shapes=[pltpu.VMEM((tm, tn), F32)],
        compiler_params=_cparams(("parallel", "parallel", "arbitrary")),
    )(xb, dp)


def _grad_x_tail(dp, w_in, partial, slots):
    _, t_len, d = dp.shape
    lo, hi = slots
    tm, tn, tk = _tile(t_len, MM_TILE), _tile(d, MM_TILE), _tile(d, MM_DEPTH)
    nkb = d // tk
    nk = (hi - lo) * nkb

    def body(a_ref, b_ref, r_ref, o_ref, acc):
        k = pl.program_id(2)

        @pl.when(k == 0)
        def _():
            acc[...] = r_ref[...]

        acc[...] += _dot(a_ref[...], b_ref[...])

        @pl.when(k == nk - 1)
        def _():
            o_ref[...] = acc[...]

    return pl.pallas_call(
        body, name="grad_x_tail", grid=(t_len // tm, d // tn, nk),
        in_specs=[pl.BlockSpec((None, tm, tk), lambda i, j, k: (lo + k // nkb, i, k % nkb)),
                  pl.BlockSpec((tk, tn), lambda i, j, k: (lo * nkb + k, j)),
                  pl.BlockSpec((tm, tn), lambda i, j, k: (i, j))],
        out_specs=pl.BlockSpec((tm, tn), lambda i, j, k: (i, j)),
        out_shape=jax.ShapeDtypeStruct((t_len, d), F32),
        scratch_shapes=[pltpu.VMEM((tm, tn), F32)],
        compiler_params=_cparams(("parallel", "parallel", "arbitrary")),
    )(dp, w_in, partial)


def _grad_x(dp, w_in, gx_direct, parts, packed, slots):
    _, t_len, d = dp.shape
    tm, tn, tk = _tile(t_len, MM_TILE), _tile(d, MM_TILE), _tile(d, MM_DEPTH)
    nkb = d // tk
    lo, hi = slots
    nk = (hi - lo) * nkb
    ni, nj = t_len // tm, d // tn
    n_parts = len(parts)

    def body(a_ref, b_ref, r_ref, *rest):
        srcs = rest[:n_parts]
        pk_ref = rest[n_parts]
        o_ref = rest[n_parts + 1]
        lands = rest[n_parts + 2:2 * n_parts + 2]
        gath = rest[2 * n_parts + 2]
        acc, send_sems, recv_sems, pk_send, pk_recv, pk_local = rest[2 * n_parts + 3:]
        i, j, k = pl.program_id(0), pl.program_id(1), pl.program_id(2)
        x, y, c = lax.axis_index("x"), lax.axis_index("y"), lax.axis_index("c")
        my_chip = 2 * x + y
        me = 4 * x + 2 * y + c
        chips = [(1 - x, y), (x, 1 - y), (1 - x, 1 - y)]

        def part_copy(t, n):
            px, py = chips[n]
            return pltpu.make_async_remote_copy(
                src_ref=srcs[t].at[2 * px + py], dst_ref=lands[t].at[my_chip],
                send_sem=send_sems.at[3 * t + n], recv_sem=recv_sems.at[3 * t + n],
                device_id=(px, py, c), device_id_type=MESH)

        def part_landing(t, n):
            px, py = chips[n]
            return pltpu.make_async_remote_copy(
                src_ref=srcs[t].at[my_chip], dst_ref=lands[t].at[2 * px + py],
                send_sem=send_sems.at[3 * t + n], recv_sem=recv_sems.at[3 * t + n],
                device_id=(px, py, c), device_id_type=MESH)

        def pk_copy(s):
            return pltpu.make_async_remote_copy(
                src_ref=pk_ref, dst_ref=gath.at[me], send_sem=pk_send, recv_sem=pk_recv.at[me],
                device_id=(s // 4, (s // 2) % 2, s % 2), device_id_type=MESH)

        @pl.when((i == 0) & (j == 0) & (k == 0))
        def _():
            for t in range(n_parts):
                for n in range(3):
                    part_copy(t, n).start()
            pltpu.make_async_copy(pk_ref, gath.at[me], pk_local).start()
            for s in range(N_DEV):
                @pl.when(s != me)
                def _(s=s):
                    pk_copy(s).start()

        @pl.when(k == 0)
        def _():
            acc[...] = r_ref[...]

        acc[...] += _dot(a_ref[...], b_ref[...])

        @pl.when(k == nk - 1)
        def _():
            o_ref[...] = acc[...]

        @pl.when((i == ni - 1) & (j == nj - 1) & (k == nk - 1))
        def _():
            for t in range(n_parts):
                for n in range(3):
                    part_landing(t, n).wait_recv()
            for t in range(n_parts):
                for n in range(3):
                    part_copy(t, n).wait_send()
            for s in range(N_DEV):
                @pl.when(s != me)
                def _(s=s):
                    pltpu.make_async_remote_copy(
                        src_ref=pk_ref, dst_ref=gath.at[s], send_sem=pk_send, recv_sem=pk_recv.at[s],
                        device_id=(s // 4, (s // 2) % 2, s % 2), device_id_type=MESH).wait_recv()
            seven = gath.at[pl.ds(0, N_DEV - 1)]
            pltpu.make_async_remote_copy(src_ref=seven, dst_ref=seven, send_sem=pk_send, recv_sem=pk_send,
                                         device_id=(x, y, 1 - c), device_id_type=MESH).wait_send()
            pltpu.make_async_copy(pk_ref, gath.at[me], pk_local).wait()

    hbm = pl.BlockSpec(memory_space=pl.ANY)
    outs = pl.pallas_call(
        body, name="grad_x", grid=(ni, nj, nk),
        in_specs=[pl.BlockSpec((None, tm, tk), lambda i, j, k: (lo + k // nkb, i, k % nkb)),
                  pl.BlockSpec((tk, tn), lambda i, j, k: (lo * nkb + k, j)),
                  pl.BlockSpec((tm, tn), lambda i, j, k: (i, j))] + [hbm] * (n_parts + 1),
        out_specs=[pl.BlockSpec((tm, tn), lambda i, j, k: (i, j))] + [hbm] * (n_parts + 1),
        out_shape=[jax.ShapeDtypeStruct((t_len, d), F32)] + [jax.ShapeDtypeStruct(p.shape, p.dtype) for p in parts]
        + [jax.ShapeDtypeStruct((N_DEV,) + packed.shape, packed.dtype)],
        scratch_shapes=[pltpu.VMEM((tm, tn), F32),
                        pltpu.SemaphoreType.DMA((3 * n_parts,)), pltpu.SemaphoreType.DMA((3 * n_parts,)),
                        pltpu.SemaphoreType.DMA(()), pltpu.SemaphoreType.DMA((N_DEV,)), pltpu.SemaphoreType.DMA(())],
        compiler_params=_cparams(("arbitrary", "arbitrary", "arbitrary")),
    )(dp, w_in, gx_direct, *parts, packed)
    return outs[0], list(outs[1:1 + n_parts]), outs[1 + n_parts]


def kernel(x, w_in, b_gate, ln_v_g, ln_v_b, w_s, b_s, conv_w, conv_b, w_oa, w_ob, w_out, ln_g, ln_b, loss_target, m_w_in, m_b_gate, m_ln_v_g, m_ln_v_b, m_w_s, m_b_s, m_conv_w, m_conv_b, m_w_oa, m_w_ob, m_w_out, m_ln_g, m_ln_b, v_w_in, v_b_gate, v_ln_v_g, v_ln_v_b, v_w_s, v_b_s, v_conv_w, v_conv_b, v_w_oa, v_w_ob, v_w_out, v_ln_g, v_ln_b):
    _, t_len, d = x.shape
    assert d % (HEADS * 128) == 0 and t_len % CHUNK == 0 and w_in.shape[2] * N_DEV == N_IN * d
    x2 = x[0]
    tgt2 = loss_target[0]
    c_arr = lax.axis_index("c").astype(jnp.int32).reshape(1)
    chip_arr = (2 * lax.axis_index("x") + lax.axis_index("y")).astype(jnp.int32).reshape(1)
    dev = 4 * lax.axis_index("x") + 2 * lax.axis_index("y") + lax.axis_index("c")

    shards = [_cast_transposed(w_in[0], "cast_w_in"), _cast_bf16(w_oa[0], "cast_w_oa"),
              _cast_bf16(w_ob[0], "cast_w_ob"), _cast_bf16(w_out[0], "cast_w_out")]
    (conv_w_g,) = _all_gather([conv_w[0]], ["lead"], "gather_conv_w", vmem=True)
    conv_w_f = jnp.transpose(conv_w_g, (1, 0, 2)).reshape(3, d)
    w_s3 = w_s[0]
    b_s3 = b_s[0].reshape(HEADS, CHUNK, 1)

    saved, ya, yb, w_in_f, w_oa_f, w_ob_f, w_out_f = _mixer_fwd(
        x2, shards, ln_v_g, ln_v_b, w_s3, b_s3, conv_w_f, conv_b)
    sa, sb, pa, pb, merged = _merge_fwd(x2, ya, yb, w_in_f, w_oa_f, w_ob_f, b_gate)
    dz, gx_direct, xb, da, db, dgate, g_ln_g, g_ln_b, err2, g_bga, g_bgb = _out_ln_loss(
        merged, w_out_f, x2, tgt2, ln_g, ln_b, sa, sb, pa, pb)
    loss = lax.psum(0.5 * jnp.sum(err2) / d, ("x", "y", "c"))

    dp, g_ws, g_bs, g_lvg, g_lvb, g_cw, g_cb = _mixer_bwd(
        da, db, w_oa_f, w_ob_f, saved, dgate, ln_v_g, ln_v_b, w_s3, b_s3, conv_w_f)
    gw_in = _grad_w_in(xb, dp)

    gw_out, land_in = _grad_w(merged, dz, "grad_w_out", exchange=gw_in)
    part_in = _pair_sum(gw_in, land_in, "col", c_arr, "grad_pair_sum_0")
    gw_oa = _grad_w(ya, da, "grad_w_oa")
    gw_ob = _grad_w(yb, db, "grad_w_ob")
    rows = [gw_oa, gw_ob, gw_out]
    lands = _pair_exchange(rows, ["row"] * 3, "grad_pair_exchange")
    parts = [part_in] + [_pair_sum(g, l, "row", c_arr, "grad_pair_sum_%d" % (n + 1))
                         for n, (g, l) in enumerate(zip(rows, lands))]
    pieces = [jnp.concatenate([g_bga, g_bgb], axis=1), g_lvg, g_lvb, g_ws, g_bs, g_cw, g_cb, g_ln_g, g_ln_b]
    sizes = [p.size for p in pieces]
    packed = jnp.concatenate([p.reshape(-1, 128) for p in pieces], axis=0)
    gx_part, lands2, gathered = _grad_x(dp, w_in_f, gx_direct, parts, packed, (0, GRAD_X_COMM_BLOCKS))
    grad_x = _grad_x_tail(dp, w_in_f, gx_part, (GRAD_X_COMM_BLOCKS, N_IN))[None]

    big = []
    for n, (w, m, v) in enumerate([(w_in, m_w_in, v_w_in), (w_oa, m_w_oa, v_w_oa), (w_ob, m_w_ob, v_w_ob),
                                   (w_out, m_w_out, v_w_out)]):
        big.append([o[None] for o in _sum_adam(parts[n], lands2[n], w[0], m[0], v[0], chip_arr, "sum_adam_%d" % n)])
    (g_w_in, d_w_in, nm_w_in, nv_w_in), (g_w_oa, d_w_oa, nm_w_oa, nv_w_oa), \
        (g_w_ob, d_w_ob, nm_w_ob, nv_w_ob), (g_w_out, d_w_out, nm_w_out, nv_w_out) = big

    total = _small_sum(gathered, "sum_small_grads")
    offs = [0]
    for s in sizes:
        offs.append(offs[-1] + s // 128)
    unpacked = [total[offs[n]:offs[n + 1]] for n in range(len(pieces))]
    g_b_gate = unpacked[0].reshape(b_gate.shape)
    g_ln_v_g = unpacked[1].reshape(ln_v_g.shape)
    g_ln_v_b = unpacked[2].reshape(ln_v_b.shape)
    g_w_s = unpacked[3].reshape(w_s.shape)
    g_b_s = unpacked[4].reshape(b_s.shape)
    g_conv_w = lax.dynamic_slice_in_dim(unpacked[5].reshape(3, d), dev * (d // N_DEV), d // N_DEV, axis=1)[None]
    g_conv_b = unpacked[6].reshape(conv_b.shape)
    g_ln_g2 = unpacked[7].reshape(ln_g.shape)
    g_ln_b2 = unpacked[8].reshape(ln_b.shape)

    small_w = [b_gate, ln_v_g, ln_v_b, w_s, b_s, conv_w, conv_b, ln_g, ln_b]
    small_g = [g_b_gate, g_ln_v_g, g_ln_v_b, g_w_s, g_b_s, g_conv_w, g_conv_b, g_ln_g2, g_ln_b2]
    small_m = [m_b_gate, m_ln_v_g, m_ln_v_b, m_w_s, m_b_s, m_conv_w, m_conv_b, m_ln_g, m_ln_b]
    small_v = [v_b_gate, v_ln_v_g, v_ln_v_b, v_w_s, v_b_s, v_conv_w, v_conv_b, v_ln_g, v_ln_b]

    def flat(a):
        return a.reshape(-1, a.shape[-1])

    res = _small_adam([flat(a) for a in small_w], [flat(a) for a in small_g], [flat(a) for a in small_m],
                      [flat(a) for a in small_v], "adam_small")
    ns = len(small_w)
    d_s = [res[n].reshape(small_w[n].shape) for n in range(ns)]
    nm_s = [res[ns + n].reshape(small_w[n].shape) for n in range(ns)]
    nv_s = [res[2 * ns + n].reshape(small_w[n].shape) for n in range(ns)]

    def ordered(first, small, oa, ob, out):
        return [first] + small[:7] + [oa, ob, out] + small[7:]

    return (loss, grad_x,
            *ordered(g_w_in, small_g, g_w_oa, g_w_ob, g_w_out),
            *ordered(d_w_in, d_s, d_w_oa, d_w_ob, d_w_out),
            *ordered(nm_w_in, nm_s, nm_w_oa, nm_w_ob, nm_w_out),
            *ordered(nv_w_in, nv_s, nv_w_oa, nv_w_ob, nv_w_out))
```

```python
import functools

import jax
import jax.numpy as jnp
from jax import lax
from jax.experimental import pallas as pl
from jax.experimental.pallas import tpu as pltpu

F32 = jnp.float32
BF16 = jnp.bfloat16
MESH = pl.DeviceIdType.MESH

N_DEV = 8
N_CHIP = 4
HEADS = 8
CHUNK = 128
N_IN = 9
N_MIX = 7
N_SAVE = 8
LN_EPS = 1e-5
DN_ALPHA = 2.0 ** 0.25
ADAM_LR = 0.001
ADAM_B1 = 0.9
ADAM_B2 = 0.999
ADAM_EPS = 1e-08
ADAM_WD = 0.01
ADAM_STEP = 10
GELU_C0 = 0.7978845608028654
GELU_C1 = 0.044715
SUBLANES = 8
FIRST_SENDS = 3
SENDS_PER_GROUP = 2
MM_TILE = 1024
MM_DEPTH = 2048
MM_DEPTH_TOKENS = 4096
GRAD_X_COMM_BLOCKS = 8
VMEM_LIMIT = 56 << 20


def _cparams(sem):
    return pltpu.CompilerParams(dimension_semantics=sem, vmem_limit_bytes=VMEM_LIMIT)


def _tile(n, want):
    t = min(n, want)
    while n % t:
        t //= 2
    return t


def _gelu(u):
    u2 = u * u
    t = jnp.tanh(u * (GELU_C0 + (GELU_C0 * GELU_C1) * u2))
    hp = 0.5 * t + 0.5
    grad = hp + (0.5 * u) * (1.0 - t * t) * (GELU_C0 + (3.0 * GELU_C0 * GELU_C1) * u2)
    return u * hp, grad


def _silu(z):
    s = jax.nn.sigmoid(z)
    sil = z * s
    return sil, s + sil * (1.0 - s)


def _fold8(a):
    return jnp.sum(a.reshape(a.shape[0] // SUBLANES, SUBLANES, a.shape[1]), axis=0)


def _dot(a, b):
    return jnp.dot(a, b, preferred_element_type=F32)


def _dot_nt(a, b):
    return lax.dot_general(a, b, (((1,), (1,)), ((), ())), preferred_element_type=F32)


def _dot_tn(a, b):
    return lax.dot_general(a, b, (((0,), (0,)), ((), ())), preferred_element_type=F32)


def _tril_mask():
    r = lax.broadcasted_iota(jnp.int32, (CHUNK, CHUNK), 0)
    c = lax.broadcasted_iota(jnp.int32, (CHUNK, CHUNK), 1)
    return c <= r


def _cast_bf16(a, name):
    rows, cols = a.shape
    rb = _tile(rows, 256)

    def body(a_ref, o_ref):
        o_ref[...] = a_ref[...].astype(BF16)

    return pl.pallas_call(
        body, name=name, grid=(rows // rb,),
        in_specs=[pl.BlockSpec((rb, cols), lambda i: (i, 0))],
        out_specs=pl.BlockSpec((rb, cols), lambda i: (i, 0)),
        out_shape=jax.ShapeDtypeStruct((rows, cols), BF16),
        compiler_params=_cparams(("parallel",)),
    )(a)


def _cast_transposed(a, name):
    rows, cols = a.shape
    tb = _tile(rows, 2048)
    tc = _tile(cols, 256)

    def body(a_ref, o_ref):
        o_ref[...] = a_ref[...].T.astype(BF16)

    return pl.pallas_call(
        body, name=name, grid=(cols // tc, rows // tb),
        in_specs=[pl.BlockSpec((tb, tc), lambda j, i: (i, j))],
        out_specs=pl.BlockSpec((tc, tb), lambda j, i: (j, i)),
        out_shape=jax.ShapeDtypeStruct((cols, rows), BF16),
        compiler_params=_cparams(("parallel", "parallel")),
    )(a)


def _shard_ref(full, kind, s, n):
    if kind == "col":
        return full.at[:, pl.ds(pl.multiple_of(s * n, 128), n)]
    if kind == "row":
        return full.at[pl.ds(pl.multiple_of(s * n, SUBLANES), n), :]
    return full.at[s]


def _all_gather(shards, kinds, name, vmem):
    nt = len(shards)
    out_shapes = []
    for a, kind in zip(shards, kinds):
        if kind == "col":
            out_shapes.append(jax.ShapeDtypeStruct((a.shape[0], N_DEV * a.shape[1]), a.dtype))
        elif kind == "row":
            out_shapes.append(jax.ShapeDtypeStruct((N_DEV * a.shape[0], a.shape[1]), a.dtype))
        else:
            out_shapes.append(jax.ShapeDtypeStruct((N_DEV,) + a.shape, a.dtype))

    def body(*refs):
        srcs, fulls = refs[:nt], refs[nt:2 * nt]
        send_sems, recv_sems, local_sems = refs[2 * nt:]
        x, y, c = lax.axis_index("x"), lax.axis_index("y"), lax.axis_index("c")
        sibling = (x, y, 1 - c)
        chips = [(1 - x, y), (x, 1 - y), (1 - x, 1 - y)]

        def dev(px, py, pc):
            return 4 * px + 2 * py + pc

        def region(t, s):
            a, kind = shards[t], kinds[t]
            n = a.shape[1] if kind == "col" else a.shape[0]
            return _shard_ref(fulls[t], kind, s, n)

        def copy(t, k, block, to, own=False):
            return pltpu.make_async_remote_copy(
                src_ref=srcs[t] if own else region(t, block), dst_ref=region(t, block),
                send_sem=send_sems.at[7 * t + k], recv_sem=recv_sems.at[7 * t + k],
                device_id=to, device_id_type=MESH)

        me = dev(x, y, c)
        started = []
        for t in range(nt):
            mine = pltpu.make_async_copy(srcs[t], region(t, me), local_sems.at[t])
            mine.start()
            started.append(mine)
        first = []
        for t in range(nt):
            first.append(copy(t, 0, me, sibling, own=True))
            for j, chip in enumerate(chips):
                first.append(copy(t, 1 + j, me, (*chip, c), own=True))
        for cp in first:
            cp.start()
        passed = []
        for t in range(nt):
            for j, chip in enumerate(chips):
                blk = dev(*chip, c)
                copy(t, 1 + j, blk, sibling).wait_recv()
                fwd = copy(t, 4 + j, blk, sibling)
                fwd.start()
                passed.append(fwd)
        for t in range(nt):
            copy(t, 0, dev(x, y, 1 - c), sibling).wait_recv()
            for j, chip in enumerate(chips):
                copy(t, 4 + j, dev(*chip, 1 - c), sibling).wait_recv()
        for cp in first + passed:
            cp.wait_send()
        for mine in started:
            mine.wait()

    space = pltpu.VMEM if vmem else pl.ANY
    return pl.pallas_call(
        body, name=name,
        in_specs=[pl.BlockSpec(memory_space=space)] * nt,
        out_specs=[pl.BlockSpec(memory_space=space)] * nt,
        out_shape=out_shapes,
        scratch_shapes=[pltpu.SemaphoreType.DMA((7 * nt,)), pltpu.SemaphoreType.DMA((7 * nt,)),
                        pltpu.SemaphoreType.DMA((nt,))],
        compiler_params=pltpu.CompilerParams(vmem_limit_bytes=VMEM_LIMIT, has_side_effects=True),
    )(*shards)


def _pair_exchange(grads, kinds, name):
    nt = len(grads)
    shard_shapes = []
    for g, kind in zip(grads, kinds):
        shard_shapes.append((g.shape[0], g.shape[1] // N_DEV) if kind == "col" else (g.shape[0] // N_DEV, g.shape[1]))

    def body(*refs):
        srcs, lands = refs[:nt], refs[nt:2 * nt]
        send_sems, recv_sems = refs[2 * nt:]
        x, y, c = lax.axis_index("x"), lax.axis_index("y"), lax.axis_index("c")
        copies = []
        for t in range(nt):
            n = shard_shapes[t][1] if kinds[t] == "col" else shard_shapes[t][0]
            for k in range(N_CHIP):
                cp = pltpu.make_async_remote_copy(
                    src_ref=_shard_ref(srcs[t], kinds[t], 2 * k + 1 - c, n), dst_ref=lands[t].at[k],
                    send_sem=send_sems.at[N_CHIP * t + k], recv_sem=recv_sems.at[N_CHIP * t + k],
                    device_id=(x, y, 1 - c), device_id_type=MESH)
                cp.start()
                copies.append(cp)
        for cp in copies:
            cp.wait()

    return pl.pallas_call(
        body, name=name,
        in_specs=[pl.BlockSpec(memory_space=pl.ANY)] * nt,
        out_specs=[pl.BlockSpec(memory_space=pl.ANY)] * nt,
        out_shape=[jax.ShapeDtypeStruct((N_CHIP,) + s, g.dtype) for s, g in zip(shard_shapes, grads)],
        scratch_shapes=[pltpu.SemaphoreType.DMA((N_CHIP * nt,)), pltpu.SemaphoreType.DMA((N_CHIP * nt,))],
        compiler_params=pltpu.CompilerParams(has_side_effects=True),
    )(*grads)


def _pair_sum(grad, land, kind, c_arr, name):
    _, r, w = land.shape
    rb = _tile(r, 256)
    nrb = r // rb

    def body(c_ref, g_ref, l_ref, o_ref):
        o_ref[...] = (g_ref[...].astype(F32) + l_ref[...].astype(F32)).astype(o_ref.dtype)

    if kind == "col":
        g_spec = pl.BlockSpec((rb, w), lambda k, i, c: (i, 2 * k + c[0]))
    else:
        g_spec = pl.BlockSpec((rb, w), lambda k, i, c: ((2 * k + c[0]) * nrb + i, 0))
    return pl.pallas_call(
        body, name=name,
        grid_spec=pltpu.PrefetchScalarGridSpec(
            num_scalar_prefetch=1, grid=(N_CHIP, nrb),
            in_specs=[g_spec, pl.BlockSpec((None, rb, w), lambda k, i, c: (k, i, 0))],
            out_specs=pl.BlockSpec((None, rb, w), lambda k, i, c: (k, i, 0))),
        out_shape=jax.ShapeDtypeStruct(land.shape, BF16),
        compiler_params=_cparams(("parallel", "parallel")),
    )(c_arr, grad, land)


def _adam(w, g, m, v):
    m = ADAM_B1 * m + (1.0 - ADAM_B1) * g
    v = ADAM_B2 * v + (1.0 - ADAM_B2) * jnp.square(g)
    m_hat = m / (1.0 - ADAM_B1 ** ADAM_STEP)
    v_hat = v / (1.0 - ADAM_B2 ** ADAM_STEP)
    delta = -ADAM_LR * (m_hat / (jnp.sqrt(v_hat) + ADAM_EPS) + ADAM_WD * w)
    return delta, m, v


def _sum_adam(part, land, w, m, v, chip_arr, name):
    r, wd = w.shape
    rb = _tile(r, 128)

    def body(k_ref, own, r1, r2, r3, w_ref, m_ref, v_ref, g_out, d_out, m_out, v_out):
        g = own[...].astype(F32) + r1[...].astype(F32) + r2[...].astype(F32) + r3[...].astype(F32)
        d, mn, vn = _adam(w_ref[...], g, m_ref[...], v_ref[...])
        g_out[...] = g
        d_out[...] = d
        m_out[...] = mn
        v_out[...] = vn

    def slot(off):
        return pl.BlockSpec((None, rb, wd), lambda i, k: ((k[0] + off) % N_CHIP, i, 0))

    plain = pl.BlockSpec((rb, wd), lambda i, k: (i, 0))
    return pl.pallas_call(
        body, name=name,
        grid_spec=pltpu.PrefetchScalarGridSpec(
            num_scalar_prefetch=1, grid=(r // rb,),
            in_specs=[slot(0), slot(1), slot(2), slot(3), plain, plain, plain],
            out_specs=[plain] * 4),
        out_shape=[jax.ShapeDtypeStruct(w.shape, F32)] * 4,
        compiler_params=_cparams(("parallel",)),
    )(chip_arr, part, land, land, land, w, m, v)


def _small_sum(gathered, name):
    _, r, w = gathered.shape

    def body(g_ref, o_ref):
        acc = g_ref[0]
        for d in range(1, N_DEV):
            acc = acc + g_ref[d]
        o_ref[...] = acc

    return pl.pallas_call(body, name=name, out_shape=jax.ShapeDtypeStruct((r, w), F32))(gathered)


def _small_adam(ws, gs, ms, vs, name):
    n = len(ws)

    def body(*refs):
        ins, outs = refs[:4 * n], refs[4 * n:]
        for t in range(n):
            d, mn, vn = _adam(ins[t][...], ins[n + t][...], ins[2 * n + t][...], ins[3 * n + t][...])
            outs[t][...] = d
            outs[n + t][...] = mn
            outs[2 * n + t][...] = vn

    shapes = [jax.ShapeDtypeStruct(w.shape, F32) for w in ws]
    return pl.pallas_call(body, name=name, out_shape=shapes * 3)(*ws, *gs, *ms, *vs)


def _mixer_fwd(x, shards, ln_v_g, ln_v_b, w_s, b_s3, conv_w, conv_b):
    t_len, d = x.shape
    hd = d // HEADS
    tm = _tile(t_len, 512)
    nt = t_len // tm
    assert nt >= 2
    w8 = shards[0].shape[0]
    bps = w8 // hd
    r8 = shards[1].shape[0]
    nq = N_IN * HEADS
    n_blocks = nq + 3 * N_DEV
    groups = [[("in", HEADS * b + HEADS - 1 - p) for b in range(N_MIX)] for p in range(HEADS)]
    groups.append([("in", q) for q in range(N_MIX * HEADS, nq)] + [(t, s) for t in (1, 2, 3) for s in range(N_DEV)])

    def owner_of(blk):
        return blk[1] // bps if blk[0] == "in" else blk[1]

    send_step = {}
    for s in range(N_DEV):
        mine = [blk for grp in groups for blk in grp if owner_of(blk) == s]
        for pos, blk in enumerate(mine):
            send_step[blk] = 0 if pos < FIRST_SENDS else 1 + (pos - FIRST_SENDS) // SENDS_PER_GROUP
    assert max(send_step.values()) < HEADS
    assert all(send_step[blk] <= max(k - 1, 0) for k in range(HEADS) for blk in groups[k])
    tail_pass_step = nt - 6 if nt >= 8 else nt - 2

    def body(x_ref, sh_in, sh_oa, sh_ob, sh_out, lng, lnb, ws_ref, bs_ref, cw_ref, cb_ref,
             save_ref, ya_ref, yb_ref, f_in, f_oa, f_ob, f_out,
             hbuf, wbuf, recv_sems, own_sems, fwd_sems, local_sems, load_sems):
        g = pl.program_id(0)
        i = pl.program_id(1)
        x, y, c = lax.axis_index("x"), lax.axis_index("y"), lax.axis_index("c")
        sibling = (x, y, 1 - c)
        chips = [(1 - x, y), (x, 1 - y), (1 - x, 1 - y)]
        shard_refs = (sh_in, sh_oa, sh_ob, sh_out)
        fulls = (f_in, f_oa, f_ob, f_out)

        def tensor(blk):
            return 0 if blk[0] == "in" else blk[0]

        def owner(blk):
            s = owner_of(blk)
            return s // 4, (s // 2) % 2, s % 2

        def bid(blk):
            return blk[1] if blk[0] == "in" else nq + (blk[0] - 1) * N_DEV + blk[1]

        def region(blk):
            if blk[0] == "in":
                return f_in.at[pl.ds(blk[1] * hd, hd), :]
            return fulls[blk[0]].at[pl.ds(blk[1] * r8, r8), :]

        def own_src(blk):
            if blk[0] == "in":
                return sh_in.at[pl.ds((blk[1] % bps) * hd, hd), :]
            return shard_refs[blk[0]]

        def rcopy(blk, to, send_sem, own):
            return pltpu.make_async_remote_copy(
                src_ref=own_src(blk) if own else region(blk), dst_ref=region(blk),
                send_sem=send_sem, recv_sem=recv_sems.at[bid(blk)], device_id=to, device_id_type=MESH)

        def send_own(blk):
            ox, oy, oc = owner(blk)

            @pl.when((x == ox) & (y == oy) & (c == oc))
            def _():
                rcopy(blk, sibling, own_sems.at[tensor(blk)], True).start()
                for chip in chips:
                    rcopy(blk, (*chip, c), own_sems.at[tensor(blk)], True).start()

        def pass_on(blk):
            ox, oy, oc = owner(blk)

            @pl.when(((x != ox) | (y != oy)) & (c == oc))
            def _():
                rcopy(blk, sibling, fwd_sems.at[tensor(blk)], False).wait_recv()
                rcopy(blk, sibling, fwd_sems.at[tensor(blk)], False).start()

        def wait_from_sibling(blk):
            @pl.when(c != owner(blk)[2])
            def _():
                rcopy(blk, sibling, fwd_sems.at[tensor(blk)], False).wait_recv()

        def local_copy(t):
            me = 4 * x + 2 * y + c
            return pltpu.make_async_copy(shard_refs[t], _shard_ref(fulls[t], "row", me, w8 if t == 0 else r8),
                                         local_sems.at[t])

        first = (g == 0) & (i == 0)

        @pl.when(first)
        def _():
            for t in range(4):
                local_copy(t).start()
            for grp in groups:
                for blk in grp:
                    if send_step[blk] == 0:
                        send_own(blk)
            for blk in groups[0]:
                pass_on(blk)

        @pl.when(i == 0)
        def _():
            for k in range(1, HEADS):
                @pl.when(g == k)
                def _(k=k):
                    for grp in groups:
                        for blk in grp:
                            if send_step[blk] == k:
                                send_own(blk)

        @pl.when(i == nt - 2)
        def _():
            for k in range(HEADS - 1):
                @pl.when(g == k)
                def _(k=k):
                    for blk in groups[k + 1]:
                        pass_on(blk)

        @pl.when((g == HEADS - 1) & (i == tail_pass_step))
        def _():
            for blk in groups[HEADS]:
                pass_on(blk)

        def fetch_weights(k):
            for blk in groups[k]:
                wait_from_sibling(blk)
            for b, blk in enumerate(groups[k]):
                ox, oy, oc = owner(blk)
                mine = (x == ox) & (y == oy) & (c == oc)

                @pl.when(mine)
                def _(b=b, blk=blk):
                    pltpu.make_async_copy(own_src(blk), wbuf.at[k % 2, b], load_sems.at[b]).start()

                @pl.when(jnp.logical_not(mine))
                def _(b=b, blk=blk):
                    pltpu.make_async_copy(region(blk), wbuf.at[k % 2, b], load_sems.at[b]).start()

        @pl.when(i == 0)
        def _():
            for k in range(HEADS):
                @pl.when(g == k)
                def _(k=k):
                    if k == 0:
                        fetch_weights(0)
                    for b, blk in enumerate(groups[k]):
                        pltpu.make_async_copy(region(blk), wbuf.at[k % 2, b], load_sems.at[b]).wait()

        @pl.when(i == nt - 1)
        def _():
            for k in range(1, HEADS):
                @pl.when(g == k - 1)
                def _(k=k):
                    fetch_weights(k)

        xt = x_ref[...].astype(BF16)
        wslot = g % 2
        u = _dot_nt(xt, wbuf[wslot, 0])
        v = _dot_nt(xt, wbuf[wslot, 1])
        z = _dot_nt(xt, wbuf[wslot, 2])
        save_ref[0] = u.astype(BF16)
        save_ref[1] = v.astype(BF16)
        save_ref[2] = z.astype(BF16)
        gu, _ = _gelu(u)
        gv, _ = _gelu(v)
        mu = jnp.mean(gv, axis=-1, keepdims=True)
        dv = gv - mu
        var = jnp.mean(dv * dv, axis=-1, keepdims=True)
        vn = (dv * lax.rsqrt(var + LN_EPS) * lng[...] + lnb[...]).astype(BF16)
        sz, _ = _silu(z)
        gate = gu * sz
        wm = jnp.where(_tril_mask(), ws_ref[0], 0.0).astype(BF16)
        bs = bs_ref[0]
        for ck in range(tm // CHUNK):
            rows = slice(ck * CHUNK, (ck + 1) * CHUNK)
            mixed = _dot(wm, vn[rows]) + bs
            ya_ref[rows, :] = (gate[rows] * mixed).astype(BF16)

        xbv = _dot_nt(xt, wbuf[wslot, 3])
        cbv = _dot_nt(xt, wbuf[wslot, 4])
        bbv = _dot_nt(xt, wbuf[wslot, 5])
        zbv = _dot_nt(xt, wbuf[wslot, 6])
        save_ref[3] = xbv.astype(BF16)
        save_ref[4] = cbv.astype(BF16)
        save_ref[5] = bbv.astype(BF16)
        save_ref[6] = zbv.astype(BF16)
        h = cbv * xbv

        @pl.when(i == 0)
        def _():
            hbuf[0:SUBLANES, :] = jnp.zeros((SUBLANES, hd), F32)

        hbuf[SUBLANES:SUBLANES + tm, :] = h
        h1 = hbuf[SUBLANES - 1:SUBLANES - 1 + tm, :]
        h2 = hbuf[SUBLANES - 2:SUBLANES - 2 + tm, :]
        conv = cb_ref[...] + cw_ref[0:1, :] * h2 + cw_ref[1:2, :] * h1 + cw_ref[2:3, :] * h
        hbuf[0:SUBLANES, :] = h[tm - SUBLANES:tm, :]
        save_ref[7] = conv.astype(BF16)
        szb, _ = _silu(zbv)
        yb_ref[...] = (bbv * conv * szb).astype(BF16)

        @pl.when((g == HEADS - 1) & (i == nt - 1))
        def _():
            for blk in groups[HEADS]:
                wait_from_sibling(blk)
            for t in range(4):
                local_copy(t).wait()
            for t in range(4):
                n = w8 if t == 0 else r8
                own_all, fwd_all = fulls[t].at[pl.ds(0, 4 * n), :], fulls[t].at[pl.ds(0, 3 * n), :]
                for ref, sem in ((own_all, own_sems.at[t]), (fwd_all, fwd_sems.at[t])):
                    pltpu.make_async_remote_copy(src_ref=ref, dst_ref=ref, send_sem=sem, recv_sem=sem,
                                                 device_id=sibling, device_id_type=MESH).wait_send()

    def head(g):
        return HEADS - 1 - g

    vec = pl.BlockSpec((1, hd), lambda g, i: (0, head(g)))
    hbm = pl.BlockSpec(memory_space=pl.ANY)
    return pl.pallas_call(
        body, name="mixer_fwd", grid=(HEADS, nt),
        in_specs=[pl.BlockSpec((tm, d), lambda g, i: (i, 0)), hbm, hbm, hbm, hbm,
                  vec, vec,
                  pl.BlockSpec((1, CHUNK, CHUNK), lambda g, i: (head(g), 0, 0)),
                  pl.BlockSpec((1, CHUNK, 1), lambda g, i: (head(g), 0, 0)),
                  pl.BlockSpec((3, hd), lambda g, i: (0, head(g))),
                  vec],
        out_specs=[pl.BlockSpec((N_SAVE, tm, hd), lambda g, i: (0, i, head(g))),
                   pl.BlockSpec((tm, hd), lambda g, i: (i, head(g))),
                   pl.BlockSpec((tm, hd), lambda g, i: (i, head(g))),
                   hbm, hbm, hbm, hbm],
        out_shape=[jax.ShapeDtypeStruct((N_SAVE, t_len, d), BF16),
                   jax.ShapeDtypeStruct((t_len, d), BF16),
                   jax.ShapeDtypeStruct((t_len, d), BF16),
                   jax.ShapeDtypeStruct((N_DEV * w8, d), BF16),
                   jax.ShapeDtypeStruct((d, d), BF16), jax.ShapeDtypeStruct((d, d), BF16),
                   jax.ShapeDtypeStruct((d, d), BF16)],
        scratch_shapes=[pltpu.VMEM((SUBLANES + tm, hd), F32), pltpu.VMEM((2, N_MIX, hd, d), BF16),
                        pltpu.SemaphoreType.DMA((n_blocks,)), pltpu.SemaphoreType.DMA((4,)),
                        pltpu.SemaphoreType.DMA((4,)), pltpu.SemaphoreType.DMA((4,)),
                        pltpu.SemaphoreType.DMA((N_MIX,))],
        compiler_params=_cparams(("arbitrary", "arbitrary")),
    )(x, *shards, ln_v_g, ln_v_b, w_s, b_s3, conv_w, conv_b)


def _merge_fwd(x, ya, yb, w_in, w_oa, w_ob, b_gate):
    t_len, d = x.shape
    tm = _tile(t_len, 256)
    tn = _tile(d, 1024)
    nj = d // tn

    def body(x_ref, ya_ref, yb_ref, wga, wgb, woa, wob, bga, bgb, sa_ref, sb_ref, pa_ref, pb_ref, mg_ref):
        xt = x_ref[...].astype(BF16)
        sa = jax.nn.sigmoid(_dot_nt(xt, wga[...]) + bga[...])
        sb = jax.nn.sigmoid(_dot_nt(xt, wgb[...]) + bgb[...])
        pa = _dot(ya_ref[...], woa[...])
        pb = _dot(yb_ref[...], wob[...])
        sa_ref[...] = sa.astype(BF16)
        sb_ref[...] = sb.astype(BF16)
        pa_ref[...] = pa.astype(BF16)
        pb_ref[...] = pb.astype(BF16)
        mg_ref[...] = (sa * pa + sb * pb).astype(BF16)

    row = pl.BlockSpec((tm, d), lambda j, i: (i, 0))
    out = pl.BlockSpec((tm, tn), lambda j, i: (i, j))
    return pl.pallas_call(
        body, name="merge_fwd", grid=(nj, t_len // tm),
        in_specs=[row, row, row,
                  pl.BlockSpec((tn, d), lambda j, i: (7 * nj + j, 0)),
                  pl.BlockSpec((tn, d), lambda j, i: (8 * nj + j, 0)),
                  pl.BlockSpec((d, tn), lambda j, i: (0, j)),
                  pl.BlockSpec((d, tn), lambda j, i: (0, j)),
                  pl.BlockSpec((1, tn), lambda j, i: (0, j)),
                  pl.BlockSpec((1, tn), lambda j, i: (0, nj + j))],
        out_specs=[out] * 5,
        out_shape=[jax.ShapeDtypeStruct((t_len, d), BF16)] * 5,
        compiler_params=_cparams(("parallel", "arbitrary")),
    )(x, ya, yb, w_in, w_in, w_oa, w_ob, b_gate, b_gate)


def _out_ln_loss(merged, w_out, x, target, ln_g, ln_b, sa, sb, pa, pb):
    t_len, d = x.shape
    tm = _tile(t_len, 256)
    nt = t_len // tm

    def body(mg_ref, w_ref, x_ref, t_ref, g_ref, b_ref, sa_ref, sb_ref, pa_ref, pb_ref,
             dz_ref, gx_ref, xb_ref, da_ref, db_ref, dg_ref, glg_ref, glb_ref, ls_ref, ga_ref, gb_ref,
             a_g, a_b, a_l, acc_a, acc_b):
        i = pl.program_id(0)
        xb_ref[...] = x_ref[...].astype(BF16)

        @pl.when(i == 0)
        def _():
            for a in (a_g, a_b, a_l, acc_a, acc_b):
                a[...] = jnp.zeros_like(a)

        zres = DN_ALPHA * x_ref[...] + _dot(mg_ref[...], w_ref[...])
        mu = jnp.mean(zres, axis=-1, keepdims=True)
        dc = zres - mu
        var = jnp.mean(dc * dc, axis=-1, keepdims=True)
        rstd = lax.rsqrt(var + LN_EPS)
        xhat = dc * rstd
        g = g_ref[...]
        err = xhat * g + b_ref[...] - t_ref[...]
        dy = err * (1.0 / d)
        a_l[...] += _fold8(err * err)
        a_g[...] += _fold8(dy * xhat)
        a_b[...] += _fold8(dy)
        dxh = dy * g
        m1 = jnp.mean(dxh, axis=-1, keepdims=True)
        m2 = jnp.mean(dxh * xhat, axis=-1, keepdims=True)
        dz = rstd * (dxh - m1 - xhat * m2)
        dz_bf = dz.astype(BF16)
        dz_ref[...] = dz_bf
        gx_ref[...] = DN_ALPHA * dz

        dm = _dot_nt(dz_bf, w_ref[...])
        sa = sa_ref[...].astype(F32)
        sb = sb_ref[...].astype(F32)
        da = dm * sa
        db = dm * sb
        da_ref[...] = da.astype(BF16)
        db_ref[...] = db.astype(BF16)
        dga = da * pa_ref[...].astype(F32) * (1.0 - sa)
        dgb = db * pb_ref[...].astype(F32) * (1.0 - sb)
        dg_ref[0] = dga.astype(BF16)
        dg_ref[1] = dgb.astype(BF16)
        acc_a[...] += _fold8(dga)
        acc_b[...] += _fold8(dgb)

        @pl.when(i == nt - 1)
        def _():
            glg_ref[...] = jnp.sum(a_g[...], axis=0, keepdims=True)
            glb_ref[...] = jnp.sum(a_b[...], axis=0, keepdims=True)
            ls_ref[...] = jnp.sum(a_l[...], axis=0, keepdims=True)
            ga_ref[...] = jnp.sum(acc_a[...], axis=0, keepdims=True)
            gb_ref[...] = jnp.sum(acc_b[...], axis=0, keepdims=True)

    row = pl.BlockSpec((tm, d), lambda i: (i, 0))
    vec = pl.BlockSpec((1, d), lambda i: (0, 0))
    act = jax.ShapeDtypeStruct((t_len, d), BF16)
    part = jax.ShapeDtypeStruct((1, d), F32)
    return pl.pallas_call(
        body, name="out_ln_loss", grid=(nt,),
        in_specs=[row, pl.BlockSpec((d, d), lambda i: (0, 0)), row, row, vec, vec, row, row, row, row],
        out_specs=[row, row, row, row, row, pl.BlockSpec((2, tm, d), lambda i: (0, i, 0)), vec, vec, vec, vec, vec],
        out_shape=[act, jax.ShapeDtypeStruct((t_len, d), F32), act, act, act,
                   jax.ShapeDtypeStruct((2, t_len, d), BF16), part, part, part, part, part],
        scratch_shapes=[pltpu.VMEM((SUBLANES, d), F32)] * 5,
        compiler_params=_cparams(("arbitrary",)),
    )(merged, w_out, x, target, ln_g, ln_b, sa, sb, pa, pb)


def _grad_w(a, b, name, exchange=None):
    t_len, m = a.shape
    n = b.shape[1]
    tm, tn, tk = _tile(m, MM_TILE), _tile(n, MM_TILE), _tile(t_len, MM_DEPTH_TOKENS)
    ni, nj, nk = m // tm, n // tn, t_len // tk

    def body(a_ref, b_ref, *rest):
        if exchange is None:
            o_ref, acc = rest
        else:
            src, o_ref, land, acc, send_sems, recv_sems = rest
        i, j, k = pl.program_id(0), pl.program_id(1), pl.program_id(2)

        def copies():
            x, y, c = lax.axis_index("x"), lax.axis_index("y"), lax.axis_index("c")
            w = exchange.shape[1] // N_DEV
            return [pltpu.make_async_remote_copy(
                src_ref=_shard_ref(src, "col", 2 * s + 1 - c, w), dst_ref=land.at[s],
                send_sem=send_sems.at[s], recv_sem=recv_sems.at[s],
                device_id=(x, y, 1 - c), device_id_type=MESH) for s in range(N_CHIP)]

        if exchange is not None:
            @pl.when((i == 0) & (j == 0) & (k == 0))
            def _():
                for cp in copies():
                    cp.start()

        @pl.when(k == 0)
        def _():
            acc[...] = jnp.zeros_like(acc)

        acc[...] += _dot_tn(a_ref[...], b_ref[...])

        @pl.when(k == nk - 1)
        def _():
            o_ref[...] = acc[...].astype(BF16)

        if exchange is not None:
            @pl.when((i == ni - 1) & (j == nj - 1) & (k == nk - 1))
            def _():
                for cp in copies():
                    cp.wait()

    in_specs = [pl.BlockSpec((tk, tm), lambda i, j, k: (k, i)), pl.BlockSpec((tk, tn), lambda i, j, k: (k, j))]
    out_specs = [pl.BlockSpec((tm, tn), lambda i, j, k: (i, j))]
    out_shape = [jax.ShapeDtypeStruct((m, n), BF16)]
    scratch = [pltpu.VMEM((tm, tn), F32)]
    args = [a, b]
    if exchange is not None:
        hbm = pl.BlockSpec(memory_space=pl.ANY)
        in_specs.append(hbm)
        out_specs.append(hbm)
        out_shape.append(jax.ShapeDtypeStruct((N_CHIP, exchange.shape[0], exchange.shape[1] // N_DEV), exchange.dtype))
        scratch += [pltpu.SemaphoreType.DMA((N_CHIP,)), pltpu.SemaphoreType.DMA((N_CHIP,))]
        args.append(exchange)
    outs = pl.pallas_call(
        body, name=name, grid=(ni, nj, nk),
        in_specs=in_specs, out_specs=out_specs, out_shape=out_shape, scratch_shapes=scratch,
        compiler_params=_cparams(("arbitrary", "arbitrary", "arbitrary")),
    )(*args)
    return outs[0] if exchange is None else (outs[0], outs[1])


def _mixer_bwd(da, db, w_oa, w_ob, saved, dgate, ln_v_g, ln_v_b, w_s, b_s3, conv_w):
    t_len, d = da.shape
    hd = d // HEADS
    tm = _tile(t_len, 512)
    nt = t_len // tm

    def body(da_ref, db_ref, woa, wob, sv, dgt, lng, lnb, ws_ref, bs_ref, cw_ref,
             dp_ref, gws_ref, gbs_ref, glg_ref, glb_ref, gcw_ref, gcb_ref,
             dbuf, carry, a_ws, a_bs, a_lg, a_lb, a_c0, a_c1, a_c2, a_cb):
        i = pl.program_id(1)

        @pl.when(i == 0)
        def _():
            carry[...] = jnp.zeros_like(carry)
            for a in (a_ws, a_bs, a_lg, a_lb, a_c0, a_c1, a_c2, a_cb):
                a[...] = jnp.zeros_like(a)

        dya_t = _dot_nt(da_ref[...], woa[...])
        dyb_t = _dot_nt(db_ref[...], wob[...])
        g = lng[...]
        beta = lnb[...]
        wm = jnp.where(_tril_mask(), ws_ref[0], 0.0).astype(BF16)
        bs = bs_ref[0]
        w0, w1, w2 = cw_ref[0:1, :], cw_ref[1:2, :], cw_ref[2:3, :]
        dbuf[tm:tm + SUBLANES, :] = carry[...]
        gws = a_ws[...]
        gbs = a_bs[...]
        lg, lb = a_lg[...], a_lb[...]
        c0, c1, c2, cb = a_c0[...], a_c1[...], a_c2[...], a_cb[...]
        for ck in reversed(range(tm // CHUNK)):
            r0 = ck * CHUNK
            rows = slice(r0, r0 + CHUNK)
            dya = dya_t[rows]
            u = sv[0, rows, :].astype(F32)
            v = sv[1, rows, :].astype(F32)
            z = sv[2, rows, :].astype(F32)
            gu, gu_grad = _gelu(u)
            gv, gv_grad = _gelu(v)
            mu = jnp.mean(gv, axis=-1, keepdims=True)
            dvc = gv - mu
            var = jnp.mean(dvc * dvc, axis=-1, keepdims=True)
            rstd = lax.rsqrt(var + LN_EPS)
            vhat = dvc * rstd
            vn = (vhat * g + beta).astype(BF16)
            sz, sz_grad = _silu(z)
            t1 = dya * sz
            dmixed = t1 * gu
            dmixed_b = dmixed.astype(BF16)
            mixed = _dot(wm, vn) + bs
            gws = gws + _dot_nt(dmixed_b, vn)
            gbs = gbs + dmixed
            dvn = _dot_tn(wm, dmixed_b)
            dp_ref[0, rows, :] = (t1 * mixed * gu_grad).astype(BF16)
            dp_ref[2, rows, :] = (dya * gu * mixed * sz_grad).astype(BF16)
            lg = lg + _fold8(dvn * vhat)
            lb = lb + _fold8(dvn)
            dvh = dvn * g
            m1 = jnp.mean(dvh, axis=-1, keepdims=True)
            m2 = jnp.mean(dvh * vhat, axis=-1, keepdims=True)
            dp_ref[1, rows, :] = (rstd * (dvh - m1 - vhat * m2) * gv_grad).astype(BF16)
            dyb = dyb_t[rows]
            xbv = sv[3, rows, :].astype(F32)
            cbv = sv[4, rows, :].astype(F32)
            bbv = sv[5, rows, :].astype(F32)
            zbv = sv[6, rows, :].astype(F32)
            conv = sv[7, rows, :].astype(F32)
            szb, szb_grad = _silu(zbv)
            dp_ref[5, rows, :] = (dyb * conv * szb).astype(BF16)
            dp_ref[6, rows, :] = (dyb * bbv * conv * szb_grad).astype(BF16)
            dconv = dyb * bbv * szb
            dbuf[r0:r0 + CHUNK, :] = dconv
            dc1 = dbuf[r0 + 1:r0 + 1 + CHUNK, :]
            dc2 = dbuf[r0 + 2:r0 + 2 + CHUNK, :]
            if ck == 0:
                carry[...] = dconv[0:SUBLANES, :]
            h = cbv * xbv
            c2 = c2 + _fold8(dconv * h)
            c1 = c1 + _fold8(dc1 * h)
            c0 = c0 + _fold8(dc2 * h)
            cb = cb + _fold8(dconv)
            dh = w2 * dconv + w1 * dc1 + w0 * dc2
            dp_ref[3, rows, :] = (dh * cbv).astype(BF16)
            dp_ref[4, rows, :] = (dh * xbv).astype(BF16)
            dp_ref[7, rows, :] = dgt[0, rows, :]
            dp_ref[8, rows, :] = dgt[1, rows, :]
        a_ws[...] = gws
        a_bs[...] = gbs
        a_lg[...], a_lb[...] = lg, lb
        a_c0[...], a_c1[...], a_c2[...], a_cb[...] = c0, c1, c2, cb

        @pl.when(i == nt - 1)
        def _():
            gws_ref[0] = jnp.where(_tril_mask(), a_ws[...], 0.0)
            gbs_ref[0] = jnp.sum(a_bs[...], axis=1, keepdims=True)
            glg_ref[...] = jnp.sum(a_lg[...], axis=0, keepdims=True)
            glb_ref[...] = jnp.sum(a_lb[...], axis=0, keepdims=True)
            gcw_ref[0:1, :] = jnp.sum(a_c0[...], axis=0, keepdims=True)
            gcw_ref[1:2, :] = jnp.sum(a_c1[...], axis=0, keepdims=True)
            gcw_ref[2:3, :] = jnp.sum(a_c2[...], axis=0, keepdims=True)
            gcb_ref[...] = jnp.sum(a_cb[...], axis=0, keepdims=True)

    def rev(i):
        return nt - 1 - i

    row = pl.BlockSpec((tm, d), lambda g, i: (rev(i), 0))
    wrow = pl.BlockSpec((hd, d), lambda g, i: (g, 0))
    vec = pl.BlockSpec((1, hd), lambda g, i: (0, g))
    acc8 = pltpu.VMEM((SUBLANES, hd), F32)
    return pl.pallas_call(
        body, name="mixer_bwd", grid=(HEADS, nt),
        in_specs=[row, row, wrow, wrow,
                  pl.BlockSpec((N_SAVE, tm, hd), lambda g, i: (0, rev(i), g)),
                  pl.BlockSpec((2, tm, hd), lambda g, i: (0, rev(i), g)),
                  vec, vec,
                  pl.BlockSpec((1, CHUNK, CHUNK), lambda g, i: (g, 0, 0)),
                  pl.BlockSpec((1, CHUNK, 1), lambda g, i: (g, 0, 0)),
                  pl.BlockSpec((3, hd), lambda g, i: (0, g))],
        out_specs=[pl.BlockSpec((N_IN, tm, hd), lambda g, i: (0, rev(i), g)),
                   pl.BlockSpec((1, CHUNK, CHUNK), lambda g, i: (g, 0, 0)),
                   pl.BlockSpec((1, CHUNK, 1), lambda g, i: (g, 0, 0)),
                   vec, vec,
                   pl.BlockSpec((3, hd), lambda g, i: (0, g)),
                   vec],
        out_shape=[jax.ShapeDtypeStruct((N_IN, t_len, d), BF16),
                   jax.ShapeDtypeStruct((HEADS, CHUNK, CHUNK), F32),
                   jax.ShapeDtypeStruct((HEADS, CHUNK, 1), F32),
                   jax.ShapeDtypeStruct((1, d), F32), jax.ShapeDtypeStruct((1, d), F32),
                   jax.ShapeDtypeStruct((3, d), F32), jax.ShapeDtypeStruct((1, d), F32)],
        scratch_shapes=[pltpu.VMEM((tm + SUBLANES, hd), F32), acc8,
                        pltpu.VMEM((CHUNK, CHUNK), F32), pltpu.VMEM((CHUNK, hd), F32),
                        acc8, acc8, acc8, acc8, acc8, acc8],
        compiler_params=_cparams(("parallel", "arbitrary")),
    )(da, db, w_oa, w_ob, saved, dgate, ln_v_g, ln_v_b, w_s, b_s3, conv_w)


def _grad_w_in(xb, dp):
    t_len, d = xb.shape
    tm, tn, tk = _tile(d, MM_TILE), _tile(d, MM_TILE), _tile(t_len, MM_DEPTH_TOKENS)
    nj = d // tn
    nk = t_len // tk

    def body(a_ref, b_ref, o_ref, acc):
        k = pl.program_id(2)

        @pl.when(k == 0)
        def _():
            acc[...] = jnp.zeros_like(acc)

        acc[...] += _dot_tn(a_ref[...], b_ref[...])

        @pl.when(k == nk - 1)
        def _():
            o_ref[...] = acc[...].astype(BF16)

    return pl.pallas_call(
        body, name="grad_w_in", grid=(d // tm, N_IN * nj, nk),
        in_specs=[pl.BlockSpec((tk, tm), lambda i, j, k: (k, i)),
                  pl.BlockSpec((None, tk, tn), lambda i, j, k: (j // nj, k, j % nj))],
        out_specs=pl.BlockSpec((tm, tn), lambda i, j, k: (i, j)),
        out_shape=jax.ShapeDtypeStruct((d, N_IN * d), BF16),
        scratch_shapes=[pltpu.VMEM((tm, tn), F32)],
        compiler_params=_cparams(("parallel", "parallel", "arbitrary")),
    )(xb, dp)


def _grad_x_tail(dp, w_in, partial, slots):
    _, t_len, d = dp.shape
    lo, hi = slots
    tm, tn, tk = _tile(t_len, MM_TILE), _tile(d, MM_TILE), _tile(d, MM_DEPTH)
    nkb = d // tk
    nk = (hi - lo) * nkb

    def body(a_ref, b_ref, r_ref, o_ref, acc):
        k = pl.program_id(2)

        @pl.when(k == 0)
        def _():
            acc[...] = r_ref[...]

        acc[...] += _dot(a_ref[...], b_ref[...])

        @pl.when(k == nk - 1)
        def _():
            o_ref[...] = acc[...]

    return pl.pallas_call(
        body, name="grad_x_tail", grid=(t_len // tm, d // tn, nk),
        in_specs=[pl.BlockSpec((None, tm, tk), lambda i, j, k: (lo + k // nkb, i, k % nkb)),
                  pl.BlockSpec((tk, tn), lambda i, j, k: (lo * nkb + k, j)),
                  pl.BlockSpec((tm, tn), lambda i, j, k: (i, j))],
        out_specs=pl.BlockSpec((tm, tn), lambda i, j, k: (i, j)),
        out_shape=jax.ShapeDtypeStruct((t_len, d), F32),
        scratch_shapes=[pltpu.VMEM((tm, tn), F32)],
        compiler_params=_cparams(("parallel", "parallel", "arbitrary")),
    )(dp, w_in, partial)


def _grad_x(dp, w_in, gx_direct, parts, packed, slots):
    _, t_len, d = dp.shape
    tm, tn, tk = _tile(t_len, MM_TILE), _tile(d, MM_TILE), _tile(d, MM_DEPTH)
    nkb = d // tk
    lo, hi = slots
    nk = (hi - lo) * nkb
    ni, nj = t_len // tm, d // tn
    n_parts = len(parts)

    def body(a_ref, b_ref, r_ref, *rest):
        srcs = rest[:n_parts]
        pk_ref = rest[n_parts]
        o_ref = rest[n_parts + 1]
        lands = rest[n_parts + 2:2 * n_parts + 2]
        gath = rest[2 * n_parts + 2]
        acc, send_sems, recv_sems, pk_send, pk_recv, pk_local = rest[2 * n_parts + 3:]
        i, j, k = pl.program_id(0), pl.program_id(1), pl.program_id(2)
        x, y, c = lax.axis_index("x"), lax.axis_index("y"), lax.axis_index("c")
        my_chip = 2 * x + y
        me = 4 * x + 2 * y + c
        chips = [(1 - x, y), (x, 1 - y), (1 - x, 1 - y)]

        def part_copy(t, n):
            px, py = chips[n]
            return pltpu.make_async_remote_copy(
                src_ref=srcs[t].at[2 * px + py], dst_ref=lands[t].at[my_chip],
                send_sem=send_sems.at[3 * t + n], recv_sem=recv_sems.at[3 * t + n],
                device_id=(px, py, c), device_id_type=MESH)

        def part_landing(t, n):
            px, py = chips[n]
            return pltpu.make_async_remote_copy(
                src_ref=srcs[t].at[my_chip], dst_ref=lands[t].at[2 * px + py],
                send_sem=send_sems.at[3 * t + n], recv_sem=recv_sems.at[3 * t + n],
                device_id=(px, py, c), device_id_type=MESH)

        def pk_copy(s):
            return pltpu.make_async_remote_copy(
                src_ref=pk_ref, dst_ref=gath.at[me], send_sem=pk_send, recv_sem=pk_recv.at[me],
                device_id=(s // 4, (s // 2) % 2, s % 2), device_id_type=MESH)

        @pl.when((i == 0) & (j == 0) & (k == 0))
        def _():
            for t in range(n_parts):
                for n in range(3):
                    part_copy(t, n).start()
            pltpu.make_async_copy(pk_ref, gath.at[me], pk_local).start()
            for s in range(N_DEV):
                @pl.when(s != me)
                def _(s=s):
                    pk_copy(s).start()

        @pl.when(k == 0)
        def _():
            acc[...] = r_ref[...]

        acc[...] += _dot(a_ref[...], b_ref[...])

        @pl.when(k == nk - 1)
        def _():
            o_ref[...] = acc[...]

        @pl.when((i == ni - 1) & (j == nj - 1) & (k == nk - 1))
        def _():
            for t in range(n_parts):
                for n in range(3):
                    part_landing(t, n).wait_recv()
            for t in range(n_parts):
                for n in range(3):
                    part_copy(t, n).wait_send()
            for s in range(N_DEV):
                @pl.when(s != me)
                def _(s=s):
                    pltpu.make_async_remote_copy(
                        src_ref=pk_ref, dst_ref=gath.at[s], send_sem=pk_send, recv_sem=pk_recv.at[s],
                        device_id=(s // 4, (s // 2) % 2, s % 2), device_id_type=MESH).wait_recv()
            seven = gath.at[pl.ds(0, N_DEV - 1)]
            pltpu.make_async_remote_copy(src_ref=seven, dst_ref=seven, send_sem=pk_send, recv_sem=pk_send,
                                         device_id=(x, y, 1 - c), device_id_type=MESH).wait_send()
            pltpu.make_async_copy(pk_ref, gath.at[me], pk_local).wait()

    hbm = pl.BlockSpec(memory_space=pl.ANY)
    outs = pl.pallas_call(
        body, name="grad_x", grid=(ni, nj, nk),
        in_specs=[pl.BlockSpec((None, tm, tk), lambda i, j, k: (lo + k // nkb, i, k % nkb)),
                  pl.BlockSpec((tk, tn), lambda i, j, k: (lo * nkb + k, j)),
                  pl.BlockSpec((tm, tn), lambda i, j, k: (i, j))] + [hbm] * (n_parts + 1),
        out_specs=[pl.BlockSpec((tm, tn), lambda i, j, k: (i, j))] + [hbm] * (n_parts + 1),
        out_shape=[jax.ShapeDtypeStruct((t_len, d), F32)] + [jax.ShapeDtypeStruct(p.shape, p.dtype) for p in parts]
        + [jax.ShapeDtypeStruct((N_DEV,) + packed.shape, packed.dtype)],
        scratch_shapes=[pltpu.VMEM((tm, tn), F32),
                        pltpu.SemaphoreType.DMA((3 * n_parts,)), pltpu.SemaphoreType.DMA((3 * n_parts,)),
                        pltpu.SemaphoreType.DMA(()), pltpu.SemaphoreType.DMA((N_DEV,)), pltpu.SemaphoreType.DMA(())],
        compiler_params=_cparams(("arbitrary", "arbitrary", "arbitrary")),
    )(dp, w_in, gx_direct, *parts, packed)
    return outs[0], list(outs[1:1 + n_parts]), outs[1 + n_parts]


def kernel(x, w_in, b_gate, ln_v_g, ln_v_b, w_s, b_s, conv_w, conv_b, w_oa, w_ob, w_out, ln_g, ln_b, loss_target, m_w_in, m_b_gate, m_ln_v_g, m_ln_v_b, m_w_s, m_b_s, m_conv_w, m_conv_b, m_w_oa, m_w_ob, m_w_out, m_ln_g, m_ln_b, v_w_in, v_b_gate, v_ln_v_g, v_ln_v_b, v_w_s, v_b_s, v_conv_w, v_conv_b, v_w_oa, v_w_ob, v_w_out, v_ln_g, v_ln_b):
    _, t_len, d = x.shape
    assert d % (HEADS * 128) == 0 and t_len % CHUNK == 0 and w_in.shape[2] * N_DEV == N_IN * d
    x2 = x[0]
    tgt2 = loss_target[0]
    c_arr = lax.axis_index("c").astype(jnp.int32).reshape(1)
    chip_arr = (2 * lax.axis_index("x") + lax.axis_index("y")).astype(jnp.int32).reshape(1)
    dev = 4 * lax.axis_index("x") + 2 * lax.axis_index("y") + lax.axis_index("c")

    shards = [_cast_transposed(w_in[0], "cast_w_in"), _cast_bf16(w_oa[0], "cast_w_oa"),
              _cast_bf16(w_ob[0], "cast_w_ob"), _cast_bf16(w_out[0], "cast_w_out")]
    (conv_w_g,) = _all_gather([conv_w[0]], ["lead"], "gather_conv_w", vmem=True)
    conv_w_f = jnp.transpose(conv_w_g, (1, 0, 2)).reshape(3, d)
    w_s3 = w_s[0]
    b_s3 = b_s[0].reshape(HEADS, CHUNK, 1)

    saved, ya, yb, w_in_f, w_oa_f, w_ob_f, w_out_f = _mixer_fwd(
        x2, shards, ln_v_g, ln_v_b, w_s3, b_s3, conv_w_f, conv_b)
    sa, sb, pa, pb, merged = _merge_fwd(x2, ya, yb, w_in_f, w_oa_f, w_ob_f, b_gate)
    dz, gx_direct, xb, da, db, dgate, g_ln_g, g_ln_b, err2, g_bga, g_bgb = _out_ln_loss(
        merged, w_out_f, x2, tgt2, ln_g, ln_b, sa, sb, pa, pb)
    loss = lax.psum(0.5 * jnp.sum(err2) / d, ("x", "y", "c"))

    dp, g_ws, g_bs, g_lvg, g_lvb, g_cw, g_cb = _mixer_bwd(
        da, db, w_oa_f, w_ob_f, saved, dgate, ln_v_g, ln_v_b, w_s3, b_s3, conv_w_f)
    gw_in = _grad_w_in(xb, dp)

    gw_out, land_in = _grad_w(merged, dz, "grad_w_out", exchange=gw_in)
    part_in = _pair_sum(gw_in, land_in, "col", c_arr, "grad_pair_sum_0")
    gw_oa = _grad_w(ya, da, "grad_w_oa")
    gw_ob = _grad_w(yb, db, "grad_w_ob")
    rows = [gw_oa, gw_ob, gw_out]
    lands = _pair_exchange(rows, ["row"] * 3, "grad_pair_exchange")
    parts = [part_in] + [_pair_sum(g, l, "row", c_arr, "grad_pair_sum_%d" % (n + 1))
                         for n, (g, l) in enumerate(zip(rows, lands))]
    pieces = [jnp.concatenate([g_bga, g_bgb], axis=1), g_lvg, g_lvb, g_ws, g_bs, g_cw, g_cb, g_ln_g, g_ln_b]
    sizes = [p.size for p in pieces]
    packed = jnp.concatenate([p.reshape(-1, 128) for p in pieces], axis=0)
    gx_part, lands2, gathered = _grad_x(dp, w_in_f, gx_direct, parts, packed, (0, GRAD_X_COMM_BLOCKS))
    grad_x = _grad_x_tail(dp, w_in_f, gx_part, (GRAD_X_COMM_BLOCKS, N_IN))[None]

    big = []
    for n, (w, m, v) in enumerate([(w_in, m_w_in, v_w_in), (w_oa, m_w_oa, v_w_oa), (w_ob, m_w_ob, v_w_ob),
                                   (w_out, m_w_out, v_w_out)]):
        big.append([o[None] for o in _sum_adam(parts[n], lands2[n], w[0], m[0], v[0], chip_arr, "sum_adam_%d" % n)])
    (g_w_in, d_w_in, nm_w_in, nv_w_in), (g_w_oa, d_w_oa, nm_w_oa, nv_w_oa), \
        (g_w_ob, d_w_ob, nm_w_ob, nv_w_ob), (g_w_out, d_w_out, nm_w_out, nv_w_out) = big

    total = _small_sum(gathered, "sum_small_grads")
    offs = [0]
    for s in sizes:
        offs.append(offs[-1] + s // 128)
    unpacked = [total[offs[n]:offs[n + 1]] for n in range(len(pieces))]
    g_b_gate = unpacked[0].reshape(b_gate.shape)
    g_ln_v_g = unpacked[1].reshape(ln_v_g.shape)
    g_ln_v_b = unpacked[2].reshape(ln_v_b.shape)
    g_w_s = unpacked[3].reshape(w_s.shape)
    g_b_s = unpacked[4].reshape(b_s.shape)
    g_conv_w = lax.dynamic_slice_in_dim(unpacked[5].reshape(3, d), dev * (d // N_DEV), d // N_DEV, axis=1)[None]
    g_conv_b = unpacked[6].reshape(conv_b.shape)
    g_ln_g2 = unpacked[7].reshape(ln_g.shape)
    g_ln_b2 = unpacked[8].reshape(ln_b.shape)

    small_w = [b_gate, ln_v_g, ln_v_b, w_s, b_s, conv_w, conv_b, ln_g, ln_b]
    small_g = [g_b_gate, g_ln_v_g, g_ln_v_b, g_w_s, g_b_s, g_conv_w, g_conv_b, g_ln_g2, g_ln_b2]
    small_m = [m_b_gate, m_ln_v_g, m_ln_v_b, m_w_s, m_b_s, m_conv_w, m_conv_b, m_ln_g, m_ln_b]
    small_v = [v_b_gate, v_ln_v_g, v_ln_v_b, v_w_s, v_b_s, v_conv_w, v_conv_b, v_ln_g, v_ln_b]

    def flat(a):
        return a.reshape(-1, a.shape[-1])

    res = _small_adam([flat(a) for a in small_w], [flat(a) for a in small_g], [flat(a) for a in small_m],
                      [flat(a) for a in small_v], "adam_small")
    ns = len(small_w)
    d_s = [res[n].reshape(small_w[n].shape) for n in range(ns)]
    nm_s = [res[ns + n].reshape(small_w[n].shape) for n in range(ns)]
    nv_s = [res[2 * ns + n].reshape(small_w[n].shape) for n in range(ns)]

    def ordered(first, small, oa, ob, out):
        return [first] + small[:7] + [oa, ob, out] + small[7:]

    return (loss, grad_x,
            *ordered(g_w_in, small_g, g_w_oa, g_w_ob, g_w_out),
            *ordered(d_w_in, d_s, d_w_oa, d_w_ob, d_w_out),
            *ordered(nm_w_in, nm_s, nm_w_oa, nm_w_ob, nm_w_out),
            *ordered(nv_w_in, nv_s, nv_w_oa, nv_w_ob, nv_w_out))
```

```python
import functools

import jax
import jax.numpy as jnp
from jax import lax
from jax.experimental import pallas as pl
from jax.experimental.pallas import tpu as pltpu

F32 = jnp.float32
BF16 = jnp.bfloat16
MESH = pl.DeviceIdType.MESH

N_DEV = 8
N_CHIP = 4
HEADS = 8
CHUNK = 128
N_IN = 9
N_MIX = 7
N_SAVE = 8
LN_EPS = 1e-5
DN_ALPHA = 2.0 ** 0.25
ADAM_LR = 0.001
ADAM_B1 = 0.9
ADAM_B2 = 0.999
ADAM_EPS = 1e-08
ADAM_WD = 0.01
ADAM_STEP = 10
GELU_C0 = 0.7978845608028654
GELU_C1 = 0.044715
SUBLANES = 8
FIRST_SENDS = 3
SENDS_PER_GROUP = 2
MM_TILE = 1024
MM_DEPTH = 2048
MM_DEPTH_TOKENS = 4096
GRAD_X_COMM_BLOCKS = 8
VMEM_LIMIT = 56 << 20


def _cparams(sem):
    return pltpu.CompilerParams(dimension_semantics=sem, vmem_limit_bytes=VMEM_LIMIT)


def _tile(n, want):
    t = min(n, want)
    while n % t:
        t //= 2
    return t


def _gelu(u):
    u2 = u * u
    t = jnp.tanh(u * (GELU_C0 + (GELU_C0 * GELU_C1) * u2))
    hp = 0.5 * t + 0.5
    grad = hp + (0.5 * u) * (1.0 - t * t) * (GELU_C0 + (3.0 * GELU_C0 * GELU_C1) * u2)
    return u * hp, grad


def _silu(z):
    s = jax.nn.sigmoid(z)
    sil = z * s
    return sil, s + sil * (1.0 - s)


def _fold8(a):
    return jnp.sum(a.reshape(a.shape[0] // SUBLANES, SUBLANES, a.shape[1]), axis=0)


def _dot(a, b):
    return jnp.dot(a, b, preferred_element_type=F32)


def _dot_nt(a, b):
    return lax.dot_general(a, b, (((1,), (1,)), ((), ())), preferred_element_type=F32)


def _dot_tn(a, b):
    return lax.dot_general(a, b, (((0,), (0,)), ((), ())), preferred_element_type=F32)


def _tril_mask():
    r = lax.broadcasted_iota(jnp.int32, (CHUNK, CHUNK), 0)
    c = lax.broadcasted_iota(jnp.int32, (CHUNK, CHUNK), 1)
    return c <= r


def _cast_bf16(a, name):
    rows, cols = a.shape
    rb = _tile(rows, 256)

    def body(a_ref, o_ref):
        o_ref[...] = a_ref[...].astype(BF16)

    return pl.pallas_call(
        body, name=name, grid=(rows // rb,),
        in_specs=[pl.BlockSpec((rb, cols), lambda i: (i, 0))],
        out_specs=pl.BlockSpec((rb, cols), lambda i: (i, 0)),
        out_shape=jax.ShapeDtypeStruct((rows, cols), BF16),
        compiler_params=_cparams(("parallel",)),
    )(a)


def _cast_transposed(a, name):
    rows, cols = a.shape
    tb = _tile(rows, 2048)
    tc = _tile(cols, 256)

    def body(a_ref, o_ref):
        o_ref[...] = a_ref[...].T.astype(BF16)

    return pl.pallas_call(
        body, name=name, grid=(cols // tc, rows // tb),
        in_specs=[pl.BlockSpec((tb, tc), lambda j, i: (i, j))],
        out_specs=pl.BlockSpec((tc, tb), lambda j, i: (j, i)),
        out_shape=jax.ShapeDtypeStruct((cols, rows), BF16),
        compiler_params=_cparams(("parallel", "parallel")),
    )(a)


def _shard_ref(full, kind, s, n):
    if kind == "col":
        return full.at[:, pl.ds(pl.multiple_of(s * n, 128), n)]
    if kind == "row":
        return full.at[pl.ds(pl.multiple_of(s * n, SUBLANES), n), :]
    return full.at[s]


def _all_gather(shards, kinds, name, vmem):
    nt = len(shards)
    out_shapes = []
    for a, kind in zip(shards, kinds):
        if kind == "col":
            out_shapes.append(jax.ShapeDtypeStruct((a.shape[0], N_DEV * a.shape[1]), a.dtype))
        elif kind == "row":
            out_shapes.append(jax.ShapeDtypeStruct((N_DEV * a.shape[0], a.shape[1]), a.dtype))
        else:
            out_shapes.append(jax.ShapeDtypeStruct((N_DEV,) + a.shape, a.dtype))

    def body(*refs):
        srcs, fulls = refs[:nt], refs[nt:2 * nt]
        send_sems, recv_sems, local_sems = refs[2 * nt:]
        x, y, c = lax.axis_index("x"), lax.axis_index("y"), lax.axis_index("c")
        sibling = (x, y, 1 - c)
        chips = [(1 - x, y), (x, 1 - y), (1 - x, 1 - y)]

        def dev(px, py, pc):
            return 4 * px + 2 * py + pc

        def region(t, s):
            a, kind = shards[t], kinds[t]
            n = a.shape[1] if kind == "col" else a.shape[0]
            return _shard_ref(fulls[t], kind, s, n)

        def copy(t, k, block, to, own=False):
            return pltpu.make_async_remote_copy(
                src_ref=srcs[t] if own else region(t, block), dst_ref=region(t, block),
                send_sem=send_sems.at[7 * t + k], recv_sem=recv_sems.at[7 * t + k],
                device_id=to, device_id_type=MESH)

        me = dev(x, y, c)
        started = []
        for t in range(nt):
            mine = pltpu.make_async_copy(srcs[t], region(t, me), local_sems.at[t])
            mine.start()
            started.append(mine)
        first = []
        for t in range(nt):
            first.append(copy(t, 0, me, sibling, own=True))
            for j, chip in enumerate(chips):
                first.append(copy(t, 1 + j, me, (*chip, c), own=True))
        for cp in first:
            cp.start()
        passed = []
        for t in range(nt):
            for j, chip in enumerate(chips):
                blk = dev(*chip, c)
                copy(t, 1 + j, blk, sibling).wait_recv()
                fwd = copy(t, 4 + j, blk, sibling)
                fwd.start()
                passed.append(fwd)
        for t in range(nt):
            copy(t, 0, dev(x, y, 1 - c), sibling).wait_recv()
            for j, chip in enumerate(chips):
                copy(t, 4 + j, dev(*chip, 1 - c), sibling).wait_recv()
        for cp in first + passed:
            cp.wait_send()
        for mine in started:
            mine.wait()

    space = pltpu.VMEM if vmem else pl.ANY
    return pl.pallas_call(
        body, name=name,
        in_specs=[pl.BlockSpec(memory_space=space)] * nt,
        out_specs=[pl.BlockSpec(memory_space=space)] * nt,
        out_shape=out_shapes,
        scratch_shapes=[pltpu.SemaphoreType.DMA((7 * nt,)), pltpu.SemaphoreType.DMA((7 * nt,)),
                        pltpu.SemaphoreType.DMA((nt,))],
        compiler_params=pltpu.CompilerParams(vmem_limit_bytes=VMEM_LIMIT, has_side_effects=True),
    )(*shards)


def _pair_exchange(grads, kinds, name):
    nt = len(grads)
    shard_shapes = []
    for g, kind in zip(grads, kinds):
        shard_shapes.append((g.shape[0], g.shape[1] // N_DEV) if kind == "col" else (g.shape[0] // N_DEV, g.shape[1]))

    def body(*refs):
        srcs, lands = refs[:nt], refs[nt:2 * nt]
        send_sems, recv_sems = refs[2 * nt:]
        x, y, c = lax.axis_index("x"), lax.axis_index("y"), lax.axis_index("c")
        copies = []
        for t in range(nt):
            n = shard_shapes[t][1] if kinds[t] == "col" else shard_shapes[t][0]
            for k in range(N_CHIP):
                cp = pltpu.make_async_remote_copy(
                    src_ref=_shard_ref(srcs[t], kinds[t], 2 * k + 1 - c, n), dst_ref=lands[t].at[k],
                    send_sem=send_sems.at[N_CHIP * t + k], recv_sem=recv_sems.at[N_CHIP * t + k],
                    device_id=(x, y, 1 - c), device_id_type=MESH)
                cp.start()
                copies.append(cp)
        for cp in copies:
            cp.wait()

    return pl.pallas_call(
        body, name=name,
        in_specs=[pl.BlockSpec(memory_space=pl.ANY)] * nt,
        out_specs=[pl.BlockSpec(memory_space=pl.ANY)] * nt,
        out_shape=[jax.ShapeDtypeStruct((N_CHIP,) + s, g.dtype) for s, g in zip(shard_shapes, grads)],
        scratch_shapes=[pltpu.SemaphoreType.DMA((N_CHIP * nt,)), pltpu.SemaphoreType.DMA((N_CHIP * nt,))],
        compiler_params=pltpu.CompilerParams(has_side_effects=True),
    )(*grads)


def _pair_sum(grad, land, kind, c_arr, name):
    _, r, w = land.shape
    rb = _tile(r, 256)
    nrb = r // rb

    def body(c_ref, g_ref, l_ref, o_ref):
        o_ref[...] = (g_ref[...].astype(F32) + l_ref[...].astype(F32)).astype(o_ref.dtype)

    if kind == "col":
        g_spec = pl.BlockSpec((rb, w), lambda k, i, c: (i, 2 * k + c[0]))
    else:
        g_spec = pl.BlockSpec((rb, w), lambda k, i, c: ((2 * k + c[0]) * nrb + i, 0))
    return pl.pallas_call(
        body, name=name,
        grid_spec=pltpu.PrefetchScalarGridSpec(
            num_scalar_prefetch=1, grid=(N_CHIP, nrb),
            in_specs=[g_spec, pl.BlockSpec((None, rb, w), lambda k, i, c: (k, i, 0))],
            out_specs=pl.BlockSpec((None, rb, w), lambda k, i, c: (k, i, 0))),
        out_shape=jax.ShapeDtypeStruct(land.shape, BF16),
        compiler_params=_cparams(("parallel", "parallel")),
    )(c_arr, grad, land)


def _adam(w, g, m, v):
    m = ADAM_B1 * m + (1.0 - ADAM_B1) * g
    v = ADAM_B2 * v + (1.0 - ADAM_B2) * jnp.square(g)
    m_hat = m / (1.0 - ADAM_B1 ** ADAM_STEP)
    v_hat = v / (1.0 - ADAM_B2 ** ADAM_STEP)
    delta = -ADAM_LR * (m_hat / (jnp.sqrt(v_hat) + ADAM_EPS) + ADAM_WD * w)
    return delta, m, v


def _sum_adam(part, land, w, m, v, chip_arr, name):
    r, wd = w.shape
    rb = _tile(r, 128)

    def body(k_ref, own, r1, r2, r3, w_ref, m_ref, v_ref, g_out, d_out, m_out, v_out):
        g = own[...].astype(F32) + r1[...].astype(F32) + r2[...].astype(F32) + r3[...].astype(F32)
        d, mn, vn = _adam(w_ref[...], g, m_ref[...], v_ref[...])
        g_out[...] = g
        d_out[...] = d
        m_out[...] = mn
        v_out[...] = vn

    def slot(off):
        return pl.BlockSpec((None, rb, wd), lambda i, k: ((k[0] + off) % N_CHIP, i, 0))

    plain = pl.BlockSpec((rb, wd), lambda i, k: (i, 0))
    return pl.pallas_call(
        body, name=name,
        grid_spec=pltpu.PrefetchScalarGridSpec(
            num_scalar_prefetch=1, grid=(r // rb,),
            in_specs=[slot(0), slot(1), slot(2), slot(3), plain, plain, plain],
            out_specs=[plain] * 4),
        out_shape=[jax.ShapeDtypeStruct(w.shape, F32)] * 4,
        compiler_params=_cparams(("parallel",)),
    )(chip_arr, part, land, land, land, w, m, v)


def _small_sum(gathered, name):
    _, r, w = gathered.shape

    def body(g_ref, o_ref):
        acc = g_ref[0]
        for d in range(1, N_DEV):
            acc = acc + g_ref[d]
        o_ref[...] = acc

    return pl.pallas_call(body, name=name, out_shape=jax.ShapeDtypeStruct((r, w), F32))(gathered)


def _small_adam(ws, gs, ms, vs, name):
    n = len(ws)

    def body(*refs):
        ins, outs = refs[:4 * n], refs[4 * n:]
        for t in range(n):
            d, mn, vn = _adam(ins[t][...], ins[n + t][...], ins[2 * n + t][...], ins[3 * n + t][...])
            outs[t][...] = d
            outs[n + t][...] = mn
            outs[2 * n + t][...] = vn

    shapes = [jax.ShapeDtypeStruct(w.shape, F32) for w in ws]
    return pl.pallas_call(body, name=name, out_shape=shapes * 3)(*ws, *gs, *ms, *vs)


def _mixer_fwd(x, shards, ln_v_g, ln_v_b, w_s, b_s3, conv_w, conv_b):
    t_len, d = x.shape
    hd = d // HEADS
    tm = _tile(t_len, 512)
    nt = t_len // tm
    assert nt >= 2
    w8 = shards[0].shape[0]
    bps = w8 // hd
    r8 = shards[1].shape[0]
    nq = N_IN * HEADS
    n_blocks = nq + 3 * N_DEV
    groups = [[("in", HEADS * b + HEADS - 1 - p) for b in range(N_MIX)] for p in range(HEADS)]
    groups.append([("in", q) for q in range(N_MIX * HEADS, nq)] + [(t, s) for t in (1, 2, 3) for s in range(N_DEV)])

    def owner_of(blk):
        return blk[1] // bps if blk[0] == "in" else blk[1]

    send_step = {}
    for s in range(N_DEV):
        mine = [blk for grp in groups for blk in grp if owner_of(blk) == s]
        for pos, blk in enumerate(mine):
            send_step[blk] = 0 if pos < FIRST_SENDS else 1 + (pos - FIRST_SENDS) // SENDS_PER_GROUP
    assert max(send_step.values()) < HEADS
    assert all(send_step[blk] <= max(k - 1, 0) for k in range(HEADS) for blk in groups[k])
    tail_pass_step = nt - 6 if nt >= 8 else nt - 2

    def body(x_ref, sh_in, sh_oa, sh_ob, sh_out, lng, lnb, ws_ref, bs_ref, cw_ref, cb_ref,
             save_ref, ya_ref, yb_ref, f_in, f_oa, f_ob, f_out,
             hbuf, wbuf, recv_sems, own_sems, fwd_sems, local_sems, load_sems):
        g = pl.program_id(0)
        i = pl.program_id(1)
        x, y, c = lax.axis_index("x"), lax.axis_index("y"), lax.axis_index("c")
        sibling = (x, y, 1 - c)
        chips = [(1 - x, y), (x, 1 - y), (1 - x, 1 - y)]
        shard_refs = (sh_in, sh_oa, sh_ob, sh_out)
        fulls = (f_in, f_oa, f_ob, f_out)

        def tensor(blk):
            return 0 if blk[0] == "in" else blk[0]

        def owner(blk):
            s = owner_of(blk)
            return s // 4, (s // 2) % 2, s % 2

        def bid(blk):
            return blk[1] if blk[0] == "in" else nq + (blk[0] - 1) * N_DEV + blk[1]

        def region(blk):
            if blk[0] == "in":
                return f_in.at[pl.ds(blk[1] * hd, hd), :]
            return fulls[blk[0]].at[pl.ds(blk[1] * r8, r8), :]

        def own_src(blk):
            if blk[0] == "in":
                return sh_in.at[pl.ds((blk[1] % bps) * hd, hd), :]
            return shard_refs[blk[0]]

        def rcopy(blk, to, send_sem, own):
            return pltpu.make_async_remote_copy(
                src_ref=own_src(blk) if own else region(blk), dst_ref=region(blk),
                send_sem=send_sem, recv_sem=recv_sems.at[bid(blk)], device_id=to, device_id_type=MESH)

        def send_own(blk):
            ox, oy, oc = owner(blk)

            @pl.when((x == ox) & (y == oy) & (c == oc))
            def _():
                for chip in reversed(chips):
                    rcopy(blk, (*chip, c), own_sems.at[tensor(blk)], True).start()
                rcopy(blk, sibling, own_sems.at[tensor(blk)], True).start()

        def pass_on(blk):
            ox, oy, oc = owner(blk)

            @pl.when(((x != ox) | (y != oy)) & (c == oc))
            def _():
                rcopy(blk, sibling, fwd_sems.at[tensor(blk)], False).wait_recv()
                rcopy(blk, sibling, fwd_sems.at[tensor(blk)], False).start()

        def wait_from_sibling(blk):
            @pl.when(c != owner(blk)[2])
            def _():
                rcopy(blk, sibling, fwd_sems.at[tensor(blk)], False).wait_recv()

        def local_copy(t):
            me = 4 * x + 2 * y + c
            return pltpu.make_async_copy(shard_refs[t], _shard_ref(fulls[t], "row", me, w8 if t == 0 else r8),
                                         local_sems.at[t])

        first = (g == 0) & (i == 0)

        @pl.when(first)
        def _():
            for t in range(4):
                local_copy(t).start()
            for grp in groups:
                for blk in grp:
                    if send_step[blk] == 0:
                        send_own(blk)
            for blk in groups[0]:
                pass_on(blk)

        @pl.when(i == 0)
        def _():
            for k in range(1, HEADS):
                @pl.when(g == k)
                def _(k=k):
                    for grp in groups:
                        for blk in grp:
                            if send_step[blk] == k:
                                send_own(blk)

        @pl.when(i == nt - 2)
        def _():
            for k in range(HEADS - 1):
                @pl.when(g == k)
                def _(k=k):
                    for blk in groups[k + 1]:
                        pass_on(blk)

        @pl.when((g == HEADS - 1) & (i == tail_pass_step))
        def _():
            for blk in groups[HEADS]:
                pass_on(blk)

        def fetch_weights(k):
            for blk in groups[k]:
                wait_from_sibling(blk)
            for b, blk in enumerate(groups[k]):
                ox, oy, oc = owner(blk)
                mine = (x == ox) & (y == oy) & (c == oc)

                @pl.when(mine)
                def _(b=b, blk=blk):
                    pltpu.make_async_copy(own_src(blk), wbuf.at[k % 2, b], load_sems.at[b]).start()

                @pl.when(jnp.logical_not(mine))
                def _(b=b, blk=blk):
                    pltpu.make_async_copy(region(blk), wbuf.at[k % 2, b], load_sems.at[b]).start()

        @pl.when(i == 0)
        def _():
            for k in range(HEADS):
                @pl.when(g == k)
                def _(k=k):
                    if k == 0:
                        fetch_weights(0)
                    for b, blk in enumerate(groups[k]):
                        pltpu.make_async_copy(region(blk), wbuf.at[k % 2, b], load_sems.at[b]).wait()

        @pl.when(i == nt - 1)
        def _():
            for k in range(1, HEADS):
                @pl.when(g == k - 1)
                def _(k=k):
                    fetch_weights(k)

        xt = x_ref[...].astype(BF16)
        wslot = g % 2
        u = _dot_nt(xt, wbuf[wslot, 0])
        v = _dot_nt(xt, wbuf[wslot, 1])
        z = _dot_nt(xt, wbuf[wslot, 2])
        save_ref[0] = u.astype(BF16)
        save_ref[1] = v.astype(BF16)
        save_ref[2] = z.astype(BF16)
        gu, _ = _gelu(u)
        gv, _ = _gelu(v)
        mu = jnp.mean(gv, axis=-1, keepdims=True)
        dv = gv - mu
        var = jnp.mean(dv * dv, axis=-1, keepdims=True)
        vn = (dv * lax.rsqrt(var + LN_EPS) * lng[...] + lnb[...]).astype(BF16)
        sz, _ = _silu(z)
        gate = gu * sz
        wm = jnp.where(_tril_mask(), ws_ref[0], 0.0).astype(BF16)
        bs = bs_ref[0]
        for ck in range(tm // CHUNK):
            rows = slice(ck * CHUNK, (ck + 1) * CHUNK)
            mixed = _dot(wm, vn[rows]) + bs
            ya_ref[rows, :] = (gate[rows] * mixed).astype(BF16)

        xbv = _dot_nt(xt, wbuf[wslot, 3])
        cbv = _dot_nt(xt, wbuf[wslot, 4])
        bbv = _dot_nt(xt, wbuf[wslot, 5])
        zbv = _dot_nt(xt, wbuf[wslot, 6])
        save_ref[3] = xbv.astype(BF16)
        save_ref[4] = cbv.astype(BF16)
        save_ref[5] = bbv.astype(BF16)
        save_ref[6] = zbv.astype(BF16)
        h = cbv * xbv

        @pl.when(i == 0)
        def _():
            hbuf[0:SUBLANES, :] = jnp.zeros((SUBLANES, hd), F32)

        hbuf[SUBLANES:SUBLANES + tm, :] = h
        h1 = hbuf[SUBLANES - 1:SUBLANES - 1 + tm, :]
        h2 = hbuf[SUBLANES - 2:SUBLANES - 2 + tm, :]
        conv = cb_ref[...] + cw_ref[0:1, :] * h2 + cw_ref[1:2, :] * h1 + cw_ref[2:3, :] * h
        hbuf[0:SUBLANES, :] = h[tm - SUBLANES:tm, :]
        save_ref[7] = conv.astype(BF16)
        szb, _ = _silu(zbv)
        yb_ref[...] = (bbv * conv * szb).astype(BF16)

        @pl.when((g == HEADS - 1) & (i == nt - 1))
        def _():
            for blk in groups[HEADS]:
                wait_from_sibling(blk)
            for t in range(4):
                local_copy(t).wait()
            for t in range(4):
                n = w8 if t == 0 else r8
                own_all, fwd_all = fulls[t].at[pl.ds(0, 4 * n), :], fulls[t].at[pl.ds(0, 3 * n), :]
                for ref, sem in ((own_all, own_sems.at[t]), (fwd_all, fwd_sems.at[t])):
                    pltpu.make_async_remote_copy(src_ref=ref, dst_ref=ref, send_sem=sem, recv_sem=sem,
                                                 device_id=sibling, device_id_type=MESH).wait_send()

    def head(g):
        return HEADS - 1 - g

    vec = pl.BlockSpec((1, hd), lambda g, i: (0, head(g)))
    hbm = pl.BlockSpec(memory_space=pl.ANY)
    return pl.pallas_call(
        body, name="mixer_fwd", grid=(HEADS, nt),
        in_specs=[pl.BlockSpec((tm, d), lambda g, i: (i, 0)), hbm, hbm, hbm, hbm,
                  vec, vec,
                  pl.BlockSpec((1, CHUNK, CHUNK), lambda g, i: (head(g), 0, 0)),
                  pl.BlockSpec((1, CHUNK, 1), lambda g, i: (head(g), 0, 0)),
                  pl.BlockSpec((3, hd), lambda g, i: (0, head(g))),
                  vec],
        out_specs=[pl.BlockSpec((N_SAVE, tm, hd), lambda g, i: (0, i, head(g))),
                   pl.BlockSpec((tm, hd), lambda g, i: (i, head(g))),
                   pl.BlockSpec((tm, hd), lambda g, i: (i, head(g))),
                   hbm, hbm, hbm, hbm],
        out_shape=[jax.ShapeDtypeStruct((N_SAVE, t_len, d), BF16),
                   jax.ShapeDtypeStruct((t_len, d), BF16),
                   jax.ShapeDtypeStruct((t_len, d), BF16),
                   jax.ShapeDtypeStruct((N_DEV * w8, d), BF16),
                   jax.ShapeDtypeStruct((d, d), BF16), jax.ShapeDtypeStruct((d, d), BF16),
                   jax.ShapeDtypeStruct((d, d), BF16)],
        scratch_shapes=[pltpu.VMEM((SUBLANES + tm, hd), F32), pltpu.VMEM((2, N_MIX, hd, d), BF16),
                        pltpu.SemaphoreType.DMA((n_blocks,)), pltpu.SemaphoreType.DMA((4,)),
                        pltpu.SemaphoreType.DMA((4,)), pltpu.SemaphoreType.DMA((4,)),
                        pltpu.SemaphoreType.DMA((N_MIX,))],
        compiler_params=_cparams(("arbitrary", "arbitrary")),
    )(x, *shards, ln_v_g, ln_v_b, w_s, b_s3, conv_w, conv_b)


def _merge_fwd(x, ya, yb, w_in, w_oa, w_ob, b_gate):
    t_len, d = x.shape
    tm = _tile(t_len, 256)
    tn = _tile(d, 1024)
    nj = d // tn

    def body(x_ref, ya_ref, yb_ref, wga, wgb, woa, wob, bga, bgb, sa_ref, sb_ref, pa_ref, pb_ref, mg_ref):
        xt = x_ref[...].astype(BF16)
        sa = jax.nn.sigmoid(_dot_nt(xt, wga[...]) + bga[...])
        sb = jax.nn.sigmoid(_dot_nt(xt, wgb[...]) + bgb[...])
        pa = _dot(ya_ref[...], woa[...])
        pb = _dot(yb_ref[...], wob[...])
        sa_ref[...] = sa.astype(BF16)
        sb_ref[...] = sb.astype(BF16)
        pa_ref[...] = pa.astype(BF16)
        pb_ref[...] = pb.astype(BF16)
        mg_ref[...] = (sa * pa + sb * pb).astype(BF16)

    row = pl.BlockSpec((tm, d), lambda j, i: (i, 0))
    out = pl.BlockSpec((tm, tn), lambda j, i: (i, j))
    return pl.pallas_call(
        body, name="merge_fwd", grid=(nj, t_len // tm),
        in_specs=[row, row, row,
                  pl.BlockSpec((tn, d), lambda j, i: (7 * nj + j, 0)),
                  pl.BlockSpec((tn, d), lambda j, i: (8 * nj + j, 0)),
                  pl.BlockSpec((d, tn), lambda j, i: (0, j)),
                  pl.BlockSpec((d, tn), lambda j, i: (0, j)),
                  pl.BlockSpec((1, tn), lambda j, i: (0, j)),
                  pl.BlockSpec((1, tn), lambda j, i: (0, nj + j))],
        out_specs=[out] * 5,
        out_shape=[jax.ShapeDtypeStruct((t_len, d), BF16)] * 5,
        compiler_params=_cparams(("parallel", "arbitrary")),
    )(x, ya, yb, w_in, w_in, w_oa, w_ob, b_gate, b_gate)


def _out_ln_loss(merged, w_out, x, target, ln_g, ln_b, sa, sb, pa, pb):
    t_len, d = x.shape
    tm = _tile(t_len, 256)
    nt = t_len // tm

    def body(mg_ref, w_ref, x_ref, t_ref, g_ref, b_ref, sa_ref, sb_ref, pa_ref, pb_ref,
             dz_ref, gx_ref, xb_ref, da_ref, db_ref, dg_ref, glg_ref, glb_ref, ls_ref, ga_ref, gb_ref,
             a_g, a_b, a_l, acc_a, acc_b):
        i = pl.program_id(0)
        xb_ref[...] = x_ref[...].astype(BF16)

        @pl.when(i == 0)
        def _():
            for a in (a_g, a_b, a_l, acc_a, acc_b):
                a[...] = jnp.zeros_like(a)

        zres = DN_ALPHA * x_ref[...] + _dot(mg_ref[...], w_ref[...])
        mu = jnp.mean(zres, axis=-1, keepdims=True)
        dc = zres - mu
        var = jnp.mean(dc * dc, axis=-1, keepdims=True)
        rstd = lax.rsqrt(var + LN_EPS)
        xhat = dc * rstd
        g = g_ref[...]
        err = xhat * g + b_ref[...] - t_ref[...]
        dy = err * (1.0 / d)
        a_l[...] += _fold8(err * err)
        a_g[...] += _fold8(dy * xhat)
        a_b[...] += _fold8(dy)
        dxh = dy * g
        m1 = jnp.mean(dxh, axis=-1, keepdims=True)
        m2 = jnp.mean(dxh * xhat, axis=-1, keepdims=True)
        dz = rstd * (dxh - m1 - xhat * m2)
        dz_bf = dz.astype(BF16)
        dz_ref[...] = dz_bf
        gx_ref[...] = DN_ALPHA * dz

        dm = _dot_nt(dz_bf, w_ref[...])
        sa = sa_ref[...].astype(F32)
        sb = sb_ref[...].astype(F32)
        da = dm * sa
        db = dm * sb
        da_ref[...] = da.astype(BF16)
        db_ref[...] = db.astype(BF16)
        dga = da * pa_ref[...].astype(F32) * (1.0 - sa)
        dgb = db * pb_ref[...].astype(F32) * (1.0 - sb)
        dg_ref[0] = dga.astype(BF16)
        dg_ref[1] = dgb.astype(BF16)
        acc_a[...] += _fold8(dga)
        acc_b[...] += _fold8(dgb)

        @pl.when(i == nt - 1)
        def _():
            glg_ref[...] = jnp.sum(a_g[...], axis=0, keepdims=True)
            glb_ref[...] = jnp.sum(a_b[...], axis=0, keepdims=True)
            ls_ref[...] = jnp.sum(a_l[...], axis=0, keepdims=True)
            ga_ref[...] = jnp.sum(acc_a[...], axis=0, keepdims=True)
            gb_ref[...] = jnp.sum(acc_b[...], axis=0, keepdims=True)

    row = pl.BlockSpec((tm, d), lambda i: (i, 0))
    vec = pl.BlockSpec((1, d), lambda i: (0, 0))
    act = jax.ShapeDtypeStruct((t_len, d), BF16)
    part = jax.ShapeDtypeStruct((1, d), F32)
    return pl.pallas_call(
        body, name="out_ln_loss", grid=(nt,),
        in_specs=[row, pl.BlockSpec((d, d), lambda i: (0, 0)), row, row, vec, vec, row, row, row, row],
        out_specs=[row, row, row, row, row, pl.BlockSpec((2, tm, d), lambda i: (0, i, 0)), vec, vec, vec, vec, vec],
        out_shape=[act, jax.ShapeDtypeStruct((t_len, d), F32), act, act, act,
                   jax.ShapeDtypeStruct((2, t_len, d), BF16), part, part, part, part, part],
        scratch_shapes=[pltpu.VMEM((SUBLANES, d), F32)] * 5,
        compiler_params=_cparams(("arbitrary",)),
    )(merged, w_out, x, target, ln_g, ln_b, sa, sb, pa, pb)


def _grad_w(a, b, name, exchange=None):
    t_len, m = a.shape
    n = b.shape[1]
    tm, tn, tk = _tile(m, MM_TILE), _tile(n, MM_TILE), _tile(t_len, MM_DEPTH_TOKENS)
    ni, nj, nk = m // tm, n // tn, t_len // tk

    def body(a_ref, b_ref, *rest):
        if exchange is None:
            o_ref, acc = rest
        else:
            src, o_ref, land, acc, send_sems, recv_sems = rest
        i, j, k = pl.program_id(0), pl.program_id(1), pl.program_id(2)

        def copies():
            x, y, c = lax.axis_index("x"), lax.axis_index("y"), lax.axis_index("c")
            w = exchange.shape[1] // N_DEV
            return [pltpu.make_async_remote_copy(
                src_ref=_shard_ref(src, "col", 2 * s + 1 - c, w), dst_ref=land.at[s],
                send_sem=send_sems.at[s], recv_sem=recv_sems.at[s],
                device_id=(x, y, 1 - c), device_id_type=MESH) for s in range(N_CHIP)]

        if exchange is not None:
            @pl.when((i == 0) & (j == 0) & (k == 0))
            def _():
                for cp in copies():
                    cp.start()

        @pl.when(k == 0)
        def _():
            acc[...] = jnp.zeros_like(acc)

        acc[...] += _dot_tn(a_ref[...], b_ref[...])

        @pl.when(k == nk - 1)
        def _():
            o_ref[...] = acc[...].astype(BF16)

        if exchange is not None:
            @pl.when((i == ni - 1) & (j == nj - 1) & (k == nk - 1))
            def _():
                for cp in copies():
                    cp.wait()

    in_specs = [pl.BlockSpec((tk, tm), lambda i, j, k: (k, i)), pl.BlockSpec((tk, tn), lambda i, j, k: (k, j))]
    out_specs = [pl.BlockSpec((tm, tn), lambda i, j, k: (i, j))]
    out_shape = [jax.ShapeDtypeStruct((m, n), BF16)]
    scratch = [pltpu.VMEM((tm, tn), F32)]
    args = [a, b]
    if exchange is not None:
        hbm = pl.BlockSpec(memory_space=pl.ANY)
        in_specs.append(hbm)
        out_specs.append(hbm)
        out_shape.append(jax.ShapeDtypeStruct((N_CHIP, exchange.shape[0], exchange.shape[1] // N_DEV), exchange.dtype))
        scratch += [pltpu.SemaphoreType.DMA((N_CHIP,)), pltpu.SemaphoreType.DMA((N_CHIP,))]
        args.append(exchange)
    outs = pl.pallas_call(
        body, name=name, grid=(ni, nj, nk),
        in_specs=in_specs, out_specs=out_specs, out_shape=out_shape, scratch_shapes=scratch,
        compiler_params=_cparams(("arbitrary", "arbitrary", "arbitrary")),
    )(*args)
    return outs[0] if exchange is None else (outs[0], outs[1])


def _mixer_bwd(da, db, w_oa, w_ob, saved, dgate, ln_v_g, ln_v_b, w_s, b_s3, conv_w):
    t_len, d = da.shape
    hd = d // HEADS
    tm = _tile(t_len, 512)
    nt = t_len // tm

    def body(da_ref, db_ref, woa, wob, sv, dgt, lng, lnb, ws_ref, bs_ref, cw_ref,
             dp_ref, gws_ref, gbs_ref, glg_ref, glb_ref, gcw_ref, gcb_ref,
             dbuf, carry, a_ws, a_bs, a_lg, a_lb, a_c0, a_c1, a_c2, a_cb):
        i = pl.program_id(1)

        @pl.when(i == 0)
        def _():
            carry[...] = jnp.zeros_like(carry)
            for a in (a_ws, a_bs, a_lg, a_lb, a_c0, a_c1, a_c2, a_cb):
                a[...] = jnp.zeros_like(a)

        dya_t = _dot_nt(da_ref[...], woa[...])
        dyb_t = _dot_nt(db_ref[...], wob[...])
        g = lng[...]
        beta = lnb[...]
        wm = jnp.where(_tril_mask(), ws_ref[0], 0.0).astype(BF16)
        bs = bs_ref[0]
        w0, w1, w2 = cw_ref[0:1, :], cw_ref[1:2, :], cw_ref[2:3, :]
        dbuf[tm:tm + SUBLANES, :] = carry[...]
        gws = a_ws[...]
        gbs = a_bs[...]
        lg, lb = a_lg[...], a_lb[...]
        c0, c1, c2, cb = a_c0[...], a_c1[...], a_c2[...], a_cb[...]
        for ck in reversed(range(tm // CHUNK)):
            r0 = ck * CHUNK
            rows = slice(r0, r0 + CHUNK)
            dya = dya_t[rows]
            u = sv[0, rows, :].astype(F32)
            v = sv[1, rows, :].astype(F32)
            z = sv[2, rows, :].astype(F32)
            gu, gu_grad = _gelu(u)
            gv, gv_grad = _gelu(v)
            mu = jnp.mean(gv, axis=-1, keepdims=True)
            dvc = gv - mu
            var = jnp.mean(dvc * dvc, axis=-1, keepdims=True)
            rstd = lax.rsqrt(var + LN_EPS)
            vhat = dvc * rstd
            vn = (vhat * g + beta).astype(BF16)
            sz, sz_grad = _silu(z)
            t1 = dya * sz
            dmixed = t1 * gu
            dmixed_b = dmixed.astype(BF16)
            mixed = _dot(wm, vn) + bs
            gws = gws + _dot_nt(dmixed_b, vn)
            gbs = gbs + dmixed
            dvn = _dot_tn(wm, dmixed_b)
            dp_ref[0, rows, :] = (t1 * mixed * gu_grad).astype(BF16)
            dp_ref[2, rows, :] = (dya * gu * mixed * sz_grad).astype(BF16)
            lg = lg + _fold8(dvn * vhat)
            lb = lb + _fold8(dvn)
            dvh = dvn * g
            m1 = jnp.mean(dvh, axis=-1, keepdims=True)
            m2 = jnp.mean(dvh * vhat, axis=-1, keepdims=True)
            dp_ref[1, rows, :] = (rstd * (dvh - m1 - vhat * m2) * gv_grad).astype(BF16)
            dyb = dyb_t[rows]
            xbv = sv[3, rows, :].astype(F32)
            cbv = sv[4, rows, :].astype(F32)
            bbv = sv[5, rows, :].astype(F32)
            zbv = sv[6, rows, :].astype(F32)
            conv = sv[7, rows, :].astype(F32)
            szb, szb_grad = _silu(zbv)
            dp_ref[5, rows, :] = (dyb * conv * szb).astype(BF16)
            dp_ref[6, rows, :] = (dyb * bbv * conv * szb_grad).astype(BF16)
            dconv = dyb * bbv * szb
            dbuf[r0:r0 + CHUNK, :] = dconv
            dc1 = dbuf[r0 + 1:r0 + 1 + CHUNK, :]
            dc2 = dbuf[r0 + 2:r0 + 2 + CHUNK, :]
            if ck == 0:
                carry[...] = dconv[0:SUBLANES, :]
            h = cbv * xbv
            c2 = c2 + _fold8(dconv * h)
            c1 = c1 + _fold8(dc1 * h)
            c0 = c0 + _fold8(dc2 * h)
            cb = cb + _fold8(dconv)
            dh = w2 * dconv + w1 * dc1 + w0 * dc2
            dp_ref[3, rows, :] = (dh * cbv).astype(BF16)
            dp_ref[4, rows, :] = (dh * xbv).astype(BF16)
            dp_ref[7, rows, :] = dgt[0, rows, :]
            dp_ref[8, rows, :] = dgt[1, rows, :]
        a_ws[...] = gws
        a_bs[...] = gbs
        a_lg[...], a_lb[...] = lg, lb
        a_c0[...], a_c1[...], a_c2[...], a_cb[...] = c0, c1, c2, cb

        @pl.when(i == nt - 1)
        def _():
            gws_ref[0] = jnp.where(_tril_mask(), a_ws[...], 0.0)
            gbs_ref[0] = jnp.sum(a_bs[...], axis=1, keepdims=True)
            glg_ref[...] = jnp.sum(a_lg[...], axis=0, keepdims=True)
            glb_ref[...] = jnp.sum(a_lb[...], axis=0, keepdims=True)
            gcw_ref[0:1, :] = jnp.sum(a_c0[...], axis=0, keepdims=True)
            gcw_ref[1:2, :] = jnp.sum(a_c1[...], axis=0, keepdims=True)
            gcw_ref[2:3, :] = jnp.sum(a_c2[...], axis=0, keepdims=True)
            gcb_ref[...] = jnp.sum(a_cb[...], axis=0, keepdims=True)

    def rev(i):
        return nt - 1 - i

    row = pl.BlockSpec((tm, d), lambda g, i: (rev(i), 0))
    wrow = pl.BlockSpec((hd, d), lambda g, i: (g, 0))
    vec = pl.BlockSpec((1, hd), lambda g, i: (0, g))
    acc8 = pltpu.VMEM((SUBLANES, hd), F32)
    return pl.pallas_call(
        body, name="mixer_bwd", grid=(HEADS, nt),
        in_specs=[row, row, wrow, wrow,
                  pl.BlockSpec((N_SAVE, tm, hd), lambda g, i: (0, rev(i), g)),
                  pl.BlockSpec((2, tm, hd), lambda g, i: (0, rev(i), g)),
                  vec, vec,
                  pl.BlockSpec((1, CHUNK, CHUNK), lambda g, i: (g, 0, 0)),
                  pl.BlockSpec((1, CHUNK, 1), lambda g, i: (g, 0, 0)),
                  pl.BlockSpec((3, hd), lambda g, i: (0, g))],
        out_specs=[pl.BlockSpec((N_IN, tm, hd), lambda g, i: (0, rev(i), g)),
                   pl.BlockSpec((1, CHUNK, CHUNK), lambda g, i: (g, 0, 0)),
                   pl.BlockSpec((1, CHUNK, 1), lambda g, i: (g, 0, 0)),
                   vec, vec,
                   pl.BlockSpec((3, hd), lambda g, i: (0, g)),
                   vec],
        out_shape=[jax.ShapeDtypeStruct((N_IN, t_len, d), BF16),
                   jax.ShapeDtypeStruct((HEADS, CHUNK, CHUNK), F32),
                   jax.ShapeDtypeStruct((HEADS, CHUNK, 1), F32),
                   jax.ShapeDtypeStruct((1, d), F32), jax.ShapeDtypeStruct((1, d), F32),
                   jax.ShapeDtypeStruct((3, d), F32), jax.ShapeDtypeStruct((1, d), F32)],
        scratch_shapes=[pltpu.VMEM((tm + SUBLANES, hd), F32), acc8,
                        pltpu.VMEM((CHUNK, CHUNK), F32), pltpu.VMEM((CHUNK, hd), F32),
                        acc8, acc8, acc8, acc8, acc8, acc8],
        compiler_params=_cparams(("parallel", "arbitrary")),
    )(da, db, w_oa, w_ob, saved, dgate, ln_v_g, ln_v_b, w_s, b_s3, conv_w)


def _grad_w_in(xb, dp):
    t_len, d = xb.shape
    tm, tn, tk = _tile(d, MM_TILE), _tile(d, MM_TILE), _tile(t_len, MM_DEPTH_TOKENS)
    nj = d // tn
    nk = t_len // tk

    def body(a_ref, b_ref, o_ref, acc):
        k = pl.program_id(2)

        @pl.when(k == 0)
        def _():
            acc[...] = jnp.zeros_like(acc)

        acc[...] += _dot_tn(a_ref[...], b_ref[...])

        @pl.when(k == nk - 1)
        def _():
            o_ref[...] = acc[...].astype(BF16)

    return pl.pallas_call(
        body, name="grad_w_in", grid=(d // tm, N_IN * nj, nk),
        in_specs=[pl.BlockSpec((tk, tm), lambda i, j, k: (k, i)),
                  pl.BlockSpec((None, tk, tn), lambda i, j, k: (j // nj, k, j % nj))],
        out_specs=pl.BlockSpec((tm, tn), lambda i, j, k: (i, j)),
        out_shape=jax.ShapeDtypeStruct((d, N_IN * d), BF16),
        scratch_shapes=[pltpu.VMEM((tm, tn), F32)],
        compiler_params=_cparams(("parallel", "parallel", "arbitrary")),
    )(xb, dp)


def _grad_x_tail(dp, w_in, partial, slots):
    _, t_len, d = dp.shape
    lo, hi = slots
    tm, tn, tk = _tile(t_len, MM_TILE), _tile(d, MM_TILE), _tile(d, MM_DEPTH)
    nkb = d // tk
    nk = (hi - lo) * nkb

    def body(a_ref, b_ref, r_ref, o_ref, acc):
        k = pl.program_id(2)

        @pl.when(k == 0)
        def _():
            acc[...] = r_ref[...]

        acc[...] += _dot(a_ref[...], b_ref[...])

        @pl.when(k == nk - 1)
        def _():
            o_ref[...] = acc[...]

    return pl.pallas_call(
        body, name="grad_x_tail", grid=(t_len // tm, d // tn, nk),
        in_specs=[pl.BlockSpec((None, tm, tk), lambda i, j, k: (lo + k // nkb, i, k % nkb)),
                  pl.BlockSpec((tk, tn), lambda i, j, k: (lo * nkb + k, j)),
                  pl.BlockSpec((tm, tn), lambda i, j, k: (i, j))],
        out_specs=pl.BlockSpec((tm, tn), lambda i, j, k: (i, j)),
        out_shape=jax.ShapeDtypeStruct((t_len, d), F32),
        scratch_shapes=[pltpu.VMEM((tm, tn), F32)],
        compiler_params=_cparams(("parallel", "parallel", "arbitrary")),
    )(dp, w_in, partial)


def _grad_x(dp, w_in, gx_direct, parts, packed, slots):
    _, t_len, d = dp.shape
    tm, tn, tk = _tile(t_len, MM_TILE), _tile(d, MM_TILE), _tile(d, MM_DEPTH)
    nkb = d // tk
    lo, hi = slots
    nk = (hi - lo) * nkb
    ni, nj = t_len // tm, d // tn
    n_parts = len(parts)

    def body(a_ref, b_ref, r_ref, *rest):
        srcs = rest[:n_parts]
        pk_ref = rest[n_parts]
        o_ref = rest[n_parts + 1]
        lands = rest[n_parts + 2:2 * n_parts + 2]
        gath = rest[2 * n_parts + 2]
        acc, send_sems, recv_sems, pk_send, pk_recv, pk_local = rest[2 * n_parts + 3:]
        i, j, k = pl.program_id(0), pl.program_id(1), pl.program_id(2)
        x, y, c = lax.axis_index("x"), lax.axis_index("y"), lax.axis_index("c")
        my_chip = 2 * x + y
        me = 4 * x + 2 * y + c
        chips = [(1 - x, y), (x, 1 - y), (1 - x, 1 - y)]

        def part_copy(t, n):
            px, py = chips[n]
            return pltpu.make_async_remote_copy(
                src_ref=srcs[t].at[2 * px + py], dst_ref=lands[t].at[my_chip],
                send_sem=send_sems.at[3 * t + n], recv_sem=recv_sems.at[3 * t + n],
                device_id=(px, py, c), device_id_type=MESH)

        def part_landing(t, n):
            px, py = chips[n]
            return pltpu.make_async_remote_copy(
                src_ref=srcs[t].at[my_chip], dst_ref=lands[t].at[2 * px + py],
                send_sem=send_sems.at[3 * t + n], recv_sem=recv_sems.at[3 * t + n],
                device_id=(px, py, c), device_id_type=MESH)

        def pk_copy(s):
            return pltpu.make_async_remote_copy(
                src_ref=pk_ref, dst_ref=gath.at[me], send_sem=pk_send, recv_sem=pk_recv.at[me],
                device_id=(s // 4, (s // 2) % 2, s % 2), device_id_type=MESH)

        @pl.when((i == 0) & (j == 0) & (k == 0))
        def _():
            for t in range(n_parts):
                for n in range(3):
                    part_copy(t, n).start()
            pltpu.make_async_copy(pk_ref, gath.at[me], pk_local).start()
            for s in range(N_DEV):
                @pl.when(s != me)
                def _(s=s):
                    pk_copy(s).start()

        @pl.when(k == 0)
        def _():
            acc[...] = r_ref[...]

        acc[...] += _dot(a_ref[...], b_ref[...])

        @pl.when(k == nk - 1)
        def _():
            o_ref[...] = acc[...]

        @pl.when((i == ni - 1) & (j == nj - 1) & (k == nk - 1))
        def _():
            for t in range(n_parts):
                for n in range(3):
                    part_landing(t, n).wait_recv()
            for t in range(n_parts):
                for n in range(3):
                    part_copy(t, n).wait_send()
            for s in range(N_DEV):
                @pl.when(s != me)
                def _(s=s):
                    pltpu.make_async_remote_copy(
                        src_ref=pk_ref, dst_ref=gath.at[s], send_sem=pk_send, recv_sem=pk_recv.at[s],
                        device_id=(s // 4, (s // 2) % 2, s % 2), device_id_type=MESH).wait_recv()
            seven = gath.at[pl.ds(0, N_DEV - 1)]
            pltpu.make_async_remote_copy(src_ref=seven, dst_ref=seven, send_sem=pk_send, recv_sem=pk_send,
                                         device_id=(x, y, 1 - c), device_id_type=MESH).wait_send()
            pltpu.make_async_copy(pk_ref, gath.at[me], pk_local).wait()

    hbm = pl.BlockSpec(memory_space=pl.ANY)
    outs = pl.pallas_call(
        body, name="grad_x", grid=(ni, nj, nk),
        in_specs=[pl.BlockSpec((None, tm, tk), lambda i, j, k: (lo + k // nkb, i, k % nkb)),
                  pl.BlockSpec((tk, tn), lambda i, j, k: (lo * nkb + k, j)),
                  pl.BlockSpec((tm, tn), lambda i, j, k: (i, j))] + [hbm] * (n_parts + 1),
        out_specs=[pl.BlockSpec((tm, tn), lambda i, j, k: (i, j))] + [hbm] * (n_parts + 1),
        out_shape=[jax.ShapeDtypeStruct((t_len, d), F32)] + [jax.ShapeDtypeStruct(p.shape, p.dtype) for p in parts]
        + [jax.ShapeDtypeStruct((N_DEV,) + packed.shape, packed.dtype)],
        scratch_shapes=[pltpu.VMEM((tm, tn), F32),
                        pltpu.SemaphoreType.DMA((3 * n_parts,)), pltpu.SemaphoreType.DMA((3 * n_parts,)),
                        pltpu.SemaphoreType.DMA(()), pltpu.SemaphoreType.DMA((N_DEV,)), pltpu.SemaphoreType.DMA(())],
        compiler_params=_cparams(("arbitrary", "arbitrary", "arbitrary")),
    )(dp, w_in, gx_direct, *parts, packed)
    return outs[0], list(outs[1:1 + n_parts]), outs[1 + n_parts]


def kernel(x, w_in, b_gate, ln_v_g, ln_v_b, w_s, b_s, conv_w, conv_b, w_oa, w_ob, w_out, ln_g, ln_b, loss_target, m_w_in, m_b_gate, m_ln_v_g, m_ln_v_b, m_w_s, m_b_s, m_conv_w, m_conv_b, m_w_oa, m_w_ob, m_w_out, m_ln_g, m_ln_b, v_w_in, v_b_gate, v_ln_v_g, v_ln_v_b, v_w_s, v_b_s, v_conv_w, v_conv_b, v_w_oa, v_w_ob, v_w_out, v_ln_g, v_ln_b):
    _, t_len, d = x.shape
    assert d % (HEADS * 128) == 0 and t_len % CHUNK == 0 and w_in.shape[2] * N_DEV == N_IN * d
    x2 = x[0]
    tgt2 = loss_target[0]
    c_arr = lax.axis_index("c").astype(jnp.int32).reshape(1)
    chip_arr = (2 * lax.axis_index("x") + lax.axis_index("y")).astype(jnp.int32).reshape(1)
    dev = 4 * lax.axis_index("x") + 2 * lax.axis_index("y") + lax.axis_index("c")

    shards = [_cast_transposed(w_in[0], "cast_w_in"), _cast_bf16(w_oa[0], "cast_w_oa"),
              _cast_bf16(w_ob[0], "cast_w_ob"), _cast_bf16(w_out[0], "cast_w_out")]
    (conv_w_g,) = _all_gather([conv_w[0]], ["lead"], "gather_conv_w", vmem=True)
    conv_w_f = jnp.transpose(conv_w_g, (1, 0, 2)).reshape(3, d)
    w_s3 = w_s[0]
    b_s3 = b_s[0].reshape(HEADS, CHUNK, 1)

    saved, ya, yb, w_in_f, w_oa_f, w_ob_f, w_out_f = _mixer_fwd(
        x2, shards, ln_v_g, ln_v_b, w_s3, b_s3, conv_w_f, conv_b)
    sa, sb, pa, pb, merged = _merge_fwd(x2, ya, yb, w_in_f, w_oa_f, w_ob_f, b_gate)
    dz, gx_direct, xb, da, db, dgate, g_ln_g, g_ln_b, err2, g_bga, g_bgb = _out_ln_loss(
        merged, w_out_f, x2, tgt2, ln_g, ln_b, sa, sb, pa, pb)
    loss = lax.psum(0.5 * jnp.sum(err2) / d, ("x", "y", "c"))

    dp, g_ws, g_bs, g_lvg, g_lvb, g_cw, g_cb = _mixer_bwd(
        da, db, w_oa_f, w_ob_f, saved, dgate, ln_v_g, ln_v_b, w_s3, b_s3, conv_w_f)
    gw_in = _grad_w_in(xb, dp)

    gw_out, land_in = _grad_w(merged, dz, "grad_w_out", exchange=gw_in)
    part_in = _pair_sum(gw_in, land_in, "col", c_arr, "grad_pair_sum_0")
    gw_oa = _grad_w(ya, da, "grad_w_oa")
    gw_ob = _grad_w(yb, db, "grad_w_ob")
    rows = [gw_oa, gw_ob, gw_out]
    lands = _pair_exchange(rows, ["row"] * 3, "grad_pair_exchange")
    parts = [part_in] + [_pair_sum(g, l, "row", c_arr, "grad_pair_sum_%d" % (n + 1))
                         for n, (g, l) in enumerate(zip(rows, lands))]
    pieces = [jnp.concatenate([g_bga, g_bgb], axis=1), g_lvg, g_lvb, g_ws, g_bs, g_cw, g_cb, g_ln_g, g_ln_b]
    sizes = [p.size for p in pieces]
    packed = jnp.concatenate([p.reshape(-1, 128) for p in pieces], axis=0)
    gx_part, lands2, gathered = _grad_x(dp, w_in_f, gx_direct, parts, packed, (0, GRAD_X_COMM_BLOCKS))
    grad_x = _grad_x_tail(dp, w_in_f, gx_part, (GRAD_X_COMM_BLOCKS, N_IN))[None]

    big = []
    for n, (w, m, v) in enumerate([(w_in, m_w_in, v_w_in), (w_oa, m_w_oa, v_w_oa), (w_ob, m_w_ob, v_w_ob),
                                   (w_out, m_w_out, v_w_out)]):
        big.append([o[None] for o in _sum_adam(parts[n], lands2[n], w[0], m[0], v[0], chip_arr, "sum_adam_%d" % n)])
    (g_w_in, d_w_in, nm_w_in, nv_w_in), (g_w_oa, d_w_oa, nm_w_oa, nv_w_oa), \
        (g_w_ob, d_w_ob, nm_w_ob, nv_w_ob), (g_w_out, d_w_out, nm_w_out, nv_w_out) = big

    total = _small_sum(gathered, "sum_small_grads")
    offs = [0]
    for s in sizes:
        offs.append(offs[-1] + s // 128)
    unpacked = [total[offs[n]:offs[n + 1]] for n in range(len(pieces))]
    g_b_gate = unpacked[0].reshape(b_gate.shape)
    g_ln_v_g = unpacked[1].reshape(ln_v_g.shape)
    g_ln_v_b = unpacked[2].reshape(ln_v_b.shape)
    g_w_s = unpacked[3].reshape(w_s.shape)
    g_b_s = unpacked[4].reshape(b_s.shape)
    g_conv_w = lax.dynamic_slice_in_dim(unpacked[5].reshape(3, d), dev * (d // N_DEV), d // N_DEV, axis=1)[None]
    g_conv_b = unpacked[6].reshape(conv_b.shape)
    g_ln_g2 = unpacked[7].reshape(ln_g.shape)
    g_ln_b2 = unpacked[8].reshape(ln_b.shape)

    small_w = [b_gate, ln_v_g, ln_v_b, w_s, b_s, conv_w, conv_b, ln_g, ln_b]
    small_g = [g_b_gate, g_ln_v_g, g_ln_v_b, g_w_s, g_b_s, g_conv_w, g_conv_b, g_ln_g2, g_ln_b2]
    small_m = [m_b_gate, m_ln_v_g, m_ln_v_b, m_w_s, m_b_s, m_conv_w, m_conv_b, m_ln_g, m_ln_b]
    small_v = [v_b_gate, v_ln_v_g, v_ln_v_b, v_w_s, v_b_s, v_conv_w, v_conv_b, v_ln_g, v_ln_b]

    def flat(a):
        return a.reshape(-1, a.shape[-1])

    res = _small_adam([flat(a) for a in small_w], [flat(a) for a in small_g], [flat(a) for a in small_m],
                      [flat(a) for a in small_v], "adam_small")
    ns = len(small_w)
    d_s = [res[n].reshape(small_w[n].shape) for n in range(ns)]
    nm_s = [res[ns + n].reshape(small_w[n].shape) for n in range(ns)]
    nv_s = [res[2 * ns + n].reshape(small_w[n].shape) for n in range(ns)]

    def ordered(first, small, oa, ob, out):
        return [first] + small[:7] + [oa, ob, out] + small[7:]

    return (loss, grad_x,
            *ordered(g_w_in, small_g, g_w_oa, g_w_ob, g_w_out),
            *ordered(d_w_in, d_s, d_w_oa, d_w_ob, d_w_out),
            *ordered(nm_w_in, nm_s, nm_w_oa, nm_w_ob, nm_w_out),
            *ordered(nv_w_in, nv_s, nv_w_oa, nv_w_ob, nv_w_out))
```

```python
import functools

import jax
import jax.numpy as jnp
from jax import lax
from jax.experimental import pallas as pl
from jax.experimental.pallas import tpu as pltpu

F32 = jnp.float32
BF16 = jnp.bfloat16
MESH = pl.DeviceIdType.MESH

N_DEV = 8
N_CHIP = 4
HEADS = 8
CHUNK = 128
N_IN = 9
N_MIX = 7
N_SAVE = 8
LN_EPS = 1e-5
DN_ALPHA = 2.0 ** 0.25
ADAM_LR = 0.001
ADAM_B1 = 0.9
ADAM_B2 = 0.999
ADAM_EPS = 1e-08
ADAM_WD = 0.01
ADAM_STEP = 10
GELU_C0 = 0.7978845608028654
GELU_C1 = 0.044715
SUBLANES = 8
FIRST_SENDS = 3
SENDS_PER_GROUP = 2
MM_TILE = 1024
MM_DEPTH = 2048
MM_DEPTH_TOKENS = 4096
GRAD_X_COMM_BLOCKS = 8
VMEM_LIMIT = 56 << 20


def _cparams(sem):
    return pltpu.CompilerParams(dimension_semantics=sem, vmem_limit_bytes=VMEM_LIMIT)


def _tile(n, want):
    t = min(n, want)
    while n % t:
        t //= 2
    return t


def _gelu(u):
    u2 = u * u
    t = jnp.tanh(u * (GELU_C0 + (GELU_C0 * GELU_C1) * u2))
    hp = 0.5 * t + 0.5
    grad = hp + (0.5 * u) * (1.0 - t * t) * (GELU_C0 + (3.0 * GELU_C0 * GELU_C1) * u2)
    return u * hp, grad


def _silu(z):
    s = jax.nn.sigmoid(z)
    sil = z * s
    return sil, s + sil * (1.0 - s)


def _fold8(a):
    return jnp.sum(a.reshape(a.shape[0] // SUBLANES, SUBLANES, a.shape[1]), axis=0)


def _dot(a, b):
    return jnp.dot(a, b, preferred_element_type=F32)


def _dot_nt(a, b):
    return lax.dot_general(a, b, (((1,), (1,)), ((), ())), preferred_element_type=F32)


def _dot_tn(a, b):
    return lax.dot_general(a, b, (((0,), (0,)), ((), ())), preferred_element_type=F32)


def _tril_mask():
    r = lax.broadcasted_iota(jnp.int32, (CHUNK, CHUNK), 0)
    c = lax.broadcasted_iota(jnp.int32, (CHUNK, CHUNK), 1)
    return c <= r


def _cast_bf16(a, name):
    rows, cols = a.shape
    rb = _tile(rows, 256)

    def body(a_ref, o_ref):
        o_ref[...] = a_ref[...].astype(BF16)

    return pl.pallas_call(
        body, name=name, grid=(rows // rb,),
        in_specs=[pl.BlockSpec((rb, cols), lambda i: (i, 0))],
        out_specs=pl.BlockSpec((rb, cols), lambda i: (i, 0)),
        out_shape=jax.ShapeDtypeStruct((rows, cols), BF16),
        compiler_params=_cparams(("parallel",)),
    )(a)


def _cast_transposed(a, name):
    rows, cols = a.shape
    tb = _tile(rows, 2048)
    tc = _tile(cols, 256)

    def body(a_ref, o_ref):
        o_ref[...] = a_ref[...].T.astype(BF16)

    return pl.pallas_call(
        body, name=name, grid=(cols // tc, rows // tb),
        in_specs=[pl.BlockSpec((tb, tc), lambda j, i: (i, j))],
        out_specs=pl.BlockSpec((tc, tb), lambda j, i: (j, i)),
        out_shape=jax.ShapeDtypeStruct((cols, rows), BF16),
        compiler_params=_cparams(("parallel", "parallel")),
    )(a)


def _shard_ref(full, kind, s, n):
    if kind == "col":
        return full.at[:, pl.ds(pl.multiple_of(s * n, 128), n)]
    if kind == "row":
        return full.at[pl.ds(pl.multiple_of(s * n, SUBLANES), n), :]
    return full.at[s]


def _all_gather(shards, kinds, name, vmem):
    nt = len(shards)
    out_shapes = []
    for a, kind in zip(shards, kinds):
        if kind == "col":
            out_shapes.append(jax.ShapeDtypeStruct((a.shape[0], N_DEV * a.shape[1]), a.dtype))
        elif kind == "row":
            out_shapes.append(jax.ShapeDtypeStruct((N_DEV * a.shape[0], a.shape[1]), a.dtype))
        else:
            out_shapes.append(jax.ShapeDtypeStruct((N_DEV,) + a.shape, a.dtype))

    def body(*refs):
        srcs, fulls = refs[:nt], refs[nt:2 * nt]
        send_sems, recv_sems, local_sems = refs[2 * nt:]
        x, y, c = lax.axis_index("x"), lax.axis_index("y"), lax.axis_index("c")
        sibling = (x, y, 1 - c)
        chips = [(1 - x, y), (x, 1 - y), (1 - x, 1 - y)]

        def dev(px, py, pc):
            return 4 * px + 2 * py + pc

        def region(t, s):
            a, kind = shards[t], kinds[t]
            n = a.shape[1] if kind == "col" else a.shape[0]
            return _shard_ref(fulls[t], kind, s, n)

        def copy(t, k, block, to, own=False):
            return pltpu.make_async_remote_copy(
                src_ref=srcs[t] if own else region(t, block), dst_ref=region(t, block),
                send_sem=send_sems.at[7 * t + k], recv_sem=recv_sems.at[7 * t + k],
                device_id=to, device_id_type=MESH)

        me = dev(x, y, c)
        started = []
        for t in range(nt):
            mine = pltpu.make_async_copy(srcs[t], region(t, me), local_sems.at[t])
            mine.start()
            started.append(mine)
        first = []
        for t in range(nt):
            first.append(copy(t, 0, me, sibling, own=True))
            for j, chip in enumerate(chips):
                first.append(copy(t, 1 + j, me, (*chip, c), own=True))
        for cp in first:
            cp.start()
        passed = []
        for t in range(nt):
            for j, chip in enumerate(chips):
                blk = dev(*chip, c)
                copy(t, 1 + j, blk, sibling).wait_recv()
                fwd = copy(t, 4 + j, blk, sibling)
                fwd.start()
                passed.append(fwd)
        for t in range(nt):
            copy(t, 0, dev(x, y, 1 - c), sibling).wait_recv()
            for j, chip in enumerate(chips):
                copy(t, 4 + j, dev(*chip, 1 - c), sibling).wait_recv()
        for cp in first + passed:
            cp.wait_send()
        for mine in started:
            mine.wait()

    space = pltpu.VMEM if vmem else pl.ANY
    return pl.pallas_call(
        body, name=name,
        in_specs=[pl.BlockSpec(memory_space=space)] * nt,
        out_specs=[pl.BlockSpec(memory_space=space)] * nt,
        out_shape=out_shapes,
        scratch_shapes=[pltpu.SemaphoreType.DMA((7 * nt,)), pltpu.SemaphoreType.DMA((7 * nt,)),
                        pltpu.SemaphoreType.DMA((nt,))],
        compiler_params=pltpu.CompilerParams(vmem_limit_bytes=VMEM_LIMIT, has_side_effects=True),
    )(*shards)


def _pair_exchange(grads, kinds, name):
    nt = len(grads)
    shard_shapes = []
    for g, kind in zip(grads, kinds):
        shard_shapes.append((g.shape[0], g.shape[1] // N_DEV) if kind == "col" else (g.shape[0] // N_DEV, g.shape[1]))

    def body(*refs):
        srcs, lands = refs[:nt], refs[nt:2 * nt]
        send_sems, recv_sems = refs[2 * nt:]
        x, y, c = lax.axis_index("x"), lax.axis_index("y"), lax.axis_index("c")
        copies = []
        for t in range(nt):
            n = shard_shapes[t][1] if kinds[t] == "col" else shard_shapes[t][0]
            for k in range(N_CHIP):
                cp = pltpu.make_async_remote_copy(
                    src_ref=_shard_ref(srcs[t], kinds[t], 2 * k + 1 - c, n), dst_ref=lands[t].at[k],
                    send_sem=send_sems.at[N_CHIP * t + k], recv_sem=recv_sems.at[N_CHIP * t + k],
                    device_id=(x, y, 1 - c), device_id_type=MESH)
                cp.start()
                copies.append(cp)
        for cp in copies:
            cp.wait()

    return pl.pallas_call(
        body, name=name,
        in_specs=[pl.BlockSpec(memory_space=pl.ANY)] * nt,
        out_specs=[pl.BlockSpec(memory_space=pl.ANY)] * nt,
        out_shape=[jax.ShapeDtypeStruct((N_CHIP,) + s, g.dtype) for s, g in zip(shard_shapes, grads)],
        scratch_shapes=[pltpu.SemaphoreType.DMA((N_CHIP * nt,)), pltpu.SemaphoreType.DMA((N_CHIP * nt,))],
        compiler_params=pltpu.CompilerParams(has_side_effects=True),
    )(*grads)


def _pair_sum(grad, land, kind, c_arr, name):
    _, r, w = land.shape
    rb = _tile(r, 1024)
    nrb = r // rb

    def body(c_ref, g_ref, l_ref, o_ref):
        o_ref[...] = (g_ref[...].astype(F32) + l_ref[...].astype(F32)).astype(o_ref.dtype)

    if kind == "col":
        g_spec = pl.BlockSpec((rb, w), lambda k, i, c: (i, 2 * k + c[0]))
    else:
        g_spec = pl.BlockSpec((rb, w), lambda k, i, c: ((2 * k + c[0]) * nrb + i, 0))
    return pl.pallas_call(
        body, name=name,
        grid_spec=pltpu.PrefetchScalarGridSpec(
            num_scalar_prefetch=1, grid=(N_CHIP, nrb),
            in_specs=[g_spec, pl.BlockSpec((None, rb, w), lambda k, i, c: (k, i, 0))],
            out_specs=pl.BlockSpec((None, rb, w), lambda k, i, c: (k, i, 0))),
        out_shape=jax.ShapeDtypeStruct(land.shape, BF16),
        compiler_params=_cparams(("parallel", "parallel")),
    )(c_arr, grad, land)


def _adam(w, g, m, v):
    m = ADAM_B1 * m + (1.0 - ADAM_B1) * g
    v = ADAM_B2 * v + (1.0 - ADAM_B2) * jnp.square(g)
    m_hat = m / (1.0 - ADAM_B1 ** ADAM_STEP)
    v_hat = v / (1.0 - ADAM_B2 ** ADAM_STEP)
    delta = -ADAM_LR * (m_hat / (jnp.sqrt(v_hat) + ADAM_EPS) + ADAM_WD * w)
    return delta, m, v


def _sum_adam(part, land, w, m, v, chip_arr, name):
    r, wd = w.shape
    rb = _tile(r, 256)

    def body(k_ref, own, r1, r2, r3, w_ref, m_ref, v_ref, g_out, d_out, m_out, v_out):
        g = own[...].astype(F32) + r1[...].astype(F32) + r2[...].astype(F32) + r3[...].astype(F32)
        d, mn, vn = _adam(w_ref[...], g, m_ref[...], v_ref[...])
        g_out[...] = g
        d_out[...] = d
        m_out[...] = mn
        v_out[...] = vn

    def slot(off):
        return pl.BlockSpec((None, rb, wd), lambda i, k: ((k[0] + off) % N_CHIP, i, 0))

    plain = pl.BlockSpec((rb, wd), lambda i, k: (i, 0))
    return pl.pallas_call(
        body, name=name,
        grid_spec=pltpu.PrefetchScalarGridSpec(
            num_scalar_prefetch=1, grid=(r // rb,),
            in_specs=[slot(0), slot(1), slot(2), slot(3), plain, plain, plain],
            out_specs=[plain] * 4),
        out_shape=[jax.ShapeDtypeStruct(w.shape, F32)] * 4,
        compiler_params=_cparams(("parallel",)),
    )(chip_arr, part, land, land, land, w, m, v)


def _small_sum(gathered, name):
    _, r, w = gathered.shape

    def body(g_ref, o_ref):
        acc = g_ref[0]
        for d in range(1, N_DEV):
            acc = acc + g_ref[d]
        o_ref[...] = acc

    return pl.pallas_call(body, name=name, out_shape=jax.ShapeDtypeStruct((r, w), F32))(gathered)


def _small_adam(ws, gs, ms, vs, name):
    n = len(ws)

    def body(*refs):
        ins, outs = refs[:4 * n], refs[4 * n:]
        for t in range(n):
            d, mn, vn = _adam(ins[t][...], ins[n + t][...], ins[2 * n + t][...], ins[3 * n + t][...])
            outs[t][...] = d
            outs[n + t][...] = mn
            outs[2 * n + t][...] = vn

    shapes = [jax.ShapeDtypeStruct(w.shape, F32) for w in ws]
    return pl.pallas_call(body, name=name, out_shape=shapes * 3)(*ws, *gs, *ms, *vs)


def _mixer_fwd(x, shards, ln_v_g, ln_v_b, w_s, b_s3, conv_w, conv_b):
    t_len, d = x.shape
    hd = d // HEADS
    tm = _tile(t_len, 512)
    nt = t_len // tm
    assert nt >= 2
    w8 = shards[0].shape[0]
    bps = w8 // hd
    r8 = shards[1].shape[0]
    nq = N_IN * HEADS
    n_blocks = nq + 3 * N_DEV
    groups = [[("in", HEADS * b + HEADS - 1 - p) for b in range(N_MIX)] for p in range(HEADS)]
    groups.append([("in", q) for q in range(N_MIX * HEADS, nq)] + [(t, s) for t in (1, 2, 3) for s in range(N_DEV)])

    def owner_of(blk):
        return blk[1] // bps if blk[0] == "in" else blk[1]

    send_step = {}
    for s in range(N_DEV):
        mine = [blk for grp in groups for blk in grp if owner_of(blk) == s]
        for pos, blk in enumerate(mine):
            send_step[blk] = 0 if pos < FIRST_SENDS else 1 + (pos - FIRST_SENDS) // SENDS_PER_GROUP
    assert max(send_step.values()) < HEADS
    assert all(send_step[blk] <= max(k - 1, 0) for k in range(HEADS) for blk in groups[k])
    tail_pass_step = nt - 6 if nt >= 8 else nt - 2

    def body(x_ref, sh_in, sh_oa, sh_ob, sh_out, lng, lnb, ws_ref, bs_ref, cw_ref, cb_ref,
             save_ref, ya_ref, yb_ref, f_in, f_oa, f_ob, f_out,
             hbuf, wbuf, recv_sems, own_sems, fwd_sems, local_sems, load_sems):
        g = pl.program_id(0)
        i = pl.program_id(1)
        x, y, c = lax.axis_index("x"), lax.axis_index("y"), lax.axis_index("c")
        sibling = (x, y, 1 - c)
        chips = [(1 - x, y), (x, 1 - y), (1 - x, 1 - y)]
        shard_refs = (sh_in, sh_oa, sh_ob, sh_out)
        fulls = (f_in, f_oa, f_ob, f_out)

        def tensor(blk):
            return 0 if blk[0] == "in" else blk[0]

        def owner(blk):
            s = owner_of(blk)
            return s // 4, (s // 2) % 2, s % 2

        def bid(blk):
            return blk[1] if blk[0] == "in" else nq + (blk[0] - 1) * N_DEV + blk[1]

        def region(blk):
            if blk[0] == "in":
                return f_in.at[pl.ds(blk[1] * hd, hd), :]
            return fulls[blk[0]].at[pl.ds(blk[1] * r8, r8), :]

        def own_src(blk):
            if blk[0] == "in":
                return sh_in.at[pl.ds((blk[1] % bps) * hd, hd), :]
            return shard_refs[blk[0]]

        def rcopy(blk, to, send_sem, own):
            return pltpu.make_async_remote_copy(
                src_ref=own_src(blk) if own else region(blk), dst_ref=region(blk),
                send_sem=send_sem, recv_sem=recv_sems.at[bid(blk)], device_id=to, device_id_type=MESH)

        def send_own(blk):
            ox, oy, oc = owner(blk)

            @pl.when((x == ox) & (y == oy) & (c == oc))
            def _():
                rcopy(blk, sibling, own_sems.at[tensor(blk)], True).start()
                for chip in chips:
                    rcopy(blk, (*chip, c), own_sems.at[tensor(blk)], True).start()

        def pass_on(blk):
            ox, oy, oc = owner(blk)

            @pl.when(((x != ox) | (y != oy)) & (c == oc))
            def _():
                rcopy(blk, sibling, fwd_sems.at[tensor(blk)], False).wait_recv()
                rcopy(blk, sibling, fwd_sems.at[tensor(blk)], False).start()

        def wait_from_sibling(blk):
            @pl.when(c != owner(blk)[2])
            def _():
                rcopy(blk, sibling, fwd_sems.at[tensor(blk)], False).wait_recv()

        def local_copy(t):
            me = 4 * x + 2 * y + c
            return pltpu.make_async_copy(shard_refs[t], _shard_ref(fulls[t], "row", me, w8 if t == 0 else r8),
                                         local_sems.at[t])

        first = (g == 0) & (i == 0)

        @pl.when(first)
        def _():
            for t in range(4):
                local_copy(t).start()
            for grp in groups:
                for blk in grp:
                    if send_step[blk] == 0:
                        send_own(blk)
            for blk in groups[0]:
                pass_on(blk)

        @pl.when(i == 0)
        def _():
            for k in range(1, HEADS):
                @pl.when(g == k)
                def _(k=k):
                    for grp in groups:
                        for blk in grp:
                            if send_step[blk] == k:
                                send_own(blk)

        @pl.when(i == nt - 2)
        def _():
            for k in range(HEADS - 1):
                @pl.when(g == k)
                def _(k=k):
                    for blk in groups[k + 1]:
                        pass_on(blk)

        @pl.when((g == HEADS - 1) & (i == tail_pass_step))
        def _():
            for blk in groups[HEADS]:
                pass_on(blk)

        def fetch_weights(k):
            for blk in groups[k]:
                wait_from_sibling(blk)
            for b, blk in enumerate(groups[k]):
                ox, oy, oc = owner(blk)
                mine = (x == ox) & (y == oy) & (c == oc)

                @pl.when(mine)
                def _(b=b, blk=blk):
                    pltpu.make_async_copy(own_src(blk), wbuf.at[k % 2, b], load_sems.at[b]).start()

                @pl.when(jnp.logical_not(mine))
                def _(b=b, blk=blk):
                    pltpu.make_async_copy(region(blk), wbuf.at[k % 2, b], load_sems.at[b]).start()

        @pl.when(i == 0)
        def _():
            for k in range(HEADS):
                @pl.when(g == k)
                def _(k=k):
                    if k == 0:
                        fetch_weights(0)
                    for b, blk in enumerate(groups[k]):
                        pltpu.make_async_copy(region(blk), wbuf.at[k % 2, b], load_sems.at[b]).wait()

        @pl.when(i == nt - 1)
        def _():
            for k in range(1, HEADS):
                @pl.when(g == k - 1)
                def _(k=k):
                    fetch_weights(k)

        xt = x_ref[...].astype(BF16)
        wslot = g % 2
        u = _dot_nt(xt, wbuf[wslot, 0])
        v = _dot_nt(xt, wbuf[wslot, 1])
        z = _dot_nt(xt, wbuf[wslot, 2])
        save_ref[0] = u.astype(BF16)
        save_ref[1] = v.astype(BF16)
        save_ref[2] = z.astype(BF16)
        gu, _ = _gelu(u)
        gv, _ = _gelu(v)
        mu = jnp.mean(gv, axis=-1, keepdims=True)
        dv = gv - mu
        var = jnp.mean(dv * dv, axis=-1, keepdims=True)
        vn = (dv * lax.rsqrt(var + LN_EPS) * lng[...] + lnb[...]).astype(BF16)
        sz, _ = _silu(z)
        gate = gu * sz
        wm = jnp.where(_tril_mask(), ws_ref[0], 0.0).astype(BF16)
        bs = bs_ref[0]
        for ck in range(tm // CHUNK):
            rows = slice(ck * CHUNK, (ck + 1) * CHUNK)
            mixed = _dot(wm, vn[rows]) + bs
            ya_ref[rows, :] = (gate[rows] * mixed).astype(BF16)

        xbv = _dot_nt(xt, wbuf[wslot, 3])
        cbv = _dot_nt(xt, wbuf[wslot, 4])
        bbv = _dot_nt(xt, wbuf[wslot, 5])
        zbv = _dot_nt(xt, wbuf[wslot, 6])
        save_ref[3] = xbv.astype(BF16)
        save_ref[4] = cbv.astype(BF16)
        save_ref[5] = bbv.astype(BF16)
        save_ref[6] = zbv.astype(BF16)
        h = cbv * xbv

        @pl.when(i == 0)
        def _():
            hbuf[0:SUBLANES, :] = jnp.zeros((SUBLANES, hd), F32)

        hbuf[SUBLANES:SUBLANES + tm, :] = h
        h1 = hbuf[SUBLANES - 1:SUBLANES - 1 + tm, :]
        h2 = hbuf[SUBLANES - 2:SUBLANES - 2 + tm, :]
        conv = cb_ref[...] + cw_ref[0:1, :] * h2 + cw_ref[1:2, :] * h1 + cw_ref[2:3, :] * h
        hbuf[0:SUBLANES, :] = h[tm - SUBLANES:tm, :]
        save_ref[7] = conv.astype(BF16)
        szb, _ = _silu(zbv)
        yb_ref[...] = (bbv * conv * szb).astype(BF16)

        @pl.when((g == HEADS - 1) & (i == nt - 1))
        def _():
            for blk in groups[HEADS]:
                wait_from_sibling(blk)
            for t in range(4):
                local_copy(t).wait()
            for t in range(4):
                n = w8 if t == 0 else r8
                own_all, fwd_all = fulls[t].at[pl.ds(0, 4 * n), :], fulls[t].at[pl.ds(0, 3 * n), :]
                for ref, sem in ((own_all, own_sems.at[t]), (fwd_all, fwd_sems.at[t])):
                    pltpu.make_async_remote_copy(src_ref=ref, dst_ref=ref, send_sem=sem, recv_sem=sem,
                                                 device_id=sibling, device_id_type=MESH).wait_send()

    def head(g):
        return HEADS - 1 - g

    vec = pl.BlockSpec((1, hd), lambda g, i: (0, head(g)))
    hbm = pl.BlockSpec(memory_space=pl.ANY)
    return pl.pallas_call(
        body, name="mixer_fwd", grid=(HEADS, nt),
        in_specs=[pl.BlockSpec((tm, d), lambda g, i: (i, 0)), hbm, hbm, hbm, hbm,
                  vec, vec,
                  pl.BlockSpec((1, CHUNK, CHUNK), lambda g, i: (head(g), 0, 0)),
                  pl.BlockSpec((1, CHUNK, 1), lambda g, i: (head(g), 0, 0)),
                  pl.BlockSpec((3, hd), lambda g, i: (0, head(g))),
                  vec],
        out_specs=[pl.BlockSpec((N_SAVE, tm, hd), lambda g, i: (0, i, head(g))),
                   pl.BlockSpec((tm, hd), lambda g, i: (i, head(g))),
                   pl.BlockSpec((tm, hd), lambda g, i: (i, head(g))),
                   hbm, hbm, hbm, hbm],
        out_shape=[jax.ShapeDtypeStruct((N_SAVE, t_len, d), BF16),
                   jax.ShapeDtypeStruct((t_len, d), BF16),
                   jax.ShapeDtypeStruct((t_len, d), BF16),
                   jax.ShapeDtypeStruct((N_DEV * w8, d), BF16),
                   jax.ShapeDtypeStruct((d, d), BF16), jax.ShapeDtypeStruct((d, d), BF16),
                   jax.ShapeDtypeStruct((d, d), BF16)],
        scratch_shapes=[pltpu.VMEM((SUBLANES + tm, hd), F32), pltpu.VMEM((2, N_MIX, hd, d), BF16),
                        pltpu.SemaphoreType.DMA((n_blocks,)), pltpu.SemaphoreType.DMA((4,)),
                        pltpu.SemaphoreType.DMA((4,)), pltpu.SemaphoreType.DMA((4,)),
                        pltpu.SemaphoreType.DMA((N_MIX,))],
        compiler_params=_cparams(("arbitrary", "arbitrary")),
    )(x, *shards, ln_v_g, ln_v_b, w_s, b_s3, conv_w, conv_b)


def _merge_fwd(x, ya, yb, w_in, w_oa, w_ob, b_gate):
    t_len, d = x.shape
    tm = _tile(t_len, 256)
    tn = _tile(d, 1024)
    nj = d // tn

    def body(x_ref, ya_ref, yb_ref, wga, wgb, woa, wob, bga, bgb, sa_ref, sb_ref, pa_ref, pb_ref, mg_ref):
        xt = x_ref[...].astype(BF16)
        sa = jax.nn.sigmoid(_dot_nt(xt, wga[...]) + bga[...])
        sb = jax.nn.sigmoid(_dot_nt(xt, wgb[...]) + bgb[...])
        pa = _dot(ya_ref[...], woa[...])
        pb = _dot(yb_ref[...], wob[...])
        sa_ref[...] = sa.astype(BF16)
        sb_ref[...] = sb.astype(BF16)
        pa_ref[...] = pa.astype(BF16)
        pb_ref[...] = pb.astype(BF16)
        mg_ref[...] = (sa * pa + sb * pb).astype(BF16)

    row = pl.BlockSpec((tm, d), lambda j, i: (i, 0))
    out = pl.BlockSpec((tm, tn), lambda j, i: (i, j))
    return pl.pallas_call(
        body, name="merge_fwd", grid=(nj, t_len // tm),
        in_specs=[row, row, row,
                  pl.BlockSpec((tn, d), lambda j, i: (7 * nj + j, 0)),
                  pl.BlockSpec((tn, d), lambda j, i: (8 * nj + j, 0)),
                  pl.BlockSpec((d, tn), lambda j, i: (0, j)),
                  pl.BlockSpec((d, tn), lambda j, i: (0, j)),
                  pl.BlockSpec((1, tn), lambda j, i: (0, j)),
                  pl.BlockSpec((1, tn), lambda j, i: (0, nj + j))],
        out_specs=[out] * 5,
        out_shape=[jax.ShapeDtypeStruct((t_len, d), BF16)] * 5,
        compiler_params=_cparams(("parallel", "arbitrary")),
    )(x, ya, yb, w_in, w_in, w_oa, w_ob, b_gate, b_gate)


def _out_ln_loss(merged, w_out, x, target, ln_g, ln_b, sa, sb, pa, pb):
    t_len, d = x.shape
    tm = _tile(t_len, 256)
    nt = t_len // tm

    def body(mg_ref, w_ref, x_ref, t_ref, g_ref, b_ref, sa_ref, sb_ref, pa_ref, pb_ref,
             dz_ref, gx_ref, xb_ref, da_ref, db_ref, dg_ref, glg_ref, glb_ref, ls_ref, ga_ref, gb_ref,
             a_g, a_b, a_l, acc_a, acc_b):
        i = pl.program_id(0)
        xb_ref[...] = x_ref[...].astype(BF16)

        @pl.when(i == 0)
        def _():
            for a in (a_g, a_b, a_l, acc_a, acc_b):
                a[...] = jnp.zeros_like(a)

        zres = DN_ALPHA * x_ref[...] + _dot(mg_ref[...], w_ref[...])
        mu = jnp.mean(zres, axis=-1, keepdims=True)
        dc = zres - mu
        var = jnp.mean(dc * dc, axis=-1, keepdims=True)
        rstd = lax.rsqrt(var + LN_EPS)
        xhat = dc * rstd
        g = g_ref[...]
        err = xhat * g + b_ref[...] - t_ref[...]
        dy = err * (1.0 / d)
        a_l[...] += _fold8(err * err)
        a_g[...] += _fold8(dy * xhat)
        a_b[...] += _fold8(dy)
        dxh = dy * g
        m1 = jnp.mean(dxh, axis=-1, keepdims=True)
        m2 = jnp.mean(dxh * xhat, axis=-1, keepdims=True)
        dz = rstd * (dxh - m1 - xhat * m2)
        dz_bf = dz.astype(BF16)
        dz_ref[...] = dz_bf
        gx_ref[...] = DN_ALPHA * dz

        dm = _dot_nt(dz_bf, w_ref[...])
        sa = sa_ref[...].astype(F32)
        sb = sb_ref[...].astype(F32)
        da = dm * sa
        db = dm * sb
        da_ref[...] = da.astype(BF16)
        db_ref[...] = db.astype(BF16)
        dga = da * pa_ref[...].astype(F32) * (1.0 - sa)
        dgb = db * pb_ref[...].astype(F32) * (1.0 - sb)
        dg_ref[0] = dga.astype(BF16)
        dg_ref[1] = dgb.astype(BF16)
        acc_a[...] += _fold8(dga)
        acc_b[...] += _fold8(dgb)

        @pl.when(i == nt - 1)
        def _():
            glg_ref[...] = jnp.sum(a_g[...], axis=0, keepdims=True)
            glb_ref[...] = jnp.sum(a_b[...], axis=0, keepdims=True)
            ls_ref[...] = jnp.sum(a_l[...], axis=0, keepdims=True)
            ga_ref[...] = jnp.sum(acc_a[...], axis=0, keepdims=True)
            gb_ref[...] = jnp.sum(acc_b[...], axis=0, keepdims=True)

    row = pl.BlockSpec((tm, d), lambda i: (i, 0))
    vec = pl.BlockSpec((1, d), lambda i: (0, 0))
    act = jax.ShapeDtypeStruct((t_len, d), BF16)
    part = jax.ShapeDtypeStruct((1, d), F32)
    return pl.pallas_call(
        body, name="out_ln_loss", grid=(nt,),
        in_specs=[row, pl.BlockSpec((d, d), lambda i: (0, 0)), row, row, vec, vec, row, row, row, row],
        out_specs=[row, row, row, row, row, pl.BlockSpec((2, tm, d), lambda i: (0, i, 0)), vec, vec, vec, vec, vec],
        out_shape=[act, jax.ShapeDtypeStruct((t_len, d), F32), act, act, act,
                   jax.ShapeDtypeStruct((2, t_len, d), BF16), part, part, part, part, part],
        scratch_shapes=[pltpu.VMEM((SUBLANES, d), F32)] * 5,
        compiler_params=_cparams(("arbitrary",)),
    )(merged, w_out, x, target, ln_g, ln_b, sa, sb, pa, pb)


def _grad_w(a, b, name, exchange=None):
    t_len, m = a.shape
    n = b.shape[1]
    tm, tn, tk = _tile(m, MM_TILE), _tile(n, MM_TILE), _tile(t_len, MM_DEPTH_TOKENS)
    ni, nj, nk = m // tm, n // tn, t_len // tk

    def body(a_ref, b_ref, *rest):
        if exchange is None:
            o_ref, acc = rest
        else:
            src, o_ref, land, acc, send_sems, recv_sems = rest
        i, j, k = pl.program_id(0), pl.program_id(1), pl.program_id(2)

        def copies():
            x, y, c = lax.axis_index("x"), lax.axis_index("y"), lax.axis_index("c")
            w = exchange.shape[1] // N_DEV
            return [pltpu.make_async_remote_copy(
                src_ref=_shard_ref(src, "col", 2 * s + 1 - c, w), dst_ref=land.at[s],
                send_sem=send_sems.at[s], recv_sem=recv_sems.at[s],
                device_id=(x, y, 1 - c), device_id_type=MESH) for s in range(N_CHIP)]

        if exchange is not None:
            @pl.when((i == 0) & (j == 0) & (k == 0))
            def _():
                for cp in copies():
                    cp.start()

        @pl.when(k == 0)
        def _():
            acc[...] = jnp.zeros_like(acc)

        acc[...] += _dot_tn(a_ref[...], b_ref[...])

        @pl.when(k == nk - 1)
        def _():
            o_ref[...] = acc[...].astype(BF16)

        if exchange is not None:
            @pl.when((i == ni - 1) & (j == nj - 1) & (k == nk - 1))
            def _():
                for cp in copies():
                    cp.wait()

    in_specs = [pl.BlockSpec((tk, tm), lambda i, j, k: (k, i)), pl.BlockSpec((tk, tn), lambda i, j, k: (k, j))]
    out_specs = [pl.BlockSpec((tm, tn), lambda i, j, k: (i, j))]
    out_shape = [jax.ShapeDtypeStruct((m, n), BF16)]
    scratch = [pltpu.VMEM((tm, tn), F32)]
    args = [a, b]
    if exchange is not None:
        hbm = pl.BlockSpec(memory_space=pl.ANY)
        in_specs.append(hbm)
        out_specs.append(hbm)
        out_shape.append(jax.ShapeDtypeStruct((N_CHIP, exchange.shape[0], exchange.shape[1] // N_DEV), exchange.dtype))
        scratch += [pltpu.SemaphoreType.DMA((N_CHIP,)), pltpu.SemaphoreType.DMA((N_CHIP,))]
        args.append(exchange)
    outs = pl.pallas_call(
        body, name=name, grid=(ni, nj, nk),
        in_specs=in_specs, out_specs=out_specs, out_shape=out_shape, scratch_shapes=scratch,
        compiler_params=_cparams(("arbitrary", "arbitrary", "arbitrary")),
    )(*args)
    return outs[0] if exchange is None else (outs[0], outs[1])


def _mixer_bwd(da, db, w_oa, w_ob, saved, dgate, ln_v_g, ln_v_b, w_s, b_s3, conv_w):
    t_len, d = da.shape
    hd = d // HEADS
    tm = _tile(t_len, 512)
    nt = t_len // tm

    def body(da_ref, db_ref, woa, wob, sv, dgt, lng, lnb, ws_ref, bs_ref, cw_ref,
             dp_ref, gws_ref, gbs_ref, glg_ref, glb_ref, gcw_ref, gcb_ref,
             dbuf, carry, a_ws, a_bs, a_lg, a_lb, a_c0, a_c1, a_c2, a_cb):
        i = pl.program_id(1)

        @pl.when(i == 0)
        def _():
            carry[...] = jnp.zeros_like(carry)
            for a in (a_ws, a_bs, a_lg, a_lb, a_c0, a_c1, a_c2, a_cb):
                a[...] = jnp.zeros_like(a)

        dya_t = _dot_nt(da_ref[...], woa[...])
        dyb_t = _dot_nt(db_ref[...], wob[...])
        g = lng[...]
        beta = lnb[...]
        wm = jnp.where(_tril_mask(), ws_ref[0], 0.0).astype(BF16)
        bs = bs_ref[0]
        w0, w1, w2 = cw_ref[0:1, :], cw_ref[1:2, :], cw_ref[2:3, :]
        dbuf[tm:tm + SUBLANES, :] = carry[...]
        gws = a_ws[...]
        gbs = a_bs[...]
        lg, lb = a_lg[...], a_lb[...]
        c0, c1, c2, cb = a_c0[...], a_c1[...], a_c2[...], a_cb[...]
        for ck in reversed(range(tm // CHUNK)):
            r0 = ck * CHUNK
            rows = slice(r0, r0 + CHUNK)
            dya = dya_t[rows]
            u = sv[0, rows, :].astype(F32)
            v = sv[1, rows, :].astype(F32)
            z = sv[2, rows, :].astype(F32)
            gu, gu_grad = _gelu(u)
            gv, gv_grad = _gelu(v)
            mu = jnp.mean(gv, axis=-1, keepdims=True)
            dvc = gv - mu
            var = jnp.mean(dvc * dvc, axis=-1, keepdims=True)
            rstd = lax.rsqrt(var + LN_EPS)
            vhat = dvc * rstd
            vn = (vhat * g + beta).astype(BF16)
            sz, sz_grad = _silu(z)
            t1 = dya * sz
            dmixed = t1 * gu
            dmixed_b = dmixed.astype(BF16)
            mixed = _dot(wm, vn) + bs
            gws = gws + _dot_nt(dmixed_b, vn)
            gbs = gbs + dmixed
            dvn = _dot_tn(wm, dmixed_b)
            dp_ref[0, rows, :] = (t1 * mixed * gu_grad).astype(BF16)
            dp_ref[2, rows, :] = (dya * gu * mixed * sz_grad).astype(BF16)
            lg = lg + _fold8(dvn * vhat)
            lb = lb + _fold8(dvn)
            dvh = dvn * g
            m1 = jnp.mean(dvh, axis=-1, keepdims=True)
            m2 = jnp.mean(dvh * vhat, axis=-1, keepdims=True)
            dp_ref[1, rows, :] = (rstd * (dvh - m1 - vhat * m2) * gv_grad).astype(BF16)
            dyb = dyb_t[rows]
            xbv = sv[3, rows, :].astype(F32)
            cbv = sv[4, rows, :].astype(F32)
            bbv = sv[5, rows, :].astype(F32)
            zbv = sv[6, rows, :].astype(F32)
            conv = sv[7, rows, :].astype(F32)
            szb, szb_grad = _silu(zbv)
            dp_ref[5, rows, :] = (dyb * conv * szb).astype(BF16)
            dp_ref[6, rows, :] = (dyb * bbv * conv * szb_grad).astype(BF16)
            dconv = dyb * bbv * szb
            dbuf[r0:r0 + CHUNK, :] = dconv
            dc1 = dbuf[r0 + 1:r0 + 1 + CHUNK, :]
            dc2 = dbuf[r0 + 2:r0 + 2 + CHUNK, :]
            if ck == 0:
                carry[...] = dconv[0:SUBLANES, :]
            h = cbv * xbv
            c2 = c2 + _fold8(dconv * h)
            c1 = c1 + _fold8(dc1 * h)
            c0 = c0 + _fold8(dc2 * h)
            cb = cb + _fold8(dconv)
            dh = w2 * dconv + w1 * dc1 + w0 * dc2
            dp_ref[3, rows, :] = (dh * cbv).astype(BF16)
            dp_ref[4, rows, :] = (dh * xbv).astype(BF16)
            dp_ref[7, rows, :] = dgt[0, rows, :]
            dp_ref[8, rows, :] = dgt[1, rows, :]
        a_ws[...] = gws
        a_bs[...] = gbs
        a_lg[...], a_lb[...] = lg, lb
        a_c0[...], a_c1[...], a_c2[...], a_cb[...] = c0, c1, c2, cb

        @pl.when(i == nt - 1)
        def _():
            gws_ref[0] = jnp.where(_tril_mask(), a_ws[...], 0.0)
            gbs_ref[0] = jnp.sum(a_bs[...], axis=1, keepdims=True)
            glg_ref[...] = jnp.sum(a_lg[...], axis=0, keepdims=True)
            glb_ref[...] = jnp.sum(a_lb[...], axis=0, keepdims=True)
            gcw_ref[0:1, :] = jnp.sum(a_c0[...], axis=0, keepdims=True)
            gcw_ref[1:2, :] = jnp.sum(a_c1[...], axis=0, keepdims=True)
            gcw_ref[2:3, :] = jnp.sum(a_c2[...], axis=0, keepdims=True)
            gcb_ref[...] = jnp.sum(a_cb[...], axis=0, keepdims=True)

    def rev(i):
        return nt - 1 - i

    row = pl.BlockSpec((tm, d), lambda g, i: (rev(i), 0))
    wrow = pl.BlockSpec((hd, d), lambda g, i: (g, 0))
    vec = pl.BlockSpec((1, hd), lambda g, i: (0, g))
    acc8 = pltpu.VMEM((SUBLANES, hd), F32)
    return pl.pallas_call(
        body, name="mixer_bwd", grid=(HEADS, nt),
        in_specs=[row, row, wrow, wrow,
                  pl.BlockSpec((N_SAVE, tm, hd), lambda g, i: (0, rev(i), g)),
                  pl.BlockSpec((2, tm, hd), lambda g, i: (0, rev(i), g)),
                  vec, vec,
                  pl.BlockSpec((1, CHUNK, CHUNK), lambda g, i: (g, 0, 0)),
                  pl.BlockSpec((1, CHUNK, 1), lambda g, i: (g, 0, 0)),
                  pl.BlockSpec((3, hd), lambda g, i: (0, g))],
        out_specs=[pl.BlockSpec((N_IN, tm, hd), lambda g, i: (0, rev(i), g)),
                   pl.BlockSpec((1, CHUNK, CHUNK), lambda g, i: (g, 0, 0)),
                   pl.BlockSpec((1, CHUNK, 1), lambda g, i: (g, 0, 0)),
                   vec, vec,
                   pl.BlockSpec((3, hd), lambda g, i: (0, g)),
                   vec],
        out_shape=[jax.ShapeDtypeStruct((N_IN, t_len, d), BF16),
                   jax.ShapeDtypeStruct((HEADS, CHUNK, CHUNK), F32),
                   jax.ShapeDtypeStruct((HEADS, CHUNK, 1), F32),
                   jax.ShapeDtypeStruct((1, d), F32), jax.ShapeDtypeStruct((1, d), F32),
                   jax.ShapeDtypeStruct((3, d), F32), jax.ShapeDtypeStruct((1, d), F32)],
        scratch_shapes=[pltpu.VMEM((tm + SUBLANES, hd), F32), acc8,
                        pltpu.VMEM((CHUNK, CHUNK), F32), pltpu.VMEM((CHUNK, hd), F32),
                        acc8, acc8, acc8, acc8, acc8, acc8],
        compiler_params=_cparams(("parallel", "arbitrary")),
    )(da, db, w_oa, w_ob, saved, dgate, ln_v_g, ln_v_b, w_s, b_s3, conv_w)


def _grad_w_in(xb, dp):
    t_len, d = xb.shape
    tm, tn, tk = _tile(d, MM_TILE), _tile(d, MM_TILE), _tile(t_len, MM_DEPTH_TOKENS)
    nj = d // tn
    nk = t_len // tk

    def body(a_ref, b_ref, o_ref, acc):
        k = pl.program_id(2)

        @pl.when(k == 0)
        def _():
            acc[...] = jnp.zeros_like(acc)

        acc[...] += _dot_tn(a_ref[...], b_ref[...])

        @pl.when(k == nk - 1)
        def _():
            o_ref[...] = acc[...].astype(BF16)

    return pl.pallas_call(
        body, name="grad_w_in", grid=(d // tm, N_IN * nj, nk),
        in_specs=[pl.BlockSpec((tk, tm), lambda i, j, k: (k, i)),
                  pl.BlockSpec((None, tk, tn), lambda i, j, k: (j // nj, k, j % nj))],
        out_specs=pl.BlockSpec((tm, tn), lambda i, j, k: (i, j)),
        out_shape=jax.ShapeDtypeStruct((d, N_IN * d), BF16),
        scratch_shapes=[pltpu.VMEM((tm, tn), F32)],
        compiler_params=_cparams(("parallel", "parallel", "arbitrary")),
    )(xb, dp)


def _grad_x_tail(dp, w_in, partial, slots):
    _, t_len, d = dp.shape
    lo, hi = slots
    tm, tn, tk = _tile(t_len, MM_TILE), _tile(d, MM_TILE), _tile(d, MM_DEPTH)
    nkb = d // tk
    nk = (hi - lo) * nkb

    def body(a_ref, b_ref, r_ref, o_ref, acc):
        k = pl.program_id(2)

        @pl.when(k == 0)
        def _():
            acc[...] = r_ref[...]

        acc[...] += _dot(a_ref[...], b_ref[...])

        @pl.when(k == nk - 1)
        def _():
            o_ref[...] = acc[...]

    return pl.pallas_call(
        body, name="grad_x_tail", grid=(t_len // tm, d // tn, nk),
        in_specs=[pl.BlockSpec((None, tm, tk), lambda i, j, k: (lo + k // nkb, i, k % nkb)),
                  pl.BlockSpec((tk, tn), lambda i, j, k: (lo * nkb + k, j)),
                  pl.BlockSpec((tm, tn), lambda i, j, k: (i, j))],
        out_specs=pl.BlockSpec((tm, tn), lambda i, j, k: (i, j)),
        out_shape=jax.ShapeDtypeStruct((t_len, d), F32),
        scratch_shapes=[pltpu.VMEM((tm, tn), F32)],
        compiler_params=_cparams(("parallel", "parallel", "arbitrary")),
    )(dp, w_in, partial)


def _grad_x(dp, w_in, gx_direct, parts, packed, slots):
    _, t_len, d = dp.shape
    tm, tn, tk = _tile(t_len, MM_TILE), _tile(d, MM_TILE), _tile(d, MM_DEPTH)
    nkb = d // tk
    lo, hi = slots
    nk = (hi - lo) * nkb
    ni, nj = t_len // tm, d // tn
    n_parts = len(parts)

    def body(a_ref, b_ref, r_ref, *rest):
        srcs = rest[:n_parts]
        pk_ref = rest[n_parts]
        o_ref = rest[n_parts + 1]
        lands = rest[n_parts + 2:2 * n_parts + 2]
        gath = rest[2 * n_parts + 2]
        acc, send_sems, recv_sems, pk_send, pk_recv, pk_local = rest[2 * n_parts + 3:]
        i, j, k = pl.program_id(0), pl.program_id(1), pl.program_id(2)
        x, y, c = lax.axis_index("x"), lax.axis_index("y"), lax.axis_index("c")
        my_chip = 2 * x + y
        me = 4 * x + 2 * y + c
        chips = [(1 - x, y), (x, 1 - y), (1 - x, 1 - y)]

        def part_copy(t, n):
            px, py = chips[n]
            return pltpu.make_async_remote_copy(
                src_ref=srcs[t].at[2 * px + py], dst_ref=lands[t].at[my_chip],
                send_sem=send_sems.at[3 * t + n], recv_sem=recv_sems.at[3 * t + n],
                device_id=(px, py, c), device_id_type=MESH)

        def part_landing(t, n):
            px, py = chips[n]
            return pltpu.make_async_remote_copy(
                src_ref=srcs[t].at[my_chip], dst_ref=lands[t].at[2 * px + py],
                send_sem=send_sems.at[3 * t + n], recv_sem=recv_sems.at[3 * t + n],
                device_id=(px, py, c), device_id_type=MESH)

        def pk_copy(s):
            return pltpu.make_async_remote_copy(
                src_ref=pk_ref, dst_ref=gath.at[me], send_sem=pk_send, recv_sem=pk_recv.at[me],
                device_id=(s // 4, (s // 2) % 2, s % 2), device_id_type=MESH)

        @pl.when((i == 0) & (j == 0) & (k == 0))
        def _():
            for t in range(n_parts):
                for n in range(3):
                    part_copy(t, n).start()
            pltpu.make_async_copy(pk_ref, gath.at[me], pk_local).start()
            for s in range(N_DEV):
                @pl.when(s != me)
                def _(s=s):
                    pk_copy(s).start()

        @pl.when(k == 0)
        def _():
            acc[...] = r_ref[...]

        acc[...] += _dot(a_ref[...], b_ref[...])

        @pl.when(k == nk - 1)
        def _():
            o_ref[...] = acc[...]

        @pl.when((i == ni - 1) & (j == nj - 1) & (k == nk - 1))
        def _():
            for t in range(n_parts):
                for n in range(3):
                    part_landing(t, n).wait_recv()
            for t in range(n_parts):
                for n in range(3):
                    part_copy(t, n).wait_send()
            for s in range(N_DEV):
                @pl.when(s != me)
                def _(s=s):
                    pltpu.make_async_remote_copy(
                        src_ref=pk_ref, dst_ref=gath.at[s], send_sem=pk_send, recv_sem=pk_recv.at[s],
                        device_id=(s // 4, (s // 2) % 2, s % 2), device_id_type=MESH).wait_recv()
            seven = gath.at[pl.ds(0, N_DEV - 1)]
            pltpu.make_async_remote_copy(src_ref=seven, dst_ref=seven, send_sem=pk_send, recv_sem=pk_send,
                                         device_id=(x, y, 1 - c), device_id_type=MESH).wait_send()
            pltpu.make_async_copy(pk_ref, gath.at[me], pk_local).wait()

    hbm = pl.BlockSpec(memory_space=pl.ANY)
    outs = pl.pallas_call(
        body, name="grad_x", grid=(ni, nj, nk),
        in_specs=[pl.BlockSpec((None, tm, tk), lambda i, j, k: (lo + k // nkb, i, k % nkb)),
                  pl.BlockSpec((tk, tn), lambda i, j, k: (lo * nkb + k, j)),
                  pl.BlockSpec((tm, tn), lambda i, j, k: (i, j))] + [hbm] * (n_parts + 1),
        out_specs=[pl.BlockSpec((tm, tn), lambda i, j, k: (i, j))] + [hbm] * (n_parts + 1),
        out_shape=[jax.ShapeDtypeStruct((t_len, d), F32)] + [jax.ShapeDtypeStruct(p.shape, p.dtype) for p in parts]
        + [jax.ShapeDtypeStruct((N_DEV,) + packed.shape, packed.dtype)],
        scratch_shapes=[pltpu.VMEM((tm, tn), F32),
                        pltpu.SemaphoreType.DMA((3 * n_parts,)), pltpu.SemaphoreType.DMA((3 * n_parts,)),
                        pltpu.SemaphoreType.DMA(()), pltpu.SemaphoreType.DMA((N_DEV,)), pltpu.SemaphoreType.DMA(())],
        compiler_params=_cparams(("arbitrary", "arbitrary", "arbitrary")),
    )(dp, w_in, gx_direct, *parts, packed)
    return outs[0], list(outs[1:1 + n_parts]), outs[1 + n_parts]


def kernel(x, w_in, b_gate, ln_v_g, ln_v_b, w_s, b_s, conv_w, conv_b, w_oa, w_ob, w_out, ln_g, ln_b, loss_target, m_w_in, m_b_gate, m_ln_v_g, m_ln_v_b, m_w_s, m_b_s, m_conv_w, m_conv_b, m_w_oa, m_w_ob, m_w_out, m_ln_g, m_ln_b, v_w_in, v_b_gate, v_ln_v_g, v_ln_v_b, v_w_s, v_b_s, v_conv_w, v_conv_b, v_w_oa, v_w_ob, v_w_out, v_ln_g, v_ln_b):
    _, t_len, d = x.shape
    assert d % (HEADS * 128) == 0 and t_len % CHUNK == 0 and w_in.shape[2] * N_DEV == N_IN * d
    x2 = x[0]
    tgt2 = loss_target[0]
    c_arr = lax.axis_index("c").astype(jnp.int32).reshape(1)
    chip_arr = (2 * lax.axis_index("x") + lax.axis_index("y")).astype(jnp.int32).reshape(1)
    dev = 4 * lax.axis_index("x") + 2 * lax.axis_index("y") + lax.axis_index("c")

    shards = [_cast_transposed(w_in[0], "cast_w_in"), _cast_bf16(w_oa[0], "cast_w_oa"),
              _cast_bf16(w_ob[0], "cast_w_ob"), _cast_bf16(w_out[0], "cast_w_out")]
    (conv_w_g,) = _all_gather([conv_w[0]], ["lead"], "gather_conv_w", vmem=True)
    conv_w_f = jnp.transpose(conv_w_g, (1, 0, 2)).reshape(3, d)
    w_s3 = w_s[0]
    b_s3 = b_s[0].reshape(HEADS, CHUNK, 1)

    saved, ya, yb, w_in_f, w_oa_f, w_ob_f, w_out_f = _mixer_fwd(
        x2, shards, ln_v_g, ln_v_b, w_s3, b_s3, conv_w_f, conv_b)
    sa, sb, pa, pb, merged = _merge_fwd(x2, ya, yb, w_in_f, w_oa_f, w_ob_f, b_gate)
    dz, gx_direct, xb, da, db, dgate, g_ln_g, g_ln_b, err2, g_bga, g_bgb = _out_ln_loss(
        merged, w_out_f, x2, tgt2, ln_g, ln_b, sa, sb, pa, pb)
    loss = lax.psum(0.5 * jnp.sum(err2) / d, ("x", "y", "c"))

    dp, g_ws, g_bs, g_lvg, g_lvb, g_cw, g_cb = _mixer_bwd(
        da, db, w_oa_f, w_ob_f, saved, dgate, ln_v_g, ln_v_b, w_s3, b_s3, conv_w_f)
    gw_in = _grad_w_in(xb, dp)

    gw_out, land_in = _grad_w(merged, dz, "grad_w_out", exchange=gw_in)
    part_in = _pair_sum(gw_in, land_in, "col", c_arr, "grad_pair_sum_0")
    gw_oa = _grad_w(ya, da, "grad_w_oa")
    gw_ob = _grad_w(yb, db, "grad_w_ob")
    rows = [gw_oa, gw_ob, gw_out]
    lands = _pair_exchange(rows, ["row"] * 3, "grad_pair_exchange")
    parts = [part_in] + [_pair_sum(g, l, "row", c_arr, "grad_pair_sum_%d" % (n + 1))
                         for n, (g, l) in enumerate(zip(rows, lands))]
    pieces = [jnp.concatenate([g_bga, g_bgb], axis=1), g_lvg, g_lvb, g_ws, g_bs, g_cw, g_cb, g_ln_g, g_ln_b]
    sizes = [p.size for p in pieces]
    packed = jnp.concatenate([p.reshape(-1, 128) for p in pieces], axis=0)
    gx_part, lands2, gathered = _grad_x(dp, w_in_f, gx_direct, parts, packed, (0, GRAD_X_COMM_BLOCKS))
    grad_x = _grad_x_tail(dp, w_in_f, gx_part, (GRAD_X_COMM_BLOCKS, N_IN))[None]

    big = []
    for n, (w, m, v) in enumerate([(w_in, m_w_in, v_w_in), (w_oa, m_w_oa, v_w_oa), (w_ob, m_w_ob, v_w_ob),
                                   (w_out, m_w_out, v_w_out)]):
        big.append([o[None] for o in _sum_adam(parts[n], lands2[n], w[0], m[0], v[0], chip_arr, "sum_adam_%d" % n)])
    (g_w_in, d_w_in, nm_w_in, nv_w_in), (g_w_oa, d_w_oa, nm_w_oa, nv_w_oa), \
        (g_w_ob, d_w_ob, nm_w_ob, nv_w_ob), (g_w_out, d_w_out, nm_w_out, nv_w_out) = big

    total = _small_sum(gathered, "sum_small_grads")
    offs = [0]
    for s in sizes:
        offs.append(offs[-1] + s // 128)
    unpacked = [total[offs[n]:offs[n + 1]] for n in range(len(pieces))]
    g_b_gate = unpacked[0].reshape(b_gate.shape)
    g_ln_v_g = unpacked[1].reshape(ln_v_g.shape)
    g_ln_v_b = unpacked[2].reshape(ln_v_b.shape)
    g_w_s = unpacked[3].reshape(w_s.shape)
    g_b_s = unpacked[4].reshape(b_s.shape)
    g_conv_w = lax.dynamic_slice_in_dim(unpacked[5].reshape(3, d), dev * (d // N_DEV), d // N_DEV, axis=1)[None]
    g_conv_b = unpacked[6].reshape(conv_b.shape)
    g_ln_g2 = unpacked[7].reshape(ln_g.shape)
    g_ln_b2 = unpacked[8].reshape(ln_b.shape)

    small_w = [b_gate, ln_v_g, ln_v_b, w_s, b_s, conv_w, conv_b, ln_g, ln_b]
    small_g = [g_b_gate, g_ln_v_g, g_ln_v_b, g_w_s, g_b_s, g_conv_w, g_conv_b, g_ln_g2, g_ln_b2]
    small_m = [m_b_gate, m_ln_v_g, m_ln_v_b, m_w_s, m_b_s, m_conv_w, m_conv_b, m_ln_g, m_ln_b]
    small_v = [v_b_gate, v_ln_v_g, v_ln_v_b, v_w_s, v_b_s, v_conv_w, v_conv_b, v_ln_g, v_ln_b]

    def flat(a):
        return a.reshape(-1, a.shape[-1])

    res = _small_adam([flat(a) for a in small_w], [flat(a) for a in small_g], [flat(a) for a in small_m],
                      [flat(a) for a in small_v], "adam_small")
    ns = len(small_w)
    d_s = [res[n].reshape(small_w[n].shape) for n in range(ns)]
    nm_s = [res[ns + n].reshape(small_w[n].shape) for n in range(ns)]
    nv_s = [res[2 * ns + n].reshape(small_w[n].shape) for n in range(ns)]

    def ordered(first, small, oa, ob, out):
        return [first] + small[:7] + [oa, ob, out] + small[7:]

    return (loss, grad_x,
            *ordered(g_w_in, small_g, g_w_oa, g_w_ob, g_w_out),
            *ordered(d_w_in, d_s, d_w_oa, d_w_ob, d_w_out),
            *ordered(nm_w_in, nm_s, nm_w_oa, nm_w_ob, nm_w_out),
            *ordered(nv_w_in, nv_s, nv_w_oa, nv_w_ob, nv_w_out))
```

```python
import functools

import jax
import jax.numpy as jnp
from jax import lax
from jax.experimental import pallas as pl
from jax.experimental.pallas import tpu as pltpu

F32 = jnp.float32
BF16 = jnp.bfloat16
MESH = pl.DeviceIdType.MESH

N_DEV = 8
N_CHIP = 4
HEADS = 8
CHUNK = 128
N_IN = 9
N_MIX = 7
N_SAVE = 8
LN_EPS = 1e-5
DN_ALPHA = 2.0 ** 0.25
ADAM_LR = 0.001
ADAM_B1 = 0.9
ADAM_B2 = 0.999
ADAM_EPS = 1e-08
ADAM_WD = 0.01
ADAM_STEP = 10
GELU_C0 = 0.7978845608028654
GELU_C1 = 0.044715
SUBLANES = 8
FIRST_SENDS = 3
SENDS_PER_GROUP = 2
MM_TILE = 1024
MM_DEPTH = 2048
MM_DEPTH_TOKENS = 4096
GRAD_X_COMM_BLOCKS = 8
VMEM_LIMIT = 56 << 20


def _cparams(sem):
    return pltpu.CompilerParams(dimension_semantics=sem, vmem_limit_bytes=VMEM_LIMIT)


def _tile(n, want):
    t = min(n, want)
    while n % t:
        t //= 2
    return t


def _gelu(u):
    u2 = u * u
    t = jnp.tanh(u * (GELU_C0 + (GELU_C0 * GELU_C1) * u2))
    hp = 0.5 * t + 0.5
    grad = hp + (0.5 * u) * (1.0 - t * t) * (GELU_C0 + (3.0 * GELU_C0 * GELU_C1) * u2)
    return u * hp, grad


def _silu(z):
    s = jax.nn.sigmoid(z)
    sil = z * s
    return sil, s + sil * (1.0 - s)


def _fold8(a):
    return jnp.sum(a.reshape(a.shape[0] // SUBLANES, SUBLANES, a.shape[1]), axis=0)


def _dot(a, b):
    return jnp.dot(a, b, preferred_element_type=F32)


def _dot_nt(a, b):
    return lax.dot_general(a, b, (((1,), (1,)), ((), ())), preferred_element_type=F32)


def _dot_tn(a, b):
    return lax.dot_general(a, b, (((0,), (0,)), ((), ())), preferred_element_type=F32)


def _tril_mask():
    r = lax.broadcasted_iota(jnp.int32, (CHUNK, CHUNK), 0)
    c = lax.broadcasted_iota(jnp.int32, (CHUNK, CHUNK), 1)
    return c <= r


def _cast_bf16(a, name):
    rows, cols = a.shape
    rb = _tile(rows, 256)

    def body(a_ref, o_ref):
        o_ref[...] = a_ref[...].astype(BF16)

    return pl.pallas_call(
        body, name=name, grid=(rows // rb,),
        in_specs=[pl.BlockSpec((rb, cols), lambda i: (i, 0))],
        out_specs=pl.BlockSpec((rb, cols), lambda i: (i, 0)),
        out_shape=jax.ShapeDtypeStruct((rows, cols), BF16),
        compiler_params=_cparams(("parallel",)),
    )(a)


def _cast_transposed(a, name):
    rows, cols = a.shape
    tb = _tile(rows, 2048)
    tc = _tile(cols, 256)

    def body(a_ref, o_ref):
        o_ref[...] = a_ref[...].T.astype(BF16)

    return pl.pallas_call(
        body, name=name, grid=(cols // tc, rows // tb),
        in_specs=[pl.BlockSpec((tb, tc), lambda j, i: (i, j))],
        out_specs=pl.BlockSpec((tc, tb), lambda j, i: (j, i)),
        out_shape=jax.ShapeDtypeStruct((cols, rows), BF16),
        compiler_params=_cparams(("parallel", "parallel")),
    )(a)


def _shard_ref(full, kind, s, n):
    if kind == "col":
        return full.at[:, pl.ds(pl.multiple_of(s * n, 128), n)]
    if kind == "row":
        return full.at[pl.ds(pl.multiple_of(s * n, SUBLANES), n), :]
    return full.at[s]


def _all_gather(shards, kinds, name, vmem):
    nt = len(shards)
    out_shapes = []
    for a, kind in zip(shards, kinds):
        if kind == "col":
            out_shapes.append(jax.ShapeDtypeStruct((a.shape[0], N_DEV * a.shape[1]), a.dtype))
        elif kind == "row":
            out_shapes.append(jax.ShapeDtypeStruct((N_DEV * a.shape[0], a.shape[1]), a.dtype))
        else:
            out_shapes.append(jax.ShapeDtypeStruct((N_DEV,) + a.shape, a.dtype))

    def body(*refs):
        srcs, fulls = refs[:nt], refs[nt:2 * nt]
        send_sems, recv_sems, local_sems = refs[2 * nt:]
        x, y, c = lax.axis_index("x"), lax.axis_index("y"), lax.axis_index("c")
        sibling = (x, y, 1 - c)
        chips = [(1 - x, y), (x, 1 - y), (1 - x, 1 - y)]

        def dev(px, py, pc):
            return 4 * px + 2 * py + pc

        def region(t, s):
            a, kind = shards[t], kinds[t]
            n = a.shape[1] if kind == "col" else a.shape[0]
            return _shard_ref(fulls[t], kind, s, n)

        def copy(t, k, block, to, own=False):
            return pltpu.make_async_remote_copy(
                src_ref=srcs[t] if own else region(t, block), dst_ref=region(t, block),
                send_sem=send_sems.at[7 * t + k], recv_sem=recv_sems.at[7 * t + k],
                device_id=to, device_id_type=MESH)

        me = dev(x, y, c)
        started = []
        for t in range(nt):
            mine = pltpu.make_async_copy(srcs[t], region(t, me), local_sems.at[t])
            mine.start()
            started.append(mine)
        first = []
        for t in range(nt):
            first.append(copy(t, 0, me, sibling, own=True))
            for j, chip in enumerate(chips):
                first.append(copy(t, 1 + j, me, (*chip, c), own=True))
        for cp in first:
            cp.start()
        passed = []
        for t in range(nt):
            for j, chip in enumerate(chips):
                blk = dev(*chip, c)
                copy(t, 1 + j, blk, sibling).wait_recv()
                fwd = copy(t, 4 + j, blk, sibling)
                fwd.start()
                passed.append(fwd)
        for t in range(nt):
            copy(t, 0, dev(x, y, 1 - c), sibling).wait_recv()
            for j, chip in enumerate(chips):
                copy(t, 4 + j, dev(*chip, 1 - c), sibling).wait_recv()
        for cp in first + passed:
            cp.wait_send()
        for mine in started:
            mine.wait()

    space = pltpu.VMEM if vmem else pl.ANY
    return pl.pallas_call(
        body, name=name,
        in_specs=[pl.BlockSpec(memory_space=space)] * nt,
        out_specs=[pl.BlockSpec(memory_space=space)] * nt,
        out_shape=out_shapes,
        scratch_shapes=[pltpu.SemaphoreType.DMA((7 * nt,)), pltpu.SemaphoreType.DMA((7 * nt,)),
                        pltpu.SemaphoreType.DMA((nt,))],
        compiler_params=pltpu.CompilerParams(vmem_limit_bytes=VMEM_LIMIT, has_side_effects=True),
    )(*shards)


def _pair_exchange(grads, kinds, name):
    nt = len(grads)
    shard_shapes = []
    for g, kind in zip(grads, kinds):
        shard_shapes.append((g.shape[0], g.shape[1] // N_DEV) if kind == "col" else (g.shape[0] // N_DEV, g.shape[1]))

    def body(*refs):
        srcs, lands = refs[:nt], refs[nt:2 * nt]
        send_sems, recv_sems = refs[2 * nt:]
        x, y, c = lax.axis_index("x"), lax.axis_index("y"), lax.axis_index("c")
        copies = []
        for t in range(nt):
            n = shard_shapes[t][1] if kinds[t] == "col" else shard_shapes[t][0]
            for k in range(N_CHIP):
                cp = pltpu.make_async_remote_copy(
                    src_ref=_shard_ref(srcs[t], kinds[t], 2 * k + 1 - c, n), dst_ref=lands[t].at[k],
                    send_sem=send_sems.at[N_CHIP * t + k], recv_sem=recv_sems.at[N_CHIP * t + k],
                    device_id=(x, y, 1 - c), device_id_type=MESH)
                cp.start()
                copies.append(cp)
        for cp in copies:
            cp.wait()

    return pl.pallas_call(
        body, name=name,
        in_specs=[pl.BlockSpec(memory_space=pl.ANY)] * nt,
        out_specs=[pl.BlockSpec(memory_space=pl.ANY)] * nt,
        out_shape=[jax.ShapeDtypeStruct((N_CHIP,) + s, g.dtype) for s, g in zip(shard_shapes, grads)],
        scratch_shapes=[pltpu.SemaphoreType.DMA((N_CHIP * nt,)), pltpu.SemaphoreType.DMA((N_CHIP * nt,))],
        compiler_params=pltpu.CompilerParams(has_side_effects=True),
    )(*grads)


def _pair_sum(grad, land, kind, c_arr, name):
    _, r, w = land.shape
    rb = _tile(r, 1024)
    nrb = r // rb

    def body(c_ref, g_ref, l_ref, o_ref):
        o_ref[...] = (g_ref[...].astype(F32) + l_ref[...].astype(F32)).astype(o_ref.dtype)

    if kind == "col":
        g_spec = pl.BlockSpec((rb, w), lambda k, i, c: (i, 2 * k + c[0]))
    else:
        g_spec = pl.BlockSpec((rb, w), lambda k, i, c: ((2 * k + c[0]) * nrb + i, 0))
    return pl.pallas_call(
        body, name=name,
        grid_spec=pltpu.PrefetchScalarGridSpec(
            num_scalar_prefetch=1, grid=(N_CHIP, nrb),
            in_specs=[g_spec, pl.BlockSpec((None, rb, w), lambda k, i, c: (k, i, 0))],
            out_specs=pl.BlockSpec((None, rb, w), lambda k, i, c: (k, i, 0))),
        out_shape=jax.ShapeDtypeStruct(land.shape, BF16),
        compiler_params=_cparams(("parallel", "parallel")),
    )(c_arr, grad, land)


def _adam(w, g, m, v):
    m = ADAM_B1 * m + (1.0 - ADAM_B1) * g
    v = ADAM_B2 * v + (1.0 - ADAM_B2) * jnp.square(g)
    m_hat = m / (1.0 - ADAM_B1 ** ADAM_STEP)
    v_hat = v / (1.0 - ADAM_B2 ** ADAM_STEP)
    delta = -ADAM_LR * (m_hat / (jnp.sqrt(v_hat) + ADAM_EPS) + ADAM_WD * w)
    return delta, m, v


def _sum_adam(part, land, w, m, v, chip_arr, name):
    r, wd = w.shape
    rb = _tile(r, 256)

    def body(k_ref, own, r1, r2, r3, w_ref, m_ref, v_ref, g_out, d_out, m_out, v_out):
        g = own[...].astype(F32) + r1[...].astype(F32) + r2[...].astype(F32) + r3[...].astype(F32)
        d, mn, vn = _adam(w_ref[...], g, m_ref[...], v_ref[...])
        g_out[...] = g
        d_out[...] = d
        m_out[...] = mn
        v_out[...] = vn

    def slot(off):
        return pl.BlockSpec((None, rb, wd), lambda i, k: ((k[0] + off) % N_CHIP, i, 0))

    plain = pl.BlockSpec((rb, wd), lambda i, k: (i, 0))
    return pl.pallas_call(
        body, name=name,
        grid_spec=pltpu.PrefetchScalarGridSpec(
            num_scalar_prefetch=1, grid=(r // rb,),
            in_specs=[slot(0), slot(1), slot(2), slot(3), plain, plain, plain],
            out_specs=[plain] * 4),
        out_shape=[jax.ShapeDtypeStruct(w.shape, F32)] * 4,
        compiler_params=_cparams(("parallel",)),
    )(chip_arr, part, land, land, land, w, m, v)


def _small_sum(gathered, name):
    _, r, w = gathered.shape

    def body(g_ref, o_ref):
        acc = g_ref[0]
        for d in range(1, N_DEV):
            acc = acc + g_ref[d]
        o_ref[...] = acc

    return pl.pallas_call(body, name=name, out_shape=jax.ShapeDtypeStruct((r, w), F32))(gathered)


def _small_adam(ws, gs, ms, vs, name):
    n = len(ws)

    def body(*refs):
        ins, outs = refs[:4 * n], refs[4 * n:]
        for t in range(n):
            d, mn, vn = _adam(ins[t][...], ins[n + t][...], ins[2 * n + t][...], ins[3 * n + t][...])
            outs[t][...] = d
            outs[n + t][...] = mn
            outs[2 * n + t][...] = vn

    shapes = [jax.ShapeDtypeStruct(w.shape, F32) for w in ws]
    return pl.pallas_call(body, name=name, out_shape=shapes * 3)(*ws, *gs, *ms, *vs)


def _mixer_fwd(x, shards, ln_v_g, ln_v_b, w_s, b_s3, conv_w, conv_b):
    t_len, d = x.shape
    hd = d // HEADS
    tm = _tile(t_len, 512)
    nt = t_len // tm
    assert nt >= 2
    w8 = shards[0].shape[0]
    bps = w8 // hd
    r8 = shards[1].shape[0]
    nq = N_IN * HEADS
    n_blocks = nq + 3 * N_DEV
    groups = [[("in", HEADS * b + HEADS - 1 - p) for b in range(N_MIX)] for p in range(HEADS)]
    groups.append([("in", q) for q in range(N_MIX * HEADS, nq)] + [(t, s) for t in (1, 2, 3) for s in range(N_DEV)])

    def owner_of(blk):
        return blk[1] // bps if blk[0] == "in" else blk[1]

    send_step = {}
    for s in range(N_DEV):
        mine = [blk for grp in groups for blk in grp if owner_of(blk) == s]
        for pos, blk in enumerate(mine):
            send_step[blk] = 0 if pos < FIRST_SENDS else 1 + (pos - FIRST_SENDS) // SENDS_PER_GROUP
    assert max(send_step.values()) < HEADS
    assert all(send_step[blk] <= max(k - 1, 0) for k in range(HEADS) for blk in groups[k])
    tail_pass_step = nt - 6 if nt >= 8 else nt - 2

    def body(x_ref, sh_in, sh_oa, sh_ob, sh_out, lng, lnb, ws_ref, bs_ref, cw_ref, cb_ref,
             save_ref, ya_ref, yb_ref, f_in, f_oa, f_ob, f_out,
             hbuf, wbuf, recv_sems, own_sems, fwd_sems, local_sems, load_sems):
        g = pl.program_id(0)
        i = pl.program_id(1)
        x, y, c = lax.axis_index("x"), lax.axis_index("y"), lax.axis_index("c")
        sibling = (x, y, 1 - c)
        chips = [(1 - x, y), (x, 1 - y), (1 - x, 1 - y)]
        shard_refs = (sh_in, sh_oa, sh_ob, sh_out)
        fulls = (f_in, f_oa, f_ob, f_out)

        def tensor(blk):
            return 0 if blk[0] == "in" else blk[0]

        def owner(blk):
            s = owner_of(blk)
            return s // 4, (s // 2) % 2, s % 2

        def bid(blk):
            return blk[1] if blk[0] == "in" else nq + (blk[0] - 1) * N_DEV + blk[1]

        def region(blk):
            if blk[0] == "in":
                return f_in.at[pl.ds(blk[1] * hd, hd), :]
            return fulls[blk[0]].at[pl.ds(blk[1] * r8, r8), :]

        def own_src(blk):
            if blk[0] == "in":
                return sh_in.at[pl.ds((blk[1] % bps) * hd, hd), :]
            return shard_refs[blk[0]]

        def rcopy(blk, to, send_sem, own):
            return pltpu.make_async_remote_copy(
                src_ref=own_src(blk) if own else region(blk), dst_ref=region(blk),
                send_sem=send_sem, recv_sem=recv_sems.at[bid(blk)], device_id=to, device_id_type=MESH)

        def send_own(blk):
            ox, oy, oc = owner(blk)

            @pl.when((x == ox) & (y == oy) & (c == oc))
            def _():
                rcopy(blk, sibling, own_sems.at[tensor(blk)], True).start()
                for chip in chips:
                    rcopy(blk, (*chip, c), own_sems.at[tensor(blk)], True).start()

        def pass_on(blk):
            ox, oy, oc = owner(blk)

            @pl.when(((x != ox) | (y != oy)) & (c == oc))
            def _():
                rcopy(blk, sibling, fwd_sems.at[tensor(blk)], False).wait_recv()
                rcopy(blk, sibling, fwd_sems.at[tensor(blk)], False).start()

        def wait_from_sibling(blk):
            @pl.when(c != owner(blk)[2])
            def _():
                rcopy(blk, sibling, fwd_sems.at[tensor(blk)], False).wait_recv()

        def local_copy(t):
            me = 4 * x + 2 * y + c
            return pltpu.make_async_copy(shard_refs[t], _shard_ref(fulls[t], "row", me, w8 if t == 0 else r8),
                                         local_sems.at[t])

        first = (g == 0) & (i == 0)

        @pl.when(first)
        def _():
            for t in range(4):
                local_copy(t).start()
            for grp in groups:
                for blk in grp:
                    if send_step[blk] == 0:
                        send_own(blk)
            for blk in groups[0]:
                pass_on(blk)

        @pl.when(i == 0)
        def _():
            for k in range(1, HEADS):
                @pl.when(g == k)
                def _(k=k):
                    for grp in groups:
                        for blk in grp:
                            if send_step[blk] == k:
                                send_own(blk)

        @pl.when(i == nt - 2)
        def _():
            for k in range(HEADS - 1):
                @pl.when(g == k)
                def _(k=k):
                    for blk in groups[k + 1]:
                        pass_on(blk)

        @pl.when((g == HEADS - 1) & (i == tail_pass_step))
        def _():
            for blk in groups[HEADS]:
                pass_on(blk)

        def fetch_weights(k):
            for blk in groups[k]:
                wait_from_sibling(blk)
            for b, blk in enumerate(groups[k]):
                ox, oy, oc = owner(blk)
                mine = (x == ox) & (y == oy) & (c == oc)

                @pl.when(mine)
                def _(b=b, blk=blk):
                    pltpu.make_async_copy(own_src(blk), wbuf.at[k % 2, b], load_sems.at[b]).start()

                @pl.when(jnp.logical_not(mine))
                def _(b=b, blk=blk):
                    pltpu.make_async_copy(region(blk), wbuf.at[k % 2, b], load_sems.at[b]).start()

        @pl.when(i == 0)
        def _():
            for k in range(HEADS):
                @pl.when(g == k)
                def _(k=k):
                    if k == 0:
                        fetch_weights(0)
                    for b, blk in enumerate(groups[k]):
                        pltpu.make_async_copy(region(blk), wbuf.at[k % 2, b], load_sems.at[b]).wait()

        @pl.when(i == nt - 1)
        def _():
            for k in range(1, HEADS):
                @pl.when(g == k - 1)
                def _(k=k):
                    fetch_weights(k)

        xt = x_ref[...].astype(BF16)
        wslot = g % 2
        u = _dot_nt(xt, wbuf[wslot, 0])
        v = _dot_nt(xt, wbuf[wslot, 1])
        z = _dot_nt(xt, wbuf[wslot, 2])
        save_ref[0] = u.astype(BF16)
        save_ref[1] = v.astype(BF16)
        save_ref[2] = z.astype(BF16)
        gu, _ = _gelu(u)
        gv, _ = _gelu(v)
        mu = jnp.mean(gv, axis=-1, keepdims=True)
        dv = gv - mu
        var = jnp.mean(dv * dv, axis=-1, keepdims=True)
        vn = (dv * lax.rsqrt(var + LN_EPS) * lng[...] + lnb[...]).astype(BF16)
        sz, _ = _silu(z)
        gate = gu * sz
        wm = jnp.where(_tril_mask(), ws_ref[0], 0.0).astype(BF16)
        bs = bs_ref[0]
        for ck in range(tm // CHUNK):
            rows = slice(ck * CHUNK, (ck + 1) * CHUNK)
            mixed = _dot(wm, vn[rows]) + bs
            ya_ref[rows, :] = (gate[rows] * mixed).astype(BF16)

        xbv = _dot_nt(xt, wbuf[wslot, 3])
        cbv = _dot_nt(xt, wbuf[wslot, 4])
        bbv = _dot_nt(xt, wbuf[wslot, 5])
        zbv = _dot_nt(xt, wbuf[wslot, 6])
        save_ref[3] = xbv.astype(BF16)
        save_ref[4] = cbv.astype(BF16)
        save_ref[5] = bbv.astype(BF16)
        save_ref[6] = zbv.astype(BF16)
        h = cbv * xbv

        @pl.when(i == 0)
        def _():
            hbuf[0:SUBLANES, :] = jnp.zeros((SUBLANES, hd), F32)

        hbuf[SUBLANES:SUBLANES + tm, :] = h
        h1 = hbuf[SUBLANES - 1:SUBLANES - 1 + tm, :]
        h2 = hbuf[SUBLANES - 2:SUBLANES - 2 + tm, :]
        conv = cb_ref[...] + cw_ref[0:1, :] * h2 + cw_ref[1:2, :] * h1 + cw_ref[2:3, :] * h
        hbuf[0:SUBLANES, :] = h[tm - SUBLANES:tm, :]
        save_ref[7] = conv.astype(BF16)
        szb, _ = _silu(zbv)
        yb_ref[...] = (bbv * conv * szb).astype(BF16)

        @pl.when((g == HEADS - 1) & (i == nt - 1))
        def _():
            for blk in groups[HEADS]:
                wait_from_sibling(blk)
            for t in range(4):
                local_copy(t).wait()
            for t in range(4):
                n = w8 if t == 0 else r8
                own_all, fwd_all = fulls[t].at[pl.ds(0, 4 * n), :], fulls[t].at[pl.ds(0, 3 * n), :]
                for ref, sem in ((own_all, own_sems.at[t]), (fwd_all, fwd_sems.at[t])):
                    pltpu.make_async_remote_copy(src_ref=ref, dst_ref=ref, send_sem=sem, recv_sem=sem,
                                                 device_id=sibling, device_id_type=MESH).wait_send()

    def head(g):
        return HEADS - 1 - g

    vec = pl.BlockSpec((1, hd), lambda g, i: (0, head(g)))
    hbm = pl.BlockSpec(memory_space=pl.ANY)
    return pl.pallas_call(
        body, name="mixer_fwd", grid=(HEADS, nt),
        in_specs=[pl.BlockSpec((tm, d), lambda g, i: (i, 0)), hbm, hbm, hbm, hbm,
                  vec, vec,
                  pl.BlockSpec((1, CHUNK, CHUNK), lambda g, i: (head(g), 0, 0)),
                  pl.BlockSpec((1, CHUNK, 1), lambda g, i: (head(g), 0, 0)),
                  pl.BlockSpec((3, hd), lambda g, i: (0, head(g))),
                  vec],
        out_specs=[pl.BlockSpec((N_SAVE, tm, hd), lambda g, i: (0, i, head(g))),
                   pl.BlockSpec((tm, hd), lambda g, i: (i, head(g))),
                   pl.BlockSpec((tm, hd), lambda g, i: (i, head(g))),
                   hbm, hbm, hbm, hbm],
        out_shape=[jax.ShapeDtypeStruct((N_SAVE, t_len, d), BF16),
                   jax.ShapeDtypeStruct((t_len, d), BF16),
                   jax.ShapeDtypeStruct((t_len, d), BF16),
                   jax.ShapeDtypeStruct((N_DEV * w8, d), BF16),
                   jax.ShapeDtypeStruct((d, d), BF16), jax.ShapeDtypeStruct((d, d), BF16),
                   jax.ShapeDtypeStruct((d, d), BF16)],
        scratch_shapes=[pltpu.VMEM((SUBLANES + tm, hd), F32), pltpu.VMEM((2, N_MIX, hd, d), BF16),
                        pltpu.SemaphoreType.DMA((n_blocks,)), pltpu.SemaphoreType.DMA((4,)),
                        pltpu.SemaphoreType.DMA((4,)), pltpu.SemaphoreType.DMA((4,)),
                        pltpu.SemaphoreType.DMA((N_MIX,))],
        compiler_params=_cparams(("arbitrary", "arbitrary")),
    )(x, *shards, ln_v_g, ln_v_b, w_s, b_s3, conv_w, conv_b)


def _merge_fwd(x, ya, yb, w_in, w_oa, w_ob, b_gate):
    t_len, d = x.shape
    tm = _tile(t_len, 256)
    tn = _tile(d, 1024)
    nj = d // tn

    def body(x_ref, ya_ref, yb_ref, wga, wgb, woa, wob, bga, bgb, sa_ref, sb_ref, pa_ref, pb_ref, mg_ref):
        xt = x_ref[...].astype(BF16)
        sa = jax.nn.sigmoid(_dot_nt(xt, wga[...]) + bga[...])
        sb = jax.nn.sigmoid(_dot_nt(xt, wgb[...]) + bgb[...])
        pa = _dot(ya_ref[...], woa[...])
        pb = _dot(yb_ref[...], wob[...])
        sa_ref[...] = sa.astype(BF16)
        sb_ref[...] = sb.astype(BF16)
        pa_ref[...] = pa.astype(BF16)
        pb_ref[...] = pb.astype(BF16)
        mg_ref[...] = (sa * pa + sb * pb).astype(BF16)

    row = pl.BlockSpec((tm, d), lambda j, i: (i, 0))
    out = pl.BlockSpec((tm, tn), lambda j, i: (i, j))
    return pl.pallas_call(
        body, name="merge_fwd", grid=(nj, t_len // tm),
        in_specs=[row, row, row,
                  pl.BlockSpec((tn, d), lambda j, i: (7 * nj + j, 0)),
                  pl.BlockSpec((tn, d), lambda j, i: (8 * nj + j, 0)),
                  pl.BlockSpec((d, tn), lambda j, i: (0, j)),
                  pl.BlockSpec((d, tn), lambda j, i: (0, j)),
                  pl.BlockSpec((1, tn), lambda j, i: (0, j)),
                  pl.BlockSpec((1, tn), lambda j, i: (0, nj + j))],
        out_specs=[out] * 5,
        out_shape=[jax.ShapeDtypeStruct((t_len, d), BF16)] * 5,
        compiler_params=_cparams(("parallel", "arbitrary")),
    )(x, ya, yb, w_in, w_in, w_oa, w_ob, b_gate, b_gate)


def _out_ln_loss(merged, w_out, x, target, ln_g, ln_b, sa, sb, pa, pb):
    t_len, d = x.shape
    tm = _tile(t_len, 256)
    nt = t_len // tm

    def body(mg_ref, w_ref, x_ref, t_ref, g_ref, b_ref, sa_ref, sb_ref, pa_ref, pb_ref,
             dz_ref, gx_ref, xb_ref, da_ref, db_ref, dg_ref, glg_ref, glb_ref, ls_ref, ga_ref, gb_ref,
             a_g, a_b, a_l, acc_a, acc_b):
        i = pl.program_id(0)
        xb_ref[...] = x_ref[...].astype(BF16)

        @pl.when(i == 0)
        def _():
            for a in (a_g, a_b, a_l, acc_a, acc_b):
                a[...] = jnp.zeros_like(a)

        zres = DN_ALPHA * x_ref[...] + _dot(mg_ref[...], w_ref[...])
        mu = jnp.mean(zres, axis=-1, keepdims=True)
        dc = zres - mu
        var = jnp.mean(dc * dc, axis=-1, keepdims=True)
        rstd = lax.rsqrt(var + LN_EPS)
        xhat = dc * rstd
        g = g_ref[...]
        err = xhat * g + b_ref[...] - t_ref[...]
        dy = err * (1.0 / d)
        a_l[...] += _fold8(err * err)
        a_g[...] += _fold8(dy * xhat)
        a_b[...] += _fold8(dy)
        dxh = dy * g
        m1 = jnp.mean(dxh, axis=-1, keepdims=True)
        m2 = jnp.mean(dxh * xhat, axis=-1, keepdims=True)
        dz = rstd * (dxh - m1 - xhat * m2)
        dz_bf = dz.astype(BF16)
        dz_ref[...] = dz_bf
        gx_ref[...] = DN_ALPHA * dz

        dm = _dot_nt(dz_bf, w_ref[...])
        sa = sa_ref[...].astype(F32)
        sb = sb_ref[...].astype(F32)
        da = dm * sa
        db = dm * sb
        da_ref[...] = da.astype(BF16)
        db_ref[...] = db.astype(BF16)
        dga = da * pa_ref[...].astype(F32) * (1.0 - sa)
        dgb = db * pb_ref[...].astype(F32) * (1.0 - sb)
        dg_ref[0] = dga.astype(BF16)
        dg_ref[1] = dgb.astype(BF16)
        acc_a[...] += _fold8(dga)
        acc_b[...] += _fold8(dgb)

        @pl.when(i == nt - 1)
        def _():
            glg_ref[...] = jnp.sum(a_g[...], axis=0, keepdims=True)
            glb_ref[...] = jnp.sum(a_b[...], axis=0, keepdims=True)
            ls_ref[...] = jnp.sum(a_l[...], axis=0, keepdims=True)
            ga_ref[...] = jnp.sum(acc_a[...], axis=0, keepdims=True)
            gb_ref[...] = jnp.sum(acc_b[...], axis=0, keepdims=True)

    row = pl.BlockSpec((tm, d), lambda i: (i, 0))
    vec = pl.BlockSpec((1, d), lambda i: (0, 0))
    act = jax.ShapeDtypeStruct((t_len, d), BF16)
    part = jax.ShapeDtypeStruct((1, d), F32)
    return pl.pallas_call(
        body, name="out_ln_loss", grid=(nt,),
        in_specs=[row, pl.BlockSpec((d, d), lambda i: (0, 0)), row, row, vec, vec, row, row, row, row],
        out_specs=[row, row, row, row, row, pl.BlockSpec((2, tm, d), lambda i: (0, i, 0)), vec, vec, vec, vec, vec],
        out_shape=[act, jax.ShapeDtypeStruct((t_len, d), F32), act, act, act,
                   jax.ShapeDtypeStruct((2, t_len, d), BF16), part, part, part, part, part],
        scratch_shapes=[pltpu.VMEM((SUBLANES, d), F32)] * 5,
        compiler_params=_cparams(("arbitrary",)),
    )(merged, w_out, x, target, ln_g, ln_b, sa, sb, pa, pb)


def _grad_w(a, b, name, exchange=None):
    t_len, m = a.shape
    n = b.shape[1]
    tm, tn, tk = _tile(m, MM_TILE), _tile(n, MM_TILE), _tile(t_len, MM_DEPTH_TOKENS)
    ni, nj, nk = m // tm, n // tn, t_len // tk

    def body(a_ref, b_ref, *rest):
        if exchange is None:
            o_ref, acc = rest
        else:
            src, o_ref, land, acc, send_sems, recv_sems = rest
        i, j, k = pl.program_id(0), pl.program_id(1), pl.program_id(2)

        def copies():
            x, y, c = lax.axis_index("x"), lax.axis_index("y"), lax.axis_index("c")
            w = exchange.shape[1] // N_DEV
            return [pltpu.make_async_remote_copy(
                src_ref=_shard_ref(src, "col", 2 * s + 1 - c, w), dst_ref=land.at[s],
                send_sem=send_sems.at[s], recv_sem=recv_sems.at[s],
                device_id=(x, y, 1 - c), device_id_type=MESH) for s in range(N_CHIP)]

        if exchange is not None:
            @pl.when((i == 0) & (j == 0) & (k == 0))
            def _():
                for cp in copies():
                    cp.start()

        @pl.when(k == 0)
        def _():
            acc[...] = jnp.zeros_like(acc)

        acc[...] += _dot_tn(a_ref[...], b_ref[...])

        @pl.when(k == nk - 1)
        def _():
            o_ref[...] = acc[...].astype(BF16)

        if exchange is not None:
            @pl.when((i == ni - 1) & (j == nj - 1) & (k == nk - 1))
            def _():
                for cp in copies():
                    cp.wait()

    in_specs = [pl.BlockSpec((tk, tm), lambda i, j, k: (k, i)), pl.BlockSpec((tk, tn), lambda i, j, k: (k, j))]
    out_specs = [pl.BlockSpec((tm, tn), lambda i, j, k: (i, j))]
    out_shape = [jax.ShapeDtypeStruct((m, n), BF16)]
    scratch = [pltpu.VMEM((tm, tn), F32)]
    args = [a, b]
    if exchange is not None:
        hbm = pl.BlockSpec(memory_space=pl.ANY)
        in_specs.append(hbm)
        out_specs.append(hbm)
        out_shape.append(jax.ShapeDtypeStruct((N_CHIP, exchange.shape[0], exchange.shape[1] // N_DEV), exchange.dtype))
        scratch += [pltpu.SemaphoreType.DMA((N_CHIP,)), pltpu.SemaphoreType.DMA((N_CHIP,))]
        args.append(exchange)
    outs = pl.pallas_call(
        body, name=name, grid=(ni, nj, nk),
        in_specs=in_specs, out_specs=out_specs, out_shape=out_shape, scratch_shapes=scratch,
        compiler_params=_cparams(("arbitrary", "arbitrary", "arbitrary")),
    )(*args)
    return outs[0] if exchange is None else (outs[0], outs[1])


def _mixer_bwd(da, db, w_oa, w_ob, saved, dgate, ln_v_g, ln_v_b, w_s, b_s3, conv_w):
    t_len, d = da.shape
    hd = d // HEADS
    tm = _tile(t_len, 1024)
    nt = t_len // tm

    def body(da_ref, db_ref, woa, wob, sv, dgt, lng, lnb, ws_ref, bs_ref, cw_ref,
             dp_ref, gws_ref, gbs_ref, glg_ref, glb_ref, gcw_ref, gcb_ref,
             dbuf, carry, a_ws, a_bs, a_lg, a_lb, a_c0, a_c1, a_c2, a_cb):
        i = pl.program_id(1)

        @pl.when(i == 0)
        def _():
            carry[...] = jnp.zeros_like(carry)
            for a in (a_ws, a_bs, a_lg, a_lb, a_c0, a_c1, a_c2, a_cb):
                a[...] = jnp.zeros_like(a)

        dya_t = _dot_nt(da_ref[...], woa[...])
        dyb_t = _dot_nt(db_ref[...], wob[...])
        g = lng[...]
        beta = lnb[...]
        wm = jnp.where(_tril_mask(), ws_ref[0], 0.0).astype(BF16)
        bs = bs_ref[0]
        w0, w1, w2 = cw_ref[0:1, :], cw_ref[1:2, :], cw_ref[2:3, :]
        dbuf[tm:tm + SUBLANES, :] = carry[...]
        gws = a_ws[...]
        gbs = a_bs[...]
        lg, lb = a_lg[...], a_lb[...]
        c0, c1, c2, cb = a_c0[...], a_c1[...], a_c2[...], a_cb[...]
        for ck in reversed(range(tm // CHUNK)):
            r0 = ck * CHUNK
            rows = slice(r0, r0 + CHUNK)
            dya = dya_t[rows]
            u = sv[0, rows, :].astype(F32)
            v = sv[1, rows, :].astype(F32)
            z = sv[2, rows, :].astype(F32)
            gu, gu_grad = _gelu(u)
            gv, gv_grad = _gelu(v)
            mu = jnp.mean(gv, axis=-1, keepdims=True)
            dvc = gv - mu
            var = jnp.mean(dvc * dvc, axis=-1, keepdims=True)
            rstd = lax.rsqrt(var + LN_EPS)
            vhat = dvc * rstd
            vn = (vhat * g + beta).astype(BF16)
            sz, sz_grad = _silu(z)
            t1 = dya * sz
            dmixed = t1 * gu
            dmixed_b = dmixed.astype(BF16)
            mixed = _dot(wm, vn) + bs
            gws = gws + _dot_nt(dmixed_b, vn)
            gbs = gbs + dmixed
            dvn = _dot_tn(wm, dmixed_b)
            dp_ref[0, rows, :] = (t1 * mixed * gu_grad).astype(BF16)
            dp_ref[2, rows, :] = (dya * gu * mixed * sz_grad).astype(BF16)
            lg = lg + _fold8(dvn * vhat)
            lb = lb + _fold8(dvn)
            dvh = dvn * g
            m1 = jnp.mean(dvh, axis=-1, keepdims=True)
            m2 = jnp.mean(dvh * vhat, axis=-1, keepdims=True)
            dp_ref[1, rows, :] = (rstd * (dvh - m1 - vhat * m2) * gv_grad).astype(BF16)
            dyb = dyb_t[rows]
            xbv = sv[3, rows, :].astype(F32)
            cbv = sv[4, rows, :].astype(F32)
            bbv = sv[5, rows, :].astype(F32)
            zbv = sv[6, rows, :].astype(F32)
            conv = sv[7, rows, :].astype(F32)
            szb, szb_grad = _silu(zbv)
            dp_ref[5, rows, :] = (dyb * conv * szb).astype(BF16)
            dp_ref[6, rows, :] = (dyb * bbv * conv * szb_grad).astype(BF16)
            dconv = dyb * bbv * szb
            dbuf[r0:r0 + CHUNK, :] = dconv
            dc1 = dbuf[r0 + 1:r0 + 1 + CHUNK, :]
            dc2 = dbuf[r0 + 2:r0 + 2 + CHUNK, :]
            if ck == 0:
                carry[...] = dconv[0:SUBLANES, :]
            h = cbv * xbv
            c2 = c2 + _fold8(dconv * h)
            c1 = c1 + _fold8(dc1 * h)
            c0 = c0 + _fold8(dc2 * h)
            cb = cb + _fold8(dconv)
            dh = w2 * dconv + w1 * dc1 + w0 * dc2
            dp_ref[3, rows, :] = (dh * cbv).astype(BF16)
            dp_ref[4, rows, :] = (dh * xbv).astype(BF16)
            dp_ref[7, rows, :] = dgt[0, rows, :]
            dp_ref[8, rows, :] = dgt[1, rows, :]
        a_ws[...] = gws
        a_bs[...] = gbs
        a_lg[...], a_lb[...] = lg, lb
        a_c0[...], a_c1[...], a_c2[...], a_cb[...] = c0, c1, c2, cb

        @pl.when(i == nt - 1)
        def _():
            gws_ref[0] = jnp.where(_tril_mask(), a_ws[...], 0.0)
            gbs_ref[0] = jnp.sum(a_bs[...], axis=1, keepdims=True)
            glg_ref[...] = jnp.sum(a_lg[...], axis=0, keepdims=True)
            glb_ref[...] = jnp.sum(a_lb[...], axis=0, keepdims=True)
            gcw_ref[0:1, :] = jnp.sum(a_c0[...], axis=0, keepdims=True)
            gcw_ref[1:2, :] = jnp.sum(a_c1[...], axis=0, keepdims=True)
            gcw_ref[2:3, :] = jnp.sum(a_c2[...], axis=0, keepdims=True)
            gcb_ref[...] = jnp.sum(a_cb[...], axis=0, keepdims=True)

    def rev(i):
        return nt - 1 - i

    row = pl.BlockSpec((tm, d), lambda g, i: (rev(i), 0))
    wrow = pl.BlockSpec((hd, d), lambda g, i: (g, 0))
    vec = pl.BlockSpec((1, hd), lambda g, i: (0, g))
    acc8 = pltpu.VMEM((SUBLANES, hd), F32)
    return pl.pallas_call(
        body, name="mixer_bwd", grid=(HEADS, nt),
        in_specs=[row, row, wrow, wrow,
                  pl.BlockSpec((N_SAVE, tm, hd), lambda g, i: (0, rev(i), g)),
                  pl.BlockSpec((2, tm, hd), lambda g, i: (0, rev(i), g)),
                  vec, vec,
                  pl.BlockSpec((1, CHUNK, CHUNK), lambda g, i: (g, 0, 0)),
                  pl.BlockSpec((1, CHUNK, 1), lambda g, i: (g, 0, 0)),
                  pl.BlockSpec((3, hd), lambda g, i: (0, g))],
        out_specs=[pl.BlockSpec((N_IN, tm, hd), lambda g, i: (0, rev(i), g)),
                   pl.BlockSpec((1, CHUNK, CHUNK), lambda g, i: (g, 0, 0)),
                   pl.BlockSpec((1, CHUNK, 1), lambda g, i: (g, 0, 0)),
                   vec, vec,
                   pl.BlockSpec((3, hd), lambda g, i: (0, g)),
                   vec],
        out_shape=[jax.ShapeDtypeStruct((N_IN, t_len, d), BF16),
                   jax.ShapeDtypeStruct((HEADS, CHUNK, CHUNK), F32),
                   jax.ShapeDtypeStruct((HEADS, CHUNK, 1), F32),
                   jax.ShapeDtypeStruct((1, d), F32), jax.ShapeDtypeStruct((1, d), F32),
                   jax.ShapeDtypeStruct((3, d), F32), jax.ShapeDtypeStruct((1, d), F32)],
        scratch_shapes=[pltpu.VMEM((tm + SUBLANES, hd), F32), acc8,
                        pltpu.VMEM((CHUNK, CHUNK), F32), pltpu.VMEM((CHUNK, hd), F32),
                        acc8, acc8, acc8, acc8, acc8, acc8],
        compiler_params=_cparams(("parallel", "arbitrary")),
    )(da, db, w_oa, w_ob, saved, dgate, ln_v_g, ln_v_b, w_s, b_s3, conv_w)


def _grad_w_in(xb, dp):
    t_len, d = xb.shape
    tm, tn, tk = _tile(d, MM_TILE), _tile(d, MM_TILE), _tile(t_len, MM_DEPTH_TOKENS)
    nj = d // tn
    nk = t_len // tk

    def body(a_ref, b_ref, o_ref, acc):
        k = pl.program_id(2)

        @pl.when(k == 0)
        def _():
            acc[...] = jnp.zeros_like(acc)

        acc[...] += _dot_tn(a_ref[...], b_ref[...])

        @pl.when(k == nk - 1)
        def _():
            o_ref[...] = acc[...].astype(BF16)

    return pl.pallas_call(
        body, name="grad_w_in", grid=(d // tm, N_IN * nj, nk),
        in_specs=[pl.BlockSpec((tk, tm), lambda i, j, k: (k, i)),
                  pl.BlockSpec((None, tk, tn), lambda i, j, k: (j // nj, k, j % nj))],
        out_specs=pl.BlockSpec((tm, tn), lambda i, j, k: (i, j)),
        out_shape=jax.ShapeDtypeStruct((d, N_IN * d), BF16),
        scratch_shapes=[pltpu.VMEM((tm, tn), F32)],
        compiler_params=_cparams(("parallel", "parallel", "arbitrary")),
    )(xb, dp)


def _grad_x_tail(dp, w_in, partial, slots):
    _, t_len, d = dp.shape
    lo, hi = slots
    tm, tn, tk = _tile(t_len, MM_TILE), _tile(d, MM_TILE), _tile(d, MM_DEPTH)
    nkb = d // tk
    nk = (hi - lo) * nkb

    def body(a_ref, b_ref, r_ref, o_ref, acc):
        k = pl.program_id(2)

        @pl.when(k == 0)
        def _():
            acc[...] = r_ref[...]

        acc[...] += _dot(a_ref[...], b_ref[...])

        @pl.when(k == nk - 1)
        def _():
            o_ref[...] = acc[...]

    return pl.pallas_call(
        body, name="grad_x_tail", grid=(t_len // tm, d // tn, nk),
        in_specs=[pl.BlockSpec((None, tm, tk), lambda i, j, k: (lo + k // nkb, i, k % nkb)),
                  pl.BlockSpec((tk, tn), lambda i, j, k: (lo * nkb + k, j)),
                  pl.BlockSpec((tm, tn), lambda i, j, k: (i, j))],
        out_specs=pl.BlockSpec((tm, tn), lambda i, j, k: (i, j)),
        out_shape=jax.ShapeDtypeStruct((t_len, d), F32),
        scratch_shapes=[pltpu.VMEM((tm, tn), F32)],
        compiler_params=_cparams(("parallel", "parallel", "arbitrary")),
    )(dp, w_in, partial)


def _grad_x(dp, w_in, gx_direct, parts, packed, slots):
    _, t_len, d = dp.shape
    tm, tn, tk = _tile(t_len, MM_TILE), _tile(d, MM_TILE), _tile(d, MM_DEPTH)
    nkb = d // tk
    lo, hi = slots
    nk = (hi - lo) * nkb
    ni, nj = t_len // tm, d // tn
    n_parts = len(parts)

    def body(a_ref, b_ref, r_ref, *rest):
        srcs = rest[:n_parts]
        pk_ref = rest[n_parts]
        o_ref = rest[n_parts + 1]
        lands = rest[n_parts + 2:2 * n_parts + 2]
        gath = rest[2 * n_parts + 2]
        acc, send_sems, recv_sems, pk_send, pk_recv, pk_local = rest[2 * n_parts + 3:]
        i, j, k = pl.program_id(0), pl.program_id(1), pl.program_id(2)
        x, y, c = lax.axis_index("x"), lax.axis_index("y"), lax.axis_index("c")
        my_chip = 2 * x + y
        me = 4 * x + 2 * y + c
        chips = [(1 - x, y), (x, 1 - y), (1 - x, 1 - y)]

        def part_copy(t, n):
            px, py = chips[n]
            return pltpu.make_async_remote_copy(
                src_ref=srcs[t].at[2 * px + py], dst_ref=lands[t].at[my_chip],
                send_sem=send_sems.at[3 * t + n], recv_sem=recv_sems.at[3 * t + n],
                device_id=(px, py, c), device_id_type=MESH)

        def part_landing(t, n):
            px, py = chips[n]
            return pltpu.make_async_remote_copy(
                src_ref=srcs[t].at[my_chip], dst_ref=lands[t].at[2 * px + py],
                send_sem=send_sems.at[3 * t + n], recv_sem=recv_sems.at[3 * t + n],
                device_id=(px, py, c), device_id_type=MESH)

        def pk_copy(s):
            return pltpu.make_async_remote_copy(
                src_ref=pk_ref, dst_ref=gath.at[me], send_sem=pk_send, recv_sem=pk_recv.at[me],
                device_id=(s // 4, (s // 2) % 2, s % 2), device_id_type=MESH)

        @pl.when((i == 0) & (j == 0) & (k == 0))
        def _():
            for t in range(n_parts):
                for n in range(3):
                    part_copy(t, n).start()
            pltpu.make_async_copy(pk_ref, gath.at[me], pk_local).start()
            for s in range(N_DEV):
                @pl.when(s != me)
                def _(s=s):
                    pk_copy(s).start()

        @pl.when(k == 0)
        def _():
            acc[...] = r_ref[...]

        acc[...] += _dot(a_ref[...], b_ref[...])

        @pl.when(k == nk - 1)
        def _():
            o_ref[...] = acc[...]

        @pl.when((i == ni - 1) & (j == nj - 1) & (k == nk - 1))
        def _():
            for t in range(n_parts):
                for n in range(3):
                    part_landing(t, n).wait_recv()
            for t in range(n_parts):
                for n in range(3):
                    part_copy(t, n).wait_send()
            for s in range(N_DEV):
                @pl.when(s != me)
                def _(s=s):
                    pltpu.make_async_remote_copy(
                        src_ref=pk_ref, dst_ref=gath.at[s], send_sem=pk_send, recv_sem=pk_recv.at[s],
                        device_id=(s // 4, (s // 2) % 2, s % 2), device_id_type=MESH).wait_recv()
            seven = gath.at[pl.ds(0, N_DEV - 1)]
            pltpu.make_async_remote_copy(src_ref=seven, dst_ref=seven, send_sem=pk_send, recv_sem=pk_send,
                                         device_id=(x, y, 1 - c), device_id_type=MESH).wait_send()
            pltpu.make_async_copy(pk_ref, gath.at[me], pk_local).wait()

    hbm = pl.BlockSpec(memory_space=pl.ANY)
    outs = pl.pallas_call(
        body, name="grad_x", grid=(ni, nj, nk),
        in_specs=[pl.BlockSpec((None, tm, tk), lambda i, j, k: (lo + k // nkb, i, k % nkb)),
                  pl.BlockSpec((tk, tn), lambda i, j, k: (lo * nkb + k, j)),
                  pl.BlockSpec((tm, tn), lambda i, j, k: (i, j))] + [hbm] * (n_parts + 1),
        out_specs=[pl.BlockSpec((tm, tn), lambda i, j, k: (i, j))] + [hbm] * (n_parts + 1),
        out_shape=[jax.ShapeDtypeStruct((t_len, d), F32)] + [jax.ShapeDtypeStruct(p.shape, p.dtype) for p in parts]
        + [jax.ShapeDtypeStruct((N_DEV,) + packed.shape, packed.dtype)],
        scratch_shapes=[pltpu.VMEM((tm, tn), F32),
                        pltpu.SemaphoreType.DMA((3 * n_parts,)), pltpu.SemaphoreType.DMA((3 * n_parts,)),
                        pltpu.SemaphoreType.DMA(()), pltpu.SemaphoreType.DMA((N_DEV,)), pltpu.SemaphoreType.DMA(())],
        compiler_params=_cparams(("arbitrary", "arbitrary", "arbitrary")),
    )(dp, w_in, gx_direct, *parts, packed)
    return outs[0], list(outs[1:1 + n_parts]), outs[1 + n_parts]


def kernel(x, w_in, b_gate, ln_v_g, ln_v_b, w_s, b_s, conv_w, conv_b, w_oa, w_ob, w_out, ln_g, ln_b, loss_target, m_w_in, m_b_gate, m_ln_v_g, m_ln_v_b, m_w_s, m_b_s, m_conv_w, m_conv_b, m_w_oa, m_w_ob, m_w_out, m_ln_g, m_ln_b, v_w_in, v_b_gate, v_ln_v_g, v_ln_v_b, v_w_s, v_b_s, v_conv_w, v_conv_b, v_w_oa, v_w_ob, v_w_out, v_ln_g, v_ln_b):
    _, t_len, d = x.shape
    assert d % (HEADS * 128) == 0 and t_len % CHUNK == 0 and w_in.shape[2] * N_DEV == N_IN * d
    x2 = x[0]
    tgt2 = loss_target[0]
    c_arr = lax.axis_index("c").astype(jnp.int32).reshape(1)
    chip_arr = (2 * lax.axis_index("x") + lax.axis_index("y")).astype(jnp.int32).reshape(1)
    dev = 4 * lax.axis_index("x") + 2 * lax.axis_index("y") + lax.axis_index("c")

    shards = [_cast_transposed(w_in[0], "cast_w_in"), _cast_bf16(w_oa[0], "cast_w_oa"),
              _cast_bf16(w_ob[0], "cast_w_ob"), _cast_bf16(w_out[0], "cast_w_out")]
    (conv_w_g,) = _all_gather([conv_w[0]], ["lead"], "gather_conv_w", vmem=True)
    conv_w_f = jnp.transpose(conv_w_g, (1, 0, 2)).reshape(3, d)
    w_s3 = w_s[0]
    b_s3 = b_s[0].reshape(HEADS, CHUNK, 1)

    saved, ya, yb, w_in_f, w_oa_f, w_ob_f, w_out_f = _mixer_fwd(
        x2, shards, ln_v_g, ln_v_b, w_s3, b_s3, conv_w_f, conv_b)
    sa, sb, pa, pb, merged = _merge_fwd(x2, ya, yb, w_in_f, w_oa_f, w_ob_f, b_gate)
    dz, gx_direct, xb, da, db, dgate, g_ln_g, g_ln_b, err2, g_bga, g_bgb = _out_ln_loss(
        merged, w_out_f, x2, tgt2, ln_g, ln_b, sa, sb, pa, pb)
    loss = lax.psum(0.5 * jnp.sum(err2) / d, ("x", "y", "c"))

    dp, g_ws, g_bs, g_lvg, g_lvb, g_cw, g_cb = _mixer_bwd(
        da, db, w_oa_f, w_ob_f, saved, dgate, ln_v_g, ln_v_b, w_s3, b_s3, conv_w_f)
    gw_in = _grad_w_in(xb, dp)

    gw_out, land_in = _grad_w(merged, dz, "grad_w_out", exchange=gw_in)
    part_in = _pair_sum(gw_in, land_in, "col", c_arr, "grad_pair_sum_0")
    gw_oa = _grad_w(ya, da, "grad_w_oa")
    gw_ob = _grad_w(yb, db, "grad_w_ob")
    rows = [gw_oa, gw_ob, gw_out]
    lands = _pair_exchange(rows, ["row"] * 3, "grad_pair_exchange")
    parts = [part_in] + [_pair_sum(g, l, "row", c_arr, "grad_pair_sum_%d" % (n + 1))
                         for n, (g, l) in enumerate(zip(rows, lands))]
    pieces = [jnp.concatenate([g_bga, g_bgb], axis=1), g_lvg, g_lvb, g_ws, g_bs, g_cw, g_cb, g_ln_g, g_ln_b]
    sizes = [p.size for p in pieces]
    packed = jnp.concatenate([p.reshape(-1, 128) for p in pieces], axis=0)
    gx_part, lands2, gathered = _grad_x(dp, w_in_f, gx_direct, parts, packed, (0, GRAD_X_COMM_BLOCKS))
    grad_x = _grad_x_tail(dp, w_in_f, gx_part, (GRAD_X_COMM_BLOCKS, N_IN))[None]

    big = []
    for n, (w, m, v) in enumerate([(w_in, m_w_in, v_w_in), (w_oa, m_w_oa, v_w_oa), (w_ob, m_w_ob, v_w_ob),
                                   (w_out, m_w_out, v_w_out)]):
        big.append([o[None] for o in _sum_adam(parts[n], lands2[n], w[0], m[0], v[0], chip_arr, "sum_adam_%d" % n)])
    (g_w_in, d_w_in, nm_w_in, nv_w_in), (g_w_oa, d_w_oa, nm_w_oa, nv_w_oa), \
        (g_w_ob, d_w_ob, nm_w_ob, nv_w_ob), (g_w_out, d_w_out, nm_w_out, nv_w_out) = big

    total = _small_sum(gathered, "sum_small_grads")
    offs = [0]
    for s in sizes:
        offs.append(offs[-1] + s // 128)
    unpacked = [total[offs[n]:offs[n + 1]] for n in range(len(pieces))]
    g_b_gate = unpacked[0].reshape(b_gate.shape)
    g_ln_v_g = unpacked[1].reshape(ln_v_g.shape)
    g_ln_v_b = unpacked[2].reshape(ln_v_b.shape)
    g_w_s = unpacked[3].reshape(w_s.shape)
    g_b_s = unpacked[4].reshape(b_s.shape)
    g_conv_w = lax.dynamic_slice_in_dim(unpacked[5].reshape(3, d), dev * (d // N_DEV), d // N_DEV, axis=1)[None]
    g_conv_b = unpacked[6].reshape(conv_b.shape)
    g_ln_g2 = unpacked[7].reshape(ln_g.shape)
    g_ln_b2 = unpacked[8].reshape(ln_b.shape)

    small_w = [b_gate, ln_v_g, ln_v_b, w_s, b_s, conv_w, conv_b, ln_g, ln_b]
    small_g = [g_b_gate, g_ln_v_g, g_ln_v_b, g_w_s, g_b_s, g_conv_w, g_conv_b, g_ln_g2, g_ln_b2]
    small_m = [m_b_gate, m_ln_v_g, m_ln_v_b, m_w_s, m_b_s, m_conv_w, m_conv_b, m_ln_g, m_ln_b]
    small_v = [v_b_gate, v_ln_v_g, v_ln_v_b, v_w_s, v_b_s, v_conv_w, v_conv_b, v_ln_g, v_ln_b]

    def flat(a):
        return a.reshape(-1, a.shape[-1])

    res = _small_adam([flat(a) for a in small_w], [flat(a) for a in small_g], [flat(a) for a in small_m],
                      [flat(a) for a in small_v], "adam_small")
    ns = len(small_w)
    d_s = [res[n].reshape(small_w[n].shape) for n in range(ns)]
    nm_s = [res[ns + n].reshape(small_w[n].shape) for n in range(ns)]
    nv_s = [res[2 * ns + n].reshape(small_w[n].shape) for n in range(ns)]

    def ordered(first, small, oa, ob, out):
        return [first] + small[:7] + [oa, ob, out] + small[7:]

    return (loss, grad_x,
            *ordered(g_w_in, small_g, g_w_oa, g_w_ob, g_w_out),
            *ordered(d_w_in, d_s, d_w_oa, d_w_ob, d_w_out),
            *ordered(nm_w_in, nm_s, nm_w_oa, nm_w_ob, nm_w_out),
            *ordered(nv_w_in, nv_s, nv_w_oa, nv_w_ob, nv_w_out))
```

```python
import functools

import jax
import jax.numpy as jnp
from jax import lax
from jax.experimental import pallas as pl
from jax.experimental.pallas import tpu as pltpu

F32 = jnp.float32
BF16 = jnp.bfloat16
MESH = pl.DeviceIdType.MESH

N_DEV = 8
N_CHIP = 4
HEADS = 8
CHUNK = 128
N_IN = 9
N_MIX = 7
N_SAVE = 8
LN_EPS = 1e-5
DN_ALPHA = 2.0 ** 0.25
ADAM_LR = 0.001
ADAM_B1 = 0.9
ADAM_B2 = 0.999
ADAM_EPS = 1e-08
ADAM_WD = 0.01
ADAM_STEP = 10
GELU_C0 = 0.7978845608028654
GELU_C1 = 0.044715
SUBLANES = 8
FIRST_SENDS = 3
SENDS_PER_GROUP = 2
MM_TILE = 1024
MM_DEPTH = 2048
MM_DEPTH_TOKENS = 4096
GRAD_X_COMM_BLOCKS = 8
VMEM_LIMIT = 56 << 20


def _cparams(sem):
    return pltpu.CompilerParams(dimension_semantics=sem, vmem_limit_bytes=VMEM_LIMIT)


def _tile(n, want):
    t = min(n, want)
    while n % t:
        t //= 2
    return t


def _gelu(u):
    u2 = u * u
    t = jnp.tanh(u * (GELU_C0 + (GELU_C0 * GELU_C1) * u2))
    hp = 0.5 * t + 0.5
    grad = hp + (0.5 * u) * (1.0 - t * t) * (GELU_C0 + (3.0 * GELU_C0 * GELU_C1) * u2)
    return u * hp, grad


def _silu(z):
    s = jax.nn.sigmoid(z)
    sil = z * s
    return sil, s + sil * (1.0 - s)


def _fold8(a):
    return jnp.sum(a.reshape(a.shape[0] // SUBLANES, SUBLANES, a.shape[1]), axis=0)


def _dot(a, b):
    return jnp.dot(a, b, preferred_element_type=F32)


def _dot_nt(a, b):
    return lax.dot_general(a, b, (((1,), (1,)), ((), ())), preferred_element_type=F32)


def _dot_tn(a, b):
    return lax.dot_general(a, b, (((0,), (0,)), ((), ())), preferred_element_type=F32)


def _tril_mask():
    r = lax.broadcasted_iota(jnp.int32, (CHUNK, CHUNK), 0)
    c = lax.broadcasted_iota(jnp.int32, (CHUNK, CHUNK), 1)
    return c <= r


def _cast_bf16(a, name):
    rows, cols = a.shape
    rb = _tile(rows, 256)

    def body(a_ref, o_ref):
        o_ref[...] = a_ref[...].astype(BF16)

    return pl.pallas_call(
        body, name=name, grid=(rows // rb,),
        in_specs=[pl.BlockSpec((rb, cols), lambda i: (i, 0))],
        out_specs=pl.BlockSpec((rb, cols), lambda i: (i, 0)),
        out_shape=jax.ShapeDtypeStruct((rows, cols), BF16),
        compiler_params=_cparams(("parallel",)),
    )(a)


def _cast_transposed(a, name):
    rows, cols = a.shape
    tb = _tile(rows, 2048)
    tc = _tile(cols, 256)

    def body(a_ref, o_ref):
        o_ref[...] = a_ref[...].T.astype(BF16)

    return pl.pallas_call(
        body, name=name, grid=(cols // tc, rows // tb),
        in_specs=[pl.BlockSpec((tb, tc), lambda j, i: (i, j))],
        out_specs=pl.BlockSpec((tc, tb), lambda j, i: (j, i)),
        out_shape=jax.ShapeDtypeStruct((cols, rows), BF16),
        compiler_params=_cparams(("parallel", "parallel")),
    )(a)


def _shard_ref(full, kind, s, n):
    if kind == "col":
        return full.at[:, pl.ds(pl.multiple_of(s * n, 128), n)]
    if kind == "row":
        return full.at[pl.ds(pl.multiple_of(s * n, SUBLANES), n), :]
    return full.at[s]


def _all_gather(shards, kinds, name, vmem):
    nt = len(shards)
    out_shapes = []
    for a, kind in zip(shards, kinds):
        if kind == "col":
            out_shapes.append(jax.ShapeDtypeStruct((a.shape[0], N_DEV * a.shape[1]), a.dtype))
        elif kind == "row":
            out_shapes.append(jax.ShapeDtypeStruct((N_DEV * a.shape[0], a.shape[1]), a.dtype))
        else:
            out_shapes.append(jax.ShapeDtypeStruct((N_DEV,) + a.shape, a.dtype))

    def body(*refs):
        srcs, fulls = refs[:nt], refs[nt:2 * nt]
        send_sems, recv_sems, local_sems = refs[2 * nt:]
        x, y, c = lax.axis_index("x"), lax.axis_index("y"), lax.axis_index("c")
        sibling = (x, y, 1 - c)
        chips = [(1 - x, y), (x, 1 - y), (1 - x, 1 - y)]

        def dev(px, py, pc):
            return 4 * px + 2 * py + pc

        def region(t, s):
            a, kind = shards[t], kinds[t]
            n = a.shape[1] if kind == "col" else a.shape[0]
            return _shard_ref(fulls[t], kind, s, n)

        def copy(t, k, block, to, own=False):
            return pltpu.make_async_remote_copy(
                src_ref=srcs[t] if own else region(t, block), dst_ref=region(t, block),
                send_sem=send_sems.at[7 * t + k], recv_sem=recv_sems.at[7 * t + k],
                device_id=to, device_id_type=MESH)

        me = dev(x, y, c)
        started = []
        for t in range(nt):
            mine = pltpu.make_async_copy(srcs[t], region(t, me), local_sems.at[t])
            mine.start()
            started.append(mine)
        first = []
        for t in range(nt):
            first.append(copy(t, 0, me, sibling, own=True))
            for j, chip in enumerate(chips):
                first.append(copy(t, 1 + j, me, (*chip, c), own=True))
        for cp in first:
            cp.start()
        passed = []
        for t in range(nt):
            for j, chip in enumerate(chips):
                blk = dev(*chip, c)
                copy(t, 1 + j, blk, sibling).wait_recv()
                fwd = copy(t, 4 + j, blk, sibling)
                fwd.start()
                passed.append(fwd)
        for t in range(nt):
            copy(t, 0, dev(x, y, 1 - c), sibling).wait_recv()
            for j, chip in enumerate(chips):
                copy(t, 4 + j, dev(*chip, 1 - c), sibling).wait_recv()
        for cp in first + passed:
            cp.wait_send()
        for mine in started:
            mine.wait()

    space = pltpu.VMEM if vmem else pl.ANY
    return pl.pallas_call(
        body, name=name,
        in_specs=[pl.BlockSpec(memory_space=space)] * nt,
        out_specs=[pl.BlockSpec(memory_space=space)] * nt,
        out_shape=out_shapes,
        scratch_shapes=[pltpu.SemaphoreType.DMA((7 * nt,)), pltpu.SemaphoreType.DMA((7 * nt,)),
                        pltpu.SemaphoreType.DMA((nt,))],
        compiler_params=pltpu.CompilerParams(vmem_limit_bytes=VMEM_LIMIT, has_side_effects=True),
    )(*shards)


def _pair_exchange(grads, kinds, name):
    nt = len(grads)
    shard_shapes = []
    for g, kind in zip(grads, kinds):
        shard_shapes.append((g.shape[0], g.shape[1] // N_DEV) if kind == "col" else (g.shape[0] // N_DEV, g.shape[1]))

    def body(*refs):
        srcs, lands = refs[:nt], refs[nt:2 * nt]
        send_sems, recv_sems = refs[2 * nt:]
        x, y, c = lax.axis_index("x"), lax.axis_index("y"), lax.axis_index("c")
        copies = []
        for t in range(nt):
            n = shard_shapes[t][1] if kinds[t] == "col" else shard_shapes[t][0]
            for k in range(N_CHIP):
                cp = pltpu.make_async_remote_copy(
                    src_ref=_shard_ref(srcs[t], kinds[t], 2 * k + 1 - c, n), dst_ref=lands[t].at[k],
                    send_sem=send_sems.at[N_CHIP * t + k], recv_sem=recv_sems.at[N_CHIP * t + k],
                    device_id=(x, y, 1 - c), device_id_type=MESH)
                cp.start()
                copies.append(cp)
        for cp in copies:
            cp.wait()

    return pl.pallas_call(
        body, name=name,
        in_specs=[pl.BlockSpec(memory_space=pl.ANY)] * nt,
        out_specs=[pl.BlockSpec(memory_space=pl.ANY)] * nt,
        out_shape=[jax.ShapeDtypeStruct((N_CHIP,) + s, g.dtype) for s, g in zip(shard_shapes, grads)],
        scratch_shapes=[pltpu.SemaphoreType.DMA((N_CHIP * nt,)), pltpu.SemaphoreType.DMA((N_CHIP * nt,))],
        compiler_params=pltpu.CompilerParams(has_side_effects=True),
    )(*grads)


def _pair_sum(grad, land, kind, c_arr, name):
    _, r, w = land.shape
    rb = _tile(r, 1024)
    nrb = r // rb

    def body(c_ref, g_ref, l_ref, o_ref):
        o_ref[...] = (g_ref[...].astype(F32) + l_ref[...].astype(F32)).astype(o_ref.dtype)

    if kind == "col":
        g_spec = pl.BlockSpec((rb, w), lambda k, i, c: (i, 2 * k + c[0]))
    else:
        g_spec = pl.BlockSpec((rb, w), lambda k, i, c: ((2 * k + c[0]) * nrb + i, 0))
    return pl.pallas_call(
        body, name=name,
        grid_spec=pltpu.PrefetchScalarGridSpec(
            num_scalar_prefetch=1, grid=(N_CHIP, nrb),
            in_specs=[g_spec, pl.BlockSpec((None, rb, w), lambda k, i, c: (k, i, 0))],
            out_specs=pl.BlockSpec((None, rb, w), lambda k, i, c: (k, i, 0))),
        out_shape=jax.ShapeDtypeStruct(land.shape, BF16),
        compiler_params=_cparams(("parallel", "parallel")),
    )(c_arr, grad, land)


def _adam(w, g, m, v):
    m = ADAM_B1 * m + (1.0 - ADAM_B1) * g
    v = ADAM_B2 * v + (1.0 - ADAM_B2) * jnp.square(g)
    m_hat = m / (1.0 - ADAM_B1 ** ADAM_STEP)
    v_hat = v / (1.0 - ADAM_B2 ** ADAM_STEP)
    delta = -ADAM_LR * (m_hat / (jnp.sqrt(v_hat) + ADAM_EPS) + ADAM_WD * w)
    return delta, m, v


def _sum_adam(part, land, w, m, v, chip_arr, name):
    r, wd = w.shape
    rb = _tile(r, 256)

    def body(k_ref, own, r1, r2, r3, w_ref, m_ref, v_ref, g_out, d_out, m_out, v_out):
        g = own[...].astype(F32) + r1[...].astype(F32) + r2[...].astype(F32) + r3[...].astype(F32)
        d, mn, vn = _adam(w_ref[...], g, m_ref[...], v_ref[...])
        g_out[...] = g
        d_out[...] = d
        m_out[...] = mn
        v_out[...] = vn

    def slot(off):
        return pl.BlockSpec((None, rb, wd), lambda i, k: ((k[0] + off) % N_CHIP, i, 0))

    plain = pl.BlockSpec((rb, wd), lambda i, k: (i, 0))
    return pl.pallas_call(
        body, name=name,
        grid_spec=pltpu.PrefetchScalarGridSpec(
            num_scalar_prefetch=1, grid=(r // rb,),
            in_specs=[slot(0), slot(1), slot(2), slot(3), plain, plain, plain],
            out_specs=[plain] * 4),
        out_shape=[jax.ShapeDtypeStruct(w.shape, F32)] * 4,
        compiler_params=_cparams(("parallel",)),
    )(chip_arr, part, land, land, land, w, m, v)


def _small_sum(gathered, name):
    _, r, w = gathered.shape

    def body(g_ref, o_ref):
        acc = g_ref[0]
        for d in range(1, N_DEV):
            acc = acc + g_ref[d]
        o_ref[...] = acc

    return pl.pallas_call(body, name=name, out_shape=jax.ShapeDtypeStruct((r, w), F32))(gathered)


def _small_adam(ws, gs, ms, vs, name):
    n = len(ws)

    def body(*refs):
        ins, outs = refs[:4 * n], refs[4 * n:]
        for t in range(n):
            d, mn, vn = _adam(ins[t][...], ins[n + t][...], ins[2 * n + t][...], ins[3 * n + t][...])
            outs[t][...] = d
            outs[n + t][...] = mn
            outs[2 * n + t][...] = vn

    shapes = [jax.ShapeDtypeStruct(w.shape, F32) for w in ws]
    return pl.pallas_call(body, name=name, out_shape=shapes * 3)(*ws, *gs, *ms, *vs)


def _mixer_fwd(x, shards, ln_v_g, ln_v_b, w_s, b_s3, conv_w, conv_b):
    t_len, d = x.shape
    hd = d // HEADS
    tm = _tile(t_len, 1024)
    nt = t_len // tm
    assert nt >= 2
    w8 = shards[0].shape[0]
    bps = w8 // hd
    r8 = shards[1].shape[0]
    nq = N_IN * HEADS
    n_blocks = nq + 3 * N_DEV
    groups = [[("in", HEADS * b + HEADS - 1 - p) for b in range(N_MIX)] for p in range(HEADS)]
    groups.append([("in", q) for q in range(N_MIX * HEADS, nq)] + [(t, s) for t in (1, 2, 3) for s in range(N_DEV)])

    def owner_of(blk):
        return blk[1] // bps if blk[0] == "in" else blk[1]

    send_step = {}
    for s in range(N_DEV):
        mine = [blk for grp in groups for blk in grp if owner_of(blk) == s]
        for pos, blk in enumerate(mine):
            send_step[blk] = 0 if pos < FIRST_SENDS else 1 + (pos - FIRST_SENDS) // SENDS_PER_GROUP
    assert max(send_step.values()) < HEADS
    assert all(send_step[blk] <= max(k - 1, 0) for k in range(HEADS) for blk in groups[k])
    tail_pass_step = nt - 6 if nt >= 8 else nt - 2

    def body(x_ref, sh_in, sh_oa, sh_ob, sh_out, lng, lnb, ws_ref, bs_ref, cw_ref, cb_ref,
             save_ref, ya_ref, yb_ref, f_in, f_oa, f_ob, f_out,
             hbuf, wbuf, recv_sems, own_sems, fwd_sems, local_sems, load_sems):
        g = pl.program_id(0)
        i = pl.program_id(1)
        x, y, c = lax.axis_index("x"), lax.axis_index("y"), lax.axis_index("c")
        sibling = (x, y, 1 - c)
        chips = [(1 - x, y), (x, 1 - y), (1 - x, 1 - y)]
        shard_refs = (sh_in, sh_oa, sh_ob, sh_out)
        fulls = (f_in, f_oa, f_ob, f_out)

        def tensor(blk):
            return 0 if blk[0] == "in" else blk[0]

        def owner(blk):
            s = owner_of(blk)
            return s // 4, (s // 2) % 2, s % 2

        def bid(blk):
            return blk[1] if blk[0] == "in" else nq + (blk[0] - 1) * N_DEV + blk[1]

        def region(blk):
            if blk[0] == "in":
                return f_in.at[pl.ds(blk[1] * hd, hd), :]
            return fulls[blk[0]].at[pl.ds(blk[1] * r8, r8), :]

        def own_src(blk):
            if blk[0] == "in":
                return sh_in.at[pl.ds((blk[1] % bps) * hd, hd), :]
            return shard_refs[blk[0]]

        def rcopy(blk, to, send_sem, own):
            return pltpu.make_async_remote_copy(
                src_ref=own_src(blk) if own else region(blk), dst_ref=region(blk),
                send_sem=send_sem, recv_sem=recv_sems.at[bid(blk)], device_id=to, device_id_type=MESH)

        def send_own(blk):
            ox, oy, oc = owner(blk)

            @pl.when((x == ox) & (y == oy) & (c == oc))
            def _():
                rcopy(blk, sibling, own_sems.at[tensor(blk)], True).start()
                for chip in chips:
                    rcopy(blk, (*chip, c), own_sems.at[tensor(blk)], True).start()

        def pass_on(blk):
            ox, oy, oc = owner(blk)

            @pl.when(((x != ox) | (y != oy)) & (c == oc))
            def _():
                rcopy(blk, sibling, fwd_sems.at[tensor(blk)], False).wait_recv()
                rcopy(blk, sibling, fwd_sems.at[tensor(blk)], False).start()

        def wait_from_sibling(blk):
            @pl.when(c != owner(blk)[2])
            def _():
                rcopy(blk, sibling, fwd_sems.at[tensor(blk)], False).wait_recv()

        def local_copy(t):
            me = 4 * x + 2 * y + c
            return pltpu.make_async_copy(shard_refs[t], _shard_ref(fulls[t], "row", me, w8 if t == 0 else r8),
                                         local_sems.at[t])

        first = (g == 0) & (i == 0)

        @pl.when(first)
        def _():
            for t in range(4):
                local_copy(t).start()
            for grp in groups:
                for blk in grp:
                    if send_step[blk] == 0:
                        send_own(blk)
            for blk in groups[0]:
                pass_on(blk)

        @pl.when(i == 0)
        def _():
            for k in range(1, HEADS):
                @pl.when(g == k)
                def _(k=k):
                    for grp in groups:
                        for blk in grp:
                            if send_step[blk] == k:
                                send_own(blk)

        @pl.when(i == nt - 2)
        def _():
            for k in range(HEADS - 1):
                @pl.when(g == k)
                def _(k=k):
                    for blk in groups[k + 1]:
                        pass_on(blk)

        @pl.when((g == HEADS - 1) & (i == tail_pass_step))
        def _():
            for blk in groups[HEADS]:
                pass_on(blk)

        def fetch_weights(k):
            for blk in groups[k]:
                wait_from_sibling(blk)
            for b, blk in enumerate(groups[k]):
                ox, oy, oc = owner(blk)
                mine = (x == ox) & (y == oy) & (c == oc)

                @pl.when(mine)
                def _(b=b, blk=blk):
                    pltpu.make_async_copy(own_src(blk), wbuf.at[k % 2, b], load_sems.at[b]).start()

                @pl.when(jnp.logical_not(mine))
                def _(b=b, blk=blk):
                    pltpu.make_async_copy(region(blk), wbuf.at[k % 2, b], load_sems.at[b]).start()

        @pl.when(i == 0)
        def _():
            for k in range(HEADS):
                @pl.when(g == k)
                def _(k=k):
                    if k == 0:
                        fetch_weights(0)
                    for b, blk in enumerate(groups[k]):
                        pltpu.make_async_copy(region(blk), wbuf.at[k % 2, b], load_sems.at[b]).wait()

        @pl.when(i == nt - 1)
        def _():
            for k in range(1, HEADS):
                @pl.when(g == k - 1)
                def _(k=k):
                    fetch_weights(k)

        xt = x_ref[...].astype(BF16)
        wslot = g % 2
        u = _dot_nt(xt, wbuf[wslot, 0])
        v = _dot_nt(xt, wbuf[wslot, 1])
        z = _dot_nt(xt, wbuf[wslot, 2])
        save_ref[0] = u.astype(BF16)
        save_ref[1] = v.astype(BF16)
        save_ref[2] = z.astype(BF16)
        gu, _ = _gelu(u)
        gv, _ = _gelu(v)
        mu = jnp.mean(gv, axis=-1, keepdims=True)
        dv = gv - mu
        var = jnp.mean(dv * dv, axis=-1, keepdims=True)
        vn = (dv * lax.rsqrt(var + LN_EPS) * lng[...] + lnb[...]).astype(BF16)
        sz, _ = _silu(z)
        gate = gu * sz
        wm = jnp.where(_tril_mask(), ws_ref[0], 0.0).astype(BF16)
        bs = bs_ref[0]
        for ck in range(tm // CHUNK):
            rows = slice(ck * CHUNK, (ck + 1) * CHUNK)
            mixed = _dot(wm, vn[rows]) + bs
            ya_ref[rows, :] = (gate[rows] * mixed).astype(BF16)

        xbv = _dot_nt(xt, wbuf[wslot, 3])
        cbv = _dot_nt(xt, wbuf[wslot, 4])
        bbv = _dot_nt(xt, wbuf[wslot, 5])
        zbv = _dot_nt(xt, wbuf[wslot, 6])
        save_ref[3] = xbv.astype(BF16)
        save_ref[4] = cbv.astype(BF16)
        save_ref[5] = bbv.astype(BF16)
        save_ref[6] = zbv.astype(BF16)
        h = cbv * xbv

        @pl.when(i == 0)
        def _():
            hbuf[0:SUBLANES, :] = jnp.zeros((SUBLANES, hd), F32)

        hbuf[SUBLANES:SUBLANES + tm, :] = h
        h1 = hbuf[SUBLANES - 1:SUBLANES - 1 + tm, :]
        h2 = hbuf[SUBLANES - 2:SUBLANES - 2 + tm, :]
        conv = cb_ref[...] + cw_ref[0:1, :] * h2 + cw_ref[1:2, :] * h1 + cw_ref[2:3, :] * h
        hbuf[0:SUBLANES, :] = h[tm - SUBLANES:tm, :]
        save_ref[7] = conv.astype(BF16)
        szb, _ = _silu(zbv)
        yb_ref[...] = (bbv * conv * szb).astype(BF16)

        @pl.when((g == HEADS - 1) & (i == nt - 1))
        def _():
            for blk in groups[HEADS]:
                wait_from_sibling(blk)
            for t in range(4):
                local_copy(t).wait()
            for t in range(4):
                n = w8 if t == 0 else r8
                own_all, fwd_all = fulls[t].at[pl.ds(0, 4 * n), :], fulls[t].at[pl.ds(0, 3 * n), :]
                for ref, sem in ((own_all, own_sems.at[t]), (fwd_all, fwd_sems.at[t])):
                    pltpu.make_async_remote_copy(src_ref=ref, dst_ref=ref, send_sem=sem, recv_sem=sem,
                                                 device_id=sibling, device_id_type=MESH).wait_send()

    def head(g):
        return HEADS - 1 - g

    vec = pl.BlockSpec((1, hd), lambda g, i: (0, head(g)))
    hbm = pl.BlockSpec(memory_space=pl.ANY)
    return pl.pallas_call(
        body, name="mixer_fwd", grid=(HEADS, nt),
        in_specs=[pl.BlockSpec((tm, d), lambda g, i: (i, 0)), hbm, hbm, hbm, hbm,
                  vec, vec,
                  pl.BlockSpec((1, CHUNK, CHUNK), lambda g, i: (head(g), 0, 0)),
                  pl.BlockSpec((1, CHUNK, 1), lambda g, i: (head(g), 0, 0)),
                  pl.BlockSpec((3, hd), lambda g, i: (0, head(g))),
                  vec],
        out_specs=[pl.BlockSpec((N_SAVE, tm, hd), lambda g, i: (0, i, head(g))),
                   pl.BlockSpec((tm, hd), lambda g, i: (i, head(g))),
                   pl.BlockSpec((tm, hd), lambda g, i: (i, head(g))),
                   hbm, hbm, hbm, hbm],
        out_shape=[jax.ShapeDtypeStruct((N_SAVE, t_len, d), BF16),
                   jax.ShapeDtypeStruct((t_len, d), BF16),
                   jax.ShapeDtypeStruct((t_len, d), BF16),
                   jax.ShapeDtypeStruct((N_DEV * w8, d), BF16),
                   jax.ShapeDtypeStruct((d, d), BF16), jax.ShapeDtypeStruct((d, d), BF16),
                   jax.ShapeDtypeStruct((d, d), BF16)],
        scratch_shapes=[pltpu.VMEM((SUBLANES + tm, hd), F32), pltpu.VMEM((2, N_MIX, hd, d), BF16),
                        pltpu.SemaphoreType.DMA((n_blocks,)), pltpu.SemaphoreType.DMA((4,)),
                        pltpu.SemaphoreType.DMA((4,)), pltpu.SemaphoreType.DMA((4,)),
                        pltpu.SemaphoreType.DMA((N_MIX,))],
        compiler_params=_cparams(("arbitrary", "arbitrary")),
    )(x, *shards, ln_v_g, ln_v_b, w_s, b_s3, conv_w, conv_b)


def _merge_fwd(x, ya, yb, w_in, w_oa, w_ob, b_gate):
    t_len, d = x.shape
    tm = _tile(t_len, 256)
    tn = _tile(d, 1024)
    nj = d // tn

    def body(x_ref, ya_ref, yb_ref, wga, wgb, woa, wob, bga, bgb, sa_ref, sb_ref, pa_ref, pb_ref, mg_ref):
        xt = x_ref[...].astype(BF16)
        sa = jax.nn.sigmoid(_dot_nt(xt, wga[...]) + bga[...])
        sb = jax.nn.sigmoid(_dot_nt(xt, wgb[...]) + bgb[...])
        pa = _dot(ya_ref[...], woa[...])
        pb = _dot(yb_ref[...], wob[...])
        sa_ref[...] = sa.astype(BF16)
        sb_ref[...] = sb.astype(BF16)
        pa_ref[...] = pa.astype(BF16)
        pb_ref[...] = pb.astype(BF16)
        mg_ref[...] = (sa * pa + sb * pb).astype(BF16)

    row = pl.BlockSpec((tm, d), lambda j, i: (i, 0))
    out = pl.BlockSpec((tm, tn), lambda j, i: (i, j))
    return pl.pallas_call(
        body, name="merge_fwd", grid=(nj, t_len // tm),
        in_specs=[row, row, row,
                  pl.BlockSpec((tn, d), lambda j, i: (7 * nj + j, 0)),
                  pl.BlockSpec((tn, d), lambda j, i: (8 * nj + j, 0)),
                  pl.BlockSpec((d, tn), lambda j, i: (0, j)),
                  pl.BlockSpec((d, tn), lambda j, i: (0, j)),
                  pl.BlockSpec((1, tn), lambda j, i: (0, j)),
                  pl.BlockSpec((1, tn), lambda j, i: (0, nj + j))],
        out_specs=[out] * 5,
        out_shape=[jax.ShapeDtypeStruct((t_len, d), BF16)] * 5,
        compiler_params=_cparams(("parallel", "arbitrary")),
    )(x, ya, yb, w_in, w_in, w_oa, w_ob, b_gate, b_gate)


def _out_ln_loss(merged, w_out, x, target, ln_g, ln_b, sa, sb, pa, pb):
    t_len, d = x.shape
    tm = _tile(t_len, 256)
    nt = t_len // tm

    def body(mg_ref, w_ref, x_ref, t_ref, g_ref, b_ref, sa_ref, sb_ref, pa_ref, pb_ref,
             dz_ref, gx_ref, xb_ref, da_ref, db_ref, dg_ref, glg_ref, glb_ref, ls_ref, ga_ref, gb_ref,
             a_g, a_b, a_l, acc_a, acc_b):
        i = pl.program_id(0)
        xb_ref[...] = x_ref[...].astype(BF16)

        @pl.when(i == 0)
        def _():
            for a in (a_g, a_b, a_l, acc_a, acc_b):
                a[...] = jnp.zeros_like(a)

        zres = DN_ALPHA * x_ref[...] + _dot(mg_ref[...], w_ref[...])
        mu = jnp.mean(zres, axis=-1, keepdims=True)
        dc = zres - mu
        var = jnp.mean(dc * dc, axis=-1, keepdims=True)
        rstd = lax.rsqrt(var + LN_EPS)
        xhat = dc * rstd
        g = g_ref[...]
        err = xhat * g + b_ref[...] - t_ref[...]
        dy = err * (1.0 / d)
        a_l[...] += _fold8(err * err)
        a_g[...] += _fold8(dy * xhat)
        a_b[...] += _fold8(dy)
        dxh = dy * g
        m1 = jnp.mean(dxh, axis=-1, keepdims=True)
        m2 = jnp.mean(dxh * xhat, axis=-1, keepdims=True)
        dz = rstd * (dxh - m1 - xhat * m2)
        dz_bf = dz.astype(BF16)
        dz_ref[...] = dz_bf
        gx_ref[...] = DN_ALPHA * dz

        dm = _dot_nt(dz_bf, w_ref[...])
        sa = sa_ref[...].astype(F32)
        sb = sb_ref[...].astype(F32)
        da = dm * sa
        db = dm * sb
        da_ref[...] = da.astype(BF16)
        db_ref[...] = db.astype(BF16)
        dga = da * pa_ref[...].astype(F32) * (1.0 - sa)
        dgb = db * pb_ref[...].astype(F32) * (1.0 - sb)
        dg_ref[0] = dga.astype(BF16)
        dg_ref[1] = dgb.astype(BF16)
        acc_a[...] += _fold8(dga)
        acc_b[...] += _fold8(dgb)

        @pl.when(i == nt - 1)
        def _():
            glg_ref[...] = jnp.sum(a_g[...], axis=0, keepdims=True)
            glb_ref[...] = jnp.sum(a_b[...], axis=0, keepdims=True)
            ls_ref[...] = jnp.sum(a_l[...], axis=0, keepdims=True)
            ga_ref[...] = jnp.sum(acc_a[...], axis=0, keepdims=True)
            gb_ref[...] = jnp.sum(acc_b[...], axis=0, keepdims=True)

    row = pl.BlockSpec((tm, d), lambda i: (i, 0))
    vec = pl.BlockSpec((1, d), lambda i: (0, 0))
    act = jax.ShapeDtypeStruct((t_len, d), BF16)
    part = jax.ShapeDtypeStruct((1, d), F32)
    return pl.pallas_call(
        body, name="out_ln_loss", grid=(nt,),
        in_specs=[row, pl.BlockSpec((d, d), lambda i: (0, 0)), row, row, vec, vec, row, row, row, row],
        out_specs=[row, row, row, row, row, pl.BlockSpec((2, tm, d), lambda i: (0, i, 0)), vec, vec, vec, vec, vec],
        out_shape=[act, jax.ShapeDtypeStruct((t_len, d), F32), act, act, act,
                   jax.ShapeDtypeStruct((2, t_len, d), BF16), part, part, part, part, part],
        scratch_shapes=[pltpu.VMEM((SUBLANES, d), F32)] * 5,
        compiler_params=_cparams(("arbitrary",)),
    )(merged, w_out, x, target, ln_g, ln_b, sa, sb, pa, pb)


def _grad_w(a, b, name, exchange=None):
    t_len, m = a.shape
    n = b.shape[1]
    tm, tn, tk = _tile(m, MM_TILE), _tile(n, MM_TILE), _tile(t_len, MM_DEPTH_TOKENS)
    ni, nj, nk = m // tm, n // tn, t_len // tk

    def body(a_ref, b_ref, *rest):
        if exchange is None:
            o_ref, acc = rest
        else:
            src, o_ref, land, acc, send_sems, recv_sems = rest
        i, j, k = pl.program_id(0), pl.program_id(1), pl.program_id(2)

        def copies():
            x, y, c = lax.axis_index("x"), lax.axis_index("y"), lax.axis_index("c")
            w = exchange.shape[1] // N_DEV
            return [pltpu.make_async_remote_copy(
                src_ref=_shard_ref(src, "col", 2 * s + 1 - c, w), dst_ref=land.at[s],
                send_sem=send_sems.at[s], recv_sem=recv_sems.at[s],
                device_id=(x, y, 1 - c), device_id_type=MESH) for s in range(N_CHIP)]

        if exchange is not None:
            @pl.when((i == 0) & (j == 0) & (k == 0))
            def _():
                for cp in copies():
                    cp.start()

        @pl.when(k == 0)
        def _():
            acc[...] = jnp.zeros_like(acc)

        acc[...] += _dot_tn(a_ref[...], b_ref[...])

        @pl.when(k == nk - 1)
        def _():
            o_ref[...] = acc[...].astype(BF16)

        if exchange is not None:
            @pl.when((i == ni - 1) & (j == nj - 1) & (k == nk - 1))
            def _():
                for cp in copies():
                    cp.wait()

    in_specs = [pl.BlockSpec((tk, tm), lambda i, j, k: (k, i)), pl.BlockSpec((tk, tn), lambda i, j, k: (k, j))]
    out_specs = [pl.BlockSpec((tm, tn), lambda i, j, k: (i, j))]
    out_shape = [jax.ShapeDtypeStruct((m, n), BF16)]
    scratch = [pltpu.VMEM((tm, tn), F32)]
    args = [a, b]
    if exchange is not None:
        hbm = pl.BlockSpec(memory_space=pl.ANY)
        in_specs.append(hbm)
        out_specs.append(hbm)
        out_shape.append(jax.ShapeDtypeStruct((N_CHIP, exchange.shape[0], exchange.shape[1] // N_DEV), exchange.dtype))
        scratch += [pltpu.SemaphoreType.DMA((N_CHIP,)), pltpu.SemaphoreType.DMA((N_CHIP,))]
        args.append(exchange)
    outs = pl.pallas_call(
        body, name=name, grid=(ni, nj, nk),
        in_specs=in_specs, out_specs=out_specs, out_shape=out_shape, scratch_shapes=scratch,
        compiler_params=_cparams(("arbitrary", "arbitrary", "arbitrary")),
    )(*args)
    return outs[0] if exchange is None else (outs[0], outs[1])


def _mixer_bwd(da, db, w_oa, w_ob, saved, dgate, ln_v_g, ln_v_b, w_s, b_s3, conv_w):
    t_len, d = da.shape
    hd = d // HEADS
    tm = _tile(t_len, 1024)
    nt = t_len // tm

    def body(da_ref, db_ref, woa, wob, sv, dgt, lng, lnb, ws_ref, bs_ref, cw_ref,
             dp_ref, gws_ref, gbs_ref, glg_ref, glb_ref, gcw_ref, gcb_ref,
             dbuf, carry, a_ws, a_bs, a_lg, a_lb, a_c0, a_c1, a_c2, a_cb):
        i = pl.program_id(1)

        @pl.when(i == 0)
        def _():
            carry[...] = jnp.zeros_like(carry)
            for a in (a_ws, a_bs, a_lg, a_lb, a_c0, a_c1, a_c2, a_cb):
                a[...] = jnp.zeros_like(a)

        dya_t = _dot_nt(da_ref[...], woa[...])
        dyb_t = _dot_nt(db_ref[...], wob[...])
        g = lng[...]
        beta = lnb[...]
        wm = jnp.where(_tril_mask(), ws_ref[0], 0.0).astype(BF16)
        bs = bs_ref[0]
        w0, w1, w2 = cw_ref[0:1, :], cw_ref[1:2, :], cw_ref[2:3, :]
        dbuf[tm:tm + SUBLANES, :] = carry[...]
        gws = a_ws[...]
        gbs = a_bs[...]
        lg, lb = a_lg[...], a_lb[...]
        c0, c1, c2, cb = a_c0[...], a_c1[...], a_c2[...], a_cb[...]
        for ck in reversed(range(tm // CHUNK)):
            r0 = ck * CHUNK
            rows = slice(r0, r0 + CHUNK)
            dya = dya_t[rows]
            u = sv[0, rows, :].astype(F32)
            v = sv[1, rows, :].astype(F32)
            z = sv[2, rows, :].astype(F32)
            gu, gu_grad = _gelu(u)
            gv, gv_grad = _gelu(v)
            mu = jnp.mean(gv, axis=-1, keepdims=True)
            dvc = gv - mu
            var = jnp.mean(dvc * dvc, axis=-1, keepdims=True)
            rstd = lax.rsqrt(var + LN_EPS)
            vhat = dvc * rstd
            vn = (vhat * g + beta).astype(BF16)
            sz, sz_grad = _silu(z)
            t1 = dya * sz
            dmixed = t1 * gu
            dmixed_b = dmixed.astype(BF16)
            mixed = _dot(wm, vn) + bs
            gws = gws + _dot_nt(dmixed_b, vn)
            gbs = gbs + dmixed
            dvn = _dot_tn(wm, dmixed_b)
            dp_ref[0, rows, :] = (t1 * mixed * gu_grad).astype(BF16)
            dp_ref[2, rows, :] = (dya * gu * mixed * sz_grad).astype(BF16)
            lg = lg + _fold8(dvn * vhat)
            lb = lb + _fold8(dvn)
            dvh = dvn * g
            m1 = jnp.mean(dvh, axis=-1, keepdims=True)
            m2 = jnp.mean(dvh * vhat, axis=-1, keepdims=True)
            dp_ref[1, rows, :] = (rstd * (dvh - m1 - vhat * m2) * gv_grad).astype(BF16)
            dyb = dyb_t[rows]
            xbv = sv[3, rows, :].astype(F32)
            cbv = sv[4, rows, :].astype(F32)
            bbv = sv[5, rows, :].astype(F32)
            zbv = sv[6, rows, :].astype(F32)
            conv = sv[7, rows, :].astype(F32)
            szb, szb_grad = _silu(zbv)
            dp_ref[5, rows, :] = (dyb * conv * szb).astype(BF16)
            dp_ref[6, rows, :] = (dyb * bbv * conv * szb_grad).astype(BF16)
            dconv = dyb * bbv * szb
            dbuf[r0:r0 + CHUNK, :] = dconv
            dc1 = dbuf[r0 + 1:r0 + 1 + CHUNK, :]
            dc2 = dbuf[r0 + 2:r0 + 2 + CHUNK, :]
            if ck == 0:
                carry[...] = dconv[0:SUBLANES, :]
            h = cbv * xbv
            c2 = c2 + _fold8(dconv * h)
            c1 = c1 + _fold8(dc1 * h)
            c0 = c0 + _fold8(dc2 * h)
            cb = cb + _fold8(dconv)
            dh = w2 * dconv + w1 * dc1 + w0 * dc2
            dp_ref[3, rows, :] = (dh * cbv).astype(BF16)
            dp_ref[4, rows, :] = (dh * xbv).astype(BF16)
            dp_ref[7, rows, :] = dgt[0, rows, :]
            dp_ref[8, rows, :] = dgt[1, rows, :]
        a_ws[...] = gws
        a_bs[...] = gbs
        a_lg[...], a_lb[...] = lg, lb
        a_c0[...], a_c1[...], a_c2[...], a_cb[...] = c0, c1, c2, cb

        @pl.when(i == nt - 1)
        def _():
            gws_ref[0] = jnp.where(_tril_mask(), a_ws[...], 0.0)
            gbs_ref[0] = jnp.sum(a_bs[...], axis=1, keepdims=True)
            glg_ref[...] = jnp.sum(a_lg[...], axis=0, keepdims=True)
            glb_ref[...] = jnp.sum(a_lb[...], axis=0, keepdims=True)
            gcw_ref[0:1, :] = jnp.sum(a_c0[...], axis=0, keepdims=True)
            gcw_ref[1:2, :] = jnp.sum(a_c1[...], axis=0, keepdims=True)
            gcw_ref[2:3, :] = jnp.sum(a_c2[...], axis=0, keepdims=True)
            gcb_ref[...] = jnp.sum(a_cb[...], axis=0, keepdims=True)

    def rev(i):
        return nt - 1 - i

    row = pl.BlockSpec((tm, d), lambda g, i: (rev(i), 0))
    wrow = pl.BlockSpec((hd, d), lambda g, i: (g, 0))
    vec = pl.BlockSpec((1, hd), lambda g, i: (0, g))
    acc8 = pltpu.VMEM((SUBLANES, hd), F32)
    return pl.pallas_call(
        body, name="mixer_bwd", grid=(HEADS, nt),
        in_specs=[row, row, wrow, wrow,
                  pl.BlockSpec((N_SAVE, tm, hd), lambda g, i: (0, rev(i), g)),
                  pl.BlockSpec((2, tm, hd), lambda g, i: (0, rev(i), g)),
                  vec, vec,
                  pl.BlockSpec((1, CHUNK, CHUNK), lambda g, i: (g, 0, 0)),
                  pl.BlockSpec((1, CHUNK, 1), lambda g, i: (g, 0, 0)),
                  pl.BlockSpec((3, hd), lambda g, i: (0, g))],
        out_specs=[pl.BlockSpec((N_IN, tm, hd), lambda g, i: (0, rev(i), g)),
                   pl.BlockSpec((1, CHUNK, CHUNK), lambda g, i: (g, 0, 0)),
                   pl.BlockSpec((1, CHUNK, 1), lambda g, i: (g, 0, 0)),
                   vec, vec,
                   pl.BlockSpec((3, hd), lambda g, i: (0, g)),
                   vec],
        out_shape=[jax.ShapeDtypeStruct((N_IN, t_len, d), BF16),
                   jax.ShapeDtypeStruct((HEADS, CHUNK, CHUNK), F32),
                   jax.ShapeDtypeStruct((HEADS, CHUNK, 1), F32),
                   jax.ShapeDtypeStruct((1, d), F32), jax.ShapeDtypeStruct((1, d), F32),
                   jax.ShapeDtypeStruct((3, d), F32), jax.ShapeDtypeStruct((1, d), F32)],
        scratch_shapes=[pltpu.VMEM((tm + SUBLANES, hd), F32), acc8,
                        pltpu.VMEM((CHUNK, CHUNK), F32), pltpu.VMEM((CHUNK, hd), F32),
                        acc8, acc8, acc8, acc8, acc8, acc8],
        compiler_params=_cparams(("parallel", "arbitrary")),
    )(da, db, w_oa, w_ob, saved, dgate, ln_v_g, ln_v_b, w_s, b_s3, conv_w)


def _grad_w_in(xb, dp):
    t_len, d = xb.shape
    tm, tn, tk = _tile(d, MM_TILE), _tile(d, MM_TILE), _tile(t_len, MM_DEPTH_TOKENS)
    nj = d // tn
    nk = t_len // tk

    def body(a_ref, b_ref, o_ref, acc):
        k = pl.program_id(2)

        @pl.when(k == 0)
        def _():
            acc[...] = jnp.zeros_like(acc)

        acc[...] += _dot_tn(a_ref[...], b_ref[...])

        @pl.when(k == nk - 1)
        def _():
            o_ref[...] = acc[...].astype(BF16)

    return pl.pallas_call(
        body, name="grad_w_in", grid=(d // tm, N_IN * nj, nk),
        in_specs=[pl.BlockSpec((tk, tm), lambda i, j, k: (k, i)),
                  pl.BlockSpec((None, tk, tn), lambda i, j, k: (j // nj, k, j % nj))],
        out_specs=pl.BlockSpec((tm, tn), lambda i, j, k: (i, j)),
        out_shape=jax.ShapeDtypeStruct((d, N_IN * d), BF16),
        scratch_shapes=[pltpu.VMEM((tm, tn), F32)],
        compiler_params=_cparams(("parallel", "parallel", "arbitrary")),
    )(xb, dp)


def _grad_x_tail(dp, w_in, partial, slots):
    _, t_len, d = dp.shape
    lo, hi = slots
    tm, tn, tk = _tile(t_len, MM_TILE), _tile(d, MM_TILE), _tile(d, MM_DEPTH)
    nkb = d // tk
    nk = (hi - lo) * nkb

    def body(a_ref, b_ref, r_ref, o_ref, acc):
        k = pl.program_id(2)

        @pl.when(k == 0)
        def _():
            acc[...] = r_ref[...]

        acc[...] += _dot(a_ref[...], b_ref[...])

        @pl.when(k == nk - 1)
        def _():
            o_ref[...] = acc[...]

    return pl.pallas_call(
        body, name="grad_x_tail", grid=(t_len // tm, d // tn, nk),
        in_specs=[pl.BlockSpec((None, tm, tk), lambda i, j, k: (lo + k // nkb, i, k % nkb)),
                  pl.BlockSpec((tk, tn), lambda i, j, k: (lo * nkb + k, j)),
                  pl.BlockSpec((tm, tn), lambda i, j, k: (i, j))],
        out_specs=pl.BlockSpec((tm, tn), lambda i, j, k: (i, j)),
        out_shape=jax.ShapeDtypeStruct((t_len, d), F32),
        scratch_shapes=[pltpu.VMEM((tm, tn), F32)],
        compiler_params=_cparams(("parallel", "parallel", "arbitrary")),
    )(dp, w_in, partial)


def _grad_x(dp, w_in, gx_direct, parts, packed, slots):
    _, t_len, d = dp.shape
    tm, tn, tk = _tile(t_len, MM_TILE), _tile(d, MM_TILE), _tile(d, MM_DEPTH)
    nkb = d // tk
    lo, hi = slots
    nk = (hi - lo) * nkb
    ni, nj = t_len // tm, d // tn
    n_parts = len(parts)

    def body(a_ref, b_ref, r_ref, *rest):
        srcs = rest[:n_parts]
        pk_ref = rest[n_parts]
        o_ref = rest[n_parts + 1]
        lands = rest[n_parts + 2:2 * n_parts + 2]
        gath = rest[2 * n_parts + 2]
        acc, send_sems, recv_sems, pk_send, pk_recv, pk_local = rest[2 * n_parts + 3:]
        i, j, k = pl.program_id(0), pl.program_id(1), pl.program_id(2)
        x, y, c = lax.axis_index("x"), lax.axis_index("y"), lax.axis_index("c")
        my_chip = 2 * x + y
        me = 4 * x + 2 * y + c
        chips = [(1 - x, y), (x, 1 - y), (1 - x, 1 - y)]

        def part_copy(t, n):
            px, py = chips[n]
            return pltpu.make_async_remote_copy(
                src_ref=srcs[t].at[2 * px + py], dst_ref=lands[t].at[my_chip],
                send_sem=send_sems.at[3 * t + n], recv_sem=recv_sems.at[3 * t + n],
                device_id=(px, py, c), device_id_type=MESH)

        def part_landing(t, n):
            px, py = chips[n]
            return pltpu.make_async_remote_copy(
                src_ref=srcs[t].at[my_chip], dst_ref=lands[t].at[2 * px + py],
                send_sem=send_sems.at[3 * t + n], recv_sem=recv_sems.at[3 * t + n],
                device_id=(px, py, c), device_id_type=MESH)

        def pk_copy(s):
            return pltpu.make_async_remote_copy(
                src_ref=pk_ref, dst_ref=gath.at[me], send_sem=pk_send, recv_sem=pk_recv.at[me],
                device_id=(s // 4, (s // 2) % 2, s % 2), device_id_type=MESH)

        @pl.when((i == 0) & (j == 0) & (k == 0))
        def _():
            for t in range(n_parts):
                for n in range(3):
                    part_copy(t, n).start()
            pltpu.make_async_copy(pk_ref, gath.at[me], pk_local).start()
            for s in range(N_DEV):
                @pl.when(s != me)
                def _(s=s):
                    pk_copy(s).start()

        @pl.when(k == 0)
        def _():
            acc[...] = r_ref[...]

        acc[...] += _dot(a_ref[...], b_ref[...])

        @pl.when(k == nk - 1)
        def _():
            o_ref[...] = acc[...]

        @pl.when((i == ni - 1) & (j == nj - 1) & (k == nk - 1))
        def _():
            for t in range(n_parts):
                for n in range(3):
                    part_landing(t, n).wait_recv()
            for t in range(n_parts):
                for n in range(3):
                    part_copy(t, n).wait_send()
            for s in range(N_DEV):
                @pl.when(s != me)
                def _(s=s):
                    pltpu.make_async_remote_copy(
                        src_ref=pk_ref, dst_ref=gath.at[s], send_sem=pk_send, recv_sem=pk_recv.at[s],
                        device_id=(s // 4, (s // 2) % 2, s % 2), device_id_type=MESH).wait_recv()
            seven = gath.at[pl.ds(0, N_DEV - 1)]
            pltpu.make_async_remote_copy(src_ref=seven, dst_ref=seven, send_sem=pk_send, recv_sem=pk_send,
                                         device_id=(x, y, 1 - c), device_id_type=MESH).wait_send()
            pltpu.make_async_copy(pk_ref, gath.at[me], pk_local).wait()

    hbm = pl.BlockSpec(memory_space=pl.ANY)
    outs = pl.pallas_call(
        body, name="grad_x", grid=(ni, nj, nk),
        in_specs=[pl.BlockSpec((None, tm, tk), lambda i, j, k: (lo + k // nkb, i, k % nkb)),
                  pl.BlockSpec((tk, tn), lambda i, j, k: (lo * nkb + k, j)),
                  pl.BlockSpec((tm, tn), lambda i, j, k: (i, j))] + [hbm] * (n_parts + 1),
        out_specs=[pl.BlockSpec((tm, tn), lambda i, j, k: (i, j))] + [hbm] * (n_parts + 1),
        out_shape=[jax.ShapeDtypeStruct((t_len, d), F32)] + [jax.ShapeDtypeStruct(p.shape, p.dtype) for p in parts]
        + [jax.ShapeDtypeStruct((N_DEV,) + packed.shape, packed.dtype)],
        scratch_shapes=[pltpu.VMEM((tm, tn), F32),
                        pltpu.SemaphoreType.DMA((3 * n_parts,)), pltpu.SemaphoreType.DMA((3 * n_parts,)),
                        pltpu.SemaphoreType.DMA(()), pltpu.SemaphoreType.DMA((N_DEV,)), pltpu.SemaphoreType.DMA(())],
        compiler_params=_cparams(("arbitrary", "arbitrary", "arbitrary")),
    )(dp, w_in, gx_direct, *parts, packed)
    return outs[0], list(outs[1:1 + n_parts]), outs[1 + n_parts]


def kernel(x, w_in, b_gate, ln_v_g, ln_v_b, w_s, b_s, conv_w, conv_b, w_oa, w_ob, w_out, ln_g, ln_b, loss_target, m_w_in, m_b_gate, m_ln_v_g, m_ln_v_b, m_w_s, m_b_s, m_conv_w, m_conv_b, m_w_oa, m_w_ob, m_w_out, m_ln_g, m_ln_b, v_w_in, v_b_gate, v_ln_v_g, v_ln_v_b, v_w_s, v_b_s, v_conv_w, v_conv_b, v_w_oa, v_w_ob, v_w_out, v_ln_g, v_ln_b):
    _, t_len, d = x.shape
    assert d % (HEADS * 128) == 0 and t_len % CHUNK == 0 and w_in.shape[2] * N_DEV == N_IN * d
    x2 = x[0]
    tgt2 = loss_target[0]
    c_arr = lax.axis_index("c").astype(jnp.int32).reshape(1)
    chip_arr = (2 * lax.axis_index("x") + lax.axis_index("y")).astype(jnp.int32).reshape(1)
    dev = 4 * lax.axis_index("x") + 2 * lax.axis_index("y") + lax.axis_index("c")

    shards = [_cast_transposed(w_in[0], "cast_w_in"), _cast_bf16(w_oa[0], "cast_w_oa"),
              _cast_bf16(w_ob[0], "cast_w_ob"), _cast_bf16(w_out[0], "cast_w_out")]
    (conv_w_g,) = _all_gather([conv_w[0]], ["lead"], "gather_conv_w", vmem=True)
    conv_w_f = jnp.transpose(conv_w_g, (1, 0, 2)).reshape(3, d)
    w_s3 = w_s[0]
    b_s3 = b_s[0].reshape(HEADS, CHUNK, 1)

    saved, ya, yb, w_in_f, w_oa_f, w_ob_f, w_out_f = _mixer_fwd(
        x2, shards, ln_v_g, ln_v_b, w_s3, b_s3, conv_w_f, conv_b)
    sa, sb, pa, pb, merged = _merge_fwd(x2, ya, yb, w_in_f, w_oa_f, w_ob_f, b_gate)
    dz, gx_direct, xb, da, db, dgate, g_ln_g, g_ln_b, err2, g_bga, g_bgb = _out_ln_loss(
        merged, w_out_f, x2, tgt2, ln_g, ln_b, sa, sb, pa, pb)
    loss = lax.psum(0.5 * jnp.sum(err2) / d, ("x", "y", "c"))

    dp, g_ws, g_bs, g_lvg, g_lvb, g_cw, g_cb = _mixer_bwd(
        da, db, w_oa_f, w_ob_f, saved, dgate, ln_v_g, ln_v_b, w_s3, b_s3, conv_w_f)
    gw_in = _grad_w_in(xb, dp)

    gw_out, land_in = _grad_w(merged, dz, "grad_w_out", exchange=gw_in)
    part_in = _pair_sum(gw_in, land_in, "col", c_arr, "grad_pair_sum_0")
    gw_oa = _grad_w(ya, da, "grad_w_oa")
    gw_ob = _grad_w(yb, db, "grad_w_ob")
    rows = [gw_oa, gw_ob, gw_out]
    lands = _pair_exchange(rows, ["row"] * 3, "grad_pair_exchange")
    parts = [part_in] + [_pair_sum(g, l, "row", c_arr, "grad_pair_sum_%d" % (n + 1))
                         for n, (g, l) in enumerate(zip(rows, lands))]
    pieces = [jnp.concatenate([g_bga, g_bgb], axis=1), g_lvg, g_lvb, g_ws, g_bs, g_cw, g_cb, g_ln_g, g_ln_b]
    sizes = [p.size for p in pieces]
    packed = jnp.concatenate([p.reshape(-1, 128) for p in pieces], axis=0)
    gx_part, lands2, gathered = _grad_x(dp, w_in_f, gx_direct, parts, packed, (0, GRAD_X_COMM_BLOCKS))
    grad_x = _grad_x_tail(dp, w_in_f, gx_part, (GRAD_X_COMM_BLOCKS, N_IN))[None]

    big = []
    for n, (w, m, v) in enumerate([(w_in, m_w_in, v_w_in), (w_oa, m_w_oa, v_w_oa), (w_ob, m_w_ob, v_w_ob),
                                   (w_out, m_w_out, v_w_out)]):
        big.append([o[None] for o in _sum_adam(parts[n], lands2[n], w[0], m[0], v[0], chip_arr, "sum_adam_%d" % n)])
    (g_w_in, d_w_in, nm_w_in, nv_w_in), (g_w_oa, d_w_oa, nm_w_oa, nv_w_oa), \
        (g_w_ob, d_w_ob, nm_w_ob, nv_w_ob), (g_w_out, d_w_out, nm_w_out, nv_w_out) = big

    total = _small_sum(gathered, "sum_small_grads")
    offs = [0]
    for s in sizes:
        offs.append(offs[-1] + s // 128)
    unpacked = [total[offs[n]:offs[n + 1]] for n in range(len(pieces))]
    g_b_gate = unpacked[0].reshape(b_gate.shape)
    g_ln_v_g = unpacked[1].reshape(ln_v_g.shape)
    g_ln_v_b = unpacked[2].reshape(ln_v_b.shape)
    g_w_s = unpacked[3].reshape(w_s.shape)
    g_b_s = unpacked[4].reshape(b_s.shape)
    g_conv_w = lax.dynamic_slice_in_dim(unpacked[5].reshape(3, d), dev * (d // N_DEV), d // N_DEV, axis=1)[None]
    g_conv_b = unpacked[6].reshape(conv_b.shape)
    g_ln_g2 = unpacked[7].reshape(ln_g.shape)
    g_ln_b2 = unpacked[8].reshape(ln_b.shape)

    small_w = [b_gate, ln_v_g, ln_v_b, w_s, b_s, conv_w, conv_b, ln_g, ln_b]
    small_g = [g_b_gate, g_ln_v_g, g_ln_v_b, g_w_s, g_b_s, g_conv_w, g_conv_b, g_ln_g2, g_ln_b2]
    small_m = [m_b_gate, m_ln_v_g, m_ln_v_b, m_w_s, m_b_s, m_conv_w, m_conv_b, m_ln_g, m_ln_b]
    small_v = [v_b_gate, v_ln_v_g, v_ln_v_b, v_w_s, v_b_s, v_conv_w, v_conv_b, v_ln_g, v_ln_b]

    def flat(a):
        return a.reshape(-1, a.shape[-1])

    res = _small_adam([flat(a) for a in small_w], [flat(a) for a in small_g], [flat(a) for a in small_m],
                      [flat(a) for a in small_v], "adam_small")
    ns = len(small_w)
    d_s = [res[n].reshape(small_w[n].shape) for n in range(ns)]
    nm_s = [res[ns + n].reshape(small_w[n].shape) for n in range(ns)]
    nv_s = [res[2 * ns + n].reshape(small_w[n].shape) for n in range(ns)]

    def ordered(first, small, oa, ob, out):
        return [first] + small[:7] + [oa, ob, out] + small[7:]

    return (loss, grad_x,
            *ordered(g_w_in, small_g, g_w_oa, g_w_ob, g_w_out),
            *ordered(d_w_in, d_s, d_w_oa, d_w_ob, d_w_out),
            *ordered(nm_w_in, nm_s, nm_w_oa, nm_w_ob, nm_w_out),
            *ordered(nv_w_in, nv_s, nv_w_oa, nv_w_ob, nv_w_out))
```
